```python
import jax, jax.numpy as jnp
from jax import lax
import numpy as np

D_MODEL = 1024
BATCH = 8
SEQ = 8192
DEPTH = 2

N_META = 16
CONV_GROUPS = 8
CONV_GROUP_DIM = 64
D_CONV = CONV_GROUPS * CONV_GROUP_DIM
CONV_WIDTH = 3
MLA_HEADS = 8
QK_NOPE = 64
QK_ROPE = 32
V_HEAD = 64
Q_LORA = 256
KV_LORA = 128
D_MLA = MLA_HEADS * V_HEAD
ROPE_BASE = 10000.0
Q_BLOCK = 128
NEG_INF = -1e30
D_FF = 2816
N_BRANCH = 2
ALPHA = (2 * DEPTH) ** 0.25
BETA = (8 * DEPTH) ** -0.25
LN_EPS = 1e-5
RMS_EPS = 1e-6
IN_SPLITS = (D_CONV, D_CONV, D_CONV, Q_LORA, KV_LORA, QK_ROPE, D_MODEL, D_MODEL)
D_IN = sum(IN_SPLITS)

kernel_name = 'hybrid_shortconv_mla_macaron_deepnorm'


def layer_norm(x, g, b):
    xf = x.astype(jnp.float32)
    mu = jnp.mean(xf, axis=-1, keepdims=True)
    var = jnp.mean(jnp.square(xf - mu), axis=-1, keepdims=True)
    return ((xf - mu) * lax.rsqrt(var + LN_EPS) * g + b).astype(x.dtype)


def rms_norm(x, g):
    xf = x.astype(jnp.float32)
    return (xf * lax.rsqrt(jnp.mean(jnp.square(xf), axis=-1, keepdims=True) + RMS_EPS) * g).astype(x.dtype)


def rope_tables(T):
    inv_freq = 1.0 / (ROPE_BASE ** (jnp.arange(0, QK_ROPE, 2, dtype=jnp.float32) / QK_ROPE))
    ang = jnp.arange(T, dtype=jnp.float32)[:, None] * inv_freq[None, :]
    return jnp.cos(ang), jnp.sin(ang)


def apply_rope(x, cos, sin):
    x1, x2 = jnp.split(x.astype(jnp.float32), 2, axis=-1)
    return jnp.concatenate([x1 * cos - x2 * sin, x2 * cos + x1 * sin], axis=-1).astype(x.dtype)


def swiglu(x, w_up, w_down):
    gate, up = jnp.split(x @ w_up, 2, axis=-1)
    return (jax.nn.silu(gate) * up) @ w_down


def causal_short_conv(u, w):
    T = u.shape[1]
    up = jnp.pad(u, ((0, 0), (CONV_WIDTH - 1, 0), (0, 0)))
    out = up[:, 0:T] * w[0]
    for k in range(1, CONV_WIDTH):
        out = out + up[:, k:k + T] * w[k]
    return out


def mla_causal_attention(q_nope, q_rope, k_nope, k_rope, v):
    bsz, T = q_nope.shape[:2]
    n_blocks = -(-T // Q_BLOCK)
    pad = n_blocks * Q_BLOCK - T
    scale = (QK_NOPE + QK_ROPE) ** -0.5

    def to_blocks(a):
        a = jnp.pad(a, ((0, 0), (0, pad), (0, 0), (0, 0)))
        return jnp.moveaxis(a.reshape(bsz, n_blocks, Q_BLOCK, *a.shape[2:]), 1, 0)

    q_pos = jnp.arange(n_blocks * Q_BLOCK, dtype=jnp.int32).reshape(n_blocks, Q_BLOCK)
    k_pos = jnp.arange(T, dtype=jnp.int32)

    def one_block(args):
        qn, qr, qp = args
        s = (jnp.einsum('bqhd,bkhd->bhqk', qn, k_nope)
             + jnp.einsum('bqhr,bkr->bhqk', qr, k_rope)).astype(jnp.float32) * scale
        s = jnp.where(k_pos[None, :] <= qp[:, None], s, NEG_INF)
        p = jax.nn.softmax(s, axis=-1).astype(v.dtype)
        return jnp.einsum('bhqk,bkhd->bqhd', p, v)

    out = lax.map(one_block, (to_blocks(q_nope), to_blocks(q_rope), q_pos))
    out = jnp.moveaxis(out, 0, 1).reshape(bsz, n_blocks * Q_BLOCK, MLA_HEADS, V_HEAD)
    return out[:, :T]


def hybrid_mixer(x, w_in, b_gate, conv_w, q_norm_g, w_uq, kv_norm_g, w_ukv, w_br_conv, w_br_mla, w_o, cos, sin):
    bsz, T, _ = x.shape
    cuts = [int(c) for c in np.cumsum(IN_SPLITS)[:-1]]
    b_in, c_in, h_in, c_q, c_kv, k_r, g_conv, g_mla = jnp.split(x @ w_in, cuts, axis=-1)
    y_conv = b_in * causal_short_conv(c_in * h_in, conv_w)
    q = (rms_norm(c_q, q_norm_g) @ w_uq).reshape(bsz, T, MLA_HEADS, QK_NOPE + QK_ROPE)
    q_nope = q[..., :QK_NOPE]
    q_rope = apply_rope(q[..., QK_NOPE:], cos[None, :, None], sin[None, :, None])
    kv = (rms_norm(c_kv, kv_norm_g) @ w_ukv).reshape(bsz, T, MLA_HEADS, QK_NOPE + V_HEAD)
    k_nope, v = kv[..., :QK_NOPE], kv[..., QK_NOPE:]
    k_rope = apply_rope(k_r, cos[None], sin[None])
    y_mla = mla_causal_attention(q_nope, q_rope, k_nope, k_rope, v).reshape(bsz, T, D_MLA)
    merged = (jax.nn.sigmoid(g_conv + b_gate[0]) * (y_conv @ w_br_conv)
              + jax.nn.sigmoid(g_mla + b_gate[1]) * (y_mla @ w_br_mla))
    return merged @ w_o


def _fwd_setup_inputs(seed: int = 0) -> dict:
    key = jax.random.key(seed)
    ks = jax.random.split(key, 18)
    L = DEPTH

    def dense(k, shape, scale=1.0):
        return jax.random.normal(k, shape, jnp.float32) * (scale * shape[-2] ** -0.5)

    def near_one(k, shape):
        return 1.0 + 0.02 * jax.random.normal(k, shape, jnp.float32)

    def small(k, shape):
        return 0.02 * jax.random.normal(k, shape, jnp.float32)

    return {
        'x': jax.random.normal(ks[0], (BATCH, SEQ, D_MODEL), jnp.float32),
        'meta_tokens': jax.random.normal(ks[1], (N_META, D_MODEL), jnp.float32),
        'ffn1_w_up': dense(ks[2], (L, D_MODEL, 2 * D_FF)),
        'ffn1_w_down': dense(ks[3], (L, D_FF, D_MODEL), BETA),
        'mix_w_in': dense(ks[4], (L, D_MODEL, D_IN)),
        'mix_b_gate': small(ks[5], (L, N_BRANCH, D_MODEL)),
        'conv_w': dense(ks[6], (L, CONV_WIDTH, D_CONV)),
        'q_norm_g': near_one(ks[7], (L, Q_LORA)),
        'w_uq': dense(ks[8], (L, Q_LORA, MLA_HEADS * (QK_NOPE + QK_ROPE))),
        'kv_norm_g': near_one(ks[9], (L, KV_LORA)),
        'w_ukv': dense(ks[10], (L, KV_LORA, MLA_HEADS * (QK_NOPE + V_HEAD))),
        'w_br_conv': dense(ks[11], (L, D_CONV, D_MODEL)),
        'w_br_mla': dense(ks[12], (L, D_MLA, D_MODEL)),
        'w_o': dense(ks[13], (L, D_MODEL, D_MODEL), BETA),
        'ffn2_w_up': dense(ks[14], (L, D_MODEL, 2 * D_FF)),
        'ffn2_w_down': dense(ks[15], (L, D_FF, D_MODEL), BETA),
        'ln_g': near_one(ks[16], (L, 3, D_MODEL)),
        'ln_b': small(ks[17], (L, 3, D_MODEL)),
    }


def _fwd_reference(x, meta_tokens, ffn1_w_up, ffn1_w_down, mix_w_in, mix_b_gate, conv_w, q_norm_g, w_uq,
              kv_norm_g, w_ukv, w_br_conv, w_br_mla, w_o, ffn2_w_up, ffn2_w_down, ln_g, ln_b):
    bsz = x.shape[0]
    meta = jnp.broadcast_to(meta_tokens[None].astype(x.dtype), (bsz, N_META, D_MODEL))
    h = jnp.concatenate([meta, x], axis=1)
    cos, sin = rope_tables(h.shape[1])
    for l in range(DEPTH):
        h = layer_norm(ALPHA * h + 0.5 * swiglu(h, ffn1_w_up[l], ffn1_w_down[l]), ln_g[l, 0], ln_b[l, 0])
        mix = hybrid_mixer(h, mix_w_in[l], mix_b_gate[l], conv_w[l], q_norm_g[l], w_uq[l], kv_norm_g[l],
                           w_ukv[l], w_br_conv[l], w_br_mla[l], w_o[l], cos, sin)
        h = layer_norm(ALPHA * h + mix, ln_g[l, 1], ln_b[l, 1])
        h = layer_norm(ALPHA * h + 0.5 * swiglu(h, ffn2_w_up[l], ffn2_w_down[l]), ln_g[l, 2], ln_b[l, 2])
    return h[:, N_META:]


import jax as _jax
import jax.numpy as _jnp

TWIN_FORMAT = 'train_step'
FWD_PARAMS = ['x', 'meta_tokens', 'ffn1_w_up', 'ffn1_w_down', 'mix_w_in', 'mix_b_gate', 'conv_w', 'q_norm_g', 'w_uq', 'kv_norm_g', 'w_ukv', 'w_br_conv', 'w_br_mla', 'w_o', 'ffn2_w_up', 'ffn2_w_down', 'ln_g', 'ln_b']
TWIN_WEIGHTS = ['meta_tokens', 'ffn1_w_up', 'ffn1_w_down', 'mix_w_in', 'mix_b_gate', 'conv_w', 'q_norm_g', 'w_uq', 'kv_norm_g', 'w_ukv', 'w_br_conv', 'w_br_mla', 'w_o', 'ffn2_w_up', 'ffn2_w_down', 'ln_g', 'ln_b']
TWIN_DIFF_INPUT = 'x'
TWIN_INPUTS = ['x', 'meta_tokens', 'ffn1_w_up', 'ffn1_w_down', 'mix_w_in', 'mix_b_gate', 'conv_w', 'q_norm_g', 'w_uq', 'kv_norm_g', 'w_ukv', 'w_br_conv', 'w_br_mla', 'w_o', 'ffn2_w_up', 'ffn2_w_down', 'ln_g', 'ln_b', 'loss_target', 'm_meta_tokens', 'm_ffn1_w_up', 'm_ffn1_w_down', 'm_mix_w_in', 'm_mix_b_gate', 'm_conv_w', 'm_q_norm_g', 'm_w_uq', 'm_kv_norm_g', 'm_w_ukv', 'm_w_br_conv', 'm_w_br_mla', 'm_w_o', 'm_ffn2_w_up', 'm_ffn2_w_down', 'm_ln_g', 'm_ln_b', 'v_meta_tokens', 'v_ffn1_w_up', 'v_ffn1_w_down', 'v_mix_w_in', 'v_mix_b_gate', 'v_conv_w', 'v_q_norm_g', 'v_w_uq', 'v_kv_norm_g', 'v_w_ukv', 'v_w_br_conv', 'v_w_br_mla', 'v_w_o', 'v_ffn2_w_up', 'v_ffn2_w_down', 'v_ln_g', 'v_ln_b']
TWIN_OUTPUTS = ['loss', 'grad_x', 'grad_meta_tokens', 'grad_ffn1_w_up', 'grad_ffn1_w_down', 'grad_mix_w_in', 'grad_mix_b_gate', 'grad_conv_w', 'grad_q_norm_g', 'grad_w_uq', 'grad_kv_norm_g', 'grad_w_ukv', 'grad_w_br_conv', 'grad_w_br_mla', 'grad_w_o', 'grad_ffn2_w_up', 'grad_ffn2_w_down', 'grad_ln_g', 'grad_ln_b', 'delta_meta_tokens', 'delta_ffn1_w_up', 'delta_ffn1_w_down', 'delta_mix_w_in', 'delta_mix_b_gate', 'delta_conv_w', 'delta_q_norm_g', 'delta_w_uq', 'delta_kv_norm_g', 'delta_w_ukv', 'delta_w_br_conv', 'delta_w_br_mla', 'delta_w_o', 'delta_ffn2_w_up', 'delta_ffn2_w_down', 'delta_ln_g', 'delta_ln_b', 'new_m_meta_tokens', 'new_m_ffn1_w_up', 'new_m_ffn1_w_down', 'new_m_mix_w_in', 'new_m_mix_b_gate', 'new_m_conv_w', 'new_m_q_norm_g', 'new_m_w_uq', 'new_m_kv_norm_g', 'new_m_w_ukv', 'new_m_w_br_conv', 'new_m_w_br_mla', 'new_m_w_o', 'new_m_ffn2_w_up', 'new_m_ffn2_w_down', 'new_m_ln_g', 'new_m_ln_b', 'new_v_meta_tokens', 'new_v_ffn1_w_up', 'new_v_ffn1_w_down', 'new_v_mix_w_in', 'new_v_mix_b_gate', 'new_v_conv_w', 'new_v_q_norm_g', 'new_v_w_uq', 'new_v_kv_norm_g', 'new_v_w_ukv', 'new_v_w_br_conv', 'new_v_w_br_mla', 'new_v_w_o', 'new_v_ffn2_w_up', 'new_v_ffn2_w_down', 'new_v_ln_g', 'new_v_ln_b']
TWIN_LEAF_KINDS = {'loss': 'loss', 'grad_x': 'grad_x', 'grad_meta_tokens': 'grad_w', 'grad_ffn1_w_up': 'grad_w', 'grad_ffn1_w_down': 'grad_w', 'grad_mix_w_in': 'grad_w', 'grad_mix_b_gate': 'grad_w', 'grad_conv_w': 'grad_w', 'grad_q_norm_g': 'grad_w', 'grad_w_uq': 'grad_w', 'grad_kv_norm_g': 'grad_w', 'grad_w_ukv': 'grad_w', 'grad_w_br_conv': 'grad_w', 'grad_w_br_mla': 'grad_w', 'grad_w_o': 'grad_w', 'grad_ffn2_w_up': 'grad_w', 'grad_ffn2_w_down': 'grad_w', 'grad_ln_g': 'grad_w', 'grad_ln_b': 'grad_w', 'delta_meta_tokens': 'delta_w', 'delta_ffn1_w_up': 'delta_w', 'delta_ffn1_w_down': 'delta_w', 'delta_mix_w_in': 'delta_w', 'delta_mix_b_gate': 'delta_w', 'delta_conv_w': 'delta_w', 'delta_q_norm_g': 'delta_w', 'delta_w_uq': 'delta_w', 'delta_kv_norm_g': 'delta_w', 'delta_w_ukv': 'delta_w', 'delta_w_br_conv': 'delta_w', 'delta_w_br_mla': 'delta_w', 'delta_w_o': 'delta_w', 'delta_ffn2_w_up': 'delta_w', 'delta_ffn2_w_down': 'delta_w', 'delta_ln_g': 'delta_w', 'delta_ln_b': 'delta_w', 'new_m_meta_tokens': 'new_m', 'new_m_ffn1_w_up': 'new_m', 'new_m_ffn1_w_down': 'new_m', 'new_m_mix_w_in': 'new_m', 'new_m_mix_b_gate': 'new_m', 'new_m_conv_w': 'new_m', 'new_m_q_norm_g': 'new_m', 'new_m_w_uq': 'new_m', 'new_m_kv_norm_g': 'new_m', 'new_m_w_ukv': 'new_m', 'new_m_w_br_conv': 'new_m', 'new_m_w_br_mla': 'new_m', 'new_m_w_o': 'new_m', 'new_m_ffn2_w_up': 'new_m', 'new_m_ffn2_w_down': 'new_m', 'new_m_ln_g': 'new_m', 'new_m_ln_b': 'new_m', 'new_v_meta_tokens': 'new_v', 'new_v_ffn1_w_up': 'new_v', 'new_v_ffn1_w_down': 'new_v', 'new_v_mix_w_in': 'new_v', 'new_v_mix_b_gate': 'new_v', 'new_v_conv_w': 'new_v', 'new_v_q_norm_g': 'new_v', 'new_v_w_uq': 'new_v', 'new_v_kv_norm_g': 'new_v', 'new_v_w_ukv': 'new_v', 'new_v_w_br_conv': 'new_v', 'new_v_w_br_mla': 'new_v', 'new_v_w_o': 'new_v', 'new_v_ffn2_w_up': 'new_v', 'new_v_ffn2_w_down': 'new_v', 'new_v_ln_g': 'new_v', 'new_v_ln_b': 'new_v'}


def _forward(args):
    return _fwd_reference(*[args[k] for k in FWD_PARAMS])


def _output_shape():
    def fwd():
        inp = _fwd_setup_inputs(0)
        return _fwd_reference(*[inp[k] for k in FWD_PARAMS])
    out = _jax.eval_shape(fwd)
    return out.shape, out.dtype

N_MICROBATCH = 1
ADAM_LR = 0.001
ADAM_B1 = 0.9
ADAM_B2 = 0.999
ADAM_EPS = 1e-08
ADAM_WD = 0.01
ADAM_STEP = 10
PER_EXAMPLE_BATCH_AXIS = {'x': 0, 'loss_target': 0}
SHARED_INPUTS = []
_WEIGHT_DTYPES = {'meta_tokens': _jnp.float32, 'ffn1_w_up': _jnp.float32, 'ffn1_w_down': _jnp.float32, 'mix_w_in': _jnp.float32, 'mix_b_gate': _jnp.float32, 'conv_w': _jnp.float32, 'q_norm_g': _jnp.float32, 'w_uq': _jnp.float32, 'kv_norm_g': _jnp.float32, 'w_ukv': _jnp.float32, 'w_br_conv': _jnp.float32, 'w_br_mla': _jnp.float32, 'w_o': _jnp.float32, 'ffn2_w_up': _jnp.float32, 'ffn2_w_down': _jnp.float32, 'ln_g': _jnp.float32, 'ln_b': _jnp.float32}
MOMENT_SCALE = {'meta_tokens': 2.779282e-03, 'ffn1_w_up': 1.721787e-02, 'ffn1_w_down': 5.613971e-02, 'mix_w_in': 4.400858e-02, 'mix_b_gate': 1.384677e-02, 'conv_w': 7.019542e-02, 'q_norm_g': 1.582579e-02, 'w_uq': 9.344796e-03, 'kv_norm_g': 3.627821e-02, 'w_ukv': 1.146532e-02, 'w_br_conv': 4.838869e-02, 'w_br_mla': 9.033721e-03, 'w_o': 9.864739e-02, 'ffn2_w_up': 1.653636e-02, 'ffn2_w_down': 5.399686e-02, 'ln_g': 2.626951e+01, 'ln_b': 1.325860e+00}


def _to_microbatches(a, axis):
    t = _jnp.moveaxis(a, axis, 0)
    t = t.reshape((N_MICROBATCH, t.shape[0] // N_MICROBATCH) + t.shape[1:])
    return _jnp.moveaxis(t, 1, axis + 1)


def setup_inputs(seed: int = 0) -> dict:
    inp = _fwd_setup_inputs(seed)
    key = _jax.random.fold_in(_jax.random.key(seed), 7919)
    shape, _ = _output_shape()
    out = dict(inp)
    out["loss_target"] = _jax.random.normal(_jax.random.fold_in(key, 0), shape, _jnp.float32)
    for i, name in enumerate(TWIN_WEIGHTS):
        w = inp[name].astype(_jnp.float32)
        if MOMENT_SCALE is None:
            s = _jnp.sqrt(_jnp.mean(_jnp.square(w)) + 1e-30)
        else:
            s = MOMENT_SCALE[name]
        km, kv = _jax.random.split(_jax.random.fold_in(key, i + 1))
        out[name] = w
        out["m_" + name] = s * _jax.random.normal(km, w.shape, _jnp.float32)
        out["v_" + name] = (s * s) * _jax.random.uniform(kv, w.shape, _jnp.float32, 0.5, 1.5)
    if N_MICROBATCH > 1:
        for name, axis in PER_EXAMPLE_BATCH_AXIS.items():
            out[name] = _to_microbatches(out[name], axis)
    return {'x': out['x'], 'meta_tokens': out['meta_tokens'], 'ffn1_w_up': out['ffn1_w_up'], 'ffn1_w_down': out['ffn1_w_down'], 'mix_w_in': out['mix_w_in'], 'mix_b_gate': out['mix_b_gate'], 'conv_w': out['conv_w'], 'q_norm_g': out['q_norm_g'], 'w_uq': out['w_uq'], 'kv_norm_g': out['kv_norm_g'], 'w_ukv': out['w_ukv'], 'w_br_conv': out['w_br_conv'], 'w_br_mla': out['w_br_mla'], 'w_o': out['w_o'], 'ffn2_w_up': out['ffn2_w_up'], 'ffn2_w_down': out['ffn2_w_down'], 'ln_g': out['ln_g'], 'ln_b': out['ln_b'], 'loss_target': out['loss_target'], 'm_meta_tokens': out['m_meta_tokens'], 'm_ffn1_w_up': out['m_ffn1_w_up'], 'm_ffn1_w_down': out['m_ffn1_w_down'], 'm_mix_w_in': out['m_mix_w_in'], 'm_mix_b_gate': out['m_mix_b_gate'], 'm_conv_w': out['m_conv_w'], 'm_q_norm_g': out['m_q_norm_g'], 'm_w_uq': out['m_w_uq'], 'm_kv_norm_g': out['m_kv_norm_g'], 'm_w_ukv': out['m_w_ukv'], 'm_w_br_conv': out['m_w_br_conv'], 'm_w_br_mla': out['m_w_br_mla'], 'm_w_o': out['m_w_o'], 'm_ffn2_w_up': out['m_ffn2_w_up'], 'm_ffn2_w_down': out['m_ffn2_w_down'], 'm_ln_g': out['m_ln_g'], 'm_ln_b': out['m_ln_b'], 'v_meta_tokens': out['v_meta_tokens'], 'v_ffn1_w_up': out['v_ffn1_w_up'], 'v_ffn1_w_down': out['v_ffn1_w_down'], 'v_mix_w_in': out['v_mix_w_in'], 'v_mix_b_gate': out['v_mix_b_gate'], 'v_conv_w': out['v_conv_w'], 'v_q_norm_g': out['v_q_norm_g'], 'v_w_uq': out['v_w_uq'], 'v_kv_norm_g': out['v_kv_norm_g'], 'v_w_ukv': out['v_w_ukv'], 'v_w_br_conv': out['v_w_br_conv'], 'v_w_br_mla': out['v_w_br_mla'], 'v_w_o': out['v_w_o'], 'v_ffn2_w_up': out['v_ffn2_w_up'], 'v_ffn2_w_down': out['v_ffn2_w_down'], 'v_ln_g': out['v_ln_g'], 'v_ln_b': out['v_ln_b']}


def _loss(weights, diff, rest, loss_target):
    with _jax.named_scope("forward"):
        args = {**rest, TWIN_DIFF_INPUT: diff, **{k: w.astype(_WEIGHT_DTYPES[k]) for k, w in weights.items()}}
        y = _forward(args)
    with _jax.named_scope("loss_head"):
        err = _jnp.square(y.astype(_jnp.float32) - loss_target)
        return 0.5 * _jnp.sum(_jnp.mean(err, axis=-1)) if err.ndim else 0.5 * err


def _adamw(w, g, m, v):
    m = ADAM_B1 * m + (1.0 - ADAM_B1) * g
    v = ADAM_B2 * v + (1.0 - ADAM_B2) * _jnp.square(g)
    m_hat = m / (1.0 - ADAM_B1 ** ADAM_STEP)
    v_hat = v / (1.0 - ADAM_B2 ** ADAM_STEP)
    delta = -ADAM_LR * (m_hat / (_jnp.sqrt(v_hat) + ADAM_EPS) + ADAM_WD * w)
    return delta, m, v


def reference(x, meta_tokens, ffn1_w_up, ffn1_w_down, mix_w_in, mix_b_gate, conv_w, q_norm_g, w_uq, kv_norm_g, w_ukv, w_br_conv, w_br_mla, w_o, ffn2_w_up, ffn2_w_down, ln_g, ln_b, loss_target, m_meta_tokens, m_ffn1_w_up, m_ffn1_w_down, m_mix_w_in, m_mix_b_gate, m_conv_w, m_q_norm_g, m_w_uq, m_kv_norm_g, m_w_ukv, m_w_br_conv, m_w_br_mla, m_w_o, m_ffn2_w_up, m_ffn2_w_down, m_ln_g, m_ln_b, v_meta_tokens, v_ffn1_w_up, v_ffn1_w_down, v_mix_w_in, v_mix_b_gate, v_conv_w, v_q_norm_g, v_w_uq, v_kv_norm_g, v_w_ukv, v_w_br_conv, v_w_br_mla, v_w_o, v_ffn2_w_up, v_ffn2_w_down, v_ln_g, v_ln_b):
    given = dict(x=x, meta_tokens=meta_tokens, ffn1_w_up=ffn1_w_up, ffn1_w_down=ffn1_w_down, mix_w_in=mix_w_in, mix_b_gate=mix_b_gate, conv_w=conv_w, q_norm_g=q_norm_g, w_uq=w_uq, kv_norm_g=kv_norm_g, w_ukv=w_ukv, w_br_conv=w_br_conv, w_br_mla=w_br_mla, w_o=w_o, ffn2_w_up=ffn2_w_up, ffn2_w_down=ffn2_w_down, ln_g=ln_g, ln_b=ln_b, loss_target=loss_target, m_meta_tokens=m_meta_tokens, m_ffn1_w_up=m_ffn1_w_up, m_ffn1_w_down=m_ffn1_w_down, m_mix_w_in=m_mix_w_in, m_mix_b_gate=m_mix_b_gate, m_conv_w=m_conv_w, m_q_norm_g=m_q_norm_g, m_w_uq=m_w_uq, m_kv_norm_g=m_kv_norm_g, m_w_ukv=m_w_ukv, m_w_br_conv=m_w_br_conv, m_w_br_mla=m_w_br_mla, m_w_o=m_w_o, m_ffn2_w_up=m_ffn2_w_up, m_ffn2_w_down=m_ffn2_w_down, m_ln_g=m_ln_g, m_ln_b=m_ln_b, v_meta_tokens=v_meta_tokens, v_ffn1_w_up=v_ffn1_w_up, v_ffn1_w_down=v_ffn1_w_down, v_mix_w_in=v_mix_w_in, v_mix_b_gate=v_mix_b_gate, v_conv_w=v_conv_w, v_q_norm_g=v_q_norm_g, v_w_uq=v_w_uq, v_kv_norm_g=v_kv_norm_g, v_w_ukv=v_w_ukv, v_w_br_conv=v_w_br_conv, v_w_br_mla=v_w_br_mla, v_w_o=v_w_o, v_ffn2_w_up=v_ffn2_w_up, v_ffn2_w_down=v_ffn2_w_down, v_ln_g=v_ln_g, v_ln_b=v_ln_b)
    weights = {n: given[n] for n in TWIN_WEIGHTS}
    shared = {n: given[n] for n in SHARED_INPUTS}
    per_example = {n: given[n] for n in ['x']}
    grad_fn = _jax.value_and_grad(_loss, argnums=(0, 1))

    def one_microbatch(ex, loss_target):
        ex = dict(ex)
        diff = ex.pop(TWIN_DIFF_INPUT)
        return grad_fn(weights, diff, {**shared, **ex}, loss_target)

    if N_MICROBATCH == 1:
        loss, (grad_w, grad_x) = one_microbatch(per_example, given["loss_target"])
    else:
        def body(carry, xs):
            loss_sum, grad_sum = carry
            l_k, (gw_k, gx_k) = one_microbatch(xs[0], xs[1])
            with _jax.named_scope("update"):
                return (loss_sum + l_k, _jax.tree.map(_jnp.add, grad_sum, gw_k)), gx_k

        init = (_jnp.zeros((), _jnp.float32), _jax.tree.map(_jnp.zeros_like, weights))
        (loss, grad_w), grad_x = _jax.lax.scan(body, init, (per_example, given["loss_target"]))
    with _jax.named_scope("update"):
        delta_w, new_m, new_v = {}, {}, {}
        for n in TWIN_WEIGHTS:
            delta_w[n], new_m[n], new_v[n] = _adamw(weights[n], grad_w[n], given["m_" + n], given["v_" + n])
    return (loss, grad_x, *[grad_w[n] for n in TWIN_WEIGHTS], *[delta_w[n] for n in TWIN_WEIGHTS],
            *[new_m[n] for n in TWIN_WEIGHTS], *[new_v[n] for n in TWIN_WEIGHTS])
```

```python
import functools

import jax
import jax.numpy as jnp
from jax import lax
from jax.experimental import pallas as pl
from jax.experimental.pallas import tpu as pltpu

F32 = jnp.float32
BF = jnp.bfloat16
MESH = pl.DeviceIdType.MESH

D_MODEL = 1024
DEPTH = 2
N_META = 16
D_CONV = 512
MLA_HEADS = 8
QK_NOPE = 64
QK_ROPE = 32
V_HEAD = 64
Q_LORA = 256
KV_LORA = 128
ROPE_BASE = 10000.0
NEG_INF = -1e30
D_FF = 2816
ALPHA = (2 * DEPTH) ** 0.25
LN_EPS = 1e-5
RMS_EPS = 1e-6
ATT_SCALE = (QK_NOPE + QK_ROPE) ** -0.5
D_IN = 4000
D_IN_PAD = 4096
Q_PAD = 256
KV_PAD = 128

ADAM_LR = 0.001
ADAM_B1 = 0.9
ADAM_B2 = 0.999
ADAM_EPS = 1e-08
ADAM_WD = 0.01
ADAM_STEP = 10

TM = 768
TMH = 384
LANES = 128
COMM_COLS = 512
COMM_ROW_BLOCK = 1472
VMEM_LIMIT_BYTES = 50 * 1024 * 1024

NT = (((1,), (1,)), ((), ()))
TN = (((0,), (0,)), ((), ()))


def _pcall(body, **kw):
    return pl.pallas_call(body, **kw)


def _params(n_axes):
    return pltpu.CompilerParams(dimension_semantics=("arbitrary",) * n_axes, vmem_limit_bytes=VMEM_LIMIT_BYTES)


def _sds(shape, dtype):
    return jax.ShapeDtypeStruct(shape, dtype)


def mm_rows(pairs, n_out, *, name, tn=None, addend=None, add_scale=1.0, out_dtype=F32):
    tp = pairs[0][0].shape[0]
    tn = tn or n_out
    in_specs, args = [], []
    for a, b, nt, kb in pairs:
        k = a.shape[1]
        in_specs.append(pl.BlockSpec((TM, k), lambda i, j: (i, 0)))
        if nt:
            in_specs.append(pl.BlockSpec((tn, k), functools.partial(lambda i, j, kb: (j, kb), kb=kb)))
        else:
            in_specs.append(pl.BlockSpec((k, tn), lambda i, j: (0, j)))
        args += [a, b]
    if addend is not None:
        in_specs.append(pl.BlockSpec((TM, tn), lambda i, j: (i, j)))
        args.append(addend)
    n_pairs = len(pairs)
    nts = [p[2] for p in pairs]

    def body(*refs):
        o_ref = refs[-1]
        acc = None
        for p in range(n_pairs):
            a = refs[2 * p][...].astype(BF)
            b = refs[2 * p + 1][...]
            d = lax.dot_general(a, b, NT if nts[p] else (((1,), (0,)), ((), ())), preferred_element_type=F32)
            acc = d if acc is None else acc + d
        if addend is not None:
            acc = acc + add_scale * refs[2 * n_pairs][...]
        o_ref[...] = acc.astype(o_ref.dtype)

    return _pcall(
        body, name=name, grid=(tp // TM, n_out // tn), in_specs=in_specs,
        out_specs=pl.BlockSpec((TM, tn), lambda i, j: (i, j)), out_shape=_sds((tp, n_out), out_dtype),
        compiler_params=_params(2),
    )(*args)


def tn_mm(a, b, *, tm, name, out_dtype=BF):
    tp, m = a.shape
    n = b.shape[1]
    nk = tp // TM

    def body(a_ref, b_ref, o_ref, acc_ref):
        k = pl.program_id(1)

        @pl.when(k == 0)
        def _():
            acc_ref[...] = jnp.zeros_like(acc_ref)

        acc_ref[...] += lax.dot_general(a_ref[...].astype(BF), b_ref[...].astype(BF), TN, preferred_element_type=F32)

        @pl.when(k == nk - 1)
        def _():
            o_ref[...] = acc_ref[...].astype(o_ref.dtype)

    return _pcall(
        body, name=name, grid=(m // tm, nk),
        in_specs=[pl.BlockSpec((TM, tm), lambda i, k: (k, i)), pl.BlockSpec((TM, n), lambda i, k: (k, 0))],
        out_specs=pl.BlockSpec((tm, n), lambda i, k: (i, 0)), out_shape=_sds((m, n), out_dtype),
        scratch_shapes=[pltpu.VMEM((tm, n), F32)], compiler_params=_params(2),
    )(a, b)


def _ln_store(z, g_ref, b_ref, xh_ref, rs_ref, hb_ref):
    mu = jnp.mean(z, axis=-1, keepdims=True)
    zc = z - mu
    var = jnp.mean(zc * zc, axis=-1, keepdims=True)
    rstd = lax.rsqrt(var + LN_EPS)
    xh = zc * rstd
    xh_ref[...] = xh
    rs_ref[...] = rstd
    hb_ref[...] = (xh * g_ref[...] + b_ref[...]).astype(BF)


def _ln_out(tp, tm=TM):
    specs = [pl.BlockSpec((tm, D_MODEL), lambda i: (i, 0)), pl.BlockSpec((tm, 1), lambda i: (i, 0)),
             pl.BlockSpec((tm, D_MODEL), lambda i: (i, 0))]
    shapes = [_sds((tp, D_MODEL), F32), _sds((tp, 1), F32), _sds((tp, D_MODEL), BF)]
    return specs, shapes


def _row_vec(n):
    return pl.BlockSpec((1, n), lambda i: (0, 0))


def ffn_up(hb, wup, *, name):
    tp = hb.shape[0]
    tn = D_FF // 2
    nj = D_FF // tn

    def body(h_ref, wg_ref, wu_ref, g_ref, u_ref, a_ref):
        h = h_ref[...]
        g = jnp.dot(h, wg_ref[...], preferred_element_type=F32)
        u = jnp.dot(h, wu_ref[...], preferred_element_type=F32)
        g_ref[...] = g.astype(BF)
        u_ref[...] = u.astype(BF)
        a_ref[...] = (g * jax.nn.sigmoid(g) * u).astype(BF)

    blk = pl.BlockSpec((TM, tn), lambda i, j: (i, j))
    return _pcall(
        body, name=name, grid=(tp // TM, nj),
        in_specs=[pl.BlockSpec((TM, D_MODEL), lambda i, j: (i, 0)), pl.BlockSpec((D_MODEL, tn), lambda i, j: (0, j)),
                  pl.BlockSpec((D_MODEL, tn), lambda i, j: (0, j + nj))],
        out_specs=[blk, blk, blk], out_shape=[_sds((tp, D_FF), BF)] * 3, compiler_params=_params(2),
    )(hb, wup, wup)


def down_ln(a, wd, xprev, gp, bp, g, b, *, name):
    tp = a.shape[0]

    def body(a_ref, wd_ref, xp_ref, gp_ref, bp_ref, g_ref, b_ref, xh_ref, rs_ref, hb_ref):
        f = jnp.dot(a_ref[...], wd_ref[...], preferred_element_type=F32)
        hprev = xp_ref[...] * gp_ref[...] + bp_ref[...]
        _ln_store(ALPHA * hprev + 0.5 * f, g_ref, b_ref, xh_ref, rs_ref, hb_ref)

    out_specs, out_shape = _ln_out(tp)
    return _pcall(
        body, name=name, grid=(tp // TM,),
        in_specs=[pl.BlockSpec((TM, D_FF), lambda i: (i, 0)), pl.BlockSpec((D_FF, D_MODEL), lambda i: (0, 0)),
                  pl.BlockSpec((TM, D_MODEL), lambda i: (i, 0))] + [_row_vec(D_MODEL)] * 4,
        out_specs=out_specs, out_shape=out_shape, compiler_params=_params(1),
    )(a, wd, xprev, gp, bp, g, b)


def _rope(x, c, s1, s2, reps):
    n = x.shape[1]
    if reps > 1:
        c, s1, s2 = (jnp.tile(t, (1, reps)) for t in (c, s1, s2))
    return x * c + pltpu.roll(x, 16, 1) * s1 + pltpu.roll(x, n - 16, 1) * s2


def _rms(x, g):
    r = lax.rsqrt(jnp.mean(x * x, axis=-1, keepdims=True) + RMS_EPS)
    return x * r * g, r


def mla_prep(p, gq, gkv, wuq_p, wukv, tabs, *, name):
    tp = p.shape[0]
    nh = MLA_HEADS

    def body(cq_ref, ckv_ref, kr_ref, gq_ref, gkv_ref, wuq_ref, wukv_ref, cq_t, s1q_t, s2q_t, ck_t, s1k_t, s2k_t,
             cqn_ref, ckvn_ref, q2_ref, kv_ref, krr_ref):
        cqn, _ = _rms(cq_ref[...], gq_ref[...])
        ckvn, _ = _rms(ckv_ref[...], gkv_ref[...])
        cqn = cqn.astype(BF)
        ckvn = ckvn.astype(BF)
        cqn_ref[...] = cqn
        ckvn_ref[...] = ckvn
        q = jnp.dot(cqn, wuq_ref[...], preferred_element_type=F32)
        q2_ref[...] = _rope(q, cq_t[...], s1q_t[...], s2q_t[...], nh).astype(BF)
        kv_ref[...] = jnp.dot(ckvn, wukv_ref[...], preferred_element_type=F32).astype(BF)
        krr_ref[...] = _rope(kr_ref[...], ck_t[...], s1k_t[...], s2k_t[...], 1).astype(BF)

    def rows(n, col=0):
        return pl.BlockSpec((TMH, n), functools.partial(lambda i, col: (i, col), col=col))

    return _pcall(
        body, name=name, grid=(tp // TMH,),
        in_specs=[rows(Q_LORA, 1536 // Q_LORA), rows(KV_LORA, 1792 // KV_LORA), rows(LANES, 1920 // LANES),
                  _row_vec(Q_LORA), _row_vec(KV_LORA),
                  pl.BlockSpec((Q_LORA, nh * Q_PAD), lambda i: (0, 0)), pl.BlockSpec((KV_LORA, nh * KV_PAD), lambda i: (0, 0)),
                  rows(Q_PAD), rows(Q_PAD), rows(Q_PAD), rows(LANES), rows(LANES), rows(LANES)],
        out_specs=[rows(Q_LORA), rows(KV_LORA), rows(nh * Q_PAD), rows(nh * KV_PAD), rows(LANES)],
        out_shape=[_sds((tp, Q_LORA), BF), _sds((tp, KV_LORA), BF), _sds((tp, nh * Q_PAD), BF),
                   _sds((tp, nh * KV_PAD), BF), _sds((tp, LANES), BF)],
        compiler_params=_params(1),
    )(p, p, p, gq, gkv, wuq_p, wukv, *tabs)


def _causal_scores(q, kvb, krb, qi, ki):
    k2 = jnp.concatenate([kvb, krb], axis=1)
    s = lax.dot_general(q, k2, NT, preferred_element_type=F32)
    qpos = qi * TM + lax.broadcasted_iota(jnp.int32, (TM, TM), 0)
    kpos = ki * TM + lax.broadcasted_iota(jnp.int32, (TM, TM), 1)
    return jnp.where(kpos <= qpos, s, NEG_INF), k2


def attn_fwd(q2, kv, kr, *, name):
    tp = q2.shape[0]
    nh = MLA_HEADS
    nb = tp // TM
    rep = TM // LANES

    def body(q_ref, kv_ref, kr_ref, o_ref, lse_ref, m_ref, l_ref, acc_ref):
        qi = pl.program_id(1)
        ki = pl.program_id(2)

        @pl.when(ki == 0)
        def _():
            m_ref[...] = jnp.full_like(m_ref, NEG_INF)
            l_ref[...] = jnp.zeros_like(l_ref)
            acc_ref[...] = jnp.zeros_like(acc_ref)

        @pl.when(ki <= qi)
        def _():
            kvb = kv_ref[...]
            s, _ = _causal_scores(q_ref[...], kvb, kr_ref[...], qi, ki)
            m_prev = m_ref[...]
            m_new = jnp.maximum(m_prev, jnp.max(s, axis=1, keepdims=True))
            alpha = jnp.exp(m_prev - m_new)
            p = jnp.exp(s - jnp.tile(m_new, (1, rep)))
            l_ref[...] = alpha * l_ref[...] + jnp.sum(p, axis=1, keepdims=True)
            acc_ref[...] = alpha * acc_ref[...] + jnp.dot(p.astype(BF), kvb, preferred_element_type=F32)
            m_ref[...] = m_new

        @pl.when(ki == nb - 1)
        def _():
            o_ref[...] = (acc_ref[...] / l_ref[...]).astype(BF)
            lse_ref[...] = m_ref[...] + jnp.log(l_ref[...])

    return _pcall(
        body, name=name, grid=(nh, nb, nb),
        in_specs=[pl.BlockSpec((TM, Q_PAD), lambda h, qi, ki: (qi, h)),
                  pl.BlockSpec((TM, KV_PAD), lambda h, qi, ki: (jnp.minimum(ki, qi), h)),
                  pl.BlockSpec((TM, LANES), lambda h, qi, ki: (jnp.minimum(ki, qi), 0))],
        out_specs=[pl.BlockSpec((TM, KV_PAD), lambda h, qi, ki: (qi, h)), pl.BlockSpec((TM, LANES), lambda h, qi, ki: (qi, h))],
        out_shape=[_sds((tp, nh * KV_PAD), BF), _sds((tp, nh * LANES), F32)],
        scratch_shapes=[pltpu.VMEM((TM, LANES), F32)] * 3, compiler_params=_params(3),
    )(q2, kv, kr)


def conv_fwd(p, w, *, name):
    tp = p.shape[0]

    def body(b_ref, c_ref, h_ref, w_ref, y_ref, cv_ref, ebuf):
        i = pl.program_id(0)

        @pl.when(i == 0)
        def _():
            ebuf[0:8, :] = jnp.zeros((8, D_CONV), F32)

        e = c_ref[...] * h_ref[...]
        ebuf[8:8 + TM, :] = e
        w_all = w_ref[...]
        conv = w_all[0:1] * ebuf[pl.ds(6, TM), :] + w_all[1:2] * ebuf[pl.ds(7, TM), :] + w_all[2:3] * e
        cv_ref[...] = conv.astype(BF)
        y_ref[...] = (b_ref[...] * conv).astype(BF)
        ebuf[0:8, :] = ebuf[TM:TM + 8, :]

    def col(j):
        return pl.BlockSpec((TM, D_CONV), functools.partial(lambda i, j: (i, j), j=j))

    return _pcall(
        body, name=name, grid=(tp // TM,),
        in_specs=[col(0), col(1), col(2), pl.BlockSpec((3, D_CONV), lambda i: (0, 0))],
        out_specs=[col(0), col(0)], out_shape=[_sds((tp, D_CONV), BF)] * 2,
        scratch_shapes=[pltpu.VMEM((TM + 8, D_CONV), F32)], compiler_params=_params(1),
    )(p, p, p, w)


def merge_out_ln(ycv, o2, p, bg, wbc, wbm_p, wo, xprev, gp, bp, g, b, *, name):
    tp = ycv.shape[0]

    def body(y_ref, o_ref, gc_ref, gm_ref, bg_ref, wbc_ref, wbm_ref, wo_ref, xp_ref, gp_ref, bp_ref, g_ref, b_ref,
             bc_ref, bm_ref, mg_ref, xh_ref, rs_ref, hb_ref):
        bc = jnp.dot(y_ref[...], wbc_ref[...], preferred_element_type=F32)
        bm = jnp.dot(o_ref[...], wbm_ref[...], preferred_element_type=F32)
        bgv = bg_ref[...]
        mg = jax.nn.sigmoid(gc_ref[...] + bgv[0:1]) * bc + jax.nn.sigmoid(gm_ref[...] + bgv[1:2]) * bm
        mgb = mg.astype(BF)
        bc_ref[...] = bc.astype(BF)
        bm_ref[...] = bm.astype(BF)
        mg_ref[...] = mgb
        mix = jnp.dot(mgb, wo_ref[...], preferred_element_type=F32)
        hprev = xp_ref[...] * gp_ref[...] + bp_ref[...]
        _ln_store(ALPHA * hprev + mix, g_ref, b_ref, xh_ref, rs_ref, hb_ref)

    def rows(n, col=0):
        return pl.BlockSpec((TMH, n), functools.partial(lambda i, col: (i, col), col=col))

    def whole(r, c):
        return pl.BlockSpec((r, c), lambda i: (0, 0))

    ln_specs, ln_shapes = _ln_out(tp, TMH)
    return _pcall(
        body, name=name, grid=(tp // TMH,),
        in_specs=[rows(D_CONV), rows(MLA_HEADS * KV_PAD), rows(D_MODEL, 2), rows(D_MODEL, 3), whole(2, D_MODEL),
                  whole(D_CONV, D_MODEL), whole(MLA_HEADS * KV_PAD, D_MODEL), whole(D_MODEL, D_MODEL), rows(D_MODEL)]
        + [_row_vec(D_MODEL)] * 4,
        out_specs=[rows(D_MODEL)] * 3 + ln_specs, out_shape=[_sds((tp, D_MODEL), BF)] * 3 + ln_shapes,
        compiler_params=_params(1),
    )(ycv, o2, p, p, bg, wbc, wbm_p, wo, xprev, gp, bp, g, b)


def loss_grad(xh, g, b, target_p, n_real, *, name):
    tp = xh.shape[0]

    def body(x_ref, g_ref, b_ref, t_ref, dy_ref, loss_ref):
        i = pl.program_id(0)

        @pl.when(i == 0)
        def _():
            loss_ref[...] = jnp.zeros_like(loss_ref)

        row = i * TM + lax.broadcasted_iota(jnp.int32, (TM, 1), 0)
        real = (row >= N_META) & (row < N_META + n_real)
        diff = jnp.where(real, x_ref[...] * g_ref[...] + b_ref[...] - t_ref[...], 0.0)
        dy_ref[...] = diff * (1.0 / D_MODEL)
        loss_ref[...] += 0.5 / D_MODEL * jnp.sum(diff * diff)

    return _pcall(
        body, name=name, grid=(tp // TM,),
        in_specs=[pl.BlockSpec((TM, D_MODEL), lambda i: (i, 0)), _row_vec(D_MODEL), _row_vec(D_MODEL),
                  pl.BlockSpec((TM, D_MODEL), lambda i: (i, 0))],
        out_specs=[pl.BlockSpec((TM, D_MODEL), lambda i: (i, 0)), pl.BlockSpec((8, LANES), lambda i: (0, 0))],
        out_shape=[_sds((tp, D_MODEL), F32), _sds((8, LANES), F32)], compiler_params=_params(1),
    )(xh, g, b, target_p)


def ln_bwd(dh, xh, rstd, g, *, branch_scale, name):
    tp = dh.shape[0]

    def body(dh_ref, xh_ref, rs_ref, g_ref, dz_ref, dzb_ref, dg_ref, db_ref):
        i = pl.program_id(0)

        @pl.when(i == 0)
        def _():
            dg_ref[...] = jnp.zeros_like(dg_ref)
            db_ref[...] = jnp.zeros_like(db_ref)

        dy = dh_ref[...]
        xhat = xh_ref[...]
        dg_ref[...] += jnp.sum(dy * xhat, axis=0, keepdims=True)
        db_ref[...] += jnp.sum(dy, axis=0, keepdims=True)
        dxh = dy * g_ref[...]
        m1 = jnp.mean(dxh, axis=-1, keepdims=True)
        m2 = jnp.mean(dxh * xhat, axis=-1, keepdims=True)
        dz = rs_ref[...] * (dxh - m1 - xhat * m2)
        dz_ref[...] = dz
        dzb_ref[...] = (branch_scale * dz).astype(BF)

    rows = pl.BlockSpec((TM, D_MODEL), lambda i: (i, 0))
    return _pcall(
        body, name=name, grid=(tp // TM,),
        in_specs=[rows, rows, pl.BlockSpec((TM, 1), lambda i: (i, 0)), _row_vec(D_MODEL)],
        out_specs=[rows, rows, _row_vec(D_MODEL), _row_vec(D_MODEL)],
        out_shape=[_sds((tp, D_MODEL), F32), _sds((tp, D_MODEL), BF), _sds((1, D_MODEL), F32), _sds((1, D_MODEL), F32)],
        compiler_params=_params(1),
    )(dh, xh, rstd, g)


def ffn_down_bwd(dzb, wd, gate, up, *, name):
    tp = dzb.shape[0]
    tn = D_FF // 2

    def body(dz_ref, wd_ref, g_ref, u_ref, dg_ref, du_ref):
        da = lax.dot_general(dz_ref[...], wd_ref[...], NT, preferred_element_type=F32)
        g = g_ref[...].astype(F32)
        u = u_ref[...].astype(F32)
        sg = jax.nn.sigmoid(g)
        dg_ref[...] = (da * u * sg * (1.0 + g * (1.0 - sg))).astype(BF)
        du_ref[...] = (da * g * sg).astype(BF)

    blk = pl.BlockSpec((TM, tn), lambda i, j: (i, j))
    return _pcall(
        body, name=name, grid=(tp // TM, D_FF // tn),
        in_specs=[pl.BlockSpec((TM, D_MODEL), lambda i, j: (i, 0)), pl.BlockSpec((tn, D_MODEL), lambda i, j: (j, 0)), blk, blk],
        out_specs=[blk, blk], out_shape=[_sds((tp, D_FF), BF)] * 2, compiler_params=_params(2),
    )(dzb, wd, gate, up)


def merge_bwd(dzb, wo, bc, bm, p, bg, wbc, wbm_p, o2, *, name):
    tp = dzb.shape[0]
    nh = MLA_HEADS

    def body(dz_ref, wo_ref, bc_ref, bm_ref, gc_ref, gm_ref, bg_ref, wbc_ref, wbm_ref, o_ref,
             dbc_ref, dbm_ref, dgg_ref, dy_ref, do_ref, dl_ref, dbg_ref):
        i = pl.program_id(0)

        @pl.when(i == 0)
        def _():
            dbg_ref[...] = jnp.zeros_like(dbg_ref)

        dmg = lax.dot_general(dz_ref[...], wo_ref[...], NT, preferred_element_type=F32)
        bgv = bg_ref[...]
        sc = jax.nn.sigmoid(gc_ref[...] + bgv[0:1])
        sm = jax.nn.sigmoid(gm_ref[...] + bgv[1:2])
        dbc = (dmg * sc).astype(BF)
        dbm = (dmg * sm).astype(BF)
        dgc = dmg * bc_ref[...].astype(F32) * sc * (1.0 - sc)
        dgm = dmg * bm_ref[...].astype(F32) * sm * (1.0 - sm)
        dbc_ref[...] = dbc
        dbm_ref[...] = dbm
        dgg_ref[...] = jnp.concatenate([dgc, dgm], axis=1).astype(BF)
        dbg_ref[...] += jnp.concatenate([jnp.sum(dgc, axis=0, keepdims=True), jnp.sum(dgm, axis=0, keepdims=True)], axis=0)
        dy_ref[...] = lax.dot_general(dbc, wbc_ref[...], NT, preferred_element_type=F32)
        do = lax.dot_general(dbm, wbm_ref[...], NT, preferred_element_type=F32)
        do_ref[...] = do.astype(BF)
        prod = do * o_ref[...].astype(F32)
        parts = []
        for h in range(nh):
            d = jnp.sum(prod[:, h * KV_PAD:(h + 1) * KV_PAD], axis=1, keepdims=True)
            parts.append(jnp.broadcast_to(d, (TMH, LANES)))
        dl_ref[...] = jnp.concatenate(parts, axis=1)

    def rows(n, col=0):
        return pl.BlockSpec((TMH, n), functools.partial(lambda i, col: (i, col), col=col))

    def whole(r, c):
        return pl.BlockSpec((r, c), lambda i: (0, 0))

    return _pcall(
        body, name=name, grid=(tp // TMH,),
        in_specs=[rows(D_MODEL), whole(D_MODEL, D_MODEL), rows(D_MODEL), rows(D_MODEL), rows(D_MODEL, 2), rows(D_MODEL, 3),
                  whole(2, D_MODEL), whole(D_CONV, D_MODEL), whole(nh * KV_PAD, D_MODEL), rows(nh * KV_PAD)],
        out_specs=[rows(D_MODEL), rows(D_MODEL), rows(2 * D_MODEL), rows(D_CONV), rows(nh * KV_PAD), rows(nh * LANES),
                   whole(2, D_MODEL)],
        out_shape=[_sds((tp, D_MODEL), BF), _sds((tp, D_MODEL), BF), _sds((tp, 2 * D_MODEL), BF), _sds((tp, D_CONV), F32),
                   _sds((tp, nh * KV_PAD), BF), _sds((tp, nh * LANES), F32), _sds((2, D_MODEL), F32)],
        compiler_params=_params(1),
    )(dzb, wo, bc, bm, p, p, bg, wbc, wbm_p, o2)


def _attn_bwd_block(q, kvb, krb, do, lse, dl, qi, ki):
    rep = TM // LANES
    s, k2 = _causal_scores(q, kvb, krb, qi, ki)
    p = jnp.exp(s - jnp.tile(lse, (1, rep)))
    dp = lax.dot_general(do, kvb, NT, preferred_element_type=F32)
    ds = p * (dp - jnp.tile(dl, (1, rep)))
    return p.astype(BF), ds.astype(BF), k2


def attn_bwd_dkv(q2, kv, kr, do2, lse, dl, *, name):
    tp = q2.shape[0]
    nh = MLA_HEADS
    nb = tp // TM

    def body(q_ref, kv_ref, kr_ref, do_ref, lse_ref, dl_ref, dkv_ref, dkr_ref, dkv_acc, dkr_acc):
        ki = pl.program_id(0)
        h = pl.program_id(1)
        qi = pl.program_id(2)

        @pl.when(qi == 0)
        def _():
            dkv_acc[...] = jnp.zeros_like(dkv_acc)

        @pl.when((qi == 0) & (h == 0))
        def _():
            dkr_acc[...] = jnp.zeros_like(dkr_acc)

        @pl.when(qi >= ki)
        def _():
            q = q_ref[...]
            do = do_ref[...]
            pb, dsb, _ = _attn_bwd_block(q, kv_ref[...], kr_ref[...], do, lse_ref[...], dl_ref[...], qi, ki)
            dk2 = lax.dot_general(dsb, q, TN, preferred_element_type=F32)
            dkv_acc[...] += lax.dot_general(pb, do, TN, preferred_element_type=F32) + dk2[:, :KV_PAD]
            dkr_acc[...] += dk2[:, KV_PAD:KV_PAD + LANES]

        @pl.when(qi == nb - 1)
        def _():
            dkv_ref[...] = dkv_acc[...].astype(BF)

        @pl.when((qi == nb - 1) & (h == nh - 1))
        def _():
            dkr_ref[...] = dkr_acc[...]

    def qrow(n):
        return pl.BlockSpec((TM, n), lambda ki, h, qi: (jnp.maximum(qi, ki), h))

    return _pcall(
        body, name=name, grid=(nb, nh, nb),
        in_specs=[qrow(Q_PAD), pl.BlockSpec((TM, KV_PAD), lambda ki, h, qi: (ki, h)),
                  pl.BlockSpec((TM, LANES), lambda ki, h, qi: (ki, 0)), qrow(KV_PAD), qrow(LANES), qrow(LANES)],
        out_specs=[pl.BlockSpec((TM, KV_PAD), lambda ki, h, qi: (ki, h)), pl.BlockSpec((TM, LANES), lambda ki, h, qi: (ki, 0))],
        out_shape=[_sds((tp, nh * KV_PAD), BF), _sds((tp, LANES), F32)],
        scratch_shapes=[pltpu.VMEM((TM, KV_PAD), F32), pltpu.VMEM((TM, LANES), F32)], compiler_params=_params(3),
    )(q2, kv, kr, do2, lse, dl)


def attn_bwd_dq(q2, kv, kr, do2, lse, dl, *, name):
    tp = q2.shape[0]
    nh = MLA_HEADS
    nb = tp // TM

    def body(q_ref, kv_ref, kr_ref, do_ref, lse_ref, dl_ref, dq_ref, dq_acc):
        qi = pl.program_id(1)
        ki = pl.program_id(2)

        @pl.when(ki == 0)
        def _():
            dq_acc[...] = jnp.zeros_like(dq_acc)

        @pl.when(ki <= qi)
        def _():
            _, dsb, k2 = _attn_bwd_block(q_ref[...], kv_ref[...], kr_ref[...], do_ref[...], lse_ref[...], dl_ref[...], qi, ki)
            dq_acc[...] += jnp.dot(dsb, k2, preferred_element_type=F32)

        @pl.when(ki == nb - 1)
        def _():
            dq_ref[...] = dq_acc[...]

    def qrow(n):
        return pl.BlockSpec((TM, n), lambda h, qi, ki: (qi, h))

    return _pcall(
        body, name=name, grid=(nh, nb, nb),
        in_specs=[qrow(Q_PAD), pl.BlockSpec((TM, KV_PAD), lambda h, qi, ki: (jnp.minimum(ki, qi), h)),
                  pl.BlockSpec((TM, LANES), lambda h, qi, ki: (jnp.minimum(ki, qi), 0)), qrow(KV_PAD), qrow(LANES), qrow(LANES)],
        out_specs=qrow(Q_PAD), out_shape=_sds((tp, nh * Q_PAD), F32),
        scratch_shapes=[pltpu.VMEM((TM, Q_PAD), F32)], compiler_params=_params(3),
    )(q2, kv, kr, do2, lse, dl)


def _rms_bwd(x, g, dy):
    r = lax.rsqrt(jnp.mean(x * x, axis=-1, keepdims=True) + RMS_EPS)
    gy = dy * g
    dx = r * gy - x * (r * r * r) * jnp.mean(x * gy, axis=-1, keepdims=True)
    return dx, jnp.sum(dy * x * r, axis=0, keepdims=True)


def mla_prep_bwd(dq2, dkv, dkr, p, gq, gkv, wuq_p, wukv, tabs_bwd, *, name):
    tp = dq2.shape[0]
    nh = MLA_HEADS

    def body(dq_ref, dkv_ref, dkr_ref, cq_ref, ckv_ref, gq_ref, gkv_ref, wuq_ref, wukv_ref,
             cq_t, s1q_t, s2q_t, ck_t, s1k_t, s2k_t, dqb_ref, dsm_ref, dgq_ref, dgkv_ref):
        i = pl.program_id(0)

        @pl.when(i == 0)
        def _():
            dgq_ref[...] = jnp.zeros_like(dgq_ref)
            dgkv_ref[...] = jnp.zeros_like(dgkv_ref)

        dqb = _rope(dq_ref[...], cq_t[...], s1q_t[...], s2q_t[...], nh).astype(BF)
        dqb_ref[...] = dqb
        dcqn = lax.dot_general(dqb, wuq_ref[...], NT, preferred_element_type=F32)
        dcq, dgq = _rms_bwd(cq_ref[...], gq_ref[...], dcqn)
        dckvn = lax.dot_general(dkv_ref[...], wukv_ref[...], NT, preferred_element_type=F32)
        dckv, dgkv = _rms_bwd(ckv_ref[...], gkv_ref[...], dckvn)
        dkr = _rope(dkr_ref[...], ck_t[...], s1k_t[...], s2k_t[...], 1)
        dsm_ref[...] = jnp.concatenate([dcq, dckv, dkr], axis=1)
        dgq_ref[...] += dgq
        dgkv_ref[...] += dgkv

    def rows(n, col=0):
        return pl.BlockSpec((TMH, n), functools.partial(lambda i, col: (i, col), col=col))

    return _pcall(
        body, name=name, grid=(tp // TMH,),
        in_specs=[rows(nh * Q_PAD), rows(nh * KV_PAD), rows(LANES), rows(Q_LORA, 1536 // Q_LORA), rows(KV_LORA, 1792 // KV_LORA),
                  _row_vec(Q_LORA), _row_vec(KV_LORA),
                  pl.BlockSpec((Q_LORA, nh * Q_PAD), lambda i: (0, 0)), pl.BlockSpec((KV_LORA, nh * KV_PAD), lambda i: (0, 0)),
                  rows(Q_PAD), rows(Q_PAD), rows(Q_PAD), rows(LANES), rows(LANES), rows(LANES)],
        out_specs=[rows(nh * Q_PAD), rows(Q_LORA + KV_LORA + LANES), _row_vec(Q_LORA), _row_vec(KV_LORA)],
        out_shape=[_sds((tp, nh * Q_PAD), BF), _sds((tp, Q_LORA + KV_LORA + LANES), F32), _sds((1, Q_LORA), F32),
                   _sds((1, KV_LORA), F32)],
        compiler_params=_params(1),
    )(dq2, dkv, dkr, p, p, gq, gkv, wuq_p, wukv, *tabs_bwd)


def conv_bwd(dy, p, conv, w, *, name):
    tp = dy.shape[0]
    nb = tp // TM

    def body(dy_ref, b_ref, c_ref, h_ref, cv_ref, w_ref, dp_ref, dw0_ref, dw1_ref, dw2_ref, dbuf):
        i = pl.program_id(0)

        @pl.when(i == 0)
        def _():
            dbuf[TM:TM + 8, :] = jnp.zeros((8, D_CONV), F32)
            dw0_ref[...] = jnp.zeros_like(dw0_ref)
            dw1_ref[...] = jnp.zeros_like(dw1_ref)
            dw2_ref[...] = jnp.zeros_like(dw2_ref)

        dyv = dy_ref[...]
        c = c_ref[...]
        hh = h_ref[...]
        dconv = dyv * b_ref[...]
        dbuf[0:TM, :] = dconv
        d1 = dbuf[pl.ds(1, TM), :]
        d2 = dbuf[pl.ds(2, TM), :]
        w_all = w_ref[...]
        de = w_all[2:3] * dconv + w_all[1:2] * d1 + w_all[0:1] * d2
        e = c * hh
        dp_ref[...] = jnp.concatenate([dyv * cv_ref[...].astype(F32), de * hh, de * c], axis=1).astype(BF)
        dw0_ref[...] += jnp.sum(d2 * e, axis=0, keepdims=True)
        dw1_ref[...] += jnp.sum(d1 * e, axis=0, keepdims=True)
        dw2_ref[...] += jnp.sum(dconv * e, axis=0, keepdims=True)
        dbuf[TM:TM + 8, :] = dbuf[0:8, :]

    def col(j):
        return pl.BlockSpec((TM, D_CONV), functools.partial(lambda i, j: (nb - 1 - i, j), j=j))

    return _pcall(
        body, name=name, grid=(nb,),
        in_specs=[col(0), col(0), col(1), col(2), col(0), pl.BlockSpec((3, D_CONV), lambda i: (0, 0))],
        out_specs=[pl.BlockSpec((TM, 3 * D_CONV), lambda i: (nb - 1 - i, 0))] + [_row_vec(D_CONV)] * 3,
        out_shape=[_sds((tp, 3 * D_CONV), BF)] + [_sds((1, D_CONV), F32)] * 3,
        scratch_shapes=[pltpu.VMEM((TM + 8, D_CONV), F32)], compiler_params=_params(1),
    )(dy, p, p, p, conv, w)


def adamw(w, g, m, v, *, name):
    r, c = w.shape
    tr = r
    for cand in (256, 128, 64, 32, 16, 8):
        if r % cand == 0 and r > cand:
            tr = cand
            break

    def body(w_ref, g_ref, m_ref, v_ref, d_ref, nm_ref, nv_ref):
        gv = g_ref[...]
        nm = ADAM_B1 * m_ref[...] + (1.0 - ADAM_B1) * gv
        nv = ADAM_B2 * v_ref[...] + (1.0 - ADAM_B2) * (gv * gv)
        m_hat = nm / (1.0 - ADAM_B1 ** ADAM_STEP)
        v_hat = nv / (1.0 - ADAM_B2 ** ADAM_STEP)
        d_ref[...] = -ADAM_LR * (m_hat / (jnp.sqrt(v_hat) + ADAM_EPS) + ADAM_WD * w_ref[...])
        nm_ref[...] = nm
        nv_ref[...] = nv

    blk = pl.BlockSpec((tr, c), lambda i: (i, 0))
    return _pcall(
        body, name=name, grid=(r // tr,), in_specs=[blk] * 4, out_specs=[blk] * 3,
        out_shape=[_sds((r, c), F32)] * 3, compiler_params=_params(1),
    )(w, g, m, v)


HBM_SPEC = pl.BlockSpec(memory_space=pltpu.HBM)


def _place():
    return lax.axis_index("x"), lax.axis_index("y"), lax.axis_index("c")


def _other_chips(x, y):
    return [(1 - x, y), (x, 1 - y), (1 - x, 1 - y)]


def gather_chips(shard, *, name):
    r, c_cols = shard.shape
    rh = r // 2

    def body(x_ref, o_ref, send_sems, recv_sems, local_sem):
        x, y, c = _place()
        me = 2 * x + y
        chips = _other_chips(x, y)
        mine = pltpu.make_async_copy(x_ref, o_ref.at[me], local_sem)
        mine.start()
        half = pl.ds(c * rh, rh)
        other = pl.ds((1 - c) * rh, rh)

        def copy(k, src, dst, to):
            return pltpu.make_async_remote_copy(src_ref=src, dst_ref=dst, send_sem=send_sems.at[k], recv_sem=recv_sems.at[k],
                                                device_id=to, device_id_type=MESH)

        sends = [copy(k, x_ref.at[half], o_ref.at[me, half], (px, py, c)) for k, (px, py) in enumerate(chips)]
        for cp in sends:
            cp.start()
        passed = []
        for k, (px, py) in enumerate(chips):
            j = 2 * px + py
            copy(k, x_ref.at[half], o_ref.at[j, half], (px, py, c)).wait_recv()
            fwd = copy(3 + k, o_ref.at[j, half], o_ref.at[j, half], (x, y, 1 - c))
            fwd.start()
            passed.append(fwd)
        for k, (px, py) in enumerate(chips):
            j = 2 * px + py
            copy(3 + k, o_ref.at[j, other], o_ref.at[j, other], (x, y, 1 - c)).wait_recv()
        for cp in sends + passed:
            cp.wait_send()
        mine.wait()

    return _pcall(
        body, name=name, in_specs=[HBM_SPEC], out_specs=HBM_SPEC, out_shape=_sds((4, r, c_cols), shard.dtype),
        scratch_shapes=[pltpu.SemaphoreType.DMA((6,)), pltpu.SemaphoreType.DMA((6,)), pltpu.SemaphoreType.DMA],
    )(shard)


def pair_exchange(g, *, name):
    n, r, c_cols = g.shape
    rh = r // 2

    def body(g_ref, o_ref, send_sem, recv_sem):
        x, y, c = _place()
        cp = pltpu.make_async_remote_copy(src_ref=g_ref.at[:, pl.ds((1 - c) * rh, rh)], dst_ref=o_ref, send_sem=send_sem,
                                          recv_sem=recv_sem, device_id=(x, y, 1 - c), device_id_type=MESH)
        cp.start()
        cp.wait()

    return _pcall(
        body, name=name, in_specs=[HBM_SPEC], out_specs=HBM_SPEC, out_shape=_sds((n, rh, c_cols), g.dtype),
        scratch_shapes=[pltpu.SemaphoreType.DMA, pltpu.SemaphoreType.DMA],
    )(g)


def pair_add(g, s1, c_idx, *, name):
    n, r, c_cols = g.shape
    rh = r // 2
    rb = COMM_ROW_BLOCK
    nrb = rh // rb

    def body(c_ref, g_ref, s_ref, o_ref):
        o_ref[...] = (g_ref[...].astype(F32) + s_ref[...].astype(F32)).astype(o_ref.dtype)

    grid_spec = pltpu.PrefetchScalarGridSpec(
        num_scalar_prefetch=1, grid=(n, nrb),
        in_specs=[pl.BlockSpec((1, rb, c_cols), lambda j, i, c_ref: (j, c_ref[0] * nrb + i, 0)),
                  pl.BlockSpec((1, rb, c_cols), lambda j, i, c_ref: (j, i, 0))],
        out_specs=pl.BlockSpec((1, rb, c_cols), lambda j, i, c_ref: (j, i, 0)),
    )
    return _pcall(body, name=name, grid_spec=grid_spec, out_shape=_sds((n, rh, c_cols), g.dtype), compiler_params=_params(2))(
        c_idx, g, s1)


def chip_scatter(ps, *, name):
    n, rh, c_cols = ps.shape

    def body(p_ref, o_ref, send_sems, recv_sems, local_sem):
        x, y, c = _place()
        me = 2 * x + y
        chips = _other_chips(x, y)
        mine = pltpu.make_async_copy(p_ref.at[me], o_ref.at[me], local_sem)
        mine.start()

        def copy(k, j_src, j_dst, to):
            return pltpu.make_async_remote_copy(src_ref=p_ref.at[j_src], dst_ref=o_ref.at[j_dst], send_sem=send_sems.at[k],
                                                recv_sem=recv_sems.at[k], device_id=to, device_id_type=MESH)

        sends = [copy(k, 2 * px + py, me, (px, py, c)) for k, (px, py) in enumerate(chips)]
        for cp in sends:
            cp.start()
        for k, (px, py) in enumerate(chips):
            copy(k, me, 2 * px + py, (px, py, c)).wait_recv()
        for cp in sends:
            cp.wait_send()
        mine.wait()

    return _pcall(
        body, name=name, in_specs=[HBM_SPEC], out_specs=HBM_SPEC, out_shape=_sds((n, rh, c_cols), ps.dtype),
        scratch_shapes=[pltpu.SemaphoreType.DMA((3,)), pltpu.SemaphoreType.DMA((3,)), pltpu.SemaphoreType.DMA],
    )(ps)


def sum_chunks(s2, *, name):
    n, rh, c_cols = s2.shape
    rb = COMM_ROW_BLOCK

    def body(s_ref, o_ref):
        acc = s_ref[0].astype(F32)
        for j in range(1, n):
            acc = acc + s_ref[j].astype(F32)
        o_ref[...] = acc

    return _pcall(
        body, name=name, grid=(rh // rb,), in_specs=[pl.BlockSpec((n, rb, c_cols), lambda i: (0, i, 0))],
        out_specs=pl.BlockSpec((rb, c_cols), lambda i: (i, 0)), out_shape=_sds((rh, c_cols), F32), compiler_params=_params(1),
    )(s2)


def pair_gather(rc, *, name):
    rh, c_cols = rc.shape

    def body(r_ref, o_ref, send_sem, recv_sem, local_sem):
        x, y, c = _place()
        mine = pltpu.make_async_copy(r_ref, o_ref.at[pl.ds(c * rh, rh)], local_sem)
        mine.start()
        cp = pltpu.make_async_remote_copy(src_ref=r_ref, dst_ref=o_ref.at[pl.ds(c * rh, rh)], send_sem=send_sem, recv_sem=recv_sem,
                                          device_id=(x, y, 1 - c), device_id_type=MESH)
        cp.start()
        pltpu.make_async_remote_copy(src_ref=r_ref, dst_ref=o_ref.at[pl.ds((1 - c) * rh, rh)], send_sem=send_sem, recv_sem=recv_sem,
                                     device_id=(x, y, 1 - c), device_id_type=MESH).wait_recv()
        cp.wait_send()
        mine.wait()

    return _pcall(
        body, name=name, in_specs=[HBM_SPEC], out_specs=HBM_SPEC, out_shape=_sds((2 * rh, c_cols), rc.dtype),
        scratch_shapes=[pltpu.SemaphoreType.DMA, pltpu.SemaphoreType.DMA, pltpu.SemaphoreType.DMA],
    )(rc)


def allreduce_small(v, *, name):
    rows, lanes = v.shape

    def body(v_ref, o_ref, buf, send_sems, recv_sems):
        x, y, c = _place()
        me = 4 * x + 2 * y + c
        buf[me] = v_ref[...]

        def peer(k):
            dx, dy, dc = (k >> 2) & 1, (k >> 1) & 1, k & 1
            return (1 - x if dx else x, 1 - y if dy else y, 1 - c if dc else c)

        def copy(k, slot, to):
            return pltpu.make_async_remote_copy(src_ref=v_ref, dst_ref=buf.at[slot], send_sem=send_sems.at[k - 1],
                                                recv_sem=recv_sems.at[k - 1], device_id=to, device_id_type=MESH)

        sends = [copy(k, me, peer(k)) for k in range(1, 8)]
        for cp in sends:
            cp.start()
        for k in range(1, 8):
            px, py, pc = peer(k)
            copy(k, 4 * px + 2 * py + pc, peer(k)).wait_recv()
        for cp in sends:
            cp.wait_send()
        acc = buf[0]
        for d in range(1, 8):
            acc = acc + buf[d]
        o_ref[...] = acc

    vmem = pl.BlockSpec(memory_space=pltpu.VMEM)
    return _pcall(
        body, name=name, in_specs=[vmem], out_specs=vmem, out_shape=_sds((rows, lanes), F32),
        scratch_shapes=[pltpu.VMEM((8, rows, lanes), F32), pltpu.SemaphoreType.DMA((7,)), pltpu.SemaphoreType.DMA((7,))],
    )(v)


def _pad_rows(n, mult):
    return -(-n // mult) * mult


def _pack(arrs, dtype, rows_mult, cols):
    flat = jnp.concatenate([a.astype(dtype).reshape(-1) for a in arrs])
    rows = _pad_rows(-(-flat.shape[0] // cols), rows_mult)
    flat = jnp.pad(flat, (0, rows * cols - flat.shape[0]))
    return flat.reshape(rows, cols)


def _unpack(flat, shapes):
    out, off = [], 0
    for s in shapes:
        n = 1
        for d in s:
            n *= d
        out.append(flat[off:off + n].reshape(s))
        off += n
    return out


def _rope_tables(tp):
    inv_freq = 1.0 / (ROPE_BASE ** (jnp.arange(0, QK_ROPE, 2, dtype=F32) / QK_ROPE))
    ang = jnp.arange(tp, dtype=F32)[:, None] * inv_freq[None, :]
    cos, sin = jnp.cos(ang), jnp.sin(ang)
    one = lambda n: jnp.ones((tp, n), F32)
    zero = lambda n: jnp.zeros((tp, n), F32)
    cq = jnp.concatenate([one(128), cos, cos, one(96)], axis=1)
    s1q = jnp.concatenate([zero(144), sin, zero(96)], axis=1)
    s2q = jnp.concatenate([zero(128), -sin, zero(112)], axis=1)
    ck = jnp.concatenate([cos, cos, zero(96)], axis=1)
    s1k = jnp.concatenate([zero(16), sin, zero(96)], axis=1)
    s2k = jnp.concatenate([-sin, zero(112)], axis=1)
    fwd = (cq * ATT_SCALE, s1q * ATT_SCALE, s2q * ATT_SCALE, ck, s1k, s2k)
    bwd = (cq * ATT_SCALE, -s1q * ATT_SCALE, -s2q * ATT_SCALE, ck, -s1k, -s2k)
    return fwd, bwd


def _pad_w_in(w):
    return jnp.concatenate([w[:, :1952], jnp.zeros((w.shape[0], 96), w.dtype), w[:, 1952:]], axis=1)


def _pad_w_uq(w):
    w = w.reshape(Q_LORA, MLA_HEADS, QK_NOPE + QK_ROPE)
    z = lambda n: jnp.zeros((Q_LORA, MLA_HEADS, n), w.dtype)
    return jnp.concatenate([w[..., :QK_NOPE], z(64), w[..., QK_NOPE:], z(96)], axis=-1).reshape(Q_LORA, MLA_HEADS * Q_PAD)


def _unpad_w_uq(w):
    w = w.reshape(Q_LORA, MLA_HEADS, Q_PAD)
    return jnp.concatenate([w[..., :QK_NOPE], w[..., 128:128 + QK_ROPE]], axis=-1).reshape(Q_LORA, MLA_HEADS * (QK_NOPE + QK_ROPE))


def _pad_w_br_mla(w):
    w = w.reshape(MLA_HEADS, V_HEAD, D_MODEL)
    return jnp.concatenate([jnp.zeros_like(w), w], axis=1).reshape(MLA_HEADS * KV_PAD, D_MODEL)


def _unpad_w_br_mla(w):
    return w.reshape(MLA_HEADS, KV_PAD, D_MODEL)[:, V_HEAD:].reshape(MLA_HEADS * V_HEAD, D_MODEL)


def _layer_fwd(l, st, xprev, gp, bp, hb, w, tabs):
    ln_g, ln_b = w["ln_g"], w["ln_b"]
    lg = lambda k: ln_g[l, k][None]
    lb = lambda k: ln_b[l, k][None]
    s = {}
    s["x0"], s["gp0"], s["bp0"], s["hb0"] = xprev, gp, bp, hb
    s["g1"], s["u1"], s["a1"] = ffn_up(hb, w["ffn1_w_up"][l], name="ffn_up")
    s["xh1"], s["rs1"], s["hb1"] = down_ln(s["a1"], w["ffn1_w_down"][l], xprev, gp, bp, lg(0), lb(0), name="ffn_down_ln")
    s["p"] = mm_rows([(s["hb1"], w["mix_w_in"][l], False, 0)], D_IN_PAD, name="mix_in", tn=1024)
    gq, gkv = w["q_norm_g"][l][None], w["kv_norm_g"][l][None]
    s["cqn"], s["ckvn"], s["q2"], s["kv"], s["kr"] = mla_prep(s["p"], gq, gkv, w["w_uq"][l], w["w_ukv"][l], tabs, name="mla_prep")
    s["o2"], s["lse"] = attn_fwd(s["q2"], s["kv"], s["kr"], name="attn_fwd")
    s["ycv"], s["conv"] = conv_fwd(s["p"], w["conv_w"][l], name="conv_fwd")
    s["bc"], s["bm"], s["mg"], s["xh2"], s["rs2"], s["hb2"] = merge_out_ln(
        s["ycv"], s["o2"], s["p"], w["mix_b_gate"][l], w["w_br_conv"][l], w["w_br_mla"][l], w["w_o"][l],
        s["xh1"], lg(0), lb(0), lg(1), lb(1), name="merge_out_ln")
    s["g2"], s["u2"], s["a2"] = ffn_up(s["hb2"], w["ffn2_w_up"][l], name="ffn_up")
    s["xh3"], s["rs3"], s["hb3"] = down_ln(s["a2"], w["ffn2_w_down"][l], s["xh2"], lg(1), lb(1), lg(2), lb(2), name="ffn_down_ln")
    st.append(s)
    return s["xh3"], lg(2), lb(2), s["hb3"]


def _ffn_bwd(dh, s, k, w_up, w_down, ln_gain, hb_in, gate, up, act, xh, rs):
    dz, dzb, dgam, dbet = ln_bwd(dh, xh, rs, ln_gain, branch_scale=0.5, name="ln_bwd")
    d_wd = tn_mm(act, dzb, tm=D_FF // 2, name="dw_down")
    dgate, dup = ffn_down_bwd(dzb, w_down, gate, up, name="ffn_down_bwd")
    d_wg = tn_mm(hb_in, dgate, tm=512, name="dw_up")
    d_wu = tn_mm(hb_in, dup, tm=512, name="dw_up")
    dh_in = mm_rows([(dgate, w_up, True, 0), (dup, w_up, True, 1)], D_MODEL, name="ffn_up_bwd", tn=512, addend=dz, add_scale=ALPHA)
    return dh_in, jnp.concatenate([d_wg, d_wu], axis=1), d_wd, dgam, dbet


def _layer_bwd(l, s, dh, w, tabs_bwd):
    ln_g = w["ln_g"]
    lg = lambda k: ln_g[l, k][None]
    g = {}
    dh, g["ffn2_w_up"], g["ffn2_w_down"], dg2, db2 = _ffn_bwd(
        dh, s, 2, w["ffn2_w_up"][l], w["ffn2_w_down"][l], lg(2), s["hb2"], s["g2"], s["u2"], s["a2"], s["xh3"], s["rs3"])
    dz, dzb, dg1, db1 = ln_bwd(dh, s["xh2"], s["rs2"], lg(1), branch_scale=1.0, name="ln_bwd")
    g["w_o"] = tn_mm(s["mg"], dzb, tm=1024, name="dw_o")
    dbc, dbm, dgg, dycv, do2, dl, g["mix_b_gate"] = merge_bwd(
        dzb, w["w_o"][l], s["bc"], s["bm"], s["p"], w["mix_b_gate"][l], w["w_br_conv"][l], w["w_br_mla"][l], s["o2"], name="merge_bwd")
    g["w_br_conv"] = tn_mm(s["ycv"], dbc, tm=512, name="dw_br_conv")
    g["w_br_mla"] = _unpad_w_br_mla(tn_mm(s["o2"], dbm, tm=1024, name="dw_br_mla"))
    dkv, dkr = attn_bwd_dkv(s["q2"], s["kv"], s["kr"], do2, s["lse"], dl, name="attn_bwd_dkv")
    dq2 = attn_bwd_dq(s["q2"], s["kv"], s["kr"], do2, s["lse"], dl, name="attn_bwd_dq")
    gq, gkv = w["q_norm_g"][l][None], w["kv_norm_g"][l][None]
    dqb, dsm, dgq, dgkv = mla_prep_bwd(dq2, dkv, dkr, s["p"], gq, gkv, w["w_uq"][l], w["w_ukv"][l], tabs_bwd, name="mla_prep_bwd")
    g["q_norm_g"], g["kv_norm_g"] = dgq[0], dgkv[0]
    g["w_uq"] = _unpad_w_uq(tn_mm(s["cqn"], dqb, tm=Q_LORA, name="dw_uq"))
    g["w_ukv"] = tn_mm(s["ckvn"], dkv, tm=KV_LORA, name="dw_ukv")
    dbch, dw0, dw1, dw2 = conv_bwd(dycv, s["p"], s["conv"], w["conv_w"][l], name="conv_bwd")
    g["conv_w"] = jnp.concatenate([dw0, dw1, dw2], axis=0)
    w_in = w["mix_w_in"][l]
    d_in = [tn_mm(s["hb1"], dbch, tm=512, name="dw_in_bch"), tn_mm(s["hb1"], dsm, tm=1024, name="dw_in_sm"),
            tn_mm(s["hb1"], dgg, tm=512, name="dw_in_gg")]
    d_in = jnp.concatenate(d_in, axis=1)
    g["mix_w_in"] = jnp.concatenate([d_in[:, :1952], d_in[:, 2048:]], axis=1)
    dh = mm_rows([(dbch, w_in, True, 0), (dsm, w_in, True, 3), (dgg, w_in, True, 1)], D_MODEL, name="mix_in_bwd", tn=512,
                 addend=dz, add_scale=ALPHA)
    dh, g["ffn1_w_up"], g["ffn1_w_down"], dg0, db0 = _ffn_bwd(
        dh, s, 0, w["ffn1_w_up"][l], w["ffn1_w_down"][l], lg(0), s["hb0"], s["g1"], s["u1"], s["a1"], s["xh1"], s["rs1"])
    g["ln_g"] = jnp.concatenate([dg0, dg1, dg2], axis=0)
    g["ln_b"] = jnp.concatenate([db0, db1, db2], axis=0)
    return dh, g


BIG = ("ffn1_w_up", "ffn1_w_down", "mix_w_in", "w_uq", "w_ukv", "w_br_conv", "w_br_mla", "w_o", "ffn2_w_up", "ffn2_w_down")
BIG_AXIS = (2, 1, 2, 2, 2, 2, 2, 1, 2, 1)
SMALL_SHARDED = ("meta_tokens", "mix_b_gate", "conv_w", "ln_g", "ln_b")
SMALL_REPLICATED = ("q_norm_g", "kv_norm_g")
WEIGHTS = ("meta_tokens", "ffn1_w_up", "ffn1_w_down", "mix_w_in", "mix_b_gate", "conv_w", "q_norm_g", "w_uq", "kv_norm_g", "w_ukv",
           "w_br_conv", "w_br_mla", "w_o", "ffn2_w_up", "ffn2_w_down", "ln_g", "ln_b")


def _view2d(a):
    return a.reshape(-1, a.shape[-1])


def _local_grads(x_row, target_row, w):
    seq = x_row.shape[0]
    t_real = N_META + seq
    tp = _pad_rows(t_real, TM)
    pad = tp - t_real
    h0 = jnp.concatenate([w["meta_tokens"], x_row, jnp.zeros((pad, D_MODEL), F32)], axis=0)
    target_p = jnp.concatenate([jnp.zeros((N_META, D_MODEL), F32), target_row, jnp.zeros((pad, D_MODEL), F32)], axis=0)
    tabs, tabs_bwd = _rope_tables(tp)
    ones = jnp.ones((1, D_MODEL), F32)
    zeros = jnp.zeros((1, D_MODEL), F32)
    saved = []
    cur = (h0, ones, zeros, h0.astype(BF))
    for l in range(DEPTH):
        cur = _layer_fwd(l, saved, *cur, w, tabs)
    dh, loss_acc = loss_grad(cur[0], cur[1], cur[2], target_p, seq, name="loss_grad")
    grads = [None] * DEPTH
    for l in reversed(range(DEPTH)):
        dh, grads[l] = _layer_bwd(l, saved[l], dh, w, tabs_bwd)
    gfull = {n: jnp.stack([grads[l][n] for l in range(DEPTH)]) for n in grads[0]}
    gfull["meta_tokens"] = dh[:N_META]
    return loss_acc, dh[N_META:t_real], gfull


def kernel(x, meta_tokens, ffn1_w_up, ffn1_w_down, mix_w_in, mix_b_gate, conv_w, q_norm_g, w_uq, kv_norm_g, w_ukv, w_br_conv, w_br_mla, w_o, ffn2_w_up, ffn2_w_down, ln_g, ln_b, loss_target, m_meta_tokens, m_ffn1_w_up, m_ffn1_w_down, m_mix_w_in, m_mix_b_gate, m_conv_w, m_q_norm_g, m_w_uq, m_kv_norm_g, m_w_ukv, m_w_br_conv, m_w_br_mla, m_w_o, m_ffn2_w_up, m_ffn2_w_down, m_ln_g, m_ln_b, v_meta_tokens, v_ffn1_w_up, v_ffn1_w_down, v_mix_w_in, v_mix_b_gate, v_conv_w, v_q_norm_g, v_w_uq, v_kv_norm_g, v_w_ukv, v_w_br_conv, v_w_br_mla, v_w_o, v_ffn2_w_up, v_ffn2_w_down, v_ln_g, v_ln_b):
    local = dict(meta_tokens=meta_tokens, ffn1_w_up=ffn1_w_up, ffn1_w_down=ffn1_w_down, mix_w_in=mix_w_in, mix_b_gate=mix_b_gate,
                 conv_w=conv_w, q_norm_g=q_norm_g, w_uq=w_uq, kv_norm_g=kv_norm_g, w_ukv=w_ukv, w_br_conv=w_br_conv,
                 w_br_mla=w_br_mla, w_o=w_o, ffn2_w_up=ffn2_w_up, ffn2_w_down=ffn2_w_down, ln_g=ln_g, ln_b=ln_b)
    mom_m = dict(zip(WEIGHTS, (m_meta_tokens, m_ffn1_w_up, m_ffn1_w_down, m_mix_w_in, m_mix_b_gate, m_conv_w, m_q_norm_g, m_w_uq,
                               m_kv_norm_g, m_w_ukv, m_w_br_conv, m_w_br_mla, m_w_o, m_ffn2_w_up, m_ffn2_w_down, m_ln_g, m_ln_b)))
    mom_v = dict(zip(WEIGHTS, (v_meta_tokens, v_ffn1_w_up, v_ffn1_w_down, v_mix_w_in, v_mix_b_gate, v_conv_w, v_q_norm_g, v_w_uq,
                               v_kv_norm_g, v_w_ukv, v_w_br_conv, v_w_br_mla, v_w_o, v_ffn2_w_up, v_ffn2_w_down, v_ln_g, v_ln_b)))
    xi, yi, ci = _place()
    chip = 2 * xi + yi

    big_rows_mult = 2 * COMM_ROW_BLOCK
    packed = _pack([local[n] for n in BIG], BF, big_rows_mult, COMM_COLS)
    gathered = gather_chips(packed, name="gather_weights").reshape(4, -1)
    w = {}
    off = 0
    for n, ax in zip(BIG, BIG_AXIS):
        sz = local[n].size
        w[n] = jnp.concatenate([gathered[j, off:off + sz].reshape(local[n].shape) for j in range(4)], axis=ax)
        off += sz
    small_vec = _pack([local[n] for n in SMALL_SHARDED], F32, 8, LANES)
    sr = small_vec.shape[0]
    own = (jnp.arange(4)[:, None, None] == chip) & (ci == 0)
    small_all = allreduce_small(jnp.where(own, small_vec[None], 0.0).reshape(4 * sr, LANES), name="gather_small").reshape(4, -1)
    off = 0
    for n in SMALL_SHARDED:
        sz = local[n].size
        w[n] = jnp.concatenate([small_all[j, off:off + sz].reshape(local[n].shape) for j in range(4)], axis=-1)
        off += sz
    for n in SMALL_REPLICATED:
        w[n] = local[n]
    w["mix_w_in"] = jnp.stack([_pad_w_in(w["mix_w_in"][l]) for l in range(DEPTH)])
    w["w_uq"] = jnp.stack([_pad_w_uq(w["w_uq"][l]) for l in range(DEPTH)])
    w["w_br_mla"] = jnp.stack([_pad_w_br_mla(w["w_br_mla"][l]) for l in range(DEPTH)])

    loss_acc, grad_x, gfull = _local_grads(x[0], loss_target[0], w)
    grad_x = grad_x[None]

    chunks = []
    for j in range(4):
        parts = []
        for n, ax in zip(BIG, BIG_AXIS):
            sh = local[n].shape[ax]
            parts.append(lax.slice_in_dim(gfull[n], j * sh, (j + 1) * sh, axis=ax))
        chunks.append(_pack(parts, BF, big_rows_mult, COMM_COLS))
    gsend = jnp.stack(chunks)
    from_sibling = pair_exchange(gsend, name="rs_pair_exchange")
    pair_sum = pair_add(gsend, from_sibling, jnp.reshape(ci, (1,)).astype(jnp.int32), name="rs_pair_add")
    from_chips = chip_scatter(pair_sum, name="rs_chip_scatter")
    reduced_half = sum_chunks(from_chips, name="rs_sum")
    reduced = pair_gather(reduced_half, name="rs_pair_gather").reshape(-1)
    gshard = dict(zip(BIG, _unpack(reduced, [local[n].shape for n in BIG])))

    small_names = SMALL_SHARDED + SMALL_REPLICATED
    small_full = [gfull[n] for n in small_names] + [loss_acc[0:1, :]]
    small_red = allreduce_small(_pack(small_full, F32, 8, LANES), name="reduce_small").reshape(-1)
    small_out = _unpack(small_red, [a.shape for a in small_full])
    loss = small_out[-1][0, 0]
    for n, full in zip(small_names, small_out[:-1]):
        if n in SMALL_SHARDED:
            sh = local[n].shape[-1]
            gshard[n] = lax.dynamic_slice_in_dim(full, chip * sh, sh, axis=full.ndim - 1)
        else:
            gshard[n] = full

    delta, new_m, new_v = {}, {}, {}
    for n in WEIGHTS:
        shape = local[n].shape
        d, nm, nv = adamw(_view2d(local[n]), _view2d(gshard[n]), _view2d(mom_m[n]), _view2d(mom_v[n]), name="adamw")
        delta[n], new_m[n], new_v[n] = d.reshape(shape), nm.reshape(shape), nv.reshape(shape)
    return (loss, grad_x, *[gshard[n] for n in WEIGHTS], *[delta[n] for n in WEIGHTS], *[new_m[n] for n in WEIGHTS],
            *[new_v[n] for n in WEIGHTS])
```

```python
import functools

import jax
import jax.numpy as jnp
from jax import lax
from jax.experimental import pallas as pl
from jax.experimental.pallas import tpu as pltpu

F32 = jnp.float32
BF = jnp.bfloat16
MESH = pl.DeviceIdType.MESH

D_MODEL = 1024
DEPTH = 2
N_META = 16
D_CONV = 512
MLA_HEADS = 8
QK_NOPE = 64
QK_ROPE = 32
V_HEAD = 64
Q_LORA = 256
KV_LORA = 128
ROPE_BASE = 10000.0
NEG_INF = -1e30
D_FF = 2816
ALPHA = (2 * DEPTH) ** 0.25
LN_EPS = 1e-5
RMS_EPS = 1e-6
ATT_SCALE = (QK_NOPE + QK_ROPE) ** -0.5
D_IN = 4000
D_IN_PAD = 4096
Q_PAD = 256
KV_PAD = 128

ADAM_LR = 0.001
ADAM_B1 = 0.9
ADAM_B2 = 0.999
ADAM_EPS = 1e-08
ADAM_WD = 0.01
ADAM_STEP = 10

TM = 768
TMH = 384
LANES = 128
COMM_COLS = 512
COMM_ROW_BLOCK = 1472
VMEM_LIMIT_BYTES = 50 * 1024 * 1024

NT = (((1,), (1,)), ((), ()))
TN = (((0,), (0,)), ((), ()))


def _pcall(body, **kw):
    return pl.pallas_call(body, **kw)


def _params(n_axes):
    return pltpu.CompilerParams(dimension_semantics=("arbitrary",) * n_axes, vmem_limit_bytes=VMEM_LIMIT_BYTES)


def _sds(shape, dtype):
    return jax.ShapeDtypeStruct(shape, dtype)


def mm_rows(pairs, n_out, *, name, tn=None, addend=None, add_scale=1.0, out_dtype=F32):
    tp = pairs[0][0].shape[0]
    tn = tn or n_out
    in_specs, args = [], []
    for a, b, nt, kb in pairs:
        k = a.shape[1]
        in_specs.append(pl.BlockSpec((TM, k), lambda i, j: (i, 0)))
        if nt:
            in_specs.append(pl.BlockSpec((tn, k), functools.partial(lambda i, j, kb: (j, kb), kb=kb)))
        else:
            in_specs.append(pl.BlockSpec((k, tn), lambda i, j: (0, j)))
        args += [a, b]
    if addend is not None:
        in_specs.append(pl.BlockSpec((TM, tn), lambda i, j: (i, j)))
        args.append(addend)
    n_pairs = len(pairs)
    nts = [p[2] for p in pairs]

    def body(*refs):
        o_ref = refs[-1]
        acc = None
        for p in range(n_pairs):
            a = refs[2 * p][...].astype(BF)
            b = refs[2 * p + 1][...]
            d = lax.dot_general(a, b, NT if nts[p] else (((1,), (0,)), ((), ())), preferred_element_type=F32)
            acc = d if acc is None else acc + d
        if addend is not None:
            acc = acc + add_scale * refs[2 * n_pairs][...]
        o_ref[...] = acc.astype(o_ref.dtype)

    return _pcall(
        body, name=name, grid=(tp // TM, n_out // tn), in_specs=in_specs,
        out_specs=pl.BlockSpec((TM, tn), lambda i, j: (i, j)), out_shape=_sds((tp, n_out), out_dtype),
        compiler_params=_params(2),
    )(*args)


def tn_mm(a, b, *, tm, name, out_dtype=BF):
    tp, m = a.shape
    n = b.shape[1]
    nk = tp // TM

    def body(a_ref, b_ref, o_ref, acc_ref):
        k = pl.program_id(1)

        @pl.when(k == 0)
        def _():
            acc_ref[...] = jnp.zeros_like(acc_ref)

        acc_ref[...] += lax.dot_general(a_ref[...].astype(BF), b_ref[...].astype(BF), TN, preferred_element_type=F32)

        @pl.when(k == nk - 1)
        def _():
            o_ref[...] = acc_ref[...].astype(o_ref.dtype)

    return _pcall(
        body, name=name, grid=(m // tm, nk),
        in_specs=[pl.BlockSpec((TM, tm), lambda i, k: (k, i)), pl.BlockSpec((TM, n), lambda i, k: (k, 0))],
        out_specs=pl.BlockSpec((tm, n), lambda i, k: (i, 0)), out_shape=_sds((m, n), out_dtype),
        scratch_shapes=[pltpu.VMEM((tm, n), F32)], compiler_params=_params(2),
    )(a, b)


def _ln_store(z, g_ref, b_ref, xh_ref, rs_ref, hb_ref):
    mu = jnp.mean(z, axis=-1, keepdims=True)
    zc = z - mu
    var = jnp.mean(zc * zc, axis=-1, keepdims=True)
    rstd = lax.rsqrt(var + LN_EPS)
    xh = zc * rstd
    xh_ref[...] = xh
    rs_ref[...] = rstd
    hb_ref[...] = (xh * g_ref[...] + b_ref[...]).astype(BF)


def _ln_out(tp, tm=TM):
    specs = [pl.BlockSpec((tm, D_MODEL), lambda i: (i, 0)), pl.BlockSpec((tm, 1), lambda i: (i, 0)),
             pl.BlockSpec((tm, D_MODEL), lambda i: (i, 0))]
    shapes = [_sds((tp, D_MODEL), F32), _sds((tp, 1), F32), _sds((tp, D_MODEL), BF)]
    return specs, shapes


def _row_vec(n):
    return pl.BlockSpec((1, n), lambda i: (0, 0))


def ffn_up(hb, wup, *, name):
    tp = hb.shape[0]
    tn = D_FF // 2
    nj = D_FF // tn

    def body(h_ref, wg_ref, wu_ref, g_ref, u_ref, a_ref):
        h = h_ref[...]
        g = jnp.dot(h, wg_ref[...], preferred_element_type=F32)
        u = jnp.dot(h, wu_ref[...], preferred_element_type=F32)
        g_ref[...] = g.astype(BF)
        u_ref[...] = u.astype(BF)
        a_ref[...] = (g * jax.nn.sigmoid(g) * u).astype(BF)

    blk = pl.BlockSpec((TM, tn), lambda i, j: (i, j))
    return _pcall(
        body, name=name, grid=(tp // TM, nj),
        in_specs=[pl.BlockSpec((TM, D_MODEL), lambda i, j: (i, 0)), pl.BlockSpec((D_MODEL, tn), lambda i, j: (0, j)),
                  pl.BlockSpec((D_MODEL, tn), lambda i, j: (0, j + nj))],
        out_specs=[blk, blk, blk], out_shape=[_sds((tp, D_FF), BF)] * 3, compiler_params=_params(2),
    )(hb, wup, wup)


def down_ln(a, wd, xprev, gp, bp, g, b, *, name):
    tp = a.shape[0]

    def body(a_ref, wd_ref, xp_ref, gp_ref, bp_ref, g_ref, b_ref, xh_ref, rs_ref, hb_ref):
        f = jnp.dot(a_ref[...], wd_ref[...], preferred_element_type=F32)
        hprev = xp_ref[...] * gp_ref[...] + bp_ref[...]
        _ln_store(ALPHA * hprev + 0.5 * f, g_ref, b_ref, xh_ref, rs_ref, hb_ref)

    out_specs, out_shape = _ln_out(tp)
    return _pcall(
        body, name=name, grid=(tp // TM,),
        in_specs=[pl.BlockSpec((TM, D_FF), lambda i: (i, 0)), pl.BlockSpec((D_FF, D_MODEL), lambda i: (0, 0)),
                  pl.BlockSpec((TM, D_MODEL), lambda i: (i, 0))] + [_row_vec(D_MODEL)] * 4,
        out_specs=out_specs, out_shape=out_shape, compiler_params=_params(1),
    )(a, wd, xprev, gp, bp, g, b)


def _rope(x, c, s1, s2, reps):
    n = x.shape[1]
    if reps > 1:
        c, s1, s2 = (jnp.tile(t, (1, reps)) for t in (c, s1, s2))
    return x * c + pltpu.roll(x, 16, 1) * s1 + pltpu.roll(x, n - 16, 1) * s2


def _rms(x, g):
    r = lax.rsqrt(jnp.mean(x * x, axis=-1, keepdims=True) + RMS_EPS)
    return x * r * g, r


def mla_prep(p, gq, gkv, wuq_p, wukv, tabs, *, name):
    tp = p.shape[0]
    nh = MLA_HEADS

    def body(cq_ref, ckv_ref, kr_ref, gq_ref, gkv_ref, wuq_ref, wukv_ref, cq_t, s1q_t, s2q_t, ck_t, s1k_t, s2k_t,
             cqn_ref, ckvn_ref, q2_ref, kv_ref, krr_ref):
        cqn, _ = _rms(cq_ref[...], gq_ref[...])
        ckvn, _ = _rms(ckv_ref[...], gkv_ref[...])
        cqn = cqn.astype(BF)
        ckvn = ckvn.astype(BF)
        cqn_ref[...] = cqn
        ckvn_ref[...] = ckvn
        q = jnp.dot(cqn, wuq_ref[...], preferred_element_type=F32)
        q2_ref[...] = _rope(q, cq_t[...], s1q_t[...], s2q_t[...], nh).astype(BF)
        kv_ref[...] = jnp.dot(ckvn, wukv_ref[...], preferred_element_type=F32).astype(BF)
        krr_ref[...] = _rope(kr_ref[...], ck_t[...], s1k_t[...], s2k_t[...], 1).astype(BF)

    def rows(n, col=0):
        return pl.BlockSpec((TMH, n), functools.partial(lambda i, col: (i, col), col=col))

    return _pcall(
        body, name=name, grid=(tp // TMH,),
        in_specs=[rows(Q_LORA, 1536 // Q_LORA), rows(KV_LORA, 1792 // KV_LORA), rows(LANES, 1920 // LANES),
                  _row_vec(Q_LORA), _row_vec(KV_LORA),
                  pl.BlockSpec((Q_LORA, nh * Q_PAD), lambda i: (0, 0)), pl.BlockSpec((KV_LORA, nh * KV_PAD), lambda i: (0, 0)),
                  rows(Q_PAD), rows(Q_PAD), rows(Q_PAD), rows(LANES), rows(LANES), rows(LANES)],
        out_specs=[rows(Q_LORA), rows(KV_LORA), rows(nh * Q_PAD), rows(nh * KV_PAD), rows(LANES)],
        out_shape=[_sds((tp, Q_LORA), BF), _sds((tp, KV_LORA), BF), _sds((tp, nh * Q_PAD), BF),
                   _sds((tp, nh * KV_PAD), BF), _sds((tp, LANES), BF)],
        compiler_params=_params(1),
    )(p, p, p, gq, gkv, wuq_p, wukv, *tabs)


def _scores(q, kvb, krb, diagonal):
    k2 = jnp.concatenate([kvb, krb], axis=1)
    s = lax.dot_general(q, k2, NT, preferred_element_type=F32)
    if diagonal:
        qpos = lax.broadcasted_iota(jnp.int32, (TM, TM), 0)
        kpos = lax.broadcasted_iota(jnp.int32, (TM, TM), 1)
        s = jnp.where(kpos <= qpos, s, NEG_INF)
    return s, k2


def _causal_steps(nb, key_major):
    if key_major:
        pairs = [(qi, ki) for ki in range(nb) for qi in range(ki, nb)]
    else:
        pairs = [(qi, ki) for qi in range(nb) for ki in range(qi + 1)]
    return jnp.array([p[0] for p in pairs], jnp.int32), jnp.array([p[1] for p in pairs], jnp.int32)


def attn_fwd(q2, kv, kr, *, name):
    tp = q2.shape[0]
    nh = MLA_HEADS
    nb = tp // TM
    rep = TM // LANES
    q_tab, k_tab = _causal_steps(nb, key_major=False)

    def body(qt_ref, kt_ref, q_ref, kv_ref, kr_ref, o_ref, lse_ref, m_ref, l_ref, acc_ref):
        s_idx = pl.program_id(1)
        qi = qt_ref[s_idx]
        ki = kt_ref[s_idx]

        @pl.when(ki == 0)
        def _():
            m_ref[...] = jnp.full_like(m_ref, NEG_INF)
            l_ref[...] = jnp.zeros_like(l_ref)
            acc_ref[...] = jnp.zeros_like(acc_ref)

        def step(diagonal):
            kvb = kv_ref[...]
            s, _ = _scores(q_ref[...], kvb, kr_ref[...], diagonal)
            m_prev = m_ref[...]
            m_new = jnp.maximum(m_prev, jnp.max(s, axis=1, keepdims=True))
            alpha = jnp.exp(m_prev - m_new)
            p = jnp.exp(s - jnp.tile(m_new, (1, rep)))
            l_ref[...] = alpha * l_ref[...] + jnp.sum(p, axis=1, keepdims=True)
            acc_ref[...] = alpha * acc_ref[...] + jnp.dot(p.astype(BF), kvb, preferred_element_type=F32)
            m_ref[...] = m_new

        @pl.when(ki < qi)
        def _():
            step(False)

        @pl.when(ki == qi)
        def _():
            step(True)
            o_ref[...] = (acc_ref[...] / l_ref[...]).astype(BF)
            lse_ref[...] = m_ref[...] + jnp.log(l_ref[...])

    grid_spec = pltpu.PrefetchScalarGridSpec(
        num_scalar_prefetch=2, grid=(nh, q_tab.shape[0]),
        in_specs=[pl.BlockSpec((TM, Q_PAD), lambda h, s, qt, kt: (qt[s], h)),
                  pl.BlockSpec((TM, KV_PAD), lambda h, s, qt, kt: (kt[s], h)),
                  pl.BlockSpec((TM, LANES), lambda h, s, qt, kt: (kt[s], 0))],
        out_specs=[pl.BlockSpec((TM, KV_PAD), lambda h, s, qt, kt: (qt[s], h)),
                   pl.BlockSpec((TM, LANES), lambda h, s, qt, kt: (qt[s], h))],
        scratch_shapes=[pltpu.VMEM((TM, LANES), F32)] * 3,
    )
    return _pcall(
        body, name=name, grid_spec=grid_spec, out_shape=[_sds((tp, nh * KV_PAD), BF), _sds((tp, nh * LANES), F32)],
        compiler_params=_params(2),
    )(q_tab, k_tab, q2, kv, kr)


def conv_fwd(p, w, *, name):
    tp = p.shape[0]

    def body(b_ref, c_ref, h_ref, w_ref, y_ref, cv_ref, ebuf):
        i = pl.program_id(0)

        @pl.when(i == 0)
        def _():
            ebuf[0:8, :] = jnp.zeros((8, D_CONV), F32)

        e = c_ref[...] * h_ref[...]
        ebuf[8:8 + TM, :] = e
        w_all = w_ref[...]
        conv = w_all[0:1] * ebuf[pl.ds(6, TM), :] + w_all[1:2] * ebuf[pl.ds(7, TM), :] + w_all[2:3] * e
        cv_ref[...] = conv.astype(BF)
        y_ref[...] = (b_ref[...] * conv).astype(BF)
        ebuf[0:8, :] = ebuf[TM:TM + 8, :]

    def col(j):
        return pl.BlockSpec((TM, D_CONV), functools.partial(lambda i, j: (i, j), j=j))

    return _pcall(
        body, name=name, grid=(tp // TM,),
        in_specs=[col(0), col(1), col(2), pl.BlockSpec((3, D_CONV), lambda i: (0, 0))],
        out_specs=[col(0), col(0)], out_shape=[_sds((tp, D_CONV), BF)] * 2,
        scratch_shapes=[pltpu.VMEM((TM + 8, D_CONV), F32)], compiler_params=_params(1),
    )(p, p, p, w)


def merge_out_ln(ycv, o2, p, bg, wbc, wbm_p, wo, xprev, gp, bp, g, b, *, name):
    tp = ycv.shape[0]

    def body(y_ref, o_ref, gc_ref, gm_ref, bg_ref, wbc_ref, wbm_ref, wo_ref, xp_ref, gp_ref, bp_ref, g_ref, b_ref,
             bc_ref, bm_ref, mg_ref, xh_ref, rs_ref, hb_ref):
        bc = jnp.dot(y_ref[...], wbc_ref[...], preferred_element_type=F32)
        bm = jnp.dot(o_ref[...], wbm_ref[...], preferred_element_type=F32)
        bgv = bg_ref[...]
        mg = jax.nn.sigmoid(gc_ref[...] + bgv[0:1]) * bc + jax.nn.sigmoid(gm_ref[...] + bgv[1:2]) * bm
        mgb = mg.astype(BF)
        bc_ref[...] = bc.astype(BF)
        bm_ref[...] = bm.astype(BF)
        mg_ref[...] = mgb
        mix = jnp.dot(mgb, wo_ref[...], preferred_element_type=F32)
        hprev = xp_ref[...] * gp_ref[...] + bp_ref[...]
        _ln_store(ALPHA * hprev + mix, g_ref, b_ref, xh_ref, rs_ref, hb_ref)

    def rows(n, col=0):
        return pl.BlockSpec((TMH, n), functools.partial(lambda i, col: (i, col), col=col))

    def whole(r, c):
        return pl.BlockSpec((r, c), lambda i: (0, 0))

    ln_specs, ln_shapes = _ln_out(tp, TMH)
    return _pcall(
        body, name=name, grid=(tp // TMH,),
        in_specs=[rows(D_CONV), rows(MLA_HEADS * KV_PAD), rows(D_MODEL, 2), rows(D_MODEL, 3), whole(2, D_MODEL),
                  whole(D_CONV, D_MODEL), whole(MLA_HEADS * KV_PAD, D_MODEL), whole(D_MODEL, D_MODEL), rows(D_MODEL)]
        + [_row_vec(D_MODEL)] * 4,
        out_specs=[rows(D_MODEL)] * 3 + ln_specs, out_shape=[_sds((tp, D_MODEL), BF)] * 3 + ln_shapes,
        compiler_params=_params(1),
    )(ycv, o2, p, p, bg, wbc, wbm_p, wo, xprev, gp, bp, g, b)


def loss_grad(xh, g, b, target_p, n_real, *, name):
    tp = xh.shape[0]

    def body(x_ref, g_ref, b_ref, t_ref, dy_ref, loss_ref):
        i = pl.program_id(0)

        @pl.when(i == 0)
        def _():
            loss_ref[...] = jnp.zeros_like(loss_ref)

        row = i * TM + lax.broadcasted_iota(jnp.int32, (TM, 1), 0)
        real = (row >= N_META) & (row < N_META + n_real)
        diff = jnp.where(real, x_ref[...] * g_ref[...] + b_ref[...] - t_ref[...], 0.0)
        dy_ref[...] = diff * (1.0 / D_MODEL)
        loss_ref[...] += 0.5 / D_MODEL * jnp.sum(diff * diff)

    return _pcall(
        body, name=name, grid=(tp // TM,),
        in_specs=[pl.BlockSpec((TM, D_MODEL), lambda i: (i, 0)), _row_vec(D_MODEL), _row_vec(D_MODEL),
                  pl.BlockSpec((TM, D_MODEL), lambda i: (i, 0))],
        out_specs=[pl.BlockSpec((TM, D_MODEL), lambda i: (i, 0)), pl.BlockSpec((8, LANES), lambda i: (0, 0))],
        out_shape=[_sds((tp, D_MODEL), F32), _sds((8, LANES), F32)], compiler_params=_params(1),
    )(xh, g, b, target_p)


def ln_bwd(dh, xh, rstd, g, *, branch_scale, name):
    tp = dh.shape[0]

    def body(dh_ref, xh_ref, rs_ref, g_ref, dz_ref, dzb_ref, dg_ref, db_ref):
        i = pl.program_id(0)

        @pl.when(i == 0)
        def _():
            dg_ref[...] = jnp.zeros_like(dg_ref)
            db_ref[...] = jnp.zeros_like(db_ref)

        dy = dh_ref[...]
        xhat = xh_ref[...]
        dg_ref[...] += jnp.sum(dy * xhat, axis=0, keepdims=True)
        db_ref[...] += jnp.sum(dy, axis=0, keepdims=True)
        dxh = dy * g_ref[...]
        m1 = jnp.mean(dxh, axis=-1, keepdims=True)
        m2 = jnp.mean(dxh * xhat, axis=-1, keepdims=True)
        dz = rs_ref[...] * (dxh - m1 - xhat * m2)
        dz_ref[...] = dz
        dzb_ref[...] = (branch_scale * dz).astype(BF)

    rows = pl.BlockSpec((TM, D_MODEL), lambda i: (i, 0))
    return _pcall(
        body, name=name, grid=(tp // TM,),
        in_specs=[rows, rows, pl.BlockSpec((TM, 1), lambda i: (i, 0)), _row_vec(D_MODEL)],
        out_specs=[rows, rows, _row_vec(D_MODEL), _row_vec(D_MODEL)],
        out_shape=[_sds((tp, D_MODEL), F32), _sds((tp, D_MODEL), BF), _sds((1, D_MODEL), F32), _sds((1, D_MODEL), F32)],
        compiler_params=_params(1),
    )(dh, xh, rstd, g)


def ffn_down_bwd(dzb, wd, gate, up, *, name):
    tp = dzb.shape[0]
    tn = D_FF // 2

    def body(dz_ref, wd_ref, g_ref, u_ref, dg_ref, du_ref):
        da = lax.dot_general(dz_ref[...], wd_ref[...], NT, preferred_element_type=F32)
        g = g_ref[...].astype(F32)
        u = u_ref[...].astype(F32)
        sg = jax.nn.sigmoid(g)
        dg_ref[...] = (da * u * sg * (1.0 + g * (1.0 - sg))).astype(BF)
        du_ref[...] = (da * g * sg).astype(BF)

    blk = pl.BlockSpec((TM, tn), lambda i, j: (i, j))
    return _pcall(
        body, name=name, grid=(tp // TM, D_FF // tn),
        in_specs=[pl.BlockSpec((TM, D_MODEL), lambda i, j: (i, 0)), pl.BlockSpec((tn, D_MODEL), lambda i, j: (j, 0)), blk, blk],
        out_specs=[blk, blk], out_shape=[_sds((tp, D_FF), BF)] * 2, compiler_params=_params(2),
    )(dzb, wd, gate, up)


def merge_bwd(dzb, wo, bc, bm, p, bg, wbc, wbm_p, o2, *, name):
    tp = dzb.shape[0]
    nh = MLA_HEADS

    def body(dz_ref, wo_ref, bc_ref, bm_ref, gc_ref, gm_ref, bg_ref, wbc_ref, wbm_ref, o_ref,
             dbc_ref, dbm_ref, dgg_ref, dy_ref, do_ref, dl_ref, dbg_ref):
        i = pl.program_id(0)

        @pl.when(i == 0)
        def _():
            dbg_ref[...] = jnp.zeros_like(dbg_ref)

        dmg = lax.dot_general(dz_ref[...], wo_ref[...], NT, preferred_element_type=F32)
        bgv = bg_ref[...]
        sc = jax.nn.sigmoid(gc_ref[...] + bgv[0:1])
        sm = jax.nn.sigmoid(gm_ref[...] + bgv[1:2])
        dbc = (dmg * sc).astype(BF)
        dbm = (dmg * sm).astype(BF)
        dgc = dmg * bc_ref[...].astype(F32) * sc * (1.0 - sc)
        dgm = dmg * bm_ref[...].astype(F32) * sm * (1.0 - sm)
        dbc_ref[...] = dbc
        dbm_ref[...] = dbm
        dgg_ref[...] = jnp.concatenate([dgc, dgm], axis=1).astype(BF)
        dbg_ref[...] += jnp.concatenate([jnp.sum(dgc, axis=0, keepdims=True), jnp.sum(dgm, axis=0, keepdims=True)], axis=0)
        dy_ref[...] = lax.dot_general(dbc, wbc_ref[...], NT, preferred_element_type=F32)
        do = lax.dot_general(dbm, wbm_ref[...], NT, preferred_element_type=F32)
        do_ref[...] = do.astype(BF)
        prod = do * o_ref[...].astype(F32)
        parts = []
        for h in range(nh):
            d = jnp.sum(prod[:, h * KV_PAD:(h + 1) * KV_PAD], axis=1, keepdims=True)
            parts.append(jnp.broadcast_to(d, (TMH, LANES)))
        dl_ref[...] = jnp.concatenate(parts, axis=1)

    def rows(n, col=0):
        return pl.BlockSpec((TMH, n), functools.partial(lambda i, col: (i, col), col=col))

    def whole(r, c):
        return pl.BlockSpec((r, c), lambda i: (0, 0))

    return _pcall(
        body, name=name, grid=(tp // TMH,),
        in_specs=[rows(D_MODEL), whole(D_MODEL, D_MODEL), rows(D_MODEL), rows(D_MODEL), rows(D_MODEL, 2), rows(D_MODEL, 3),
                  whole(2, D_MODEL), whole(D_CONV, D_MODEL), whole(nh * KV_PAD, D_MODEL), rows(nh * KV_PAD)],
        out_specs=[rows(D_MODEL), rows(D_MODEL), rows(2 * D_MODEL), rows(D_CONV), rows(nh * KV_PAD), rows(nh * LANES),
                   whole(2, D_MODEL)],
        out_shape=[_sds((tp, D_MODEL), BF), _sds((tp, D_MODEL), BF), _sds((tp, 2 * D_MODEL), BF), _sds((tp, D_CONV), F32),
                   _sds((tp, nh * KV_PAD), BF), _sds((tp, nh * LANES), F32), _sds((2, D_MODEL), F32)],
        compiler_params=_params(1),
    )(dzb, wo, bc, bm, p, p, bg, wbc, wbm_p, o2)


def attn_bwd(q2, kv, kr, do2, lse, dl, *, name):
    tp = q2.shape[0]
    nh = MLA_HEADS
    nb = tp // TM
    rep = TM // LANES
    q_tab, k_tab = _causal_steps(nb, key_major=True)

    def body(qt_ref, kt_ref, q_ref, kv_ref, kr_ref, do_ref, lse_ref, dl_ref, dkv_ref, dkr_ref, dq_ref, dkv_acc, dkr_acc):
        s_idx = pl.program_id(1)
        qi = qt_ref[s_idx]
        ki = kt_ref[s_idx]

        @pl.when(s_idx == 0)
        def _():
            dq_ref[...] = jnp.zeros_like(dq_ref)

        @pl.when(qi == ki)
        def _():
            dkv_acc[...] = jnp.zeros_like(dkv_acc)
            dkr_acc[...] = jnp.zeros_like(dkr_acc)

        def step(diagonal):
            q = q_ref[...]
            do = do_ref[...]
            kvb = kv_ref[...]
            s, k2 = _scores(q, kvb, kr_ref[...], diagonal)
            p = jnp.exp(s - jnp.tile(lse_ref[...], (1, rep)))
            dp = lax.dot_general(do, kvb, NT, preferred_element_type=F32)
            dsb = (p * (dp - jnp.tile(dl_ref[...], (1, rep)))).astype(BF)
            dk2 = lax.dot_general(dsb, q, TN, preferred_element_type=F32)
            dkv_acc[...] += lax.dot_general(p.astype(BF), do, TN, preferred_element_type=F32) + dk2[:, :KV_PAD]
            dkr_acc[...] += dk2[:, KV_PAD:KV_PAD + LANES]
            rows = pl.ds(pl.multiple_of(qi * TM, TM), TM)
            dq_ref[rows, :] += jnp.dot(dsb, k2, preferred_element_type=F32)

        @pl.when(qi == ki)
        def _():
            step(True)

        @pl.when(qi > ki)
        def _():
            step(False)

        @pl.when(qi == nb - 1)
        def _():
            dkv_ref[...] = dkv_acc[...].astype(BF)
            dkr_ref[...] = dkr_acc[...]

    def qrow(n):
        return pl.BlockSpec((TM, n), lambda h, s, qt, kt: (qt[s], h))

    def krow(n):
        return pl.BlockSpec((TM, n), lambda h, s, qt, kt: (kt[s], h))

    grid_spec = pltpu.PrefetchScalarGridSpec(
        num_scalar_prefetch=2, grid=(nh, q_tab.shape[0]),
        in_specs=[qrow(Q_PAD), krow(KV_PAD), pl.BlockSpec((TM, LANES), lambda h, s, qt, kt: (kt[s], 0)),
                  qrow(KV_PAD), qrow(LANES), qrow(LANES)],
        out_specs=[krow(KV_PAD), krow(LANES), pl.BlockSpec((tp, Q_PAD), lambda h, s, qt, kt: (0, h))],
        scratch_shapes=[pltpu.VMEM((TM, KV_PAD), F32), pltpu.VMEM((TM, LANES), F32)],
    )
    return _pcall(
        body, name=name, grid_spec=grid_spec,
        out_shape=[_sds((tp, nh * KV_PAD), BF), _sds((tp, nh * LANES), F32), _sds((tp, nh * Q_PAD), F32)],
        compiler_params=_params(2),
    )(q_tab, k_tab, q2, kv, kr, do2, lse, dl)


def _rms_bwd(x, g, dy):
    r = lax.rsqrt(jnp.mean(x * x, axis=-1, keepdims=True) + RMS_EPS)
    gy = dy * g
    dx = r * gy - x * (r * r * r) * jnp.mean(x * gy, axis=-1, keepdims=True)
    return dx, jnp.sum(dy * x * r, axis=0, keepdims=True)


def mla_prep_bwd(dq2, dkv, dkr, p, gq, gkv, wuq_p, wukv, tabs_bwd, *, name):
    tp = dq2.shape[0]
    nh = MLA_HEADS

    def body(dq_ref, dkv_ref, dkr_ref, cq_ref, ckv_ref, gq_ref, gkv_ref, wuq_ref, wukv_ref,
             cq_t, s1q_t, s2q_t, ck_t, s1k_t, s2k_t, dqb_ref, dsm_ref, dgq_ref, dgkv_ref):
        i = pl.program_id(0)

        @pl.when(i == 0)
        def _():
            dgq_ref[...] = jnp.zeros_like(dgq_ref)
            dgkv_ref[...] = jnp.zeros_like(dgkv_ref)

        dqb = _rope(dq_ref[...], cq_t[...], s1q_t[...], s2q_t[...], nh).astype(BF)
        dqb_ref[...] = dqb
        dcqn = lax.dot_general(dqb, wuq_ref[...], NT, preferred_element_type=F32)
        dcq, dgq = _rms_bwd(cq_ref[...], gq_ref[...], dcqn)
        dckvn = lax.dot_general(dkv_ref[...], wukv_ref[...], NT, preferred_element_type=F32)
        dckv, dgkv = _rms_bwd(ckv_ref[...], gkv_ref[...], dckvn)
        dkr_heads = dkr_ref[...]
        dkr_sum = dkr_heads[:, :LANES]
        for h in range(1, nh):
            dkr_sum = dkr_sum + dkr_heads[:, h * LANES:(h + 1) * LANES]
        dkr = _rope(dkr_sum, ck_t[...], s1k_t[...], s2k_t[...], 1)
        dsm_ref[...] = jnp.concatenate([dcq, dckv, dkr], axis=1)
        dgq_ref[...] += dgq
        dgkv_ref[...] += dgkv

    def rows(n, col=0):
        return pl.BlockSpec((TMH, n), functools.partial(lambda i, col: (i, col), col=col))

    return _pcall(
        body, name=name, grid=(tp // TMH,),
        in_specs=[rows(nh * Q_PAD), rows(nh * KV_PAD), rows(nh * LANES), rows(Q_LORA, 1536 // Q_LORA), rows(KV_LORA, 1792 // KV_LORA),
                  _row_vec(Q_LORA), _row_vec(KV_LORA),
                  pl.BlockSpec((Q_LORA, nh * Q_PAD), lambda i: (0, 0)), pl.BlockSpec((KV_LORA, nh * KV_PAD), lambda i: (0, 0)),
                  rows(Q_PAD), rows(Q_PAD), rows(Q_PAD), rows(LANES), rows(LANES), rows(LANES)],
        out_specs=[rows(nh * Q_PAD), rows(Q_LORA + KV_LORA + LANES), _row_vec(Q_LORA), _row_vec(KV_LORA)],
        out_shape=[_sds((tp, nh * Q_PAD), BF), _sds((tp, Q_LORA + KV_LORA + LANES), F32), _sds((1, Q_LORA), F32),
                   _sds((1, KV_LORA), F32)],
        compiler_params=_params(1),
    )(dq2, dkv, dkr, p, p, gq, gkv, wuq_p, wukv, *tabs_bwd)


def conv_bwd(dy, p, conv, w, *, name):
    tp = dy.shape[0]
    nb = tp // TM

    def body(dy_ref, b_ref, c_ref, h_ref, cv_ref, w_ref, dp_ref, dw0_ref, dw1_ref, dw2_ref, dbuf):
        i = pl.program_id(0)

        @pl.when(i == 0)
        def _():
            dbuf[TM:TM + 8, :] = jnp.zeros((8, D_CONV), F32)
            dw0_ref[...] = jnp.zeros_like(dw0_ref)
            dw1_ref[...] = jnp.zeros_like(dw1_ref)
            dw2_ref[...] = jnp.zeros_like(dw2_ref)

        dyv = dy_ref[...]
        c = c_ref[...]
        hh = h_ref[...]
        dconv = dyv * b_ref[...]
        dbuf[0:TM, :] = dconv
        d1 = dbuf[pl.ds(1, TM), :]
        d2 = dbuf[pl.ds(2, TM), :]
        w_all = w_ref[...]
        de = w_all[2:3] * dconv + w_all[1:2] * d1 + w_all[0:1] * d2
        e = c * hh
        dp_ref[...] = jnp.concatenate([dyv * cv_ref[...].astype(F32), de * hh, de * c], axis=1).astype(BF)
        dw0_ref[...] += jnp.sum(d2 * e, axis=0, keepdims=True)
        dw1_ref[...] += jnp.sum(d1 * e, axis=0, keepdims=True)
        dw2_ref[...] += jnp.sum(dconv * e, axis=0, keepdims=True)
        dbuf[TM:TM + 8, :] = dbuf[0:8, :]

    def col(j):
        return pl.BlockSpec((TM, D_CONV), functools.partial(lambda i, j: (nb - 1 - i, j), j=j))

    return _pcall(
        body, name=name, grid=(nb,),
        in_specs=[col(0), col(0), col(1), col(2), col(0), pl.BlockSpec((3, D_CONV), lambda i: (0, 0))],
        out_specs=[pl.BlockSpec((TM, 3 * D_CONV), lambda i: (nb - 1 - i, 0))] + [_row_vec(D_CONV)] * 3,
        out_shape=[_sds((tp, 3 * D_CONV), BF)] + [_sds((1, D_CONV), F32)] * 3,
        scratch_shapes=[pltpu.VMEM((TM + 8, D_CONV), F32)], compiler_params=_params(1),
    )(dy, p, p, p, conv, w)


def adamw(w, g, m, v, *, name):
    r, c = w.shape
    tr = r
    for cand in (256, 128, 64, 32, 16, 8):
        if r % cand == 0 and r > cand:
            tr = cand
            break

    def body(w_ref, g_ref, m_ref, v_ref, d_ref, nm_ref, nv_ref):
        gv = g_ref[...]
        nm = ADAM_B1 * m_ref[...] + (1.0 - ADAM_B1) * gv
        nv = ADAM_B2 * v_ref[...] + (1.0 - ADAM_B2) * (gv * gv)
        m_hat = nm / (1.0 - ADAM_B1 ** ADAM_STEP)
        v_hat = nv / (1.0 - ADAM_B2 ** ADAM_STEP)
        d_ref[...] = -ADAM_LR * (m_hat / (jnp.sqrt(v_hat) + ADAM_EPS) + ADAM_WD * w_ref[...])
        nm_ref[...] = nm
        nv_ref[...] = nv

    blk = pl.BlockSpec((tr, c), lambda i: (i, 0))
    return _pcall(
        body, name=name, grid=(r // tr,), in_specs=[blk] * 4, out_specs=[blk] * 3,
        out_shape=[_sds((r, c), F32)] * 3, compiler_params=_params(1),
    )(w, g, m, v)


HBM_SPEC = pl.BlockSpec(memory_space=pltpu.HBM)


def _place():
    return lax.axis_index("x"), lax.axis_index("y"), lax.axis_index("c")


def _other_chips(x, y):
    return [(1 - x, y), (x, 1 - y), (1 - x, 1 - y)]


def gather_chips(shard, *, name):
    r, c_cols = shard.shape
    rh = r // 2

    def body(x_ref, prefilled_ref, o_ref, send_sems, recv_sems):
        x, y, c = _place()
        me = 2 * x + y
        chips = _other_chips(x, y)
        half = pl.ds(c * rh, rh)
        other = pl.ds((1 - c) * rh, rh)

        def copy(k, src, dst, to):
            return pltpu.make_async_remote_copy(src_ref=src, dst_ref=dst, send_sem=send_sems.at[k], recv_sem=recv_sems.at[k],
                                                device_id=to, device_id_type=MESH)

        sends = [copy(k, x_ref.at[half], o_ref.at[me, half], (px, py, c)) for k, (px, py) in enumerate(chips)]
        for cp in sends:
            cp.start()
        passed = []
        for k, (px, py) in enumerate(chips):
            j = 2 * px + py
            copy(k, x_ref.at[half], o_ref.at[j, half], (px, py, c)).wait_recv()
            fwd = copy(3 + k, o_ref.at[j, half], o_ref.at[j, half], (x, y, 1 - c))
            fwd.start()
            passed.append(fwd)
        for k, (px, py) in enumerate(chips):
            j = 2 * px + py
            copy(3 + k, o_ref.at[j, other], o_ref.at[j, other], (x, y, 1 - c)).wait_recv()
        for cp in sends + passed:
            cp.wait_send()

    return _pcall(
        body, name=name, in_specs=[HBM_SPEC, HBM_SPEC], out_specs=HBM_SPEC, out_shape=_sds((4, r, c_cols), shard.dtype),
        input_output_aliases={1: 0}, scratch_shapes=[pltpu.SemaphoreType.DMA((6,)), pltpu.SemaphoreType.DMA((6,))],
    )(shard, jnp.broadcast_to(shard[None], (4, r, c_cols)))


def pair_exchange(g, *, name):
    n, r, c_cols = g.shape
    rh = r // 2

    def body(g_ref, o_ref, send_sem, recv_sem):
        x, y, c = _place()
        cp = pltpu.make_async_remote_copy(src_ref=g_ref.at[:, pl.ds((1 - c) * rh, rh)], dst_ref=o_ref, send_sem=send_sem,
                                          recv_sem=recv_sem, device_id=(x, y, 1 - c), device_id_type=MESH)
        cp.start()
        cp.wait()

    return _pcall(
        body, name=name, in_specs=[HBM_SPEC], out_specs=HBM_SPEC, out_shape=_sds((n, rh, c_cols), g.dtype),
        scratch_shapes=[pltpu.SemaphoreType.DMA, pltpu.SemaphoreType.DMA],
    )(g)


def pair_add(g, s1, c_idx, *, name):
    n, r, c_cols = g.shape
    rh = r // 2
    rb = COMM_ROW_BLOCK
    nrb = rh // rb

    def body(c_ref, g_ref, s_ref, o_ref):
        o_ref[...] = (g_ref[...].astype(F32) + s_ref[...].astype(F32)).astype(o_ref.dtype)

    grid_spec = pltpu.PrefetchScalarGridSpec(
        num_scalar_prefetch=1, grid=(n, nrb),
        in_specs=[pl.BlockSpec((1, rb, c_cols), lambda j, i, c_ref: (j, c_ref[0] * nrb + i, 0)),
                  pl.BlockSpec((1, rb, c_cols), lambda j, i, c_ref: (j, i, 0))],
        out_specs=pl.BlockSpec((1, rb, c_cols), lambda j, i, c_ref: (j, i, 0)),
    )
    return _pcall(body, name=name, grid_spec=grid_spec, out_shape=_sds((n, rh, c_cols), g.dtype), compiler_params=_params(2))(
        c_idx, g, s1)


def chip_scatter(ps, *, name):
    n, rh, c_cols = ps.shape

    def body(p_ref, prefilled_ref, o_ref, send_sems, recv_sems):
        x, y, c = _place()
        me = 2 * x + y
        chips = _other_chips(x, y)

        def copy(k, j_src, j_dst, to):
            return pltpu.make_async_remote_copy(src_ref=p_ref.at[j_src], dst_ref=o_ref.at[j_dst], send_sem=send_sems.at[k],
                                                recv_sem=recv_sems.at[k], device_id=to, device_id_type=MESH)

        sends = [copy(k, 2 * px + py, me, (px, py, c)) for k, (px, py) in enumerate(chips)]
        for cp in sends:
            cp.start()
        for k, (px, py) in enumerate(chips):
            copy(k, me, 2 * px + py, (px, py, c)).wait_recv()
        for cp in sends:
            cp.wait_send()

    xi, yi, _ = _place()
    own = jnp.arange(n)[:, None, None] == 2 * xi + yi
    return _pcall(
        body, name=name, in_specs=[HBM_SPEC, HBM_SPEC], out_specs=HBM_SPEC, out_shape=_sds((n, rh, c_cols), ps.dtype),
        input_output_aliases={1: 0}, scratch_shapes=[pltpu.SemaphoreType.DMA((3,)), pltpu.SemaphoreType.DMA((3,))],
    )(ps, jnp.where(own, ps, jnp.zeros_like(ps)))


def sum_chunks(s2, *, name):
    n, rh, c_cols = s2.shape
    rb = COMM_ROW_BLOCK

    def body(s_ref, o_ref):
        acc = s_ref[0].astype(F32)
        for j in range(1, n):
            acc = acc + s_ref[j].astype(F32)
        o_ref[...] = acc

    return _pcall(
        body, name=name, grid=(rh // rb,), in_specs=[pl.BlockSpec((n, rb, c_cols), lambda i: (0, i, 0))],
        out_specs=pl.BlockSpec((rb, c_cols), lambda i: (i, 0)), out_shape=_sds((rh, c_cols), F32), compiler_params=_params(1),
    )(s2)


def pair_gather(rc, *, name):
    rh, c_cols = rc.shape

    def body(r_ref, prefilled_ref, o_ref, send_sem, recv_sem):
        x, y, c = _place()
        cp = pltpu.make_async_remote_copy(src_ref=r_ref, dst_ref=o_ref.at[pl.ds(c * rh, rh)], send_sem=send_sem, recv_sem=recv_sem,
                                          device_id=(x, y, 1 - c), device_id_type=MESH)
        cp.start()
        pltpu.make_async_remote_copy(src_ref=r_ref, dst_ref=o_ref.at[pl.ds((1 - c) * rh, rh)], send_sem=send_sem, recv_sem=recv_sem,
                                     device_id=(x, y, 1 - c), device_id_type=MESH).wait_recv()
        cp.wait_send()

    return _pcall(
        body, name=name, in_specs=[HBM_SPEC, HBM_SPEC], out_specs=HBM_SPEC, out_shape=_sds((2 * rh, c_cols), rc.dtype),
        input_output_aliases={1: 0}, scratch_shapes=[pltpu.SemaphoreType.DMA, pltpu.SemaphoreType.DMA],
    )(rc, jnp.concatenate([rc, rc], axis=0))


def allreduce_small(v, *, name):
    rows, lanes = v.shape

    def body(v_ref, o_ref, buf, send_sems, recv_sems):
        x, y, c = _place()
        me = 4 * x + 2 * y + c
        buf[me] = v_ref[...]

        def peer(k):
            dx, dy, dc = (k >> 2) & 1, (k >> 1) & 1, k & 1
            return (1 - x if dx else x, 1 - y if dy else y, 1 - c if dc else c)

        def copy(k, slot, to):
            return pltpu.make_async_remote_copy(src_ref=v_ref, dst_ref=buf.at[slot], send_sem=send_sems.at[k - 1],
                                                recv_sem=recv_sems.at[k - 1], device_id=to, device_id_type=MESH)

        sends = [copy(k, me, peer(k)) for k in range(1, 8)]
        for cp in sends:
            cp.start()
        for k in range(1, 8):
            px, py, pc = peer(k)
            copy(k, 4 * px + 2 * py + pc, peer(k)).wait_recv()
        for cp in sends:
            cp.wait_send()
        acc = buf[0]
        for d in range(1, 8):
            acc = acc + buf[d]
        o_ref[...] = acc

    vmem = pl.BlockSpec(memory_space=pltpu.VMEM)
    return _pcall(
        body, name=name, in_specs=[vmem], out_specs=vmem, out_shape=_sds((rows, lanes), F32),
        scratch_shapes=[pltpu.VMEM((8, rows, lanes), F32), pltpu.SemaphoreType.DMA((7,)), pltpu.SemaphoreType.DMA((7,))],
    )(v)


def _pad_rows(n, mult):
    return -(-n // mult) * mult


def _pack(arrs, dtype, rows_mult, cols):
    flat = jnp.concatenate([a.astype(dtype).reshape(-1) for a in arrs])
    rows = _pad_rows(-(-flat.shape[0] // cols), rows_mult)
    flat = jnp.pad(flat, (0, rows * cols - flat.shape[0]))
    return flat.reshape(rows, cols)


def _unpack(flat, shapes):
    out, off = [], 0
    for s in shapes:
        n = 1
        for d in s:
            n *= d
        out.append(flat[off:off + n].reshape(s))
        off += n
    return out


def _rope_tables(tp):
    inv_freq = 1.0 / (ROPE_BASE ** (jnp.arange(0, QK_ROPE, 2, dtype=F32) / QK_ROPE))
    ang = jnp.arange(tp, dtype=F32)[:, None] * inv_freq[None, :]
    cos, sin = jnp.cos(ang), jnp.sin(ang)
    one = lambda n: jnp.ones((tp, n), F32)
    zero = lambda n: jnp.zeros((tp, n), F32)
    cq = jnp.concatenate([one(128), cos, cos, one(96)], axis=1)
    s1q = jnp.concatenate([zero(144), sin, zero(96)], axis=1)
    s2q = jnp.concatenate([zero(128), -sin, zero(112)], axis=1)
    ck = jnp.concatenate([cos, cos, zero(96)], axis=1)
    s1k = jnp.concatenate([zero(16), sin, zero(96)], axis=1)
    s2k = jnp.concatenate([-sin, zero(112)], axis=1)
    fwd = (cq * ATT_SCALE, s1q * ATT_SCALE, s2q * ATT_SCALE, ck, s1k, s2k)
    bwd = (cq * ATT_SCALE, -s1q * ATT_SCALE, -s2q * ATT_SCALE, ck, -s1k, -s2k)
    return fwd, bwd


def _pad_w_in(w):
    return jnp.concatenate([w[:, :1952], jnp.zeros((w.shape[0], 96), w.dtype), w[:, 1952:]], axis=1)


def _pad_w_uq(w):
    w = w.reshape(Q_LORA, MLA_HEADS, QK_NOPE + QK_ROPE)
    z = lambda n: jnp.zeros((Q_LORA, MLA_HEADS, n), w.dtype)
    return jnp.concatenate([w[..., :QK_NOPE], z(64), w[..., QK_NOPE:], z(96)], axis=-1).reshape(Q_LORA, MLA_HEADS * Q_PAD)


def _unpad_w_uq(w):
    w = w.reshape(Q_LORA, MLA_HEADS, Q_PAD)
    return jnp.concatenate([w[..., :QK_NOPE], w[..., 128:128 + QK_ROPE]], axis=-1).reshape(Q_LORA, MLA_HEADS * (QK_NOPE + QK_ROPE))


def _pad_w_br_mla(w):
    w = w.reshape(MLA_HEADS, V_HEAD, D_MODEL)
    return jnp.concatenate([jnp.zeros_like(w), w], axis=1).reshape(MLA_HEADS * KV_PAD, D_MODEL)


def _unpad_w_br_mla(w):
    return w.reshape(MLA_HEADS, KV_PAD, D_MODEL)[:, V_HEAD:].reshape(MLA_HEADS * V_HEAD, D_MODEL)


def _layer_fwd(l, st, xprev, gp, bp, hb, w, tabs):
    ln_g, ln_b = w["ln_g"], w["ln_b"]
    lg = lambda k: ln_g[l, k][None]
    lb = lambda k: ln_b[l, k][None]
    s = {}
    s["x0"], s["gp0"], s["bp0"], s["hb0"] = xprev, gp, bp, hb
    s["g1"], s["u1"], s["a1"] = ffn_up(hb, w["ffn1_w_up"][l], name="ffn_up")
    s["xh1"], s["rs1"], s["hb1"] = down_ln(s["a1"], w["ffn1_w_down"][l], xprev, gp, bp, lg(0), lb(0), name="ffn_down_ln")
    s["p"] = mm_rows([(s["hb1"], w["mix_w_in"][l], False, 0)], D_IN_PAD, name="mix_in", tn=1024)
    gq, gkv = w["q_norm_g"][l][None], w["kv_norm_g"][l][None]
    s["cqn"], s["ckvn"], s["q2"], s["kv"], s["kr"] = mla_prep(s["p"], gq, gkv, w["w_uq"][l], w["w_ukv"][l], tabs, name="mla_prep")
    s["o2"], s["lse"] = attn_fwd(s["q2"], s["kv"], s["kr"], name="attn_fwd")
    s["ycv"], s["conv"] = conv_fwd(s["p"], w["conv_w"][l], name="conv_fwd")
    s["bc"], s["bm"], s["mg"], s["xh2"], s["rs2"], s["hb2"] = merge_out_ln(
        s["ycv"], s["o2"], s["p"], w["mix_b_gate"][l], w["w_br_conv"][l], w["w_br_mla"][l], w["w_o"][l],
        s["xh1"], lg(0), lb(0), lg(1), lb(1), name="merge_out_ln")
    s["g2"], s["u2"], s["a2"] = ffn_up(s["hb2"], w["ffn2_w_up"][l], name="ffn_up")
    s["xh3"], s["rs3"], s["hb3"] = down_ln(s["a2"], w["ffn2_w_down"][l], s["xh2"], lg(1), lb(1), lg(2), lb(2), name="ffn_down_ln")
    st.append(s)
    return s["xh3"], lg(2), lb(2), s["hb3"]


def _ffn_bwd(dh, s, k, w_up, w_down, ln_gain, hb_in, gate, up, act, xh, rs):
    dz, dzb, dgam, dbet = ln_bwd(dh, xh, rs, ln_gain, branch_scale=0.5, name="ln_bwd")
    d_wd = tn_mm(act, dzb, tm=D_FF // 2, name="dw_down")
    dgate, dup = ffn_down_bwd(dzb, w_down, gate, up, name="ffn_down_bwd")
    d_wg = tn_mm(hb_in, dgate, tm=512, name="dw_up")
    d_wu = tn_mm(hb_in, dup, tm=512, name="dw_up")
    dh_in = mm_rows([(dgate, w_up, True, 0), (dup, w_up, True, 1)], D_MODEL, name="ffn_up_bwd", tn=512, addend=dz, add_scale=ALPHA)
    return dh_in, jnp.concatenate([d_wg, d_wu], axis=1), d_wd, dgam, dbet


def _layer_bwd(l, s, dh, w, tabs_bwd):
    ln_g = w["ln_g"]
    lg = lambda k: ln_g[l, k][None]
    g = {}
    dh, g["ffn2_w_up"], g["ffn2_w_down"], dg2, db2 = _ffn_bwd(
        dh, s, 2, w["ffn2_w_up"][l], w["ffn2_w_down"][l], lg(2), s["hb2"], s["g2"], s["u2"], s["a2"], s["xh3"], s["rs3"])
    dz, dzb, dg1, db1 = ln_bwd(dh, s["xh2"], s["rs2"], lg(1), branch_scale=1.0, name="ln_bwd")
    g["w_o"] = tn_mm(s["mg"], dzb, tm=1024, name="dw_o")
    dbc, dbm, dgg, dycv, do2, dl, g["mix_b_gate"] = merge_bwd(
        dzb, w["w_o"][l], s["bc"], s["bm"], s["p"], w["mix_b_gate"][l], w["w_br_conv"][l], w["w_br_mla"][l], s["o2"], name="merge_bwd")
    g["w_br_conv"] = tn_mm(s["ycv"], dbc, tm=512, name="dw_br_conv")
    g["w_br_mla"] = _unpad_w_br_mla(tn_mm(s["o2"], dbm, tm=1024, name="dw_br_mla"))
    dkv, dkr, dq2 = attn_bwd(s["q2"], s["kv"], s["kr"], do2, s["lse"], dl, name="attn_bwd")
    gq, gkv = w["q_norm_g"][l][None], w["kv_norm_g"][l][None]
    dqb, dsm, dgq, dgkv = mla_prep_bwd(dq2, dkv, dkr, s["p"], gq, gkv, w["w_uq"][l], w["w_ukv"][l], tabs_bwd, name="mla_prep_bwd")
    g["q_norm_g"], g["kv_norm_g"] = dgq[0], dgkv[0]
    g["w_uq"] = _unpad_w_uq(tn_mm(s["cqn"], dqb, tm=Q_LORA, name="dw_uq"))
    g["w_ukv"] = tn_mm(s["ckvn"], dkv, tm=KV_LORA, name="dw_ukv")
    dbch, dw0, dw1, dw2 = conv_bwd(dycv, s["p"], s["conv"], w["conv_w"][l], name="conv_bwd")
    g["conv_w"] = jnp.concatenate([dw0, dw1, dw2], axis=0)
    w_in = w["mix_w_in"][l]
    d_in = [tn_mm(s["hb1"], dbch, tm=512, name="dw_in_bch"), tn_mm(s["hb1"], dsm, tm=1024, name="dw_in_sm"),
            tn_mm(s["hb1"], dgg, tm=512, name="dw_in_gg")]
    d_in = jnp.concatenate(d_in, axis=1)
    g["mix_w_in"] = jnp.concatenate([d_in[:, :1952], d_in[:, 2048:]], axis=1)
    dh = mm_rows([(dbch, w_in, True, 0), (dsm, w_in, True, 3), (dgg, w_in, True, 1)], D_MODEL, name="mix_in_bwd", tn=512,
                 addend=dz, add_scale=ALPHA)
    dh, g["ffn1_w_up"], g["ffn1_w_down"], dg0, db0 = _ffn_bwd(
        dh, s, 0, w["ffn1_w_up"][l], w["ffn1_w_down"][l], lg(0), s["hb0"], s["g1"], s["u1"], s["a1"], s["xh1"], s["rs1"])
    g["ln_g"] = jnp.concatenate([dg0, dg1, dg2], axis=0)
    g["ln_b"] = jnp.concatenate([db0, db1, db2], axis=0)
    return dh, g


BIG = ("ffn1_w_up", "ffn1_w_down", "mix_w_in", "w_uq", "w_ukv", "w_br_conv", "w_br_mla", "w_o", "ffn2_w_up", "ffn2_w_down")
BIG_AXIS = (2, 1, 2, 2, 2, 2, 2, 1, 2, 1)
SMALL_SHARDED = ("meta_tokens", "mix_b_gate", "conv_w", "ln_g", "ln_b")
SMALL_REPLICATED = ("q_norm_g", "kv_norm_g")
WEIGHTS = ("meta_tokens", "ffn1_w_up", "ffn1_w_down", "mix_w_in", "mix_b_gate", "conv_w", "q_norm_g", "w_uq", "kv_norm_g", "w_ukv",
           "w_br_conv", "w_br_mla", "w_o", "ffn2_w_up", "ffn2_w_down", "ln_g", "ln_b")


def _view2d(a):
    return a.reshape(-1, a.shape[-1])


def _local_grads(x_row, target_row, w):
    seq = x_row.shape[0]
    t_real = N_META + seq
    tp = _pad_rows(t_real, TM)
    pad = tp - t_real
    h0 = jnp.concatenate([w["meta_tokens"], x_row, jnp.zeros((pad, D_MODEL), F32)], axis=0)
    target_p = jnp.concatenate([jnp.zeros((N_META, D_MODEL), F32), target_row, jnp.zeros((pad, D_MODEL), F32)], axis=0)
    tabs, tabs_bwd = _rope_tables(tp)
    ones = jnp.ones((1, D_MODEL), F32)
    zeros = jnp.zeros((1, D_MODEL), F32)
    saved = []
    cur = (h0, ones, zeros, h0.astype(BF))
    for l in range(DEPTH):
        cur = _layer_fwd(l, saved, *cur, w, tabs)
    dh, loss_acc = loss_grad(cur[0], cur[1], cur[2], target_p, seq, name="loss_grad")
    grads = [None] * DEPTH
    for l in reversed(range(DEPTH)):
        dh, grads[l] = _layer_bwd(l, saved[l], dh, w, tabs_bwd)
    gfull = {n: jnp.stack([grads[l][n] for l in range(DEPTH)]) for n in grads[0]}
    gfull["meta_tokens"] = dh[:N_META]
    return loss_acc, dh[N_META:t_real], gfull


def kernel(x, meta_tokens, ffn1_w_up, ffn1_w_down, mix_w_in, mix_b_gate, conv_w, q_norm_g, w_uq, kv_norm_g, w_ukv, w_br_conv, w_br_mla, w_o, ffn2_w_up, ffn2_w_down, ln_g, ln_b, loss_target, m_meta_tokens, m_ffn1_w_up, m_ffn1_w_down, m_mix_w_in, m_mix_b_gate, m_conv_w, m_q_norm_g, m_w_uq, m_kv_norm_g, m_w_ukv, m_w_br_conv, m_w_br_mla, m_w_o, m_ffn2_w_up, m_ffn2_w_down, m_ln_g, m_ln_b, v_meta_tokens, v_ffn1_w_up, v_ffn1_w_down, v_mix_w_in, v_mix_b_gate, v_conv_w, v_q_norm_g, v_w_uq, v_kv_norm_g, v_w_ukv, v_w_br_conv, v_w_br_mla, v_w_o, v_ffn2_w_up, v_ffn2_w_down, v_ln_g, v_ln_b):
    local = dict(meta_tokens=meta_tokens, ffn1_w_up=ffn1_w_up, ffn1_w_down=ffn1_w_down, mix_w_in=mix_w_in, mix_b_gate=mix_b_gate,
                 conv_w=conv_w, q_norm_g=q_norm_g, w_uq=w_uq, kv_norm_g=kv_norm_g, w_ukv=w_ukv, w_br_conv=w_br_conv,
                 w_br_mla=w_br_mla, w_o=w_o, ffn2_w_up=ffn2_w_up, ffn2_w_down=ffn2_w_down, ln_g=ln_g, ln_b=ln_b)
    mom_m = dict(zip(WEIGHTS, (m_meta_tokens, m_ffn1_w_up, m_ffn1_w_down, m_mix_w_in, m_mix_b_gate, m_conv_w, m_q_norm_g, m_w_uq,
                               m_kv_norm_g, m_w_ukv, m_w_br_conv, m_w_br_mla, m_w_o, m_ffn2_w_up, m_ffn2_w_down, m_ln_g, m_ln_b)))
    mom_v = dict(zip(WEIGHTS, (v_meta_tokens, v_ffn1_w_up, v_ffn1_w_down, v_mix_w_in, v_mix_b_gate, v_conv_w, v_q_norm_g, v_w_uq,
                               v_kv_norm_g, v_w_ukv, v_w_br_conv, v_w_br_mla, v_w_o, v_ffn2_w_up, v_ffn2_w_down, v_ln_g, v_ln_b)))
    xi, yi, ci = _place()
    chip = 2 * xi + yi

    big_rows_mult = 2 * COMM_ROW_BLOCK
    packed = _pack([local[n] for n in BIG], BF, big_rows_mult, COMM_COLS)
    gathered = gather_chips(packed, name="gather_weights").reshape(4, -1)
    w = {}
    off = 0
    for n, ax in zip(BIG, BIG_AXIS):
        sz = local[n].size
        w[n] = jnp.concatenate([gathered[j, off:off + sz].reshape(local[n].shape) for j in range(4)], axis=ax)
        off += sz
    small_vec = _pack([local[n] for n in SMALL_SHARDED], F32, 8, LANES)
    sr = small_vec.shape[0]
    own = (jnp.arange(4)[:, None, None] == chip) & (ci == 0)
    small_all = allreduce_small(jnp.where(own, small_vec[None], 0.0).reshape(4 * sr, LANES), name="gather_small").reshape(4, -1)
    off = 0
    for n in SMALL_SHARDED:
        sz = local[n].size
        w[n] = jnp.concatenate([small_all[j, off:off + sz].reshape(local[n].shape) for j in range(4)], axis=-1)
        off += sz
    for n in SMALL_REPLICATED:
        w[n] = local[n]
    w["mix_w_in"] = jnp.stack([_pad_w_in(w["mix_w_in"][l]) for l in range(DEPTH)])
    w["w_uq"] = jnp.stack([_pad_w_uq(w["w_uq"][l]) for l in range(DEPTH)])
    w["w_br_mla"] = jnp.stack([_pad_w_br_mla(w["w_br_mla"][l]) for l in range(DEPTH)])

    loss_acc, grad_x, gfull = _local_grads(x[0], loss_target[0], w)
    grad_x = grad_x[None]

    chunks = []
    for j in range(4):
        parts = []
        for n, ax in zip(BIG, BIG_AXIS):
            sh = local[n].shape[ax]
            parts.append(lax.slice_in_dim(gfull[n], j * sh, (j + 1) * sh, axis=ax))
        chunks.append(_pack(parts, BF, big_rows_mult, COMM_COLS))
    gsend = jnp.stack(chunks)
    from_sibling = pair_exchange(gsend, name="rs_pair_exchange")
    pair_sum = pair_add(gsend, from_sibling, jnp.reshape(ci, (1,)).astype(jnp.int32), name="rs_pair_add")
    from_chips = chip_scatter(pair_sum, name="rs_chip_scatter")
    reduced_half = sum_chunks(from_chips, name="rs_sum")
    reduced = pair_gather(reduced_half, name="rs_pair_gather").reshape(-1)
    gshard = dict(zip(BIG, _unpack(reduced, [local[n].shape for n in BIG])))

    small_names = SMALL_SHARDED + SMALL_REPLICATED
    small_full = [gfull[n] for n in small_names] + [loss_acc[0:1, :]]
    small_red = allreduce_small(_pack(small_full, F32, 8, LANES), name="reduce_small").reshape(-1)
    small_out = _unpack(small_red, [a.shape for a in small_full])
    loss = small_out[-1][0, 0]
    for n, full in zip(small_names, small_out[:-1]):
        if n in SMALL_SHARDED:
            sh = local[n].shape[-1]
            gshard[n] = lax.dynamic_slice_in_dim(full, chip * sh, sh, axis=full.ndim - 1)
        else:
            gshard[n] = full

    delta, new_m, new_v = {}, {}, {}
    for n in WEIGHTS:
        shape = local[n].shape
        d, nm, nv = adamw(_view2d(local[n]), _view2d(gshard[n]), _view2d(mom_m[n]), _view2d(mom_v[n]), name="adamw")
        delta[n], new_m[n], new_v[n] = d.reshape(shape), nm.reshape(shape), nv.reshape(shape)
    return (loss, grad_x, *[gshard[n] for n in WEIGHTS], *[delta[n] for n in WEIGHTS], *[new_m[n] for n in WEIGHTS],
            *[new_v[n] for n in WEIGHTS])
```

```python
import functools

import jax
import jax.numpy as jnp
from jax import lax
from jax.experimental import pallas as pl
from jax.experimental.pallas import tpu as pltpu

F32 = jnp.float32
BF = jnp.bfloat16
MESH = pl.DeviceIdType.MESH

D_MODEL = 1024
DEPTH = 2
N_META = 16
D_CONV = 512
MLA_HEADS = 8
QK_NOPE = 64
QK_ROPE = 32
V_HEAD = 64
Q_LORA = 256
KV_LORA = 128
ROPE_BASE = 10000.0
NEG_INF = -1e30
D_FF = 2816
ALPHA = (2 * DEPTH) ** 0.25
LN_EPS = 1e-5
RMS_EPS = 1e-6
ATT_SCALE = (QK_NOPE + QK_ROPE) ** -0.5
D_IN = 4000
D_IN_PAD = 4096
Q_PAD = 256
KV_PAD = 128

ADAM_LR = 0.001
ADAM_B1 = 0.9
ADAM_B2 = 0.999
ADAM_EPS = 1e-08
ADAM_WD = 0.01
ADAM_STEP = 10

TM = 768
TMH = 384
LANES = 128
COMM_COLS = 512
COMM_ROW_BLOCK = 1472
VMEM_LIMIT_BYTES = 50 * 1024 * 1024

NT = (((1,), (1,)), ((), ()))
TN = (((0,), (0,)), ((), ()))


def _pcall(body, **kw):
    return pl.pallas_call(body, **kw)


def _params(n_axes):
    return pltpu.CompilerParams(dimension_semantics=("arbitrary",) * n_axes, vmem_limit_bytes=VMEM_LIMIT_BYTES)


def _sds(shape, dtype):
    return jax.ShapeDtypeStruct(shape, dtype)


def mm_rows(pairs, n_out, *, name, tn=None, addend=None, add_scale=1.0, out_dtype=F32):
    tp = pairs[0][0].shape[0]
    tn = tn or n_out
    in_specs, args = [], []
    for a, b, nt, kb in pairs:
        k = a.shape[1]
        in_specs.append(pl.BlockSpec((TM, k), lambda i, j: (i, 0)))
        if nt:
            in_specs.append(pl.BlockSpec((tn, k), functools.partial(lambda i, j, kb: (j, kb), kb=kb)))
        else:
            in_specs.append(pl.BlockSpec((k, tn), lambda i, j: (0, j)))
        args += [a, b]
    if addend is not None:
        in_specs.append(pl.BlockSpec((TM, tn), lambda i, j: (i, j)))
        args.append(addend)
    n_pairs = len(pairs)
    nts = [p[2] for p in pairs]

    def body(*refs):
        o_ref = refs[-1]
        acc = None
        for p in range(n_pairs):
            a = refs[2 * p][...].astype(BF)
            b = refs[2 * p + 1][...]
            d = lax.dot_general(a, b, NT if nts[p] else (((1,), (0,)), ((), ())), preferred_element_type=F32)
            acc = d if acc is None else acc + d
        if addend is not None:
            acc = acc + add_scale * refs[2 * n_pairs][...]
        o_ref[...] = acc.astype(o_ref.dtype)

    return _pcall(
        body, name=name, grid=(tp // TM, n_out // tn), in_specs=in_specs,
        out_specs=pl.BlockSpec((TM, tn), lambda i, j: (i, j)), out_shape=_sds((tp, n_out), out_dtype),
        compiler_params=_params(2),
    )(*args)


def tn_mm(a, b, *, tm, name, out_dtype=BF, shard=None, layer=0, slot0=0, dst=None):
    tp, m = a.shape
    n = b.shape[1]
    nk = tp // TM
    if shard is None:
        pieces, out_block, out_index, out_full = 1, (tm, n), (lambda i, k: (i, 0)), (m, n)
    elif shard[0] == "cols":
        pieces = n // shard[1]
        out_block, out_full = (1, pieces, tm, shard[1]), (DEPTH, 4, m, shard[1])
        out_index = lambda i, k: (layer, slot0 // pieces, i, 0)
    else:
        pieces = tm // shard[1]
        out_block, out_full = (1, pieces, shard[1], n), (DEPTH, 4, m // 4, n)
        out_index = lambda i, k: (layer, i, 0, 0)

    def body(a_ref, b_ref, *rest):
        o_ref, acc_ref = rest[-2], rest[-1]
        k = pl.program_id(1)

        @pl.when(k == 0)
        def _():
            acc_ref[...] = jnp.zeros_like(acc_ref)

        acc_ref[...] += lax.dot_general(a_ref[...].astype(BF), b_ref[...].astype(BF), TN, preferred_element_type=F32)

        @pl.when(k == nk - 1)
        def _():
            if shard is None:
                o_ref[...] = acc_ref[...].astype(o_ref.dtype)
            elif shard[0] == "cols":
                for j in range(pieces):
                    o_ref[0, j] = acc_ref[:, j * shard[1]:(j + 1) * shard[1]].astype(o_ref.dtype)
            else:
                for j in range(pieces):
                    o_ref[0, j] = acc_ref[j * shard[1]:(j + 1) * shard[1], :].astype(o_ref.dtype)

    in_specs = [pl.BlockSpec((TM, tm), lambda i, k: (k, i)), pl.BlockSpec((TM, n), lambda i, k: (k, 0))]
    args = [a, b]
    aliases = {}
    if dst is not None:
        in_specs.append(pl.BlockSpec(memory_space=pl.ANY))
        args.append(dst)
        aliases = {2: 0}
    return _pcall(
        body, name=name, grid=(m // tm, nk), in_specs=in_specs, out_specs=pl.BlockSpec(out_block, out_index),
        out_shape=_sds(out_full, out_dtype), input_output_aliases=aliases,
        scratch_shapes=[pltpu.VMEM((tm, n), F32)], compiler_params=_params(2),
    )(*args)


def _ln_store(z, g_ref, b_ref, xh_ref, rs_ref, hb_ref):
    mu = jnp.mean(z, axis=-1, keepdims=True)
    zc = z - mu
    var = jnp.mean(zc * zc, axis=-1, keepdims=True)
    rstd = lax.rsqrt(var + LN_EPS)
    xh = zc * rstd
    xh_ref[...] = xh
    rs_ref[...] = rstd
    hb_ref[...] = (xh * g_ref[...] + b_ref[...]).astype(BF)


def _ln_out(tp, tm=TM):
    specs = [pl.BlockSpec((tm, D_MODEL), lambda i: (i, 0)), pl.BlockSpec((tm, 1), lambda i: (i, 0)),
             pl.BlockSpec((tm, D_MODEL), lambda i: (i, 0))]
    shapes = [_sds((tp, D_MODEL), F32), _sds((tp, 1), F32), _sds((tp, D_MODEL), BF)]
    return specs, shapes


def _row_vec(n):
    return pl.BlockSpec((1, n), lambda i: (0, 0))


def ffn_up(hb, wup, *, name):
    tp = hb.shape[0]
    tn = D_FF // 2
    nj = D_FF // tn

    def body(h_ref, wg_ref, wu_ref, g_ref, u_ref, a_ref):
        h = h_ref[...]
        g = jnp.dot(h, wg_ref[...], preferred_element_type=F32)
        u = jnp.dot(h, wu_ref[...], preferred_element_type=F32)
        g_ref[...] = g.astype(BF)
        u_ref[...] = u.astype(BF)
        a_ref[...] = (g * jax.nn.sigmoid(g) * u).astype(BF)

    blk = pl.BlockSpec((TM, tn), lambda i, j: (i, j))
    return _pcall(
        body, name=name, grid=(tp // TM, nj),
        in_specs=[pl.BlockSpec((TM, D_MODEL), lambda i, j: (i, 0)), pl.BlockSpec((D_MODEL, tn), lambda i, j: (0, j)),
                  pl.BlockSpec((D_MODEL, tn), lambda i, j: (0, j + nj))],
        out_specs=[blk, blk, blk], out_shape=[_sds((tp, D_FF), BF)] * 3, compiler_params=_params(2),
    )(hb, wup, wup)


def down_ln(a, wd, xprev, gp, bp, g, b, *, name):
    tp = a.shape[0]

    def body(a_ref, wd_ref, xp_ref, gp_ref, bp_ref, g_ref, b_ref, xh_ref, rs_ref, hb_ref):
        f = jnp.dot(a_ref[...], wd_ref[...], preferred_element_type=F32)
        hprev = xp_ref[...] * gp_ref[...] + bp_ref[...]
        _ln_store(ALPHA * hprev + 0.5 * f, g_ref, b_ref, xh_ref, rs_ref, hb_ref)

    out_specs, out_shape = _ln_out(tp)
    return _pcall(
        body, name=name, grid=(tp // TM,),
        in_specs=[pl.BlockSpec((TM, D_FF), lambda i: (i, 0)), pl.BlockSpec((D_FF, D_MODEL), lambda i: (0, 0)),
                  pl.BlockSpec((TM, D_MODEL), lambda i: (i, 0))] + [_row_vec(D_MODEL)] * 4,
        out_specs=out_specs, out_shape=out_shape, compiler_params=_params(1),
    )(a, wd, xprev, gp, bp, g, b)


def _rope(x, c, s1, s2, reps):
    n = x.shape[1]
    if reps > 1:
        c, s1, s2 = (jnp.tile(t, (1, reps)) for t in (c, s1, s2))
    return x * c + pltpu.roll(x, 16, 1) * s1 + pltpu.roll(x, n - 16, 1) * s2


def _rms(x, g):
    r = lax.rsqrt(jnp.mean(x * x, axis=-1, keepdims=True) + RMS_EPS)
    return x * r * g, r


def mla_prep(p, gq, gkv, wuq_p, wukv, tabs, *, name):
    tp = p.shape[0]
    nh = MLA_HEADS

    def body(cq_ref, ckv_ref, kr_ref, gq_ref, gkv_ref, wuq_ref, wukv_ref, cq_t, s1q_t, s2q_t, ck_t, s1k_t, s2k_t,
             cqn_ref, ckvn_ref, q2_ref, kv_ref, krr_ref):
        cqn, _ = _rms(cq_ref[...], gq_ref[...])
        ckvn, _ = _rms(ckv_ref[...], gkv_ref[...])
        cqn = cqn.astype(BF)
        ckvn = ckvn.astype(BF)
        cqn_ref[...] = cqn
        ckvn_ref[...] = ckvn
        q = jnp.dot(cqn, wuq_ref[...], preferred_element_type=F32)
        q2_ref[...] = _rope(q, cq_t[...], s1q_t[...], s2q_t[...], nh).astype(BF)
        kv_ref[...] = jnp.dot(ckvn, wukv_ref[...], preferred_element_type=F32).astype(BF)
        krr_ref[...] = _rope(kr_ref[...], ck_t[...], s1k_t[...], s2k_t[...], 1).astype(BF)

    def rows(n, col=0):
        return pl.BlockSpec((TMH, n), functools.partial(lambda i, col: (i, col), col=col))

    return _pcall(
        body, name=name, grid=(tp // TMH,),
        in_specs=[rows(Q_LORA, 1536 // Q_LORA), rows(KV_LORA, 1792 // KV_LORA), rows(LANES, 1920 // LANES),
                  _row_vec(Q_LORA), _row_vec(KV_LORA),
                  pl.BlockSpec((Q_LORA, nh * Q_PAD), lambda i: (0, 0)), pl.BlockSpec((KV_LORA, nh * KV_PAD), lambda i: (0, 0)),
                  rows(Q_PAD), rows(Q_PAD), rows(Q_PAD), rows(LANES), rows(LANES), rows(LANES)],
        out_specs=[rows(Q_LORA), rows(KV_LORA), rows(nh * Q_PAD), rows(nh * KV_PAD), rows(LANES)],
        out_shape=[_sds((tp, Q_LORA), BF), _sds((tp, KV_LORA), BF), _sds((tp, nh * Q_PAD), BF),
                   _sds((tp, nh * KV_PAD), BF), _sds((tp, LANES), BF)],
        compiler_params=_params(1),
    )(p, p, p, gq, gkv, wuq_p, wukv, *tabs)


def _scores(q, kvb, krb, diagonal):
    k2 = jnp.concatenate([kvb, krb], axis=1)
    s = lax.dot_general(q, k2, NT, preferred_element_type=F32)
    if diagonal:
        qpos = lax.broadcasted_iota(jnp.int32, (TM, TM), 0)
        kpos = lax.broadcasted_iota(jnp.int32, (TM, TM), 1)
        s = jnp.where(kpos <= qpos, s, NEG_INF)
    return s, k2


def _causal_steps(nb, key_major):
    if key_major:
        pairs = [(qi, ki) for ki in range(nb) for qi in range(ki, nb)]
    else:
        pairs = [(qi, ki) for qi in range(nb) for ki in range(qi + 1)]
    return jnp.array([p[0] for p in pairs], jnp.int32), jnp.array([p[1] for p in pairs], jnp.int32)


def attn_fwd(q2, kv, kr, *, name):
    tp = q2.shape[0]
    nh = MLA_HEADS
    nb = tp // TM
    rep = TM // LANES
    q_tab, k_tab = _causal_steps(nb, key_major=False)

    def body(qt_ref, kt_ref, q_ref, kv_ref, kr_ref, o_ref, lse_ref, m_ref, l_ref, acc_ref):
        s_idx = pl.program_id(1)
        qi = qt_ref[s_idx]
        ki = kt_ref[s_idx]

        @pl.when(ki == 0)
        def _():
            m_ref[...] = jnp.full_like(m_ref, NEG_INF)
            l_ref[...] = jnp.zeros_like(l_ref)
            acc_ref[...] = jnp.zeros_like(acc_ref)

        def step(diagonal):
            kvb = kv_ref[...]
            s, _ = _scores(q_ref[...], kvb, kr_ref[...], diagonal)
            m_prev = m_ref[...]
            m_new = jnp.maximum(m_prev, jnp.max(s, axis=1, keepdims=True))
            alpha = jnp.exp(m_prev - m_new)
            p = jnp.exp(s - jnp.tile(m_new, (1, rep)))
            l_ref[...] = alpha * l_ref[...] + jnp.sum(p, axis=1, keepdims=True)
            acc_ref[...] = alpha * acc_ref[...] + jnp.dot(p.astype(BF), kvb, preferred_element_type=F32)
            m_ref[...] = m_new

        @pl.when(ki < qi)
        def _():
            step(False)

        @pl.when(ki == qi)
        def _():
            step(True)
            o_ref[...] = (acc_ref[...] / l_ref[...]).astype(BF)
            lse_ref[...] = m_ref[...] + jnp.log(l_ref[...])

    grid_spec = pltpu.PrefetchScalarGridSpec(
        num_scalar_prefetch=2, grid=(nh, q_tab.shape[0]),
        in_specs=[pl.BlockSpec((TM, Q_PAD), lambda h, s, qt, kt: (qt[s], h)),
                  pl.BlockSpec((TM, KV_PAD), lambda h, s, qt, kt: (kt[s], h)),
                  pl.BlockSpec((TM, LANES), lambda h, s, qt, kt: (kt[s], 0))],
        out_specs=[pl.BlockSpec((TM, KV_PAD), lambda h, s, qt, kt: (qt[s], h)),
                   pl.BlockSpec((TM, LANES), lambda h, s, qt, kt: (qt[s], h))],
        scratch_shapes=[pltpu.VMEM((TM, LANES), F32)] * 3,
    )
    return _pcall(
        body, name=name, grid_spec=grid_spec, out_shape=[_sds((tp, nh * KV_PAD), BF), _sds((tp, nh * LANES), F32)],
        compiler_params=_params(2),
    )(q_tab, k_tab, q2, kv, kr)


def conv_fwd(p, w, *, name):
    tp = p.shape[0]

    def body(b_ref, c_ref, h_ref, w_ref, y_ref, cv_ref, ebuf):
        i = pl.program_id(0)

        @pl.when(i == 0)
        def _():
            ebuf[0:8, :] = jnp.zeros((8, D_CONV), F32)

        e = c_ref[...] * h_ref[...]
        ebuf[8:8 + TM, :] = e
        w_all = w_ref[...]
        conv = w_all[0:1] * ebuf[pl.ds(6, TM), :] + w_all[1:2] * ebuf[pl.ds(7, TM), :] + w_all[2:3] * e
        cv_ref[...] = conv.astype(BF)
        y_ref[...] = (b_ref[...] * conv).astype(BF)
        ebuf[0:8, :] = ebuf[TM:TM + 8, :]

    def col(j):
        return pl.BlockSpec((TM, D_CONV), functools.partial(lambda i, j: (i, j), j=j))

    return _pcall(
        body, name=name, grid=(tp // TM,),
        in_specs=[col(0), col(1), col(2), pl.BlockSpec((3, D_CONV), lambda i: (0, 0))],
        out_specs=[col(0), col(0)], out_shape=[_sds((tp, D_CONV), BF)] * 2,
        scratch_shapes=[pltpu.VMEM((TM + 8, D_CONV), F32)], compiler_params=_params(1),
    )(p, p, p, w)


def merge_out_ln(ycv, o2, p, bg, wbc, wbm_p, wo, xprev, gp, bp, g, b, *, name):
    tp = ycv.shape[0]

    def body(y_ref, o_ref, gc_ref, gm_ref, bg_ref, wbc_ref, wbm_ref, wo_ref, xp_ref, gp_ref, bp_ref, g_ref, b_ref,
             bc_ref, bm_ref, mg_ref, xh_ref, rs_ref, hb_ref):
        bc = jnp.dot(y_ref[...], wbc_ref[...], preferred_element_type=F32)
        bm = jnp.dot(o_ref[...], wbm_ref[...], preferred_element_type=F32)
        bgv = bg_ref[...]
        mg = jax.nn.sigmoid(gc_ref[...] + bgv[0:1]) * bc + jax.nn.sigmoid(gm_ref[...] + bgv[1:2]) * bm
        mgb = mg.astype(BF)
        bc_ref[...] = bc.astype(BF)
        bm_ref[...] = bm.astype(BF)
        mg_ref[...] = mgb
        mix = jnp.dot(mgb, wo_ref[...], preferred_element_type=F32)
        hprev = xp_ref[...] * gp_ref[...] + bp_ref[...]
        _ln_store(ALPHA * hprev + mix, g_ref, b_ref, xh_ref, rs_ref, hb_ref)

    def rows(n, col=0):
        return pl.BlockSpec((TMH, n), functools.partial(lambda i, col: (i, col), col=col))

    def whole(r, c):
        return pl.BlockSpec((r, c), lambda i: (0, 0))

    ln_specs, ln_shapes = _ln_out(tp, TMH)
    return _pcall(
        body, name=name, grid=(tp // TMH,),
        in_specs=[rows(D_CONV), rows(MLA_HEADS * KV_PAD), rows(D_MODEL, 2), rows(D_MODEL, 3), whole(2, D_MODEL),
                  whole(D_CONV, D_MODEL), whole(MLA_HEADS * KV_PAD, D_MODEL), whole(D_MODEL, D_MODEL), rows(D_MODEL)]
        + [_row_vec(D_MODEL)] * 4,
        out_specs=[rows(D_MODEL)] * 3 + ln_specs, out_shape=[_sds((tp, D_MODEL), BF)] * 3 + ln_shapes,
        compiler_params=_params(1),
    )(ycv, o2, p, p, bg, wbc, wbm_p, wo, xprev, gp, bp, g, b)


def loss_grad(xh, g, b, target_p, n_real, *, name):
    tp = xh.shape[0]

    def body(x_ref, g_ref, b_ref, t_ref, dy_ref, loss_ref):
        i = pl.program_id(0)

        @pl.when(i == 0)
        def _():
            loss_ref[...] = jnp.zeros_like(loss_ref)

        row = i * TM + lax.broadcasted_iota(jnp.int32, (TM, 1), 0)
        real = (row >= N_META) & (row < N_META + n_real)
        diff = jnp.where(real, x_ref[...] * g_ref[...] + b_ref[...] - t_ref[...], 0.0)
        dy_ref[...] = diff * (1.0 / D_MODEL)
        loss_ref[...] += 0.5 / D_MODEL * jnp.sum(diff * diff)

    return _pcall(
        body, name=name, grid=(tp // TM,),
        in_specs=[pl.BlockSpec((TM, D_MODEL), lambda i: (i, 0)), _row_vec(D_MODEL), _row_vec(D_MODEL),
                  pl.BlockSpec((TM, D_MODEL), lambda i: (i, 0))],
        out_specs=[pl.BlockSpec((TM, D_MODEL), lambda i: (i, 0)), pl.BlockSpec((8, LANES), lambda i: (0, 0))],
        out_shape=[_sds((tp, D_MODEL), F32), _sds((8, LANES), F32)], compiler_params=_params(1),
    )(xh, g, b, target_p)


def ln_bwd(dh, xh, rstd, g, *, branch_scale, name):
    tp = dh.shape[0]

    def body(dh_ref, xh_ref, rs_ref, g_ref, dz_ref, dzb_ref, dg_ref, db_ref):
        i = pl.program_id(0)

        @pl.when(i == 0)
        def _():
            dg_ref[...] = jnp.zeros_like(dg_ref)
            db_ref[...] = jnp.zeros_like(db_ref)

        dy = dh_ref[...]
        xhat = xh_ref[...]
        dg_ref[...] += jnp.sum(dy * xhat, axis=0, keepdims=True)
        db_ref[...] += jnp.sum(dy, axis=0, keepdims=True)
        dxh = dy * g_ref[...]
        m1 = jnp.mean(dxh, axis=-1, keepdims=True)
        m2 = jnp.mean(dxh * xhat, axis=-1, keepdims=True)
        dz = rs_ref[...] * (dxh - m1 - xhat * m2)
        dz_ref[...] = dz
        dzb_ref[...] = (branch_scale * dz).astype(BF)

    rows = pl.BlockSpec((TM, D_MODEL), lambda i: (i, 0))
    return _pcall(
        body, name=name, grid=(tp // TM,),
        in_specs=[rows, rows, pl.BlockSpec((TM, 1), lambda i: (i, 0)), _row_vec(D_MODEL)],
        out_specs=[rows, rows, _row_vec(D_MODEL), _row_vec(D_MODEL)],
        out_shape=[_sds((tp, D_MODEL), F32), _sds((tp, D_MODEL), BF), _sds((1, D_MODEL), F32), _sds((1, D_MODEL), F32)],
        compiler_params=_params(1),
    )(dh, xh, rstd, g)


def ffn_down_bwd(dzb, wd, gate, up, *, name):
    tp = dzb.shape[0]
    tn = D_FF // 2

    def body(dz_ref, wd_ref, g_ref, u_ref, dg_ref, du_ref):
        da = lax.dot_general(dz_ref[...], wd_ref[...], NT, preferred_element_type=F32)
        g = g_ref[...].astype(F32)
        u = u_ref[...].astype(F32)
        sg = jax.nn.sigmoid(g)
        dg_ref[...] = (da * u * sg * (1.0 + g * (1.0 - sg))).astype(BF)
        du_ref[...] = (da * g * sg).astype(BF)

    blk = pl.BlockSpec((TM, tn), lambda i, j: (i, j))
    return _pcall(
        body, name=name, grid=(tp // TM, D_FF // tn),
        in_specs=[pl.BlockSpec((TM, D_MODEL), lambda i, j: (i, 0)), pl.BlockSpec((tn, D_MODEL), lambda i, j: (j, 0)), blk, blk],
        out_specs=[blk, blk], out_shape=[_sds((tp, D_FF), BF)] * 2, compiler_params=_params(2),
    )(dzb, wd, gate, up)


def merge_bwd(dzb, wo, bc, bm, p, bg, wbc, wbm_p, o2, *, name):
    tp = dzb.shape[0]
    nh = MLA_HEADS

    def body(dz_ref, wo_ref, bc_ref, bm_ref, gc_ref, gm_ref, bg_ref, wbc_ref, wbm_ref, o_ref,
             dbc_ref, dbm_ref, dgg_ref, dy_ref, do_ref, dl_ref, dbg_ref):
        i = pl.program_id(0)

        @pl.when(i == 0)
        def _():
            dbg_ref[...] = jnp.zeros_like(dbg_ref)

        dmg = lax.dot_general(dz_ref[...], wo_ref[...], NT, preferred_element_type=F32)
        bgv = bg_ref[...]
        sc = jax.nn.sigmoid(gc_ref[...] + bgv[0:1])
        sm = jax.nn.sigmoid(gm_ref[...] + bgv[1:2])
        dbc = (dmg * sc).astype(BF)
        dbm = (dmg * sm).astype(BF)
        dgc = dmg * bc_ref[...].astype(F32) * sc * (1.0 - sc)
        dgm = dmg * bm_ref[...].astype(F32) * sm * (1.0 - sm)
        dbc_ref[...] = dbc
        dbm_ref[...] = dbm
        dgg_ref[...] = jnp.concatenate([dgc, dgm], axis=1).astype(BF)
        dbg_ref[...] += jnp.concatenate([jnp.sum(dgc, axis=0, keepdims=True), jnp.sum(dgm, axis=0, keepdims=True)], axis=0)
        dy_ref[...] = lax.dot_general(dbc, wbc_ref[...], NT, preferred_element_type=F32)
        do = lax.dot_general(dbm, wbm_ref[...], NT, preferred_element_type=F32)
        do_ref[...] = do.astype(BF)
        prod = do * o_ref[...].astype(F32)
        parts = []
        for h in range(nh):
            d = jnp.sum(prod[:, h * KV_PAD:(h + 1) * KV_PAD], axis=1, keepdims=True)
            parts.append(jnp.broadcast_to(d, (TMH, LANES)))
        dl_ref[...] = jnp.concatenate(parts, axis=1)

    def rows(n, col=0):
        return pl.BlockSpec((TMH, n), functools.partial(lambda i, col: (i, col), col=col))

    def whole(r, c):
        return pl.BlockSpec((r, c), lambda i: (0, 0))

    return _pcall(
        body, name=name, grid=(tp // TMH,),
        in_specs=[rows(D_MODEL), whole(D_MODEL, D_MODEL), rows(D_MODEL), rows(D_MODEL), rows(D_MODEL, 2), rows(D_MODEL, 3),
                  whole(2, D_MODEL), whole(D_CONV, D_MODEL), whole(nh * KV_PAD, D_MODEL), rows(nh * KV_PAD)],
        out_specs=[rows(D_MODEL), rows(D_MODEL), rows(2 * D_MODEL), rows(D_CONV), rows(nh * KV_PAD), rows(nh * LANES),
                   whole(2, D_MODEL)],
        out_shape=[_sds((tp, D_MODEL), BF), _sds((tp, D_MODEL), BF), _sds((tp, 2 * D_MODEL), BF), _sds((tp, D_CONV), F32),
                   _sds((tp, nh * KV_PAD), BF), _sds((tp, nh * LANES), F32), _sds((2, D_MODEL), F32)],
        compiler_params=_params(1),
    )(dzb, wo, bc, bm, p, p, bg, wbc, wbm_p, o2)


def attn_bwd(q2, kv, kr, do2, lse, dl, *, name):
    tp = q2.shape[0]
    nh = MLA_HEADS
    nb = tp // TM
    rep = TM // LANES
    q_tab, k_tab = _causal_steps(nb, key_major=True)

    def body(qt_ref, kt_ref, q_ref, kv_ref, kr_ref, do_ref, lse_ref, dl_ref, dkv_ref, dkr_ref, dq_ref, dkv_acc, dkr_acc):
        s_idx = pl.program_id(1)
        qi = qt_ref[s_idx]
        ki = kt_ref[s_idx]

        @pl.when(s_idx == 0)
        def _():
            dq_ref[...] = jnp.zeros_like(dq_ref)

        @pl.when(qi == ki)
        def _():
            dkv_acc[...] = jnp.zeros_like(dkv_acc)
            dkr_acc[...] = jnp.zeros_like(dkr_acc)

        def step(diagonal):
            q = q_ref[...]
            do = do_ref[...]
            kvb = kv_ref[...]
            s, k2 = _scores(q, kvb, kr_ref[...], diagonal)
            p = jnp.exp(s - jnp.tile(lse_ref[...], (1, rep)))
            dp = lax.dot_general(do, kvb, NT, preferred_element_type=F32)
            dsb = (p * (dp - jnp.tile(dl_ref[...], (1, rep)))).astype(BF)
            dk2 = lax.dot_general(dsb, q, TN, preferred_element_type=F32)
            dkv_acc[...] += lax.dot_general(p.astype(BF), do, TN, preferred_element_type=F32) + dk2[:, :KV_PAD]
            dkr_acc[...] += dk2[:, KV_PAD:KV_PAD + LANES]
            rows = pl.ds(pl.multiple_of(qi * TM, TM), TM)
            dq_ref[rows, :] += jnp.dot(dsb, k2, preferred_element_type=F32)

        @pl.when(qi == ki)
        def _():
            step(True)

        @pl.when(qi > ki)
        def _():
            step(False)

        @pl.when(qi == nb - 1)
        def _():
            dkv_ref[...] = dkv_acc[...].astype(BF)
            dkr_ref[...] = dkr_acc[...]

    def qrow(n):
        return pl.BlockSpec((TM, n), lambda h, s, qt, kt: (qt[s], h))

    def krow(n):
        return pl.BlockSpec((TM, n), lambda h, s, qt, kt: (kt[s], h))

    grid_spec = pltpu.PrefetchScalarGridSpec(
        num_scalar_prefetch=2, grid=(nh, q_tab.shape[0]),
        in_specs=[qrow(Q_PAD), krow(KV_PAD), pl.BlockSpec((TM, LANES), lambda h, s, qt, kt: (kt[s], 0)),
                  qrow(KV_PAD), qrow(LANES), qrow(LANES)],
        out_specs=[krow(KV_PAD), krow(LANES), pl.BlockSpec((tp, Q_PAD), lambda h, s, qt, kt: (0, h))],
        scratch_shapes=[pltpu.VMEM((TM, KV_PAD), F32), pltpu.VMEM((TM, LANES), F32)],
    )
    return _pcall(
        body, name=name, grid_spec=grid_spec,
        out_shape=[_sds((tp, nh * KV_PAD), BF), _sds((tp, nh * LANES), F32), _sds((tp, nh * Q_PAD), F32)],
        compiler_params=_params(2),
    )(q_tab, k_tab, q2, kv, kr, do2, lse, dl)


def _rms_bwd(x, g, dy):
    r = lax.rsqrt(jnp.mean(x * x, axis=-1, keepdims=True) + RMS_EPS)
    gy = dy * g
    dx = r * gy - x * (r * r * r) * jnp.mean(x * gy, axis=-1, keepdims=True)
    return dx, jnp.sum(dy * x * r, axis=0, keepdims=True)


def mla_prep_bwd(dq2, dkv, dkr, p, gq, gkv, wuq_p, wukv, tabs_bwd, *, name):
    tp = dq2.shape[0]
    nh = MLA_HEADS

    def body(dq_ref, dkv_ref, dkr_ref, cq_ref, ckv_ref, gq_ref, gkv_ref, wuq_ref, wukv_ref,
             cq_t, s1q_t, s2q_t, ck_t, s1k_t, s2k_t, dqb_ref, dsm_ref, dgq_ref, dgkv_ref):
        i = pl.program_id(0)

        @pl.when(i == 0)
        def _():
            dgq_ref[...] = jnp.zeros_like(dgq_ref)
            dgkv_ref[...] = jnp.zeros_like(dgkv_ref)

        dqb = _rope(dq_ref[...], cq_t[...], s1q_t[...], s2q_t[...], nh).astype(BF)
        dqb_ref[...] = dqb
        dcqn = lax.dot_general(dqb, wuq_ref[...], NT, preferred_element_type=F32)
        dcq, dgq = _rms_bwd(cq_ref[...], gq_ref[...], dcqn)
        dckvn = lax.dot_general(dkv_ref[...], wukv_ref[...], NT, preferred_element_type=F32)
        dckv, dgkv = _rms_bwd(ckv_ref[...], gkv_ref[...], dckvn)
        dkr_heads = dkr_ref[...]
        dkr_sum = dkr_heads[:, :LANES]
        for h in range(1, nh):
            dkr_sum = dkr_sum + dkr_heads[:, h * LANES:(h + 1) * LANES]
        dkr = _rope(dkr_sum, ck_t[...], s1k_t[...], s2k_t[...], 1)
        dsm_ref[...] = jnp.concatenate([dcq, dckv, dkr], axis=1)
        dgq_ref[...] += dgq
        dgkv_ref[...] += dgkv

    def rows(n, col=0):
        return pl.BlockSpec((TMH, n), functools.partial(lambda i, col: (i, col), col=col))

    return _pcall(
        body, name=name, grid=(tp // TMH,),
        in_specs=[rows(nh * Q_PAD), rows(nh * KV_PAD), rows(nh * LANES), rows(Q_LORA, 1536 // Q_LORA), rows(KV_LORA, 1792 // KV_LORA),
                  _row_vec(Q_LORA), _row_vec(KV_LORA),
                  pl.BlockSpec((Q_LORA, nh * Q_PAD), lambda i: (0, 0)), pl.BlockSpec((KV_LORA, nh * KV_PAD), lambda i: (0, 0)),
                  rows(Q_PAD), rows(Q_PAD), rows(Q_PAD), rows(LANES), rows(LANES), rows(LANES)],
        out_specs=[rows(nh * Q_PAD), rows(Q_LORA + KV_LORA + LANES), _row_vec(Q_LORA), _row_vec(KV_LORA)],
        out_shape=[_sds((tp, nh * Q_PAD), BF), _sds((tp, Q_LORA + KV_LORA + LANES), F32), _sds((1, Q_LORA), F32),
                   _sds((1, KV_LORA), F32)],
        compiler_params=_params(1),
    )(dq2, dkv, dkr, p, p, gq, gkv, wuq_p, wukv, *tabs_bwd)


def conv_bwd(dy, p, conv, w, *, name):
    tp = dy.shape[0]
    nb = tp // TM

    def body(dy_ref, b_ref, c_ref, h_ref, cv_ref, w_ref, dp_ref, dw0_ref, dw1_ref, dw2_ref, dbuf):
        i = pl.program_id(0)

        @pl.when(i == 0)
        def _():
            dbuf[TM:TM + 8, :] = jnp.zeros((8, D_CONV), F32)
            dw0_ref[...] = jnp.zeros_like(dw0_ref)
            dw1_ref[...] = jnp.zeros_like(dw1_ref)
            dw2_ref[...] = jnp.zeros_like(dw2_ref)

        dyv = dy_ref[...]
        c = c_ref[...]
        hh = h_ref[...]
        dconv = dyv * b_ref[...]
        dbuf[0:TM, :] = dconv
        d1 = dbuf[pl.ds(1, TM), :]
        d2 = dbuf[pl.ds(2, TM), :]
        w_all = w_ref[...]
        de = w_all[2:3] * dconv + w_all[1:2] * d1 + w_all[0:1] * d2
        e = c * hh
        dp_ref[...] = jnp.concatenate([dyv * cv_ref[...].astype(F32), de * hh, de * c], axis=1).astype(BF)
        dw0_ref[...] += jnp.sum(d2 * e, axis=0, keepdims=True)
        dw1_ref[...] += jnp.sum(d1 * e, axis=0, keepdims=True)
        dw2_ref[...] += jnp.sum(dconv * e, axis=0, keepdims=True)
        dbuf[TM:TM + 8, :] = dbuf[0:8, :]

    def col(j):
        return pl.BlockSpec((TM, D_CONV), functools.partial(lambda i, j: (nb - 1 - i, j), j=j))

    return _pcall(
        body, name=name, grid=(nb,),
        in_specs=[col(0), col(0), col(1), col(2), col(0), pl.BlockSpec((3, D_CONV), lambda i: (0, 0))],
        out_specs=[pl.BlockSpec((TM, 3 * D_CONV), lambda i: (nb - 1 - i, 0))] + [_row_vec(D_CONV)] * 3,
        out_shape=[_sds((tp, 3 * D_CONV), BF)] + [_sds((1, D_CONV), F32)] * 3,
        scratch_shapes=[pltpu.VMEM((TM + 8, D_CONV), F32)], compiler_params=_params(1),
    )(dy, p, p, p, conv, w)


def adamw(w, g, m, v, *, name):
    r, c = w.shape
    tr = r
    for cand in (256, 128, 64, 32, 16, 8):
        if r % cand == 0 and r > cand:
            tr = cand
            break

    def body(w_ref, g_ref, m_ref, v_ref, d_ref, nm_ref, nv_ref):
        gv = g_ref[...]
        nm = ADAM_B1 * m_ref[...] + (1.0 - ADAM_B1) * gv
        nv = ADAM_B2 * v_ref[...] + (1.0 - ADAM_B2) * (gv * gv)
        m_hat = nm / (1.0 - ADAM_B1 ** ADAM_STEP)
        v_hat = nv / (1.0 - ADAM_B2 ** ADAM_STEP)
        d_ref[...] = -ADAM_LR * (m_hat / (jnp.sqrt(v_hat) + ADAM_EPS) + ADAM_WD * w_ref[...])
        nm_ref[...] = nm
        nv_ref[...] = nv

    blk = pl.BlockSpec((tr, c), lambda i: (i, 0))
    return _pcall(
        body, name=name, grid=(r // tr,), in_specs=[blk] * 4, out_specs=[blk] * 3,
        out_shape=[_sds((r, c), F32)] * 3, compiler_params=_params(1),
    )(w, g, m, v)


HBM_SPEC = pl.BlockSpec(memory_space=pltpu.HBM)


def _place():
    return lax.axis_index("x"), lax.axis_index("y"), lax.axis_index("c")


def _other_chips(x, y):
    return [(1 - x, y), (x, 1 - y), (1 - x, 1 - y)]


def gather_chips(shards, *, name):
    n = len(shards)

    def body(*refs):
        x_refs, o_refs = refs[:n], refs[2 * n:3 * n]
        send_sems, recv_sems = refs[3 * n:]
        x, y, c = _place()
        me = 2 * x + y
        chips = _other_chips(x, y)
        mine = pl.ds(c, 1)
        other = pl.ds(1 - c, 1)

        def copy(sem, src, dst, to):
            return pltpu.make_async_remote_copy(src_ref=src, dst_ref=dst, send_sem=send_sems.at[sem], recv_sem=recv_sems.at[sem],
                                                device_id=to, device_id_type=MESH)

        sends = []
        for i in range(n):
            for k, (px, py) in enumerate(chips):
                sends.append(copy(6 * i + k, x_refs[i].at[mine], o_refs[i].at[me, mine], (px, py, c)))
        for cp in sends:
            cp.start()
        passed = []
        for k, (px, py) in enumerate(chips):
            j = 2 * px + py
            for i in range(n):
                copy(6 * i + k, x_refs[i].at[mine], o_refs[i].at[j, mine], (px, py, c)).wait_recv()
                fwd = copy(6 * i + 3 + k, o_refs[i].at[j, mine], o_refs[i].at[j, mine], (x, y, 1 - c))
                fwd.start()
                passed.append(fwd)
        for k, (px, py) in enumerate(chips):
            j = 2 * px + py
            for i in range(n):
                copy(6 * i + 3 + k, o_refs[i].at[j, other], o_refs[i].at[j, other], (x, y, 1 - c)).wait_recv()
        for cp in sends + passed:
            cp.wait_send()

    prefilled = [jnp.broadcast_to(s[None], (4,) + s.shape) for s in shards]
    return _pcall(
        body, name=name, in_specs=[HBM_SPEC] * (2 * n), out_specs=[HBM_SPEC] * n,
        out_shape=[_sds(p.shape, p.dtype) for p in prefilled], input_output_aliases={n + i: i for i in range(n)},
        scratch_shapes=[pltpu.SemaphoreType.DMA((6 * n,)), pltpu.SemaphoreType.DMA((6 * n,))],
    )(*shards, *prefilled)


def pair_exchange(gs, *, name):
    n = len(gs)

    def body(*refs):
        g_refs, o_refs = refs[:n], refs[n:2 * n]
        send_sems, recv_sems = refs[2 * n:]
        x, y, c = _place()
        cps = [pltpu.make_async_remote_copy(src_ref=g_refs[i].at[1 - c], dst_ref=o_refs[i], send_sem=send_sems.at[i],
                                            recv_sem=recv_sems.at[i], device_id=(x, y, 1 - c), device_id_type=MESH)
               for i in range(n)]
        for cp in cps:
            cp.start()
        for cp in cps:
            cp.wait()

    return _pcall(
        body, name=name, in_specs=[HBM_SPEC] * n, out_specs=[HBM_SPEC] * n, out_shape=[_sds(g.shape[1:], g.dtype) for g in gs],
        scratch_shapes=[pltpu.SemaphoreType.DMA((n,)), pltpu.SemaphoreType.DMA((n,))],
    )(*gs)


def _comm_rows(a, b, itemsize):
    return a // 2 if a * b * itemsize > (3 << 19) and a % 16 == 0 else a


def pair_add(g, s1, c_idx, *, name):
    _, n, a, b = g.shape
    ta = _comm_rows(a, b, 2)

    def body(c_ref, g_ref, s_ref, o_ref):
        o_ref[...] = (g_ref[0].astype(F32) + s_ref[...].astype(F32)).astype(o_ref.dtype)

    grid_spec = pltpu.PrefetchScalarGridSpec(
        num_scalar_prefetch=1, grid=(n, a // ta),
        in_specs=[pl.BlockSpec((1, 1, ta, b), lambda j, i, c_ref: (c_ref[0], j, i, 0)),
                  pl.BlockSpec((1, ta, b), lambda j, i, c_ref: (j, i, 0))],
        out_specs=pl.BlockSpec((1, ta, b), lambda j, i, c_ref: (j, i, 0)),
    )
    return _pcall(body, name=name, grid_spec=grid_spec, out_shape=_sds((n, a, b), g.dtype), compiler_params=_params(2))(
        c_idx, g, s1)


def chip_scatter(pss, *, name):
    n = len(pss)

    def body(*refs):
        p_refs, o_refs = refs[:n], refs[2 * n:3 * n]
        send_sems, recv_sems = refs[3 * n:]
        x, y, c = _place()
        me = 2 * x + y
        chips = _other_chips(x, y)

        def copy(i, k, j_src, j_dst, to):
            return pltpu.make_async_remote_copy(src_ref=p_refs[i].at[j_src], dst_ref=o_refs[i].at[j_dst],
                                                send_sem=send_sems.at[3 * i + k], recv_sem=recv_sems.at[3 * i + k],
                                                device_id=to, device_id_type=MESH)

        sends = [copy(i, k, 2 * px + py, me, (px, py, c)) for i in range(n) for k, (px, py) in enumerate(chips)]
        for cp in sends:
            cp.start()
        for i in range(n):
            for k, (px, py) in enumerate(chips):
                copy(i, k, me, 2 * px + py, (px, py, c)).wait_recv()
        for cp in sends:
            cp.wait_send()

    xi, yi, _ = _place()
    own = jnp.arange(4)[:, None, None] == 2 * xi + yi
    prefilled = [jnp.where(own, p, jnp.zeros_like(p)) for p in pss]
    return _pcall(
        body, name=name, in_specs=[HBM_SPEC] * (2 * n), out_specs=[HBM_SPEC] * n, out_shape=[_sds(p.shape, p.dtype) for p in pss],
        input_output_aliases={n + i: i for i in range(n)},
        scratch_shapes=[pltpu.SemaphoreType.DMA((3 * n,)), pltpu.SemaphoreType.DMA((3 * n,))],
    )(*pss, *prefilled)


def sum_chunks(s2, *, name):
    n, a, b = s2.shape
    ta = _comm_rows(a, b, 4)

    def body(s_ref, o_ref):
        acc = s_ref[0].astype(F32)
        for j in range(1, n):
            acc = acc + s_ref[j].astype(F32)
        o_ref[...] = acc

    return _pcall(
        body, name=name, grid=(a // ta,), in_specs=[pl.BlockSpec((n, ta, b), lambda i: (0, i, 0))],
        out_specs=pl.BlockSpec((ta, b), lambda i: (i, 0)), out_shape=_sds((a, b), F32), compiler_params=_params(1),
    )(s2)


def pair_gather(rcs, *, name):
    n = len(rcs)

    def body(*refs):
        r_refs, o_refs = refs[:n], refs[2 * n:3 * n]
        send_sems, recv_sems = refs[3 * n:]
        x, y, c = _place()

        def copy(i, layer):
            return pltpu.make_async_remote_copy(src_ref=r_refs[i], dst_ref=o_refs[i].at[layer], send_sem=send_sems.at[i],
                                                recv_sem=recv_sems.at[i], device_id=(x, y, 1 - c), device_id_type=MESH)

        sends = [copy(i, c) for i in range(n)]
        for cp in sends:
            cp.start()
        for i in range(n):
            copy(i, 1 - c).wait_recv()
        for cp in sends:
            cp.wait_send()

    prefilled = [jnp.broadcast_to(r[None], (DEPTH,) + r.shape) for r in rcs]
    return _pcall(
        body, name=name, in_specs=[HBM_SPEC] * (2 * n), out_specs=[HBM_SPEC] * n,
        out_shape=[_sds(p.shape, p.dtype) for p in prefilled], input_output_aliases={n + i: i for i in range(n)},
        scratch_shapes=[pltpu.SemaphoreType.DMA((n,)), pltpu.SemaphoreType.DMA((n,))],
    )(*rcs, *prefilled)


def exchange_small(arrs, *, reduce, name):
    n = len(arrs)

    def body(*refs):
        v_refs, o_refs = refs[:n], refs[n:2 * n]
        bufs = refs[2 * n:3 * n] if reduce else o_refs
        send_sems, recv_sems = refs[-2:]
        x, y, c = _place()
        me = 4 * x + 2 * y + c
        for i in range(n):
            bufs[i][me] = v_refs[i][...]

        def peer(k):
            dx, dy, dc = (k >> 2) & 1, (k >> 1) & 1, k & 1
            return (1 - x if dx else x, 1 - y if dy else y, 1 - c if dc else c)

        def copy(i, k, slot):
            return pltpu.make_async_remote_copy(src_ref=v_refs[i], dst_ref=bufs[i].at[slot], send_sem=send_sems.at[7 * i + k - 1],
                                                recv_sem=recv_sems.at[7 * i + k - 1], device_id=peer(k), device_id_type=MESH)

        sends = [copy(i, k, me) for i in range(n) for k in range(1, 8)]
        for cp in sends:
            cp.start()
        for i in range(n):
            for k in range(1, 8):
                px, py, pc = peer(k)
                copy(i, k, 4 * px + 2 * py + pc).wait_recv()
        for cp in sends:
            cp.wait_send()
        if reduce:
            for i in range(n):
                acc = bufs[i][0]
                for d in range(1, 8):
                    acc = acc + bufs[i][d]
                o_refs[i][...] = acc

    vmem = pl.BlockSpec(memory_space=pltpu.VMEM)
    stacked = [(8,) + a.shape for a in arrs]
    return _pcall(
        body, name=name, in_specs=[vmem] * n, out_specs=[vmem] * n,
        out_shape=[_sds(a.shape if reduce else s, F32) for a, s in zip(arrs, stacked)],
        scratch_shapes=([pltpu.VMEM(s, F32) for s in stacked] if reduce else [])
        + [pltpu.SemaphoreType.DMA((7 * n,)), pltpu.SemaphoreType.DMA((7 * n,))],
    )(*arrs)


def _pad_rows(n, mult):
    return -(-n // mult) * mult


def _chip_major(g, b):
    return g.reshape(g.shape[0], 4, b).transpose(1, 0, 2)


def _rope_tables(tp):
    inv_freq = 1.0 / (ROPE_BASE ** (jnp.arange(0, QK_ROPE, 2, dtype=F32) / QK_ROPE))
    ang = jnp.arange(tp, dtype=F32)[:, None] * inv_freq[None, :]
    cos, sin = jnp.cos(ang), jnp.sin(ang)
    one = lambda n: jnp.ones((tp, n), F32)
    zero = lambda n: jnp.zeros((tp, n), F32)
    cq = jnp.concatenate([one(128), cos, cos, one(96)], axis=1)
    s1q = jnp.concatenate([zero(144), sin, zero(96)], axis=1)
    s2q = jnp.concatenate([zero(128), -sin, zero(112)], axis=1)
    ck = jnp.concatenate([cos, cos, zero(96)], axis=1)
    s1k = jnp.concatenate([zero(16), sin, zero(96)], axis=1)
    s2k = jnp.concatenate([-sin, zero(112)], axis=1)
    fwd = (cq * ATT_SCALE, s1q * ATT_SCALE, s2q * ATT_SCALE, ck, s1k, s2k)
    bwd = (cq * ATT_SCALE, -s1q * ATT_SCALE, -s2q * ATT_SCALE, ck, -s1k, -s2k)
    return fwd, bwd


def _pad_w_in(w):
    return jnp.concatenate([w[:, :1952], jnp.zeros((w.shape[0], 96), w.dtype), w[:, 1952:]], axis=1)


def _pad_w_uq(w):
    w = w.reshape(Q_LORA, MLA_HEADS, QK_NOPE + QK_ROPE)
    z = lambda n: jnp.zeros((Q_LORA, MLA_HEADS, n), w.dtype)
    return jnp.concatenate([w[..., :QK_NOPE], z(64), w[..., QK_NOPE:], z(96)], axis=-1).reshape(Q_LORA, MLA_HEADS * Q_PAD)


def _unpad_w_uq(w):
    w = w.reshape(Q_LORA, MLA_HEADS, Q_PAD)
    return jnp.concatenate([w[..., :QK_NOPE], w[..., 128:128 + QK_ROPE]], axis=-1).reshape(Q_LORA, MLA_HEADS * (QK_NOPE + QK_ROPE))


def _pad_w_br_mla(w):
    w = w.reshape(MLA_HEADS, V_HEAD, D_MODEL)
    return jnp.concatenate([jnp.zeros_like(w), w], axis=1).reshape(MLA_HEADS * KV_PAD, D_MODEL)


def _unpad_w_br_mla(w):
    return w.reshape(MLA_HEADS, KV_PAD, D_MODEL)[:, V_HEAD:].reshape(MLA_HEADS * V_HEAD, D_MODEL)


def _layer_fwd(l, st, xprev, gp, bp, hb, w, tabs):
    ln_g, ln_b = w["ln_g"], w["ln_b"]
    lg = lambda k: ln_g[l, k][None]
    lb = lambda k: ln_b[l, k][None]
    s = {}
    s["x0"], s["gp0"], s["bp0"], s["hb0"] = xprev, gp, bp, hb
    s["g1"], s["u1"], s["a1"] = ffn_up(hb, w["ffn1_w_up"][l], name="ffn_up")
    s["xh1"], s["rs1"], s["hb1"] = down_ln(s["a1"], w["ffn1_w_down"][l], xprev, gp, bp, lg(0), lb(0), name="ffn_down_ln")
    s["p"] = mm_rows([(s["hb1"], w["mix_w_in"][l], False, 0)], D_IN_PAD, name="mix_in", tn=1024)
    gq, gkv = w["q_norm_g"][l][None], w["kv_norm_g"][l][None]
    s["cqn"], s["ckvn"], s["q2"], s["kv"], s["kr"] = mla_prep(s["p"], gq, gkv, w["w_uq"][l], w["w_ukv"][l], tabs, name="mla_prep")
    s["o2"], s["lse"] = attn_fwd(s["q2"], s["kv"], s["kr"], name="attn_fwd")
    s["ycv"], s["conv"] = conv_fwd(s["p"], w["conv_w"][l], name="conv_fwd")
    s["bc"], s["bm"], s["mg"], s["xh2"], s["rs2"], s["hb2"] = merge_out_ln(
        s["ycv"], s["o2"], s["p"], w["mix_b_gate"][l], w["w_br_conv"][l], w["w_br_mla"][l], w["w_o"][l],
        s["xh1"], lg(0), lb(0), lg(1), lb(1), name="merge_out_ln")
    s["g2"], s["u2"], s["a2"] = ffn_up(s["hb2"], w["ffn2_w_up"][l], name="ffn_up")
    s["xh3"], s["rs3"], s["hb3"] = down_ln(s["a2"], w["ffn2_w_down"][l], s["xh2"], lg(1), lb(1), lg(2), lb(2), name="ffn_down_ln")
    st.append(s)
    return s["xh3"], lg(2), lb(2), s["hb3"]


def _ffn_bwd(l, dh, w_up, w_down, ln_gain, hb_in, gate, up, act, xh, rs, dst_up, dst_down):
    dz, dzb, dgam, dbet = ln_bwd(dh, xh, rs, ln_gain, branch_scale=0.5, name="ln_bwd")
    d_wd = tn_mm(act, dzb, tm=D_FF // 2, name="dw_down", shard=("rows", D_FF // 4), layer=l, dst=dst_down)
    dgate, dup = ffn_down_bwd(dzb, w_down, gate, up, name="ffn_down_bwd")
    d_w = tn_mm(hb_in, dgate, tm=512, name="dw_up", shard=("cols", D_FF // 2), layer=l, slot0=0, dst=dst_up)
    d_w = tn_mm(hb_in, dup, tm=512, name="dw_up", shard=("cols", D_FF // 2), layer=l, slot0=2, dst=d_w)
    dh_in = mm_rows([(dgate, w_up, True, 0), (dup, w_up, True, 1)], D_MODEL, name="ffn_up_bwd", tn=512, addend=dz, add_scale=ALPHA)
    return dh_in, d_w, d_wd, dgam, dbet


def _layer_bwd(l, s, dh, w, tabs_bwd, dst):
    ln_g = w["ln_g"]
    lg = lambda k: ln_g[l, k][None]
    g = {}
    dh, dst["ffn2_w_up"], dst["ffn2_w_down"], dg2, db2 = _ffn_bwd(
        l, dh, w["ffn2_w_up"][l], w["ffn2_w_down"][l], lg(2), s["hb2"], s["g2"], s["u2"], s["a2"], s["xh3"], s["rs3"],
        dst.get("ffn2_w_up"), dst.get("ffn2_w_down"))
    dz, dzb, dg1, db1 = ln_bwd(dh, s["xh2"], s["rs2"], lg(1), branch_scale=1.0, name="ln_bwd")
    dst["w_o"] = tn_mm(s["mg"], dzb, tm=1024, name="dw_o", shard=("rows", D_MODEL // 4), layer=l, dst=dst.get("w_o"))
    dbc, dbm, dgg, dycv, do2, dl, g["mix_b_gate"] = merge_bwd(
        dzb, w["w_o"][l], s["bc"], s["bm"], s["p"], w["mix_b_gate"][l], w["w_br_conv"][l], w["w_br_mla"][l], s["o2"], name="merge_bwd")
    dst["w_br_conv"] = tn_mm(s["ycv"], dbc, tm=512, name="dw_br_conv", shard=("cols", D_MODEL // 4), layer=l,
                             dst=dst.get("w_br_conv"))
    g["w_br_mla"] = _chip_major(_unpad_w_br_mla(tn_mm(s["o2"], dbm, tm=1024, name="dw_br_mla")), D_MODEL // 4)
    dkv, dkr, dq2 = attn_bwd(s["q2"], s["kv"], s["kr"], do2, s["lse"], dl, name="attn_bwd")
    gq, gkv = w["q_norm_g"][l][None], w["kv_norm_g"][l][None]
    dqb, dsm, g["q_norm_g"], g["kv_norm_g"] = mla_prep_bwd(dq2, dkv, dkr, s["p"], gq, gkv, w["w_uq"][l], w["w_ukv"][l], tabs_bwd,
                                                           name="mla_prep_bwd")
    g["w_uq"] = _chip_major(_unpad_w_uq(tn_mm(s["cqn"], dqb, tm=Q_LORA, name="dw_uq")), MLA_HEADS * (QK_NOPE + QK_ROPE) // 4)
    dst["w_ukv"] = tn_mm(s["ckvn"], dkv, tm=KV_LORA, name="dw_ukv", shard=("cols", MLA_HEADS * KV_PAD // 4), layer=l,
                         dst=dst.get("w_ukv"))
    dbch, dw0, dw1, dw2 = conv_bwd(dycv, s["p"], s["conv"], w["conv_w"][l], name="conv_bwd")
    g["conv_w"] = jnp.concatenate([dw0, dw1, dw2], axis=0)
    w_in = w["mix_w_in"][l]
    d_bch = tn_mm(s["hb1"], dbch, tm=512, name="dw_in_bch")
    d_sm = tn_mm(s["hb1"], dsm, tm=1024, name="dw_in_sm")
    d_gg = tn_mm(s["hb1"], dgg, tm=512, name="dw_in_gg")
    g["mix_w_in"] = _chip_major(jnp.concatenate([d_bch, d_sm[:, :1952 - 1536], d_gg], axis=1), D_IN // 4)
    dh = mm_rows([(dbch, w_in, True, 0), (dsm, w_in, True, 3), (dgg, w_in, True, 1)], D_MODEL, name="mix_in_bwd", tn=512,
                 addend=dz, add_scale=ALPHA)
    dh, dst["ffn1_w_up"], dst["ffn1_w_down"], dg0, db0 = _ffn_bwd(
        l, dh, w["ffn1_w_up"][l], w["ffn1_w_down"][l], lg(0), s["hb0"], s["g1"], s["u1"], s["a1"], s["xh1"], s["rs1"],
        dst.get("ffn1_w_up"), dst.get("ffn1_w_down"))
    g["ln_g"] = jnp.concatenate([dg0, dg1, dg2], axis=0)
    g["ln_b"] = jnp.concatenate([db0, db1, db2], axis=0)
    return dh, g


BIG = ("ffn1_w_up", "ffn1_w_down", "mix_w_in", "w_uq", "w_ukv", "w_br_conv", "w_br_mla", "w_o", "ffn2_w_up", "ffn2_w_down")
BIG_AXIS = (2, 1, 2, 2, 2, 2, 2, 1, 2, 1)
SMALL_SHARDED = ("meta_tokens", "mix_b_gate", "conv_w", "ln_g", "ln_b")
SMALL_REPLICATED = ("q_norm_g", "kv_norm_g")
WEIGHTS = ("meta_tokens", "ffn1_w_up", "ffn1_w_down", "mix_w_in", "mix_b_gate", "conv_w", "q_norm_g", "w_uq", "kv_norm_g", "w_ukv",
           "w_br_conv", "w_br_mla", "w_o", "ffn2_w_up", "ffn2_w_down", "ln_g", "ln_b")


def _view2d(a):
    return a.reshape(-1, a.shape[-1])


def _local_grads(x_row, target_row, w):
    seq = x_row.shape[0]
    t_real = N_META + seq
    tp = _pad_rows(t_real, TM)
    pad = tp - t_real
    h0 = jnp.concatenate([w["meta_tokens"], x_row, jnp.zeros((pad, D_MODEL), F32)], axis=0)
    target_p = jnp.concatenate([jnp.zeros((N_META, D_MODEL), F32), target_row, jnp.zeros((pad, D_MODEL), F32)], axis=0)
    tabs, tabs_bwd = _rope_tables(tp)
    ones = jnp.ones((1, D_MODEL), F32)
    zeros = jnp.zeros((1, D_MODEL), F32)
    saved = []
    cur = (h0, ones, zeros, h0.astype(BF))
    for l in range(DEPTH):
        cur = _layer_fwd(l, saved, *cur, w, tabs)
    dh, loss_acc = loss_grad(cur[0], cur[1], cur[2], target_p, seq, name="loss_grad")
    grads = [None] * DEPTH
    gfull = {}
    for l in reversed(range(DEPTH)):
        dh, grads[l] = _layer_bwd(l, saved[l], dh, w, tabs_bwd, gfull)
    for n in grads[0]:
        per_layer = [grads[l][n] for l in range(DEPTH)]
        gfull[n] = jnp.stack(per_layer) if n in BIG else jnp.concatenate(per_layer, axis=0)
    gfull["meta_tokens"] = dh[:N_META]
    return loss_acc, dh[N_META:t_real], gfull


def kernel(x, meta_tokens, ffn1_w_up, ffn1_w_down, mix_w_in, mix_b_gate, conv_w, q_norm_g, w_uq, kv_norm_g, w_ukv, w_br_conv, w_br_mla, w_o, ffn2_w_up, ffn2_w_down, ln_g, ln_b, loss_target, m_meta_tokens, m_ffn1_w_up, m_ffn1_w_down, m_mix_w_in, m_mix_b_gate, m_conv_w, m_q_norm_g, m_w_uq, m_kv_norm_g, m_w_ukv, m_w_br_conv, m_w_br_mla, m_w_o, m_ffn2_w_up, m_ffn2_w_down, m_ln_g, m_ln_b, v_meta_tokens, v_ffn1_w_up, v_ffn1_w_down, v_mix_w_in, v_mix_b_gate, v_conv_w, v_q_norm_g, v_w_uq, v_kv_norm_g, v_w_ukv, v_w_br_conv, v_w_br_mla, v_w_o, v_ffn2_w_up, v_ffn2_w_down, v_ln_g, v_ln_b):
    local = dict(meta_tokens=meta_tokens, ffn1_w_up=ffn1_w_up, ffn1_w_down=ffn1_w_down, mix_w_in=mix_w_in, mix_b_gate=mix_b_gate,
                 conv_w=conv_w, q_norm_g=q_norm_g, w_uq=w_uq, kv_norm_g=kv_norm_g, w_ukv=w_ukv, w_br_conv=w_br_conv,
                 w_br_mla=w_br_mla, w_o=w_o, ffn2_w_up=ffn2_w_up, ffn2_w_down=ffn2_w_down, ln_g=ln_g, ln_b=ln_b)
    mom_m = dict(zip(WEIGHTS, (m_meta_tokens, m_ffn1_w_up, m_ffn1_w_down, m_mix_w_in, m_mix_b_gate, m_conv_w, m_q_norm_g, m_w_uq,
                               m_kv_norm_g, m_w_ukv, m_w_br_conv, m_w_br_mla, m_w_o, m_ffn2_w_up, m_ffn2_w_down, m_ln_g, m_ln_b)))
    mom_v = dict(zip(WEIGHTS, (v_meta_tokens, v_ffn1_w_up, v_ffn1_w_down, v_mix_w_in, v_mix_b_gate, v_conv_w, v_q_norm_g, v_w_uq,
                               v_kv_norm_g, v_w_ukv, v_w_br_conv, v_w_br_mla, v_w_o, v_ffn2_w_up, v_ffn2_w_down, v_ln_g, v_ln_b)))
    xi, yi, ci = _place()
    chip = 2 * xi + yi

    gathered = gather_chips([local[n].astype(BF) for n in BIG], name="gather_weights")
    w = {n: jnp.concatenate([g[j] for j in range(4)], axis=ax) for n, ax, g in zip(BIG, BIG_AXIS, gathered)}
    stacked = exchange_small([_view2d(local[n]) for n in SMALL_SHARDED], reduce=False, name="gather_small")
    for n, st in zip(SMALL_SHARDED, stacked):
        full = jnp.concatenate([st[2 * j] for j in range(4)], axis=-1)
        w[n] = full.reshape(local[n].shape[:-1] + (full.shape[-1],))
    for n in SMALL_REPLICATED:
        w[n] = local[n]
    w["mix_w_in"] = jnp.stack([_pad_w_in(w["mix_w_in"][l]) for l in range(DEPTH)])
    w["w_uq"] = jnp.stack([_pad_w_uq(w["w_uq"][l]) for l in range(DEPTH)])
    w["w_br_mla"] = jnp.stack([_pad_w_br_mla(w["w_br_mla"][l]) for l in range(DEPTH)])

    loss_acc, grad_x, gfull = _local_grads(x[0], loss_target[0], w)
    grad_x = grad_x[None]

    c_idx = jnp.reshape(ci, (1,)).astype(jnp.int32)
    glist = [gfull[n] for n in BIG]
    from_sibling = pair_exchange(glist, name="rs_pair_exchange")
    pair_sums = [pair_add(g, s, c_idx, name="rs_pair_add") for g, s in zip(glist, from_sibling)]
    from_chips = chip_scatter(pair_sums, name="rs_chip_scatter")
    reduced = pair_gather([sum_chunks(r, name="rs_sum") for r in from_chips], name="rs_pair_gather")
    gshard = {n: r.reshape(local[n].shape) for n, r in zip(BIG, reduced)}

    small_names = SMALL_SHARDED + SMALL_REPLICATED
    small_red = exchange_small([gfull[n] for n in small_names] + [loss_acc], reduce=True, name="reduce_small")
    loss = small_red[-1][0, 0]
    for n, full in zip(small_names, small_red[:-1]):
        if n in SMALL_SHARDED:
            sh = local[n].shape[-1]
            full = lax.dynamic_slice_in_dim(full, chip * sh, sh, axis=1)
        gshard[n] = full.reshape(local[n].shape)

    delta, new_m, new_v = {}, {}, {}
    for n in WEIGHTS:
        shape = local[n].shape
        d, nm, nv = adamw(_view2d(local[n]), _view2d(gshard[n]), _view2d(mom_m[n]), _view2d(mom_v[n]), name="adamw")
        delta[n], new_m[n], new_v[n] = d.reshape(shape), nm.reshape(shape), nv.reshape(shape)
    return (loss, grad_x, *[gshard[n] for n in WEIGHTS], *[delta[n] for n in WEIGHTS], *[new_m[n] for n in WEIGHTS],
            *[new_v[n] for n in WEIGHTS])
```

```python
import functools

import jax
import jax.numpy as jnp
from jax import lax
from jax.experimental import pallas as pl
from jax.experimental.pallas import tpu as pltpu

F32 = jnp.float32
BF = jnp.bfloat16
MESH = pl.DeviceIdType.MESH

D_MODEL = 1024
DEPTH = 2
N_META = 16
D_CONV = 512
MLA_HEADS = 8
QK_NOPE = 64
QK_ROPE = 32
V_HEAD = 64
Q_LORA = 256
KV_LORA = 128
ROPE_BASE = 10000.0
NEG_INF = -1e30
D_FF = 2816
ALPHA = (2 * DEPTH) ** 0.25
LN_EPS = 1e-5
RMS_EPS = 1e-6
ATT_SCALE = (QK_NOPE + QK_ROPE) ** -0.5
D_IN = 4000
D_IN_PAD = 4096
Q_PAD = 256
KV_PAD = 128

ADAM_LR = 0.001
ADAM_B1 = 0.9
ADAM_B2 = 0.999
ADAM_EPS = 1e-08
ADAM_WD = 0.01
ADAM_STEP = 10

TM = 768
TMH = 384
LANES = 128
COMM_COLS = 512
COMM_ROW_BLOCK = 1472
VMEM_LIMIT_BYTES = 50 * 1024 * 1024

NT = (((1,), (1,)), ((), ()))
TN = (((0,), (0,)), ((), ()))


def _pcall(body, **kw):
    return pl.pallas_call(body, **kw)


def _params(n_axes):
    return pltpu.CompilerParams(dimension_semantics=("arbitrary",) * n_axes, vmem_limit_bytes=VMEM_LIMIT_BYTES)


def _sds(shape, dtype):
    return jax.ShapeDtypeStruct(shape, dtype)


def mm_rows(pairs, n_out, *, name, tn=None, addend=None, add_scale=1.0, out_dtype=F32):
    tp = pairs[0][0].shape[0]
    tn = tn or n_out
    in_specs, args = [], []
    for a, b, nt, kb in pairs:
        k = a.shape[1]
        in_specs.append(pl.BlockSpec((TM, k), lambda i, j: (i, 0)))
        if nt:
            in_specs.append(pl.BlockSpec((tn, k), functools.partial(lambda i, j, kb: (j, kb), kb=kb)))
        else:
            in_specs.append(pl.BlockSpec((k, tn), lambda i, j: (0, j)))
        args += [a, b]
    if addend is not None:
        in_specs.append(pl.BlockSpec((TM, tn), lambda i, j: (i, j)))
        args.append(addend)
    n_pairs = len(pairs)
    nts = [p[2] for p in pairs]

    def body(*refs):
        o_ref = refs[-1]
        acc = None
        for p in range(n_pairs):
            a = refs[2 * p][...].astype(BF)
            b = refs[2 * p + 1][...]
            d = lax.dot_general(a, b, NT if nts[p] else (((1,), (0,)), ((), ())), preferred_element_type=F32)
            acc = d if acc is None else acc + d
        if addend is not None:
            acc = acc + add_scale * refs[2 * n_pairs][...]
        o_ref[...] = acc.astype(o_ref.dtype)

    return _pcall(
        body, name=name, grid=(tp // TM, n_out // tn), in_specs=in_specs,
        out_specs=pl.BlockSpec((TM, tn), lambda i, j: (i, j)), out_shape=_sds((tp, n_out), out_dtype),
        compiler_params=_params(2),
    )(*args)


def tn_mm(a, b, *, tm, name, out_dtype=BF, shard=None, layer=0, slot0=0, dst=None):
    tp, m = a.shape
    n = b.shape[1]
    nk = tp // TM
    if shard is None:
        pieces, out_block, out_index, out_full = 1, (tm, n), (lambda i, k: (i, 0)), (m, n)
    elif shard[0] == "cols":
        pieces = n // shard[1]
        out_block, out_full = (1, pieces, tm, shard[1]), (DEPTH, 4, m, shard[1])
        out_index = lambda i, k: (layer, slot0 // pieces, i, 0)
    else:
        pieces = tm // shard[1]
        out_block, out_full = (1, pieces, shard[1], n), (DEPTH, 4, m // 4, n)
        out_index = lambda i, k: (layer, i, 0, 0)

    def body(a_ref, b_ref, *rest):
        o_ref, acc_ref = rest[-2], rest[-1]
        k = pl.program_id(1)

        @pl.when(k == 0)
        def _():
            acc_ref[...] = jnp.zeros_like(acc_ref)

        acc_ref[...] += lax.dot_general(a_ref[...].astype(BF), b_ref[...].astype(BF), TN, preferred_element_type=F32)

        @pl.when(k == nk - 1)
        def _():
            if shard is None:
                o_ref[...] = acc_ref[...].astype(o_ref.dtype)
            elif shard[0] == "cols":
                for j in range(pieces):
                    o_ref[0, j] = acc_ref[:, j * shard[1]:(j + 1) * shard[1]].astype(o_ref.dtype)
            else:
                for j in range(pieces):
                    o_ref[0, j] = acc_ref[j * shard[1]:(j + 1) * shard[1], :].astype(o_ref.dtype)

    in_specs = [pl.BlockSpec((TM, tm), lambda i, k: (k, i)), pl.BlockSpec((TM, n), lambda i, k: (k, 0))]
    args = [a, b]
    aliases = {}
    if dst is not None:
        in_specs.append(pl.BlockSpec(memory_space=pl.ANY))
        args.append(dst)
        aliases = {2: 0}
    return _pcall(
        body, name=name, grid=(m // tm, nk), in_specs=in_specs, out_specs=pl.BlockSpec(out_block, out_index),
        out_shape=_sds(out_full, out_dtype), input_output_aliases=aliases,
        scratch_shapes=[pltpu.VMEM((tm, n), F32)], compiler_params=_params(2),
    )(*args)


def _ln_store(z, g_ref, b_ref, xh_ref, rs_ref, hb_ref):
    mu = jnp.mean(z, axis=-1, keepdims=True)
    zc = z - mu
    var = jnp.mean(zc * zc, axis=-1, keepdims=True)
    rstd = lax.rsqrt(var + LN_EPS)
    xh = zc * rstd
    xh_ref[...] = xh
    rs_ref[...] = rstd
    hb_ref[...] = (xh * g_ref[...] + b_ref[...]).astype(BF)


def _ln_out(tp, tm=TM):
    specs = [pl.BlockSpec((tm, D_MODEL), lambda i: (i, 0)), pl.BlockSpec((tm, 1), lambda i: (i, 0)),
             pl.BlockSpec((tm, D_MODEL), lambda i: (i, 0))]
    shapes = [_sds((tp, D_MODEL), F32), _sds((tp, 1), F32), _sds((tp, D_MODEL), BF)]
    return specs, shapes


def _row_vec(n):
    return pl.BlockSpec((1, n), lambda i: (0, 0))


def ffn_up(hb, wup, *, name):
    tp = hb.shape[0]
    tn = D_FF // 2
    nj = D_FF // tn

    def body(h_ref, wg_ref, wu_ref, g_ref, u_ref, a_ref):
        h = h_ref[...]
        g = jnp.dot(h, wg_ref[...], preferred_element_type=F32)
        u = jnp.dot(h, wu_ref[...], preferred_element_type=F32)
        g_ref[...] = g.astype(BF)
        u_ref[...] = u.astype(BF)
        a_ref[...] = (g * jax.nn.sigmoid(g) * u).astype(BF)

    blk = pl.BlockSpec((TM, tn), lambda i, j: (i, j))
    return _pcall(
        body, name=name, grid=(tp // TM, nj),
        in_specs=[pl.BlockSpec((TM, D_MODEL), lambda i, j: (i, 0)), pl.BlockSpec((D_MODEL, tn), lambda i, j: (0, j)),
                  pl.BlockSpec((D_MODEL, tn), lambda i, j: (0, j + nj))],
        out_specs=[blk, blk, blk], out_shape=[_sds((tp, D_FF), BF)] * 3, compiler_params=_params(2),
    )(hb, wup, wup)


def down_ln(a, wd, xprev, gp, bp, g, b, *, name):
    tp = a.shape[0]

    def body(a_ref, wd_ref, xp_ref, gp_ref, bp_ref, g_ref, b_ref, xh_ref, rs_ref, hb_ref):
        f = jnp.dot(a_ref[...], wd_ref[...], preferred_element_type=F32)
        hprev = xp_ref[...] * gp_ref[...] + bp_ref[...]
        _ln_store(ALPHA * hprev + 0.5 * f, g_ref, b_ref, xh_ref, rs_ref, hb_ref)

    out_specs, out_shape = _ln_out(tp)
    return _pcall(
        body, name=name, grid=(tp // TM,),
        in_specs=[pl.BlockSpec((TM, D_FF), lambda i: (i, 0)), pl.BlockSpec((D_FF, D_MODEL), lambda i: (0, 0)),
                  pl.BlockSpec((TM, D_MODEL), lambda i: (i, 0))] + [_row_vec(D_MODEL)] * 4,
        out_specs=out_specs, out_shape=out_shape, compiler_params=_params(1),
    )(a, wd, xprev, gp, bp, g, b)


def _rope(x, c, s1, s2, reps):
    n = x.shape[1]
    if reps > 1:
        c, s1, s2 = (jnp.tile(t, (1, reps)) for t in (c, s1, s2))
    return x * c + pltpu.roll(x, 16, 1) * s1 + pltpu.roll(x, n - 16, 1) * s2


def _rms(x, g):
    r = lax.rsqrt(jnp.mean(x * x, axis=-1, keepdims=True) + RMS_EPS)
    return x * r * g, r


def mla_prep(p, gq, gkv, wuq_p, wukv, tabs, *, name):
    tp = p.shape[0]
    nh = MLA_HEADS

    def body(cq_ref, ckv_ref, kr_ref, gq_ref, gkv_ref, wuq_ref, wukv_ref, cq_t, s1q_t, s2q_t, ck_t, s1k_t, s2k_t,
             cqn_ref, ckvn_ref, q2_ref, kv_ref, krr_ref):
        cqn, _ = _rms(cq_ref[...], gq_ref[...])
        ckvn, _ = _rms(ckv_ref[...], gkv_ref[...])
        cqn = cqn.astype(BF)
        ckvn = ckvn.astype(BF)
        cqn_ref[...] = cqn
        ckvn_ref[...] = ckvn
        q = jnp.dot(cqn, wuq_ref[...], preferred_element_type=F32)
        q2_ref[...] = _rope(q, cq_t[...], s1q_t[...], s2q_t[...], nh).astype(BF)
        kv_ref[...] = jnp.dot(ckvn, wukv_ref[...], preferred_element_type=F32).astype(BF)
        krr_ref[...] = _rope(kr_ref[...], ck_t[...], s1k_t[...], s2k_t[...], 1).astype(BF)

    def rows(n, col=0):
        return pl.BlockSpec((TMH, n), functools.partial(lambda i, col: (i, col), col=col))

    return _pcall(
        body, name=name, grid=(tp // TMH,),
        in_specs=[rows(Q_LORA, 1536 // Q_LORA), rows(KV_LORA, 1792 // KV_LORA), rows(LANES, 1920 // LANES),
                  _row_vec(Q_LORA), _row_vec(KV_LORA),
                  pl.BlockSpec((Q_LORA, nh * Q_PAD), lambda i: (0, 0)), pl.BlockSpec((KV_LORA, nh * KV_PAD), lambda i: (0, 0)),
                  rows(Q_PAD), rows(Q_PAD), rows(Q_PAD), rows(LANES), rows(LANES), rows(LANES)],
        out_specs=[rows(Q_LORA), rows(KV_LORA), rows(nh * Q_PAD), rows(nh * KV_PAD), rows(LANES)],
        out_shape=[_sds((tp, Q_LORA), BF), _sds((tp, KV_LORA), BF), _sds((tp, nh * Q_PAD), BF),
                   _sds((tp, nh * KV_PAD), BF), _sds((tp, LANES), BF)],
        compiler_params=_params(1),
    )(p, p, p, gq, gkv, wuq_p, wukv, *tabs)


def _scores(q, kvb, krb, diagonal):
    k2 = jnp.concatenate([kvb, krb], axis=1)
    s = lax.dot_general(q, k2, NT, preferred_element_type=F32)
    if diagonal:
        qpos = lax.broadcasted_iota(jnp.int32, (TM, TM), 0)
        kpos = lax.broadcasted_iota(jnp.int32, (TM, TM), 1)
        s = jnp.where(kpos <= qpos, s, NEG_INF)
    return s, k2


def _key_rows(k):
    return pl.ds(pl.multiple_of(k * TM, TM), TM)


def attn_fwd(q2, kv, kr, *, name):
    tp = q2.shape[0]
    nh = MLA_HEADS
    nb = tp // TM
    rep = TM // LANES

    def body(q_ref, kv_ref, kr_ref, o_ref, lse_ref, m_ref, l_ref, acc_ref):
        qi = pl.program_id(1)
        m_ref[...] = jnp.full_like(m_ref, NEG_INF)
        l_ref[...] = jnp.zeros_like(l_ref)
        acc_ref[...] = jnp.zeros_like(acc_ref)

        def step(k, diagonal):
            kvb = kv_ref[_key_rows(k), :]
            s, _ = _scores(q_ref[...], kvb, kr_ref[_key_rows(k), :], diagonal)
            m_prev = m_ref[...]
            m_new = jnp.maximum(m_prev, jnp.max(s, axis=1, keepdims=True))
            alpha = jnp.exp(m_prev - m_new)
            p = jnp.exp(s - jnp.tile(m_new, (1, rep)))
            l_ref[...] = alpha * l_ref[...] + jnp.sum(p, axis=1, keepdims=True)
            acc_ref[...] = alpha * acc_ref[...] + jnp.dot(p.astype(BF), kvb, preferred_element_type=F32)
            m_ref[...] = m_new

        def below_diagonal(k, carry):
            step(k, False)
            return carry

        lax.fori_loop(0, qi, below_diagonal, 0)
        step(qi, True)
        o_ref[...] = (acc_ref[...] / l_ref[...]).astype(BF)
        lse_ref[...] = m_ref[...] + jnp.log(l_ref[...])

    return _pcall(
        body, name=name, grid=(nh, nb),
        in_specs=[pl.BlockSpec((TM, Q_PAD), lambda h, qi: (qi, h)), pl.BlockSpec((tp, KV_PAD), lambda h, qi: (0, h)),
                  pl.BlockSpec((tp, LANES), lambda h, qi: (0, 0))],
        out_specs=[pl.BlockSpec((TM, KV_PAD), lambda h, qi: (qi, h)), pl.BlockSpec((TM, LANES), lambda h, qi: (qi, h))],
        out_shape=[_sds((tp, nh * KV_PAD), BF), _sds((tp, nh * LANES), F32)],
        scratch_shapes=[pltpu.VMEM((TM, LANES), F32)] * 3, compiler_params=_params(2),
    )(q2, kv, kr)


def conv_fwd(p, w, *, name):
    tp = p.shape[0]

    def body(b_ref, c_ref, h_ref, w_ref, y_ref, cv_ref, ebuf):
        i = pl.program_id(0)

        @pl.when(i == 0)
        def _():
            ebuf[0:8, :] = jnp.zeros((8, D_CONV), F32)

        e = c_ref[...] * h_ref[...]
        ebuf[8:8 + TM, :] = e
        w_all = w_ref[...]
        conv = w_all[0:1] * ebuf[pl.ds(6, TM), :] + w_all[1:2] * ebuf[pl.ds(7, TM), :] + w_all[2:3] * e
        cv_ref[...] = conv.astype(BF)
        y_ref[...] = (b_ref[...] * conv).astype(BF)
        ebuf[0:8, :] = ebuf[TM:TM + 8, :]

    def col(j):
        return pl.BlockSpec((TM, D_CONV), functools.partial(lambda i, j: (i, j), j=j))

    return _pcall(
        body, name=name, grid=(tp // TM,),
        in_specs=[col(0), col(1), col(2), pl.BlockSpec((3, D_CONV), lambda i: (0, 0))],
        out_specs=[col(0), col(0)], out_shape=[_sds((tp, D_CONV), BF)] * 2,
        scratch_shapes=[pltpu.VMEM((TM + 8, D_CONV), F32)], compiler_params=_params(1),
    )(p, p, p, w)


def merge_out_ln(ycv, o2, p, bg, wbc, wbm_p, wo, xprev, gp, bp, g, b, *, name):
    tp = ycv.shape[0]

    def body(y_ref, o_ref, gc_ref, gm_ref, bg_ref, wbc_ref, wbm_ref, wo_ref, xp_ref, gp_ref, bp_ref, g_ref, b_ref,
             bc_ref, bm_ref, mg_ref, xh_ref, rs_ref, hb_ref):
        bc = jnp.dot(y_ref[...], wbc_ref[...], preferred_element_type=F32)
        bm = jnp.dot(o_ref[...], wbm_ref[...], preferred_element_type=F32)
        bgv = bg_ref[...]
        mg = jax.nn.sigmoid(gc_ref[...] + bgv[0:1]) * bc + jax.nn.sigmoid(gm_ref[...] + bgv[1:2]) * bm
        mgb = mg.astype(BF)
        bc_ref[...] = bc.astype(BF)
        bm_ref[...] = bm.astype(BF)
        mg_ref[...] = mgb
        mix = jnp.dot(mgb, wo_ref[...], preferred_element_type=F32)
        hprev = xp_ref[...] * gp_ref[...] + bp_ref[...]
        _ln_store(ALPHA * hprev + mix, g_ref, b_ref, xh_ref, rs_ref, hb_ref)

    def rows(n, col=0):
        return pl.BlockSpec((TMH, n), functools.partial(lambda i, col: (i, col), col=col))

    def whole(r, c):
        return pl.BlockSpec((r, c), lambda i: (0, 0))

    ln_specs, ln_shapes = _ln_out(tp, TMH)
    return _pcall(
        body, name=name, grid=(tp // TMH,),
        in_specs=[rows(D_CONV), rows(MLA_HEADS * KV_PAD), rows(D_MODEL, 2), rows(D_MODEL, 3), whole(2, D_MODEL),
                  whole(D_CONV, D_MODEL), whole(MLA_HEADS * KV_PAD, D_MODEL), whole(D_MODEL, D_MODEL), rows(D_MODEL)]
        + [_row_vec(D_MODEL)] * 4,
        out_specs=[rows(D_MODEL)] * 3 + ln_specs, out_shape=[_sds((tp, D_MODEL), BF)] * 3 + ln_shapes,
        compiler_params=_params(1),
    )(ycv, o2, p, p, bg, wbc, wbm_p, wo, xprev, gp, bp, g, b)


def loss_grad(xh, g, b, target_p, n_real, *, name):
    tp = xh.shape[0]

    def body(x_ref, g_ref, b_ref, t_ref, dy_ref, loss_ref):
        i = pl.program_id(0)

        @pl.when(i == 0)
        def _():
            loss_ref[...] = jnp.zeros_like(loss_ref)

        row = i * TM + lax.broadcasted_iota(jnp.int32, (TM, 1), 0)
        real = (row >= N_META) & (row < N_META + n_real)
        diff = jnp.where(real, x_ref[...] * g_ref[...] + b_ref[...] - t_ref[...], 0.0)
        dy_ref[...] = diff * (1.0 / D_MODEL)
        loss_ref[...] += 0.5 / D_MODEL * jnp.sum(diff * diff)

    return _pcall(
        body, name=name, grid=(tp // TM,),
        in_specs=[pl.BlockSpec((TM, D_MODEL), lambda i: (i, 0)), _row_vec(D_MODEL), _row_vec(D_MODEL),
                  pl.BlockSpec((TM, D_MODEL), lambda i: (i, 0))],
        out_specs=[pl.BlockSpec((TM, D_MODEL), lambda i: (i, 0)), pl.BlockSpec((8, LANES), lambda i: (0, 0))],
        out_shape=[_sds((tp, D_MODEL), F32), _sds((8, LANES), F32)], compiler_params=_params(1),
    )(xh, g, b, target_p)


def ln_bwd(dh, xh, rstd, g, *, branch_scale, name):
    tp = dh.shape[0]

    def body(dh_ref, xh_ref, rs_ref, g_ref, dz_ref, dzb_ref, dg_ref, db_ref):
        i = pl.program_id(0)

        @pl.when(i == 0)
        def _():
            dg_ref[...] = jnp.zeros_like(dg_ref)
            db_ref[...] = jnp.zeros_like(db_ref)

        dy = dh_ref[...]
        xhat = xh_ref[...]
        dg_ref[...] += jnp.sum(dy * xhat, axis=0, keepdims=True)
        db_ref[...] += jnp.sum(dy, axis=0, keepdims=True)
        dxh = dy * g_ref[...]
        m1 = jnp.mean(dxh, axis=-1, keepdims=True)
        m2 = jnp.mean(dxh * xhat, axis=-1, keepdims=True)
        dz = rs_ref[...] * (dxh - m1 - xhat * m2)
        dz_ref[...] = dz
        dzb_ref[...] = (branch_scale * dz).astype(BF)

    rows = pl.BlockSpec((TM, D_MODEL), lambda i: (i, 0))
    return _pcall(
        body, name=name, grid=(tp // TM,),
        in_specs=[rows, rows, pl.BlockSpec((TM, 1), lambda i: (i, 0)), _row_vec(D_MODEL)],
        out_specs=[rows, rows, _row_vec(D_MODEL), _row_vec(D_MODEL)],
        out_shape=[_sds((tp, D_MODEL), F32), _sds((tp, D_MODEL), BF), _sds((1, D_MODEL), F32), _sds((1, D_MODEL), F32)],
        compiler_params=_params(1),
    )(dh, xh, rstd, g)


def ffn_down_bwd(dzb, wd, gate, up, *, name):
    tp = dzb.shape[0]
    tn = D_FF // 2

    def body(dz_ref, wd_ref, g_ref, u_ref, dg_ref, du_ref):
        da = lax.dot_general(dz_ref[...], wd_ref[...], NT, preferred_element_type=F32)
        g = g_ref[...].astype(F32)
        u = u_ref[...].astype(F32)
        sg = jax.nn.sigmoid(g)
        dg_ref[...] = (da * u * sg * (1.0 + g * (1.0 - sg))).astype(BF)
        du_ref[...] = (da * g * sg).astype(BF)

    blk = pl.BlockSpec((TM, tn), lambda i, j: (i, j))
    return _pcall(
        body, name=name, grid=(tp // TM, D_FF // tn),
        in_specs=[pl.BlockSpec((TM, D_MODEL), lambda i, j: (i, 0)), pl.BlockSpec((tn, D_MODEL), lambda i, j: (j, 0)), blk, blk],
        out_specs=[blk, blk], out_shape=[_sds((tp, D_FF), BF)] * 2, compiler_params=_params(2),
    )(dzb, wd, gate, up)


def merge_bwd(dzb, wo, bc, bm, p, bg, wbc, wbm_p, o2, *, name):
    tp = dzb.shape[0]
    nh = MLA_HEADS

    def body(dz_ref, wo_ref, bc_ref, bm_ref, gc_ref, gm_ref, bg_ref, wbc_ref, wbm_ref, o_ref,
             dbc_ref, dbm_ref, dgg_ref, dy_ref, do_ref, dl_ref, dbg_ref):
        i = pl.program_id(0)

        @pl.when(i == 0)
        def _():
            dbg_ref[...] = jnp.zeros_like(dbg_ref)

        dmg = lax.dot_general(dz_ref[...], wo_ref[...], NT, preferred_element_type=F32)
        bgv = bg_ref[...]
        sc = jax.nn.sigmoid(gc_ref[...] + bgv[0:1])
        sm = jax.nn.sigmoid(gm_ref[...] + bgv[1:2])
        dbc = (dmg * sc).astype(BF)
        dbm = (dmg * sm).astype(BF)
        dgc = dmg * bc_ref[...].astype(F32) * sc * (1.0 - sc)
        dgm = dmg * bm_ref[...].astype(F32) * sm * (1.0 - sm)
        dbc_ref[...] = dbc
        dbm_ref[...] = dbm
        dgg_ref[...] = jnp.concatenate([dgc, dgm], axis=1).astype(BF)
        dbg_ref[...] += jnp.concatenate([jnp.sum(dgc, axis=0, keepdims=True), jnp.sum(dgm, axis=0, keepdims=True)], axis=0)
        dy_ref[...] = lax.dot_general(dbc, wbc_ref[...], NT, preferred_element_type=F32)
        do = lax.dot_general(dbm, wbm_ref[...], NT, preferred_element_type=F32)
        do_ref[...] = do.astype(BF)
        prod = do * o_ref[...].astype(F32)
        parts = []
        for h in range(nh):
            d = jnp.sum(prod[:, h * KV_PAD:(h + 1) * KV_PAD], axis=1, keepdims=True)
            parts.append(jnp.broadcast_to(d, (TMH, LANES)))
        dl_ref[...] = jnp.concatenate(parts, axis=1)

    def rows(n, col=0):
        return pl.BlockSpec((TMH, n), functools.partial(lambda i, col: (i, col), col=col))

    def whole(r, c):
        return pl.BlockSpec((r, c), lambda i: (0, 0))

    return _pcall(
        body, name=name, grid=(tp // TMH,),
        in_specs=[rows(D_MODEL), whole(D_MODEL, D_MODEL), rows(D_MODEL), rows(D_MODEL), rows(D_MODEL, 2), rows(D_MODEL, 3),
                  whole(2, D_MODEL), whole(D_CONV, D_MODEL), whole(nh * KV_PAD, D_MODEL), rows(nh * KV_PAD)],
        out_specs=[rows(D_MODEL), rows(D_MODEL), rows(2 * D_MODEL), rows(D_CONV), rows(nh * KV_PAD), rows(nh * LANES),
                   whole(2, D_MODEL)],
        out_shape=[_sds((tp, D_MODEL), BF), _sds((tp, D_MODEL), BF), _sds((tp, 2 * D_MODEL), BF), _sds((tp, D_CONV), F32),
                   _sds((tp, nh * KV_PAD), BF), _sds((tp, nh * LANES), F32), _sds((2, D_MODEL), F32)],
        compiler_params=_params(1),
    )(dzb, wo, bc, bm, p, p, bg, wbc, wbm_p, o2)


def attn_bwd(q2, kv, kr, do2, lse, dl, *, name):
    tp = q2.shape[0]
    nh = MLA_HEADS
    nb = tp // TM
    rep = TM // LANES

    def body(q_ref, kv_ref, kr_ref, do_ref, lse_ref, dl_ref, dq_ref, dkv_ref, dkr_ref, dq_acc):
        qi = pl.program_id(1)

        @pl.when(qi == 0)
        def _():
            dkv_ref[...] = jnp.zeros_like(dkv_ref)
            dkr_ref[...] = jnp.zeros_like(dkr_ref)

        dq_acc[...] = jnp.zeros_like(dq_acc)

        def step(k, diagonal):
            rows = _key_rows(k)
            q = q_ref[...]
            do = do_ref[...]
            kvb = kv_ref[rows, :]
            s, k2 = _scores(q, kvb, kr_ref[rows, :], diagonal)
            p = jnp.exp(s - jnp.tile(lse_ref[...], (1, rep)))
            dp = lax.dot_general(do, kvb, NT, preferred_element_type=F32)
            dsb = (p * (dp - jnp.tile(dl_ref[...], (1, rep)))).astype(BF)
            dk2 = lax.dot_general(dsb, q, TN, preferred_element_type=F32)
            dkv_ref[rows, :] += lax.dot_general(p.astype(BF), do, TN, preferred_element_type=F32) + dk2[:, :KV_PAD]
            dkr_ref[rows, :] += dk2[:, KV_PAD:KV_PAD + LANES]
            dq_acc[...] += jnp.dot(dsb, k2, preferred_element_type=F32)

        def below_diagonal(k, carry):
            step(k, False)
            return carry

        lax.fori_loop(0, qi, below_diagonal, 0)
        step(qi, True)
        dq_ref[...] = dq_acc[...]

    def qrow(n):
        return pl.BlockSpec((TM, n), lambda h, qi: (qi, h))

    def head(n):
        return pl.BlockSpec((tp, n), lambda h, qi: (0, h))

    return _pcall(
        body, name=name, grid=(nh, nb),
        in_specs=[qrow(Q_PAD), head(KV_PAD), pl.BlockSpec((tp, LANES), lambda h, qi: (0, 0)), qrow(KV_PAD), qrow(LANES), qrow(LANES)],
        out_specs=[qrow(Q_PAD), head(KV_PAD), head(LANES)],
        out_shape=[_sds((tp, nh * Q_PAD), F32), _sds((tp, nh * KV_PAD), F32), _sds((tp, nh * LANES), F32)],
        scratch_shapes=[pltpu.VMEM((TM, Q_PAD), F32)], compiler_params=_params(2),
    )(q2, kv, kr, do2, lse, dl)


def _rms_bwd(x, g, dy):
    r = lax.rsqrt(jnp.mean(x * x, axis=-1, keepdims=True) + RMS_EPS)
    gy = dy * g
    dx = r * gy - x * (r * r * r) * jnp.mean(x * gy, axis=-1, keepdims=True)
    return dx, jnp.sum(dy * x * r, axis=0, keepdims=True)


def mla_prep_bwd(dq2, dkv, dkr, p, gq, gkv, wuq_p, wukv, tabs_bwd, *, name):
    tp = dq2.shape[0]
    nh = MLA_HEADS

    def body(dq_ref, dkv_ref, dkr_ref, cq_ref, ckv_ref, gq_ref, gkv_ref, wuq_ref, wukv_ref,
             cq_t, s1q_t, s2q_t, ck_t, s1k_t, s2k_t, dqb_ref, dsm_ref, dgq_ref, dgkv_ref):
        i = pl.program_id(0)

        @pl.when(i == 0)
        def _():
            dgq_ref[...] = jnp.zeros_like(dgq_ref)
            dgkv_ref[...] = jnp.zeros_like(dgkv_ref)

        dqb = _rope(dq_ref[...], cq_t[...], s1q_t[...], s2q_t[...], nh).astype(BF)
        dqb_ref[...] = dqb
        dcqn = lax.dot_general(dqb, wuq_ref[...], NT, preferred_element_type=F32)
        dcq, dgq = _rms_bwd(cq_ref[...], gq_ref[...], dcqn)
        dckvn = lax.dot_general(dkv_ref[...].astype(BF), wukv_ref[...], NT, preferred_element_type=F32)
        dckv, dgkv = _rms_bwd(ckv_ref[...], gkv_ref[...], dckvn)
        dkr_heads = dkr_ref[...]
        dkr_sum = dkr_heads[:, :LANES]
        for h in range(1, nh):
            dkr_sum = dkr_sum + dkr_heads[:, h * LANES:(h + 1) * LANES]
        dkr = _rope(dkr_sum, ck_t[...], s1k_t[...], s2k_t[...], 1)
        dsm_ref[...] = jnp.concatenate([dcq, dckv, dkr], axis=1)
        dgq_ref[...] += dgq
        dgkv_ref[...] += dgkv

    def rows(n, col=0):
        return pl.BlockSpec((TMH, n), functools.partial(lambda i, col: (i, col), col=col))

    return _pcall(
        body, name=name, grid=(tp // TMH,),
        in_specs=[rows(nh * Q_PAD), rows(nh * KV_PAD), rows(nh * LANES), rows(Q_LORA, 1536 // Q_LORA), rows(KV_LORA, 1792 // KV_LORA),
                  _row_vec(Q_LORA), _row_vec(KV_LORA),
                  pl.BlockSpec((Q_LORA, nh * Q_PAD), lambda i: (0, 0)), pl.BlockSpec((KV_LORA, nh * KV_PAD), lambda i: (0, 0)),
                  rows(Q_PAD), rows(Q_PAD), rows(Q_PAD), rows(LANES), rows(LANES), rows(LANES)],
        out_specs=[rows(nh * Q_PAD), rows(Q_LORA + KV_LORA + LANES), _row_vec(Q_LORA), _row_vec(KV_LORA)],
        out_shape=[_sds((tp, nh * Q_PAD), BF), _sds((tp, Q_LORA + KV_LORA + LANES), F32), _sds((1, Q_LORA), F32),
                   _sds((1, KV_LORA), F32)],
        compiler_params=_params(1),
    )(dq2, dkv, dkr, p, p, gq, gkv, wuq_p, wukv, *tabs_bwd)


def conv_bwd(dy, p, conv, w, *, name):
    tp = dy.shape[0]
    nb = tp // TM

    def body(dy_ref, b_ref, c_ref, h_ref, cv_ref, w_ref, dp_ref, dw0_ref, dw1_ref, dw2_ref, dbuf):
        i = pl.program_id(0)

        @pl.when(i == 0)
        def _():
            dbuf[TM:TM + 8, :] = jnp.zeros((8, D_CONV), F32)
            dw0_ref[...] = jnp.zeros_like(dw0_ref)
            dw1_ref[...] = jnp.zeros_like(dw1_ref)
            dw2_ref[...] = jnp.zeros_like(dw2_ref)

        dyv = dy_ref[...]
        c = c_ref[...]
        hh = h_ref[...]
        dconv = dyv * b_ref[...]
        dbuf[0:TM, :] = dconv
        d1 = dbuf[pl.ds(1, TM), :]
        d2 = dbuf[pl.ds(2, TM), :]
        w_all = w_ref[...]
        de = w_all[2:3] * dconv + w_all[1:2] * d1 + w_all[0:1] * d2
        e = c * hh
        dp_ref[...] = jnp.concatenate([dyv * cv_ref[...].astype(F32), de * hh, de * c], axis=1).astype(BF)
        dw0_ref[...] += jnp.sum(d2 * e, axis=0, keepdims=True)
        dw1_ref[...] += jnp.sum(d1 * e, axis=0, keepdims=True)
        dw2_ref[...] += jnp.sum(dconv * e, axis=0, keepdims=True)
        dbuf[TM:TM + 8, :] = dbuf[0:8, :]

    def col(j):
        return pl.BlockSpec((TM, D_CONV), functools.partial(lambda i, j: (nb - 1 - i, j), j=j))

    return _pcall(
        body, name=name, grid=(nb,),
        in_specs=[col(0), col(0), col(1), col(2), col(0), pl.BlockSpec((3, D_CONV), lambda i: (0, 0))],
        out_specs=[pl.BlockSpec((TM, 3 * D_CONV), lambda i: (nb - 1 - i, 0))] + [_row_vec(D_CONV)] * 3,
        out_shape=[_sds((tp, 3 * D_CONV), BF)] + [_sds((1, D_CONV), F32)] * 3,
        scratch_shapes=[pltpu.VMEM((TM + 8, D_CONV), F32)], compiler_params=_params(1),
    )(dy, p, p, p, conv, w)


def adamw(w, g, m, v, *, name):
    r, c = w.shape
    tr = r
    for cand in (256, 128, 64, 32, 16, 8):
        if r % cand == 0 and r > cand:
            tr = cand
            break

    def body(w_ref, g_ref, m_ref, v_ref, d_ref, nm_ref, nv_ref):
        gv = g_ref[...]
        nm = ADAM_B1 * m_ref[...] + (1.0 - ADAM_B1) * gv
        nv = ADAM_B2 * v_ref[...] + (1.0 - ADAM_B2) * (gv * gv)
        m_hat = nm / (1.0 - ADAM_B1 ** ADAM_STEP)
        v_hat = nv / (1.0 - ADAM_B2 ** ADAM_STEP)
        d_ref[...] = -ADAM_LR * (m_hat / (jnp.sqrt(v_hat) + ADAM_EPS) + ADAM_WD * w_ref[...])
        nm_ref[...] = nm
        nv_ref[...] = nv

    blk = pl.BlockSpec((tr, c), lambda i: (i, 0))
    return _pcall(
        body, name=name, grid=(r // tr,), in_specs=[blk] * 4, out_specs=[blk] * 3,
        out_shape=[_sds((r, c), F32)] * 3, compiler_params=_params(1),
    )(w, g, m, v)


HBM_SPEC = pl.BlockSpec(memory_space=pltpu.HBM)


def _place():
    return lax.axis_index("x"), lax.axis_index("y"), lax.axis_index("c")


def _other_chips(x, y):
    return [(1 - x, y), (x, 1 - y), (1 - x, 1 - y)]


def gather_chips(shards, *, name):
    n = len(shards)

    def body(*refs):
        x_refs, o_refs = refs[:n], refs[2 * n:3 * n]
        send_sems, recv_sems = refs[3 * n:]
        x, y, c = _place()
        me = 2 * x + y
        chips = _other_chips(x, y)
        mine = pl.ds(c, 1)
        other = pl.ds(1 - c, 1)

        def copy(sem, src, dst, to):
            return pltpu.make_async_remote_copy(src_ref=src, dst_ref=dst, send_sem=send_sems.at[sem], recv_sem=recv_sems.at[sem],
                                                device_id=to, device_id_type=MESH)

        sends = []
        for i in range(n):
            for k, (px, py) in enumerate(chips):
                sends.append(copy(6 * i + k, x_refs[i].at[mine], o_refs[i].at[me, mine], (px, py, c)))
        for cp in sends:
            cp.start()
        passed = []
        for k, (px, py) in enumerate(chips):
            j = 2 * px + py
            for i in range(n):
                copy(6 * i + k, x_refs[i].at[mine], o_refs[i].at[j, mine], (px, py, c)).wait_recv()
                fwd = copy(6 * i + 3 + k, o_refs[i].at[j, mine], o_refs[i].at[j, mine], (x, y, 1 - c))
                fwd.start()
                passed.append(fwd)
        for k, (px, py) in enumerate(chips):
            j = 2 * px + py
            for i in range(n):
                copy(6 * i + 3 + k, o_refs[i].at[j, other], o_refs[i].at[j, other], (x, y, 1 - c)).wait_recv()
        for cp in sends + passed:
            cp.wait_send()

    prefilled = [jnp.broadcast_to(s[None], (4,) + s.shape) for s in shards]
    return _pcall(
        body, name=name, in_specs=[HBM_SPEC] * (2 * n), out_specs=[HBM_SPEC] * n,
        out_shape=[_sds(p.shape, p.dtype) for p in prefilled], input_output_aliases={n + i: i for i in range(n)},
        scratch_shapes=[pltpu.SemaphoreType.DMA((6 * n,)), pltpu.SemaphoreType.DMA((6 * n,))],
    )(*shards, *prefilled)


def pair_exchange(gs, *, name):
    n = len(gs)

    def body(*refs):
        g_refs, o_refs = refs[:n], refs[n:2 * n]
        send_sems, recv_sems = refs[2 * n:]
        x, y, c = _place()
        cps = [pltpu.make_async_remote_copy(src_ref=g_refs[i].at[1 - c], dst_ref=o_refs[i], send_sem=send_sems.at[i],
                                            recv_sem=recv_sems.at[i], device_id=(x, y, 1 - c), device_id_type=MESH)
               for i in range(n)]
        for cp in cps:
            cp.start()
        for cp in cps:
            cp.wait()

    return _pcall(
        body, name=name, in_specs=[HBM_SPEC] * n, out_specs=[HBM_SPEC] * n, out_shape=[_sds(g.shape[1:], g.dtype) for g in gs],
        scratch_shapes=[pltpu.SemaphoreType.DMA((n,)), pltpu.SemaphoreType.DMA((n,))],
    )(*gs)


def _comm_rows(a, b, itemsize):
    return a // 2 if a * b * itemsize > (3 << 19) and a % 16 == 0 else a


def pair_add(g, s1, c_idx, *, name):
    _, n, a, b = g.shape
    ta = _comm_rows(a, b, 2)

    def body(c_ref, g_ref, s_ref, o_ref):
        o_ref[...] = (g_ref[0].astype(F32) + s_ref[...].astype(F32)).astype(o_ref.dtype)

    grid_spec = pltpu.PrefetchScalarGridSpec(
        num_scalar_prefetch=1, grid=(n, a // ta),
        in_specs=[pl.BlockSpec((1, 1, ta, b), lambda j, i, c_ref: (c_ref[0], j, i, 0)),
                  pl.BlockSpec((1, ta, b), lambda j, i, c_ref: (j, i, 0))],
        out_specs=pl.BlockSpec((1, ta, b), lambda j, i, c_ref: (j, i, 0)),
    )
    return _pcall(body, name=name, grid_spec=grid_spec, out_shape=_sds((n, a, b), g.dtype), compiler_params=_params(2))(
        c_idx, g, s1)


def chip_scatter(pss, *, name):
    n = len(pss)

    def body(*refs):
        p_refs, o_refs = refs[:n], refs[2 * n:3 * n]
        send_sems, recv_sems = refs[3 * n:]
        x, y, c = _place()
        me = 2 * x + y
        chips = _other_chips(x, y)

        def copy(i, k, j_src, j_dst, to):
            return pltpu.make_async_remote_copy(src_ref=p_refs[i].at[j_src], dst_ref=o_refs[i].at[j_dst],
                                                send_sem=send_sems.at[3 * i + k], recv_sem=recv_sems.at[3 * i + k],
                                                device_id=to, device_id_type=MESH)

        sends = [copy(i, k, 2 * px + py, me, (px, py, c)) for i in range(n) for k, (px, py) in enumerate(chips)]
        for cp in sends:
            cp.start()
        for i in range(n):
            for k, (px, py) in enumerate(chips):
                copy(i, k, me, 2 * px + py, (px, py, c)).wait_recv()
        for cp in sends:
            cp.wait_send()

    xi, yi, _ = _place()
    own = jnp.arange(4)[:, None, None] == 2 * xi + yi
    prefilled = [jnp.where(own, p, jnp.zeros_like(p)) for p in pss]
    return _pcall(
        body, name=name, in_specs=[HBM_SPEC] * (2 * n), out_specs=[HBM_SPEC] * n, out_shape=[_sds(p.shape, p.dtype) for p in pss],
        input_output_aliases={n + i: i for i in range(n)},
        scratch_shapes=[pltpu.SemaphoreType.DMA((3 * n,)), pltpu.SemaphoreType.DMA((3 * n,))],
    )(*pss, *prefilled)


def sum_chunks(s2, *, name):
    n, a, b = s2.shape
    ta = _comm_rows(a, b, 4)

    def body(s_ref, o_ref):
        acc = s_ref[0].astype(F32)
        for j in range(1, n):
            acc = acc + s_ref[j].astype(F32)
        o_ref[...] = acc

    return _pcall(
        body, name=name, grid=(a // ta,), in_specs=[pl.BlockSpec((n, ta, b), lambda i: (0, i, 0))],
        out_specs=pl.BlockSpec((ta, b), lambda i: (i, 0)), out_shape=_sds((a, b), F32), compiler_params=_params(1),
    )(s2)


def pair_gather(rcs, *, name):
    n = len(rcs)

    def body(*refs):
        r_refs, o_refs = refs[:n], refs[2 * n:3 * n]
        send_sems, recv_sems = refs[3 * n:]
        x, y, c = _place()

        def copy(i, layer):
            return pltpu.make_async_remote_copy(src_ref=r_refs[i], dst_ref=o_refs[i].at[layer], send_sem=send_sems.at[i],
                                                recv_sem=recv_sems.at[i], device_id=(x, y, 1 - c), device_id_type=MESH)

        sends = [copy(i, c) for i in range(n)]
        for cp in sends:
            cp.start()
        for i in range(n):
            copy(i, 1 - c).wait_recv()
        for cp in sends:
            cp.wait_send()

    prefilled = [jnp.broadcast_to(r[None], (DEPTH,) + r.shape) for r in rcs]
    return _pcall(
        body, name=name, in_specs=[HBM_SPEC] * (2 * n), out_specs=[HBM_SPEC] * n,
        out_shape=[_sds(p.shape, p.dtype) for p in prefilled], input_output_aliases={n + i: i for i in range(n)},
        scratch_shapes=[pltpu.SemaphoreType.DMA((n,)), pltpu.SemaphoreType.DMA((n,))],
    )(*rcs, *prefilled)


def exchange_small(arrs, *, reduce, name):
    n = len(arrs)

    def body(*refs):
        v_refs, o_refs = refs[:n], refs[n:2 * n]
        bufs = refs[2 * n:3 * n] if reduce else o_refs
        send_sems, recv_sems = refs[-2:]
        x, y, c = _place()
        me = 4 * x + 2 * y + c
        for i in range(n):
            bufs[i][me] = v_refs[i][...]

        def peer(k):
            dx, dy, dc = (k >> 2) & 1, (k >> 1) & 1, k & 1
            return (1 - x if dx else x, 1 - y if dy else y, 1 - c if dc else c)

        def copy(i, k, slot):
            return pltpu.make_async_remote_copy(src_ref=v_refs[i], dst_ref=bufs[i].at[slot], send_sem=send_sems.at[7 * i + k - 1],
                                                recv_sem=recv_sems.at[7 * i + k - 1], device_id=peer(k), device_id_type=MESH)

        sends = [copy(i, k, me) for i in range(n) for k in range(1, 8)]
        for cp in sends:
            cp.start()
        for i in range(n):
            for k in range(1, 8):
                px, py, pc = peer(k)
                copy(i, k, 4 * px + 2 * py + pc).wait_recv()
        for cp in sends:
            cp.wait_send()
        if reduce:
            for i in range(n):
                acc = bufs[i][0]
                for d in range(1, 8):
                    acc = acc + bufs[i][d]
                o_refs[i][...] = acc

    vmem = pl.BlockSpec(memory_space=pltpu.VMEM)
    stacked = [(8,) + a.shape for a in arrs]
    return _pcall(
        body, name=name, in_specs=[vmem] * n, out_specs=[vmem] * n,
        out_shape=[_sds(a.shape if reduce else s, F32) for a, s in zip(arrs, stacked)],
        scratch_shapes=([pltpu.VMEM(s, F32) for s in stacked] if reduce else [])
        + [pltpu.SemaphoreType.DMA((7 * n,)), pltpu.SemaphoreType.DMA((7 * n,))],
    )(*arrs)


def _pad_rows(n, mult):
    return -(-n // mult) * mult


def _chip_major(g, b):
    return g.reshape(g.shape[0], 4, b).transpose(1, 0, 2)


def _rope_tables(tp):
    inv_freq = 1.0 / (ROPE_BASE ** (jnp.arange(0, QK_ROPE, 2, dtype=F32) / QK_ROPE))
    ang = jnp.arange(tp, dtype=F32)[:, None] * inv_freq[None, :]
    cos, sin = jnp.cos(ang), jnp.sin(ang)
    one = lambda n: jnp.ones((tp, n), F32)
    zero = lambda n: jnp.zeros((tp, n), F32)
    cq = jnp.concatenate([one(128), cos, cos, one(96)], axis=1)
    s1q = jnp.concatenate([zero(144), sin, zero(96)], axis=1)
    s2q = jnp.concatenate([zero(128), -sin, zero(112)], axis=1)
    ck = jnp.concatenate([cos, cos, zero(96)], axis=1)
    s1k = jnp.concatenate([zero(16), sin, zero(96)], axis=1)
    s2k = jnp.concatenate([-sin, zero(112)], axis=1)
    fwd = (cq * ATT_SCALE, s1q * ATT_SCALE, s2q * ATT_SCALE, ck, s1k, s2k)
    bwd = (cq * ATT_SCALE, -s1q * ATT_SCALE, -s2q * ATT_SCALE, ck, -s1k, -s2k)
    return fwd, bwd


def _pad_w_in(w):
    return jnp.concatenate([w[:, :1952], jnp.zeros((w.shape[0], 96), w.dtype), w[:, 1952:]], axis=1)


def _pad_w_uq(w):
    w = w.reshape(Q_LORA, MLA_HEADS, QK_NOPE + QK_ROPE)
    z = lambda n: jnp.zeros((Q_LORA, MLA_HEADS, n), w.dtype)
    return jnp.concatenate([w[..., :QK_NOPE], z(64), w[..., QK_NOPE:], z(96)], axis=-1).reshape(Q_LORA, MLA_HEADS * Q_PAD)


def _unpad_w_uq(w):
    w = w.reshape(Q_LORA, MLA_HEADS, Q_PAD)
    return jnp.concatenate([w[..., :QK_NOPE], w[..., 128:128 + QK_ROPE]], axis=-1).reshape(Q_LORA, MLA_HEADS * (QK_NOPE + QK_ROPE))


def _pad_w_br_mla(w):
    w = w.reshape(MLA_HEADS, V_HEAD, D_MODEL)
    return jnp.concatenate([jnp.zeros_like(w), w], axis=1).reshape(MLA_HEADS * KV_PAD, D_MODEL)


def _unpad_w_br_mla(w):
    return w.reshape(MLA_HEADS, KV_PAD, D_MODEL)[:, V_HEAD:].reshape(MLA_HEADS * V_HEAD, D_MODEL)


def _layer_fwd(l, st, xprev, gp, bp, hb, w, tabs):
    ln_g, ln_b = w["ln_g"], w["ln_b"]
    lg = lambda k: ln_g[l, k][None]
    lb = lambda k: ln_b[l, k][None]
    s = {}
    s["x0"], s["gp0"], s["bp0"], s["hb0"] = xprev, gp, bp, hb
    s["g1"], s["u1"], s["a1"] = ffn_up(hb, w["ffn1_w_up"][l], name="ffn_up")
    s["xh1"], s["rs1"], s["hb1"] = down_ln(s["a1"], w["ffn1_w_down"][l], xprev, gp, bp, lg(0), lb(0), name="ffn_down_ln")
    s["p"] = mm_rows([(s["hb1"], w["mix_w_in"][l], False, 0)], D_IN_PAD, name="mix_in", tn=1024)
    gq, gkv = w["q_norm_g"][l][None], w["kv_norm_g"][l][None]
    s["cqn"], s["ckvn"], s["q2"], s["kv"], s["kr"] = mla_prep(s["p"], gq, gkv, w["w_uq"][l], w["w_ukv"][l], tabs, name="mla_prep")
    s["o2"], s["lse"] = attn_fwd(s["q2"], s["kv"], s["kr"], name="attn_fwd")
    s["ycv"], s["conv"] = conv_fwd(s["p"], w["conv_w"][l], name="conv_fwd")
    s["bc"], s["bm"], s["mg"], s["xh2"], s["rs2"], s["hb2"] = merge_out_ln(
        s["ycv"], s["o2"], s["p"], w["mix_b_gate"][l], w["w_br_conv"][l], w["w_br_mla"][l], w["w_o"][l],
        s["xh1"], lg(0), lb(0), lg(1), lb(1), name="merge_out_ln")
    s["g2"], s["u2"], s["a2"] = ffn_up(s["hb2"], w["ffn2_w_up"][l], name="ffn_up")
    s["xh3"], s["rs3"], s["hb3"] = down_ln(s["a2"], w["ffn2_w_down"][l], s["xh2"], lg(1), lb(1), lg(2), lb(2), name="ffn_down_ln")
    st.append(s)
    return s["xh3"], lg(2), lb(2), s["hb3"]


def _ffn_bwd(l, dh, w_up, w_down, ln_gain, hb_in, gate, up, act, xh, rs, dst_up, dst_down):
    dz, dzb, dgam, dbet = ln_bwd(dh, xh, rs, ln_gain, branch_scale=0.5, name="ln_bwd")
    d_wd = tn_mm(act, dzb, tm=D_FF // 2, name="dw_down", shard=("rows", D_FF // 4), layer=l, dst=dst_down)
    dgate, dup = ffn_down_bwd(dzb, w_down, gate, up, name="ffn_down_bwd")
    d_w = tn_mm(hb_in, dgate, tm=512, name="dw_up", shard=("cols", D_FF // 2), layer=l, slot0=0, dst=dst_up)
    d_w = tn_mm(hb_in, dup, tm=512, name="dw_up", shard=("cols", D_FF // 2), layer=l, slot0=2, dst=d_w)
    dh_in = mm_rows([(dgate, w_up, True, 0), (dup, w_up, True, 1)], D_MODEL, name="ffn_up_bwd", tn=512, addend=dz, add_scale=ALPHA)
    return dh_in, d_w, d_wd, dgam, dbet


def _layer_bwd(l, s, dh, w, tabs_bwd, dst):
    ln_g = w["ln_g"]
    lg = lambda k: ln_g[l, k][None]
    g = {}
    dh, dst["ffn2_w_up"], dst["ffn2_w_down"], dg2, db2 = _ffn_bwd(
        l, dh, w["ffn2_w_up"][l], w["ffn2_w_down"][l], lg(2), s["hb2"], s["g2"], s["u2"], s["a2"], s["xh3"], s["rs3"],
        dst.get("ffn2_w_up"), dst.get("ffn2_w_down"))
    dz, dzb, dg1, db1 = ln_bwd(dh, s["xh2"], s["rs2"], lg(1), branch_scale=1.0, name="ln_bwd")
    dst["w_o"] = tn_mm(s["mg"], dzb, tm=1024, name="dw_o", shard=("rows", D_MODEL // 4), layer=l, dst=dst.get("w_o"))
    dbc, dbm, dgg, dycv, do2, dl, g["mix_b_gate"] = merge_bwd(
        dzb, w["w_o"][l], s["bc"], s["bm"], s["p"], w["mix_b_gate"][l], w["w_br_conv"][l], w["w_br_mla"][l], s["o2"], name="merge_bwd")
    dst["w_br_conv"] = tn_mm(s["ycv"], dbc, tm=512, name="dw_br_conv", shard=("cols", D_MODEL // 4), layer=l,
                             dst=dst.get("w_br_conv"))
    g["w_br_mla"] = _chip_major(_unpad_w_br_mla(tn_mm(s["o2"], dbm, tm=1024, name="dw_br_mla")), D_MODEL // 4)
    dq2, dkv, dkr = attn_bwd(s["q2"], s["kv"], s["kr"], do2, s["lse"], dl, name="attn_bwd")
    gq, gkv = w["q_norm_g"][l][None], w["kv_norm_g"][l][None]
    dqb, dsm, g["q_norm_g"], g["kv_norm_g"] = mla_prep_bwd(dq2, dkv, dkr, s["p"], gq, gkv, w["w_uq"][l], w["w_ukv"][l], tabs_bwd,
                                                           name="mla_prep_bwd")
    g["w_uq"] = _chip_major(_unpad_w_uq(tn_mm(s["cqn"], dqb, tm=Q_LORA, name="dw_uq")), MLA_HEADS * (QK_NOPE + QK_ROPE) // 4)
    dst["w_ukv"] = tn_mm(s["ckvn"], dkv, tm=KV_LORA, name="dw_ukv", shard=("cols", MLA_HEADS * KV_PAD // 4), layer=l,
                         dst=dst.get("w_ukv"))
    dbch, dw0, dw1, dw2 = conv_bwd(dycv, s["p"], s["conv"], w["conv_w"][l], name="conv_bwd")
    g["conv_w"] = jnp.concatenate([dw0, dw1, dw2], axis=0)
    w_in = w["mix_w_in"][l]
    d_bch = tn_mm(s["hb1"], dbch, tm=512, name="dw_in_bch")
    d_sm = tn_mm(s["hb1"], dsm, tm=1024, name="dw_in_sm")
    d_gg = tn_mm(s["hb1"], dgg, tm=512, name="dw_in_gg")
    g["mix_w_in"] = _chip_major(jnp.concatenate([d_bch, d_sm[:, :1952 - 1536], d_gg], axis=1), D_IN // 4)
    dh = mm_rows([(dbch, w_in, True, 0), (dsm, w_in, True, 3), (dgg, w_in, True, 1)], D_MODEL, name="mix_in_bwd", tn=512,
                 addend=dz, add_scale=ALPHA)
    dh, dst["ffn1_w_up"], dst["ffn1_w_down"], dg0, db0 = _ffn_bwd(
        l, dh, w["ffn1_w_up"][l], w["ffn1_w_down"][l], lg(0), s["hb0"], s["g1"], s["u1"], s["a1"], s["xh1"], s["rs1"],
        dst.get("ffn1_w_up"), dst.get("ffn1_w_down"))
    g["ln_g"] = jnp.concatenate([dg0, dg1, dg2], axis=0)
    g["ln_b"] = jnp.concatenate([db0, db1, db2], axis=0)
    return dh, g


BIG = ("ffn1_w_up", "ffn1_w_down", "mix_w_in", "w_uq", "w_ukv", "w_br_conv", "w_br_mla", "w_o", "ffn2_w_up", "ffn2_w_down")
BIG_AXIS = (2, 1, 2, 2, 2, 2, 2, 1, 2, 1)
SMALL_SHARDED = ("meta_tokens", "mix_b_gate", "conv_w", "ln_g", "ln_b")
SMALL_REPLICATED = ("q_norm_g", "kv_norm_g")
WEIGHTS = ("meta_tokens", "ffn1_w_up", "ffn1_w_down", "mix_w_in", "mix_b_gate", "conv_w", "q_norm_g", "w_uq", "kv_norm_g", "w_ukv",
           "w_br_conv", "w_br_mla", "w_o", "ffn2_w_up", "ffn2_w_down", "ln_g", "ln_b")


def _view2d(a):
    return a.reshape(-1, a.shape[-1])


def _local_grads(x_row, target_row, w):
    seq = x_row.shape[0]
    t_real = N_META + seq
    tp = _pad_rows(t_real, TM)
    pad = tp - t_real
    h0 = jnp.concatenate([w["meta_tokens"], x_row, jnp.zeros((pad, D_MODEL), F32)], axis=0)
    target_p = jnp.concatenate([jnp.zeros((N_META, D_MODEL), F32), target_row, jnp.zeros((pad, D_MODEL), F32)], axis=0)
    tabs, tabs_bwd = _rope_tables(tp)
    ones = jnp.ones((1, D_MODEL), F32)
    zeros = jnp.zeros((1, D_MODEL), F32)
    saved = []
    cur = (h0, ones, zeros, h0.astype(BF))
    for l in range(DEPTH):
        cur = _layer_fwd(l, saved, *cur, w, tabs)
    dh, loss_acc = loss_grad(cur[0], cur[1], cur[2], target_p, seq, name="loss_grad")
    grads = [None] * DEPTH
    gfull = {}
    for l in reversed(range(DEPTH)):
        dh, grads[l] = _layer_bwd(l, saved[l], dh, w, tabs_bwd, gfull)
    for n in grads[0]:
        per_layer = [grads[l][n] for l in range(DEPTH)]
        gfull[n] = jnp.stack(per_layer) if n in BIG else jnp.concatenate(per_layer, axis=0)
    gfull["meta_tokens"] = dh[:N_META]
    return loss_acc, dh[N_META:t_real], gfull


def kernel(x, meta_tokens, ffn1_w_up, ffn1_w_down, mix_w_in, mix_b_gate, conv_w, q_norm_g, w_uq, kv_norm_g, w_ukv, w_br_conv, w_br_mla, w_o, ffn2_w_up, ffn2_w_down, ln_g, ln_b, loss_target, m_meta_tokens, m_ffn1_w_up, m_ffn1_w_down, m_mix_w_in, m_mix_b_gate, m_conv_w, m_q_norm_g, m_w_uq, m_kv_norm_g, m_w_ukv, m_w_br_conv, m_w_br_mla, m_w_o, m_ffn2_w_up, m_ffn2_w_down, m_ln_g, m_ln_b, v_meta_tokens, v_ffn1_w_up, v_ffn1_w_down, v_mix_w_in, v_mix_b_gate, v_conv_w, v_q_norm_g, v_w_uq, v_kv_norm_g, v_w_ukv, v_w_br_conv, v_w_br_mla, v_w_o, v_ffn2_w_up, v_ffn2_w_down, v_ln_g, v_ln_b):
    local = dict(meta_tokens=meta_tokens, ffn1_w_up=ffn1_w_up, ffn1_w_down=ffn1_w_down, mix_w_in=mix_w_in, mix_b_gate=mix_b_gate,
                 conv_w=conv_w, q_norm_g=q_norm_g, w_uq=w_uq, kv_norm_g=kv_norm_g, w_ukv=w_ukv, w_br_conv=w_br_conv,
                 w_br_mla=w_br_mla, w_o=w_o, ffn2_w_up=ffn2_w_up, ffn2_w_down=ffn2_w_down, ln_g=ln_g, ln_b=ln_b)
    mom_m = dict(zip(WEIGHTS, (m_meta_tokens, m_ffn1_w_up, m_ffn1_w_down, m_mix_w_in, m_mix_b_gate, m_conv_w, m_q_norm_g, m_w_uq,
                               m_kv_norm_g, m_w_ukv, m_w_br_conv, m_w_br_mla, m_w_o, m_ffn2_w_up, m_ffn2_w_down, m_ln_g, m_ln_b)))
    mom_v = dict(zip(WEIGHTS, (v_meta_tokens, v_ffn1_w_up, v_ffn1_w_down, v_mix_w_in, v_mix_b_gate, v_conv_w, v_q_norm_g, v_w_uq,
                               v_kv_norm_g, v_w_ukv, v_w_br_conv, v_w_br_mla, v_w_o, v_ffn2_w_up, v_ffn2_w_down, v_ln_g, v_ln_b)))
    xi, yi, ci = _place()
    chip = 2 * xi + yi

    gathered = gather_chips([local[n].astype(BF) for n in BIG], name="gather_weights")
    w = {n: jnp.concatenate([g[j] for j in range(4)], axis=ax) for n, ax, g in zip(BIG, BIG_AXIS, gathered)}
    stacked = exchange_small([_view2d(local[n]) for n in SMALL_SHARDED], reduce=False, name="gather_small")
    for n, st in zip(SMALL_SHARDED, stacked):
        full = jnp.concatenate([st[2 * j] for j in range(4)], axis=-1)
        w[n] = full.reshape(local[n].shape[:-1] + (full.shape[-1],))
    for n in SMALL_REPLICATED:
        w[n] = local[n]
    w["mix_w_in"] = jnp.stack([_pad_w_in(w["mix_w_in"][l]) for l in range(DEPTH)])
    w["w_uq"] = jnp.stack([_pad_w_uq(w["w_uq"][l]) for l in range(DEPTH)])
    w["w_br_mla"] = jnp.stack([_pad_w_br_mla(w["w_br_mla"][l]) for l in range(DEPTH)])

    loss_acc, grad_x, gfull = _local_grads(x[0], loss_target[0], w)
    grad_x = grad_x[None]

    c_idx = jnp.reshape(ci, (1,)).astype(jnp.int32)
    glist = [gfull[n] for n in BIG]
    from_sibling = pair_exchange(glist, name="rs_pair_exchange")
    pair_sums = [pair_add(g, s, c_idx, name="rs_pair_add") for g, s in zip(glist, from_sibling)]
    from_chips = chip_scatter(pair_sums, name="rs_chip_scatter")
    reduced = pair_gather([sum_chunks(r, name="rs_sum") for r in from_chips], name="rs_pair_gather")
    gshard = {n: r.reshape(local[n].shape) for n, r in zip(BIG, reduced)}

    small_names = SMALL_SHARDED + SMALL_REPLICATED
    small_red = exchange_small([gfull[n] for n in small_names] + [loss_acc], reduce=True, name="reduce_small")
    loss = small_red[-1][0, 0]
    for n, full in zip(small_names, small_red[:-1]):
        if n in SMALL_SHARDED:
            sh = local[n].shape[-1]
            full = lax.dynamic_slice_in_dim(full, chip * sh, sh, axis=1)
        gshard[n] = full.reshape(local[n].shape)

    delta, new_m, new_v = {}, {}, {}
    for n in WEIGHTS:
        shape = local[n].shape
        d, nm, nv = adamw(_view2d(local[n]), _view2d(gshard[n]), _view2d(mom_m[n]), _view2d(mom_v[n]), name="adamw")
        delta[n], new_m[n], new_v[n] = d.reshape(shape), nm.reshape(shape), nv.reshape(shape)
    return (loss, grad_x, *[gshard[n] for n in WEIGHTS], *[delta[n] for n in WEIGHTS], *[new_m[n] for n in WEIGHTS],
            *[new_v[n] for n in WEIGHTS])
```

```python
import functools

import jax
import jax.numpy as jnp
from jax import lax
from jax.experimental import pallas as pl
from jax.experimental.pallas import tpu as pltpu

F32 = jnp.float32
BF = jnp.bfloat16
MESH = pl.DeviceIdType.MESH

D_MODEL = 1024
DEPTH = 2
N_META = 16
D_CONV = 512
MLA_HEADS = 8
QK_NOPE = 64
QK_ROPE = 32
V_HEAD = 64
Q_LORA = 256
KV_LORA = 128
ROPE_BASE = 10000.0
NEG_INF = -1e30
D_FF = 2816
ALPHA = (2 * DEPTH) ** 0.25
LN_EPS = 1e-5
RMS_EPS = 1e-6
ATT_SCALE = (QK_NOPE + QK_ROPE) ** -0.5
LOG2E = 1.4426950408889634
LN2 = 0.6931471805599453
D_IN = 4000
D_IN_PAD = 4096
Q_PAD = 256
KV_PAD = 128

ADAM_LR = 0.001
ADAM_B1 = 0.9
ADAM_B2 = 0.999
ADAM_EPS = 1e-08
ADAM_WD = 0.01
ADAM_STEP = 10

TM = 768
TMH = 384
LANES = 128
COMM_COLS = 512
COMM_ROW_BLOCK = 1472
VMEM_LIMIT_BYTES = 50 * 1024 * 1024

NT = (((1,), (1,)), ((), ()))
TN = (((0,), (0,)), ((), ()))


def _pcall(body, **kw):
    return pl.pallas_call(body, **kw)


def _params(n_axes):
    return pltpu.CompilerParams(dimension_semantics=("arbitrary",) * n_axes, vmem_limit_bytes=VMEM_LIMIT_BYTES)


def _sds(shape, dtype):
    return jax.ShapeDtypeStruct(shape, dtype)


def mm_rows(pairs, n_out, *, name, tn=None, addend=None, add_scale=1.0, out_dtype=F32):
    tp = pairs[0][0].shape[0]
    tn = tn or n_out
    in_specs, args = [], []
    for a, b, nt, kb in pairs:
        k = a.shape[1]
        in_specs.append(pl.BlockSpec((TM, k), lambda i, j: (i, 0)))
        if nt:
            in_specs.append(pl.BlockSpec((tn, k), functools.partial(lambda i, j, kb: (j, kb), kb=kb)))
        else:
            in_specs.append(pl.BlockSpec((k, tn), lambda i, j: (0, j)))
        args += [a, b]
    if addend is not None:
        in_specs.append(pl.BlockSpec((TM, tn), lambda i, j: (i, j)))
        args.append(addend)
    n_pairs = len(pairs)
    nts = [p[2] for p in pairs]

    def body(*refs):
        o_ref = refs[-1]
        acc = None
        for p in range(n_pairs):
            a = refs[2 * p][...].astype(BF)
            b = refs[2 * p + 1][...]
            d = lax.dot_general(a, b, NT if nts[p] else (((1,), (0,)), ((), ())), preferred_element_type=F32)
            acc = d if acc is None else acc + d
        if addend is not None:
            acc = acc + add_scale * refs[2 * n_pairs][...]
        o_ref[...] = acc.astype(o_ref.dtype)

    return _pcall(
        body, name=name, grid=(tp // TM, n_out // tn), in_specs=in_specs,
        out_specs=pl.BlockSpec((TM, tn), lambda i, j: (i, j)), out_shape=_sds((tp, n_out), out_dtype),
        compiler_params=_params(2),
    )(*args)


def tn_mm(a, b, *, tm, name, out_dtype=BF, shard=None, layer=0, slot0=0, dst=None):
    tp, m = a.shape
    n = b.shape[1]
    nk = tp // TM
    if shard is None:
        pieces, out_block, out_index, out_full = 1, (tm, n), (lambda i, k: (i, 0)), (m, n)
    elif shard[0] == "cols":
        pieces = n // shard[1]
        out_block, out_full = (1, pieces, tm, shard[1]), (DEPTH, 4, m, shard[1])
        out_index = lambda i, k: (layer, slot0 // pieces, i, 0)
    else:
        pieces = tm // shard[1]
        out_block, out_full = (1, pieces, shard[1], n), (DEPTH, 4, m // 4, n)
        out_index = lambda i, k: (layer, i, 0, 0)

    def body(a_ref, b_ref, *rest):
        o_ref, acc_ref = rest[-2], rest[-1]
        k = pl.program_id(1)

        @pl.when(k == 0)
        def _():
            acc_ref[...] = jnp.zeros_like(acc_ref)

        acc_ref[...] += lax.dot_general(a_ref[...].astype(BF), b_ref[...].astype(BF), TN, preferred_element_type=F32)

        @pl.when(k == nk - 1)
        def _():
            if shard is None:
                o_ref[...] = acc_ref[...].astype(o_ref.dtype)
            elif shard[0] == "cols":
                for j in range(pieces):
                    o_ref[0, j] = acc_ref[:, j * shard[1]:(j + 1) * shard[1]].astype(o_ref.dtype)
            else:
                for j in range(pieces):
                    o_ref[0, j] = acc_ref[j * shard[1]:(j + 1) * shard[1], :].astype(o_ref.dtype)

    in_specs = [pl.BlockSpec((TM, tm), lambda i, k: (k, i)), pl.BlockSpec((TM, n), lambda i, k: (k, 0))]
    args = [a, b]
    aliases = {}
    if dst is not None:
        in_specs.append(pl.BlockSpec(memory_space=pl.ANY))
        args.append(dst)
        aliases = {2: 0}
    return _pcall(
        body, name=name, grid=(m // tm, nk), in_specs=in_specs, out_specs=pl.BlockSpec(out_block, out_index),
        out_shape=_sds(out_full, out_dtype), input_output_aliases=aliases,
        scratch_shapes=[pltpu.VMEM((tm, n), F32)], compiler_params=_params(2),
    )(*args)


def _ln_store(z, g_ref, b_ref, xh_ref, rs_ref, hb_ref):
    mu = jnp.mean(z, axis=-1, keepdims=True)
    zc = z - mu
    var = jnp.mean(zc * zc, axis=-1, keepdims=True)
    rstd = lax.rsqrt(var + LN_EPS)
    xh = zc * rstd
    xh_ref[...] = xh
    rs_ref[...] = rstd
    hb_ref[...] = (xh * g_ref[...] + b_ref[...]).astype(BF)


def _ln_out(tp, tm=TM):
    specs = [pl.BlockSpec((tm, D_MODEL), lambda i: (i, 0)), pl.BlockSpec((tm, 1), lambda i: (i, 0)),
             pl.BlockSpec((tm, D_MODEL), lambda i: (i, 0))]
    shapes = [_sds((tp, D_MODEL), F32), _sds((tp, 1), F32), _sds((tp, D_MODEL), BF)]
    return specs, shapes


def _row_vec(n):
    return pl.BlockSpec((1, n), lambda i: (0, 0))


def ffn_up(hb, wup, *, name):
    tp = hb.shape[0]
    tn = D_FF // 2
    nj = D_FF // tn

    def body(h_ref, wg_ref, wu_ref, g_ref, u_ref, a_ref):
        h = h_ref[...]
        g = jnp.dot(h, wg_ref[...], preferred_element_type=F32)
        u = jnp.dot(h, wu_ref[...], preferred_element_type=F32)
        g_ref[...] = g.astype(BF)
        u_ref[...] = u.astype(BF)
        a_ref[...] = (g * jax.nn.sigmoid(g) * u).astype(BF)

    blk = pl.BlockSpec((TM, tn), lambda i, j: (i, j))
    return _pcall(
        body, name=name, grid=(tp // TM, nj),
        in_specs=[pl.BlockSpec((TM, D_MODEL), lambda i, j: (i, 0)), pl.BlockSpec((D_MODEL, tn), lambda i, j: (0, j)),
                  pl.BlockSpec((D_MODEL, tn), lambda i, j: (0, j + nj))],
        out_specs=[blk, blk, blk], out_shape=[_sds((tp, D_FF), BF)] * 3, compiler_params=_params(2),
    )(hb, wup, wup)


def down_ln(a, wd, xprev, gp, bp, g, b, *, name):
    tp = a.shape[0]

    def body(a_ref, wd_ref, xp_ref, gp_ref, bp_ref, g_ref, b_ref, xh_ref, rs_ref, hb_ref):
        f = jnp.dot(a_ref[...], wd_ref[...], preferred_element_type=F32)
        hprev = xp_ref[...] * gp_ref[...] + bp_ref[...]
        _ln_store(ALPHA * hprev + 0.5 * f, g_ref, b_ref, xh_ref, rs_ref, hb_ref)

    out_specs, out_shape = _ln_out(tp)
    return _pcall(
        body, name=name, grid=(tp // TM,),
        in_specs=[pl.BlockSpec((TM, D_FF), lambda i: (i, 0)), pl.BlockSpec((D_FF, D_MODEL), lambda i: (0, 0)),
                  pl.BlockSpec((TM, D_MODEL), lambda i: (i, 0))] + [_row_vec(D_MODEL)] * 4,
        out_specs=out_specs, out_shape=out_shape, compiler_params=_params(1),
    )(a, wd, xprev, gp, bp, g, b)


def _rope(x, c, s1, s2, reps):
    n = x.shape[1]
    if reps > 1:
        c, s1, s2 = (jnp.tile(t, (1, reps)) for t in (c, s1, s2))
    return x * c + pltpu.roll(x, 16, 1) * s1 + pltpu.roll(x, n - 16, 1) * s2


def _rms(x, g):
    r = lax.rsqrt(jnp.mean(x * x, axis=-1, keepdims=True) + RMS_EPS)
    return x * r * g, r


def mla_prep(p, gq, gkv, wuq_p, wukv, tabs, *, name):
    tp = p.shape[0]
    nh = MLA_HEADS

    def body(cq_ref, ckv_ref, kr_ref, gq_ref, gkv_ref, wuq_ref, wukv_ref, cq_t, s1q_t, s2q_t, ck_t, s1k_t, s2k_t,
             cqn_ref, ckvn_ref, q2_ref, kv_ref, krr_ref):
        cqn, _ = _rms(cq_ref[...], gq_ref[...])
        ckvn, _ = _rms(ckv_ref[...], gkv_ref[...])
        cqn = cqn.astype(BF)
        ckvn = ckvn.astype(BF)
        cqn_ref[...] = cqn
        ckvn_ref[...] = ckvn
        q = jnp.dot(cqn, wuq_ref[...], preferred_element_type=F32)
        q2_ref[...] = _rope(q, cq_t[...], s1q_t[...], s2q_t[...], nh).astype(BF)
        kv_ref[...] = jnp.dot(ckvn, wukv_ref[...], preferred_element_type=F32).astype(BF)
        krr_ref[...] = _rope(kr_ref[...], ck_t[...], s1k_t[...], s2k_t[...], 1).astype(BF)

    def rows(n, col=0):
        return pl.BlockSpec((TMH, n), functools.partial(lambda i, col: (i, col), col=col))

    return _pcall(
        body, name=name, grid=(tp // TMH,),
        in_specs=[rows(Q_LORA, 1536 // Q_LORA), rows(KV_LORA, 1792 // KV_LORA), rows(LANES, 1920 // LANES),
                  _row_vec(Q_LORA), _row_vec(KV_LORA),
                  pl.BlockSpec((Q_LORA, nh * Q_PAD), lambda i: (0, 0)), pl.BlockSpec((KV_LORA, nh * KV_PAD), lambda i: (0, 0)),
                  rows(Q_PAD), rows(Q_PAD), rows(Q_PAD), rows(LANES), rows(LANES), rows(LANES)],
        out_specs=[rows(Q_LORA), rows(KV_LORA), rows(nh * Q_PAD), rows(nh * KV_PAD), rows(LANES)],
        out_shape=[_sds((tp, Q_LORA), BF), _sds((tp, KV_LORA), BF), _sds((tp, nh * Q_PAD), BF),
                   _sds((tp, nh * KV_PAD), BF), _sds((tp, LANES), BF)],
        compiler_params=_params(1),
    )(p, p, p, gq, gkv, wuq_p, wukv, *tabs)


def _causal_mask(s):
    qpos = lax.broadcasted_iota(jnp.int32, (TM, TM), 0)
    kpos = lax.broadcasted_iota(jnp.int32, (TM, TM), 1)
    return jnp.where(kpos <= qpos, s, NEG_INF)


def _key_rows(k):
    return pl.ds(pl.multiple_of(k * TM, TM), TM)


def _pipelined_key_blocks(n, prefetch, process):
    prefetch(0, 0)

    def pair(j, carry):
        prefetch(2 * j + 1, 1)
        process(2 * j, 0, False)
        prefetch(2 * j + 2, 0)
        process(2 * j + 1, 1, False)
        return carry

    lax.fori_loop(0, n // 2, pair, 0)

    @pl.when(n % 2 == 1)
    def _():
        prefetch(n, 1)
        process(n - 1, 0, False)
        process(n, 1, True)

    @pl.when(n % 2 == 0)
    def _():
        process(n, 0, True)


def attn_fwd(q2, kv, kr, *, name):
    tp = q2.shape[0]
    nh = MLA_HEADS
    nb = tp // TM
    rep = TM // LANES

    def body(q_ref, kv_ref, kr_ref, o_ref, lse_ref, m_ref, l_ref, acc_ref, s0_ref, s1_ref):
        qi = pl.program_id(1)
        s_refs = (s0_ref, s1_ref)
        m_ref[...] = jnp.full_like(m_ref, NEG_INF)
        l_ref[...] = jnp.zeros_like(l_ref)
        acc_ref[...] = jnp.zeros_like(acc_ref)

        def prefetch(k, slot):
            k2 = jnp.concatenate([kv_ref[_key_rows(k), :], kr_ref[_key_rows(k), :]], axis=1)
            s_refs[slot][...] = lax.dot_general(q_ref[...], k2, NT, preferred_element_type=F32)

        def process(k, slot, diagonal):
            s = s_refs[slot][...]
            if diagonal:
                s = _causal_mask(s)
            m_prev = m_ref[...]
            m_new = jnp.maximum(m_prev, jnp.max(s, axis=1, keepdims=True))
            alpha = jnp.exp2(m_prev - m_new)
            p = jnp.exp2(s - jnp.tile(m_new, (1, rep)))
            l_ref[...] = alpha * l_ref[...] + jnp.sum(p, axis=1, keepdims=True)
            acc_ref[...] = alpha * acc_ref[...] + jnp.dot(p.astype(BF), kv_ref[_key_rows(k), :], preferred_element_type=F32)
            m_ref[...] = m_new

        _pipelined_key_blocks(qi, prefetch, process)
        o_ref[...] = (acc_ref[...] / l_ref[...]).astype(BF)
        lse_ref[...] = m_ref[...] + jnp.log2(l_ref[...])

    return _pcall(
        body, name=name, grid=(nh, nb),
        in_specs=[pl.BlockSpec((TM, Q_PAD), lambda h, qi: (qi, h)), pl.BlockSpec((tp, KV_PAD), lambda h, qi: (0, h)),
                  pl.BlockSpec((tp, LANES), lambda h, qi: (0, 0))],
        out_specs=[pl.BlockSpec((TM, KV_PAD), lambda h, qi: (qi, h)), pl.BlockSpec((TM, LANES), lambda h, qi: (qi, h))],
        out_shape=[_sds((tp, nh * KV_PAD), BF), _sds((tp, nh * LANES), F32)],
        scratch_shapes=[pltpu.VMEM((TM, LANES), F32)] * 3 + [pltpu.VMEM((TM, TM), F32)] * 2, compiler_params=_params(2),
    )(q2, kv, kr)


def conv_fwd(p, w, *, name):
    tp = p.shape[0]

    def body(b_ref, c_ref, h_ref, w_ref, y_ref, cv_ref, ebuf):
        i = pl.program_id(0)

        @pl.when(i == 0)
        def _():
            ebuf[0:8, :] = jnp.zeros((8, D_CONV), F32)

        e = c_ref[...] * h_ref[...]
        ebuf[8:8 + TM, :] = e
        w_all = w_ref[...]
        conv = w_all[0:1] * ebuf[pl.ds(6, TM), :] + w_all[1:2] * ebuf[pl.ds(7, TM), :] + w_all[2:3] * e
        cv_ref[...] = conv.astype(BF)
        y_ref[...] = (b_ref[...] * conv).astype(BF)
        ebuf[0:8, :] = ebuf[TM:TM + 8, :]

    def col(j):
        return pl.BlockSpec((TM, D_CONV), functools.partial(lambda i, j: (i, j), j=j))

    return _pcall(
        body, name=name, grid=(tp // TM,),
        in_specs=[col(0), col(1), col(2), pl.BlockSpec((3, D_CONV), lambda i: (0, 0))],
        out_specs=[col(0), col(0)], out_shape=[_sds((tp, D_CONV), BF)] * 2,
        scratch_shapes=[pltpu.VMEM((TM + 8, D_CONV), F32)], compiler_params=_params(1),
    )(p, p, p, w)


def merge_out_ln(ycv, o2, p, bg, wbc, wbm_p, wo, xprev, gp, bp, g, b, *, name):
    tp = ycv.shape[0]

    def body(y_ref, o_ref, gc_ref, gm_ref, bg_ref, wbc_ref, wbm_ref, wo_ref, xp_ref, gp_ref, bp_ref, g_ref, b_ref,
             bc_ref, bm_ref, mg_ref, xh_ref, rs_ref, hb_ref):
        bc = jnp.dot(y_ref[...], wbc_ref[...], preferred_element_type=F32)
        bm = jnp.dot(o_ref[...], wbm_ref[...], preferred_element_type=F32)
        bgv = bg_ref[...]
        mg = jax.nn.sigmoid(gc_ref[...] + bgv[0:1]) * bc + jax.nn.sigmoid(gm_ref[...] + bgv[1:2]) * bm
        mgb = mg.astype(BF)
        bc_ref[...] = bc.astype(BF)
        bm_ref[...] = bm.astype(BF)
        mg_ref[...] = mgb
        mix = jnp.dot(mgb, wo_ref[...], preferred_element_type=F32)
        hprev = xp_ref[...] * gp_ref[...] + bp_ref[...]
        _ln_store(ALPHA * hprev + mix, g_ref, b_ref, xh_ref, rs_ref, hb_ref)

    def rows(n, col=0):
        return pl.BlockSpec((TMH, n), functools.partial(lambda i, col: (i, col), col=col))

    def whole(r, c):
        return pl.BlockSpec((r, c), lambda i: (0, 0))

    ln_specs, ln_shapes = _ln_out(tp, TMH)
    return _pcall(
        body, name=name, grid=(tp // TMH,),
        in_specs=[rows(D_CONV), rows(MLA_HEADS * KV_PAD), rows(D_MODEL, 2), rows(D_MODEL, 3), whole(2, D_MODEL),
                  whole(D_CONV, D_MODEL), whole(MLA_HEADS * KV_PAD, D_MODEL), whole(D_MODEL, D_MODEL), rows(D_MODEL)]
        + [_row_vec(D_MODEL)] * 4,
        out_specs=[rows(D_MODEL)] * 3 + ln_specs, out_shape=[_sds((tp, D_MODEL), BF)] * 3 + ln_shapes,
        compiler_params=_params(1),
    )(ycv, o2, p, p, bg, wbc, wbm_p, wo, xprev, gp, bp, g, b)


def loss_grad(xh, g, b, target_p, n_real, *, name):
    tp = xh.shape[0]

    def body(x_ref, g_ref, b_ref, t_ref, dy_ref, loss_ref):
        i = pl.program_id(0)

        @pl.when(i == 0)
        def _():
            loss_ref[...] = jnp.zeros_like(loss_ref)

        row = i * TM + lax.broadcasted_iota(jnp.int32, (TM, 1), 0)
        real = (row >= N_META) & (row < N_META + n_real)
        diff = jnp.where(real, x_ref[...] * g_ref[...] + b_ref[...] - t_ref[...], 0.0)
        dy_ref[...] = diff * (1.0 / D_MODEL)
        loss_ref[...] += 0.5 / D_MODEL * jnp.sum(diff * diff)

    return _pcall(
        body, name=name, grid=(tp // TM,),
        in_specs=[pl.BlockSpec((TM, D_MODEL), lambda i: (i, 0)), _row_vec(D_MODEL), _row_vec(D_MODEL),
                  pl.BlockSpec((TM, D_MODEL), lambda i: (i, 0))],
        out_specs=[pl.BlockSpec((TM, D_MODEL), lambda i: (i, 0)), pl.BlockSpec((8, LANES), lambda i: (0, 0))],
        out_shape=[_sds((tp, D_MODEL), F32), _sds((8, LANES), F32)], compiler_params=_params(1),
    )(xh, g, b, target_p)


def ln_bwd(dh, xh, rstd, g, *, branch_scale, name):
    tp = dh.shape[0]

    def body(dh_ref, xh_ref, rs_ref, g_ref, dz_ref, dzb_ref, dg_ref, db_ref):
        i = pl.program_id(0)

        @pl.when(i == 0)
        def _():
            dg_ref[...] = jnp.zeros_like(dg_ref)
            db_ref[...] = jnp.zeros_like(db_ref)

        dy = dh_ref[...]
        xhat = xh_ref[...]
        dg_ref[...] += jnp.sum(dy * xhat, axis=0, keepdims=True)
        db_ref[...] += jnp.sum(dy, axis=0, keepdims=True)
        dxh = dy * g_ref[...]
        m1 = jnp.mean(dxh, axis=-1, keepdims=True)
        m2 = jnp.mean(dxh * xhat, axis=-1, keepdims=True)
        dz = rs_ref[...] * (dxh - m1 - xhat * m2)
        dz_ref[...] = dz
        dzb_ref[...] = (branch_scale * dz).astype(BF)

    rows = pl.BlockSpec((TM, D_MODEL), lambda i: (i, 0))
    return _pcall(
        body, name=name, grid=(tp // TM,),
        in_specs=[rows, rows, pl.BlockSpec((TM, 1), lambda i: (i, 0)), _row_vec(D_MODEL)],
        out_specs=[rows, rows, _row_vec(D_MODEL), _row_vec(D_MODEL)],
        out_shape=[_sds((tp, D_MODEL), F32), _sds((tp, D_MODEL), BF), _sds((1, D_MODEL), F32), _sds((1, D_MODEL), F32)],
        compiler_params=_params(1),
    )(dh, xh, rstd, g)


def ffn_down_bwd(dzb, wd, gate, up, *, name):
    tp = dzb.shape[0]
    tn = D_FF // 2

    def body(dz_ref, wd_ref, g_ref, u_ref, dg_ref, du_ref):
        da = lax.dot_general(dz_ref[...], wd_ref[...], NT, preferred_element_type=F32)
        g = g_ref[...].astype(F32)
        u = u_ref[...].astype(F32)
        sg = jax.nn.sigmoid(g)
        dg_ref[...] = (da * u * sg * (1.0 + g * (1.0 - sg))).astype(BF)
        du_ref[...] = (da * g * sg).astype(BF)

    blk = pl.BlockSpec((TM, tn), lambda i, j: (i, j))
    return _pcall(
        body, name=name, grid=(tp // TM, D_FF // tn),
        in_specs=[pl.BlockSpec((TM, D_MODEL), lambda i, j: (i, 0)), pl.BlockSpec((tn, D_MODEL), lambda i, j: (j, 0)), blk, blk],
        out_specs=[blk, blk], out_shape=[_sds((tp, D_FF), BF)] * 2, compiler_params=_params(2),
    )(dzb, wd, gate, up)


def merge_bwd(dzb, wo, bc, bm, p, bg, wbc, wbm_p, o2, *, name):
    tp = dzb.shape[0]
    nh = MLA_HEADS

    def body(dz_ref, wo_ref, bc_ref, bm_ref, gc_ref, gm_ref, bg_ref, wbc_ref, wbm_ref, o_ref,
             dbc_ref, dbm_ref, dgg_ref, dy_ref, do_ref, dl_ref, dbg_ref):
        i = pl.program_id(0)

        @pl.when(i == 0)
        def _():
            dbg_ref[...] = jnp.zeros_like(dbg_ref)

        dmg = lax.dot_general(dz_ref[...], wo_ref[...], NT, preferred_element_type=F32)
        bgv = bg_ref[...]
        sc = jax.nn.sigmoid(gc_ref[...] + bgv[0:1])
        sm = jax.nn.sigmoid(gm_ref[...] + bgv[1:2])
        dbc = (dmg * sc).astype(BF)
        dbm = (dmg * sm).astype(BF)
        dgc = dmg * bc_ref[...].astype(F32) * sc * (1.0 - sc)
        dgm = dmg * bm_ref[...].astype(F32) * sm * (1.0 - sm)
        dbc_ref[...] = dbc
        dbm_ref[...] = dbm
        dgg_ref[...] = jnp.concatenate([dgc, dgm], axis=1).astype(BF)
        dbg_ref[...] += jnp.concatenate([jnp.sum(dgc, axis=0, keepdims=True), jnp.sum(dgm, axis=0, keepdims=True)], axis=0)
        dy_ref[...] = lax.dot_general(dbc, wbc_ref[...], NT, preferred_element_type=F32)
        do = lax.dot_general(dbm, wbm_ref[...], NT, preferred_element_type=F32)
        do_ref[...] = do.astype(BF)
        prod = do * o_ref[...].astype(F32)
        parts = []
        for h in range(nh):
            d = jnp.sum(prod[:, h * KV_PAD:(h + 1) * KV_PAD], axis=1, keepdims=True)
            parts.append(jnp.broadcast_to(d, (TMH, LANES)))
        dl_ref[...] = jnp.concatenate(parts, axis=1)

    def rows(n, col=0):
        return pl.BlockSpec((TMH, n), functools.partial(lambda i, col: (i, col), col=col))

    def whole(r, c):
        return pl.BlockSpec((r, c), lambda i: (0, 0))

    return _pcall(
        body, name=name, grid=(tp // TMH,),
        in_specs=[rows(D_MODEL), whole(D_MODEL, D_MODEL), rows(D_MODEL), rows(D_MODEL), rows(D_MODEL, 2), rows(D_MODEL, 3),
                  whole(2, D_MODEL), whole(D_CONV, D_MODEL), whole(nh * KV_PAD, D_MODEL), rows(nh * KV_PAD)],
        out_specs=[rows(D_MODEL), rows(D_MODEL), rows(2 * D_MODEL), rows(D_CONV), rows(nh * KV_PAD), rows(nh * LANES),
                   whole(2, D_MODEL)],
        out_shape=[_sds((tp, D_MODEL), BF), _sds((tp, D_MODEL), BF), _sds((tp, 2 * D_MODEL), BF), _sds((tp, D_CONV), F32),
                   _sds((tp, nh * KV_PAD), BF), _sds((tp, nh * LANES), F32), _sds((2, D_MODEL), F32)],
        compiler_params=_params(1),
    )(dzb, wo, bc, bm, p, p, bg, wbc, wbm_p, o2)


def attn_bwd(q2, kv, kr, do2, lse, dl, *, name):
    tp = q2.shape[0]
    nh = MLA_HEADS
    nb = tp // TM
    rep = TM // LANES

    def body(q_ref, kv_ref, kr_ref, do_ref, lse_ref, dl_ref, dq_ref, dkv_ref, dkr_ref, dq_acc, s0_ref, s1_ref, dp0_ref, dp1_ref):
        qi = pl.program_id(1)
        s_refs, dp_refs = (s0_ref, s1_ref), (dp0_ref, dp1_ref)

        @pl.when(qi == 0)
        def _():
            dkv_ref[...] = jnp.zeros_like(dkv_ref)
            dkr_ref[...] = jnp.zeros_like(dkr_ref)

        dq_acc[...] = jnp.zeros_like(dq_acc)

        def prefetch(k, slot):
            kvb = kv_ref[_key_rows(k), :]
            k2 = jnp.concatenate([kvb, kr_ref[_key_rows(k), :]], axis=1)
            s_refs[slot][...] = lax.dot_general(q_ref[...], k2, NT, preferred_element_type=F32)
            dp_refs[slot][...] = lax.dot_general(do_ref[...], kvb, NT, preferred_element_type=F32)

        def process(k, slot, diagonal):
            rows = _key_rows(k)
            s = s_refs[slot][...]
            if diagonal:
                s = _causal_mask(s)
            p = jnp.exp2(s - jnp.tile(lse_ref[...], (1, rep)))
            dsb = (p * (dp_refs[slot][...] - jnp.tile(dl_ref[...], (1, rep)))).astype(BF)
            dk2 = lax.dot_general(dsb, q_ref[...], TN, preferred_element_type=F32) * LN2
            dkv_ref[rows, :] += lax.dot_general(p.astype(BF), do_ref[...], TN, preferred_element_type=F32) + dk2[:, :KV_PAD]
            dkr_ref[rows, :] += dk2[:, KV_PAD:KV_PAD + LANES]
            k2 = jnp.concatenate([kv_ref[rows, :], kr_ref[rows, :]], axis=1)
            dq_acc[...] += jnp.dot(dsb, k2, preferred_element_type=F32)

        _pipelined_key_blocks(qi, prefetch, process)
        dq_ref[...] = dq_acc[...]

    def qrow(n):
        return pl.BlockSpec((TM, n), lambda h, qi: (qi, h))

    def head(n):
        return pl.BlockSpec((tp, n), lambda h, qi: (0, h))

    return _pcall(
        body, name=name, grid=(nh, nb),
        in_specs=[qrow(Q_PAD), head(KV_PAD), pl.BlockSpec((tp, LANES), lambda h, qi: (0, 0)), qrow(KV_PAD), qrow(LANES), qrow(LANES)],
        out_specs=[qrow(Q_PAD), head(KV_PAD), head(LANES)],
        out_shape=[_sds((tp, nh * Q_PAD), F32), _sds((tp, nh * KV_PAD), F32), _sds((tp, nh * LANES), F32)],
        scratch_shapes=[pltpu.VMEM((TM, Q_PAD), F32)] + [pltpu.VMEM((TM, TM), F32)] * 4, compiler_params=_params(2),
    )(q2, kv, kr, do2, lse, dl)


def _rms_bwd(x, g, dy):
    r = lax.rsqrt(jnp.mean(x * x, axis=-1, keepdims=True) + RMS_EPS)
    gy = dy * g
    dx = r * gy - x * (r * r * r) * jnp.mean(x * gy, axis=-1, keepdims=True)
    return dx, jnp.sum(dy * x * r, axis=0, keepdims=True)


def mla_prep_bwd(dq2, dkv, dkr, p, gq, gkv, wuq_p, wukv, tabs_bwd, *, name):
    tp = dq2.shape[0]
    nh = MLA_HEADS

    def body(dq_ref, dkv_ref, dkr_ref, cq_ref, ckv_ref, gq_ref, gkv_ref, wuq_ref, wukv_ref,
             cq_t, s1q_t, s2q_t, ck_t, s1k_t, s2k_t, dqb_ref, dsm_ref, dgq_ref, dgkv_ref):
        i = pl.program_id(0)

        @pl.when(i == 0)
        def _():
            dgq_ref[...] = jnp.zeros_like(dgq_ref)
            dgkv_ref[...] = jnp.zeros_like(dgkv_ref)

        dqb = _rope(dq_ref[...], cq_t[...], s1q_t[...], s2q_t[...], nh).astype(BF)
        dqb_ref[...] = dqb
        dcqn = lax.dot_general(dqb, wuq_ref[...], NT, preferred_element_type=F32)
        dcq, dgq = _rms_bwd(cq_ref[...], gq_ref[...], dcqn)
        dckvn = lax.dot_general(dkv_ref[...].astype(BF), wukv_ref[...], NT, preferred_element_type=F32)
        dckv, dgkv = _rms_bwd(ckv_ref[...], gkv_ref[...], dckvn)
        dkr_heads = dkr_ref[...]
        dkr_sum = dkr_heads[:, :LANES]
        for h in range(1, nh):
            dkr_sum = dkr_sum + dkr_heads[:, h * LANES:(h + 1) * LANES]
        dkr = _rope(dkr_sum, ck_t[...], s1k_t[...], s2k_t[...], 1)
        dsm_ref[...] = jnp.concatenate([dcq, dckv, dkr], axis=1)
        dgq_ref[...] += dgq
        dgkv_ref[...] += dgkv

    def rows(n, col=0):
        return pl.BlockSpec((TMH, n), functools.partial(lambda i, col: (i, col), col=col))

    return _pcall(
        body, name=name, grid=(tp // TMH,),
        in_specs=[rows(nh * Q_PAD), rows(nh * KV_PAD), rows(nh * LANES), rows(Q_LORA, 1536 // Q_LORA), rows(KV_LORA, 1792 // KV_LORA),
                  _row_vec(Q_LORA), _row_vec(KV_LORA),
                  pl.BlockSpec((Q_LORA, nh * Q_PAD), lambda i: (0, 0)), pl.BlockSpec((KV_LORA, nh * KV_PAD), lambda i: (0, 0)),
                  rows(Q_PAD), rows(Q_PAD), rows(Q_PAD), rows(LANES), rows(LANES), rows(LANES)],
        out_specs=[rows(nh * Q_PAD), rows(Q_LORA + KV_LORA + LANES), _row_vec(Q_LORA), _row_vec(KV_LORA)],
        out_shape=[_sds((tp, nh * Q_PAD), BF), _sds((tp, Q_LORA + KV_LORA + LANES), F32), _sds((1, Q_LORA), F32),
                   _sds((1, KV_LORA), F32)],
        compiler_params=_params(1),
    )(dq2, dkv, dkr, p, p, gq, gkv, wuq_p, wukv, *tabs_bwd)


def conv_bwd(dy, p, conv, w, *, name):
    tp = dy.shape[0]
    nb = tp // TM

    def body(dy_ref, b_ref, c_ref, h_ref, cv_ref, w_ref, dp_ref, dw0_ref, dw1_ref, dw2_ref, dbuf):
        i = pl.program_id(0)

        @pl.when(i == 0)
        def _():
            dbuf[TM:TM + 8, :] = jnp.zeros((8, D_CONV), F32)
            dw0_ref[...] = jnp.zeros_like(dw0_ref)
            dw1_ref[...] = jnp.zeros_like(dw1_ref)
            dw2_ref[...] = jnp.zeros_like(dw2_ref)

        dyv = dy_ref[...]
        c = c_ref[...]
        hh = h_ref[...]
        dconv = dyv * b_ref[...]
        dbuf[0:TM, :] = dconv
        d1 = dbuf[pl.ds(1, TM), :]
        d2 = dbuf[pl.ds(2, TM), :]
        w_all = w_ref[...]
        de = w_all[2:3] * dconv + w_all[1:2] * d1 + w_all[0:1] * d2
        e = c * hh
        dp_ref[...] = jnp.concatenate([dyv * cv_ref[...].astype(F32), de * hh, de * c], axis=1).astype(BF)
        dw0_ref[...] += jnp.sum(d2 * e, axis=0, keepdims=True)
        dw1_ref[...] += jnp.sum(d1 * e, axis=0, keepdims=True)
        dw2_ref[...] += jnp.sum(dconv * e, axis=0, keepdims=True)
        dbuf[TM:TM + 8, :] = dbuf[0:8, :]

    def col(j):
        return pl.BlockSpec((TM, D_CONV), functools.partial(lambda i, j: (nb - 1 - i, j), j=j))

    return _pcall(
        body, name=name, grid=(nb,),
        in_specs=[col(0), col(0), col(1), col(2), col(0), pl.BlockSpec((3, D_CONV), lambda i: (0, 0))],
        out_specs=[pl.BlockSpec((TM, 3 * D_CONV), lambda i: (nb - 1 - i, 0))] + [_row_vec(D_CONV)] * 3,
        out_shape=[_sds((tp, 3 * D_CONV), BF)] + [_sds((1, D_CONV), F32)] * 3,
        scratch_shapes=[pltpu.VMEM((TM + 8, D_CONV), F32)], compiler_params=_params(1),
    )(dy, p, p, p, conv, w)


def adamw(w, g, m, v, *, name):
    r, c = w.shape
    tr = r
    for cand in (256, 128, 64, 32, 16, 8):
        if r % cand == 0 and r > cand:
            tr = cand
            break

    def body(w_ref, g_ref, m_ref, v_ref, d_ref, nm_ref, nv_ref):
        gv = g_ref[...]
        nm = ADAM_B1 * m_ref[...] + (1.0 - ADAM_B1) * gv
        nv = ADAM_B2 * v_ref[...] + (1.0 - ADAM_B2) * (gv * gv)
        m_hat = nm / (1.0 - ADAM_B1 ** ADAM_STEP)
        v_hat = nv / (1.0 - ADAM_B2 ** ADAM_STEP)
        d_ref[...] = -ADAM_LR * (m_hat / (jnp.sqrt(v_hat) + ADAM_EPS) + ADAM_WD * w_ref[...])
        nm_ref[...] = nm
        nv_ref[...] = nv

    blk = pl.BlockSpec((tr, c), lambda i: (i, 0))
    return _pcall(
        body, name=name, grid=(r // tr,), in_specs=[blk] * 4, out_specs=[blk] * 3,
        out_shape=[_sds((r, c), F32)] * 3, compiler_params=_params(1),
    )(w, g, m, v)


HBM_SPEC = pl.BlockSpec(memory_space=pltpu.HBM)


def _place():
    return lax.axis_index("x"), lax.axis_index("y"), lax.axis_index("c")


def _other_chips(x, y):
    return [(1 - x, y), (x, 1 - y), (1 - x, 1 - y)]


def gather_chips(shards, *, name):
    n = len(shards)

    def body(*refs):
        x_refs, o_refs = refs[:n], refs[2 * n:3 * n]
        send_sems, recv_sems = refs[3 * n:]
        x, y, c = _place()
        me = 2 * x + y
        chips = _other_chips(x, y)
        mine = pl.ds(c, 1)
        other = pl.ds(1 - c, 1)

        def copy(sem, src, dst, to):
            return pltpu.make_async_remote_copy(src_ref=src, dst_ref=dst, send_sem=send_sems.at[sem], recv_sem=recv_sems.at[sem],
                                                device_id=to, device_id_type=MESH)

        sends = []
        for i in range(n):
            for k, (px, py) in enumerate(chips):
                sends.append(copy(6 * i + k, x_refs[i].at[mine], o_refs[i].at[me, mine], (px, py, c)))
        for cp in sends:
            cp.start()
        passed = []
        for k, (px, py) in enumerate(chips):
            j = 2 * px + py
            for i in range(n):
                copy(6 * i + k, x_refs[i].at[mine], o_refs[i].at[j, mine], (px, py, c)).wait_recv()
                fwd = copy(6 * i + 3 + k, o_refs[i].at[j, mine], o_refs[i].at[j, mine], (x, y, 1 - c))
                fwd.start()
                passed.append(fwd)
        for k, (px, py) in enumerate(chips):
            j = 2 * px + py
            for i in range(n):
                copy(6 * i + 3 + k, o_refs[i].at[j, other], o_refs[i].at[j, other], (x, y, 1 - c)).wait_recv()
        for cp in sends + passed:
            cp.wait_send()

    prefilled = [jnp.broadcast_to(s[None], (4,) + s.shape) for s in shards]
    return _pcall(
        body, name=name, in_specs=[HBM_SPEC] * (2 * n), out_specs=[HBM_SPEC] * n,
        out_shape=[_sds(p.shape, p.dtype) for p in prefilled], input_output_aliases={n + i: i for i in range(n)},
        scratch_shapes=[pltpu.SemaphoreType.DMA((6 * n,)), pltpu.SemaphoreType.DMA((6 * n,))],
    )(*shards, *prefilled)


def pair_exchange(gs, *, name):
    n = len(gs)

    def body(*refs):
        g_refs, o_refs = refs[:n], refs[n:2 * n]
        send_sems, recv_sems = refs[2 * n:]
        x, y, c = _place()
        cps = [pltpu.make_async_remote_copy(src_ref=g_refs[i].at[1 - c], dst_ref=o_refs[i], send_sem=send_sems.at[i],
                                            recv_sem=recv_sems.at[i], device_id=(x, y, 1 - c), device_id_type=MESH)
               for i in range(n)]
        for cp in cps:
            cp.start()
        for cp in cps:
            cp.wait()

    return _pcall(
        body, name=name, in_specs=[HBM_SPEC] * n, out_specs=[HBM_SPEC] * n, out_shape=[_sds(g.shape[1:], g.dtype) for g in gs],
        scratch_shapes=[pltpu.SemaphoreType.DMA((n,)), pltpu.SemaphoreType.DMA((n,))],
    )(*gs)


def _comm_rows(a, b, itemsize):
    return a // 2 if a * b * itemsize > (3 << 19) and a % 16 == 0 else a


def pair_add(g, s1, c_idx, *, name):
    _, n, a, b = g.shape
    ta = _comm_rows(a, b, 2)

    def body(c_ref, g_ref, s_ref, o_ref):
        o_ref[...] = (g_ref[0].astype(F32) + s_ref[...].astype(F32)).astype(o_ref.dtype)

    grid_spec = pltpu.PrefetchScalarGridSpec(
        num_scalar_prefetch=1, grid=(n, a // ta),
        in_specs=[pl.BlockSpec((1, 1, ta, b), lambda j, i, c_ref: (c_ref[0], j, i, 0)),
                  pl.BlockSpec((1, ta, b), lambda j, i, c_ref: (j, i, 0))],
        out_specs=pl.BlockSpec((1, ta, b), lambda j, i, c_ref: (j, i, 0)),
    )
    return _pcall(body, name=name, grid_spec=grid_spec, out_shape=_sds((n, a, b), g.dtype), compiler_params=_params(2))(
        c_idx, g, s1)


def chip_scatter(pss, *, name):
    n = len(pss)

    def body(*refs):
        p_refs, o_refs = refs[:n], refs[2 * n:3 * n]
        send_sems, recv_sems = refs[3 * n:]
        x, y, c = _place()
        me = 2 * x + y
        chips = _other_chips(x, y)

        def copy(i, k, j_src, j_dst, to):
            return pltpu.make_async_remote_copy(src_ref=p_refs[i].at[j_src], dst_ref=o_refs[i].at[j_dst],
                                                send_sem=send_sems.at[3 * i + k], recv_sem=recv_sems.at[3 * i + k],
                                                device_id=to, device_id_type=MESH)

        sends = [copy(i, k, 2 * px + py, me, (px, py, c)) for i in range(n) for k, (px, py) in enumerate(chips)]
        for cp in sends:
            cp.start()
        for i in range(n):
            for k, (px, py) in enumerate(chips):
                copy(i, k, me, 2 * px + py, (px, py, c)).wait_recv()
        for cp in sends:
            cp.wait_send()

    xi, yi, _ = _place()
    own = jnp.arange(4)[:, None, None] == 2 * xi + yi
    prefilled = [jnp.where(own, p, jnp.zeros_like(p)) for p in pss]
    return _pcall(
        body, name=name, in_specs=[HBM_SPEC] * (2 * n), out_specs=[HBM_SPEC] * n, out_shape=[_sds(p.shape, p.dtype) for p in pss],
        input_output_aliases={n + i: i for i in range(n)},
        scratch_shapes=[pltpu.SemaphoreType.DMA((3 * n,)), pltpu.SemaphoreType.DMA((3 * n,))],
    )(*pss, *prefilled)


def sum_chunks(s2, *, name):
    n, a, b = s2.shape
    ta = _comm_rows(a, b, 4)

    def body(s_ref, o_ref):
        acc = s_ref[0].astype(F32)
        for j in range(1, n):
            acc = acc + s_ref[j].astype(F32)
        o_ref[...] = acc

    return _pcall(
        body, name=name, grid=(a // ta,), in_specs=[pl.BlockSpec((n, ta, b), lambda i: (0, i, 0))],
        out_specs=pl.BlockSpec((ta, b), lambda i: (i, 0)), out_shape=_sds((a, b), F32), compiler_params=_params(1),
    )(s2)


def pair_gather(rcs, *, name):
    n = len(rcs)

    def body(*refs):
        r_refs, o_refs = refs[:n], refs[2 * n:3 * n]
        send_sems, recv_sems = refs[3 * n:]
        x, y, c = _place()

        def copy(i, layer):
            return pltpu.make_async_remote_copy(src_ref=r_refs[i], dst_ref=o_refs[i].at[layer], send_sem=send_sems.at[i],
                                                recv_sem=recv_sems.at[i], device_id=(x, y, 1 - c), device_id_type=MESH)

        sends = [copy(i, c) for i in range(n)]
        for cp in sends:
            cp.start()
        for i in range(n):
            copy(i, 1 - c).wait_recv()
        for cp in sends:
            cp.wait_send()

    prefilled = [jnp.broadcast_to(r[None], (DEPTH,) + r.shape) for r in rcs]
    return _pcall(
        body, name=name, in_specs=[HBM_SPEC] * (2 * n), out_specs=[HBM_SPEC] * n,
        out_shape=[_sds(p.shape, p.dtype) for p in prefilled], input_output_aliases={n + i: i for i in range(n)},
        scratch_shapes=[pltpu.SemaphoreType.DMA((n,)), pltpu.SemaphoreType.DMA((n,))],
    )(*rcs, *prefilled)


def exchange_small(arrs, *, reduce, name):
    n = len(arrs)

    def body(*refs):
        v_refs, o_refs = refs[:n], refs[n:2 * n]
        bufs = refs[2 * n:3 * n] if reduce else o_refs
        send_sems, recv_sems = refs[-2:]
        x, y, c = _place()
        me = 4 * x + 2 * y + c
        for i in range(n):
            bufs[i][me] = v_refs[i][...]

        def peer(k):
            dx, dy, dc = (k >> 2) & 1, (k >> 1) & 1, k & 1
            return (1 - x if dx else x, 1 - y if dy else y, 1 - c if dc else c)

        def copy(i, k, slot):
            return pltpu.make_async_remote_copy(src_ref=v_refs[i], dst_ref=bufs[i].at[slot], send_sem=send_sems.at[7 * i + k - 1],
                                                recv_sem=recv_sems.at[7 * i + k - 1], device_id=peer(k), device_id_type=MESH)

        sends = [copy(i, k, me) for i in range(n) for k in range(1, 8)]
        for cp in sends:
            cp.start()
        for i in range(n):
            for k in range(1, 8):
                px, py, pc = peer(k)
                copy(i, k, 4 * px + 2 * py + pc).wait_recv()
        for cp in sends:
            cp.wait_send()
        if reduce:
            for i in range(n):
                acc = bufs[i][0]
                for d in range(1, 8):
                    acc = acc + bufs[i][d]
                o_refs[i][...] = acc

    vmem = pl.BlockSpec(memory_space=pltpu.VMEM)
    stacked = [(8,) + a.shape for a in arrs]
    return _pcall(
        body, name=name, in_specs=[vmem] * n, out_specs=[vmem] * n,
        out_shape=[_sds(a.shape if reduce else s, F32) for a, s in zip(arrs, stacked)],
        scratch_shapes=([pltpu.VMEM(s, F32) for s in stacked] if reduce else [])
        + [pltpu.SemaphoreType.DMA((7 * n,)), pltpu.SemaphoreType.DMA((7 * n,))],
    )(*arrs)


def _pad_rows(n, mult):
    return -(-n // mult) * mult


def _chip_major(g, b):
    return g.reshape(g.shape[0], 4, b).transpose(1, 0, 2)


def _rope_tables(tp):
    inv_freq = 1.0 / (ROPE_BASE ** (jnp.arange(0, QK_ROPE, 2, dtype=F32) / QK_ROPE))
    ang = jnp.arange(tp, dtype=F32)[:, None] * inv_freq[None, :]
    cos, sin = jnp.cos(ang), jnp.sin(ang)
    one = lambda n: jnp.ones((tp, n), F32)
    zero = lambda n: jnp.zeros((tp, n), F32)
    cq = jnp.concatenate([one(128), cos, cos, one(96)], axis=1)
    s1q = jnp.concatenate([zero(144), sin, zero(96)], axis=1)
    s2q = jnp.concatenate([zero(128), -sin, zero(112)], axis=1)
    ck = jnp.concatenate([cos, cos, zero(96)], axis=1)
    s1k = jnp.concatenate([zero(16), sin, zero(96)], axis=1)
    s2k = jnp.concatenate([-sin, zero(112)], axis=1)
    fwd = (cq * (ATT_SCALE * LOG2E), s1q * (ATT_SCALE * LOG2E), s2q * (ATT_SCALE * LOG2E), ck, s1k, s2k)
    bwd = (cq * ATT_SCALE, -s1q * ATT_SCALE, -s2q * ATT_SCALE, ck, -s1k, -s2k)
    return fwd, bwd


def _pad_w_in(w):
    return jnp.concatenate([w[:, :1952], jnp.zeros((w.shape[0], 96), w.dtype), w[:, 1952:]], axis=1)


def _pad_w_uq(w):
    w = w.reshape(Q_LORA, MLA_HEADS, QK_NOPE + QK_ROPE)
    z = lambda n: jnp.zeros((Q_LORA, MLA_HEADS, n), w.dtype)
    return jnp.concatenate([w[..., :QK_NOPE], z(64), w[..., QK_NOPE:], z(96)], axis=-1).reshape(Q_LORA, MLA_HEADS * Q_PAD)


def _unpad_w_uq(w):
    w = w.reshape(Q_LORA, MLA_HEADS, Q_PAD)
    return jnp.concatenate([w[..., :QK_NOPE], w[..., 128:128 + QK_ROPE]], axis=-1).reshape(Q_LORA, MLA_HEADS * (QK_NOPE + QK_ROPE))


def _pad_w_br_mla(w):
    w = w.reshape(MLA_HEADS, V_HEAD, D_MODEL)
    return jnp.concatenate([jnp.zeros_like(w), w], axis=1).reshape(MLA_HEADS * KV_PAD, D_MODEL)


def _unpad_w_br_mla(w):
    return w.reshape(MLA_HEADS, KV_PAD, D_MODEL)[:, V_HEAD:].reshape(MLA_HEADS * V_HEAD, D_MODEL)


def _layer_fwd(l, st, xprev, gp, bp, hb, w, tabs):
    ln_g, ln_b = w["ln_g"], w["ln_b"]
    lg = lambda k: ln_g[l, k][None]
    lb = lambda k: ln_b[l, k][None]
    s = {}
    s["x0"], s["gp0"], s["bp0"], s["hb0"] = xprev, gp, bp, hb
    s["g1"], s["u1"], s["a1"] = ffn_up(hb, w["ffn1_w_up"][l], name="ffn_up")
    s["xh1"], s["rs1"], s["hb1"] = down_ln(s["a1"], w["ffn1_w_down"][l], xprev, gp, bp, lg(0), lb(0), name="ffn_down_ln")
    s["p"] = mm_rows([(s["hb1"], w["mix_w_in"][l], False, 0)], D_IN_PAD, name="mix_in", tn=1024)
    gq, gkv = w["q_norm_g"][l][None], w["kv_norm_g"][l][None]
    s["cqn"], s["ckvn"], s["q2"], s["kv"], s["kr"] = mla_prep(s["p"], gq, gkv, w["w_uq"][l], w["w_ukv"][l], tabs, name="mla_prep")
    s["o2"], s["lse"] = attn_fwd(s["q2"], s["kv"], s["kr"], name="attn_fwd")
    s["ycv"], s["conv"] = conv_fwd(s["p"], w["conv_w"][l], name="conv_fwd")
    s["bc"], s["bm"], s["mg"], s["xh2"], s["rs2"], s["hb2"] = merge_out_ln(
        s["ycv"], s["o2"], s["p"], w["mix_b_gate"][l], w["w_br_conv"][l], w["w_br_mla"][l], w["w_o"][l],
        s["xh1"], lg(0), lb(0), lg(1), lb(1), name="merge_out_ln")
    s["g2"], s["u2"], s["a2"] = ffn_up(s["hb2"], w["ffn2_w_up"][l], name="ffn_up")
    s["xh3"], s["rs3"], s["hb3"] = down_ln(s["a2"], w["ffn2_w_down"][l], s["xh2"], lg(1), lb(1), lg(2), lb(2), name="ffn_down_ln")
    st.append(s)
    return s["xh3"], lg(2), lb(2), s["hb3"]


def _ffn_bwd(l, dh, w_up, w_down, ln_gain, hb_in, gate, up, act, xh, rs, dst_up, dst_down):
    dz, dzb, dgam, dbet = ln_bwd(dh, xh, rs, ln_gain, branch_scale=0.5, name="ln_bwd")
    d_wd = tn_mm(act, dzb, tm=D_FF // 2, name="dw_down", shard=("rows", D_FF // 4), layer=l, dst=dst_down)
    dgate, dup = ffn_down_bwd(dzb, w_down, gate, up, name="ffn_down_bwd")
    d_w = tn_mm(hb_in, dgate, tm=512, name="dw_up", shard=("cols", D_FF // 2), layer=l, slot0=0, dst=dst_up)
    d_w = tn_mm(hb_in, dup, tm=512, name="dw_up", shard=("cols", D_FF // 2), layer=l, slot0=2, dst=d_w)
    dh_in = mm_rows([(dgate, w_up, True, 0), (dup, w_up, True, 1)], D_MODEL, name="ffn_up_bwd", tn=512, addend=dz, add_scale=ALPHA)
    return dh_in, d_w, d_wd, dgam, dbet


def _layer_bwd(l, s, dh, w, tabs_bwd, dst):
    ln_g = w["ln_g"]
    lg = lambda k: ln_g[l, k][None]
    g = {}
    dh, dst["ffn2_w_up"], dst["ffn2_w_down"], dg2, db2 = _ffn_bwd(
        l, dh, w["ffn2_w_up"][l], w["ffn2_w_down"][l], lg(2), s["hb2"], s["g2"], s["u2"], s["a2"], s["xh3"], s["rs3"],
        dst.get("ffn2_w_up"), dst.get("ffn2_w_down"))
    dz, dzb, dg1, db1 = ln_bwd(dh, s["xh2"], s["rs2"], lg(1), branch_scale=1.0, name="ln_bwd")
    dst["w_o"] = tn_mm(s["mg"], dzb, tm=1024, name="dw_o", shard=("rows", D_MODEL // 4), layer=l, dst=dst.get("w_o"))
    dbc, dbm, dgg, dycv, do2, dl, g["mix_b_gate"] = merge_bwd(
        dzb, w["w_o"][l], s["bc"], s["bm"], s["p"], w["mix_b_gate"][l], w["w_br_conv"][l], w["w_br_mla"][l], s["o2"], name="merge_bwd")
    dst["w_br_conv"] = tn_mm(s["ycv"], dbc, tm=512, name="dw_br_conv", shard=("cols", D_MODEL // 4), layer=l,
                             dst=dst.get("w_br_conv"))
    g["w_br_mla"] = _chip_major(_unpad_w_br_mla(tn_mm(s["o2"], dbm, tm=1024, name="dw_br_mla")), D_MODEL // 4)
    dq2, dkv, dkr = attn_bwd(s["q2"], s["kv"], s["kr"], do2, s["lse"], dl, name="attn_bwd")
    gq, gkv = w["q_norm_g"][l][None], w["kv_norm_g"][l][None]
    dqb, dsm, g["q_norm_g"], g["kv_norm_g"] = mla_prep_bwd(dq2, dkv, dkr, s["p"], gq, gkv, w["w_uq"][l], w["w_ukv"][l], tabs_bwd,
                                                           name="mla_prep_bwd")
    g["w_uq"] = _chip_major(_unpad_w_uq(tn_mm(s["cqn"], dqb, tm=Q_LORA, name="dw_uq")), MLA_HEADS * (QK_NOPE + QK_ROPE) // 4)
    dst["w_ukv"] = tn_mm(s["ckvn"], dkv, tm=KV_LORA, name="dw_ukv", shard=("cols", MLA_HEADS * KV_PAD // 4), layer=l,
                         dst=dst.get("w_ukv"))
    dbch, dw0, dw1, dw2 = conv_bwd(dycv, s["p"], s["conv"], w["conv_w"][l], name="conv_bwd")
    g["conv_w"] = jnp.concatenate([dw0, dw1, dw2], axis=0)
    w_in = w["mix_w_in"][l]
    d_bch = tn_mm(s["hb1"], dbch, tm=512, name="dw_in_bch")
    d_sm = tn_mm(s["hb1"], dsm, tm=1024, name="dw_in_sm")
    d_gg = tn_mm(s["hb1"], dgg, tm=512, name="dw_in_gg")
    g["mix_w_in"] = _chip_major(jnp.concatenate([d_bch, d_sm[:, :1952 - 1536], d_gg], axis=1), D_IN // 4)
    dh = mm_rows([(dbch, w_in, True, 0), (dsm, w_in, True, 3), (dgg, w_in, True, 1)], D_MODEL, name="mix_in_bwd", tn=512,
                 addend=dz, add_scale=ALPHA)
    dh, dst["ffn1_w_up"], dst["ffn1_w_down"], dg0, db0 = _ffn_bwd(
        l, dh, w["ffn1_w_up"][l], w["ffn1_w_down"][l], lg(0), s["hb0"], s["g1"], s["u1"], s["a1"], s["xh1"], s["rs1"],
        dst.get("ffn1_w_up"), dst.get("ffn1_w_down"))
    g["ln_g"] = jnp.concatenate([dg0, dg1, dg2], axis=0)
    g["ln_b"] = jnp.concatenate([db0, db1, db2], axis=0)
    return dh, g


BIG = ("ffn1_w_up", "ffn1_w_down", "mix_w_in", "w_uq", "w_ukv", "w_br_conv", "w_br_mla", "w_o", "ffn2_w_up", "ffn2_w_down")
BIG_AXIS = (2, 1, 2, 2, 2, 2, 2, 1, 2, 1)
SMALL_SHARDED = ("meta_tokens", "mix_b_gate", "conv_w", "ln_g", "ln_b")
SMALL_REPLICATED = ("q_norm_g", "kv_norm_g")
WEIGHTS = ("meta_tokens", "ffn1_w_up", "ffn1_w_down", "mix_w_in", "mix_b_gate", "conv_w", "q_norm_g", "w_uq", "kv_norm_g", "w_ukv",
           "w_br_conv", "w_br_mla", "w_o", "ffn2_w_up", "ffn2_w_down", "ln_g", "ln_b")


def _view2d(a):
    return a.reshape(-1, a.shape[-1])


def _local_grads(x_row, target_row, w):
    seq = x_row.shape[0]
    t_real = N_META + seq
    tp = _pad_rows(t_real, TM)
    pad = tp - t_real
    h0 = jnp.concatenate([w["meta_tokens"], x_row, jnp.zeros((pad, D_MODEL), F32)], axis=0)
    target_p = jnp.concatenate([jnp.zeros((N_META, D_MODEL), F32), target_row, jnp.zeros((pad, D_MODEL), F32)], axis=0)
    tabs, tabs_bwd = _rope_tables(tp)
    ones = jnp.ones((1, D_MODEL), F32)
    zeros = jnp.zeros((1, D_MODEL), F32)
    saved = []
    cur = (h0, ones, zeros, h0.astype(BF))
    for l in range(DEPTH):
        cur = _layer_fwd(l, saved, *cur, w, tabs)
    dh, loss_acc = loss_grad(cur[0], cur[1], cur[2], target_p, seq, name="loss_grad")
    grads = [None] * DEPTH
    gfull = {}
    for l in reversed(range(DEPTH)):
        dh, grads[l] = _layer_bwd(l, saved[l], dh, w, tabs_bwd, gfull)
    for n in grads[0]:
        per_layer = [grads[l][n] for l in range(DEPTH)]
        gfull[n] = jnp.stack(per_layer) if n in BIG else jnp.concatenate(per_layer, axis=0)
    gfull["meta_tokens"] = dh[:N_META]
    return loss_acc, dh[N_META:t_real], gfull


def kernel(x, meta_tokens, ffn1_w_up, ffn1_w_down, mix_w_in, mix_b_gate, conv_w, q_norm_g, w_uq, kv_norm_g, w_ukv, w_br_conv, w_br_mla, w_o, ffn2_w_up, ffn2_w_down, ln_g, ln_b, loss_target, m_meta_tokens, m_ffn1_w_up, m_ffn1_w_down, m_mix_w_in, m_mix_b_gate, m_conv_w, m_q_norm_g, m_w_uq, m_kv_norm_g, m_w_ukv, m_w_br_conv, m_w_br_mla, m_w_o, m_ffn2_w_up, m_ffn2_w_down, m_ln_g, m_ln_b, v_meta_tokens, v_ffn1_w_up, v_ffn1_w_down, v_mix_w_in, v_mix_b_gate, v_conv_w, v_q_norm_g, v_w_uq, v_kv_norm_g, v_w_ukv, v_w_br_conv, v_w_br_mla, v_w_o, v_ffn2_w_up, v_ffn2_w_down, v_ln_g, v_ln_b):
    local = dict(meta_tokens=meta_tokens, ffn1_w_up=ffn1_w_up, ffn1_w_down=ffn1_w_down, mix_w_in=mix_w_in, mix_b_gate=mix_b_gate,
                 conv_w=conv_w, q_norm_g=q_norm_g, w_uq=w_uq, kv_norm_g=kv_norm_g, w_ukv=w_ukv, w_br_conv=w_br_conv,
                 w_br_mla=w_br_mla, w_o=w_o, ffn2_w_up=ffn2_w_up, ffn2_w_down=ffn2_w_down, ln_g=ln_g, ln_b=ln_b)
    mom_m = dict(zip(WEIGHTS, (m_meta_tokens, m_ffn1_w_up, m_ffn1_w_down, m_mix_w_in, m_mix_b_gate, m_conv_w, m_q_norm_g, m_w_uq,
                               m_kv_norm_g, m_w_ukv, m_w_br_conv, m_w_br_mla, m_w_o, m_ffn2_w_up, m_ffn2_w_down, m_ln_g, m_ln_b)))
    mom_v = dict(zip(WEIGHTS, (v_meta_tokens, v_ffn1_w_up, v_ffn1_w_down, v_mix_w_in, v_mix_b_gate, v_conv_w, v_q_norm_g, v_w_uq,
                               v_kv_norm_g, v_w_ukv, v_w_br_conv, v_w_br_mla, v_w_o, v_ffn2_w_up, v_ffn2_w_down, v_ln_g, v_ln_b)))
    xi, yi, ci = _place()
    chip = 2 * xi + yi

    gathered = gather_chips([local[n].astype(BF) for n in BIG], name="gather_weights")
    w = {n: jnp.concatenate([g[j] for j in range(4)], axis=ax) for n, ax, g in zip(BIG, BIG_AXIS, gathered)}
    stacked = exchange_small([_view2d(local[n]) for n in SMALL_SHARDED], reduce=False, name="gather_small")
    for n, st in zip(SMALL_SHARDED, stacked):
        full = jnp.concatenate([st[2 * j] for j in range(4)], axis=-1)
        w[n] = full.reshape(local[n].shape[:-1] + (full.shape[-1],))
    for n in SMALL_REPLICATED:
        w[n] = local[n]
    w["mix_w_in"] = jnp.stack([_pad_w_in(w["mix_w_in"][l]) for l in range(DEPTH)])
    w["w_uq"] = jnp.stack([_pad_w_uq(w["w_uq"][l]) for l in range(DEPTH)])
    w["w_br_mla"] = jnp.stack([_pad_w_br_mla(w["w_br_mla"][l]) for l in range(DEPTH)])

    loss_acc, grad_x, gfull = _local_grads(x[0], loss_target[0], w)
    grad_x = grad_x[None]

    c_idx = jnp.reshape(ci, (1,)).astype(jnp.int32)
    glist = [gfull[n] for n in BIG]
    from_sibling = pair_exchange(glist, name="rs_pair_exchange")
    pair_sums = [pair_add(g, s, c_idx, name="rs_pair_add") for g, s in zip(glist, from_sibling)]
    from_chips = chip_scatter(pair_sums, name="rs_chip_scatter")
    reduced = pair_gather([sum_chunks(r, name="rs_sum") for r in from_chips], name="rs_pair_gather")
    gshard = {n: r.reshape(local[n].shape) for n, r in zip(BIG, reduced)}

    small_names = SMALL_SHARDED + SMALL_REPLICATED
    small_red = exchange_small([gfull[n] for n in small_names] + [loss_acc], reduce=True, name="reduce_small")
    loss = small_red[-1][0, 0]
    for n, full in zip(small_names, small_red[:-1]):
        if n in SMALL_SHARDED:
            sh = local[n].shape[-1]
            full = lax.dynamic_slice_in_dim(full, chip * sh, sh, axis=1)
        gshard[n] = full.reshape(local[n].shape)

    delta, new_m, new_v = {}, {}, {}
    for n in WEIGHTS:
        shape = local[n].shape
        d, nm, nv = adamw(_view2d(local[n]), _view2d(gshard[n]), _view2d(mom_m[n]), _view2d(mom_v[n]), name="adamw")
        delta[n], new_m[n], new_v[n] = d.reshape(shape), nm.reshape(shape), nv.reshape(shape)
    return (loss, grad_x, *[gshard[n] for n in WEIGHTS], *[delta[n] for n in WEIGHTS], *[new_m[n] for n in WEIGHTS],
            *[new_v[n] for n in WEIGHTS])
```

```python
import functools

import jax
import jax.numpy as jnp
from jax import lax
from jax.experimental import pallas as pl
from jax.experimental.pallas import tpu as pltpu

F32 = jnp.float32
BF = jnp.bfloat16
MESH = pl.DeviceIdType.MESH

D_MODEL = 1024
DEPTH = 2
N_META = 16
D_CONV = 512
MLA_HEADS = 8
QK_NOPE = 64
QK_ROPE = 32
V_HEAD = 64
Q_LORA = 256
KV_LORA = 128
ROPE_BASE = 10000.0
NEG_INF = -1e30
D_FF = 2816
ALPHA = (2 * DEPTH) ** 0.25
LN_EPS = 1e-5
RMS_EPS = 1e-6
ATT_SCALE = (QK_NOPE + QK_ROPE) ** -0.5
LOG2E = 1.4426950408889634
LN2 = 0.6931471805599453
D_IN = 4000
D_IN_PAD = 4096
Q_PAD = 256
KV_PAD = 128

ADAM_LR = 0.001
ADAM_B1 = 0.9
ADAM_B2 = 0.999
ADAM_EPS = 1e-08
ADAM_WD = 0.01
ADAM_STEP = 10

TM = 768
TMH = 384
LANES = 128
COMM_COLS = 512
COMM_ROW_BLOCK = 1472
VMEM_LIMIT_BYTES = 50 * 1024 * 1024

NT = (((1,), (1,)), ((), ()))
TN = (((0,), (0,)), ((), ()))


def _pcall(body, **kw):
    return pl.pallas_call(body, **kw)


def _params(n_axes):
    return pltpu.CompilerParams(dimension_semantics=("arbitrary",) * n_axes, vmem_limit_bytes=VMEM_LIMIT_BYTES)


def _sds(shape, dtype):
    return jax.ShapeDtypeStruct(shape, dtype)


def mm_rows(pairs, n_out, *, name, tn=None, addend=None, add_scale=1.0, out_dtype=F32):
    tp = pairs[0][0].shape[0]
    tn = tn or n_out
    in_specs, args = [], []
    for a, b, nt, kb in pairs:
        k = a.shape[1]
        in_specs.append(pl.BlockSpec((TM, k), lambda i, j: (i, 0)))
        if nt:
            in_specs.append(pl.BlockSpec((tn, k), functools.partial(lambda i, j, kb: (j, kb), kb=kb)))
        else:
            in_specs.append(pl.BlockSpec((k, tn), lambda i, j: (0, j)))
        args += [a, b]
    if addend is not None:
        in_specs.append(pl.BlockSpec((TM, tn), lambda i, j: (i, j)))
        args.append(addend)
    n_pairs = len(pairs)
    nts = [p[2] for p in pairs]

    def body(*refs):
        o_ref = refs[-1]
        acc = None
        for p in range(n_pairs):
            a = refs[2 * p][...].astype(BF)
            b = refs[2 * p + 1][...]
            d = lax.dot_general(a, b, NT if nts[p] else (((1,), (0,)), ((), ())), preferred_element_type=F32)
            acc = d if acc is None else acc + d
        if addend is not None:
            acc = acc + add_scale * refs[2 * n_pairs][...]
        o_ref[...] = acc.astype(o_ref.dtype)

    return _pcall(
        body, name=name, grid=(tp // TM, n_out // tn), in_specs=in_specs,
        out_specs=pl.BlockSpec((TM, tn), lambda i, j: (i, j)), out_shape=_sds((tp, n_out), out_dtype),
        compiler_params=_params(2),
    )(*args)


def tn_mm(a, b, *, tm, name, out_dtype=BF, shard=None, slot0=0, dst=None):
    tp, m = a.shape
    n = b.shape[1]
    nk = tp // TM
    if shard is None:
        pieces, out_block, out_index, out_full = 1, (tm, n), (lambda i, k: (i, 0)), (m, n)
    elif shard[0] == "cols":
        pieces = n // shard[1]
        out_block, out_full = (pieces, tm, shard[1]), (4, m, shard[1])
        out_index = lambda i, k: (slot0 // pieces, i, 0)
    else:
        pieces = tm // shard[1]
        out_block, out_full = (pieces, shard[1], n), (4, m // 4, n)
        out_index = lambda i, k: (i, 0, 0)

    def body(a_ref, b_ref, *rest):
        o_ref, acc_ref = rest[-2], rest[-1]
        k = pl.program_id(1)

        @pl.when(k == 0)
        def _():
            acc_ref[...] = jnp.zeros_like(acc_ref)

        acc_ref[...] += lax.dot_general(a_ref[...].astype(BF), b_ref[...].astype(BF), TN, preferred_element_type=F32)

        @pl.when(k == nk - 1)
        def _():
            if shard is None:
                o_ref[...] = acc_ref[...].astype(o_ref.dtype)
            elif shard[0] == "cols":
                for j in range(pieces):
                    o_ref[j] = acc_ref[:, j * shard[1]:(j + 1) * shard[1]].astype(o_ref.dtype)
            else:
                for j in range(pieces):
                    o_ref[j] = acc_ref[j * shard[1]:(j + 1) * shard[1], :].astype(o_ref.dtype)

    in_specs = [pl.BlockSpec((TM, tm), lambda i, k: (k, i)), pl.BlockSpec((TM, n), lambda i, k: (k, 0))]
    args = [a, b]
    aliases = {}
    if dst is not None:
        in_specs.append(pl.BlockSpec(memory_space=pl.ANY))
        args.append(dst)
        aliases = {2: 0}
    return _pcall(
        body, name=name, grid=(m // tm, nk), in_specs=in_specs, out_specs=pl.BlockSpec(out_block, out_index),
        out_shape=_sds(out_full, out_dtype), input_output_aliases=aliases,
        scratch_shapes=[pltpu.VMEM((tm, n), F32)], compiler_params=_params(2),
    )(*args)


def _ln_store(z, g_ref, b_ref, xh_ref, rs_ref, hb_ref):
    mu = jnp.mean(z, axis=-1, keepdims=True)
    zc = z - mu
    var = jnp.mean(zc * zc, axis=-1, keepdims=True)
    rstd = lax.rsqrt(var + LN_EPS)
    xh = zc * rstd
    xh_ref[...] = xh
    rs_ref[...] = rstd
    hb_ref[...] = (xh * g_ref[...] + b_ref[...]).astype(BF)


def _ln_out(tp, tm=TM):
    specs = [pl.BlockSpec((tm, D_MODEL), lambda i: (i, 0)), pl.BlockSpec((tm, 1), lambda i: (i, 0)),
             pl.BlockSpec((tm, D_MODEL), lambda i: (i, 0))]
    shapes = [_sds((tp, D_MODEL), F32), _sds((tp, 1), F32), _sds((tp, D_MODEL), BF)]
    return specs, shapes


def _row_vec(n):
    return pl.BlockSpec((1, n), lambda i: (0, 0))


def ffn_up(hb, wup, *, name):
    tp = hb.shape[0]
    tn = D_FF // 2
    nj = D_FF // tn

    def body(h_ref, wg_ref, wu_ref, g_ref, u_ref, a_ref):
        h = h_ref[...]
        g = jnp.dot(h, wg_ref[...], preferred_element_type=F32)
        u = jnp.dot(h, wu_ref[...], preferred_element_type=F32)
        g_ref[...] = g.astype(BF)
        u_ref[...] = u.astype(BF)
        a_ref[...] = (g * jax.nn.sigmoid(g) * u).astype(BF)

    blk = pl.BlockSpec((TM, tn), lambda i, j: (i, j))
    return _pcall(
        body, name=name, grid=(tp // TM, nj),
        in_specs=[pl.BlockSpec((TM, D_MODEL), lambda i, j: (i, 0)), pl.BlockSpec((D_MODEL, tn), lambda i, j: (0, j)),
                  pl.BlockSpec((D_MODEL, tn), lambda i, j: (0, j + nj))],
        out_specs=[blk, blk, blk], out_shape=[_sds((tp, D_FF), BF)] * 3, compiler_params=_params(2),
    )(hb, wup, wup)


def down_ln(a, wd, xprev, gp, bp, g, b, *, name):
    tp = a.shape[0]

    def body(a_ref, wd_ref, xp_ref, gp_ref, bp_ref, g_ref, b_ref, xh_ref, rs_ref, hb_ref):
        f = jnp.dot(a_ref[...], wd_ref[...], preferred_element_type=F32)
        hprev = xp_ref[...] * gp_ref[...] + bp_ref[...]
        _ln_store(ALPHA * hprev + 0.5 * f, g_ref, b_ref, xh_ref, rs_ref, hb_ref)

    out_specs, out_shape = _ln_out(tp)
    return _pcall(
        body, name=name, grid=(tp // TM,),
        in_specs=[pl.BlockSpec((TM, D_FF), lambda i: (i, 0)), pl.BlockSpec((D_FF, D_MODEL), lambda i: (0, 0)),
                  pl.BlockSpec((TM, D_MODEL), lambda i: (i, 0))] + [_row_vec(D_MODEL)] * 4,
        out_specs=out_specs, out_shape=out_shape, compiler_params=_params(1),
    )(a, wd, xprev, gp, bp, g, b)


def _rope(x, c, s1, s2, reps):
    n = x.shape[1]
    if reps > 1:
        c, s1, s2 = (jnp.tile(t, (1, reps)) for t in (c, s1, s2))
    return x * c + pltpu.roll(x, 16, 1) * s1 + pltpu.roll(x, n - 16, 1) * s2


def _rms(x, g):
    r = lax.rsqrt(jnp.mean(x * x, axis=-1, keepdims=True) + RMS_EPS)
    return x * r * g, r


def mla_prep(p, gq, gkv, wuq_p, wukv, tabs, *, name):
    tp = p.shape[0]
    nh = MLA_HEADS

    def body(cq_ref, ckv_ref, kr_ref, gq_ref, gkv_ref, wuq_ref, wukv_ref, cq_t, s1q_t, s2q_t, ck_t, s1k_t, s2k_t,
             cqn_ref, ckvn_ref, q2_ref, kv_ref, krr_ref):
        cqn, _ = _rms(cq_ref[...], gq_ref[...])
        ckvn, _ = _rms(ckv_ref[...], gkv_ref[...])
        cqn = cqn.astype(BF)
        ckvn = ckvn.astype(BF)
        cqn_ref[...] = cqn
        ckvn_ref[...] = ckvn
        q = jnp.dot(cqn, wuq_ref[...], preferred_element_type=F32)
        q2_ref[...] = _rope(q, cq_t[...], s1q_t[...], s2q_t[...], nh).astype(BF)
        kv_ref[...] = jnp.dot(ckvn, wukv_ref[...], preferred_element_type=F32).astype(BF)
        krr_ref[...] = _rope(kr_ref[...], ck_t[...], s1k_t[...], s2k_t[...], 1).astype(BF)

    def rows(n, col=0):
        return pl.BlockSpec((TMH, n), functools.partial(lambda i, col: (i, col), col=col))

    return _pcall(
        body, name=name, grid=(tp // TMH,),
        in_specs=[rows(Q_LORA, 1536 // Q_LORA), rows(KV_LORA, 1792 // KV_LORA), rows(LANES, 1920 // LANES),
                  _row_vec(Q_LORA), _row_vec(KV_LORA),
                  pl.BlockSpec((Q_LORA, nh * Q_PAD), lambda i: (0, 0)), pl.BlockSpec((KV_LORA, nh * KV_PAD), lambda i: (0, 0)),
                  rows(Q_PAD), rows(Q_PAD), rows(Q_PAD), rows(LANES), rows(LANES), rows(LANES)],
        out_specs=[rows(Q_LORA), rows(KV_LORA), rows(nh * Q_PAD), rows(nh * KV_PAD), rows(LANES)],
        out_shape=[_sds((tp, Q_LORA), BF), _sds((tp, KV_LORA), BF), _sds((tp, nh * Q_PAD), BF),
                   _sds((tp, nh * KV_PAD), BF), _sds((tp, LANES), BF)],
        compiler_params=_params(1),
    )(p, p, p, gq, gkv, wuq_p, wukv, *tabs)


def _causal_mask(s):
    qpos = lax.broadcasted_iota(jnp.int32, (TM, TM), 0)
    kpos = lax.broadcasted_iota(jnp.int32, (TM, TM), 1)
    return jnp.where(kpos <= qpos, s, NEG_INF)


def _key_rows(k):
    return pl.ds(pl.multiple_of(k * TM, TM), TM)


def _pipelined_key_blocks(n, prefetch, process):
    prefetch(0, 0)

    def pair(j, carry):
        prefetch(2 * j + 1, 1)
        process(2 * j, 0, False)
        prefetch(2 * j + 2, 0)
        process(2 * j + 1, 1, False)
        return carry

    lax.fori_loop(0, n // 2, pair, 0)

    @pl.when(n % 2 == 1)
    def _():
        prefetch(n, 1)
        process(n - 1, 0, False)
        process(n, 1, True)

    @pl.when(n % 2 == 0)
    def _():
        process(n, 0, True)


def _side_call(body_main, side, *, name, grid, in_specs, out_specs, out_shape, scratch_shapes, args):
    n_in, n_out, n_scr = len(in_specs), len(out_specs), len(scratch_shapes)
    s_in, s_pre, n_sems, program = side if side is not None else ((), (), 0, None)
    a, b = len(s_in), len(s_pre)

    def body(*refs):
        in_refs = refs[:n_in]
        out_refs = refs[n_in + a + b:n_in + a + b + n_out]
        scr = refs[n_in + a + 2 * b + n_out:n_in + a + 2 * b + n_out + n_scr]
        if side is not None:
            side_in = refs[n_in:n_in + a]
            side_out = refs[n_in + a + b + n_out:n_in + a + 2 * b + n_out]
            start, finish = program(side_in, side_out, refs[-2], refs[-1])

            @pl.when((pl.program_id(0) == 0) & (pl.program_id(1) == 0))
            def _():
                start()

        body_main(in_refs, out_refs, scr)
        if side is not None:
            @pl.when((pl.program_id(0) == grid[0] - 1) & (pl.program_id(1) == grid[1] - 1))
            def _():
                finish()

    sems = [pltpu.SemaphoreType.DMA((n_sems,))] * 2 if side is not None else []
    return _pcall(
        body, name=name, grid=grid, in_specs=list(in_specs) + [HBM_SPEC] * (a + b), out_specs=list(out_specs) + [HBM_SPEC] * b,
        out_shape=list(out_shape) + [_sds(p.shape, p.dtype) for p in s_pre],
        input_output_aliases={n_in + a + i: n_out + i for i in range(b)},
        scratch_shapes=list(scratch_shapes) + sems, compiler_params=_params(2),
    )(*args, *s_in, *s_pre)


def attn_fwd(q2, kv, kr, *, name, side=None):
    tp = q2.shape[0]
    nh = MLA_HEADS
    nb = tp // TM
    rep = TM // LANES

    def body(in_refs, out_refs, scratch):
        q_ref, kv_ref, kr_ref = in_refs
        o_ref, lse_ref = out_refs
        m_ref, l_ref, acc_ref, s0_ref, s1_ref = scratch
        qi = pl.program_id(1)
        s_refs = (s0_ref, s1_ref)
        m_ref[...] = jnp.full_like(m_ref, NEG_INF)
        l_ref[...] = jnp.zeros_like(l_ref)
        acc_ref[...] = jnp.zeros_like(acc_ref)

        def prefetch(k, slot):
            k2 = jnp.concatenate([kv_ref[_key_rows(k), :], kr_ref[_key_rows(k), :]], axis=1)
            s_refs[slot][...] = lax.dot_general(q_ref[...], k2, NT, preferred_element_type=F32)

        def process(k, slot, diagonal):
            s = s_refs[slot][...]
            if diagonal:
                s = _causal_mask(s)
            m_prev = m_ref[...]
            m_new = jnp.maximum(m_prev, jnp.max(s, axis=1, keepdims=True))
            alpha = jnp.exp2(m_prev - m_new)
            p = jnp.exp2(s - jnp.tile(m_new, (1, rep)))
            l_ref[...] = alpha * l_ref[...] + jnp.sum(p, axis=1, keepdims=True)
            acc_ref[...] = alpha * acc_ref[...] + jnp.dot(p.astype(BF), kv_ref[_key_rows(k), :], preferred_element_type=F32)
            m_ref[...] = m_new

        _pipelined_key_blocks(qi, prefetch, process)
        o_ref[...] = (acc_ref[...] / l_ref[...]).astype(BF)
        lse_ref[...] = m_ref[...] + jnp.log2(l_ref[...])

    return _side_call(
        body, side, name=name, grid=(nh, nb),
        in_specs=[pl.BlockSpec((TM, Q_PAD), lambda h, qi: (qi, h)), pl.BlockSpec((tp, KV_PAD), lambda h, qi: (0, h)),
                  pl.BlockSpec((tp, LANES), lambda h, qi: (0, 0))],
        out_specs=[pl.BlockSpec((TM, KV_PAD), lambda h, qi: (qi, h)), pl.BlockSpec((TM, LANES), lambda h, qi: (qi, h))],
        out_shape=[_sds((tp, nh * KV_PAD), BF), _sds((tp, nh * LANES), F32)],
        scratch_shapes=[pltpu.VMEM((TM, LANES), F32)] * 3 + [pltpu.VMEM((TM, TM), F32)] * 2, args=(q2, kv, kr),
    )


def conv_fwd(p, w, *, name):
    tp = p.shape[0]

    def body(b_ref, c_ref, h_ref, w_ref, y_ref, cv_ref, ebuf):
        i = pl.program_id(0)

        @pl.when(i == 0)
        def _():
            ebuf[0:8, :] = jnp.zeros((8, D_CONV), F32)

        e = c_ref[...] * h_ref[...]
        ebuf[8:8 + TM, :] = e
        w_all = w_ref[...]
        conv = w_all[0:1] * ebuf[pl.ds(6, TM), :] + w_all[1:2] * ebuf[pl.ds(7, TM), :] + w_all[2:3] * e
        cv_ref[...] = conv.astype(BF)
        y_ref[...] = (b_ref[...] * conv).astype(BF)
        ebuf[0:8, :] = ebuf[TM:TM + 8, :]

    def col(j):
        return pl.BlockSpec((TM, D_CONV), functools.partial(lambda i, j: (i, j), j=j))

    return _pcall(
        body, name=name, grid=(tp // TM,),
        in_specs=[col(0), col(1), col(2), pl.BlockSpec((3, D_CONV), lambda i: (0, 0))],
        out_specs=[col(0), col(0)], out_shape=[_sds((tp, D_CONV), BF)] * 2,
        scratch_shapes=[pltpu.VMEM((TM + 8, D_CONV), F32)], compiler_params=_params(1),
    )(p, p, p, w)


def merge_out_ln(ycv, o2, p, bg, wbc, wbm_p, wo, xprev, gp, bp, g, b, *, name):
    tp = ycv.shape[0]

    def body(y_ref, o_ref, gc_ref, gm_ref, bg_ref, wbc_ref, wbm_ref, wo_ref, xp_ref, gp_ref, bp_ref, g_ref, b_ref,
             bc_ref, bm_ref, mg_ref, xh_ref, rs_ref, hb_ref):
        bc = jnp.dot(y_ref[...], wbc_ref[...], preferred_element_type=F32)
        bm = jnp.dot(o_ref[...], wbm_ref[...], preferred_element_type=F32)
        bgv = bg_ref[...]
        mg = jax.nn.sigmoid(gc_ref[...] + bgv[0:1]) * bc + jax.nn.sigmoid(gm_ref[...] + bgv[1:2]) * bm
        mgb = mg.astype(BF)
        bc_ref[...] = bc.astype(BF)
        bm_ref[...] = bm.astype(BF)
        mg_ref[...] = mgb
        mix = jnp.dot(mgb, wo_ref[...], preferred_element_type=F32)
        hprev = xp_ref[...] * gp_ref[...] + bp_ref[...]
        _ln_store(ALPHA * hprev + mix, g_ref, b_ref, xh_ref, rs_ref, hb_ref)

    def rows(n, col=0):
        return pl.BlockSpec((TMH, n), functools.partial(lambda i, col: (i, col), col=col))

    def whole(r, c):
        return pl.BlockSpec((r, c), lambda i: (0, 0))

    ln_specs, ln_shapes = _ln_out(tp, TMH)
    return _pcall(
        body, name=name, grid=(tp // TMH,),
        in_specs=[rows(D_CONV), rows(MLA_HEADS * KV_PAD), rows(D_MODEL, 2), rows(D_MODEL, 3), whole(2, D_MODEL),
                  whole(D_CONV, D_MODEL), whole(MLA_HEADS * KV_PAD, D_MODEL), whole(D_MODEL, D_MODEL), rows(D_MODEL)]
        + [_row_vec(D_MODEL)] * 4,
        out_specs=[rows(D_MODEL)] * 3 + ln_specs, out_shape=[_sds((tp, D_MODEL), BF)] * 3 + ln_shapes,
        compiler_params=_params(1),
    )(ycv, o2, p, p, bg, wbc, wbm_p, wo, xprev, gp, bp, g, b)


def loss_grad(xh, g, b, target_p, n_real, *, name):
    tp = xh.shape[0]

    def body(x_ref, g_ref, b_ref, t_ref, dy_ref, loss_ref):
        i = pl.program_id(0)

        @pl.when(i == 0)
        def _():
            loss_ref[...] = jnp.zeros_like(loss_ref)

        row = i * TM + lax.broadcasted_iota(jnp.int32, (TM, 1), 0)
        real = (row >= N_META) & (row < N_META + n_real)
        diff = jnp.where(real, x_ref[...] * g_ref[...] + b_ref[...] - t_ref[...], 0.0)
        dy_ref[...] = diff * (1.0 / D_MODEL)
        loss_ref[...] += 0.5 / D_MODEL * jnp.sum(diff * diff)

    return _pcall(
        body, name=name, grid=(tp // TM,),
        in_specs=[pl.BlockSpec((TM, D_MODEL), lambda i: (i, 0)), _row_vec(D_MODEL), _row_vec(D_MODEL),
                  pl.BlockSpec((TM, D_MODEL), lambda i: (i, 0))],
        out_specs=[pl.BlockSpec((TM, D_MODEL), lambda i: (i, 0)), pl.BlockSpec((8, LANES), lambda i: (0, 0))],
        out_shape=[_sds((tp, D_MODEL), F32), _sds((8, LANES), F32)], compiler_params=_params(1),
    )(xh, g, b, target_p)


def ln_bwd(dh, xh, rstd, g, *, branch_scale, name):
    tp = dh.shape[0]

    def body(dh_ref, xh_ref, rs_ref, g_ref, dz_ref, dzb_ref, dg_ref, db_ref):
        i = pl.program_id(0)

        @pl.when(i == 0)
        def _():
            dg_ref[...] = jnp.zeros_like(dg_ref)
            db_ref[...] = jnp.zeros_like(db_ref)

        dy = dh_ref[...]
        xhat = xh_ref[...]
        dg_ref[...] += jnp.sum(dy * xhat, axis=0, keepdims=True)
        db_ref[...] += jnp.sum(dy, axis=0, keepdims=True)
        dxh = dy * g_ref[...]
        m1 = jnp.mean(dxh, axis=-1, keepdims=True)
        m2 = jnp.mean(dxh * xhat, axis=-1, keepdims=True)
        dz = rs_ref[...] * (dxh - m1 - xhat * m2)
        dz_ref[...] = dz
        dzb_ref[...] = (branch_scale * dz).astype(BF)

    rows = pl.BlockSpec((TM, D_MODEL), lambda i: (i, 0))
    return _pcall(
        body, name=name, grid=(tp // TM,),
        in_specs=[rows, rows, pl.BlockSpec((TM, 1), lambda i: (i, 0)), _row_vec(D_MODEL)],
        out_specs=[rows, rows, _row_vec(D_MODEL), _row_vec(D_MODEL)],
        out_shape=[_sds((tp, D_MODEL), F32), _sds((tp, D_MODEL), BF), _sds((1, D_MODEL), F32), _sds((1, D_MODEL), F32)],
        compiler_params=_params(1),
    )(dh, xh, rstd, g)


def ffn_down_bwd(dzb, wd, gate, up, *, name):
    tp = dzb.shape[0]
    tn = D_FF // 2

    def body(dz_ref, wd_ref, g_ref, u_ref, dg_ref, du_ref):
        da = lax.dot_general(dz_ref[...], wd_ref[...], NT, preferred_element_type=F32)
        g = g_ref[...].astype(F32)
        u = u_ref[...].astype(F32)
        sg = jax.nn.sigmoid(g)
        dg_ref[...] = (da * u * sg * (1.0 + g * (1.0 - sg))).astype(BF)
        du_ref[...] = (da * g * sg).astype(BF)

    blk = pl.BlockSpec((TM, tn), lambda i, j: (i, j))
    return _pcall(
        body, name=name, grid=(tp // TM, D_FF // tn),
        in_specs=[pl.BlockSpec((TM, D_MODEL), lambda i, j: (i, 0)), pl.BlockSpec((tn, D_MODEL), lambda i, j: (j, 0)), blk, blk],
        out_specs=[blk, blk], out_shape=[_sds((tp, D_FF), BF)] * 2, compiler_params=_params(2),
    )(dzb, wd, gate, up)


def merge_bwd(dzb, wo, bc, bm, p, bg, wbc, wbm_p, o2, *, name):
    tp = dzb.shape[0]
    nh = MLA_HEADS

    def body(dz_ref, wo_ref, bc_ref, bm_ref, gc_ref, gm_ref, bg_ref, wbc_ref, wbm_ref, o_ref,
             dbc_ref, dbm_ref, dgg_ref, dy_ref, do_ref, dl_ref, dbg_ref):
        i = pl.program_id(0)

        @pl.when(i == 0)
        def _():
            dbg_ref[...] = jnp.zeros_like(dbg_ref)

        dmg = lax.dot_general(dz_ref[...], wo_ref[...], NT, preferred_element_type=F32)
        bgv = bg_ref[...]
        sc = jax.nn.sigmoid(gc_ref[...] + bgv[0:1])
        sm = jax.nn.sigmoid(gm_ref[...] + bgv[1:2])
        dbc = (dmg * sc).astype(BF)
        dbm = (dmg * sm).astype(BF)
        dgc = dmg * bc_ref[...].astype(F32) * sc * (1.0 - sc)
        dgm = dmg * bm_ref[...].astype(F32) * sm * (1.0 - sm)
        dbc_ref[...] = dbc
        dbm_ref[...] = dbm
        dgg_ref[...] = jnp.concatenate([dgc, dgm], axis=1).astype(BF)
        dbg_ref[...] += jnp.concatenate([jnp.sum(dgc, axis=0, keepdims=True), jnp.sum(dgm, axis=0, keepdims=True)], axis=0)
        dy_ref[...] = lax.dot_general(dbc, wbc_ref[...], NT, preferred_element_type=F32)
        do = lax.dot_general(dbm, wbm_ref[...], NT, preferred_element_type=F32)
        do_ref[...] = do.astype(BF)
        prod = do * o_ref[...].astype(F32)
        parts = []
        for h in range(nh):
            d = jnp.sum(prod[:, h * KV_PAD:(h + 1) * KV_PAD], axis=1, keepdims=True)
            parts.append(jnp.broadcast_to(d, (TMH, LANES)))
        dl_ref[...] = jnp.concatenate(parts, axis=1)

    def rows(n, col=0):
        return pl.BlockSpec((TMH, n), functools.partial(lambda i, col: (i, col), col=col))

    def whole(r, c):
        return pl.BlockSpec((r, c), lambda i: (0, 0))

    return _pcall(
        body, name=name, grid=(tp // TMH,),
        in_specs=[rows(D_MODEL), whole(D_MODEL, D_MODEL), rows(D_MODEL), rows(D_MODEL), rows(D_MODEL, 2), rows(D_MODEL, 3),
                  whole(2, D_MODEL), whole(D_CONV, D_MODEL), whole(nh * KV_PAD, D_MODEL), rows(nh * KV_PAD)],
        out_specs=[rows(D_MODEL), rows(D_MODEL), rows(2 * D_MODEL), rows(D_CONV), rows(nh * KV_PAD), rows(nh * LANES),
                   whole(2, D_MODEL)],
        out_shape=[_sds((tp, D_MODEL), BF), _sds((tp, D_MODEL), BF), _sds((tp, 2 * D_MODEL), BF), _sds((tp, D_CONV), F32),
                   _sds((tp, nh * KV_PAD), BF), _sds((tp, nh * LANES), F32), _sds((2, D_MODEL), F32)],
        compiler_params=_params(1),
    )(dzb, wo, bc, bm, p, p, bg, wbc, wbm_p, o2)


def attn_bwd(q2, kv, kr, do2, lse, dl, *, name, side=None):
    tp = q2.shape[0]
    nh = MLA_HEADS
    nb = tp // TM
    rep = TM // LANES

    def body(in_refs, out_refs, scratch):
        q_ref, kv_ref, kr_ref, do_ref, lse_ref, dl_ref = in_refs
        dq_ref, dkv_ref, dkr_ref = out_refs
        dq_acc, s0_ref, s1_ref, dp0_ref, dp1_ref = scratch
        qi = pl.program_id(1)
        s_refs, dp_refs = (s0_ref, s1_ref), (dp0_ref, dp1_ref)

        @pl.when(qi == 0)
        def _():
            dkv_ref[...] = jnp.zeros_like(dkv_ref)
            dkr_ref[...] = jnp.zeros_like(dkr_ref)

        dq_acc[...] = jnp.zeros_like(dq_acc)

        def prefetch(k, slot):
            kvb = kv_ref[_key_rows(k), :]
            k2 = jnp.concatenate([kvb, kr_ref[_key_rows(k), :]], axis=1)
            s_refs[slot][...] = lax.dot_general(q_ref[...], k2, NT, preferred_element_type=F32)
            dp_refs[slot][...] = lax.dot_general(do_ref[...], kvb, NT, preferred_element_type=F32)

        def process(k, slot, diagonal):
            rows = _key_rows(k)
            s = s_refs[slot][...]
            if diagonal:
                s = _causal_mask(s)
            p = jnp.exp2(s - jnp.tile(lse_ref[...], (1, rep)))
            dsb = (p * (dp_refs[slot][...] - jnp.tile(dl_ref[...], (1, rep)))).astype(BF)
            dk2 = lax.dot_general(dsb, q_ref[...], TN, preferred_element_type=F32) * LN2
            dkv_ref[rows, :] += lax.dot_general(p.astype(BF), do_ref[...], TN, preferred_element_type=F32) + dk2[:, :KV_PAD]
            dkr_ref[rows, :] += dk2[:, KV_PAD:KV_PAD + LANES]
            k2 = jnp.concatenate([kv_ref[rows, :], kr_ref[rows, :]], axis=1)
            dq_acc[...] += jnp.dot(dsb, k2, preferred_element_type=F32)

        _pipelined_key_blocks(qi, prefetch, process)
        dq_ref[...] = dq_acc[...]

    def qrow(n):
        return pl.BlockSpec((TM, n), lambda h, qi: (qi, h))

    def head(n):
        return pl.BlockSpec((tp, n), lambda h, qi: (0, h))

    return _side_call(
        body, side, name=name, grid=(nh, nb),
        in_specs=[qrow(Q_PAD), head(KV_PAD), pl.BlockSpec((tp, LANES), lambda h, qi: (0, 0)), qrow(KV_PAD), qrow(LANES), qrow(LANES)],
        out_specs=[qrow(Q_PAD), head(KV_PAD), head(LANES)],
        out_shape=[_sds((tp, nh * Q_PAD), F32), _sds((tp, nh * KV_PAD), F32), _sds((tp, nh * LANES), F32)],
        scratch_shapes=[pltpu.VMEM((TM, Q_PAD), F32)] + [pltpu.VMEM((TM, TM), F32)] * 4, args=(q2, kv, kr, do2, lse, dl),
    )


def _rms_bwd(x, g, dy):
    r = lax.rsqrt(jnp.mean(x * x, axis=-1, keepdims=True) + RMS_EPS)
    gy = dy * g
    dx = r * gy - x * (r * r * r) * jnp.mean(x * gy, axis=-1, keepdims=True)
    return dx, jnp.sum(dy * x * r, axis=0, keepdims=True)


def mla_prep_bwd(dq2, dkv, dkr, p, gq, gkv, wuq_p, wukv, tabs_bwd, *, name):
    tp = dq2.shape[0]
    nh = MLA_HEADS

    def body(dq_ref, dkv_ref, dkr_ref, cq_ref, ckv_ref, gq_ref, gkv_ref, wuq_ref, wukv_ref,
             cq_t, s1q_t, s2q_t, ck_t, s1k_t, s2k_t, dqb_ref, dsm_ref, dgq_ref, dgkv_ref):
        i = pl.program_id(0)

        @pl.when(i == 0)
        def _():
            dgq_ref[...] = jnp.zeros_like(dgq_ref)
            dgkv_ref[...] = jnp.zeros_like(dgkv_ref)

        dqb = _rope(dq_ref[...], cq_t[...], s1q_t[...], s2q_t[...], nh).astype(BF)
        dqb_ref[...] = dqb
        dcqn = lax.dot_general(dqb, wuq_ref[...], NT, preferred_element_type=F32)
        dcq, dgq = _rms_bwd(cq_ref[...], gq_ref[...], dcqn)
        dckvn = lax.dot_general(dkv_ref[...].astype(BF), wukv_ref[...], NT, preferred_element_type=F32)
        dckv, dgkv = _rms_bwd(ckv_ref[...], gkv_ref[...], dckvn)
        dkr_heads = dkr_ref[...]
        dkr_sum = dkr_heads[:, :LANES]
        for h in range(1, nh):
            dkr_sum = dkr_sum + dkr_heads[:, h * LANES:(h + 1) * LANES]
        dkr = _rope(dkr_sum, ck_t[...], s1k_t[...], s2k_t[...], 1)
        dsm_ref[...] = jnp.concatenate([dcq, dckv, dkr], axis=1)
        dgq_ref[...] += dgq
        dgkv_ref[...] += dgkv

    def rows(n, col=0):
        return pl.BlockSpec((TMH, n), functools.partial(lambda i, col: (i, col), col=col))

    return _pcall(
        body, name=name, grid=(tp // TMH,),
        in_specs=[rows(nh * Q_PAD), rows(nh * KV_PAD), rows(nh * LANES), rows(Q_LORA, 1536 // Q_LORA), rows(KV_LORA, 1792 // KV_LORA),
                  _row_vec(Q_LORA), _row_vec(KV_LORA),
                  pl.BlockSpec((Q_LORA, nh * Q_PAD), lambda i: (0, 0)), pl.BlockSpec((KV_LORA, nh * KV_PAD), lambda i: (0, 0)),
                  rows(Q_PAD), rows(Q_PAD), rows(Q_PAD), rows(LANES), rows(LANES), rows(LANES)],
        out_specs=[rows(nh * Q_PAD), rows(Q_LORA + KV_LORA + LANES), _row_vec(Q_LORA), _row_vec(KV_LORA)],
        out_shape=[_sds((tp, nh * Q_PAD), BF), _sds((tp, Q_LORA + KV_LORA + LANES), F32), _sds((1, Q_LORA), F32),
                   _sds((1, KV_LORA), F32)],
        compiler_params=_params(1),
    )(dq2, dkv, dkr, p, p, gq, gkv, wuq_p, wukv, *tabs_bwd)


def conv_bwd(dy, p, conv, w, *, name):
    tp = dy.shape[0]
    nb = tp // TM

    def body(dy_ref, b_ref, c_ref, h_ref, cv_ref, w_ref, dp_ref, dw0_ref, dw1_ref, dw2_ref, dbuf):
        i = pl.program_id(0)

        @pl.when(i == 0)
        def _():
            dbuf[TM:TM + 8, :] = jnp.zeros((8, D_CONV), F32)
            dw0_ref[...] = jnp.zeros_like(dw0_ref)
            dw1_ref[...] = jnp.zeros_like(dw1_ref)
            dw2_ref[...] = jnp.zeros_like(dw2_ref)

        dyv = dy_ref[...]
        c = c_ref[...]
        hh = h_ref[...]
        dconv = dyv * b_ref[...]
        dbuf[0:TM, :] = dconv
        d1 = dbuf[pl.ds(1, TM), :]
        d2 = dbuf[pl.ds(2, TM), :]
        w_all = w_ref[...]
        de = w_all[2:3] * dconv + w_all[1:2] * d1 + w_all[0:1] * d2
        e = c * hh
        dp_ref[...] = jnp.concatenate([dyv * cv_ref[...].astype(F32), de * hh, de * c], axis=1).astype(BF)
        dw0_ref[...] += jnp.sum(d2 * e, axis=0, keepdims=True)
        dw1_ref[...] += jnp.sum(d1 * e, axis=0, keepdims=True)
        dw2_ref[...] += jnp.sum(dconv * e, axis=0, keepdims=True)
        dbuf[TM:TM + 8, :] = dbuf[0:8, :]

    def col(j):
        return pl.BlockSpec((TM, D_CONV), functools.partial(lambda i, j: (nb - 1 - i, j), j=j))

    return _pcall(
        body, name=name, grid=(nb,),
        in_specs=[col(0), col(0), col(1), col(2), col(0), pl.BlockSpec((3, D_CONV), lambda i: (0, 0))],
        out_specs=[pl.BlockSpec((TM, 3 * D_CONV), lambda i: (nb - 1 - i, 0))] + [_row_vec(D_CONV)] * 3,
        out_shape=[_sds((tp, 3 * D_CONV), BF)] + [_sds((1, D_CONV), F32)] * 3,
        scratch_shapes=[pltpu.VMEM((TM + 8, D_CONV), F32)], compiler_params=_params(1),
    )(dy, p, p, p, conv, w)


def adamw(w, g, m, v, *, name):
    r, c = w.shape
    tr = r
    for cand in (256, 128, 64, 32, 16, 8):
        if r % cand == 0 and r > cand:
            tr = cand
            break

    def body(w_ref, g_ref, m_ref, v_ref, d_ref, nm_ref, nv_ref):
        gv = g_ref[...]
        nm = ADAM_B1 * m_ref[...] + (1.0 - ADAM_B1) * gv
        nv = ADAM_B2 * v_ref[...] + (1.0 - ADAM_B2) * (gv * gv)
        m_hat = nm / (1.0 - ADAM_B1 ** ADAM_STEP)
        v_hat = nv / (1.0 - ADAM_B2 ** ADAM_STEP)
        d_ref[...] = -ADAM_LR * (m_hat / (jnp.sqrt(v_hat) + ADAM_EPS) + ADAM_WD * w_ref[...])
        nm_ref[...] = nm
        nv_ref[...] = nv

    blk = pl.BlockSpec((tr, c), lambda i: (i, 0))
    return _pcall(
        body, name=name, grid=(r // tr,), in_specs=[blk] * 4, out_specs=[blk] * 3,
        out_shape=[_sds((r, c), F32)] * 3, compiler_params=_params(1),
    )(w, g, m, v)


HBM_SPEC = pl.BlockSpec(memory_space=pltpu.HBM)


def _place():
    return lax.axis_index("x"), lax.axis_index("y"), lax.axis_index("c")


def _other_chips(x, y):
    return [(1 - x, y), (x, 1 - y), (1 - x, 1 - y)]


def _half(ref_or_shape_rows, c):
    return pl.ds(c * (ref_or_shape_rows // 2), ref_or_shape_rows // 2)


def gather_side(shards, layer):
    n = len(shards)

    def program(x_refs, o_refs, send_sems, recv_sems):
        x, y, c = _place()
        me = 2 * x + y
        chips = _other_chips(x, y)

        def copy(sem, src, dst, to):
            return pltpu.make_async_remote_copy(src_ref=src, dst_ref=dst, send_sem=send_sems.at[sem], recv_sem=recv_sems.at[sem],
                                                device_id=to, device_id_type=MESH)

        def src(i):
            return x_refs[i].at[layer, _half(x_refs[i].shape[1], c)]

        def dst(i, slot, cc):
            return o_refs[i].at[slot, _half(o_refs[i].shape[1], cc)]

        sends = [copy(6 * i + k, src(i), dst(i, me, c), (px, py, c)) for i in range(n) for k, (px, py) in enumerate(chips)]
        passed = [copy(6 * i + 3 + k, dst(i, 2 * px + py, c), dst(i, 2 * px + py, c), (x, y, 1 - c))
                  for k, (px, py) in enumerate(chips) for i in range(n)]

        def start():
            for cp in sends:
                cp.start()

        def finish():
            pos = 0
            for k, (px, py) in enumerate(chips):
                for i in range(n):
                    copy(6 * i + k, src(i), dst(i, 2 * px + py, c), (px, py, c)).wait_recv()
                    passed[pos].start()
                    pos += 1
            for k, (px, py) in enumerate(chips):
                for i in range(n):
                    copy(6 * i + 3 + k, dst(i, 2 * px + py, 1 - c), dst(i, 2 * px + py, 1 - c), (x, y, 1 - c)).wait_recv()
            for cp in sends + passed:
                cp.wait_send()

        return start, finish

    prefilled = [jnp.broadcast_to(s[layer][None], (4,) + s.shape[1:]) for s in shards]
    return list(shards), prefilled, 6 * n, program


def scatter_side(pss):
    n = len(pss)

    def program(p_refs, o_refs, send_sems, recv_sems):
        x, y, c = _place()
        me = 2 * x + y
        chips = _other_chips(x, y)

        def copy(i, k, j_src, j_dst, to):
            return pltpu.make_async_remote_copy(src_ref=p_refs[i].at[j_src], dst_ref=o_refs[i].at[j_dst],
                                                send_sem=send_sems.at[3 * i + k], recv_sem=recv_sems.at[3 * i + k],
                                                device_id=to, device_id_type=MESH)

        sends = [copy(i, k, 2 * px + py, me, (px, py, c)) for i in range(n) for k, (px, py) in enumerate(chips)]

        def start():
            for cp in sends:
                cp.start()

        def finish():
            for i in range(n):
                for k, (px, py) in enumerate(chips):
                    copy(i, k, me, 2 * px + py, (px, py, c)).wait_recv()
            for cp in sends:
                cp.wait_send()

        return start, finish

    xi, yi, _ = _place()
    own = jnp.arange(4)[:, None, None] == 2 * xi + yi
    prefilled = [jnp.where(own, p, jnp.zeros_like(p)) for p in pss]
    return list(pss), prefilled, 3 * n, program


def exchange_alone(side, *, name):
    inputs, prefilled, n_sems, program = side
    a, b = len(inputs), len(prefilled)

    def body(*refs):
        start, finish = program(refs[:a], refs[a + b:a + 2 * b], refs[-2], refs[-1])
        start()
        finish()

    return _pcall(
        body, name=name, in_specs=[HBM_SPEC] * (a + b), out_specs=[HBM_SPEC] * b, out_shape=[_sds(p.shape, p.dtype) for p in prefilled],
        input_output_aliases={a + i: i for i in range(b)}, scratch_shapes=[pltpu.SemaphoreType.DMA((n_sems,))] * 2,
    )(*inputs, *prefilled)


def pair_exchange(gs, *, name):
    n = len(gs)

    def body(*refs):
        g_refs, o_refs = refs[:n], refs[n:2 * n]
        send_sems, recv_sems = refs[2 * n:]
        x, y, c = _place()
        cps = [pltpu.make_async_remote_copy(src_ref=g_refs[i].at[:, _half(g_refs[i].shape[1], 1 - c)], dst_ref=o_refs[i],
                                            send_sem=send_sems.at[i], recv_sem=recv_sems.at[i], device_id=(x, y, 1 - c),
                                            device_id_type=MESH)
               for i in range(n)]
        for cp in cps:
            cp.start()
        for cp in cps:
            cp.wait()

    return _pcall(
        body, name=name, in_specs=[HBM_SPEC] * n, out_specs=[HBM_SPEC] * n,
        out_shape=[_sds((4, g.shape[1] // 2, g.shape[2]), g.dtype) for g in gs],
        scratch_shapes=[pltpu.SemaphoreType.DMA((n,)), pltpu.SemaphoreType.DMA((n,))],
    )(*gs)


def _comm_rows(a, b, itemsize):
    return a // 2 if a * b * itemsize > (3 << 19) and a % 16 == 0 else a


def pair_add(g, s1, c_idx, *, name):
    n, a, b = g.shape
    ah = a // 2
    ta = _comm_rows(ah, b, 2)
    nblk = ah // ta

    def body(c_ref, g_ref, s_ref, o_ref):
        o_ref[...] = (g_ref[...].astype(F32) + s_ref[...].astype(F32)).astype(o_ref.dtype)

    grid_spec = pltpu.PrefetchScalarGridSpec(
        num_scalar_prefetch=1, grid=(n, nblk),
        in_specs=[pl.BlockSpec((1, ta, b), lambda j, i, c_ref: (j, c_ref[0] * nblk + i, 0)),
                  pl.BlockSpec((1, ta, b), lambda j, i, c_ref: (j, i, 0))],
        out_specs=pl.BlockSpec((1, ta, b), lambda j, i, c_ref: (j, i, 0)),
    )
    return _pcall(body, name=name, grid_spec=grid_spec, out_shape=_sds((n, ah, b), g.dtype), compiler_params=_params(2))(
        c_idx, g, s1)


def sum_chunks(s2, *, name):
    n, a, b = s2.shape
    ta = _comm_rows(a, b, 4)

    def body(s_ref, o_ref):
        acc = s_ref[0].astype(F32)
        for j in range(1, n):
            acc = acc + s_ref[j].astype(F32)
        o_ref[...] = acc

    return _pcall(
        body, name=name, grid=(a // ta,), in_specs=[pl.BlockSpec((n, ta, b), lambda i: (0, i, 0))],
        out_specs=pl.BlockSpec((ta, b), lambda i: (i, 0)), out_shape=_sds((a, b), F32), compiler_params=_params(1),
    )(s2)


def pair_gather(rcs, *, name):
    n = len(rcs)

    def body(*refs):
        r_refs, o_refs = refs[:n], refs[2 * n:3 * n]
        send_sems, recv_sems = refs[3 * n:]
        x, y, c = _place()

        def copy(i, half):
            return pltpu.make_async_remote_copy(src_ref=r_refs[i], dst_ref=o_refs[i].at[half], send_sem=send_sems.at[i],
                                                recv_sem=recv_sems.at[i], device_id=(x, y, 1 - c), device_id_type=MESH)

        sends = [copy(i, c) for i in range(n)]
        for cp in sends:
            cp.start()
        for i in range(n):
            copy(i, 1 - c).wait_recv()
        for cp in sends:
            cp.wait_send()

    prefilled = [jnp.broadcast_to(r[None], (2,) + r.shape) for r in rcs]
    return _pcall(
        body, name=name, in_specs=[HBM_SPEC] * (2 * n), out_specs=[HBM_SPEC] * n,
        out_shape=[_sds(p.shape, p.dtype) for p in prefilled], input_output_aliases={n + i: i for i in range(n)},
        scratch_shapes=[pltpu.SemaphoreType.DMA((n,)), pltpu.SemaphoreType.DMA((n,))],
    )(*rcs, *prefilled)


def exchange_small(arrs, *, reduce, name):
    n = len(arrs)

    def body(*refs):
        v_refs, o_refs = refs[:n], refs[n:2 * n]
        bufs = refs[2 * n:3 * n] if reduce else o_refs
        send_sems, recv_sems = refs[-2:]
        x, y, c = _place()
        me = 4 * x + 2 * y + c
        for i in range(n):
            bufs[i][me] = v_refs[i][...]

        def peer(k):
            dx, dy, dc = (k >> 2) & 1, (k >> 1) & 1, k & 1
            return (1 - x if dx else x, 1 - y if dy else y, 1 - c if dc else c)

        def copy(i, k, slot):
            return pltpu.make_async_remote_copy(src_ref=v_refs[i], dst_ref=bufs[i].at[slot], send_sem=send_sems.at[7 * i + k - 1],
                                                recv_sem=recv_sems.at[7 * i + k - 1], device_id=peer(k), device_id_type=MESH)

        sends = [copy(i, k, me) for i in range(n) for k in range(1, 8)]
        for cp in sends:
            cp.start()
        for i in range(n):
            for k in range(1, 8):
                px, py, pc = peer(k)
                copy(i, k, 4 * px + 2 * py + pc).wait_recv()
        for cp in sends:
            cp.wait_send()
        if reduce:
            for i in range(n):
                acc = bufs[i][0]
                for d in range(1, 8):
                    acc = acc + bufs[i][d]
                o_refs[i][...] = acc

    vmem = pl.BlockSpec(memory_space=pltpu.VMEM)
    stacked = [(8,) + a.shape for a in arrs]
    return _pcall(
        body, name=name, in_specs=[vmem] * n, out_specs=[vmem] * n,
        out_shape=[_sds(a.shape if reduce else s, F32) for a, s in zip(arrs, stacked)],
        scratch_shapes=([pltpu.VMEM(s, F32) for s in stacked] if reduce else [])
        + [pltpu.SemaphoreType.DMA((7 * n,)), pltpu.SemaphoreType.DMA((7 * n,))],
    )(*arrs)


def _pad_rows(n, mult):
    return -(-n // mult) * mult


def _chip_major(g, b):
    return g.reshape(g.shape[0], 4, b).transpose(1, 0, 2)


def _rope_tables(tp):
    inv_freq = 1.0 / (ROPE_BASE ** (jnp.arange(0, QK_ROPE, 2, dtype=F32) / QK_ROPE))
    ang = jnp.arange(tp, dtype=F32)[:, None] * inv_freq[None, :]
    cos, sin = jnp.cos(ang), jnp.sin(ang)
    one = lambda n: jnp.ones((tp, n), F32)
    zero = lambda n: jnp.zeros((tp, n), F32)
    cq = jnp.concatenate([one(128), cos, cos, one(96)], axis=1)
    s1q = jnp.concatenate([zero(144), sin, zero(96)], axis=1)
    s2q = jnp.concatenate([zero(128), -sin, zero(112)], axis=1)
    ck = jnp.concatenate([cos, cos, zero(96)], axis=1)
    s1k = jnp.concatenate([zero(16), sin, zero(96)], axis=1)
    s2k = jnp.concatenate([-sin, zero(112)], axis=1)
    fwd = (cq * (ATT_SCALE * LOG2E), s1q * (ATT_SCALE * LOG2E), s2q * (ATT_SCALE * LOG2E), ck, s1k, s2k)
    bwd = (cq * ATT_SCALE, -s1q * ATT_SCALE, -s2q * ATT_SCALE, ck, -s1k, -s2k)
    return fwd, bwd


def _pad_w_in(w):
    return jnp.concatenate([w[:, :1952], jnp.zeros((w.shape[0], 96), w.dtype), w[:, 1952:]], axis=1)


def _pad_w_uq(w):
    w = w.reshape(Q_LORA, MLA_HEADS, QK_NOPE + QK_ROPE)
    z = lambda n: jnp.zeros((Q_LORA, MLA_HEADS, n), w.dtype)
    return jnp.concatenate([w[..., :QK_NOPE], z(64), w[..., QK_NOPE:], z(96)], axis=-1).reshape(Q_LORA, MLA_HEADS * Q_PAD)


def _unpad_w_uq(w):
    w = w.reshape(Q_LORA, MLA_HEADS, Q_PAD)
    return jnp.concatenate([w[..., :QK_NOPE], w[..., 128:128 + QK_ROPE]], axis=-1).reshape(Q_LORA, MLA_HEADS * (QK_NOPE + QK_ROPE))


def _pad_w_br_mla(w):
    w = w.reshape(MLA_HEADS, V_HEAD, D_MODEL)
    return jnp.concatenate([jnp.zeros_like(w), w], axis=1).reshape(MLA_HEADS * KV_PAD, D_MODEL)


def _unpad_w_br_mla(w):
    return w.reshape(MLA_HEADS, KV_PAD, D_MODEL)[:, V_HEAD:].reshape(MLA_HEADS * V_HEAD, D_MODEL)


def _layer_fwd(l, st, xprev, gp, bp, hb, w, tabs, side=None):
    ln_g, ln_b = w["ln_g"], w["ln_b"]
    lg = lambda k: ln_g[l, k][None]
    lb = lambda k: ln_b[l, k][None]
    s = {}
    s["x0"], s["gp0"], s["bp0"], s["hb0"] = xprev, gp, bp, hb
    s["g1"], s["u1"], s["a1"] = ffn_up(hb, w["ffn1_w_up"][l], name="ffn_up")
    s["xh1"], s["rs1"], s["hb1"] = down_ln(s["a1"], w["ffn1_w_down"][l], xprev, gp, bp, lg(0), lb(0), name="ffn_down_ln")
    s["p"] = mm_rows([(s["hb1"], w["mix_w_in"][l], False, 0)], D_IN_PAD, name="mix_in", tn=1024)
    gq, gkv = w["q_norm_g"][l][None], w["kv_norm_g"][l][None]
    s["cqn"], s["ckvn"], s["q2"], s["kv"], s["kr"] = mla_prep(s["p"], gq, gkv, w["w_uq"][l], w["w_ukv"][l], tabs, name="mla_prep")
    s["o2"], s["lse"], *extras = attn_fwd(s["q2"], s["kv"], s["kr"], name="attn_fwd", side=side)
    s["ycv"], s["conv"] = conv_fwd(s["p"], w["conv_w"][l], name="conv_fwd")
    s["bc"], s["bm"], s["mg"], s["xh2"], s["rs2"], s["hb2"] = merge_out_ln(
        s["ycv"], s["o2"], s["p"], w["mix_b_gate"][l], w["w_br_conv"][l], w["w_br_mla"][l], w["w_o"][l],
        s["xh1"], lg(0), lb(0), lg(1), lb(1), name="merge_out_ln")
    s["g2"], s["u2"], s["a2"] = ffn_up(s["hb2"], w["ffn2_w_up"][l], name="ffn_up")
    s["xh3"], s["rs3"], s["hb3"] = down_ln(s["a2"], w["ffn2_w_down"][l], s["xh2"], lg(1), lb(1), lg(2), lb(2), name="ffn_down_ln")
    st.append(s)
    return (s["xh3"], lg(2), lb(2), s["hb3"]), extras


def _ffn_bwd(dh, w_up, w_down, ln_gain, hb_in, gate, up, act, xh, rs):
    dz, dzb, dgam, dbet = ln_bwd(dh, xh, rs, ln_gain, branch_scale=0.5, name="ln_bwd")
    d_wd = tn_mm(act, dzb, tm=D_FF // 2, name="dw_down", shard=("rows", D_FF // 4))
    dgate, dup = ffn_down_bwd(dzb, w_down, gate, up, name="ffn_down_bwd")
    d_w = tn_mm(hb_in, dgate, tm=512, name="dw_up", shard=("cols", D_FF // 2), slot0=0)
    d_w = tn_mm(hb_in, dup, tm=512, name="dw_up", shard=("cols", D_FF // 2), slot0=2, dst=d_w)
    dh_in = mm_rows([(dgate, w_up, True, 0), (dup, w_up, True, 1)], D_MODEL, name="ffn_up_bwd", tn=512, addend=dz, add_scale=ALPHA)
    return dh_in, d_w, d_wd, dgam, dbet


def _layer_bwd(l, s, dh, w, tabs_bwd, side=None):
    ln_g = w["ln_g"]
    lg = lambda k: ln_g[l, k][None]
    g = {}
    dh, g["ffn2_w_up"], g["ffn2_w_down"], dg2, db2 = _ffn_bwd(
        dh, w["ffn2_w_up"][l], w["ffn2_w_down"][l], lg(2), s["hb2"], s["g2"], s["u2"], s["a2"], s["xh3"], s["rs3"])
    dz, dzb, dg1, db1 = ln_bwd(dh, s["xh2"], s["rs2"], lg(1), branch_scale=1.0, name="ln_bwd")
    g["w_o"] = tn_mm(s["mg"], dzb, tm=1024, name="dw_o", shard=("rows", D_MODEL // 4))
    dbc, dbm, dgg, dycv, do2, dl, g["mix_b_gate"] = merge_bwd(
        dzb, w["w_o"][l], s["bc"], s["bm"], s["p"], w["mix_b_gate"][l], w["w_br_conv"][l], w["w_br_mla"][l], s["o2"], name="merge_bwd")
    g["w_br_conv"] = tn_mm(s["ycv"], dbc, tm=512, name="dw_br_conv", shard=("cols", D_MODEL // 4))
    g["w_br_mla"] = _chip_major(_unpad_w_br_mla(tn_mm(s["o2"], dbm, tm=1024, name="dw_br_mla")), D_MODEL // 4)
    dq2, dkv, dkr, *extras = attn_bwd(s["q2"], s["kv"], s["kr"], do2, s["lse"], dl, name="attn_bwd", side=side)
    gq, gkv = w["q_norm_g"][l][None], w["kv_norm_g"][l][None]
    dqb, dsm, g["q_norm_g"], g["kv_norm_g"] = mla_prep_bwd(dq2, dkv, dkr, s["p"], gq, gkv, w["w_uq"][l], w["w_ukv"][l], tabs_bwd,
                                                           name="mla_prep_bwd")
    g["w_uq"] = _chip_major(_unpad_w_uq(tn_mm(s["cqn"], dqb, tm=Q_LORA, name="dw_uq")), MLA_HEADS * (QK_NOPE + QK_ROPE) // 4)
    g["w_ukv"] = tn_mm(s["ckvn"], dkv, tm=KV_LORA, name="dw_ukv", shard=("cols", MLA_HEADS * KV_PAD // 4))
    dbch, dw0, dw1, dw2 = conv_bwd(dycv, s["p"], s["conv"], w["conv_w"][l], name="conv_bwd")
    g["conv_w"] = jnp.concatenate([dw0, dw1, dw2], axis=0)
    w_in = w["mix_w_in"][l]
    d_bch = tn_mm(s["hb1"], dbch, tm=512, name="dw_in_bch")
    d_sm = tn_mm(s["hb1"], dsm, tm=1024, name="dw_in_sm")
    d_gg = tn_mm(s["hb1"], dgg, tm=512, name="dw_in_gg")
    g["mix_w_in"] = _chip_major(jnp.concatenate([d_bch, d_sm[:, :1952 - 1536], d_gg], axis=1), D_IN // 4)
    dh = mm_rows([(dbch, w_in, True, 0), (dsm, w_in, True, 3), (dgg, w_in, True, 1)], D_MODEL, name="mix_in_bwd", tn=512,
                 addend=dz, add_scale=ALPHA)
    dh, g["ffn1_w_up"], g["ffn1_w_down"], dg0, db0 = _ffn_bwd(
        dh, w["ffn1_w_up"][l], w["ffn1_w_down"][l], lg(0), s["hb0"], s["g1"], s["u1"], s["a1"], s["xh1"], s["rs1"])
    g["ln_g"] = jnp.concatenate([dg0, dg1, dg2], axis=0)
    g["ln_b"] = jnp.concatenate([db0, db1, db2], axis=0)
    return dh, g, extras


BIG = ("ffn1_w_up", "ffn1_w_down", "mix_w_in", "w_uq", "w_ukv", "w_br_conv", "w_br_mla", "w_o", "ffn2_w_up", "ffn2_w_down")
BIG_AXIS = (2, 1, 2, 2, 2, 2, 2, 1, 2, 1)
SMALL_SHARDED = ("meta_tokens", "mix_b_gate", "conv_w", "ln_g", "ln_b")
SMALL_REPLICATED = ("q_norm_g", "kv_norm_g")
WEIGHTS = ("meta_tokens", "ffn1_w_up", "ffn1_w_down", "mix_w_in", "mix_b_gate", "conv_w", "q_norm_g", "w_uq", "kv_norm_g", "w_ukv",
           "w_br_conv", "w_br_mla", "w_o", "ffn2_w_up", "ffn2_w_down", "ln_g", "ln_b")


def _view2d(a):
    return a.reshape(-1, a.shape[-1])


def _local_grads(x_row, target_row, w, fetch_layer1=None, send_layer1=None):
    seq = x_row.shape[0]
    t_real = N_META + seq
    tp = _pad_rows(t_real, TM)
    pad = tp - t_real
    h0 = jnp.concatenate([w["meta_tokens"], x_row, jnp.zeros((pad, D_MODEL), F32)], axis=0)
    target_p = jnp.concatenate([jnp.zeros((N_META, D_MODEL), F32), target_row, jnp.zeros((pad, D_MODEL), F32)], axis=0)
    tabs, tabs_bwd = _rope_tables(tp)
    ones = jnp.ones((1, D_MODEL), F32)
    zeros = jnp.zeros((1, D_MODEL), F32)
    saved = []
    cur = (h0, ones, zeros, h0.astype(BF))
    cur, fetched = _layer_fwd(0, saved, *cur, w, tabs, side=fetch_layer1[0] if fetch_layer1 else None)
    if fetch_layer1:
        fetch_layer1[1](fetched)
    cur, _ = _layer_fwd(1, saved, *cur, w, tabs)
    dh, loss_acc = loss_grad(cur[0], cur[1], cur[2], target_p, seq, name="loss_grad")
    dh, g1, _ = _layer_bwd(1, saved[1], dh, w, tabs_bwd)
    dh, g0, sent = _layer_bwd(0, saved[0], dh, w, tabs_bwd, side=send_layer1(g1) if send_layer1 else None)
    return loss_acc, dh[N_META:t_real], dh[:N_META], [g0, g1], sent


def kernel(x, meta_tokens, ffn1_w_up, ffn1_w_down, mix_w_in, mix_b_gate, conv_w, q_norm_g, w_uq, kv_norm_g, w_ukv, w_br_conv, w_br_mla, w_o, ffn2_w_up, ffn2_w_down, ln_g, ln_b, loss_target, m_meta_tokens, m_ffn1_w_up, m_ffn1_w_down, m_mix_w_in, m_mix_b_gate, m_conv_w, m_q_norm_g, m_w_uq, m_kv_norm_g, m_w_ukv, m_w_br_conv, m_w_br_mla, m_w_o, m_ffn2_w_up, m_ffn2_w_down, m_ln_g, m_ln_b, v_meta_tokens, v_ffn1_w_up, v_ffn1_w_down, v_mix_w_in, v_mix_b_gate, v_conv_w, v_q_norm_g, v_w_uq, v_kv_norm_g, v_w_ukv, v_w_br_conv, v_w_br_mla, v_w_o, v_ffn2_w_up, v_ffn2_w_down, v_ln_g, v_ln_b):
    local = dict(meta_tokens=meta_tokens, ffn1_w_up=ffn1_w_up, ffn1_w_down=ffn1_w_down, mix_w_in=mix_w_in, mix_b_gate=mix_b_gate,
                 conv_w=conv_w, q_norm_g=q_norm_g, w_uq=w_uq, kv_norm_g=kv_norm_g, w_ukv=w_ukv, w_br_conv=w_br_conv,
                 w_br_mla=w_br_mla, w_o=w_o, ffn2_w_up=ffn2_w_up, ffn2_w_down=ffn2_w_down, ln_g=ln_g, ln_b=ln_b)
    mom_m = dict(zip(WEIGHTS, (m_meta_tokens, m_ffn1_w_up, m_ffn1_w_down, m_mix_w_in, m_mix_b_gate, m_conv_w, m_q_norm_g, m_w_uq,
                               m_kv_norm_g, m_w_ukv, m_w_br_conv, m_w_br_mla, m_w_o, m_ffn2_w_up, m_ffn2_w_down, m_ln_g, m_ln_b)))
    mom_v = dict(zip(WEIGHTS, (v_meta_tokens, v_ffn1_w_up, v_ffn1_w_down, v_mix_w_in, v_mix_b_gate, v_conv_w, v_q_norm_g, v_w_uq,
                               v_kv_norm_g, v_w_ukv, v_w_br_conv, v_w_br_mla, v_w_o, v_ffn2_w_up, v_ffn2_w_down, v_ln_g, v_ln_b)))
    xi, yi, ci = _place()
    chip = 2 * xi + yi

    shards = [local[n].astype(BF) for n in BIG]
    w = {n: [None] * DEPTH for n in BIG}

    def install(l, gathered):
        for n, ax, g in zip(BIG, BIG_AXIS, gathered):
            w[n][l] = jnp.concatenate([g[j] for j in range(4)], axis=ax - 1)
        w["mix_w_in"][l] = _pad_w_in(w["mix_w_in"][l])
        w["w_uq"][l] = _pad_w_uq(w["w_uq"][l])
        w["w_br_mla"][l] = _pad_w_br_mla(w["w_br_mla"][l])

    install(0, exchange_alone(gather_side(shards, 0), name="gather_weights"))
    stacked = exchange_small([_view2d(local[n]) for n in SMALL_SHARDED], reduce=False, name="gather_small")
    for n, st in zip(SMALL_SHARDED, stacked):
        full = jnp.concatenate([st[2 * j] for j in range(4)], axis=-1)
        w[n] = full.reshape(local[n].shape[:-1] + (full.shape[-1],))
    for n in SMALL_REPLICATED:
        w[n] = local[n]

    c_idx = jnp.reshape(ci, (1,)).astype(jnp.int32)

    def pair_sums(g):
        glist = [g[n] for n in BIG]
        from_sibling = pair_exchange(glist, name="rs_pair_exchange")
        return [pair_add(a, s, c_idx, name="rs_pair_add") for a, s in zip(glist, from_sibling)]

    def finish_reduce(from_chips):
        reduced = pair_gather([sum_chunks(r, name="rs_sum") for r in from_chips], name="rs_pair_gather")
        return [r.reshape(s.shape[1:]) for r, s in zip(reduced, shards)]

    loss_acc, grad_x, d_meta, grads, from_chips1 = _local_grads(
        x[0], loss_target[0], w, fetch_layer1=(gather_side(shards, 1), lambda got: install(1, got)),
        send_layer1=lambda g1: scatter_side(pair_sums(g1)))
    grad_x = grad_x[None]
    from_chips0 = exchange_alone(scatter_side(pair_sums(grads[0])), name="rs_chip_scatter")
    reduced = [finish_reduce(from_chips0), finish_reduce(from_chips1)]
    gshard = {n: jnp.stack([reduced[l][i] for l in range(DEPTH)]) for i, n in enumerate(BIG)}

    small_names = SMALL_SHARDED + SMALL_REPLICATED
    gsmall = {n: jnp.concatenate([grads[l][n] for l in range(DEPTH)], axis=0) for n in small_names if n != "meta_tokens"}
    gsmall["meta_tokens"] = d_meta
    small_red = exchange_small([gsmall[n] for n in small_names] + [loss_acc], reduce=True, name="reduce_small")
    loss = small_red[-1][0, 0]
    for n, full in zip(small_names, small_red[:-1]):
        if n in SMALL_SHARDED:
            sh = local[n].shape[-1]
            full = lax.dynamic_slice_in_dim(full, chip * sh, sh, axis=1)
        gshard[n] = full.reshape(local[n].shape)

    delta, new_m, new_v = {}, {}, {}
    for n in WEIGHTS:
        shape = local[n].shape
        d, nm, nv = adamw(_view2d(local[n]), _view2d(gshard[n]), _view2d(mom_m[n]), _view2d(mom_v[n]), name="adamw")
        delta[n], new_m[n], new_v[n] = d.reshape(shape), nm.reshape(shape), nv.reshape(shape)
    return (loss, grad_x, *[gshard[n] for n in WEIGHTS], *[delta[n] for n in WEIGHTS], *[new_m[n] for n in WEIGHTS],
            *[new_v[n] for n in WEIGHTS])
```

```python
import functools

import jax
import jax.numpy as jnp
from jax import lax
from jax.experimental import pallas as pl
from jax.experimental.pallas import tpu as pltpu

F32 = jnp.float32
BF = jnp.bfloat16
MESH = pl.DeviceIdType.MESH

D_MODEL = 1024
DEPTH = 2
N_META = 16
D_CONV = 512
MLA_HEADS = 8
QK_NOPE = 64
QK_ROPE = 32
V_HEAD = 64
Q_LORA = 256
KV_LORA = 128
ROPE_BASE = 10000.0
NEG_INF = -1e30
D_FF = 2816
ALPHA = (2 * DEPTH) ** 0.25
LN_EPS = 1e-5
RMS_EPS = 1e-6
ATT_SCALE = (QK_NOPE + QK_ROPE) ** -0.5
LOG2E = 1.4426950408889634
LN2 = 0.6931471805599453
D_IN = 4000
D_IN_PAD = 4096
Q_PAD = 256
KV_PAD = 128

ADAM_LR = 0.001
ADAM_B1 = 0.9
ADAM_B2 = 0.999
ADAM_EPS = 1e-08
ADAM_WD = 0.01
ADAM_STEP = 10

TM = 768
TMH = 384
LANES = 128
COMM_COLS = 512
COMM_ROW_BLOCK = 1472
VMEM_LIMIT_BYTES = 50 * 1024 * 1024

NT = (((1,), (1,)), ((), ()))
TN = (((0,), (0,)), ((), ()))


def _pcall(body, **kw):
    return pl.pallas_call(body, **kw)


def _params(n_axes):
    return pltpu.CompilerParams(dimension_semantics=("arbitrary",) * n_axes, vmem_limit_bytes=VMEM_LIMIT_BYTES)


def _sds(shape, dtype):
    return jax.ShapeDtypeStruct(shape, dtype)


def mm_rows(pairs, n_out, *, name, tn=None, addend=None, add_scale=1.0, out_dtype=F32):
    tp = pairs[0][0].shape[0]
    tn = tn or n_out
    in_specs, args = [], []
    for a, b, nt, kb in pairs:
        k = a.shape[1]
        in_specs.append(pl.BlockSpec((TM, k), lambda i, j: (i, 0)))
        if nt:
            in_specs.append(pl.BlockSpec((tn, k), functools.partial(lambda i, j, kb: (j, kb), kb=kb)))
        else:
            in_specs.append(pl.BlockSpec((k, tn), lambda i, j: (0, j)))
        args += [a, b]
    if addend is not None:
        in_specs.append(pl.BlockSpec((TM, tn), lambda i, j: (i, j)))
        args.append(addend)
    n_pairs = len(pairs)
    nts = [p[2] for p in pairs]

    def body(*refs):
        o_ref = refs[-1]
        acc = None
        for p in range(n_pairs):
            a = refs[2 * p][...].astype(BF)
            b = refs[2 * p + 1][...]
            d = lax.dot_general(a, b, NT if nts[p] else (((1,), (0,)), ((), ())), preferred_element_type=F32)
            acc = d if acc is None else acc + d
        if addend is not None:
            acc = acc + add_scale * refs[2 * n_pairs][...]
        o_ref[...] = acc.astype(o_ref.dtype)

    return _pcall(
        body, name=name, grid=(tp // TM, n_out // tn), in_specs=in_specs,
        out_specs=pl.BlockSpec((TM, tn), lambda i, j: (i, j)), out_shape=_sds((tp, n_out), out_dtype),
        compiler_params=_params(2),
    )(*args)


def tn_mm(a, b, *, tm, name, out_dtype=BF, shard=None, slot0=0, dst=None):
    tp, m = a.shape
    n = b.shape[1]
    nk = tp // TM
    if shard is None:
        pieces, out_block, out_index, out_full = 1, (tm, n), (lambda i, k: (i, 0)), (m, n)
    elif shard[0] == "cols":
        pieces = n // shard[1]
        out_block, out_full = (pieces, tm, shard[1]), (4, m, shard[1])
        out_index = lambda i, k: (slot0 // pieces, i, 0)
    else:
        pieces = tm // shard[1]
        out_block, out_full = (pieces, shard[1], n), (4, m // 4, n)
        out_index = lambda i, k: (i, 0, 0)

    def body(a_ref, b_ref, *rest):
        o_ref, acc_ref = rest[-2], rest[-1]
        k = pl.program_id(1)

        @pl.when(k == 0)
        def _():
            acc_ref[...] = jnp.zeros_like(acc_ref)

        acc_ref[...] += lax.dot_general(a_ref[...].astype(BF), b_ref[...].astype(BF), TN, preferred_element_type=F32)

        @pl.when(k == nk - 1)
        def _():
            if shard is None:
                o_ref[...] = acc_ref[...].astype(o_ref.dtype)
            elif shard[0] == "cols":
                for j in range(pieces):
                    o_ref[j] = acc_ref[:, j * shard[1]:(j + 1) * shard[1]].astype(o_ref.dtype)
            else:
                for j in range(pieces):
                    o_ref[j] = acc_ref[j * shard[1]:(j + 1) * shard[1], :].astype(o_ref.dtype)

    in_specs = [pl.BlockSpec((TM, tm), lambda i, k: (k, i)), pl.BlockSpec((TM, n), lambda i, k: (k, 0))]
    args = [a, b]
    aliases = {}
    if dst is not None:
        in_specs.append(pl.BlockSpec(memory_space=pl.ANY))
        args.append(dst)
        aliases = {2: 0}
    return _pcall(
        body, name=name, grid=(m // tm, nk), in_specs=in_specs, out_specs=pl.BlockSpec(out_block, out_index),
        out_shape=_sds(out_full, out_dtype), input_output_aliases=aliases,
        scratch_shapes=[pltpu.VMEM((tm, n), F32)], compiler_params=_params(2),
    )(*args)


def _ln_store(z, g_ref, b_ref, xh_ref, rs_ref, hb_ref):
    mu = jnp.mean(z, axis=-1, keepdims=True)
    zc = z - mu
    var = jnp.mean(zc * zc, axis=-1, keepdims=True)
    rstd = lax.rsqrt(var + LN_EPS)
    xh = zc * rstd
    xh_ref[...] = xh
    rs_ref[...] = rstd
    hb_ref[...] = (xh * g_ref[...] + b_ref[...]).astype(BF)


def _ln_out(tp, tm=TM):
    specs = [pl.BlockSpec((tm, D_MODEL), lambda i: (i, 0)), pl.BlockSpec((tm, 1), lambda i: (i, 0)),
             pl.BlockSpec((tm, D_MODEL), lambda i: (i, 0))]
    shapes = [_sds((tp, D_MODEL), F32), _sds((tp, 1), F32), _sds((tp, D_MODEL), BF)]
    return specs, shapes


def _row_vec(n):
    return pl.BlockSpec((1, n), lambda i: (0, 0))


def ffn_up(hb, wup, *, name, side=None):
    tp = hb.shape[0]
    tn = D_FF // 2
    nj = D_FF // tn

    def body(in_refs, out_refs, scratch):
        h_ref, wg_ref, wu_ref = in_refs
        g_ref, u_ref, a_ref = out_refs
        h = h_ref[...]
        g = jnp.dot(h, wg_ref[...], preferred_element_type=F32)
        u = jnp.dot(h, wu_ref[...], preferred_element_type=F32)
        g_ref[...] = g.astype(BF)
        u_ref[...] = u.astype(BF)
        a_ref[...] = (g * jax.nn.sigmoid(g) * u).astype(BF)

    blk = pl.BlockSpec((TM, tn), lambda i, j: (i, j))
    return _side_call(
        body, side, name=name, grid=(tp // TM, nj),
        in_specs=[pl.BlockSpec((TM, D_MODEL), lambda i, j: (i, 0)), pl.BlockSpec((D_MODEL, tn), lambda i, j: (0, j)),
                  pl.BlockSpec((D_MODEL, tn), lambda i, j: (0, j + nj))],
        out_specs=[blk, blk, blk], out_shape=[_sds((tp, D_FF), BF)] * 3, scratch_shapes=[], args=(hb, wup, wup),
    )


def down_ln(a, wd, xprev, gp, bp, g, b, *, name):
    tp = a.shape[0]

    def body(a_ref, wd_ref, xp_ref, gp_ref, bp_ref, g_ref, b_ref, xh_ref, rs_ref, hb_ref):
        f = jnp.dot(a_ref[...], wd_ref[...], preferred_element_type=F32)
        hprev = xp_ref[...] * gp_ref[...] + bp_ref[...]
        _ln_store(ALPHA * hprev + 0.5 * f, g_ref, b_ref, xh_ref, rs_ref, hb_ref)

    out_specs, out_shape = _ln_out(tp)
    return _pcall(
        body, name=name, grid=(tp // TM,),
        in_specs=[pl.BlockSpec((TM, D_FF), lambda i: (i, 0)), pl.BlockSpec((D_FF, D_MODEL), lambda i: (0, 0)),
                  pl.BlockSpec((TM, D_MODEL), lambda i: (i, 0))] + [_row_vec(D_MODEL)] * 4,
        out_specs=out_specs, out_shape=out_shape, compiler_params=_params(1),
    )(a, wd, xprev, gp, bp, g, b)


def _rope(x, c, s1, s2, reps):
    n = x.shape[1]
    if reps > 1:
        c, s1, s2 = (jnp.tile(t, (1, reps)) for t in (c, s1, s2))
    return x * c + pltpu.roll(x, 16, 1) * s1 + pltpu.roll(x, n - 16, 1) * s2


def _rms(x, g):
    r = lax.rsqrt(jnp.mean(x * x, axis=-1, keepdims=True) + RMS_EPS)
    return x * r * g, r


def mla_prep(p, gq, gkv, wuq_p, wukv, tabs, *, name):
    tp = p.shape[0]
    nh = MLA_HEADS

    def body(cq_ref, ckv_ref, kr_ref, gq_ref, gkv_ref, wuq_ref, wukv_ref, cq_t, s1q_t, s2q_t, ck_t, s1k_t, s2k_t,
             cqn_ref, ckvn_ref, q2_ref, kv_ref, krr_ref):
        cqn, _ = _rms(cq_ref[...], gq_ref[...])
        ckvn, _ = _rms(ckv_ref[...], gkv_ref[...])
        cqn = cqn.astype(BF)
        ckvn = ckvn.astype(BF)
        cqn_ref[...] = cqn
        ckvn_ref[...] = ckvn
        q = jnp.dot(cqn, wuq_ref[...], preferred_element_type=F32)
        q2_ref[...] = _rope(q, cq_t[...], s1q_t[...], s2q_t[...], nh).astype(BF)
        kv_ref[...] = jnp.dot(ckvn, wukv_ref[...], preferred_element_type=F32).astype(BF)
        krr_ref[...] = _rope(kr_ref[...], ck_t[...], s1k_t[...], s2k_t[...], 1).astype(BF)

    def rows(n, col=0):
        return pl.BlockSpec((TMH, n), functools.partial(lambda i, col: (i, col), col=col))

    return _pcall(
        body, name=name, grid=(tp // TMH,),
        in_specs=[rows(Q_LORA, 1536 // Q_LORA), rows(KV_LORA, 1792 // KV_LORA), rows(LANES, 1920 // LANES),
                  _row_vec(Q_LORA), _row_vec(KV_LORA),
                  pl.BlockSpec((Q_LORA, nh * Q_PAD), lambda i: (0, 0)), pl.BlockSpec((KV_LORA, nh * KV_PAD), lambda i: (0, 0)),
                  rows(Q_PAD), rows(Q_PAD), rows(Q_PAD), rows(LANES), rows(LANES), rows(LANES)],
        out_specs=[rows(Q_LORA), rows(KV_LORA), rows(nh * Q_PAD), rows(nh * KV_PAD), rows(LANES)],
        out_shape=[_sds((tp, Q_LORA), BF), _sds((tp, KV_LORA), BF), _sds((tp, nh * Q_PAD), BF),
                   _sds((tp, nh * KV_PAD), BF), _sds((tp, LANES), BF)],
        compiler_params=_params(1),
    )(p, p, p, gq, gkv, wuq_p, wukv, *tabs)


def _causal_mask(s):
    qpos = lax.broadcasted_iota(jnp.int32, (TM, TM), 0)
    kpos = lax.broadcasted_iota(jnp.int32, (TM, TM), 1)
    return jnp.where(kpos <= qpos, s, NEG_INF)


def _key_rows(k):
    return pl.ds(pl.multiple_of(k * TM, TM), TM)


def _pipelined_key_blocks(n, prefetch, process):
    prefetch(0, 0)

    def pair(j, carry):
        prefetch(2 * j + 1, 1)
        process(2 * j, 0, False)
        prefetch(2 * j + 2, 0)
        process(2 * j + 1, 1, False)
        return carry

    lax.fori_loop(0, n // 2, pair, 0)

    @pl.when(n % 2 == 1)
    def _():
        prefetch(n, 1)
        process(n - 1, 0, False)
        process(n, 1, True)

    @pl.when(n % 2 == 0)
    def _():
        process(n, 0, True)


def _side_call(body_main, side, *, name, grid, in_specs, out_specs, out_shape, scratch_shapes, args):
    n_in, n_out, n_scr = len(in_specs), len(out_specs), len(scratch_shapes)
    s_in, s_pre, n_sems, program = side if side is not None else ((), (), 0, None)
    a, b = len(s_in), len(s_pre)

    def body(*refs):
        in_refs = refs[:n_in]
        out_refs = refs[n_in + a + b:n_in + a + b + n_out]
        scr = refs[n_in + a + 2 * b + n_out:n_in + a + 2 * b + n_out + n_scr]
        if side is not None:
            side_in = refs[n_in:n_in + a]
            side_out = refs[n_in + a + b + n_out:n_in + a + 2 * b + n_out]
            start, finish = program(side_in, side_out, refs[-2], refs[-1])

            @pl.when((pl.program_id(0) == 0) & (pl.program_id(1) == 0))
            def _():
                start()

        body_main(in_refs, out_refs, scr)
        if side is not None:
            @pl.when((pl.program_id(0) == grid[0] - 1) & (pl.program_id(1) == grid[1] - 1))
            def _():
                finish()

    sems = [pltpu.SemaphoreType.DMA((n_sems,))] * 2 if side is not None else []
    return _pcall(
        body, name=name, grid=grid, in_specs=list(in_specs) + [HBM_SPEC] * (a + b), out_specs=list(out_specs) + [HBM_SPEC] * b,
        out_shape=list(out_shape) + [_sds(p.shape, p.dtype) for p in s_pre],
        input_output_aliases={n_in + a + i: n_out + i for i in range(b)},
        scratch_shapes=list(scratch_shapes) + sems, compiler_params=_params(2),
    )(*args, *s_in, *s_pre)


def attn_fwd(q2, kv, kr, *, name, side=None):
    tp = q2.shape[0]
    nh = MLA_HEADS
    nb = tp // TM
    rep = TM // LANES

    def body(in_refs, out_refs, scratch):
        q_ref, kv_ref, kr_ref = in_refs
        o_ref, lse_ref = out_refs
        m_ref, l_ref, acc_ref, s0_ref, s1_ref = scratch
        qi = pl.program_id(1)
        s_refs = (s0_ref, s1_ref)
        m_ref[...] = jnp.full_like(m_ref, NEG_INF)
        l_ref[...] = jnp.zeros_like(l_ref)
        acc_ref[...] = jnp.zeros_like(acc_ref)

        def prefetch(k, slot):
            k2 = jnp.concatenate([kv_ref[_key_rows(k), :], kr_ref[_key_rows(k), :]], axis=1)
            s_refs[slot][...] = lax.dot_general(q_ref[...], k2, NT, preferred_element_type=F32)

        def process(k, slot, diagonal):
            s = s_refs[slot][...]
            if diagonal:
                s = _causal_mask(s)
            m_prev = m_ref[...]
            m_new = jnp.maximum(m_prev, jnp.max(s, axis=1, keepdims=True))
            alpha = jnp.exp2(m_prev - m_new)
            p = jnp.exp2(s - jnp.tile(m_new, (1, rep)))
            l_ref[...] = alpha * l_ref[...] + jnp.sum(p, axis=1, keepdims=True)
            acc_ref[...] = alpha * acc_ref[...] + jnp.dot(p.astype(BF), kv_ref[_key_rows(k), :], preferred_element_type=F32)
            m_ref[...] = m_new

        _pipelined_key_blocks(qi, prefetch, process)
        o_ref[...] = (acc_ref[...] / l_ref[...]).astype(BF)
        lse_ref[...] = m_ref[...] + jnp.log2(l_ref[...])

    return _side_call(
        body, side, name=name, grid=(nh, nb),
        in_specs=[pl.BlockSpec((TM, Q_PAD), lambda h, qi: (qi, h)), pl.BlockSpec((tp, KV_PAD), lambda h, qi: (0, h)),
                  pl.BlockSpec((tp, LANES), lambda h, qi: (0, 0))],
        out_specs=[pl.BlockSpec((TM, KV_PAD), lambda h, qi: (qi, h)), pl.BlockSpec((TM, LANES), lambda h, qi: (qi, h))],
        out_shape=[_sds((tp, nh * KV_PAD), BF), _sds((tp, nh * LANES), F32)],
        scratch_shapes=[pltpu.VMEM((TM, LANES), F32)] * 3 + [pltpu.VMEM((TM, TM), F32)] * 2, args=(q2, kv, kr),
    )


def conv_fwd(p, w, *, name):
    tp = p.shape[0]

    def body(b_ref, c_ref, h_ref, w_ref, y_ref, cv_ref, ebuf):
        i = pl.program_id(0)

        @pl.when(i == 0)
        def _():
            ebuf[0:8, :] = jnp.zeros((8, D_CONV), F32)

        e = c_ref[...] * h_ref[...]
        ebuf[8:8 + TM, :] = e
        w_all = w_ref[...]
        conv = w_all[0:1] * ebuf[pl.ds(6, TM), :] + w_all[1:2] * ebuf[pl.ds(7, TM), :] + w_all[2:3] * e
        cv_ref[...] = conv.astype(BF)
        y_ref[...] = (b_ref[...] * conv).astype(BF)
        ebuf[0:8, :] = ebuf[TM:TM + 8, :]

    def col(j):
        return pl.BlockSpec((TM, D_CONV), functools.partial(lambda i, j: (i, j), j=j))

    return _pcall(
        body, name=name, grid=(tp // TM,),
        in_specs=[col(0), col(1), col(2), pl.BlockSpec((3, D_CONV), lambda i: (0, 0))],
        out_specs=[col(0), col(0)], out_shape=[_sds((tp, D_CONV), BF)] * 2,
        scratch_shapes=[pltpu.VMEM((TM + 8, D_CONV), F32)], compiler_params=_params(1),
    )(p, p, p, w)


def merge_out_ln(ycv, o2, p, bg, wbc, wbm_p, wo, xprev, gp, bp, g, b, *, name):
    tp = ycv.shape[0]

    def body(y_ref, o_ref, gc_ref, gm_ref, bg_ref, wbc_ref, wbm_ref, wo_ref, xp_ref, gp_ref, bp_ref, g_ref, b_ref,
             bc_ref, bm_ref, mg_ref, xh_ref, rs_ref, hb_ref):
        bc = jnp.dot(y_ref[...], wbc_ref[...], preferred_element_type=F32)
        bm = jnp.dot(o_ref[...], wbm_ref[...], preferred_element_type=F32)
        bgv = bg_ref[...]
        mg = jax.nn.sigmoid(gc_ref[...] + bgv[0:1]) * bc + jax.nn.sigmoid(gm_ref[...] + bgv[1:2]) * bm
        mgb = mg.astype(BF)
        bc_ref[...] = bc.astype(BF)
        bm_ref[...] = bm.astype(BF)
        mg_ref[...] = mgb
        mix = jnp.dot(mgb, wo_ref[...], preferred_element_type=F32)
        hprev = xp_ref[...] * gp_ref[...] + bp_ref[...]
        _ln_store(ALPHA * hprev + mix, g_ref, b_ref, xh_ref, rs_ref, hb_ref)

    def rows(n, col=0):
        return pl.BlockSpec((TMH, n), functools.partial(lambda i, col: (i, col), col=col))

    def whole(r, c):
        return pl.BlockSpec((r, c), lambda i: (0, 0))

    ln_specs, ln_shapes = _ln_out(tp, TMH)
    return _pcall(
        body, name=name, grid=(tp // TMH,),
        in_specs=[rows(D_CONV), rows(MLA_HEADS * KV_PAD), rows(D_MODEL, 2), rows(D_MODEL, 3), whole(2, D_MODEL),
                  whole(D_CONV, D_MODEL), whole(MLA_HEADS * KV_PAD, D_MODEL), whole(D_MODEL, D_MODEL), rows(D_MODEL)]
        + [_row_vec(D_MODEL)] * 4,
        out_specs=[rows(D_MODEL)] * 3 + ln_specs, out_shape=[_sds((tp, D_MODEL), BF)] * 3 + ln_shapes,
        compiler_params=_params(1),
    )(ycv, o2, p, p, bg, wbc, wbm_p, wo, xprev, gp, bp, g, b)


def loss_grad(xh, g, b, target_p, n_real, *, name):
    tp = xh.shape[0]

    def body(x_ref, g_ref, b_ref, t_ref, dy_ref, loss_ref):
        i = pl.program_id(0)

        @pl.when(i == 0)
        def _():
            loss_ref[...] = jnp.zeros_like(loss_ref)

        row = i * TM + lax.broadcasted_iota(jnp.int32, (TM, 1), 0)
        real = (row >= N_META) & (row < N_META + n_real)
        diff = jnp.where(real, x_ref[...] * g_ref[...] + b_ref[...] - t_ref[...], 0.0)
        dy_ref[...] = diff * (1.0 / D_MODEL)
        loss_ref[...] += 0.5 / D_MODEL * jnp.sum(diff * diff)

    return _pcall(
        body, name=name, grid=(tp // TM,),
        in_specs=[pl.BlockSpec((TM, D_MODEL), lambda i: (i, 0)), _row_vec(D_MODEL), _row_vec(D_MODEL),
                  pl.BlockSpec((TM, D_MODEL), lambda i: (i, 0))],
        out_specs=[pl.BlockSpec((TM, D_MODEL), lambda i: (i, 0)), pl.BlockSpec((8, LANES), lambda i: (0, 0))],
        out_shape=[_sds((tp, D_MODEL), F32), _sds((8, LANES), F32)], compiler_params=_params(1),
    )(xh, g, b, target_p)


def ln_bwd(dh, xh, rstd, g, *, branch_scale, name):
    tp = dh.shape[0]

    def body(dh_ref, xh_ref, rs_ref, g_ref, dz_ref, dzb_ref, dg_ref, db_ref):
        i = pl.program_id(0)

        @pl.when(i == 0)
        def _():
            dg_ref[...] = jnp.zeros_like(dg_ref)
            db_ref[...] = jnp.zeros_like(db_ref)

        dy = dh_ref[...]
        xhat = xh_ref[...]
        dg_ref[...] += jnp.sum(dy * xhat, axis=0, keepdims=True)
        db_ref[...] += jnp.sum(dy, axis=0, keepdims=True)
        dxh = dy * g_ref[...]
        m1 = jnp.mean(dxh, axis=-1, keepdims=True)
        m2 = jnp.mean(dxh * xhat, axis=-1, keepdims=True)
        dz = rs_ref[...] * (dxh - m1 - xhat * m2)
        dz_ref[...] = dz
        dzb_ref[...] = (branch_scale * dz).astype(BF)

    rows = pl.BlockSpec((TM, D_MODEL), lambda i: (i, 0))
    return _pcall(
        body, name=name, grid=(tp // TM,),
        in_specs=[rows, rows, pl.BlockSpec((TM, 1), lambda i: (i, 0)), _row_vec(D_MODEL)],
        out_specs=[rows, rows, _row_vec(D_MODEL), _row_vec(D_MODEL)],
        out_shape=[_sds((tp, D_MODEL), F32), _sds((tp, D_MODEL), BF), _sds((1, D_MODEL), F32), _sds((1, D_MODEL), F32)],
        compiler_params=_params(1),
    )(dh, xh, rstd, g)


def ffn_down_bwd(dzb, wd, gate, up, *, name):
    tp = dzb.shape[0]
    tn = D_FF // 2

    def body(dz_ref, wd_ref, g_ref, u_ref, dg_ref, du_ref):
        da = lax.dot_general(dz_ref[...], wd_ref[...], NT, preferred_element_type=F32)
        g = g_ref[...].astype(F32)
        u = u_ref[...].astype(F32)
        sg = jax.nn.sigmoid(g)
        dg_ref[...] = (da * u * sg * (1.0 + g * (1.0 - sg))).astype(BF)
        du_ref[...] = (da * g * sg).astype(BF)

    blk = pl.BlockSpec((TM, tn), lambda i, j: (i, j))
    return _pcall(
        body, name=name, grid=(tp // TM, D_FF // tn),
        in_specs=[pl.BlockSpec((TM, D_MODEL), lambda i, j: (i, 0)), pl.BlockSpec((tn, D_MODEL), lambda i, j: (j, 0)), blk, blk],
        out_specs=[blk, blk], out_shape=[_sds((tp, D_FF), BF)] * 2, compiler_params=_params(2),
    )(dzb, wd, gate, up)


def merge_bwd(dzb, wo, bc, bm, p, bg, wbc, wbm_p, o2, *, name):
    tp = dzb.shape[0]
    nh = MLA_HEADS

    def body(dz_ref, wo_ref, bc_ref, bm_ref, gc_ref, gm_ref, bg_ref, wbc_ref, wbm_ref, o_ref,
             dbc_ref, dbm_ref, dgg_ref, dy_ref, do_ref, dl_ref, dbg_ref):
        i = pl.program_id(0)

        @pl.when(i == 0)
        def _():
            dbg_ref[...] = jnp.zeros_like(dbg_ref)

        dmg = lax.dot_general(dz_ref[...], wo_ref[...], NT, preferred_element_type=F32)
        bgv = bg_ref[...]
        sc = jax.nn.sigmoid(gc_ref[...] + bgv[0:1])
        sm = jax.nn.sigmoid(gm_ref[...] + bgv[1:2])
        dbc = (dmg * sc).astype(BF)
        dbm = (dmg * sm).astype(BF)
        dgc = dmg * bc_ref[...].astype(F32) * sc * (1.0 - sc)
        dgm = dmg * bm_ref[...].astype(F32) * sm * (1.0 - sm)
        dbc_ref[...] = dbc
        dbm_ref[...] = dbm
        dgg_ref[...] = jnp.concatenate([dgc, dgm], axis=1).astype(BF)
        dbg_ref[...] += jnp.concatenate([jnp.sum(dgc, axis=0, keepdims=True), jnp.sum(dgm, axis=0, keepdims=True)], axis=0)
        dy_ref[...] = lax.dot_general(dbc, wbc_ref[...], NT, preferred_element_type=F32)
        do = lax.dot_general(dbm, wbm_ref[...], NT, preferred_element_type=F32)
        do_ref[...] = do.astype(BF)
        prod = do * o_ref[...].astype(F32)
        parts = []
        for h in range(nh):
            d = jnp.sum(prod[:, h * KV_PAD:(h + 1) * KV_PAD], axis=1, keepdims=True)
            parts.append(jnp.broadcast_to(d, (TMH, LANES)))
        dl_ref[...] = jnp.concatenate(parts, axis=1)

    def rows(n, col=0):
        return pl.BlockSpec((TMH, n), functools.partial(lambda i, col: (i, col), col=col))

    def whole(r, c):
        return pl.BlockSpec((r, c), lambda i: (0, 0))

    return _pcall(
        body, name=name, grid=(tp // TMH,),
        in_specs=[rows(D_MODEL), whole(D_MODEL, D_MODEL), rows(D_MODEL), rows(D_MODEL), rows(D_MODEL, 2), rows(D_MODEL, 3),
                  whole(2, D_MODEL), whole(D_CONV, D_MODEL), whole(nh * KV_PAD, D_MODEL), rows(nh * KV_PAD)],
        out_specs=[rows(D_MODEL), rows(D_MODEL), rows(2 * D_MODEL), rows(D_CONV), rows(nh * KV_PAD), rows(nh * LANES),
                   whole(2, D_MODEL)],
        out_shape=[_sds((tp, D_MODEL), BF), _sds((tp, D_MODEL), BF), _sds((tp, 2 * D_MODEL), BF), _sds((tp, D_CONV), F32),
                   _sds((tp, nh * KV_PAD), BF), _sds((tp, nh * LANES), F32), _sds((2, D_MODEL), F32)],
        compiler_params=_params(1),
    )(dzb, wo, bc, bm, p, p, bg, wbc, wbm_p, o2)


def attn_bwd(q2, kv, kr, do2, lse, dl, *, name, side=None):
    tp = q2.shape[0]
    nh = MLA_HEADS
    nb = tp // TM
    rep = TM // LANES

    def body(in_refs, out_refs, scratch):
        q_ref, kv_ref, kr_ref, do_ref, lse_ref, dl_ref = in_refs
        dq_ref, dkv_ref, dkr_ref = out_refs
        dq_acc, s0_ref, s1_ref, dp0_ref, dp1_ref = scratch
        qi = pl.program_id(1)
        s_refs, dp_refs = (s0_ref, s1_ref), (dp0_ref, dp1_ref)

        @pl.when(qi == 0)
        def _():
            dkv_ref[...] = jnp.zeros_like(dkv_ref)
            dkr_ref[...] = jnp.zeros_like(dkr_ref)

        dq_acc[...] = jnp.zeros_like(dq_acc)

        def prefetch(k, slot):
            kvb = kv_ref[_key_rows(k), :]
            k2 = jnp.concatenate([kvb, kr_ref[_key_rows(k), :]], axis=1)
            s_refs[slot][...] = lax.dot_general(q_ref[...], k2, NT, preferred_element_type=F32)
            dp_refs[slot][...] = lax.dot_general(do_ref[...], kvb, NT, preferred_element_type=F32)

        def process(k, slot, diagonal):
            rows = _key_rows(k)
            s = s_refs[slot][...]
            if diagonal:
                s = _causal_mask(s)
            p = jnp.exp2(s - jnp.tile(lse_ref[...], (1, rep)))
            dsb = (p * (dp_refs[slot][...] - jnp.tile(dl_ref[...], (1, rep)))).astype(BF)
            dk2 = lax.dot_general(dsb, q_ref[...], TN, preferred_element_type=F32) * LN2
            dkv_ref[rows, :] += lax.dot_general(p.astype(BF), do_ref[...], TN, preferred_element_type=F32) + dk2[:, :KV_PAD]
            dkr_ref[rows, :] += dk2[:, KV_PAD:KV_PAD + LANES]
            k2 = jnp.concatenate([kv_ref[rows, :], kr_ref[rows, :]], axis=1)
            dq_acc[...] += jnp.dot(dsb, k2, preferred_element_type=F32)

        _pipelined_key_blocks(qi, prefetch, process)
        dq_ref[...] = dq_acc[...]

    def qrow(n):
        return pl.BlockSpec((TM, n), lambda h, qi: (qi, h))

    def head(n):
        return pl.BlockSpec((tp, n), lambda h, qi: (0, h))

    return _side_call(
        body, side, name=name, grid=(nh, nb),
        in_specs=[qrow(Q_PAD), head(KV_PAD), pl.BlockSpec((tp, LANES), lambda h, qi: (0, 0)), qrow(KV_PAD), qrow(LANES), qrow(LANES)],
        out_specs=[qrow(Q_PAD), head(KV_PAD), head(LANES)],
        out_shape=[_sds((tp, nh * Q_PAD), F32), _sds((tp, nh * KV_PAD), F32), _sds((tp, nh * LANES), F32)],
        scratch_shapes=[pltpu.VMEM((TM, Q_PAD), F32)] + [pltpu.VMEM((TM, TM), F32)] * 4, args=(q2, kv, kr, do2, lse, dl),
    )


def _rms_bwd(x, g, dy):
    r = lax.rsqrt(jnp.mean(x * x, axis=-1, keepdims=True) + RMS_EPS)
    gy = dy * g
    dx = r * gy - x * (r * r * r) * jnp.mean(x * gy, axis=-1, keepdims=True)
    return dx, jnp.sum(dy * x * r, axis=0, keepdims=True)


def mla_prep_bwd(dq2, dkv, dkr, p, gq, gkv, wuq_p, wukv, tabs_bwd, *, name):
    tp = dq2.shape[0]
    nh = MLA_HEADS

    def body(dq_ref, dkv_ref, dkr_ref, cq_ref, ckv_ref, gq_ref, gkv_ref, wuq_ref, wukv_ref,
             cq_t, s1q_t, s2q_t, ck_t, s1k_t, s2k_t, dqb_ref, dsm_ref, dgq_ref, dgkv_ref):
        i = pl.program_id(0)

        @pl.when(i == 0)
        def _():
            dgq_ref[...] = jnp.zeros_like(dgq_ref)
            dgkv_ref[...] = jnp.zeros_like(dgkv_ref)

        dqb = _rope(dq_ref[...], cq_t[...], s1q_t[...], s2q_t[...], nh).astype(BF)
        dqb_ref[...] = dqb
        dcqn = lax.dot_general(dqb, wuq_ref[...], NT, preferred_element_type=F32)
        dcq, dgq = _rms_bwd(cq_ref[...], gq_ref[...], dcqn)
        dckvn = lax.dot_general(dkv_ref[...].astype(BF), wukv_ref[...], NT, preferred_element_type=F32)
        dckv, dgkv = _rms_bwd(ckv_ref[...], gkv_ref[...], dckvn)
        dkr_heads = dkr_ref[...]
        dkr_sum = dkr_heads[:, :LANES]
        for h in range(1, nh):
            dkr_sum = dkr_sum + dkr_heads[:, h * LANES:(h + 1) * LANES]
        dkr = _rope(dkr_sum, ck_t[...], s1k_t[...], s2k_t[...], 1)
        dsm_ref[...] = jnp.concatenate([dcq, dckv, dkr], axis=1)
        dgq_ref[...] += dgq
        dgkv_ref[...] += dgkv

    def rows(n, col=0):
        return pl.BlockSpec((TMH, n), functools.partial(lambda i, col: (i, col), col=col))

    return _pcall(
        body, name=name, grid=(tp // TMH,),
        in_specs=[rows(nh * Q_PAD), rows(nh * KV_PAD), rows(nh * LANES), rows(Q_LORA, 1536 // Q_LORA), rows(KV_LORA, 1792 // KV_LORA),
                  _row_vec(Q_LORA), _row_vec(KV_LORA),
                  pl.BlockSpec((Q_LORA, nh * Q_PAD), lambda i: (0, 0)), pl.BlockSpec((KV_LORA, nh * KV_PAD), lambda i: (0, 0)),
                  rows(Q_PAD), rows(Q_PAD), rows(Q_PAD), rows(LANES), rows(LANES), rows(LANES)],
        out_specs=[rows(nh * Q_PAD), rows(Q_LORA + KV_LORA + LANES), _row_vec(Q_LORA), _row_vec(KV_LORA)],
        out_shape=[_sds((tp, nh * Q_PAD), BF), _sds((tp, Q_LORA + KV_LORA + LANES), F32), _sds((1, Q_LORA), F32),
                   _sds((1, KV_LORA), F32)],
        compiler_params=_params(1),
    )(dq2, dkv, dkr, p, p, gq, gkv, wuq_p, wukv, *tabs_bwd)


def conv_bwd(dy, p, conv, w, *, name):
    tp = dy.shape[0]
    nb = tp // TM

    def body(dy_ref, b_ref, c_ref, h_ref, cv_ref, w_ref, dp_ref, dw0_ref, dw1_ref, dw2_ref, dbuf):
        i = pl.program_id(0)

        @pl.when(i == 0)
        def _():
            dbuf[TM:TM + 8, :] = jnp.zeros((8, D_CONV), F32)
            dw0_ref[...] = jnp.zeros_like(dw0_ref)
            dw1_ref[...] = jnp.zeros_like(dw1_ref)
            dw2_ref[...] = jnp.zeros_like(dw2_ref)

        dyv = dy_ref[...]
        c = c_ref[...]
        hh = h_ref[...]
        dconv = dyv * b_ref[...]
        dbuf[0:TM, :] = dconv
        d1 = dbuf[pl.ds(1, TM), :]
        d2 = dbuf[pl.ds(2, TM), :]
        w_all = w_ref[...]
        de = w_all[2:3] * dconv + w_all[1:2] * d1 + w_all[0:1] * d2
        e = c * hh
        dp_ref[...] = jnp.concatenate([dyv * cv_ref[...].astype(F32), de * hh, de * c], axis=1).astype(BF)
        dw0_ref[...] += jnp.sum(d2 * e, axis=0, keepdims=True)
        dw1_ref[...] += jnp.sum(d1 * e, axis=0, keepdims=True)
        dw2_ref[...] += jnp.sum(dconv * e, axis=0, keepdims=True)
        dbuf[TM:TM + 8, :] = dbuf[0:8, :]

    def col(j):
        return pl.BlockSpec((TM, D_CONV), functools.partial(lambda i, j: (nb - 1 - i, j), j=j))

    return _pcall(
        body, name=name, grid=(nb,),
        in_specs=[col(0), col(0), col(1), col(2), col(0), pl.BlockSpec((3, D_CONV), lambda i: (0, 0))],
        out_specs=[pl.BlockSpec((TM, 3 * D_CONV), lambda i: (nb - 1 - i, 0))] + [_row_vec(D_CONV)] * 3,
        out_shape=[_sds((tp, 3 * D_CONV), BF)] + [_sds((1, D_CONV), F32)] * 3,
        scratch_shapes=[pltpu.VMEM((TM + 8, D_CONV), F32)], compiler_params=_params(1),
    )(dy, p, p, p, conv, w)


def adamw(w, g, m, v, *, name):
    r, c = w.shape
    tr = r
    for cand in (256, 128, 64, 32, 16, 8):
        if r % cand == 0 and r > cand:
            tr = cand
            break

    def body(w_ref, g_ref, m_ref, v_ref, d_ref, nm_ref, nv_ref):
        gv = g_ref[...]
        nm = ADAM_B1 * m_ref[...] + (1.0 - ADAM_B1) * gv
        nv = ADAM_B2 * v_ref[...] + (1.0 - ADAM_B2) * (gv * gv)
        m_hat = nm / (1.0 - ADAM_B1 ** ADAM_STEP)
        v_hat = nv / (1.0 - ADAM_B2 ** ADAM_STEP)
        d_ref[...] = -ADAM_LR * (m_hat / (jnp.sqrt(v_hat) + ADAM_EPS) + ADAM_WD * w_ref[...])
        nm_ref[...] = nm
        nv_ref[...] = nv

    blk = pl.BlockSpec((tr, c), lambda i: (i, 0))
    return _pcall(
        body, name=name, grid=(r // tr,), in_specs=[blk] * 4, out_specs=[blk] * 3,
        out_shape=[_sds((r, c), F32)] * 3, compiler_params=_params(1),
    )(w, g, m, v)


HBM_SPEC = pl.BlockSpec(memory_space=pltpu.HBM)


def _place():
    return lax.axis_index("x"), lax.axis_index("y"), lax.axis_index("c")


def _other_chips(x, y):
    return [(1 - x, y), (x, 1 - y), (1 - x, 1 - y)]


def _half(ref_or_shape_rows, c):
    return pl.ds(c * (ref_or_shape_rows // 2), ref_or_shape_rows // 2)


def gather_side(items):
    n = len(items)
    shards = [s for s, _ in items]
    layers = [l for _, l in items]

    def program(x_refs, o_refs, send_sems, recv_sems):
        x, y, c = _place()
        me = 2 * x + y
        chips = _other_chips(x, y)

        def copy(sem, src, dst, to):
            return pltpu.make_async_remote_copy(src_ref=src, dst_ref=dst, send_sem=send_sems.at[sem], recv_sem=recv_sems.at[sem],
                                                device_id=to, device_id_type=MESH)

        def src(i):
            return x_refs[i].at[layers[i], _half(x_refs[i].shape[1], c)]

        def dst(i, slot, cc):
            return o_refs[i].at[slot, _half(o_refs[i].shape[1], cc)]

        sends = [copy(6 * i + k, src(i), dst(i, me, c), (px, py, c)) for i in range(n) for k, (px, py) in enumerate(chips)]
        passed = [copy(6 * i + 3 + k, dst(i, 2 * px + py, c), dst(i, 2 * px + py, c), (x, y, 1 - c))
                  for k, (px, py) in enumerate(chips) for i in range(n)]

        def start():
            for cp in sends:
                cp.start()

        def finish():
            pos = 0
            for k, (px, py) in enumerate(chips):
                for i in range(n):
                    copy(6 * i + k, src(i), dst(i, 2 * px + py, c), (px, py, c)).wait_recv()
                    passed[pos].start()
                    pos += 1
            for k, (px, py) in enumerate(chips):
                for i in range(n):
                    copy(6 * i + 3 + k, dst(i, 2 * px + py, 1 - c), dst(i, 2 * px + py, 1 - c), (x, y, 1 - c)).wait_recv()
            for cp in sends + passed:
                cp.wait_send()

        return start, finish

    prefilled = [jnp.broadcast_to(s[l][None], (4,) + s.shape[1:]) for s, l in items]
    return shards, prefilled, 6 * n, program


def scatter_side(pss):
    n = len(pss)

    def program(p_refs, o_refs, send_sems, recv_sems):
        x, y, c = _place()
        me = 2 * x + y
        chips = _other_chips(x, y)

        def copy(i, k, j_src, j_dst, to):
            return pltpu.make_async_remote_copy(src_ref=p_refs[i].at[j_src], dst_ref=o_refs[i].at[j_dst],
                                                send_sem=send_sems.at[3 * i + k], recv_sem=recv_sems.at[3 * i + k],
                                                device_id=to, device_id_type=MESH)

        sends = [copy(i, k, 2 * px + py, me, (px, py, c)) for i in range(n) for k, (px, py) in enumerate(chips)]

        def start():
            for cp in sends:
                cp.start()

        def finish():
            for i in range(n):
                for k, (px, py) in enumerate(chips):
                    copy(i, k, me, 2 * px + py, (px, py, c)).wait_recv()
            for cp in sends:
                cp.wait_send()

        return start, finish

    xi, yi, _ = _place()
    own = jnp.arange(4)[:, None, None] == 2 * xi + yi
    prefilled = [jnp.where(own, p, jnp.zeros_like(p)) for p in pss]
    return list(pss), prefilled, 3 * n, program


def exchange_alone(side, *, name):
    inputs, prefilled, n_sems, program = side
    a, b = len(inputs), len(prefilled)

    def body(*refs):
        start, finish = program(refs[:a], refs[a + b:a + 2 * b], refs[-2], refs[-1])
        start()
        finish()

    return _pcall(
        body, name=name, in_specs=[HBM_SPEC] * (a + b), out_specs=[HBM_SPEC] * b, out_shape=[_sds(p.shape, p.dtype) for p in prefilled],
        input_output_aliases={a + i: i for i in range(b)}, scratch_shapes=[pltpu.SemaphoreType.DMA((n_sems,))] * 2,
    )(*inputs, *prefilled)


def pair_exchange(gs, *, name):
    n = len(gs)

    def body(*refs):
        g_refs, o_refs = refs[:n], refs[n:2 * n]
        send_sems, recv_sems = refs[2 * n:]
        x, y, c = _place()
        cps = [pltpu.make_async_remote_copy(src_ref=g_refs[i].at[:, _half(g_refs[i].shape[1], 1 - c)], dst_ref=o_refs[i],
                                            send_sem=send_sems.at[i], recv_sem=recv_sems.at[i], device_id=(x, y, 1 - c),
                                            device_id_type=MESH)
               for i in range(n)]
        for cp in cps:
            cp.start()
        for cp in cps:
            cp.wait()

    return _pcall(
        body, name=name, in_specs=[HBM_SPEC] * n, out_specs=[HBM_SPEC] * n,
        out_shape=[_sds((4, g.shape[1] // 2, g.shape[2]), g.dtype) for g in gs],
        scratch_shapes=[pltpu.SemaphoreType.DMA((n,)), pltpu.SemaphoreType.DMA((n,))],
    )(*gs)


def _comm_rows(a, b, itemsize):
    return a // 2 if a * b * itemsize > (3 << 19) and a % 16 == 0 else a


def pair_add(g, s1, c_idx, *, name):
    n, a, b = g.shape
    ah = a // 2
    ta = _comm_rows(ah, b, 2)
    nblk = ah // ta

    def body(c_ref, g_ref, s_ref, o_ref):
        o_ref[...] = (g_ref[...].astype(F32) + s_ref[...].astype(F32)).astype(o_ref.dtype)

    grid_spec = pltpu.PrefetchScalarGridSpec(
        num_scalar_prefetch=1, grid=(n, nblk),
        in_specs=[pl.BlockSpec((1, ta, b), lambda j, i, c_ref: (j, c_ref[0] * nblk + i, 0)),
                  pl.BlockSpec((1, ta, b), lambda j, i, c_ref: (j, i, 0))],
        out_specs=pl.BlockSpec((1, ta, b), lambda j, i, c_ref: (j, i, 0)),
    )
    return _pcall(body, name=name, grid_spec=grid_spec, out_shape=_sds((n, ah, b), g.dtype), compiler_params=_params(2))(
        c_idx, g, s1)


def sum_chunks(s2, *, name):
    n, a, b = s2.shape
    ta = _comm_rows(a, b, 4)

    def body(s_ref, o_ref):
        acc = s_ref[0].astype(F32)
        for j in range(1, n):
            acc = acc + s_ref[j].astype(F32)
        o_ref[...] = acc

    return _pcall(
        body, name=name, grid=(a // ta,), in_specs=[pl.BlockSpec((n, ta, b), lambda i: (0, i, 0))],
        out_specs=pl.BlockSpec((ta, b), lambda i: (i, 0)), out_shape=_sds((a, b), F32), compiler_params=_params(1),
    )(s2)


def pair_gather(rcs, *, name):
    n = len(rcs)

    def body(*refs):
        r_refs, o_refs = refs[:n], refs[2 * n:3 * n]
        send_sems, recv_sems = refs[3 * n:]
        x, y, c = _place()

        def copy(i, half):
            return pltpu.make_async_remote_copy(src_ref=r_refs[i], dst_ref=o_refs[i].at[half], send_sem=send_sems.at[i],
                                                recv_sem=recv_sems.at[i], device_id=(x, y, 1 - c), device_id_type=MESH)

        sends = [copy(i, c) for i in range(n)]
        for cp in sends:
            cp.start()
        for i in range(n):
            copy(i, 1 - c).wait_recv()
        for cp in sends:
            cp.wait_send()

    prefilled = [jnp.broadcast_to(r[None], (2,) + r.shape) for r in rcs]
    return _pcall(
        body, name=name, in_specs=[HBM_SPEC] * (2 * n), out_specs=[HBM_SPEC] * n,
        out_shape=[_sds(p.shape, p.dtype) for p in prefilled], input_output_aliases={n + i: i for i in range(n)},
        scratch_shapes=[pltpu.SemaphoreType.DMA((n,)), pltpu.SemaphoreType.DMA((n,))],
    )(*rcs, *prefilled)


def exchange_small(arrs, *, reduce, name):
    n = len(arrs)

    def body(*refs):
        v_refs, o_refs = refs[:n], refs[n:2 * n]
        bufs = refs[2 * n:3 * n] if reduce else o_refs
        send_sems, recv_sems = refs[-2:]
        x, y, c = _place()
        me = 4 * x + 2 * y + c
        for i in range(n):
            bufs[i][me] = v_refs[i][...]

        def peer(k):
            dx, dy, dc = (k >> 2) & 1, (k >> 1) & 1, k & 1
            return (1 - x if dx else x, 1 - y if dy else y, 1 - c if dc else c)

        def copy(i, k, slot):
            return pltpu.make_async_remote_copy(src_ref=v_refs[i], dst_ref=bufs[i].at[slot], send_sem=send_sems.at[7 * i + k - 1],
                                                recv_sem=recv_sems.at[7 * i + k - 1], device_id=peer(k), device_id_type=MESH)

        sends = [copy(i, k, me) for i in range(n) for k in range(1, 8)]
        for cp in sends:
            cp.start()
        for i in range(n):
            for k in range(1, 8):
                px, py, pc = peer(k)
                copy(i, k, 4 * px + 2 * py + pc).wait_recv()
        for cp in sends:
            cp.wait_send()
        if reduce:
            for i in range(n):
                acc = bufs[i][0]
                for d in range(1, 8):
                    acc = acc + bufs[i][d]
                o_refs[i][...] = acc

    vmem = pl.BlockSpec(memory_space=pltpu.VMEM)
    stacked = [(8,) + a.shape for a in arrs]
    return _pcall(
        body, name=name, in_specs=[vmem] * n, out_specs=[vmem] * n,
        out_shape=[_sds(a.shape if reduce else s, F32) for a, s in zip(arrs, stacked)],
        scratch_shapes=([pltpu.VMEM(s, F32) for s in stacked] if reduce else [])
        + [pltpu.SemaphoreType.DMA((7 * n,)), pltpu.SemaphoreType.DMA((7 * n,))],
    )(*arrs)


def _pad_rows(n, mult):
    return -(-n // mult) * mult


def _chip_major(g, b):
    return g.reshape(g.shape[0], 4, b).transpose(1, 0, 2)


def _rope_tables(tp):
    inv_freq = 1.0 / (ROPE_BASE ** (jnp.arange(0, QK_ROPE, 2, dtype=F32) / QK_ROPE))
    ang = jnp.arange(tp, dtype=F32)[:, None] * inv_freq[None, :]
    cos, sin = jnp.cos(ang), jnp.sin(ang)
    one = lambda n: jnp.ones((tp, n), F32)
    zero = lambda n: jnp.zeros((tp, n), F32)
    cq = jnp.concatenate([one(128), cos, cos, one(96)], axis=1)
    s1q = jnp.concatenate([zero(144), sin, zero(96)], axis=1)
    s2q = jnp.concatenate([zero(128), -sin, zero(112)], axis=1)
    ck = jnp.concatenate([cos, cos, zero(96)], axis=1)
    s1k = jnp.concatenate([zero(16), sin, zero(96)], axis=1)
    s2k = jnp.concatenate([-sin, zero(112)], axis=1)
    fwd = (cq * (ATT_SCALE * LOG2E), s1q * (ATT_SCALE * LOG2E), s2q * (ATT_SCALE * LOG2E), ck, s1k, s2k)
    bwd = (cq * ATT_SCALE, -s1q * ATT_SCALE, -s2q * ATT_SCALE, ck, -s1k, -s2k)
    return fwd, bwd


def _pad_w_in(w):
    return jnp.concatenate([w[:, :1952], jnp.zeros((w.shape[0], 96), w.dtype), w[:, 1952:]], axis=1)


def _pad_w_uq(w):
    w = w.reshape(Q_LORA, MLA_HEADS, QK_NOPE + QK_ROPE)
    z = lambda n: jnp.zeros((Q_LORA, MLA_HEADS, n), w.dtype)
    return jnp.concatenate([w[..., :QK_NOPE], z(64), w[..., QK_NOPE:], z(96)], axis=-1).reshape(Q_LORA, MLA_HEADS * Q_PAD)


def _unpad_w_uq(w):
    w = w.reshape(Q_LORA, MLA_HEADS, Q_PAD)
    return jnp.concatenate([w[..., :QK_NOPE], w[..., 128:128 + QK_ROPE]], axis=-1).reshape(Q_LORA, MLA_HEADS * (QK_NOPE + QK_ROPE))


def _pad_w_br_mla(w):
    w = w.reshape(MLA_HEADS, V_HEAD, D_MODEL)
    return jnp.concatenate([jnp.zeros_like(w), w], axis=1).reshape(MLA_HEADS * KV_PAD, D_MODEL)


def _unpad_w_br_mla(w):
    return w.reshape(MLA_HEADS, KV_PAD, D_MODEL)[:, V_HEAD:].reshape(MLA_HEADS * V_HEAD, D_MODEL)


def _riding(hooks, where, l, *args):
    make = hooks.get(where)
    ride = make(l, *args) if make else None
    return ride if ride else (None, lambda results: None)


def _layer_fwd(l, st, xprev, gp, bp, hb, w, tabs, hooks):
    ln_g, ln_b = w["ln_g"], w["ln_b"]
    lg = lambda k: ln_g[l, k][None]
    lb = lambda k: ln_b[l, k][None]
    s = {}
    s["x0"], s["gp0"], s["bp0"], s["hb0"] = xprev, gp, bp, hb
    side, got = _riding(hooks, "ffn1_fwd", l)
    s["g1"], s["u1"], s["a1"], *extras = ffn_up(hb, w["ffn1_w_up"][l], name="ffn_up", side=side)
    got(extras)
    s["xh1"], s["rs1"], s["hb1"] = down_ln(s["a1"], w["ffn1_w_down"][l], xprev, gp, bp, lg(0), lb(0), name="ffn_down_ln")
    s["p"] = mm_rows([(s["hb1"], w["mix_w_in"][l], False, 0)], D_IN_PAD, name="mix_in", tn=1024)
    gq, gkv = w["q_norm_g"][l][None], w["kv_norm_g"][l][None]
    s["cqn"], s["ckvn"], s["q2"], s["kv"], s["kr"] = mla_prep(s["p"], gq, gkv, w["w_uq"][l], w["w_ukv"][l], tabs, name="mla_prep")
    side, got = _riding(hooks, "attn_fwd", l)
    s["o2"], s["lse"], *extras = attn_fwd(s["q2"], s["kv"], s["kr"], name="attn_fwd", side=side)
    got(extras)
    s["ycv"], s["conv"] = conv_fwd(s["p"], w["conv_w"][l], name="conv_fwd")
    s["bc"], s["bm"], s["mg"], s["xh2"], s["rs2"], s["hb2"] = merge_out_ln(
        s["ycv"], s["o2"], s["p"], w["mix_b_gate"][l], w["w_br_conv"][l], w["w_br_mla"][l], w["w_o"][l],
        s["xh1"], lg(0), lb(0), lg(1), lb(1), name="merge_out_ln")
    s["g2"], s["u2"], s["a2"] = ffn_up(s["hb2"], w["ffn2_w_up"][l], name="ffn_up")
    s["xh3"], s["rs3"], s["hb3"] = down_ln(s["a2"], w["ffn2_w_down"][l], s["xh2"], lg(1), lb(1), lg(2), lb(2), name="ffn_down_ln")
    st.append(s)
    return s["xh3"], lg(2), lb(2), s["hb3"]


def _ffn_bwd(dh, w_up, w_down, ln_gain, hb_in, gate, up, act, xh, rs):
    dz, dzb, dgam, dbet = ln_bwd(dh, xh, rs, ln_gain, branch_scale=0.5, name="ln_bwd")
    d_wd = tn_mm(act, dzb, tm=D_FF // 2, name="dw_down", shard=("rows", D_FF // 4))
    dgate, dup = ffn_down_bwd(dzb, w_down, gate, up, name="ffn_down_bwd")
    d_w = tn_mm(hb_in, dgate, tm=512, name="dw_up", shard=("cols", D_FF // 2), slot0=0)
    d_w = tn_mm(hb_in, dup, tm=512, name="dw_up", shard=("cols", D_FF // 2), slot0=2, dst=d_w)
    dh_in = mm_rows([(dgate, w_up, True, 0), (dup, w_up, True, 1)], D_MODEL, name="ffn_up_bwd", tn=512, addend=dz, add_scale=ALPHA)
    return dh_in, d_w, d_wd, dgam, dbet


def _layer_bwd(l, s, dh, w, tabs_bwd, hooks):
    ln_g = w["ln_g"]
    lg = lambda k: ln_g[l, k][None]
    g = {}
    dh, g["ffn2_w_up"], g["ffn2_w_down"], dg2, db2 = _ffn_bwd(
        dh, w["ffn2_w_up"][l], w["ffn2_w_down"][l], lg(2), s["hb2"], s["g2"], s["u2"], s["a2"], s["xh3"], s["rs3"])
    dz, dzb, dg1, db1 = ln_bwd(dh, s["xh2"], s["rs2"], lg(1), branch_scale=1.0, name="ln_bwd")
    g["w_o"] = tn_mm(s["mg"], dzb, tm=1024, name="dw_o", shard=("rows", D_MODEL // 4))
    dbc, dbm, dgg, dycv, do2, dl, g["mix_b_gate"] = merge_bwd(
        dzb, w["w_o"][l], s["bc"], s["bm"], s["p"], w["mix_b_gate"][l], w["w_br_conv"][l], w["w_br_mla"][l], s["o2"], name="merge_bwd")
    g["w_br_conv"] = tn_mm(s["ycv"], dbc, tm=512, name="dw_br_conv", shard=("cols", D_MODEL // 4))
    g["w_br_mla"] = _chip_major(_unpad_w_br_mla(tn_mm(s["o2"], dbm, tm=1024, name="dw_br_mla")), D_MODEL // 4)
    side, got = _riding(hooks, "attn_bwd", l, g)
    dq2, dkv, dkr, *extras = attn_bwd(s["q2"], s["kv"], s["kr"], do2, s["lse"], dl, name="attn_bwd", side=side)
    got(extras)
    gq, gkv = w["q_norm_g"][l][None], w["kv_norm_g"][l][None]
    dqb, dsm, g["q_norm_g"], g["kv_norm_g"] = mla_prep_bwd(dq2, dkv, dkr, s["p"], gq, gkv, w["w_uq"][l], w["w_ukv"][l], tabs_bwd,
                                                           name="mla_prep_bwd")
    g["w_uq"] = _chip_major(_unpad_w_uq(tn_mm(s["cqn"], dqb, tm=Q_LORA, name="dw_uq")), MLA_HEADS * (QK_NOPE + QK_ROPE) // 4)
    g["w_ukv"] = tn_mm(s["ckvn"], dkv, tm=KV_LORA, name="dw_ukv", shard=("cols", MLA_HEADS * KV_PAD // 4))
    dbch, dw0, dw1, dw2 = conv_bwd(dycv, s["p"], s["conv"], w["conv_w"][l], name="conv_bwd")
    g["conv_w"] = jnp.concatenate([dw0, dw1, dw2], axis=0)
    w_in = w["mix_w_in"][l]
    d_bch = tn_mm(s["hb1"], dbch, tm=512, name="dw_in_bch")
    d_sm = tn_mm(s["hb1"], dsm, tm=1024, name="dw_in_sm")
    d_gg = tn_mm(s["hb1"], dgg, tm=512, name="dw_in_gg")
    g["mix_w_in"] = _chip_major(jnp.concatenate([d_bch, d_sm[:, :1952 - 1536], d_gg], axis=1), D_IN // 4)
    dh = mm_rows([(dbch, w_in, True, 0), (dsm, w_in, True, 3), (dgg, w_in, True, 1)], D_MODEL, name="mix_in_bwd", tn=512,
                 addend=dz, add_scale=ALPHA)
    dh, g["ffn1_w_up"], g["ffn1_w_down"], dg0, db0 = _ffn_bwd(
        dh, w["ffn1_w_up"][l], w["ffn1_w_down"][l], lg(0), s["hb0"], s["g1"], s["u1"], s["a1"], s["xh1"], s["rs1"])
    g["ln_g"] = jnp.concatenate([dg0, dg1, dg2], axis=0)
    g["ln_b"] = jnp.concatenate([db0, db1, db2], axis=0)
    return dh, g


BIG = ("ffn1_w_up", "ffn1_w_down", "mix_w_in", "w_uq", "w_ukv", "w_br_conv", "w_br_mla", "w_o", "ffn2_w_up", "ffn2_w_down")
BIG_AXIS = (2, 1, 2, 2, 2, 2, 2, 1, 2, 1)
FFN1_MATRICES = ("ffn1_w_up", "ffn1_w_down")
MIXER_MATRICES = ("mix_w_in", "w_uq", "w_ukv", "w_br_conv", "w_br_mla", "w_o")
FFN2_MATRICES = ("ffn2_w_up", "ffn2_w_down")
SMALL_SHARDED = ("meta_tokens", "mix_b_gate", "conv_w", "ln_g", "ln_b")
SMALL_REPLICATED = ("q_norm_g", "kv_norm_g")
WEIGHTS = ("meta_tokens", "ffn1_w_up", "ffn1_w_down", "mix_w_in", "mix_b_gate", "conv_w", "q_norm_g", "w_uq", "kv_norm_g", "w_ukv",
           "w_br_conv", "w_br_mla", "w_o", "ffn2_w_up", "ffn2_w_down", "ln_g", "ln_b")


def _view2d(a):
    return a.reshape(-1, a.shape[-1])


def _local_grads(x_row, target_row, w, hooks=None):
    hooks = hooks or {}
    seq = x_row.shape[0]
    t_real = N_META + seq
    tp = _pad_rows(t_real, TM)
    pad = tp - t_real
    h0 = jnp.concatenate([w["meta_tokens"], x_row, jnp.zeros((pad, D_MODEL), F32)], axis=0)
    target_p = jnp.concatenate([jnp.zeros((N_META, D_MODEL), F32), target_row, jnp.zeros((pad, D_MODEL), F32)], axis=0)
    tabs, tabs_bwd = _rope_tables(tp)
    ones = jnp.ones((1, D_MODEL), F32)
    zeros = jnp.zeros((1, D_MODEL), F32)
    saved = []
    cur = (h0, ones, zeros, h0.astype(BF))
    for l in range(DEPTH):
        cur = _layer_fwd(l, saved, *cur, w, tabs, hooks)
    dh, loss_acc = loss_grad(cur[0], cur[1], cur[2], target_p, seq, name="loss_grad")
    grads = [None] * DEPTH
    for l in reversed(range(DEPTH)):
        dh, grads[l] = _layer_bwd(l, saved[l], dh, w, tabs_bwd, hooks)
        if "layer_bwd_done" in hooks:
            hooks["layer_bwd_done"](l, grads[l])
    return loss_acc, dh[N_META:t_real], dh[:N_META], grads


def kernel(x, meta_tokens, ffn1_w_up, ffn1_w_down, mix_w_in, mix_b_gate, conv_w, q_norm_g, w_uq, kv_norm_g, w_ukv, w_br_conv, w_br_mla, w_o, ffn2_w_up, ffn2_w_down, ln_g, ln_b, loss_target, m_meta_tokens, m_ffn1_w_up, m_ffn1_w_down, m_mix_w_in, m_mix_b_gate, m_conv_w, m_q_norm_g, m_w_uq, m_kv_norm_g, m_w_ukv, m_w_br_conv, m_w_br_mla, m_w_o, m_ffn2_w_up, m_ffn2_w_down, m_ln_g, m_ln_b, v_meta_tokens, v_ffn1_w_up, v_ffn1_w_down, v_mix_w_in, v_mix_b_gate, v_conv_w, v_q_norm_g, v_w_uq, v_kv_norm_g, v_w_ukv, v_w_br_conv, v_w_br_mla, v_w_o, v_ffn2_w_up, v_ffn2_w_down, v_ln_g, v_ln_b):
    local = dict(meta_tokens=meta_tokens, ffn1_w_up=ffn1_w_up, ffn1_w_down=ffn1_w_down, mix_w_in=mix_w_in, mix_b_gate=mix_b_gate,
                 conv_w=conv_w, q_norm_g=q_norm_g, w_uq=w_uq, kv_norm_g=kv_norm_g, w_ukv=w_ukv, w_br_conv=w_br_conv,
                 w_br_mla=w_br_mla, w_o=w_o, ffn2_w_up=ffn2_w_up, ffn2_w_down=ffn2_w_down, ln_g=ln_g, ln_b=ln_b)
    mom_m = dict(zip(WEIGHTS, (m_meta_tokens, m_ffn1_w_up, m_ffn1_w_down, m_mix_w_in, m_mix_b_gate, m_conv_w, m_q_norm_g, m_w_uq,
                               m_kv_norm_g, m_w_ukv, m_w_br_conv, m_w_br_mla, m_w_o, m_ffn2_w_up, m_ffn2_w_down, m_ln_g, m_ln_b)))
    mom_v = dict(zip(WEIGHTS, (v_meta_tokens, v_ffn1_w_up, v_ffn1_w_down, v_mix_w_in, v_mix_b_gate, v_conv_w, v_q_norm_g, v_w_uq,
                               v_kv_norm_g, v_w_ukv, v_w_br_conv, v_w_br_mla, v_w_o, v_ffn2_w_up, v_ffn2_w_down, v_ln_g, v_ln_b)))
    xi, yi, ci = _place()
    chip = 2 * xi + yi

    shards = {n: local[n].astype(BF) for n in BIG}
    axis = dict(zip(BIG, BIG_AXIS))
    pad_layout = {"mix_w_in": _pad_w_in, "w_uq": _pad_w_uq, "w_br_mla": _pad_w_br_mla}
    w = {n: [None] * DEPTH for n in BIG}

    def fetch(keys):
        def install(gathered):
            for (n, l), g in zip(keys, gathered):
                full = jnp.concatenate([g[j] for j in range(4)], axis=axis[n] - 1)
                w[n][l] = pad_layout[n](full) if n in pad_layout else full
        return gather_side([(shards[n], l) for n, l in keys]), install

    first, install_first = fetch([(n, 0) for n in FFN1_MATRICES])
    install_first(exchange_alone(first, name="gather_weights"))
    fetch_under = {("ffn1_fwd", 0): [(n, 0) for n in MIXER_MATRICES],
                   ("attn_fwd", 0): [(n, 0) for n in FFN2_MATRICES] + [(n, 1) for n in BIG]}
    hooks = {where: functools.partial(lambda l, where: fetch(fetch_under[where, l]) if (where, l) in fetch_under else None, where=where)
             for where in ("ffn1_fwd", "attn_fwd")}
    stacked = exchange_small([_view2d(local[n]) for n in SMALL_SHARDED], reduce=False, name="gather_small")
    for n, st in zip(SMALL_SHARDED, stacked):
        full = jnp.concatenate([st[2 * j] for j in range(4)], axis=-1)
        w[n] = full.reshape(local[n].shape[:-1] + (full.shape[-1],))
    for n in SMALL_REPLICATED:
        w[n] = local[n]

    c_idx = jnp.reshape(ci, (1,)).astype(jnp.int32)
    done, from_chips = {}, {}

    def send(keys, grad_of):
        glist = [grad_of[k] for k in keys]
        from_sibling = pair_exchange(glist, name="rs_pair_exchange")
        sums = [pair_add(a, s, c_idx, name="rs_pair_add") for a, s in zip(glist, from_sibling)]
        return scatter_side(sums), lambda results: from_chips.update(zip(keys, results))

    early = FFN2_MATRICES + ("w_o", "w_br_conv", "w_br_mla")
    hooks["layer_bwd_done"] = lambda l, g: done.update({(n, l): g[n] for n in BIG})
    hooks["attn_bwd"] = lambda l, g: send([(n, 1) for n in BIG] + [(n, 0) for n in early],
                                          {**done, **{(n, 0): g[n] for n in early}}) if l == 0 else None

    loss_acc, grad_x, d_meta, grads = _local_grads(x[0], loss_target[0], w, hooks)
    grad_x = grad_x[None]
    last, keep_last = send([(n, 0) for n in BIG if n not in early], done)
    keep_last(exchange_alone(last, name="rs_chip_scatter"))
    keys = [(n, l) for n in BIG for l in range(DEPTH)]
    reduced = pair_gather([sum_chunks(from_chips[k], name="rs_sum") for k in keys], name="rs_pair_gather")
    reduced = {k: r.reshape(local[k[0]].shape[1:]) for k, r in zip(keys, reduced)}
    gshard = {n: jnp.stack([reduced[n, l] for l in range(DEPTH)]) for n in BIG}

    small_names = SMALL_SHARDED + SMALL_REPLICATED
    gsmall = {n: jnp.concatenate([grads[l][n] for l in range(DEPTH)], axis=0) for n in small_names if n != "meta_tokens"}
    gsmall["meta_tokens"] = d_meta
    small_red = exchange_small([gsmall[n] for n in small_names] + [loss_acc], reduce=True, name="reduce_small")
    loss = small_red[-1][0, 0]
    for n, full in zip(small_names, small_red[:-1]):
        if n in SMALL_SHARDED:
            sh = local[n].shape[-1]
            full = lax.dynamic_slice_in_dim(full, chip * sh, sh, axis=1)
        gshard[n] = full.reshape(local[n].shape)

    delta, new_m, new_v = {}, {}, {}
    for n in WEIGHTS:
        shape = local[n].shape
        d, nm, nv = adamw(_view2d(local[n]), _view2d(gshard[n]), _view2d(mom_m[n]), _view2d(mom_v[n]), name="adamw")
        delta[n], new_m[n], new_v[n] = d.reshape(shape), nm.reshape(shape), nv.reshape(shape)
    return (loss, grad_x, *[gshard[n] for n in WEIGHTS], *[delta[n] for n in WEIGHTS], *[new_m[n] for n in WEIGHTS],
            *[new_v[n] for n in WEIGHTS])
```

```python
import functools

import jax
import jax.numpy as jnp
from jax import lax
from jax.experimental import pallas as pl
from jax.experimental.pallas import tpu as pltpu

F32 = jnp.float32
BF = jnp.bfloat16
MESH = pl.DeviceIdType.MESH

D_MODEL = 1024
DEPTH = 2
N_META = 16
D_CONV = 512
MLA_HEADS = 8
QK_NOPE = 64
QK_ROPE = 32
V_HEAD = 64
Q_LORA = 256
KV_LORA = 128
ROPE_BASE = 10000.0
NEG_INF = -1e30
D_FF = 2816
ALPHA = (2 * DEPTH) ** 0.25
LN_EPS = 1e-5
RMS_EPS = 1e-6
ATT_SCALE = (QK_NOPE + QK_ROPE) ** -0.5
LOG2E = 1.4426950408889634
LN2 = 0.6931471805599453
D_IN = 4000
D_IN_PAD = 4096
Q_PAD = 256
KV_PAD = 128

ADAM_LR = 0.001
ADAM_B1 = 0.9
ADAM_B2 = 0.999
ADAM_EPS = 1e-08
ADAM_WD = 0.01
ADAM_STEP = 10

TM = 768
TMH = 384
LANES = 128
COMM_COLS = 512
COMM_ROW_BLOCK = 1472
VMEM_LIMIT_BYTES = 50 * 1024 * 1024

NT = (((1,), (1,)), ((), ()))
TN = (((0,), (0,)), ((), ()))


def _pcall(body, **kw):
    return pl.pallas_call(body, **kw)


def _params(n_axes):
    return pltpu.CompilerParams(dimension_semantics=("arbitrary",) * n_axes, vmem_limit_bytes=VMEM_LIMIT_BYTES)


def _sds(shape, dtype):
    return jax.ShapeDtypeStruct(shape, dtype)


def mm_rows(pairs, n_out, *, name, tn=None, addend=None, add_scale=1.0, out_dtype=F32):
    tp = pairs[0][0].shape[0]
    tn = tn or n_out
    in_specs, args = [], []
    for a, b, nt, kb in pairs:
        k = a.shape[1]
        in_specs.append(pl.BlockSpec((TM, k), lambda i, j: (i, 0)))
        if nt:
            in_specs.append(pl.BlockSpec((tn, k), functools.partial(lambda i, j, kb: (j, kb), kb=kb)))
        else:
            in_specs.append(pl.BlockSpec((k, tn), lambda i, j: (0, j)))
        args += [a, b]
    if addend is not None:
        in_specs.append(pl.BlockSpec((TM, tn), lambda i, j: (i, j)))
        args.append(addend)
    n_pairs = len(pairs)
    nts = [p[2] for p in pairs]

    def body(*refs):
        o_ref = refs[-1]
        acc = None
        for p in range(n_pairs):
            a = refs[2 * p][...].astype(BF)
            b = refs[2 * p + 1][...]
            d = lax.dot_general(a, b, NT if nts[p] else (((1,), (0,)), ((), ())), preferred_element_type=F32)
            acc = d if acc is None else acc + d
        if addend is not None:
            acc = acc + add_scale * refs[2 * n_pairs][...]
        o_ref[...] = acc.astype(o_ref.dtype)

    return _pcall(
        body, name=name, grid=(tp // TM, n_out // tn), in_specs=in_specs,
        out_specs=pl.BlockSpec((TM, tn), lambda i, j: (i, j)), out_shape=_sds((tp, n_out), out_dtype),
        compiler_params=_params(2),
    )(*args)


def tn_mm(a, b, *, tm, name, out_dtype=BF, shard=None, slot0=0, dst=None):
    tp, m = a.shape
    n = b.shape[1]
    nk = tp // TM
    if shard is None:
        pieces, out_block, out_index, out_full = 1, (tm, n), (lambda i, k: (i, 0)), (m, n)
    elif shard[0] == "cols":
        pieces = n // shard[1]
        out_block, out_full = (pieces, tm, shard[1]), (4, m, shard[1])
        out_index = lambda i, k: (slot0 // pieces, i, 0)
    else:
        pieces = tm // shard[1]
        out_block, out_full = (pieces, shard[1], n), (4, m // 4, n)
        out_index = lambda i, k: (i, 0, 0)

    def body(a_ref, b_ref, *rest):
        o_ref, acc_ref = rest[-2], rest[-1]
        k = pl.program_id(1)

        @pl.when(k == 0)
        def _():
            acc_ref[...] = jnp.zeros_like(acc_ref)

        acc_ref[...] += lax.dot_general(a_ref[...].astype(BF), b_ref[...].astype(BF), TN, preferred_element_type=F32)

        @pl.when(k == nk - 1)
        def _():
            if shard is None:
                o_ref[...] = acc_ref[...].astype(o_ref.dtype)
            elif shard[0] == "cols":
                for j in range(pieces):
                    o_ref[j] = acc_ref[:, j * shard[1]:(j + 1) * shard[1]].astype(o_ref.dtype)
            else:
                for j in range(pieces):
                    o_ref[j] = acc_ref[j * shard[1]:(j + 1) * shard[1], :].astype(o_ref.dtype)

    in_specs = [pl.BlockSpec((TM, tm), lambda i, k: (k, i)), pl.BlockSpec((TM, n), lambda i, k: (k, 0))]
    args = [a, b]
    aliases = {}
    if dst is not None:
        in_specs.append(pl.BlockSpec(memory_space=pl.ANY))
        args.append(dst)
        aliases = {2: 0}
    return _pcall(
        body, name=name, grid=(m // tm, nk), in_specs=in_specs, out_specs=pl.BlockSpec(out_block, out_index),
        out_shape=_sds(out_full, out_dtype), input_output_aliases=aliases,
        scratch_shapes=[pltpu.VMEM((tm, n), F32)], compiler_params=_params(2),
    )(*args)


def _ln_store(z, g_ref, b_ref, xh_ref, rs_ref, hb_ref):
    mu = jnp.mean(z, axis=-1, keepdims=True)
    zc = z - mu
    var = jnp.mean(zc * zc, axis=-1, keepdims=True)
    rstd = lax.rsqrt(var + LN_EPS)
    xh = zc * rstd
    xh_ref[...] = xh
    rs_ref[...] = rstd
    hb_ref[...] = (xh * g_ref[...] + b_ref[...]).astype(BF)


def _ln_out(tp, tm=TM):
    specs = [pl.BlockSpec((tm, D_MODEL), lambda i: (i, 0)), pl.BlockSpec((tm, 1), lambda i: (i, 0)),
             pl.BlockSpec((tm, D_MODEL), lambda i: (i, 0))]
    shapes = [_sds((tp, D_MODEL), F32), _sds((tp, 1), F32), _sds((tp, D_MODEL), BF)]
    return specs, shapes


def _row_vec(n):
    return pl.BlockSpec((1, n), lambda i: (0, 0))


def ffn_up(hb, wup, *, name, side=None):
    tp = hb.shape[0]
    tn = D_FF // 2
    nj = D_FF // tn

    def body(in_refs, out_refs, scratch):
        h_ref, wg_ref, wu_ref = in_refs
        g_ref, u_ref, a_ref = out_refs
        h = h_ref[...]
        g = jnp.dot(h, wg_ref[...], preferred_element_type=F32)
        u = jnp.dot(h, wu_ref[...], preferred_element_type=F32)
        g_ref[...] = g.astype(BF)
        u_ref[...] = u.astype(BF)
        a_ref[...] = (g * jax.nn.sigmoid(g) * u).astype(BF)

    blk = pl.BlockSpec((TM, tn), lambda i, j: (i, j))
    return _side_call(
        body, side, name=name, grid=(tp // TM, nj),
        in_specs=[pl.BlockSpec((TM, D_MODEL), lambda i, j: (i, 0)), pl.BlockSpec((D_MODEL, tn), lambda i, j: (0, j)),
                  pl.BlockSpec((D_MODEL, tn), lambda i, j: (0, j + nj))],
        out_specs=[blk, blk, blk], out_shape=[_sds((tp, D_FF), BF)] * 3, scratch_shapes=[], args=(hb, wup, wup),
    )


def down_ln(a, wd, xprev, gp, bp, g, b, *, name):
    tp = a.shape[0]

    def body(a_ref, wd_ref, xp_ref, gp_ref, bp_ref, g_ref, b_ref, xh_ref, rs_ref, hb_ref):
        f = jnp.dot(a_ref[...], wd_ref[...], preferred_element_type=F32)
        hprev = xp_ref[...] * gp_ref[...] + bp_ref[...]
        _ln_store(ALPHA * hprev + 0.5 * f, g_ref, b_ref, xh_ref, rs_ref, hb_ref)

    out_specs, out_shape = _ln_out(tp)
    return _pcall(
        body, name=name, grid=(tp // TM,),
        in_specs=[pl.BlockSpec((TM, D_FF), lambda i: (i, 0)), pl.BlockSpec((D_FF, D_MODEL), lambda i: (0, 0)),
                  pl.BlockSpec((TM, D_MODEL), lambda i: (i, 0))] + [_row_vec(D_MODEL)] * 4,
        out_specs=out_specs, out_shape=out_shape, compiler_params=_params(1),
    )(a, wd, xprev, gp, bp, g, b)


def _rope(x, c, s1, s2, reps):
    n = x.shape[1]
    if reps > 1:
        c, s1, s2 = (jnp.tile(t, (1, reps)) for t in (c, s1, s2))
    return x * c + pltpu.roll(x, 16, 1) * s1 + pltpu.roll(x, n - 16, 1) * s2


def _rms(x, g):
    r = lax.rsqrt(jnp.mean(x * x, axis=-1, keepdims=True) + RMS_EPS)
    return x * r * g, r


def mla_prep(p, gq, gkv, wuq_p, wukv, tabs, *, name):
    tp = p.shape[0]
    nh = MLA_HEADS

    def body(cq_ref, ckv_ref, kr_ref, gq_ref, gkv_ref, wuq_ref, wukv_ref, cq_t, s1q_t, s2q_t, ck_t, s1k_t, s2k_t,
             cqn_ref, ckvn_ref, q2_ref, kv_ref, krr_ref):
        cqn, _ = _rms(cq_ref[...].astype(F32), gq_ref[...])
        ckvn, _ = _rms(ckv_ref[...].astype(F32), gkv_ref[...])
        cqn = cqn.astype(BF)
        ckvn = ckvn.astype(BF)
        cqn_ref[...] = cqn
        ckvn_ref[...] = ckvn
        q = jnp.dot(cqn, wuq_ref[...], preferred_element_type=F32)
        q2_ref[...] = _rope(q, cq_t[...], s1q_t[...], s2q_t[...], nh).astype(BF)
        kv_ref[...] = jnp.dot(ckvn, wukv_ref[...], preferred_element_type=F32).astype(BF)
        krr_ref[...] = _rope(kr_ref[...].astype(F32), ck_t[...], s1k_t[...], s2k_t[...], 1).astype(BF)

    def rows(n, col=0):
        return pl.BlockSpec((TMH, n), functools.partial(lambda i, col: (i, col), col=col))

    return _pcall(
        body, name=name, grid=(tp // TMH,),
        in_specs=[rows(Q_LORA, 1536 // Q_LORA), rows(KV_LORA, 1792 // KV_LORA), rows(LANES, 1920 // LANES),
                  _row_vec(Q_LORA), _row_vec(KV_LORA),
                  pl.BlockSpec((Q_LORA, nh * Q_PAD), lambda i: (0, 0)), pl.BlockSpec((KV_LORA, nh * KV_PAD), lambda i: (0, 0)),
                  rows(Q_PAD), rows(Q_PAD), rows(Q_PAD), rows(LANES), rows(LANES), rows(LANES)],
        out_specs=[rows(Q_LORA), rows(KV_LORA), rows(nh * Q_PAD), rows(nh * KV_PAD), rows(LANES)],
        out_shape=[_sds((tp, Q_LORA), BF), _sds((tp, KV_LORA), BF), _sds((tp, nh * Q_PAD), BF),
                   _sds((tp, nh * KV_PAD), BF), _sds((tp, LANES), BF)],
        compiler_params=_params(1),
    )(p, p, p, gq, gkv, wuq_p, wukv, *tabs)


def _causal_mask(s):
    qpos = lax.broadcasted_iota(jnp.int32, (TM, TM), 0)
    kpos = lax.broadcasted_iota(jnp.int32, (TM, TM), 1)
    return jnp.where(kpos <= qpos, s, NEG_INF)


def _key_rows(k):
    return pl.ds(pl.multiple_of(k * TM, TM), TM)


def _pipelined_key_blocks(n, prefetch, process):
    prefetch(0, 0)

    def pair(j, carry):
        prefetch(2 * j + 1, 1)
        process(2 * j, 0, False)
        prefetch(2 * j + 2, 0)
        process(2 * j + 1, 1, False)
        return carry

    lax.fori_loop(0, n // 2, pair, 0)

    @pl.when(n % 2 == 1)
    def _():
        prefetch(n, 1)
        process(n - 1, 0, False)
        process(n, 1, True)

    @pl.when(n % 2 == 0)
    def _():
        process(n, 0, True)


def _side_call(body_main, side, *, name, grid, in_specs, out_specs, out_shape, scratch_shapes, args):
    n_in, n_out, n_scr = len(in_specs), len(out_specs), len(scratch_shapes)
    s_in, s_pre, n_sems, program = side if side is not None else ((), (), 0, None)
    a, b = len(s_in), len(s_pre)

    def body(*refs):
        in_refs = refs[:n_in]
        out_refs = refs[n_in + a + b:n_in + a + b + n_out]
        scr = refs[n_in + a + 2 * b + n_out:n_in + a + 2 * b + n_out + n_scr]
        if side is not None:
            side_in = refs[n_in:n_in + a]
            side_out = refs[n_in + a + b + n_out:n_in + a + 2 * b + n_out]
            start, finish = program(side_in, side_out, refs[-2], refs[-1])

            @pl.when((pl.program_id(0) == 0) & (pl.program_id(1) == 0))
            def _():
                start()

        body_main(in_refs, out_refs, scr)
        if side is not None:
            @pl.when((pl.program_id(0) == grid[0] - 1) & (pl.program_id(1) == grid[1] - 1))
            def _():
                finish()

    sems = [pltpu.SemaphoreType.DMA((n_sems,))] * 2 if side is not None else []
    return _pcall(
        body, name=name, grid=grid, in_specs=list(in_specs) + [HBM_SPEC] * (a + b), out_specs=list(out_specs) + [HBM_SPEC] * b,
        out_shape=list(out_shape) + [_sds(p.shape, p.dtype) for p in s_pre],
        input_output_aliases={n_in + a + i: n_out + i for i in range(b)},
        scratch_shapes=list(scratch_shapes) + sems, compiler_params=_params(2),
    )(*args, *s_in, *s_pre)


def attn_fwd(q2, kv, kr, *, name, side=None):
    tp = q2.shape[0]
    nh = MLA_HEADS
    nb = tp // TM
    rep = TM // LANES

    def body(in_refs, out_refs, scratch):
        q_ref, kv_ref, kr_ref = in_refs
        o_ref, lse_ref = out_refs
        m_ref, l_ref, acc_ref, s0_ref, s1_ref = scratch
        qi = pl.program_id(1)
        s_refs = (s0_ref, s1_ref)
        m_ref[...] = jnp.full_like(m_ref, NEG_INF)
        l_ref[...] = jnp.zeros_like(l_ref)
        acc_ref[...] = jnp.zeros_like(acc_ref)

        def prefetch(k, slot):
            k2 = jnp.concatenate([kv_ref[_key_rows(k), :], kr_ref[_key_rows(k), :]], axis=1)
            s_refs[slot][...] = lax.dot_general(q_ref[...], k2, NT, preferred_element_type=F32)

        def process(k, slot, diagonal):
            s = s_refs[slot][...]
            if diagonal:
                s = _causal_mask(s)
            m_prev = m_ref[...]
            m_new = jnp.maximum(m_prev, jnp.max(s, axis=1, keepdims=True))
            alpha = jnp.exp2(m_prev - m_new)
            p = jnp.exp2(s - jnp.tile(m_new, (1, rep)))
            l_ref[...] = alpha * l_ref[...] + jnp.sum(p, axis=1, keepdims=True)
            acc_ref[...] = alpha * acc_ref[...] + jnp.dot(p.astype(BF), kv_ref[_key_rows(k), :], preferred_element_type=F32)
            m_ref[...] = m_new

        _pipelined_key_blocks(qi, prefetch, process)
        o_ref[...] = (acc_ref[...] / l_ref[...]).astype(BF)
        lse_ref[...] = m_ref[...] + jnp.log2(l_ref[...])

    return _side_call(
        body, side, name=name, grid=(nh, nb),
        in_specs=[pl.BlockSpec((TM, Q_PAD), lambda h, qi: (qi, h)), pl.BlockSpec((tp, KV_PAD), lambda h, qi: (0, h)),
                  pl.BlockSpec((tp, LANES), lambda h, qi: (0, 0))],
        out_specs=[pl.BlockSpec((TM, KV_PAD), lambda h, qi: (qi, h)), pl.BlockSpec((TM, LANES), lambda h, qi: (qi, h))],
        out_shape=[_sds((tp, nh * KV_PAD), BF), _sds((tp, nh * LANES), F32)],
        scratch_shapes=[pltpu.VMEM((TM, LANES), F32)] * 3 + [pltpu.VMEM((TM, TM), F32)] * 2, args=(q2, kv, kr),
    )


def conv_fwd(p, w, *, name):
    tp = p.shape[0]

    def body(b_ref, c_ref, h_ref, w_ref, y_ref, cv_ref, ebuf):
        i = pl.program_id(0)

        @pl.when(i == 0)
        def _():
            ebuf[0:8, :] = jnp.zeros((8, D_CONV), F32)

        e = c_ref[...].astype(F32) * h_ref[...].astype(F32)
        ebuf[8:8 + TM, :] = e
        w_all = w_ref[...]
        conv = w_all[0:1] * ebuf[pl.ds(6, TM), :] + w_all[1:2] * ebuf[pl.ds(7, TM), :] + w_all[2:3] * e
        cv_ref[...] = conv.astype(BF)
        y_ref[...] = (b_ref[...].astype(F32) * conv).astype(BF)
        ebuf[0:8, :] = ebuf[TM:TM + 8, :]

    def col(j):
        return pl.BlockSpec((TM, D_CONV), functools.partial(lambda i, j: (i, j), j=j))

    return _pcall(
        body, name=name, grid=(tp // TM,),
        in_specs=[col(0), col(1), col(2), pl.BlockSpec((3, D_CONV), lambda i: (0, 0))],
        out_specs=[col(0), col(0)], out_shape=[_sds((tp, D_CONV), BF)] * 2,
        scratch_shapes=[pltpu.VMEM((TM + 8, D_CONV), F32)], compiler_params=_params(1),
    )(p, p, p, w)


def merge_out_ln(ycv, o2, p, bg, wbc, wbm_p, wo, xprev, gp, bp, g, b, *, name):
    tp = ycv.shape[0]

    def body(y_ref, o_ref, gc_ref, gm_ref, bg_ref, wbc_ref, wbm_ref, wo_ref, xp_ref, gp_ref, bp_ref, g_ref, b_ref,
             bc_ref, bm_ref, mg_ref, xh_ref, rs_ref, hb_ref):
        bc = jnp.dot(y_ref[...], wbc_ref[...], preferred_element_type=F32)
        bm = jnp.dot(o_ref[...], wbm_ref[...], preferred_element_type=F32)
        bgv = bg_ref[...]
        mg = (jax.nn.sigmoid(gc_ref[...].astype(F32) + bgv[0:1]) * bc
              + jax.nn.sigmoid(gm_ref[...].astype(F32) + bgv[1:2]) * bm)
        mgb = mg.astype(BF)
        bc_ref[...] = bc.astype(BF)
        bm_ref[...] = bm.astype(BF)
        mg_ref[...] = mgb
        mix = jnp.dot(mgb, wo_ref[...], preferred_element_type=F32)
        hprev = xp_ref[...] * gp_ref[...] + bp_ref[...]
        _ln_store(ALPHA * hprev + mix, g_ref, b_ref, xh_ref, rs_ref, hb_ref)

    def rows(n, col=0):
        return pl.BlockSpec((TMH, n), functools.partial(lambda i, col: (i, col), col=col))

    def whole(r, c):
        return pl.BlockSpec((r, c), lambda i: (0, 0))

    ln_specs, ln_shapes = _ln_out(tp, TMH)
    return _pcall(
        body, name=name, grid=(tp // TMH,),
        in_specs=[rows(D_CONV), rows(MLA_HEADS * KV_PAD), rows(D_MODEL, 2), rows(D_MODEL, 3), whole(2, D_MODEL),
                  whole(D_CONV, D_MODEL), whole(MLA_HEADS * KV_PAD, D_MODEL), whole(D_MODEL, D_MODEL), rows(D_MODEL)]
        + [_row_vec(D_MODEL)] * 4,
        out_specs=[rows(D_MODEL)] * 3 + ln_specs, out_shape=[_sds((tp, D_MODEL), BF)] * 3 + ln_shapes,
        compiler_params=_params(1),
    )(ycv, o2, p, p, bg, wbc, wbm_p, wo, xprev, gp, bp, g, b)


def loss_grad(xh, g, b, target_p, n_real, *, name):
    tp = xh.shape[0]

    def body(x_ref, g_ref, b_ref, t_ref, dy_ref, loss_ref):
        i = pl.program_id(0)

        @pl.when(i == 0)
        def _():
            loss_ref[...] = jnp.zeros_like(loss_ref)

        row = i * TM + lax.broadcasted_iota(jnp.int32, (TM, 1), 0)
        real = (row >= N_META) & (row < N_META + n_real)
        diff = jnp.where(real, x_ref[...] * g_ref[...] + b_ref[...] - t_ref[...], 0.0)
        dy_ref[...] = diff * (1.0 / D_MODEL)
        loss_ref[...] += 0.5 / D_MODEL * jnp.sum(diff * diff)

    return _pcall(
        body, name=name, grid=(tp // TM,),
        in_specs=[pl.BlockSpec((TM, D_MODEL), lambda i: (i, 0)), _row_vec(D_MODEL), _row_vec(D_MODEL),
                  pl.BlockSpec((TM, D_MODEL), lambda i: (i, 0))],
        out_specs=[pl.BlockSpec((TM, D_MODEL), lambda i: (i, 0)), pl.BlockSpec((8, LANES), lambda i: (0, 0))],
        out_shape=[_sds((tp, D_MODEL), F32), _sds((8, LANES), F32)], compiler_params=_params(1),
    )(xh, g, b, target_p)


def ln_bwd(dh, xh, rstd, g, *, branch_scale, name):
    tp = dh.shape[0]

    def body(dh_ref, xh_ref, rs_ref, g_ref, dz_ref, dzb_ref, dg_ref, db_ref):
        i = pl.program_id(0)

        @pl.when(i == 0)
        def _():
            dg_ref[...] = jnp.zeros_like(dg_ref)
            db_ref[...] = jnp.zeros_like(db_ref)

        dy = dh_ref[...]
        xhat = xh_ref[...]
        dg_ref[...] += jnp.sum(dy * xhat, axis=0, keepdims=True)
        db_ref[...] += jnp.sum(dy, axis=0, keepdims=True)
        dxh = dy * g_ref[...]
        m1 = jnp.mean(dxh, axis=-1, keepdims=True)
        m2 = jnp.mean(dxh * xhat, axis=-1, keepdims=True)
        dz = rs_ref[...] * (dxh - m1 - xhat * m2)
        dz_ref[...] = dz
        dzb_ref[...] = (branch_scale * dz).astype(BF)

    rows = pl.BlockSpec((TM, D_MODEL), lambda i: (i, 0))
    return _pcall(
        body, name=name, grid=(tp // TM,),
        in_specs=[rows, rows, pl.BlockSpec((TM, 1), lambda i: (i, 0)), _row_vec(D_MODEL)],
        out_specs=[rows, rows, _row_vec(D_MODEL), _row_vec(D_MODEL)],
        out_shape=[_sds((tp, D_MODEL), F32), _sds((tp, D_MODEL), BF), _sds((1, D_MODEL), F32), _sds((1, D_MODEL), F32)],
        compiler_params=_params(1),
    )(dh, xh, rstd, g)


def ffn_down_bwd(dzb, wd, gate, up, *, name):
    tp = dzb.shape[0]
    tn = D_FF // 2

    def body(dz_ref, wd_ref, g_ref, u_ref, dg_ref, du_ref):
        da = lax.dot_general(dz_ref[...], wd_ref[...], NT, preferred_element_type=F32)
        g = g_ref[...].astype(F32)
        u = u_ref[...].astype(F32)
        sg = jax.nn.sigmoid(g)
        dg_ref[...] = (da * u * sg * (1.0 + g * (1.0 - sg))).astype(BF)
        du_ref[...] = (da * g * sg).astype(BF)

    blk = pl.BlockSpec((TM, tn), lambda i, j: (i, j))
    return _pcall(
        body, name=name, grid=(tp // TM, D_FF // tn),
        in_specs=[pl.BlockSpec((TM, D_MODEL), lambda i, j: (i, 0)), pl.BlockSpec((tn, D_MODEL), lambda i, j: (j, 0)), blk, blk],
        out_specs=[blk, blk], out_shape=[_sds((tp, D_FF), BF)] * 2, compiler_params=_params(2),
    )(dzb, wd, gate, up)


def merge_bwd(dzb, wo, bc, bm, p, bg, wbc, wbm_p, o2, *, name):
    tp = dzb.shape[0]
    nh = MLA_HEADS

    def body(dz_ref, wo_ref, bc_ref, bm_ref, gc_ref, gm_ref, bg_ref, wbc_ref, wbm_ref, o_ref,
             dbc_ref, dbm_ref, dgg_ref, dy_ref, do_ref, dl_ref, dbg_ref):
        i = pl.program_id(0)

        @pl.when(i == 0)
        def _():
            dbg_ref[...] = jnp.zeros_like(dbg_ref)

        dmg = lax.dot_general(dz_ref[...], wo_ref[...], NT, preferred_element_type=F32)
        bgv = bg_ref[...]
        sc = jax.nn.sigmoid(gc_ref[...].astype(F32) + bgv[0:1])
        sm = jax.nn.sigmoid(gm_ref[...].astype(F32) + bgv[1:2])
        dbc = (dmg * sc).astype(BF)
        dbm = (dmg * sm).astype(BF)
        dgc = dmg * bc_ref[...].astype(F32) * sc * (1.0 - sc)
        dgm = dmg * bm_ref[...].astype(F32) * sm * (1.0 - sm)
        dbc_ref[...] = dbc
        dbm_ref[...] = dbm
        dgg_ref[...] = jnp.concatenate([dgc, dgm], axis=1).astype(BF)
        dbg_ref[...] += jnp.concatenate([jnp.sum(dgc, axis=0, keepdims=True), jnp.sum(dgm, axis=0, keepdims=True)], axis=0)
        dy_ref[...] = lax.dot_general(dbc, wbc_ref[...], NT, preferred_element_type=F32)
        do = lax.dot_general(dbm, wbm_ref[...], NT, preferred_element_type=F32)
        do_ref[...] = do.astype(BF)
        prod = do * o_ref[...].astype(F32)
        parts = []
        for h in range(nh):
            d = jnp.sum(prod[:, h * KV_PAD:(h + 1) * KV_PAD], axis=1, keepdims=True)
            parts.append(jnp.broadcast_to(d, (TMH, LANES)))
        dl_ref[...] = jnp.concatenate(parts, axis=1)

    def rows(n, col=0):
        return pl.BlockSpec((TMH, n), functools.partial(lambda i, col: (i, col), col=col))

    def whole(r, c):
        return pl.BlockSpec((r, c), lambda i: (0, 0))

    return _pcall(
        body, name=name, grid=(tp // TMH,),
        in_specs=[rows(D_MODEL), whole(D_MODEL, D_MODEL), rows(D_MODEL), rows(D_MODEL), rows(D_MODEL, 2), rows(D_MODEL, 3),
                  whole(2, D_MODEL), whole(D_CONV, D_MODEL), whole(nh * KV_PAD, D_MODEL), rows(nh * KV_PAD)],
        out_specs=[rows(D_MODEL), rows(D_MODEL), rows(2 * D_MODEL, 1), rows(D_CONV), rows(nh * KV_PAD), rows(nh * LANES),
                   whole(2, D_MODEL)],
        out_shape=[_sds((tp, D_MODEL), BF), _sds((tp, D_MODEL), BF), _sds((tp, D_IN_PAD), BF), _sds((tp, D_CONV), F32),
                   _sds((tp, nh * KV_PAD), BF), _sds((tp, nh * LANES), F32), _sds((2, D_MODEL), F32)],
        compiler_params=_params(1),
    )(dzb, wo, bc, bm, p, p, bg, wbc, wbm_p, o2)


def attn_bwd(q2, kv, kr, do2, lse, dl, *, name, side=None):
    tp = q2.shape[0]
    nh = MLA_HEADS
    nb = tp // TM
    rep = TM // LANES

    def body(in_refs, out_refs, scratch):
        q_ref, kv_ref, kr_ref, do_ref, lse_ref, dl_ref = in_refs
        dq_ref, dkv_ref, dkr_ref = out_refs
        dq_acc, s0_ref, s1_ref, dp0_ref, dp1_ref = scratch
        qi = pl.program_id(1)
        s_refs, dp_refs = (s0_ref, s1_ref), (dp0_ref, dp1_ref)

        @pl.when(qi == 0)
        def _():
            dkv_ref[...] = jnp.zeros_like(dkv_ref)
            dkr_ref[...] = jnp.zeros_like(dkr_ref)

        dq_acc[...] = jnp.zeros_like(dq_acc)

        def prefetch(k, slot):
            kvb = kv_ref[_key_rows(k), :]
            k2 = jnp.concatenate([kvb, kr_ref[_key_rows(k), :]], axis=1)
            s_refs[slot][...] = lax.dot_general(q_ref[...], k2, NT, preferred_element_type=F32)
            dp_refs[slot][...] = lax.dot_general(do_ref[...], kvb, NT, preferred_element_type=F32)

        def process(k, slot, diagonal):
            rows = _key_rows(k)
            s = s_refs[slot][...]
            if diagonal:
                s = _causal_mask(s)
            p = jnp.exp2(s - jnp.tile(lse_ref[...], (1, rep)))
            dsb = (p * (dp_refs[slot][...] - jnp.tile(dl_ref[...], (1, rep)))).astype(BF)
            dk2 = lax.dot_general(dsb, q_ref[...], TN, preferred_element_type=F32) * LN2
            dkv_ref[rows, :] += lax.dot_general(p.astype(BF), do_ref[...], TN, preferred_element_type=F32) + dk2[:, :KV_PAD]
            dkr_ref[rows, :] += dk2[:, KV_PAD:KV_PAD + LANES]
            k2 = jnp.concatenate([kv_ref[rows, :], kr_ref[rows, :]], axis=1)
            dq_acc[...] += jnp.dot(dsb, k2, preferred_element_type=F32)

        _pipelined_key_blocks(qi, prefetch, process)
        dq_ref[...] = dq_acc[...]

    def qrow(n):
        return pl.BlockSpec((TM, n), lambda h, qi: (qi, h))

    def head(n):
        return pl.BlockSpec((tp, n), lambda h, qi: (0, h))

    return _side_call(
        body, side, name=name, grid=(nh, nb),
        in_specs=[qrow(Q_PAD), head(KV_PAD), pl.BlockSpec((tp, LANES), lambda h, qi: (0, 0)), qrow(KV_PAD), qrow(LANES), qrow(LANES)],
        out_specs=[qrow(Q_PAD), head(KV_PAD), head(LANES)],
        out_shape=[_sds((tp, nh * Q_PAD), F32), _sds((tp, nh * KV_PAD), F32), _sds((tp, nh * LANES), F32)],
        scratch_shapes=[pltpu.VMEM((TM, Q_PAD), F32)] + [pltpu.VMEM((TM, TM), F32)] * 4, args=(q2, kv, kr, do2, lse, dl),
    )


def _rms_bwd(x, g, dy):
    r = lax.rsqrt(jnp.mean(x * x, axis=-1, keepdims=True) + RMS_EPS)
    gy = dy * g
    dx = r * gy - x * (r * r * r) * jnp.mean(x * gy, axis=-1, keepdims=True)
    return dx, jnp.sum(dy * x * r, axis=0, keepdims=True)


def mla_prep_bwd(dq2, dkv, dkr, p, gq, gkv, wuq_p, wukv, tabs_bwd, dp, *, name):
    tp = dq2.shape[0]
    nh = MLA_HEADS

    def body(dq_ref, dkv_ref, dkr_ref, cq_ref, ckv_ref, gq_ref, gkv_ref, wuq_ref, wukv_ref,
             cq_t, s1q_t, s2q_t, ck_t, s1k_t, s2k_t, dp_in_ref, dqb_ref, dsm_ref, dgq_ref, dgkv_ref):
        i = pl.program_id(0)

        @pl.when(i == 0)
        def _():
            dgq_ref[...] = jnp.zeros_like(dgq_ref)
            dgkv_ref[...] = jnp.zeros_like(dgkv_ref)

        dqb = _rope(dq_ref[...], cq_t[...], s1q_t[...], s2q_t[...], nh).astype(BF)
        dqb_ref[...] = dqb
        dcqn = lax.dot_general(dqb, wuq_ref[...], NT, preferred_element_type=F32)
        dcq, dgq = _rms_bwd(cq_ref[...].astype(F32), gq_ref[...], dcqn)
        dckvn = lax.dot_general(dkv_ref[...].astype(BF), wukv_ref[...], NT, preferred_element_type=F32)
        dckv, dgkv = _rms_bwd(ckv_ref[...].astype(F32), gkv_ref[...], dckvn)
        dkr_heads = dkr_ref[...]
        dkr_sum = dkr_heads[:, :LANES]
        for h in range(1, nh):
            dkr_sum = dkr_sum + dkr_heads[:, h * LANES:(h + 1) * LANES]
        dkr = _rope(dkr_sum, ck_t[...], s1k_t[...], s2k_t[...], 1)
        dsm_ref[...] = jnp.concatenate([dcq, dckv, dkr], axis=1).astype(BF)
        dgq_ref[...] += dgq
        dgkv_ref[...] += dgkv

    def rows(n, col=0):
        return pl.BlockSpec((TMH, n), functools.partial(lambda i, col: (i, col), col=col))

    return _pcall(
        body, name=name, grid=(tp // TMH,),
        in_specs=[rows(nh * Q_PAD), rows(nh * KV_PAD), rows(nh * LANES), rows(Q_LORA, 1536 // Q_LORA), rows(KV_LORA, 1792 // KV_LORA),
                  _row_vec(Q_LORA), _row_vec(KV_LORA),
                  pl.BlockSpec((Q_LORA, nh * Q_PAD), lambda i: (0, 0)), pl.BlockSpec((KV_LORA, nh * KV_PAD), lambda i: (0, 0)),
                  rows(Q_PAD), rows(Q_PAD), rows(Q_PAD), rows(LANES), rows(LANES), rows(LANES), pl.BlockSpec(memory_space=pl.ANY)],
        out_specs=[rows(nh * Q_PAD), rows(Q_LORA + KV_LORA + LANES, 1536 // (Q_LORA + KV_LORA + LANES)), _row_vec(Q_LORA),
                   _row_vec(KV_LORA)],
        out_shape=[_sds((tp, nh * Q_PAD), BF), _sds(dp.shape, dp.dtype), _sds((1, Q_LORA), F32), _sds((1, KV_LORA), F32)],
        input_output_aliases={15: 1}, compiler_params=_params(1),
    )(dq2, dkv, dkr, p, p, gq, gkv, wuq_p, wukv, *tabs_bwd, dp)


def conv_bwd(dy, p, conv, w, dp, *, name):
    tp = dy.shape[0]
    nb = tp // TM

    def body(dy_ref, b_ref, c_ref, h_ref, cv_ref, w_ref, dp_in_ref, dp_ref, dw0_ref, dw1_ref, dw2_ref, dbuf):
        i = pl.program_id(0)

        @pl.when(i == 0)
        def _():
            dbuf[TM:TM + 8, :] = jnp.zeros((8, D_CONV), F32)
            dw0_ref[...] = jnp.zeros_like(dw0_ref)
            dw1_ref[...] = jnp.zeros_like(dw1_ref)
            dw2_ref[...] = jnp.zeros_like(dw2_ref)

        dyv = dy_ref[...]
        c = c_ref[...].astype(F32)
        hh = h_ref[...].astype(F32)
        dconv = dyv * b_ref[...].astype(F32)
        dbuf[0:TM, :] = dconv
        d1 = dbuf[pl.ds(1, TM), :]
        d2 = dbuf[pl.ds(2, TM), :]
        w_all = w_ref[...]
        de = w_all[2:3] * dconv + w_all[1:2] * d1 + w_all[0:1] * d2
        e = c * hh
        dp_ref[...] = jnp.concatenate([dyv * cv_ref[...].astype(F32), de * hh, de * c], axis=1).astype(BF)
        dw0_ref[...] += jnp.sum(d2 * e, axis=0, keepdims=True)
        dw1_ref[...] += jnp.sum(d1 * e, axis=0, keepdims=True)
        dw2_ref[...] += jnp.sum(dconv * e, axis=0, keepdims=True)
        dbuf[TM:TM + 8, :] = dbuf[0:8, :]

    def col(j):
        return pl.BlockSpec((TM, D_CONV), functools.partial(lambda i, j: (nb - 1 - i, j), j=j))

    return _pcall(
        body, name=name, grid=(nb,),
        in_specs=[col(0), col(0), col(1), col(2), col(0), pl.BlockSpec((3, D_CONV), lambda i: (0, 0)),
                  pl.BlockSpec(memory_space=pl.ANY)],
        out_specs=[pl.BlockSpec((TM, 3 * D_CONV), lambda i: (nb - 1 - i, 0))] + [_row_vec(D_CONV)] * 3,
        out_shape=[_sds(dp.shape, dp.dtype)] + [_sds((1, D_CONV), F32)] * 3, input_output_aliases={6: 0},
        scratch_shapes=[pltpu.VMEM((TM + 8, D_CONV), F32)], compiler_params=_params(1),
    )(dy, p, p, p, conv, w, dp)


def adamw(w, g, m, v, *, name):
    r, c = w.shape
    tr = r
    for cand in (256, 128, 64, 32, 16, 8):
        if r % cand == 0 and r > cand:
            tr = cand
            break

    def body(w_ref, g_ref, m_ref, v_ref, d_ref, nm_ref, nv_ref):
        gv = g_ref[...]
        nm = ADAM_B1 * m_ref[...] + (1.0 - ADAM_B1) * gv
        nv = ADAM_B2 * v_ref[...] + (1.0 - ADAM_B2) * (gv * gv)
        m_hat = nm / (1.0 - ADAM_B1 ** ADAM_STEP)
        v_hat = nv / (1.0 - ADAM_B2 ** ADAM_STEP)
        d_ref[...] = -ADAM_LR * (m_hat / (jnp.sqrt(v_hat) + ADAM_EPS) + ADAM_WD * w_ref[...])
        nm_ref[...] = nm
        nv_ref[...] = nv

    blk = pl.BlockSpec((tr, c), lambda i: (i, 0))
    return _pcall(
        body, name=name, grid=(r // tr,), in_specs=[blk] * 4, out_specs=[blk] * 3,
        out_shape=[_sds((r, c), F32)] * 3, compiler_params=_params(1),
    )(w, g, m, v)


HBM_SPEC = pl.BlockSpec(memory_space=pltpu.HBM)


def _place():
    return lax.axis_index("x"), lax.axis_index("y"), lax.axis_index("c")


def _other_chips(x, y):
    return [(1 - x, y), (x, 1 - y), (1 - x, 1 - y)]


def _half(ref_or_shape_rows, c):
    return pl.ds(c * (ref_or_shape_rows // 2), ref_or_shape_rows // 2)


def gather_side(items):
    n = len(items)
    shards = [s for s, _ in items]
    layers = [l for _, l in items]

    def program(x_refs, o_refs, send_sems, recv_sems):
        x, y, c = _place()
        me = 2 * x + y
        chips = _other_chips(x, y)

        def copy(sem, src, dst, to):
            return pltpu.make_async_remote_copy(src_ref=src, dst_ref=dst, send_sem=send_sems.at[sem], recv_sem=recv_sems.at[sem],
                                                device_id=to, device_id_type=MESH)

        def src(i):
            return x_refs[i].at[layers[i], _half(x_refs[i].shape[1], c)]

        def dst(i, slot, cc):
            return o_refs[i].at[slot, _half(o_refs[i].shape[1], cc)]

        sends = [copy(6 * i + k, src(i), dst(i, me, c), (px, py, c)) for i in range(n) for k, (px, py) in enumerate(chips)]
        passed = [copy(6 * i + 3 + k, dst(i, 2 * px + py, c), dst(i, 2 * px + py, c), (x, y, 1 - c))
                  for k, (px, py) in enumerate(chips) for i in range(n)]

        def start():
            for cp in sends:
                cp.start()

        def finish():
            pos = 0
            for k, (px, py) in enumerate(chips):
                for i in range(n):
                    copy(6 * i + k, src(i), dst(i, 2 * px + py, c), (px, py, c)).wait_recv()
                    passed[pos].start()
                    pos += 1
            for k, (px, py) in enumerate(chips):
                for i in range(n):
                    copy(6 * i + 3 + k, dst(i, 2 * px + py, 1 - c), dst(i, 2 * px + py, 1 - c), (x, y, 1 - c)).wait_recv()
            for cp in sends + passed:
                cp.wait_send()

        return start, finish

    prefilled = [jnp.broadcast_to(s[l][None], (4,) + s.shape[1:]) for s, l in items]
    return shards, prefilled, 6 * n, program


def scatter_side(pss):
    n = len(pss)

    def program(p_refs, o_refs, send_sems, recv_sems):
        x, y, c = _place()
        me = 2 * x + y
        chips = _other_chips(x, y)

        def copy(i, k, j_src, j_dst, to):
            return pltpu.make_async_remote_copy(src_ref=p_refs[i].at[j_src], dst_ref=o_refs[i].at[j_dst],
                                                send_sem=send_sems.at[3 * i + k], recv_sem=recv_sems.at[3 * i + k],
                                                device_id=to, device_id_type=MESH)

        sends = [copy(i, k, 2 * px + py, me, (px, py, c)) for i in range(n) for k, (px, py) in enumerate(chips)]

        def start():
            for cp in sends:
                cp.start()

        def finish():
            for i in range(n):
                for k, (px, py) in enumerate(chips):
                    copy(i, k, me, 2 * px + py, (px, py, c)).wait_recv()
            for cp in sends:
                cp.wait_send()

        return start, finish

    xi, yi, _ = _place()
    own = jnp.arange(4)[:, None, None] == 2 * xi + yi
    prefilled = [jnp.where(own, p, jnp.zeros_like(p)) for p in pss]
    return list(pss), prefilled, 3 * n, program


def exchange_alone(side, *, name):
    inputs, prefilled, n_sems, program = side
    a, b = len(inputs), len(prefilled)

    def body(*refs):
        start, finish = program(refs[:a], refs[a + b:a + 2 * b], refs[-2], refs[-1])
        start()
        finish()

    return _pcall(
        body, name=name, in_specs=[HBM_SPEC] * (a + b), out_specs=[HBM_SPEC] * b, out_shape=[_sds(p.shape, p.dtype) for p in prefilled],
        input_output_aliases={a + i: i for i in range(b)}, scratch_shapes=[pltpu.SemaphoreType.DMA((n_sems,))] * 2,
    )(*inputs, *prefilled)


def pair_exchange(gs, *, name):
    n = len(gs)

    def body(*refs):
        g_refs, o_refs = refs[:n], refs[n:2 * n]
        send_sems, recv_sems = refs[2 * n:]
        x, y, c = _place()
        cps = [pltpu.make_async_remote_copy(src_ref=g_refs[i].at[:, _half(g_refs[i].shape[1], 1 - c)], dst_ref=o_refs[i],
                                            send_sem=send_sems.at[i], recv_sem=recv_sems.at[i], device_id=(x, y, 1 - c),
                                            device_id_type=MESH)
               for i in range(n)]
        for cp in cps:
            cp.start()
        for cp in cps:
            cp.wait()

    return _pcall(
        body, name=name, in_specs=[HBM_SPEC] * n, out_specs=[HBM_SPEC] * n,
        out_shape=[_sds((4, g.shape[1] // 2, g.shape[2]), g.dtype) for g in gs],
        scratch_shapes=[pltpu.SemaphoreType.DMA((n,)), pltpu.SemaphoreType.DMA((n,))],
    )(*gs)


def _comm_rows(a, b, itemsize):
    return a // 2 if a * b * itemsize > (3 << 19) and a % 16 == 0 else a


def pair_add(g, s1, c_idx, *, name):
    n, a, b = g.shape
    ah = a // 2
    ta = _comm_rows(ah, b, 2)
    nblk = ah // ta

    def body(c_ref, g_ref, s_ref, o_ref):
        o_ref[...] = (g_ref[...].astype(F32) + s_ref[...].astype(F32)).astype(o_ref.dtype)

    grid_spec = pltpu.PrefetchScalarGridSpec(
        num_scalar_prefetch=1, grid=(n, nblk),
        in_specs=[pl.BlockSpec((1, ta, b), lambda j, i, c_ref: (j, c_ref[0] * nblk + i, 0)),
                  pl.BlockSpec((1, ta, b), lambda j, i, c_ref: (j, i, 0))],
        out_specs=pl.BlockSpec((1, ta, b), lambda j, i, c_ref: (j, i, 0)),
    )
    return _pcall(body, name=name, grid_spec=grid_spec, out_shape=_sds((n, ah, b), g.dtype), compiler_params=_params(2))(
        c_idx, g, s1)


def sum_chunks(s2, *, name):
    n, a, b = s2.shape
    ta = _comm_rows(a, b, 4)

    def body(s_ref, o_ref):
        acc = s_ref[0].astype(F32)
        for j in range(1, n):
            acc = acc + s_ref[j].astype(F32)
        o_ref[...] = acc

    return _pcall(
        body, name=name, grid=(a // ta,), in_specs=[pl.BlockSpec((n, ta, b), lambda i: (0, i, 0))],
        out_specs=pl.BlockSpec((ta, b), lambda i: (i, 0)), out_shape=_sds((a, b), F32), compiler_params=_params(1),
    )(s2)


def pair_gather(rcs, *, name):
    n = len(rcs)

    def body(*refs):
        r_refs, o_refs = refs[:n], refs[2 * n:3 * n]
        send_sems, recv_sems = refs[3 * n:]
        x, y, c = _place()

        def copy(i, half):
            return pltpu.make_async_remote_copy(src_ref=r_refs[i], dst_ref=o_refs[i].at[half], send_sem=send_sems.at[i],
                                                recv_sem=recv_sems.at[i], device_id=(x, y, 1 - c), device_id_type=MESH)

        sends = [copy(i, c) for i in range(n)]
        for cp in sends:
            cp.start()
        for i in range(n):
            copy(i, 1 - c).wait_recv()
        for cp in sends:
            cp.wait_send()

    prefilled = [jnp.broadcast_to(r[None], (2,) + r.shape) for r in rcs]
    return _pcall(
        body, name=name, in_specs=[HBM_SPEC] * (2 * n), out_specs=[HBM_SPEC] * n,
        out_shape=[_sds(p.shape, p.dtype) for p in prefilled], input_output_aliases={n + i: i for i in range(n)},
        scratch_shapes=[pltpu.SemaphoreType.DMA((n,)), pltpu.SemaphoreType.DMA((n,))],
    )(*rcs, *prefilled)


def exchange_small(arrs, *, reduce, name):
    n = len(arrs)

    def body(*refs):
        v_refs, o_refs = refs[:n], refs[n:2 * n]
        bufs = refs[2 * n:3 * n] if reduce else o_refs
        send_sems, recv_sems = refs[-2:]
        x, y, c = _place()
        me = 4 * x + 2 * y + c
        for i in range(n):
            bufs[i][me] = v_refs[i][...]

        def peer(k):
            dx, dy, dc = (k >> 2) & 1, (k >> 1) & 1, k & 1
            return (1 - x if dx else x, 1 - y if dy else y, 1 - c if dc else c)

        def copy(i, k, slot):
            return pltpu.make_async_remote_copy(src_ref=v_refs[i], dst_ref=bufs[i].at[slot], send_sem=send_sems.at[7 * i + k - 1],
                                                recv_sem=recv_sems.at[7 * i + k - 1], device_id=peer(k), device_id_type=MESH)

        sends = [copy(i, k, me) for i in range(n) for k in range(1, 8)]
        for cp in sends:
            cp.start()
        for i in range(n):
            for k in range(1, 8):
                px, py, pc = peer(k)
                copy(i, k, 4 * px + 2 * py + pc).wait_recv()
        for cp in sends:
            cp.wait_send()
        if reduce:
            for i in range(n):
                acc = bufs[i][0]
                for d in range(1, 8):
                    acc = acc + bufs[i][d]
                o_refs[i][...] = acc

    vmem = pl.BlockSpec(memory_space=pltpu.VMEM)
    stacked = [(8,) + a.shape for a in arrs]
    return _pcall(
        body, name=name, in_specs=[vmem] * n, out_specs=[vmem] * n,
        out_shape=[_sds(a.shape if reduce else s, F32) for a, s in zip(arrs, stacked)],
        scratch_shapes=([pltpu.VMEM(s, F32) for s in stacked] if reduce else [])
        + [pltpu.SemaphoreType.DMA((7 * n,)), pltpu.SemaphoreType.DMA((7 * n,))],
    )(*arrs)


def _pad_rows(n, mult):
    return -(-n // mult) * mult


def _chip_major(g, b):
    return g.reshape(g.shape[0], 4, b).transpose(1, 0, 2)


def _rope_tables(tp):
    inv_freq = 1.0 / (ROPE_BASE ** (jnp.arange(0, QK_ROPE, 2, dtype=F32) / QK_ROPE))
    ang = jnp.arange(tp, dtype=F32)[:, None] * inv_freq[None, :]
    cos, sin = jnp.cos(ang), jnp.sin(ang)
    one = lambda n: jnp.ones((tp, n), F32)
    zero = lambda n: jnp.zeros((tp, n), F32)
    cq = jnp.concatenate([one(128), cos, cos, one(96)], axis=1)
    s1q = jnp.concatenate([zero(144), sin, zero(96)], axis=1)
    s2q = jnp.concatenate([zero(128), -sin, zero(112)], axis=1)
    ck = jnp.concatenate([cos, cos, zero(96)], axis=1)
    s1k = jnp.concatenate([zero(16), sin, zero(96)], axis=1)
    s2k = jnp.concatenate([-sin, zero(112)], axis=1)
    fwd = (cq * (ATT_SCALE * LOG2E), s1q * (ATT_SCALE * LOG2E), s2q * (ATT_SCALE * LOG2E), ck, s1k, s2k)
    bwd = (cq * ATT_SCALE, -s1q * ATT_SCALE, -s2q * ATT_SCALE, ck, -s1k, -s2k)
    return fwd, bwd


def _pad_w_in(w):
    return jnp.concatenate([w[:, :1952], jnp.zeros((w.shape[0], 96), w.dtype), w[:, 1952:]], axis=1)


def _pad_w_uq(w):
    w = w.reshape(Q_LORA, MLA_HEADS, QK_NOPE + QK_ROPE)
    z = lambda n: jnp.zeros((Q_LORA, MLA_HEADS, n), w.dtype)
    return jnp.concatenate([w[..., :QK_NOPE], z(64), w[..., QK_NOPE:], z(96)], axis=-1).reshape(Q_LORA, MLA_HEADS * Q_PAD)


def _unpad_w_uq(w):
    w = w.reshape(Q_LORA, MLA_HEADS, Q_PAD)
    return jnp.concatenate([w[..., :QK_NOPE], w[..., 128:128 + QK_ROPE]], axis=-1).reshape(Q_LORA, MLA_HEADS * (QK_NOPE + QK_ROPE))


def _pad_w_br_mla(w):
    w = w.reshape(MLA_HEADS, V_HEAD, D_MODEL)
    return jnp.concatenate([jnp.zeros_like(w), w], axis=1).reshape(MLA_HEADS * KV_PAD, D_MODEL)


def _unpad_w_br_mla(w):
    return w.reshape(MLA_HEADS, KV_PAD, D_MODEL)[:, V_HEAD:].reshape(MLA_HEADS * V_HEAD, D_MODEL)


def _riding(hooks, where, l, *args):
    make = hooks.get(where)
    ride = make(l, *args) if make else None
    return ride if ride else (None, lambda results: None)


def _layer_fwd(l, st, xprev, gp, bp, hb, w, tabs, hooks):
    ln_g, ln_b = w["ln_g"], w["ln_b"]
    lg = lambda k: ln_g[l, k][None]
    lb = lambda k: ln_b[l, k][None]
    s = {}
    s["x0"], s["gp0"], s["bp0"], s["hb0"] = xprev, gp, bp, hb
    side, got = _riding(hooks, "ffn1_fwd", l)
    s["g1"], s["u1"], s["a1"], *extras = ffn_up(hb, w["ffn1_w_up"][l], name="ffn_up", side=side)
    got(extras)
    s["xh1"], s["rs1"], s["hb1"] = down_ln(s["a1"], w["ffn1_w_down"][l], xprev, gp, bp, lg(0), lb(0), name="ffn_down_ln")
    s["p"] = mm_rows([(s["hb1"], w["mix_w_in"][l], False, 0)], D_IN_PAD, name="mix_in", tn=1024, out_dtype=BF)
    gq, gkv = w["q_norm_g"][l][None], w["kv_norm_g"][l][None]
    s["cqn"], s["ckvn"], s["q2"], s["kv"], s["kr"] = mla_prep(s["p"], gq, gkv, w["w_uq"][l], w["w_ukv"][l], tabs, name="mla_prep")
    side, got = _riding(hooks, "attn_fwd", l)
    s["o2"], s["lse"], *extras = attn_fwd(s["q2"], s["kv"], s["kr"], name="attn_fwd", side=side)
    got(extras)
    s["ycv"], s["conv"] = conv_fwd(s["p"], w["conv_w"][l], name="conv_fwd")
    s["bc"], s["bm"], s["mg"], s["xh2"], s["rs2"], s["hb2"] = merge_out_ln(
        s["ycv"], s["o2"], s["p"], w["mix_b_gate"][l], w["w_br_conv"][l], w["w_br_mla"][l], w["w_o"][l],
        s["xh1"], lg(0), lb(0), lg(1), lb(1), name="merge_out_ln")
    s["g2"], s["u2"], s["a2"] = ffn_up(s["hb2"], w["ffn2_w_up"][l], name="ffn_up")
    s["xh3"], s["rs3"], s["hb3"] = down_ln(s["a2"], w["ffn2_w_down"][l], s["xh2"], lg(1), lb(1), lg(2), lb(2), name="ffn_down_ln")
    st.append(s)
    return s["xh3"], lg(2), lb(2), s["hb3"]


def _ffn_bwd(dh, w_up, w_down, ln_gain, hb_in, gate, up, act, xh, rs):
    dz, dzb, dgam, dbet = ln_bwd(dh, xh, rs, ln_gain, branch_scale=0.5, name="ln_bwd")
    d_wd = tn_mm(act, dzb, tm=D_FF // 2, name="dw_down", shard=("rows", D_FF // 4))
    dgate, dup = ffn_down_bwd(dzb, w_down, gate, up, name="ffn_down_bwd")
    d_w = tn_mm(hb_in, dgate, tm=512, name="dw_up", shard=("cols", D_FF // 2), slot0=0)
    d_w = tn_mm(hb_in, dup, tm=512, name="dw_up", shard=("cols", D_FF // 2), slot0=2, dst=d_w)
    dh_in = mm_rows([(dgate, w_up, True, 0), (dup, w_up, True, 1)], D_MODEL, name="ffn_up_bwd", tn=512, addend=dz, add_scale=ALPHA)
    return dh_in, d_w, d_wd, dgam, dbet


def _layer_bwd(l, s, dh, w, tabs_bwd, hooks):
    ln_g = w["ln_g"]
    lg = lambda k: ln_g[l, k][None]
    g = {}
    dh, g["ffn2_w_up"], g["ffn2_w_down"], dg2, db2 = _ffn_bwd(
        dh, w["ffn2_w_up"][l], w["ffn2_w_down"][l], lg(2), s["hb2"], s["g2"], s["u2"], s["a2"], s["xh3"], s["rs3"])
    dz, dzb, dg1, db1 = ln_bwd(dh, s["xh2"], s["rs2"], lg(1), branch_scale=1.0, name="ln_bwd")
    g["w_o"] = tn_mm(s["mg"], dzb, tm=1024, name="dw_o", shard=("rows", D_MODEL // 4))
    dbc, dbm, dp, dycv, do2, dl, g["mix_b_gate"] = merge_bwd(
        dzb, w["w_o"][l], s["bc"], s["bm"], s["p"], w["mix_b_gate"][l], w["w_br_conv"][l], w["w_br_mla"][l], s["o2"], name="merge_bwd")
    g["w_br_conv"] = tn_mm(s["ycv"], dbc, tm=512, name="dw_br_conv", shard=("cols", D_MODEL // 4))
    g["w_br_mla"] = _chip_major(_unpad_w_br_mla(tn_mm(s["o2"], dbm, tm=1024, name="dw_br_mla")), D_MODEL // 4)
    side, got = _riding(hooks, "attn_bwd", l, g)
    dq2, dkv, dkr, *extras = attn_bwd(s["q2"], s["kv"], s["kr"], do2, s["lse"], dl, name="attn_bwd", side=side)
    got(extras)
    gq, gkv = w["q_norm_g"][l][None], w["kv_norm_g"][l][None]
    dqb, dp, g["q_norm_g"], g["kv_norm_g"] = mla_prep_bwd(dq2, dkv, dkr, s["p"], gq, gkv, w["w_uq"][l], w["w_ukv"][l], tabs_bwd, dp,
                                                          name="mla_prep_bwd")
    g["w_uq"] = _chip_major(_unpad_w_uq(tn_mm(s["cqn"], dqb, tm=Q_LORA, name="dw_uq")), MLA_HEADS * (QK_NOPE + QK_ROPE) // 4)
    g["w_ukv"] = tn_mm(s["ckvn"], dkv, tm=KV_LORA, name="dw_ukv", shard=("cols", MLA_HEADS * KV_PAD // 4))
    dp, dw0, dw1, dw2 = conv_bwd(dycv, s["p"], s["conv"], w["conv_w"][l], dp, name="conv_bwd")
    g["conv_w"] = jnp.concatenate([dw0, dw1, dw2], axis=0)
    d_in = tn_mm(s["hb1"], dp, tm=512, name="dw_in")
    g["mix_w_in"] = _chip_major(jnp.concatenate([d_in[:, :1952], d_in[:, 2048:]], axis=1), D_IN // 4)
    dh = mm_rows([(dp, w["mix_w_in"][l], True, 0)], D_MODEL, name="mix_in_bwd", tn=512, addend=dz, add_scale=ALPHA)
    dh, g["ffn1_w_up"], g["ffn1_w_down"], dg0, db0 = _ffn_bwd(
        dh, w["ffn1_w_up"][l], w["ffn1_w_down"][l], lg(0), s["hb0"], s["g1"], s["u1"], s["a1"], s["xh1"], s["rs1"])
    g["ln_g"] = jnp.concatenate([dg0, dg1, dg2], axis=0)
    g["ln_b"] = jnp.concatenate([db0, db1, db2], axis=0)
    return dh, g


BIG = ("ffn1_w_up", "ffn1_w_down", "mix_w_in", "w_uq", "w_ukv", "w_br_conv", "w_br_mla", "w_o", "ffn2_w_up", "ffn2_w_down")
BIG_AXIS = (2, 1, 2, 2, 2, 2, 2, 1, 2, 1)
FFN1_MATRICES = ("ffn1_w_up", "ffn1_w_down")
MIXER_MATRICES = ("mix_w_in", "w_uq", "w_ukv", "w_br_conv", "w_br_mla", "w_o")
FFN2_MATRICES = ("ffn2_w_up", "ffn2_w_down")
SMALL_SHARDED = ("meta_tokens", "mix_b_gate", "conv_w", "ln_g", "ln_b")
SMALL_REPLICATED = ("q_norm_g", "kv_norm_g")
WEIGHTS = ("meta_tokens", "ffn1_w_up", "ffn1_w_down", "mix_w_in", "mix_b_gate", "conv_w", "q_norm_g", "w_uq", "kv_norm_g", "w_ukv",
           "w_br_conv", "w_br_mla", "w_o", "ffn2_w_up", "ffn2_w_down", "ln_g", "ln_b")


def _view2d(a):
    return a.reshape(-1, a.shape[-1])


def _local_grads(x_row, target_row, w, hooks=None):
    hooks = hooks or {}
    seq = x_row.shape[0]
    t_real = N_META + seq
    tp = _pad_rows(t_real, TM)
    pad = tp - t_real
    h0 = jnp.concatenate([w["meta_tokens"], x_row, jnp.zeros((pad, D_MODEL), F32)], axis=0)
    target_p = jnp.concatenate([jnp.zeros((N_META, D_MODEL), F32), target_row, jnp.zeros((pad, D_MODEL), F32)], axis=0)
    tabs, tabs_bwd = _rope_tables(tp)
    ones = jnp.ones((1, D_MODEL), F32)
    zeros = jnp.zeros((1, D_MODEL), F32)
    saved = []
    cur = (h0, ones, zeros, h0.astype(BF))
    for l in range(DEPTH):
        cur = _layer_fwd(l, saved, *cur, w, tabs, hooks)
    dh, loss_acc = loss_grad(cur[0], cur[1], cur[2], target_p, seq, name="loss_grad")
    grads = [None] * DEPTH
    for l in reversed(range(DEPTH)):
        dh, grads[l] = _layer_bwd(l, saved[l], dh, w, tabs_bwd, hooks)
        if "layer_bwd_done" in hooks:
            hooks["layer_bwd_done"](l, grads[l])
    return loss_acc, dh[N_META:t_real], dh[:N_META], grads


def kernel(x, meta_tokens, ffn1_w_up, ffn1_w_down, mix_w_in, mix_b_gate, conv_w, q_norm_g, w_uq, kv_norm_g, w_ukv, w_br_conv, w_br_mla, w_o, ffn2_w_up, ffn2_w_down, ln_g, ln_b, loss_target, m_meta_tokens, m_ffn1_w_up, m_ffn1_w_down, m_mix_w_in, m_mix_b_gate, m_conv_w, m_q_norm_g, m_w_uq, m_kv_norm_g, m_w_ukv, m_w_br_conv, m_w_br_mla, m_w_o, m_ffn2_w_up, m_ffn2_w_down, m_ln_g, m_ln_b, v_meta_tokens, v_ffn1_w_up, v_ffn1_w_down, v_mix_w_in, v_mix_b_gate, v_conv_w, v_q_norm_g, v_w_uq, v_kv_norm_g, v_w_ukv, v_w_br_conv, v_w_br_mla, v_w_o, v_ffn2_w_up, v_ffn2_w_down, v_ln_g, v_ln_b):
    local = dict(meta_tokens=meta_tokens, ffn1_w_up=ffn1_w_up, ffn1_w_down=ffn1_w_down, mix_w_in=mix_w_in, mix_b_gate=mix_b_gate,
                 conv_w=conv_w, q_norm_g=q_norm_g, w_uq=w_uq, kv_norm_g=kv_norm_g, w_ukv=w_ukv, w_br_conv=w_br_conv,
                 w_br_mla=w_br_mla, w_o=w_o, ffn2_w_up=ffn2_w_up, ffn2_w_down=ffn2_w_down, ln_g=ln_g, ln_b=ln_b)
    mom_m = dict(zip(WEIGHTS, (m_meta_tokens, m_ffn1_w_up, m_ffn1_w_down, m_mix_w_in, m_mix_b_gate, m_conv_w, m_q_norm_g, m_w_uq,
                               m_kv_norm_g, m_w_ukv, m_w_br_conv, m_w_br_mla, m_w_o, m_ffn2_w_up, m_ffn2_w_down, m_ln_g, m_ln_b)))
    mom_v = dict(zip(WEIGHTS, (v_meta_tokens, v_ffn1_w_up, v_ffn1_w_down, v_mix_w_in, v_mix_b_gate, v_conv_w, v_q_norm_g, v_w_uq,
                               v_kv_norm_g, v_w_ukv, v_w_br_conv, v_w_br_mla, v_w_o, v_ffn2_w_up, v_ffn2_w_down, v_ln_g, v_ln_b)))
    xi, yi, ci = _place()
    chip = 2 * xi + yi

    shards = {n: local[n].astype(BF) for n in BIG}
    axis = dict(zip(BIG, BIG_AXIS))
    pad_layout = {"mix_w_in": _pad_w_in, "w_uq": _pad_w_uq, "w_br_mla": _pad_w_br_mla}
    w = {n: [None] * DEPTH for n in BIG}

    def fetch(keys):
        def install(gathered):
            for (n, l), g in zip(keys, gathered):
                full = jnp.concatenate([g[j] for j in range(4)], axis=axis[n] - 1)
                w[n][l] = pad_layout[n](full) if n in pad_layout else full
        return gather_side([(shards[n], l) for n, l in keys]), install

    first, install_first = fetch([(n, 0) for n in FFN1_MATRICES])
    install_first(exchange_alone(first, name="gather_weights"))
    fetch_under = {("ffn1_fwd", 0): [(n, 0) for n in MIXER_MATRICES],
                   ("attn_fwd", 0): [(n, 0) for n in FFN2_MATRICES] + [(n, 1) for n in BIG]}
    hooks = {where: functools.partial(lambda l, where: fetch(fetch_under[where, l]) if (where, l) in fetch_under else None, where=where)
             for where in ("ffn1_fwd", "attn_fwd")}
    stacked = exchange_small([_view2d(local[n]) for n in SMALL_SHARDED], reduce=False, name="gather_small")
    for n, st in zip(SMALL_SHARDED, stacked):
        full = jnp.concatenate([st[2 * j] for j in range(4)], axis=-1)
        w[n] = full.reshape(local[n].shape[:-1] + (full.shape[-1],))
    for n in SMALL_REPLICATED:
        w[n] = local[n]

    c_idx = jnp.reshape(ci, (1,)).astype(jnp.int32)
    done, from_chips = {}, {}

    def send(keys, grad_of):
        glist = [grad_of[k] for k in keys]
        from_sibling = pair_exchange(glist, name="rs_pair_exchange")
        sums = [pair_add(a, s, c_idx, name="rs_pair_add") for a, s in zip(glist, from_sibling)]
        return scatter_side(sums), lambda results: from_chips.update(zip(keys, results))

    early = FFN2_MATRICES + ("w_o", "w_br_conv", "w_br_mla")
    hooks["layer_bwd_done"] = lambda l, g: done.update({(n, l): g[n] for n in BIG})
    hooks["attn_bwd"] = lambda l, g: send([(n, 1) for n in BIG] + [(n, 0) for n in early],
                                          {**done, **{(n, 0): g[n] for n in early}}) if l == 0 else None

    loss_acc, grad_x, d_meta, grads = _local_grads(x[0], loss_target[0], w, hooks)
    grad_x = grad_x[None]
    last, keep_last = send([(n, 0) for n in BIG if n not in early], done)
    keep_last(exchange_alone(last, name="rs_chip_scatter"))
    keys = [(n, l) for n in BIG for l in range(DEPTH)]
    reduced = pair_gather([sum_chunks(from_chips[k], name="rs_sum") for k in keys], name="rs_pair_gather")
    reduced = {k: r.reshape(local[k[0]].shape[1:]) for k, r in zip(keys, reduced)}
    gshard = {n: jnp.stack([reduced[n, l] for l in range(DEPTH)]) for n in BIG}

    small_names = SMALL_SHARDED + SMALL_REPLICATED
    gsmall = {n: jnp.concatenate([grads[l][n] for l in range(DEPTH)], axis=0) for n in small_names if n != "meta_tokens"}
    gsmall["meta_tokens"] = d_meta
    small_red = exchange_small([gsmall[n] for n in small_names] + [loss_acc], reduce=True, name="reduce_small")
    loss = small_red[-1][0, 0]
    for n, full in zip(small_names, small_red[:-1]):
        if n in SMALL_SHARDED:
            sh = local[n].shape[-1]
            full = lax.dynamic_slice_in_dim(full, chip * sh, sh, axis=1)
        gshard[n] = full.reshape(local[n].shape)

    delta, new_m, new_v = {}, {}, {}
    for n in WEIGHTS:
        shape = local[n].shape
        d, nm, nv = adamw(_view2d(local[n]), _view2d(gshard[n]), _view2d(mom_m[n]), _view2d(mom_v[n]), name="adamw")
        delta[n], new_m[n], new_v[n] = d.reshape(shape), nm.reshape(shape), nv.reshape(shape)
    return (loss, grad_x, *[gshard[n] for n in WEIGHTS], *[delta[n] for n in WEIGHTS], *[new_m[n] for n in WEIGHTS],
            *[new_v[n] for n in WEIGHTS])
```

```python
import functools

import jax
import jax.numpy as jnp
from jax import lax
from jax.experimental import pallas as pl
from jax.experimental.pallas import tpu as pltpu

F32 = jnp.float32
BF = jnp.bfloat16
MESH = pl.DeviceIdType.MESH

D_MODEL = 1024
DEPTH = 2
N_META = 16
D_CONV = 512
MLA_HEADS = 8
QK_NOPE = 64
QK_ROPE = 32
V_HEAD = 64
Q_LORA = 256
KV_LORA = 128
ROPE_BASE = 10000.0
NEG_INF = -1e30
D_FF = 2816
ALPHA = (2 * DEPTH) ** 0.25
LN_EPS = 1e-5
RMS_EPS = 1e-6
ATT_SCALE = (QK_NOPE + QK_ROPE) ** -0.5
LOG2E = 1.4426950408889634
LN2 = 0.6931471805599453
D_IN = 4000
D_IN_PAD = 4096
Q_PAD = 256
KV_PAD = 128

ADAM_LR = 0.001
ADAM_B1 = 0.9
ADAM_B2 = 0.999
ADAM_EPS = 1e-08
ADAM_WD = 0.01
ADAM_STEP = 10

TM = 768
TMH = 384
LANES = 128
COMM_COLS = 512
COMM_ROW_BLOCK = 1472
VMEM_LIMIT_BYTES = 50 * 1024 * 1024

NT = (((1,), (1,)), ((), ()))
TN = (((0,), (0,)), ((), ()))


def _pcall(body, **kw):
    return pl.pallas_call(body, **kw)


def _params(n_axes):
    return pltpu.CompilerParams(dimension_semantics=("arbitrary",) * n_axes, vmem_limit_bytes=VMEM_LIMIT_BYTES)


def _sds(shape, dtype):
    return jax.ShapeDtypeStruct(shape, dtype)


def mm_rows(pairs, n_out, *, name, tn=None, addend=None, add_scale=1.0, out_dtype=F32):
    tp = pairs[0][0].shape[0]
    tn = tn or n_out
    in_specs, args = [], []
    for a, b, nt, kb in pairs:
        k = a.shape[1]
        in_specs.append(pl.BlockSpec((TM, k), lambda i, j: (i, 0)))
        if nt and b.ndim == 3:
            in_specs.append(pl.BlockSpec((2, tn, k // 2), functools.partial(lambda i, j, kb: (kb, j, 0), kb=kb)))
        elif nt:
            in_specs.append(pl.BlockSpec((tn, k), functools.partial(lambda i, j, kb: (j, kb), kb=kb)))
        else:
            in_specs.append(pl.BlockSpec((k, tn), lambda i, j: (0, j)))
        args += [a, b]
    if addend is not None:
        in_specs.append(pl.BlockSpec((TM, tn), lambda i, j: (i, j)))
        args.append(addend)
    n_pairs = len(pairs)
    nts = [p[2] for p in pairs]

    def body(*refs):
        o_ref = refs[-1]
        acc = None
        for p in range(n_pairs):
            a = refs[2 * p][...].astype(BF)
            b = refs[2 * p + 1][...]
            if b.ndim == 3:
                b = jnp.concatenate([b[0], b[1]], axis=1)
            d = lax.dot_general(a, b, NT if nts[p] else (((1,), (0,)), ((), ())), preferred_element_type=F32)
            acc = d if acc is None else acc + d
        if addend is not None:
            acc = acc + add_scale * refs[2 * n_pairs][...]
        o_ref[...] = acc.astype(o_ref.dtype)

    return _pcall(
        body, name=name, grid=(tp // TM, n_out // tn), in_specs=in_specs,
        out_specs=pl.BlockSpec((TM, tn), lambda i, j: (i, j)), out_shape=_sds((tp, n_out), out_dtype),
        compiler_params=_params(2),
    )(*args)


def tn_mm(a, b, *, tm, name, out_dtype=BF, shard=None, slot0=0, dst=None):
    tp, m = a.shape
    n = b.shape[1]
    nk = tp // TM
    if shard is None:
        pieces, out_block, out_index, out_full = 1, (tm, n), (lambda i, k: (i, 0)), (m, n)
    elif shard[0] == "cols":
        pieces = n // shard[1]
        out_block, out_full = (pieces, tm, shard[1]), (4, m, shard[1])
        out_index = lambda i, k: (slot0 // pieces, i, 0)
    else:
        pieces = tm // shard[1]
        out_block, out_full = (pieces, shard[1], n), (4, m // 4, n)
        out_index = lambda i, k: (i, 0, 0)

    def body(a_ref, b_ref, *rest):
        o_ref, acc_ref = rest[-2], rest[-1]
        k = pl.program_id(1)

        @pl.when(k == 0)
        def _():
            acc_ref[...] = jnp.zeros_like(acc_ref)

        acc_ref[...] += lax.dot_general(a_ref[...].astype(BF), b_ref[...].astype(BF), TN, preferred_element_type=F32)

        @pl.when(k == nk - 1)
        def _():
            if shard is None:
                o_ref[...] = acc_ref[...].astype(o_ref.dtype)
            elif shard[0] == "cols":
                for j in range(pieces):
                    o_ref[j] = acc_ref[:, j * shard[1]:(j + 1) * shard[1]].astype(o_ref.dtype)
            else:
                for j in range(pieces):
                    o_ref[j] = acc_ref[j * shard[1]:(j + 1) * shard[1], :].astype(o_ref.dtype)

    in_specs = [pl.BlockSpec((TM, tm), lambda i, k: (k, i)), pl.BlockSpec((TM, n), lambda i, k: (k, 0))]
    args = [a, b]
    aliases = {}
    if dst is not None:
        in_specs.append(pl.BlockSpec(memory_space=pl.ANY))
        args.append(dst)
        aliases = {2: 0}
    return _pcall(
        body, name=name, grid=(m // tm, nk), in_specs=in_specs, out_specs=pl.BlockSpec(out_block, out_index),
        out_shape=_sds(out_full, out_dtype), input_output_aliases=aliases,
        scratch_shapes=[pltpu.VMEM((tm, n), F32)], compiler_params=_params(2),
    )(*args)


def _ln_store(z, g_ref, b_ref, xh_ref, rs_ref, hb_ref):
    mu = jnp.mean(z, axis=-1, keepdims=True)
    zc = z - mu
    var = jnp.mean(zc * zc, axis=-1, keepdims=True)
    rstd = lax.rsqrt(var + LN_EPS)
    xh = zc * rstd
    xh_ref[...] = xh
    rs_ref[...] = rstd
    hb_ref[...] = (xh * g_ref[...] + b_ref[...]).astype(BF)


def _ln_out(tp, tm=TM):
    specs = [pl.BlockSpec((tm, D_MODEL), lambda i: (i, 0)), pl.BlockSpec((tm, 1), lambda i: (i, 0)),
             pl.BlockSpec((tm, D_MODEL), lambda i: (i, 0))]
    shapes = [_sds((tp, D_MODEL), F32), _sds((tp, 1), F32), _sds((tp, D_MODEL), BF)]
    return specs, shapes


def _row_vec(n):
    return pl.BlockSpec((1, n), lambda i: (0, 0))


def ffn_up(hb, wup, *, name, side=None):
    tp = hb.shape[0]
    tn = D_FF // 2
    nj = D_FF // tn

    def body(in_refs, out_refs, scratch):
        h_ref, wg_ref, wu_ref = in_refs
        g_ref, u_ref, a_ref = out_refs
        h = h_ref[...]
        g = jnp.dot(h, wg_ref[0], preferred_element_type=F32)
        u = jnp.dot(h, wu_ref[0], preferred_element_type=F32)
        g_ref[...] = g.astype(BF)
        u_ref[...] = u.astype(BF)
        a_ref[...] = (g * jax.nn.sigmoid(g) * u).astype(BF)

    blk = pl.BlockSpec((TM, tn), lambda i, j: (i, j))
    return _side_call(
        body, side, name=name, grid=(tp // TM, nj),
        in_specs=[pl.BlockSpec((TM, D_MODEL), lambda i, j: (i, 0)), pl.BlockSpec((1, D_MODEL, tn), lambda i, j: (j, 0, 0)),
                  pl.BlockSpec((1, D_MODEL, tn), lambda i, j: (j + nj, 0, 0))],
        out_specs=[blk, blk, blk], out_shape=[_sds((tp, D_FF), BF)] * 3, scratch_shapes=[], args=(hb, wup, wup),
    )


def down_ln(a, wd, xprev, gp, bp, g, b, *, name):
    tp = a.shape[0]

    def body(a_ref, wd_ref, xp_ref, gp_ref, bp_ref, g_ref, b_ref, xh_ref, rs_ref, hb_ref):
        wd = jnp.concatenate([wd_ref[j] for j in range(4)], axis=0)
        f = jnp.dot(a_ref[...], wd, preferred_element_type=F32)
        hprev = xp_ref[...] * gp_ref[...] + bp_ref[...]
        _ln_store(ALPHA * hprev + 0.5 * f, g_ref, b_ref, xh_ref, rs_ref, hb_ref)

    out_specs, out_shape = _ln_out(tp)
    return _pcall(
        body, name=name, grid=(tp // TM,),
        in_specs=[pl.BlockSpec((TM, D_FF), lambda i: (i, 0)), pl.BlockSpec((4, D_FF // 4, D_MODEL), lambda i: (0, 0, 0)),
                  pl.BlockSpec((TM, D_MODEL), lambda i: (i, 0))] + [_row_vec(D_MODEL)] * 4,
        out_specs=out_specs, out_shape=out_shape, compiler_params=_params(1),
    )(a, wd, xprev, gp, bp, g, b)


def _rope(x, c, s1, s2, reps):
    n = x.shape[1]
    if reps > 1:
        c, s1, s2 = (jnp.tile(t, (1, reps)) for t in (c, s1, s2))
    return x * c + pltpu.roll(x, 16, 1) * s1 + pltpu.roll(x, n - 16, 1) * s2


def _rms(x, g):
    r = lax.rsqrt(jnp.mean(x * x, axis=-1, keepdims=True) + RMS_EPS)
    return x * r * g, r


def mla_prep(p, gq, gkv, wuq_p, wukv, tabs, *, name):
    tp = p.shape[0]
    nh = MLA_HEADS

    def body(cq_ref, ckv_ref, kr_ref, gq_ref, gkv_ref, wuq_ref, wukv_ref, cq_t, s1q_t, s2q_t, ck_t, s1k_t, s2k_t,
             cqn_ref, ckvn_ref, q2_ref, kv_ref, krr_ref):
        cqn, _ = _rms(cq_ref[...].astype(F32), gq_ref[...])
        ckvn, _ = _rms(ckv_ref[...].astype(F32), gkv_ref[...])
        cqn = cqn.astype(BF)
        ckvn = ckvn.astype(BF)
        cqn_ref[...] = cqn
        ckvn_ref[...] = ckvn
        q = jnp.dot(cqn, wuq_ref[...], preferred_element_type=F32)
        q2_ref[...] = _rope(q, cq_t[...], s1q_t[...], s2q_t[...], nh).astype(BF)
        kv_ref[...] = jnp.dot(ckvn, wukv_ref[...], preferred_element_type=F32).astype(BF)
        krr_ref[...] = _rope(kr_ref[...].astype(F32), ck_t[...], s1k_t[...], s2k_t[...], 1).astype(BF)

    def rows(n, col=0):
        return pl.BlockSpec((TMH, n), functools.partial(lambda i, col: (i, col), col=col))

    return _pcall(
        body, name=name, grid=(tp // TMH,),
        in_specs=[rows(Q_LORA, 1536 // Q_LORA), rows(KV_LORA, 1792 // KV_LORA), rows(LANES, 1920 // LANES),
                  _row_vec(Q_LORA), _row_vec(KV_LORA),
                  pl.BlockSpec((Q_LORA, nh * Q_PAD), lambda i: (0, 0)), pl.BlockSpec((KV_LORA, nh * KV_PAD), lambda i: (0, 0)),
                  rows(Q_PAD), rows(Q_PAD), rows(Q_PAD), rows(LANES), rows(LANES), rows(LANES)],
        out_specs=[rows(Q_LORA), rows(KV_LORA), rows(nh * Q_PAD), rows(nh * KV_PAD), rows(LANES)],
        out_shape=[_sds((tp, Q_LORA), BF), _sds((tp, KV_LORA), BF), _sds((tp, nh * Q_PAD), BF),
                   _sds((tp, nh * KV_PAD), BF), _sds((tp, LANES), BF)],
        compiler_params=_params(1),
    )(p, p, p, gq, gkv, wuq_p, wukv, *tabs)


def _causal_mask(s):
    qpos = lax.broadcasted_iota(jnp.int32, (TM, TM), 0)
    kpos = lax.broadcasted_iota(jnp.int32, (TM, TM), 1)
    return jnp.where(kpos <= qpos, s, NEG_INF)


def _key_rows(k):
    return pl.ds(pl.multiple_of(k * TM, TM), TM)


def _pipelined_key_blocks(n, prefetch, process):
    prefetch(0, 0)

    def pair(j, carry):
        prefetch(2 * j + 1, 1)
        process(2 * j, 0, False)
        prefetch(2 * j + 2, 0)
        process(2 * j + 1, 1, False)
        return carry

    lax.fori_loop(0, n // 2, pair, 0)

    @pl.when(n % 2 == 1)
    def _():
        prefetch(n, 1)
        process(n - 1, 0, False)
        process(n, 1, True)

    @pl.when(n % 2 == 0)
    def _():
        process(n, 0, True)


def _side_call(body_main, side, *, name, grid, in_specs, out_specs, out_shape, scratch_shapes, args):
    n_in, n_out, n_scr = len(in_specs), len(out_specs), len(scratch_shapes)
    s_in, s_pre, n_sems, program = side if side is not None else ((), (), 0, None)
    a, b = len(s_in), len(s_pre)

    def body(*refs):
        in_refs = refs[:n_in]
        out_refs = refs[n_in + a + b:n_in + a + b + n_out]
        scr = refs[n_in + a + 2 * b + n_out:n_in + a + 2 * b + n_out + n_scr]
        if side is not None:
            side_in = refs[n_in:n_in + a]
            side_out = refs[n_in + a + b + n_out:n_in + a + 2 * b + n_out]
            start, finish = program(side_in, side_out, refs[-2], refs[-1])

            @pl.when((pl.program_id(0) == 0) & (pl.program_id(1) == 0))
            def _():
                start()

        body_main(in_refs, out_refs, scr)
        if side is not None:
            @pl.when((pl.program_id(0) == grid[0] - 1) & (pl.program_id(1) == grid[1] - 1))
            def _():
                finish()

    sems = [pltpu.SemaphoreType.DMA((n_sems,))] * 2 if side is not None else []
    return _pcall(
        body, name=name, grid=grid, in_specs=list(in_specs) + [HBM_SPEC] * (a + b), out_specs=list(out_specs) + [HBM_SPEC] * b,
        out_shape=list(out_shape) + [_sds(p.shape, p.dtype) for p in s_pre],
        input_output_aliases={n_in + a + i: n_out + i for i in range(b)},
        scratch_shapes=list(scratch_shapes) + sems, compiler_params=_params(2),
    )(*args, *s_in, *s_pre)


def attn_fwd(q2, kv, kr, *, name, side=None):
    tp = q2.shape[0]
    nh = MLA_HEADS
    nb = tp // TM
    rep = TM // LANES

    def body(in_refs, out_refs, scratch):
        q_ref, kv_ref, kr_ref = in_refs
        o_ref, lse_ref = out_refs
        m_ref, l_ref, acc_ref, s0_ref, s1_ref, p_ref, alpha_ref = scratch
        qi = pl.program_id(1)
        s_refs = (s0_ref, s1_ref)
        m_ref[...] = jnp.full_like(m_ref, NEG_INF)
        l_ref[...] = jnp.zeros_like(l_ref)
        acc_ref[...] = jnp.zeros_like(acc_ref)

        def prefetch(k, slot):
            k2 = jnp.concatenate([kv_ref[_key_rows(k), :], kr_ref[_key_rows(k), :]], axis=1)
            s_refs[slot][...] = lax.dot_general(q_ref[...], k2, NT, preferred_element_type=F32)

        def process(k, slot, diagonal):
            for r in range(TM // LANES):
                rows = slice(r * LANES, (r + 1) * LANES)
                s = s_refs[slot][rows, :]
                if diagonal:
                    qpos = r * LANES + lax.broadcasted_iota(jnp.int32, (LANES, TM), 0)
                    s = jnp.where(lax.broadcasted_iota(jnp.int32, (LANES, TM), 1) <= qpos, s, NEG_INF)
                m_prev = m_ref[rows, :]
                m_new = jnp.maximum(m_prev, jnp.max(s, axis=1, keepdims=True))
                alpha = jnp.exp2(m_prev - m_new)
                p = jnp.exp2(s - jnp.tile(m_new, (1, rep)))
                lane_sums = p[:, 0:LANES]
                for t in range(1, rep):
                    lane_sums = lane_sums + p[:, t * LANES:(t + 1) * LANES]
                l_ref[rows, :] = alpha * l_ref[rows, :] + lane_sums
                p_ref[rows, :] = p.astype(BF)
                alpha_ref[rows, :] = alpha
                m_ref[rows, :] = m_new
            acc_ref[...] = alpha_ref[...] * acc_ref[...] + jnp.dot(p_ref[...], kv_ref[_key_rows(k), :], preferred_element_type=F32)

        _pipelined_key_blocks(qi, prefetch, process)
        l = jnp.sum(l_ref[...], axis=1, keepdims=True)
        o_ref[...] = (acc_ref[...] / l).astype(BF)
        lse_ref[...] = m_ref[...] + jnp.log2(l)

    return _side_call(
        body, side, name=name, grid=(nh, nb),
        in_specs=[pl.BlockSpec((TM, Q_PAD), lambda h, qi: (qi, h)), pl.BlockSpec((tp, KV_PAD), lambda h, qi: (0, h)),
                  pl.BlockSpec((tp, LANES), lambda h, qi: (0, 0))],
        out_specs=[pl.BlockSpec((TM, KV_PAD), lambda h, qi: (qi, h)), pl.BlockSpec((TM, LANES), lambda h, qi: (qi, h))],
        out_shape=[_sds((tp, nh * KV_PAD), BF), _sds((tp, nh * LANES), F32)],
        scratch_shapes=[pltpu.VMEM((TM, LANES), F32)] * 3 + [pltpu.VMEM((TM, TM), F32)] * 2
        + [pltpu.VMEM((TM, TM), BF), pltpu.VMEM((TM, LANES), F32)], args=(q2, kv, kr),
    )


def conv_fwd(p, w, *, name):
    tp = p.shape[0]

    def body(b_ref, c_ref, h_ref, w_ref, y_ref, cv_ref, ebuf):
        i = pl.program_id(0)

        @pl.when(i == 0)
        def _():
            ebuf[0:8, :] = jnp.zeros((8, D_CONV), F32)

        e = c_ref[...].astype(F32) * h_ref[...].astype(F32)
        ebuf[8:8 + TM, :] = e
        w_all = w_ref[...]
        conv = w_all[0:1] * ebuf[pl.ds(6, TM), :] + w_all[1:2] * ebuf[pl.ds(7, TM), :] + w_all[2:3] * e
        cv_ref[...] = conv.astype(BF)
        y_ref[...] = (b_ref[...].astype(F32) * conv).astype(BF)
        ebuf[0:8, :] = ebuf[TM:TM + 8, :]

    def col(j):
        return pl.BlockSpec((TM, D_CONV), functools.partial(lambda i, j: (i, j), j=j))

    return _pcall(
        body, name=name, grid=(tp // TM,),
        in_specs=[col(0), col(1), col(2), pl.BlockSpec((3, D_CONV), lambda i: (0, 0))],
        out_specs=[col(0), col(0)], out_shape=[_sds((tp, D_CONV), BF)] * 2,
        scratch_shapes=[pltpu.VMEM((TM + 8, D_CONV), F32)], compiler_params=_params(1),
    )(p, p, p, w)


def merge_out_ln(ycv, o2, p, bg, wbc, wbm_p, wo, xprev, gp, bp, g, b, *, name):
    tp = ycv.shape[0]

    def body(y_ref, o_ref, gc_ref, gm_ref, bg_ref, wbc_ref, wbm_ref, wo_ref, xp_ref, gp_ref, bp_ref, g_ref, b_ref,
             bc_ref, bm_ref, mg_ref, xh_ref, rs_ref, hb_ref):
        bc = jnp.dot(y_ref[...], wbc_ref[...], preferred_element_type=F32)
        bm = jnp.dot(o_ref[...], wbm_ref[...], preferred_element_type=F32)
        bgv = bg_ref[...]
        mg = (jax.nn.sigmoid(gc_ref[...].astype(F32) + bgv[0:1]) * bc
              + jax.nn.sigmoid(gm_ref[...].astype(F32) + bgv[1:2]) * bm)
        mgb = mg.astype(BF)
        bc_ref[...] = bc.astype(BF)
        bm_ref[...] = bm.astype(BF)
        mg_ref[...] = mgb
        mix = jnp.dot(mgb, wo_ref[...], preferred_element_type=F32)
        hprev = xp_ref[...] * gp_ref[...] + bp_ref[...]
        _ln_store(ALPHA * hprev + mix, g_ref, b_ref, xh_ref, rs_ref, hb_ref)

    def rows(n, col=0):
        return pl.BlockSpec((TMH, n), functools.partial(lambda i, col: (i, col), col=col))

    def whole(r, c):
        return pl.BlockSpec((r, c), lambda i: (0, 0))

    ln_specs, ln_shapes = _ln_out(tp, TMH)
    return _pcall(
        body, name=name, grid=(tp // TMH,),
        in_specs=[rows(D_CONV), rows(MLA_HEADS * KV_PAD), rows(D_MODEL, 2), rows(D_MODEL, 3), whole(2, D_MODEL),
                  whole(D_CONV, D_MODEL), whole(MLA_HEADS * KV_PAD, D_MODEL), whole(D_MODEL, D_MODEL), rows(D_MODEL)]
        + [_row_vec(D_MODEL)] * 4,
        out_specs=[rows(D_MODEL)] * 3 + ln_specs, out_shape=[_sds((tp, D_MODEL), BF)] * 3 + ln_shapes,
        compiler_params=_params(1),
    )(ycv, o2, p, p, bg, wbc, wbm_p, wo, xprev, gp, bp, g, b)


def loss_grad(xh, g, b, target_p, n_real, *, name):
    tp = xh.shape[0]

    def body(x_ref, g_ref, b_ref, t_ref, dy_ref, loss_ref):
        i = pl.program_id(0)

        @pl.when(i == 0)
        def _():
            loss_ref[...] = jnp.zeros_like(loss_ref)

        row = i * TM + lax.broadcasted_iota(jnp.int32, (TM, 1), 0)
        real = (row >= N_META) & (row < N_META + n_real)
        diff = jnp.where(real, x_ref[...] * g_ref[...] + b_ref[...] - t_ref[...], 0.0)
        dy_ref[...] = diff * (1.0 / D_MODEL)
        loss_ref[...] += 0.5 / D_MODEL * jnp.sum(diff * diff)

    return _pcall(
        body, name=name, grid=(tp // TM,),
        in_specs=[pl.BlockSpec((TM, D_MODEL), lambda i: (i, 0)), _row_vec(D_MODEL), _row_vec(D_MODEL),
                  pl.BlockSpec((TM, D_MODEL), lambda i: (i, 0))],
        out_specs=[pl.BlockSpec((TM, D_MODEL), lambda i: (i, 0)), pl.BlockSpec((8, LANES), lambda i: (0, 0))],
        out_shape=[_sds((tp, D_MODEL), F32), _sds((8, LANES), F32)], compiler_params=_params(1),
    )(xh, g, b, target_p)


def ln_bwd(dh, xh, rstd, g, *, branch_scale, name):
    tp = dh.shape[0]

    def body(dh_ref, xh_ref, rs_ref, g_ref, dz_ref, dzb_ref, dg_ref, db_ref):
        i = pl.program_id(0)

        @pl.when(i == 0)
        def _():
            dg_ref[...] = jnp.zeros_like(dg_ref)
            db_ref[...] = jnp.zeros_like(db_ref)

        dy = dh_ref[...]
        xhat = xh_ref[...]
        dg_ref[...] += jnp.sum(dy * xhat, axis=0, keepdims=True)
        db_ref[...] += jnp.sum(dy, axis=0, keepdims=True)
        dxh = dy * g_ref[...]
        m1 = jnp.mean(dxh, axis=-1, keepdims=True)
        m2 = jnp.mean(dxh * xhat, axis=-1, keepdims=True)
        dz = rs_ref[...] * (dxh - m1 - xhat * m2)
        dz_ref[...] = dz
        dzb_ref[...] = (branch_scale * dz).astype(BF)

    rows = pl.BlockSpec((TM, D_MODEL), lambda i: (i, 0))
    return _pcall(
        body, name=name, grid=(tp // TM,),
        in_specs=[rows, rows, pl.BlockSpec((TM, 1), lambda i: (i, 0)), _row_vec(D_MODEL)],
        out_specs=[rows, rows, _row_vec(D_MODEL), _row_vec(D_MODEL)],
        out_shape=[_sds((tp, D_MODEL), F32), _sds((tp, D_MODEL), BF), _sds((1, D_MODEL), F32), _sds((1, D_MODEL), F32)],
        compiler_params=_params(1),
    )(dh, xh, rstd, g)


def ffn_down_bwd(dzb, wd, gate, up, *, name):
    tp = dzb.shape[0]
    tn = D_FF // 2

    def body(dz_ref, wd_ref, g_ref, u_ref, dg_ref, du_ref):
        wd = jnp.concatenate([wd_ref[0], wd_ref[1]], axis=0)
        da = lax.dot_general(dz_ref[...], wd, NT, preferred_element_type=F32)
        g = g_ref[...].astype(F32)
        u = u_ref[...].astype(F32)
        sg = jax.nn.sigmoid(g)
        dg_ref[...] = (da * u * sg * (1.0 + g * (1.0 - sg))).astype(BF)
        du_ref[...] = (da * g * sg).astype(BF)

    blk = pl.BlockSpec((TM, tn), lambda i, j: (i, j))
    return _pcall(
        body, name=name, grid=(tp // TM, D_FF // tn),
        in_specs=[pl.BlockSpec((TM, D_MODEL), lambda i, j: (i, 0)), pl.BlockSpec((2, tn // 2, D_MODEL), lambda i, j: (j, 0, 0)), blk, blk],
        out_specs=[blk, blk], out_shape=[_sds((tp, D_FF), BF)] * 2, compiler_params=_params(2),
    )(dzb, wd, gate, up)


def merge_bwd(dzb, wo, bc, bm, p, bg, wbc, wbm_p, o2, *, name):
    tp = dzb.shape[0]
    nh = MLA_HEADS

    def body(dz_ref, wo_ref, bc_ref, bm_ref, gc_ref, gm_ref, bg_ref, wbc_ref, wbm_ref, o_ref,
             dbc_ref, dbm_ref, dgg_ref, dy_ref, do_ref, dl_ref, dbg_ref):
        i = pl.program_id(0)

        @pl.when(i == 0)
        def _():
            dbg_ref[...] = jnp.zeros_like(dbg_ref)

        dmg = lax.dot_general(dz_ref[...], wo_ref[...], NT, preferred_element_type=F32)
        bgv = bg_ref[...]
        sc = jax.nn.sigmoid(gc_ref[...].astype(F32) + bgv[0:1])
        sm = jax.nn.sigmoid(gm_ref[...].astype(F32) + bgv[1:2])
        dbc = (dmg * sc).astype(BF)
        dbm = (dmg * sm).astype(BF)
        dgc = dmg * bc_ref[...].astype(F32) * sc * (1.0 - sc)
        dgm = dmg * bm_ref[...].astype(F32) * sm * (1.0 - sm)
        dbc_ref[...] = dbc
        dbm_ref[...] = dbm
        dgg_ref[...] = jnp.concatenate([dgc, dgm], axis=1).astype(BF)
        dbg_ref[...] += jnp.concatenate([jnp.sum(dgc, axis=0, keepdims=True), jnp.sum(dgm, axis=0, keepdims=True)], axis=0)
        dy_ref[...] = lax.dot_general(dbc, wbc_ref[...], NT, preferred_element_type=F32)
        do = lax.dot_general(dbm, wbm_ref[...], NT, preferred_element_type=F32)
        do_ref[...] = do.astype(BF)
        prod = do * o_ref[...].astype(F32)
        parts = []
        for h in range(nh):
            d = jnp.sum(prod[:, h * KV_PAD:(h + 1) * KV_PAD], axis=1, keepdims=True)
            parts.append(jnp.broadcast_to(d, (TMH, LANES)))
        dl_ref[...] = jnp.concatenate(parts, axis=1)

    def rows(n, col=0):
        return pl.BlockSpec((TMH, n), functools.partial(lambda i, col: (i, col), col=col))

    def whole(r, c):
        return pl.BlockSpec((r, c), lambda i: (0, 0))

    return _pcall(
        body, name=name, grid=(tp // TMH,),
        in_specs=[rows(D_MODEL), whole(D_MODEL, D_MODEL), rows(D_MODEL), rows(D_MODEL), rows(D_MODEL, 2), rows(D_MODEL, 3),
                  whole(2, D_MODEL), whole(D_CONV, D_MODEL), whole(nh * KV_PAD, D_MODEL), rows(nh * KV_PAD)],
        out_specs=[rows(D_MODEL), rows(D_MODEL), rows(2 * D_MODEL, 1), rows(D_CONV), rows(nh * KV_PAD), rows(nh * LANES),
                   whole(2, D_MODEL)],
        out_shape=[_sds((tp, D_MODEL), BF), _sds((tp, D_MODEL), BF), _sds((tp, D_IN_PAD), BF), _sds((tp, D_CONV), F32),
                   _sds((tp, nh * KV_PAD), BF), _sds((tp, nh * LANES), F32), _sds((2, D_MODEL), F32)],
        compiler_params=_params(1),
    )(dzb, wo, bc, bm, p, p, bg, wbc, wbm_p, o2)


def attn_bwd(q2, kv, kr, do2, lse, dl, *, name, side=None):
    tp = q2.shape[0]
    nh = MLA_HEADS
    nb = tp // TM
    rep = TM // LANES

    def body(in_refs, out_refs, scratch):
        q_ref, kv_ref, kr_ref, do_ref, lse_ref, dl_ref = in_refs
        dq_ref, dkv_ref, dkr_ref = out_refs
        dq_acc, s0_ref, s1_ref, dp0_ref, dp1_ref = scratch
        qi = pl.program_id(1)
        s_refs, dp_refs = (s0_ref, s1_ref), (dp0_ref, dp1_ref)

        @pl.when(qi == 0)
        def _():
            dkv_ref[...] = jnp.zeros_like(dkv_ref)
            dkr_ref[...] = jnp.zeros_like(dkr_ref)

        dq_acc[...] = jnp.zeros_like(dq_acc)

        def prefetch(k, slot):
            kvb = kv_ref[_key_rows(k), :]
            k2 = jnp.concatenate([kvb, kr_ref[_key_rows(k), :]], axis=1)
            s_refs[slot][...] = lax.dot_general(q_ref[...], k2, NT, preferred_element_type=F32)
            dp_refs[slot][...] = lax.dot_general(do_ref[...], kvb, NT, preferred_element_type=F32)

        def process(k, slot, diagonal):
            rows = _key_rows(k)
            s = s_refs[slot][...]
            if diagonal:
                s = _causal_mask(s)
            p = jnp.exp2(s - jnp.tile(lse_ref[...], (1, rep)))
            dsb = (p * (dp_refs[slot][...] - jnp.tile(dl_ref[...], (1, rep)))).astype(BF)
            dk2 = lax.dot_general(dsb, q_ref[...], TN, preferred_element_type=F32) * LN2
            dkv_ref[rows, :] += lax.dot_general(p.astype(BF), do_ref[...], TN, preferred_element_type=F32) + dk2[:, :KV_PAD]
            dkr_ref[rows, :] += dk2[:, KV_PAD:KV_PAD + LANES]
            k2 = jnp.concatenate([kv_ref[rows, :], kr_ref[rows, :]], axis=1)
            dq_acc[...] += jnp.dot(dsb, k2, preferred_element_type=F32)

        _pipelined_key_blocks(qi, prefetch, process)
        dq_ref[...] = dq_acc[...]

    def qrow(n):
        return pl.BlockSpec((TM, n), lambda h, qi: (qi, h))

    def head(n):
        return pl.BlockSpec((tp, n), lambda h, qi: (0, h))

    return _side_call(
        body, side, name=name, grid=(nh, nb),
        in_specs=[qrow(Q_PAD), head(KV_PAD), pl.BlockSpec((tp, LANES), lambda h, qi: (0, 0)), qrow(KV_PAD), qrow(LANES), qrow(LANES)],
        out_specs=[qrow(Q_PAD), head(KV_PAD), head(LANES)],
        out_shape=[_sds((tp, nh * Q_PAD), F32), _sds((tp, nh * KV_PAD), F32), _sds((tp, nh * LANES), F32)],
        scratch_shapes=[pltpu.VMEM((TM, Q_PAD), F32)] + [pltpu.VMEM((TM, TM), F32)] * 4, args=(q2, kv, kr, do2, lse, dl),
    )


def _rms_bwd(x, g, dy):
    r = lax.rsqrt(jnp.mean(x * x, axis=-1, keepdims=True) + RMS_EPS)
    gy = dy * g
    dx = r * gy - x * (r * r * r) * jnp.mean(x * gy, axis=-1, keepdims=True)
    return dx, jnp.sum(dy * x * r, axis=0, keepdims=True)


def mla_prep_bwd(dq2, dkv, dkr, p, gq, gkv, wuq_p, wukv, tabs_bwd, dp, *, name):
    tp = dq2.shape[0]
    nh = MLA_HEADS

    def body(dq_ref, dkv_ref, dkr_ref, cq_ref, ckv_ref, gq_ref, gkv_ref, wuq_ref, wukv_ref,
             cq_t, s1q_t, s2q_t, ck_t, s1k_t, s2k_t, dp_in_ref, dqb_ref, dsm_ref, dgq_ref, dgkv_ref):
        i = pl.program_id(0)

        @pl.when(i == 0)
        def _():
            dgq_ref[...] = jnp.zeros_like(dgq_ref)
            dgkv_ref[...] = jnp.zeros_like(dgkv_ref)

        dqb = _rope(dq_ref[...], cq_t[...], s1q_t[...], s2q_t[...], nh).astype(BF)
        dqb_ref[...] = dqb
        dcqn = lax.dot_general(dqb, wuq_ref[...], NT, preferred_element_type=F32)
        dcq, dgq = _rms_bwd(cq_ref[...].astype(F32), gq_ref[...], dcqn)
        dckvn = lax.dot_general(dkv_ref[...].astype(BF), wukv_ref[...], NT, preferred_element_type=F32)
        dckv, dgkv = _rms_bwd(ckv_ref[...].astype(F32), gkv_ref[...], dckvn)
        dkr_heads = dkr_ref[...]
        dkr_sum = dkr_heads[:, :LANES]
        for h in range(1, nh):
            dkr_sum = dkr_sum + dkr_heads[:, h * LANES:(h + 1) * LANES]
        dkr = _rope(dkr_sum, ck_t[...], s1k_t[...], s2k_t[...], 1)
        dsm_ref[...] = jnp.concatenate([dcq, dckv, dkr], axis=1).astype(BF)
        dgq_ref[...] += dgq
        dgkv_ref[...] += dgkv

    def rows(n, col=0):
        return pl.BlockSpec((TMH, n), functools.partial(lambda i, col: (i, col), col=col))

    return _pcall(
        body, name=name, grid=(tp // TMH,),
        in_specs=[rows(nh * Q_PAD), rows(nh * KV_PAD), rows(nh * LANES), rows(Q_LORA, 1536 // Q_LORA), rows(KV_LORA, 1792 // KV_LORA),
                  _row_vec(Q_LORA), _row_vec(KV_LORA),
                  pl.BlockSpec((Q_LORA, nh * Q_PAD), lambda i: (0, 0)), pl.BlockSpec((KV_LORA, nh * KV_PAD), lambda i: (0, 0)),
                  rows(Q_PAD), rows(Q_PAD), rows(Q_PAD), rows(LANES), rows(LANES), rows(LANES), pl.BlockSpec(memory_space=pl.ANY)],
        out_specs=[rows(nh * Q_PAD), rows(Q_LORA + KV_LORA + LANES, 1536 // (Q_LORA + KV_LORA + LANES)), _row_vec(Q_LORA),
                   _row_vec(KV_LORA)],
        out_shape=[_sds((tp, nh * Q_PAD), BF), _sds(dp.shape, dp.dtype), _sds((1, Q_LORA), F32), _sds((1, KV_LORA), F32)],
        input_output_aliases={15: 1}, compiler_params=_params(1),
    )(dq2, dkv, dkr, p, p, gq, gkv, wuq_p, wukv, *tabs_bwd, dp)


def conv_bwd(dy, p, conv, w, dp, *, name):
    tp = dy.shape[0]
    nb = tp // TM

    def body(dy_ref, b_ref, c_ref, h_ref, cv_ref, w_ref, dp_in_ref, dp_ref, dw0_ref, dw1_ref, dw2_ref, dbuf):
        i = pl.program_id(0)

        @pl.when(i == 0)
        def _():
            dbuf[TM:TM + 8, :] = jnp.zeros((8, D_CONV), F32)
            dw0_ref[...] = jnp.zeros_like(dw0_ref)
            dw1_ref[...] = jnp.zeros_like(dw1_ref)
            dw2_ref[...] = jnp.zeros_like(dw2_ref)

        dyv = dy_ref[...]
        c = c_ref[...].astype(F32)
        hh = h_ref[...].astype(F32)
        dconv = dyv * b_ref[...].astype(F32)
        dbuf[0:TM, :] = dconv
        d1 = dbuf[pl.ds(1, TM), :]
        d2 = dbuf[pl.ds(2, TM), :]
        w_all = w_ref[...]
        de = w_all[2:3] * dconv + w_all[1:2] * d1 + w_all[0:1] * d2
        e = c * hh
        dp_ref[...] = jnp.concatenate([dyv * cv_ref[...].astype(F32), de * hh, de * c], axis=1).astype(BF)
        dw0_ref[...] += jnp.sum(d2 * e, axis=0, keepdims=True)
        dw1_ref[...] += jnp.sum(d1 * e, axis=0, keepdims=True)
        dw2_ref[...] += jnp.sum(dconv * e, axis=0, keepdims=True)
        dbuf[TM:TM + 8, :] = dbuf[0:8, :]

    def col(j):
        return pl.BlockSpec((TM, D_CONV), functools.partial(lambda i, j: (nb - 1 - i, j), j=j))

    return _pcall(
        body, name=name, grid=(nb,),
        in_specs=[col(0), col(0), col(1), col(2), col(0), pl.BlockSpec((3, D_CONV), lambda i: (0, 0)),
                  pl.BlockSpec(memory_space=pl.ANY)],
        out_specs=[pl.BlockSpec((TM, 3 * D_CONV), lambda i: (nb - 1 - i, 0))] + [_row_vec(D_CONV)] * 3,
        out_shape=[_sds(dp.shape, dp.dtype)] + [_sds((1, D_CONV), F32)] * 3, input_output_aliases={6: 0},
        scratch_shapes=[pltpu.VMEM((TM + 8, D_CONV), F32)], compiler_params=_params(1),
    )(dy, p, p, p, conv, w, dp)


def adamw(w, g, m, v, *, name):
    r, c = w.shape
    tr = r
    for cand in (256, 128, 64, 32, 16, 8):
        if r % cand == 0 and r > cand:
            tr = cand
            break

    def body(w_ref, g_ref, m_ref, v_ref, d_ref, nm_ref, nv_ref):
        gv = g_ref[...]
        nm = ADAM_B1 * m_ref[...] + (1.0 - ADAM_B1) * gv
        nv = ADAM_B2 * v_ref[...] + (1.0 - ADAM_B2) * (gv * gv)
        m_hat = nm / (1.0 - ADAM_B1 ** ADAM_STEP)
        v_hat = nv / (1.0 - ADAM_B2 ** ADAM_STEP)
        d_ref[...] = -ADAM_LR * (m_hat / (jnp.sqrt(v_hat) + ADAM_EPS) + ADAM_WD * w_ref[...])
        nm_ref[...] = nm
        nv_ref[...] = nv

    blk = pl.BlockSpec((tr, c), lambda i: (i, 0))
    return _pcall(
        body, name=name, grid=(r // tr,), in_specs=[blk] * 4, out_specs=[blk] * 3,
        out_shape=[_sds((r, c), F32)] * 3, compiler_params=_params(1),
    )(w, g, m, v)


HBM_SPEC = pl.BlockSpec(memory_space=pltpu.HBM)


def _place():
    return lax.axis_index("x"), lax.axis_index("y"), lax.axis_index("c")


def _other_chips(x, y):
    return [(1 - x, y), (x, 1 - y), (1 - x, 1 - y)]


def _half(ref_or_shape_rows, c):
    return pl.ds(c * (ref_or_shape_rows // 2), ref_or_shape_rows // 2)


def gather_side(items):
    n = len(items)
    shards = [s for s, _ in items]
    layers = [l for _, l in items]

    def program(x_refs, o_refs, send_sems, recv_sems):
        x, y, c = _place()
        me = 2 * x + y
        chips = _other_chips(x, y)

        def copy(sem, src, dst, to):
            return pltpu.make_async_remote_copy(src_ref=src, dst_ref=dst, send_sem=send_sems.at[sem], recv_sem=recv_sems.at[sem],
                                                device_id=to, device_id_type=MESH)

        def src(i):
            return x_refs[i].at[layers[i], _half(x_refs[i].shape[1], c)]

        def dst(i, slot, cc):
            return o_refs[i].at[slot, _half(o_refs[i].shape[1], cc)]

        sends = [copy(6 * i + k, src(i), dst(i, me, c), (px, py, c)) for i in range(n) for k, (px, py) in enumerate(chips)]
        passed = [copy(6 * i + 3 + k, dst(i, 2 * px + py, c), dst(i, 2 * px + py, c), (x, y, 1 - c))
                  for k, (px, py) in enumerate(chips) for i in range(n)]

        def start():
            for cp in sends:
                cp.start()

        def finish():
            pos = 0
            for k, (px, py) in enumerate(chips):
                for i in range(n):
                    copy(6 * i + k, src(i), dst(i, 2 * px + py, c), (px, py, c)).wait_recv()
                    passed[pos].start()
                    pos += 1
            for k, (px, py) in enumerate(chips):
                for i in range(n):
                    copy(6 * i + 3 + k, dst(i, 2 * px + py, 1 - c), dst(i, 2 * px + py, 1 - c), (x, y, 1 - c)).wait_recv()
            for cp in sends + passed:
                cp.wait_send()

        return start, finish

    prefilled = [jnp.broadcast_to(s[l][None], (4,) + s.shape[1:]) for s, l in items]
    return shards, prefilled, 6 * n, program


def scatter_side(pss):
    n = len(pss)

    def program(p_refs, o_refs, send_sems, recv_sems):
        x, y, c = _place()
        me = 2 * x + y
        chips = _other_chips(x, y)

        def copy(i, k, j_src, j_dst, to):
            return pltpu.make_async_remote_copy(src_ref=p_refs[i].at[j_src], dst_ref=o_refs[i].at[j_dst],
                                                send_sem=send_sems.at[3 * i + k], recv_sem=recv_sems.at[3 * i + k],
                                                device_id=to, device_id_type=MESH)

        sends = [copy(i, k, 2 * px + py, me, (px, py, c)) for i in range(n) for k, (px, py) in enumerate(chips)]

        def start():
            for cp in sends:
                cp.start()

        def finish():
            for i in range(n):
                for k, (px, py) in enumerate(chips):
                    copy(i, k, me, 2 * px + py, (px, py, c)).wait_recv()
            for cp in sends:
                cp.wait_send()

        return start, finish

    xi, yi, _ = _place()
    own = jnp.arange(4)[:, None, None] == 2 * xi + yi
    prefilled = [jnp.where(own, p, jnp.zeros_like(p)) for p in pss]
    return list(pss), prefilled, 3 * n, program


def exchange_alone(side, *, name):
    inputs, prefilled, n_sems, program = side
    a, b = len(inputs), len(prefilled)

    def body(*refs):
        start, finish = program(refs[:a], refs[a + b:a + 2 * b], refs[-2], refs[-1])
        start()
        finish()

    return _pcall(
        body, name=name, in_specs=[HBM_SPEC] * (a + b), out_specs=[HBM_SPEC] * b, out_shape=[_sds(p.shape, p.dtype) for p in prefilled],
        input_output_aliases={a + i: i for i in range(b)}, scratch_shapes=[pltpu.SemaphoreType.DMA((n_sems,))] * 2,
    )(*inputs, *prefilled)


def pair_exchange(gs, *, name):
    n = len(gs)

    def body(*refs):
        g_refs, o_refs = refs[:n], refs[n:2 * n]
        send_sems, recv_sems = refs[2 * n:]
        x, y, c = _place()
        cps = [pltpu.make_async_remote_copy(src_ref=g_refs[i].at[:, _half(g_refs[i].shape[1], 1 - c)], dst_ref=o_refs[i],
                                            send_sem=send_sems.at[i], recv_sem=recv_sems.at[i], device_id=(x, y, 1 - c),
                                            device_id_type=MESH)
               for i in range(n)]
        for cp in cps:
            cp.start()
        for cp in cps:
            cp.wait()

    return _pcall(
        body, name=name, in_specs=[HBM_SPEC] * n, out_specs=[HBM_SPEC] * n,
        out_shape=[_sds((4, g.shape[1] // 2, g.shape[2]), g.dtype) for g in gs],
        scratch_shapes=[pltpu.SemaphoreType.DMA((n,)), pltpu.SemaphoreType.DMA((n,))],
    )(*gs)


def _comm_rows(a, b, itemsize):
    return a // 2 if a * b * itemsize > (3 << 19) and a % 16 == 0 else a


def pair_add(g, s1, c_idx, *, name):
    n, a, b = g.shape
    ah = a // 2
    ta = _comm_rows(ah, b, 2)
    nblk = ah // ta

    def body(c_ref, g_ref, s_ref, o_ref):
        o_ref[...] = (g_ref[...].astype(F32) + s_ref[...].astype(F32)).astype(o_ref.dtype)

    grid_spec = pltpu.PrefetchScalarGridSpec(
        num_scalar_prefetch=1, grid=(n, nblk),
        in_specs=[pl.BlockSpec((1, ta, b), lambda j, i, c_ref: (j, c_ref[0] * nblk + i, 0)),
                  pl.BlockSpec((1, ta, b), lambda j, i, c_ref: (j, i, 0))],
        out_specs=pl.BlockSpec((1, ta, b), lambda j, i, c_ref: (j, i, 0)),
    )
    return _pcall(body, name=name, grid_spec=grid_spec, out_shape=_sds((n, ah, b), g.dtype), compiler_params=_params(2))(
        c_idx, g, s1)


def sum_chunks(s2, *, name):
    n, a, b = s2.shape
    ta = _comm_rows(a, b, 4)

    def body(s_ref, o_ref):
        acc = s_ref[0].astype(F32)
        for j in range(1, n):
            acc = acc + s_ref[j].astype(F32)
        o_ref[...] = acc

    return _pcall(
        body, name=name, grid=(a // ta,), in_specs=[pl.BlockSpec((n, ta, b), lambda i: (0, i, 0))],
        out_specs=pl.BlockSpec((ta, b), lambda i: (i, 0)), out_shape=_sds((a, b), F32), compiler_params=_params(1),
    )(s2)


def pair_gather(rcs, *, name):
    n = len(rcs)

    def body(*refs):
        r_refs, o_refs = refs[:n], refs[2 * n:3 * n]
        send_sems, recv_sems = refs[3 * n:]
        x, y, c = _place()

        def copy(i, half):
            return pltpu.make_async_remote_copy(src_ref=r_refs[i], dst_ref=o_refs[i].at[half], send_sem=send_sems.at[i],
                                                recv_sem=recv_sems.at[i], device_id=(x, y, 1 - c), device_id_type=MESH)

        sends = [copy(i, c) for i in range(n)]
        for cp in sends:
            cp.start()
        for i in range(n):
            copy(i, 1 - c).wait_recv()
        for cp in sends:
            cp.wait_send()

    prefilled = [jnp.broadcast_to(r[None], (2,) + r.shape) for r in rcs]
    return _pcall(
        body, name=name, in_specs=[HBM_SPEC] * (2 * n), out_specs=[HBM_SPEC] * n,
        out_shape=[_sds(p.shape, p.dtype) for p in prefilled], input_output_aliases={n + i: i for i in range(n)},
        scratch_shapes=[pltpu.SemaphoreType.DMA((n,)), pltpu.SemaphoreType.DMA((n,))],
    )(*rcs, *prefilled)


def exchange_small(arrs, *, reduce, name):
    n = len(arrs)

    def body(*refs):
        v_refs, o_refs = refs[:n], refs[n:2 * n]
        bufs = refs[2 * n:3 * n] if reduce else o_refs
        send_sems, recv_sems = refs[-2:]
        x, y, c = _place()
        me = 4 * x + 2 * y + c
        for i in range(n):
            bufs[i][me] = v_refs[i][...]

        def peer(k):
            dx, dy, dc = (k >> 2) & 1, (k >> 1) & 1, k & 1
            return (1 - x if dx else x, 1 - y if dy else y, 1 - c if dc else c)

        def copy(i, k, slot):
            return pltpu.make_async_remote_copy(src_ref=v_refs[i], dst_ref=bufs[i].at[slot], send_sem=send_sems.at[7 * i + k - 1],
                                                recv_sem=recv_sems.at[7 * i + k - 1], device_id=peer(k), device_id_type=MESH)

        sends = [copy(i, k, me) for i in range(n) for k in range(1, 8)]
        for cp in sends:
            cp.start()
        for i in range(n):
            for k in range(1, 8):
                px, py, pc = peer(k)
                copy(i, k, 4 * px + 2 * py + pc).wait_recv()
        for cp in sends:
            cp.wait_send()
        if reduce:
            for i in range(n):
                acc = bufs[i][0]
                for d in range(1, 8):
                    acc = acc + bufs[i][d]
                o_refs[i][...] = acc

    vmem = pl.BlockSpec(memory_space=pltpu.VMEM)
    stacked = [(8,) + a.shape for a in arrs]
    return _pcall(
        body, name=name, in_specs=[vmem] * n, out_specs=[vmem] * n,
        out_shape=[_sds(a.shape if reduce else s, F32) for a, s in zip(arrs, stacked)],
        scratch_shapes=([pltpu.VMEM(s, F32) for s in stacked] if reduce else [])
        + [pltpu.SemaphoreType.DMA((7 * n,)), pltpu.SemaphoreType.DMA((7 * n,))],
    )(*arrs)


def _pad_rows(n, mult):
    return -(-n // mult) * mult


def _chip_major(g, b):
    return g.reshape(g.shape[0], 4, b).transpose(1, 0, 2)


def _rope_tables(tp):
    inv_freq = 1.0 / (ROPE_BASE ** (jnp.arange(0, QK_ROPE, 2, dtype=F32) / QK_ROPE))
    ang = jnp.arange(tp, dtype=F32)[:, None] * inv_freq[None, :]
    cos, sin = jnp.cos(ang), jnp.sin(ang)
    one = lambda n: jnp.ones((tp, n), F32)
    zero = lambda n: jnp.zeros((tp, n), F32)
    cq = jnp.concatenate([one(128), cos, cos, one(96)], axis=1)
    s1q = jnp.concatenate([zero(144), sin, zero(96)], axis=1)
    s2q = jnp.concatenate([zero(128), -sin, zero(112)], axis=1)
    ck = jnp.concatenate([cos, cos, zero(96)], axis=1)
    s1k = jnp.concatenate([zero(16), sin, zero(96)], axis=1)
    s2k = jnp.concatenate([-sin, zero(112)], axis=1)
    fwd = (cq * (ATT_SCALE * LOG2E), s1q * (ATT_SCALE * LOG2E), s2q * (ATT_SCALE * LOG2E), ck, s1k, s2k)
    bwd = (cq * ATT_SCALE, -s1q * ATT_SCALE, -s2q * ATT_SCALE, ck, -s1k, -s2k)
    return fwd, bwd


def _pad_w_in(w):
    return jnp.concatenate([w[:, :1952], jnp.zeros((w.shape[0], 96), w.dtype), w[:, 1952:]], axis=1)


def _pad_w_uq(w):
    w = w.reshape(Q_LORA, MLA_HEADS, QK_NOPE + QK_ROPE)
    z = lambda n: jnp.zeros((Q_LORA, MLA_HEADS, n), w.dtype)
    return jnp.concatenate([w[..., :QK_NOPE], z(64), w[..., QK_NOPE:], z(96)], axis=-1).reshape(Q_LORA, MLA_HEADS * Q_PAD)


def _unpad_w_uq(w):
    w = w.reshape(Q_LORA, MLA_HEADS, Q_PAD)
    return jnp.concatenate([w[..., :QK_NOPE], w[..., 128:128 + QK_ROPE]], axis=-1).reshape(Q_LORA, MLA_HEADS * (QK_NOPE + QK_ROPE))


def _pad_w_br_mla(w):
    w = w.reshape(MLA_HEADS, V_HEAD, D_MODEL)
    return jnp.concatenate([jnp.zeros_like(w), w], axis=1).reshape(MLA_HEADS * KV_PAD, D_MODEL)


def _unpad_w_br_mla(w):
    return w.reshape(MLA_HEADS, KV_PAD, D_MODEL)[:, V_HEAD:].reshape(MLA_HEADS * V_HEAD, D_MODEL)


def _riding(hooks, where, l, *args):
    make = hooks.get(where)
    ride = make(l, *args) if make else None
    return ride if ride else (None, lambda results: None)


def _layer_fwd(l, st, xprev, gp, bp, hb, w, tabs, hooks):
    ln_g, ln_b = w["ln_g"], w["ln_b"]
    lg = lambda k: ln_g[l, k][None]
    lb = lambda k: ln_b[l, k][None]
    s = {}
    s["x0"], s["gp0"], s["bp0"], s["hb0"] = xprev, gp, bp, hb
    side, got = _riding(hooks, "ffn1_fwd", l)
    s["g1"], s["u1"], s["a1"], *extras = ffn_up(hb, w["ffn1_w_up"][l], name="ffn_up", side=side)
    got(extras)
    s["xh1"], s["rs1"], s["hb1"] = down_ln(s["a1"], w["ffn1_w_down"][l], xprev, gp, bp, lg(0), lb(0), name="ffn_down_ln")
    s["p"] = mm_rows([(s["hb1"], w["mix_w_in"][l], False, 0)], D_IN_PAD, name="mix_in", tn=1024, out_dtype=BF)
    gq, gkv = w["q_norm_g"][l][None], w["kv_norm_g"][l][None]
    s["cqn"], s["ckvn"], s["q2"], s["kv"], s["kr"] = mla_prep(s["p"], gq, gkv, w["w_uq"][l], w["w_ukv"][l], tabs, name="mla_prep")
    side, got = _riding(hooks, "attn_fwd", l)
    s["o2"], s["lse"], *extras = attn_fwd(s["q2"], s["kv"], s["kr"], name="attn_fwd", side=side)
    got(extras)
    s["ycv"], s["conv"] = conv_fwd(s["p"], w["conv_w"][l], name="conv_fwd")
    s["bc"], s["bm"], s["mg"], s["xh2"], s["rs2"], s["hb2"] = merge_out_ln(
        s["ycv"], s["o2"], s["p"], w["mix_b_gate"][l], w["w_br_conv"][l], w["w_br_mla"][l], w["w_o"][l],
        s["xh1"], lg(0), lb(0), lg(1), lb(1), name="merge_out_ln")
    s["g2"], s["u2"], s["a2"] = ffn_up(s["hb2"], w["ffn2_w_up"][l], name="ffn_up")
    s["xh3"], s["rs3"], s["hb3"] = down_ln(s["a2"], w["ffn2_w_down"][l], s["xh2"], lg(1), lb(1), lg(2), lb(2), name="ffn_down_ln")
    st.append(s)
    return s["xh3"], lg(2), lb(2), s["hb3"]


def _ffn_bwd(dh, w_up, w_down, ln_gain, hb_in, gate, up, act, xh, rs):
    dz, dzb, dgam, dbet = ln_bwd(dh, xh, rs, ln_gain, branch_scale=0.5, name="ln_bwd")
    d_wd = tn_mm(act, dzb, tm=D_FF // 2, name="dw_down", shard=("rows", D_FF // 4))
    dgate, dup = ffn_down_bwd(dzb, w_down, gate, up, name="ffn_down_bwd")
    d_w = tn_mm(hb_in, dgate, tm=512, name="dw_up", shard=("cols", D_FF // 2), slot0=0)
    d_w = tn_mm(hb_in, dup, tm=512, name="dw_up", shard=("cols", D_FF // 2), slot0=2, dst=d_w)
    dh_in = mm_rows([(dgate, w_up, True, 0), (dup, w_up, True, 1)], D_MODEL, name="ffn_up_bwd", tn=512, addend=dz, add_scale=ALPHA)
    return dh_in, d_w, d_wd, dgam, dbet


def _layer_bwd(l, s, dh, w, tabs_bwd, hooks):
    ln_g = w["ln_g"]
    lg = lambda k: ln_g[l, k][None]
    g = {}
    dh, g["ffn2_w_up"], g["ffn2_w_down"], dg2, db2 = _ffn_bwd(
        dh, w["ffn2_w_up"][l], w["ffn2_w_down"][l], lg(2), s["hb2"], s["g2"], s["u2"], s["a2"], s["xh3"], s["rs3"])
    dz, dzb, dg1, db1 = ln_bwd(dh, s["xh2"], s["rs2"], lg(1), branch_scale=1.0, name="ln_bwd")
    g["w_o"] = tn_mm(s["mg"], dzb, tm=1024, name="dw_o", shard=("rows", D_MODEL // 4))
    dbc, dbm, dp, dycv, do2, dl, g["mix_b_gate"] = merge_bwd(
        dzb, w["w_o"][l], s["bc"], s["bm"], s["p"], w["mix_b_gate"][l], w["w_br_conv"][l], w["w_br_mla"][l], s["o2"], name="merge_bwd")
    g["w_br_conv"] = tn_mm(s["ycv"], dbc, tm=512, name="dw_br_conv", shard=("cols", D_MODEL // 4))
    g["w_br_mla"] = _chip_major(_unpad_w_br_mla(tn_mm(s["o2"], dbm, tm=1024, name="dw_br_mla")), D_MODEL // 4)
    side, got = _riding(hooks, "attn_bwd", l, g)
    dq2, dkv, dkr, *extras = attn_bwd(s["q2"], s["kv"], s["kr"], do2, s["lse"], dl, name="attn_bwd", side=side)
    got(extras)
    gq, gkv = w["q_norm_g"][l][None], w["kv_norm_g"][l][None]
    dqb, dp, g["q_norm_g"], g["kv_norm_g"] = mla_prep_bwd(dq2, dkv, dkr, s["p"], gq, gkv, w["w_uq"][l], w["w_ukv"][l], tabs_bwd, dp,
                                                          name="mla_prep_bwd")
    g["w_uq"] = _chip_major(_unpad_w_uq(tn_mm(s["cqn"], dqb, tm=Q_LORA, name="dw_uq")), MLA_HEADS * (QK_NOPE + QK_ROPE) // 4)
    g["w_ukv"] = tn_mm(s["ckvn"], dkv, tm=KV_LORA, name="dw_ukv", shard=("cols", MLA_HEADS * KV_PAD // 4))
    dp, dw0, dw1, dw2 = conv_bwd(dycv, s["p"], s["conv"], w["conv_w"][l], dp, name="conv_bwd")
    g["conv_w"] = jnp.concatenate([dw0, dw1, dw2], axis=0)
    d_in = tn_mm(s["hb1"], dp, tm=512, name="dw_in")
    g["mix_w_in"] = _chip_major(jnp.concatenate([d_in[:, :1952], d_in[:, 2048:]], axis=1), D_IN // 4)
    dh = mm_rows([(dp, w["mix_w_in"][l], True, 0)], D_MODEL, name="mix_in_bwd", tn=512, addend=dz, add_scale=ALPHA)
    dh, g["ffn1_w_up"], g["ffn1_w_down"], dg0, db0 = _ffn_bwd(
        dh, w["ffn1_w_up"][l], w["ffn1_w_down"][l], lg(0), s["hb0"], s["g1"], s["u1"], s["a1"], s["xh1"], s["rs1"])
    g["ln_g"] = jnp.concatenate([dg0, dg1, dg2], axis=0)
    g["ln_b"] = jnp.concatenate([db0, db1, db2], axis=0)
    return dh, g


BIG = ("ffn1_w_up", "ffn1_w_down", "mix_w_in", "w_uq", "w_ukv", "w_br_conv", "w_br_mla", "w_o", "ffn2_w_up", "ffn2_w_down")
BIG_AXIS = (2, 1, 2, 2, 2, 2, 2, 1, 2, 1)
FFN1_MATRICES = ("ffn1_w_up", "ffn1_w_down")
MIXER_MATRICES = ("mix_w_in", "w_uq", "w_ukv", "w_br_conv", "w_br_mla", "w_o")
FFN2_MATRICES = ("ffn2_w_up", "ffn2_w_down")
SMALL_SHARDED = ("meta_tokens", "mix_b_gate", "conv_w", "ln_g", "ln_b")
SMALL_REPLICATED = ("q_norm_g", "kv_norm_g")
WEIGHTS = ("meta_tokens", "ffn1_w_up", "ffn1_w_down", "mix_w_in", "mix_b_gate", "conv_w", "q_norm_g", "w_uq", "kv_norm_g", "w_ukv",
           "w_br_conv", "w_br_mla", "w_o", "ffn2_w_up", "ffn2_w_down", "ln_g", "ln_b")


def _view2d(a):
    return a.reshape(-1, a.shape[-1])


def _local_grads(x_row, target_row, w, hooks=None):
    hooks = hooks or {}
    seq = x_row.shape[0]
    t_real = N_META + seq
    tp = _pad_rows(t_real, TM)
    pad = tp - t_real
    h0 = jnp.concatenate([w["meta_tokens"], x_row, jnp.zeros((pad, D_MODEL), F32)], axis=0)
    target_p = jnp.concatenate([jnp.zeros((N_META, D_MODEL), F32), target_row, jnp.zeros((pad, D_MODEL), F32)], axis=0)
    tabs, tabs_bwd = _rope_tables(tp)
    ones = jnp.ones((1, D_MODEL), F32)
    zeros = jnp.zeros((1, D_MODEL), F32)
    saved = []
    cur = (h0, ones, zeros, h0.astype(BF))
    for l in range(DEPTH):
        cur = _layer_fwd(l, saved, *cur, w, tabs, hooks)
    dh, loss_acc = loss_grad(cur[0], cur[1], cur[2], target_p, seq, name="loss_grad")
    grads = [None] * DEPTH
    for l in reversed(range(DEPTH)):
        dh, grads[l] = _layer_bwd(l, saved[l], dh, w, tabs_bwd, hooks)
        if "layer_bwd_done" in hooks:
            hooks["layer_bwd_done"](l, grads[l])
    return loss_acc, dh[N_META:t_real], dh[:N_META], grads


def kernel(x, meta_tokens, ffn1_w_up, ffn1_w_down, mix_w_in, mix_b_gate, conv_w, q_norm_g, w_uq, kv_norm_g, w_ukv, w_br_conv, w_br_mla, w_o, ffn2_w_up, ffn2_w_down, ln_g, ln_b, loss_target, m_meta_tokens, m_ffn1_w_up, m_ffn1_w_down, m_mix_w_in, m_mix_b_gate, m_conv_w, m_q_norm_g, m_w_uq, m_kv_norm_g, m_w_ukv, m_w_br_conv, m_w_br_mla, m_w_o, m_ffn2_w_up, m_ffn2_w_down, m_ln_g, m_ln_b, v_meta_tokens, v_ffn1_w_up, v_ffn1_w_down, v_mix_w_in, v_mix_b_gate, v_conv_w, v_q_norm_g, v_w_uq, v_kv_norm_g, v_w_ukv, v_w_br_conv, v_w_br_mla, v_w_o, v_ffn2_w_up, v_ffn2_w_down, v_ln_g, v_ln_b):
    local = dict(meta_tokens=meta_tokens, ffn1_w_up=ffn1_w_up, ffn1_w_down=ffn1_w_down, mix_w_in=mix_w_in, mix_b_gate=mix_b_gate,
                 conv_w=conv_w, q_norm_g=q_norm_g, w_uq=w_uq, kv_norm_g=kv_norm_g, w_ukv=w_ukv, w_br_conv=w_br_conv,
                 w_br_mla=w_br_mla, w_o=w_o, ffn2_w_up=ffn2_w_up, ffn2_w_down=ffn2_w_down, ln_g=ln_g, ln_b=ln_b)
    mom_m = dict(zip(WEIGHTS, (m_meta_tokens, m_ffn1_w_up, m_ffn1_w_down, m_mix_w_in, m_mix_b_gate, m_conv_w, m_q_norm_g, m_w_uq,
                               m_kv_norm_g, m_w_ukv, m_w_br_conv, m_w_br_mla, m_w_o, m_ffn2_w_up, m_ffn2_w_down, m_ln_g, m_ln_b)))
    mom_v = dict(zip(WEIGHTS, (v_meta_tokens, v_ffn1_w_up, v_ffn1_w_down, v_mix_w_in, v_mix_b_gate, v_conv_w, v_q_norm_g, v_w_uq,
                               v_kv_norm_g, v_w_ukv, v_w_br_conv, v_w_br_mla, v_w_o, v_ffn2_w_up, v_ffn2_w_down, v_ln_g, v_ln_b)))
    xi, yi, ci = _place()
    chip = 2 * xi + yi

    shards = {n: local[n].astype(BF) for n in BIG}
    axis = dict(zip(BIG, BIG_AXIS))
    pad_layout = {"mix_w_in": _pad_w_in, "w_uq": _pad_w_uq, "w_br_mla": _pad_w_br_mla}
    w = {n: [None] * DEPTH for n in BIG}

    def fetch(keys):
        def install(gathered):
            for (n, l), g in zip(keys, gathered):
                if n in FFN1_MATRICES + FFN2_MATRICES:
                    w[n][l] = g
                    continue
                full = jnp.concatenate([g[j] for j in range(4)], axis=axis[n] - 1)
                w[n][l] = pad_layout[n](full) if n in pad_layout else full
        return gather_side([(shards[n], l) for n, l in keys]), install

    first, install_first = fetch([(n, 0) for n in FFN1_MATRICES])
    install_first(exchange_alone(first, name="gather_weights"))
    fetch_under = {("ffn1_fwd", 0): [(n, 0) for n in MIXER_MATRICES],
                   ("attn_fwd", 0): [(n, 0) for n in FFN2_MATRICES] + [(n, 1) for n in BIG]}
    hooks = {where: functools.partial(lambda l, where: fetch(fetch_under[where, l]) if (where, l) in fetch_under else None, where=where)
             for where in ("ffn1_fwd", "attn_fwd")}
    stacked = exchange_small([_view2d(local[n]) for n in SMALL_SHARDED], reduce=False, name="gather_small")
    for n, st in zip(SMALL_SHARDED, stacked):
        full = jnp.concatenate([st[2 * j] for j in range(4)], axis=-1)
        w[n] = full.reshape(local[n].shape[:-1] + (full.shape[-1],))
    for n in SMALL_REPLICATED:
        w[n] = local[n]

    c_idx = jnp.reshape(ci, (1,)).astype(jnp.int32)
    done, from_chips = {}, {}

    def send(keys, grad_of):
        glist = [grad_of[k] for k in keys]
        from_sibling = pair_exchange(glist, name="rs_pair_exchange")
        sums = [pair_add(a, s, c_idx, name="rs_pair_add") for a, s in zip(glist, from_sibling)]
        return scatter_side(sums), lambda results: from_chips.update(zip(keys, results))

    early = FFN2_MATRICES + ("w_o", "w_br_conv", "w_br_mla")
    hooks["layer_bwd_done"] = lambda l, g: done.update({(n, l): g[n] for n in BIG})
    hooks["attn_bwd"] = lambda l, g: send([(n, 1) for n in BIG] + [(n, 0) for n in early],
                                          {**done, **{(n, 0): g[n] for n in early}}) if l == 0 else None

    loss_acc, grad_x, d_meta, grads = _local_grads(x[0], loss_target[0], w, hooks)
    grad_x = grad_x[None]
    last, keep_last = send([(n, 0) for n in BIG if n not in early], done)
    keep_last(exchange_alone(last, name="rs_chip_scatter"))
    keys = [(n, l) for n in BIG for l in range(DEPTH)]
    reduced = pair_gather([sum_chunks(from_chips[k], name="rs_sum") for k in keys], name="rs_pair_gather")
    reduced = {k: r.reshape(local[k[0]].shape[1:]) for k, r in zip(keys, reduced)}
    gshard = {n: jnp.stack([reduced[n, l] for l in range(DEPTH)]) for n in BIG}

    small_names = SMALL_SHARDED + SMALL_REPLICATED
    gsmall = {n: jnp.concatenate([grads[l][n] for l in range(DEPTH)], axis=0) for n in small_names if n != "meta_tokens"}
    gsmall["meta_tokens"] = d_meta
    small_red = exchange_small([gsmall[n] for n in small_names] + [loss_acc], reduce=True, name="reduce_small")
    loss = small_red[-1][0, 0]
    for n, full in zip(small_names, small_red[:-1]):
        if n in SMALL_SHARDED:
            sh = local[n].shape[-1]
            full = lax.dynamic_slice_in_dim(full, chip * sh, sh, axis=1)
        gshard[n] = full.reshape(local[n].shape)

    delta, new_m, new_v = {}, {}, {}
    for n in WEIGHTS:
        shape = local[n].shape
        d, nm, nv = adamw(_view2d(local[n]), _view2d(gshard[n]), _view2d(mom_m[n]), _view2d(mom_v[n]), name="adamw")
        delta[n], new_m[n], new_v[n] = d.reshape(shape), nm.reshape(shape), nv.reshape(shape)
    return (loss, grad_x, *[gshard[n] for n in WEIGHTS], *[delta[n] for n in WEIGHTS], *[new_m[n] for n in WEIGHTS],
            *[new_v[n] for n in WEIGHTS])
```

```python
import functools

import jax
import jax.numpy as jnp
from jax import lax
from jax.experimental import pallas as pl
from jax.experimental.pallas import tpu as pltpu

F32 = jnp.float32
BF = jnp.bfloat16
MESH = pl.DeviceIdType.MESH

D_MODEL = 1024
DEPTH = 2
N_META = 16
D_CONV = 512
MLA_HEADS = 8
QK_NOPE = 64
QK_ROPE = 32
V_HEAD = 64
Q_LORA = 256
KV_LORA = 128
ROPE_BASE = 10000.0
NEG_INF = -1e30
D_FF = 2816
ALPHA = (2 * DEPTH) ** 0.25
LN_EPS = 1e-5
RMS_EPS = 1e-6
ATT_SCALE = (QK_NOPE + QK_ROPE) ** -0.5
LOG2E = 1.4426950408889634
LN2 = 0.6931471805599453
D_IN = 4000
D_IN_PAD = 4096
Q_PAD = 256
KV_PAD = 128

ADAM_LR = 0.001
ADAM_B1 = 0.9
ADAM_B2 = 0.999
ADAM_EPS = 1e-08
ADAM_WD = 0.01
ADAM_STEP = 10

TM = 768
TMH = 384
LANES = 128
COMM_COLS = 512
COMM_ROW_BLOCK = 1472
VMEM_LIMIT_BYTES = 50 * 1024 * 1024

NT = (((1,), (1,)), ((), ()))
TN = (((0,), (0,)), ((), ()))


def _pcall(body, **kw):
    return pl.pallas_call(body, **kw)


def _params(n_axes):
    return pltpu.CompilerParams(dimension_semantics=("arbitrary",) * n_axes, vmem_limit_bytes=VMEM_LIMIT_BYTES)


def _sds(shape, dtype):
    return jax.ShapeDtypeStruct(shape, dtype)


def mm_rows(pairs, n_out, *, name, tn=None, addend=None, add_scale=1.0, out_dtype=F32, side=None):
    tp = pairs[0][0].shape[0]
    tn = tn or n_out
    in_specs, args = [], []
    for a, b, nt, kb in pairs:
        k = a.shape[1]
        in_specs.append(pl.BlockSpec((TM, k), lambda i, j: (i, 0)))
        if nt and b.ndim == 3:
            in_specs.append(pl.BlockSpec((2, tn, k // 2), functools.partial(lambda i, j, kb: (kb, j, 0), kb=kb)))
        elif nt:
            in_specs.append(pl.BlockSpec((tn, k), functools.partial(lambda i, j, kb: (j, kb), kb=kb)))
        else:
            in_specs.append(pl.BlockSpec((k, tn), lambda i, j: (0, j)))
        args += [a, b]
    if addend is not None:
        in_specs.append(pl.BlockSpec((TM, tn), lambda i, j: (i, j)))
        args.append(addend)
    n_pairs = len(pairs)
    nts = [p[2] for p in pairs]

    def body(refs, out_refs, scratch):
        o_ref = out_refs[0]
        acc = None
        for p in range(n_pairs):
            a = refs[2 * p][...].astype(BF)
            b = refs[2 * p + 1][...]
            if b.ndim == 3:
                b = jnp.concatenate([b[0], b[1]], axis=1)
            d = lax.dot_general(a, b, NT if nts[p] else (((1,), (0,)), ((), ())), preferred_element_type=F32)
            acc = d if acc is None else acc + d
        if addend is not None:
            acc = acc + add_scale * refs[2 * n_pairs][...]
        o_ref[...] = acc.astype(o_ref.dtype)

    out = _side_call(
        body, side, name=name, grid=(tp // TM, n_out // tn), in_specs=in_specs,
        out_specs=[pl.BlockSpec((TM, tn), lambda i, j: (i, j))], out_shape=[_sds((tp, n_out), out_dtype)],
        scratch_shapes=[], args=args,
    )
    return out if side is not None else out[0]


def tn_mm(a, b, *, tm, name, out_dtype=BF, shard=None, slot0=0, dst=None):
    tp, m = a.shape
    n = b.shape[1]
    nk = tp // TM
    if shard is None:
        pieces, out_block, out_index, out_full = 1, (tm, n), (lambda i, k: (i, 0)), (m, n)
    elif shard[0] == "cols":
        pieces = n // shard[1]
        out_block, out_full = (pieces, tm, shard[1]), (4, m, shard[1])
        out_index = lambda i, k: (slot0 // pieces, i, 0)
    else:
        pieces = tm // shard[1]
        out_block, out_full = (pieces, shard[1], n), (4, m // 4, n)
        out_index = lambda i, k: (i, 0, 0)

    def body(a_ref, b_ref, *rest):
        o_ref, acc_ref = rest[-2], rest[-1]
        k = pl.program_id(1)

        @pl.when(k == 0)
        def _():
            acc_ref[...] = jnp.zeros_like(acc_ref)

        acc_ref[...] += lax.dot_general(a_ref[...].astype(BF), b_ref[...].astype(BF), TN, preferred_element_type=F32)

        @pl.when(k == nk - 1)
        def _():
            if shard is None:
                o_ref[...] = acc_ref[...].astype(o_ref.dtype)
            elif shard[0] == "cols":
                for j in range(pieces):
                    o_ref[j] = acc_ref[:, j * shard[1]:(j + 1) * shard[1]].astype(o_ref.dtype)
            else:
                for j in range(pieces):
                    o_ref[j] = acc_ref[j * shard[1]:(j + 1) * shard[1], :].astype(o_ref.dtype)

    in_specs = [pl.BlockSpec((TM, tm), lambda i, k: (k, i)), pl.BlockSpec((TM, n), lambda i, k: (k, 0))]
    args = [a, b]
    aliases = {}
    if dst is not None:
        in_specs.append(pl.BlockSpec(memory_space=pl.ANY))
        args.append(dst)
        aliases = {2: 0}
    return _pcall(
        body, name=name, grid=(m // tm, nk), in_specs=in_specs, out_specs=pl.BlockSpec(out_block, out_index),
        out_shape=_sds(out_full, out_dtype), input_output_aliases=aliases,
        scratch_shapes=[pltpu.VMEM((tm, n), F32)], compiler_params=_params(2),
    )(*args)


def _ln_store(z, g_ref, b_ref, xh_ref, rs_ref, hb_ref):
    mu = jnp.mean(z, axis=-1, keepdims=True)
    zc = z - mu
    var = jnp.mean(zc * zc, axis=-1, keepdims=True)
    rstd = lax.rsqrt(var + LN_EPS)
    xh = zc * rstd
    xh_ref[...] = xh
    rs_ref[...] = rstd
    hb_ref[...] = (xh * g_ref[...] + b_ref[...]).astype(BF)


def _ln_out(tp, tm=TM):
    specs = [pl.BlockSpec((tm, D_MODEL), lambda i: (i, 0)), pl.BlockSpec((tm, 1), lambda i: (i, 0)),
             pl.BlockSpec((tm, D_MODEL), lambda i: (i, 0))]
    shapes = [_sds((tp, D_MODEL), F32), _sds((tp, 1), F32), _sds((tp, D_MODEL), BF)]
    return specs, shapes


def _row_vec(n):
    return pl.BlockSpec((1, n), lambda i: (0, 0))


def ffn_up(hb, wup, *, name, side=None):
    tp = hb.shape[0]
    tn = D_FF // 2
    nj = D_FF // tn

    def body(in_refs, out_refs, scratch):
        h_ref, wg_ref, wu_ref = in_refs
        g_ref, u_ref, a_ref = out_refs
        h = h_ref[...]
        g = jnp.dot(h, wg_ref[0], preferred_element_type=F32)
        u = jnp.dot(h, wu_ref[0], preferred_element_type=F32)
        g_ref[...] = g.astype(BF)
        u_ref[...] = u.astype(BF)
        a_ref[...] = (g * jax.nn.sigmoid(g) * u).astype(BF)

    blk = pl.BlockSpec((TM, tn), lambda i, j: (i, j))
    return _side_call(
        body, side, name=name, grid=(tp // TM, nj),
        in_specs=[pl.BlockSpec((TM, D_MODEL), lambda i, j: (i, 0)), pl.BlockSpec((1, D_MODEL, tn), lambda i, j: (j, 0, 0)),
                  pl.BlockSpec((1, D_MODEL, tn), lambda i, j: (j + nj, 0, 0))],
        out_specs=[blk, blk, blk], out_shape=[_sds((tp, D_FF), BF)] * 3, scratch_shapes=[], args=(hb, wup, wup),
    )


def down_ln(a, wd, xprev, gp, bp, g, b, *, name):
    tp = a.shape[0]

    def body(a_ref, wd_ref, xp_ref, gp_ref, bp_ref, g_ref, b_ref, xh_ref, rs_ref, hb_ref):
        wd = jnp.concatenate([wd_ref[j] for j in range(4)], axis=0)
        f = jnp.dot(a_ref[...], wd, preferred_element_type=F32)
        hprev = xp_ref[...] * gp_ref[...] + bp_ref[...]
        _ln_store(ALPHA * hprev + 0.5 * f, g_ref, b_ref, xh_ref, rs_ref, hb_ref)

    out_specs, out_shape = _ln_out(tp)
    return _pcall(
        body, name=name, grid=(tp // TM,),
        in_specs=[pl.BlockSpec((TM, D_FF), lambda i: (i, 0)), pl.BlockSpec((4, D_FF // 4, D_MODEL), lambda i: (0, 0, 0)),
                  pl.BlockSpec((TM, D_MODEL), lambda i: (i, 0))] + [_row_vec(D_MODEL)] * 4,
        out_specs=out_specs, out_shape=out_shape, compiler_params=_params(1),
    )(a, wd, xprev, gp, bp, g, b)


def _rope(x, c, s1, s2, reps):
    n = x.shape[1]
    if reps > 1:
        c, s1, s2 = (jnp.tile(t, (1, reps)) for t in (c, s1, s2))
    return x * c + pltpu.roll(x, 16, 1) * s1 + pltpu.roll(x, n - 16, 1) * s2


def _rms(x, g):
    r = lax.rsqrt(jnp.mean(x * x, axis=-1, keepdims=True) + RMS_EPS)
    return x * r * g, r


def mla_prep(p, gq, gkv, wuq_p, wukv, tabs, *, name):
    tp = p.shape[0]
    nh = MLA_HEADS

    def body(cq_ref, ckv_ref, kr_ref, gq_ref, gkv_ref, wuq_ref, wukv_ref, cq_t, s1q_t, s2q_t, ck_t, s1k_t, s2k_t,
             cqn_ref, ckvn_ref, q2_ref, kv_ref, krr_ref):
        cqn, _ = _rms(cq_ref[...].astype(F32), gq_ref[...])
        ckvn, _ = _rms(ckv_ref[...].astype(F32), gkv_ref[...])
        cqn = cqn.astype(BF)
        ckvn = ckvn.astype(BF)
        cqn_ref[...] = cqn
        ckvn_ref[...] = ckvn
        q = jnp.dot(cqn, wuq_ref[...], preferred_element_type=F32)
        q2_ref[...] = _rope(q, cq_t[...], s1q_t[...], s2q_t[...], nh).astype(BF)
        kv_ref[...] = jnp.dot(ckvn, wukv_ref[...], preferred_element_type=F32).astype(BF)
        krr_ref[...] = _rope(kr_ref[...].astype(F32), ck_t[...], s1k_t[...], s2k_t[...], 1).astype(BF)

    def rows(n, col=0):
        return pl.BlockSpec((TMH, n), functools.partial(lambda i, col: (i, col), col=col))

    return _pcall(
        body, name=name, grid=(tp // TMH,),
        in_specs=[rows(Q_LORA, 1536 // Q_LORA), rows(KV_LORA, 1792 // KV_LORA), rows(LANES, 1920 // LANES),
                  _row_vec(Q_LORA), _row_vec(KV_LORA),
                  pl.BlockSpec((Q_LORA, nh * Q_PAD), lambda i: (0, 0)), pl.BlockSpec((KV_LORA, nh * KV_PAD), lambda i: (0, 0)),
                  rows(Q_PAD), rows(Q_PAD), rows(Q_PAD), rows(LANES), rows(LANES), rows(LANES)],
        out_specs=[rows(Q_LORA), rows(KV_LORA), rows(nh * Q_PAD), rows(nh * KV_PAD), rows(LANES)],
        out_shape=[_sds((tp, Q_LORA), BF), _sds((tp, KV_LORA), BF), _sds((tp, nh * Q_PAD), BF),
                   _sds((tp, nh * KV_PAD), BF), _sds((tp, LANES), BF)],
        compiler_params=_params(1),
    )(p, p, p, gq, gkv, wuq_p, wukv, *tabs)


def _causal_mask(s):
    qpos = lax.broadcasted_iota(jnp.int32, (TM, TM), 0)
    kpos = lax.broadcasted_iota(jnp.int32, (TM, TM), 1)
    return jnp.where(kpos <= qpos, s, NEG_INF)


def _key_rows(k):
    return pl.ds(pl.multiple_of(k * TM, TM), TM)


def _pipelined_key_blocks(n, prefetch, process):
    prefetch(0, 0)

    def pair(j, carry):
        prefetch(2 * j + 1, 1)
        process(2 * j, 0, False)
        prefetch(2 * j + 2, 0)
        process(2 * j + 1, 1, False)
        return carry

    lax.fori_loop(0, n // 2, pair, 0)

    @pl.when(n % 2 == 1)
    def _():
        prefetch(n, 1)
        process(n - 1, 0, False)
        process(n, 1, True)

    @pl.when(n % 2 == 0)
    def _():
        process(n, 0, True)


def _side_call(body_main, side, *, name, grid, in_specs, out_specs, out_shape, scratch_shapes, args):
    n_in, n_out, n_scr = len(in_specs), len(out_specs), len(scratch_shapes)
    s_in, s_pre, n_sems, program = side if side is not None else ((), (), 0, None)
    a, b = len(s_in), len(s_pre)

    def body(*refs):
        in_refs = refs[:n_in]
        out_refs = refs[n_in + a + b:n_in + a + b + n_out]
        scr = refs[n_in + a + 2 * b + n_out:n_in + a + 2 * b + n_out + n_scr]
        if side is not None:
            side_in = refs[n_in:n_in + a]
            side_out = refs[n_in + a + b + n_out:n_in + a + 2 * b + n_out]
            start, finish = program(side_in, side_out, refs[-2], refs[-1])

            @pl.when((pl.program_id(0) == 0) & (pl.program_id(1) == 0))
            def _():
                start()

        body_main(in_refs, out_refs, scr)
        if side is not None:
            @pl.when((pl.program_id(0) == grid[0] - 1) & (pl.program_id(1) == grid[1] - 1))
            def _():
                finish()

    sems = [pltpu.SemaphoreType.DMA((n_sems,))] * 2 if side is not None else []
    return _pcall(
        body, name=name, grid=grid, in_specs=list(in_specs) + [HBM_SPEC] * (a + b), out_specs=list(out_specs) + [HBM_SPEC] * b,
        out_shape=list(out_shape) + [_sds(p.shape, p.dtype) for p in s_pre],
        input_output_aliases={n_in + a + i: n_out + i for i in range(b)},
        scratch_shapes=list(scratch_shapes) + sems, compiler_params=_params(2),
    )(*args, *s_in, *s_pre)


def attn_fwd(q2, kv, kr, *, name, side=None):
    tp = q2.shape[0]
    nh = MLA_HEADS
    nb = tp // TM
    rep = TM // LANES

    def body(in_refs, out_refs, scratch):
        q_ref, kv_ref, kr_ref = in_refs
        o_ref, lse_ref = out_refs
        m_ref, l_ref, acc_ref, s0_ref, s1_ref, p_ref, alpha_ref = scratch
        qi = pl.program_id(1)
        s_refs = (s0_ref, s1_ref)
        m_ref[...] = jnp.full_like(m_ref, NEG_INF)
        l_ref[...] = jnp.zeros_like(l_ref)
        acc_ref[...] = jnp.zeros_like(acc_ref)

        def prefetch(k, slot):
            k2 = jnp.concatenate([kv_ref[_key_rows(k), :], kr_ref[_key_rows(k), :]], axis=1)
            s_refs[slot][...] = lax.dot_general(q_ref[...], k2, NT, preferred_element_type=F32)

        def process(k, slot, diagonal):
            for r in range(TM // LANES):
                rows = slice(r * LANES, (r + 1) * LANES)
                s = s_refs[slot][rows, :]
                if diagonal:
                    qpos = r * LANES + lax.broadcasted_iota(jnp.int32, (LANES, TM), 0)
                    s = jnp.where(lax.broadcasted_iota(jnp.int32, (LANES, TM), 1) <= qpos, s, NEG_INF)
                m_prev = m_ref[rows, :]
                m_new = jnp.maximum(m_prev, jnp.max(s, axis=1, keepdims=True))
                alpha = jnp.exp2(m_prev - m_new)
                p = jnp.exp2(s - jnp.tile(m_new, (1, rep)))
                lane_sums = p[:, 0:LANES]
                for t in range(1, rep):
                    lane_sums = lane_sums + p[:, t * LANES:(t + 1) * LANES]
                l_ref[rows, :] = alpha * l_ref[rows, :] + lane_sums
                p_ref[rows, :] = p.astype(BF)
                alpha_ref[rows, :] = alpha
                m_ref[rows, :] = m_new
            acc_ref[...] = alpha_ref[...] * acc_ref[...] + jnp.dot(p_ref[...], kv_ref[_key_rows(k), :], preferred_element_type=F32)

        _pipelined_key_blocks(qi, prefetch, process)
        l = jnp.sum(l_ref[...], axis=1, keepdims=True)
        o_ref[...] = (acc_ref[...] / l).astype(BF)
        lse_ref[...] = m_ref[...] + jnp.log2(l)

    return _side_call(
        body, side, name=name, grid=(nh, nb),
        in_specs=[pl.BlockSpec((TM, Q_PAD), lambda h, qi: (qi, h)), pl.BlockSpec((tp, KV_PAD), lambda h, qi: (0, h)),
                  pl.BlockSpec((tp, LANES), lambda h, qi: (0, 0))],
        out_specs=[pl.BlockSpec((TM, KV_PAD), lambda h, qi: (qi, h)), pl.BlockSpec((TM, LANES), lambda h, qi: (qi, h))],
        out_shape=[_sds((tp, nh * KV_PAD), BF), _sds((tp, nh * LANES), F32)],
        scratch_shapes=[pltpu.VMEM((TM, LANES), F32)] * 3 + [pltpu.VMEM((TM, TM), F32)] * 2
        + [pltpu.VMEM((TM, TM), BF), pltpu.VMEM((TM, LANES), F32)], args=(q2, kv, kr),
    )


def conv_fwd(p, w, *, name):
    tp = p.shape[0]

    def body(b_ref, c_ref, h_ref, w_ref, y_ref, cv_ref, ebuf):
        i = pl.program_id(0)

        @pl.when(i == 0)
        def _():
            ebuf[0:8, :] = jnp.zeros((8, D_CONV), F32)

        e = c_ref[...].astype(F32) * h_ref[...].astype(F32)
        ebuf[8:8 + TM, :] = e
        w_all = w_ref[...]
        conv = w_all[0:1] * ebuf[pl.ds(6, TM), :] + w_all[1:2] * ebuf[pl.ds(7, TM), :] + w_all[2:3] * e
        cv_ref[...] = conv.astype(BF)
        y_ref[...] = (b_ref[...].astype(F32) * conv).astype(BF)
        ebuf[0:8, :] = ebuf[TM:TM + 8, :]

    def col(j):
        return pl.BlockSpec((TM, D_CONV), functools.partial(lambda i, j: (i, j), j=j))

    return _pcall(
        body, name=name, grid=(tp // TM,),
        in_specs=[col(0), col(1), col(2), pl.BlockSpec((3, D_CONV), lambda i: (0, 0))],
        out_specs=[col(0), col(0)], out_shape=[_sds((tp, D_CONV), BF)] * 2,
        scratch_shapes=[pltpu.VMEM((TM + 8, D_CONV), F32)], compiler_params=_params(1),
    )(p, p, p, w)


def merge_out_ln(ycv, o2, p, bg, wbc, wbm_p, wo, xprev, gp, bp, g, b, *, name):
    tp = ycv.shape[0]

    def body(y_ref, o_ref, gc_ref, gm_ref, bg_ref, wbc_ref, wbm_ref, wo_ref, xp_ref, gp_ref, bp_ref, g_ref, b_ref,
             bc_ref, bm_ref, mg_ref, xh_ref, rs_ref, hb_ref):
        bc = jnp.dot(y_ref[...], wbc_ref[...], preferred_element_type=F32)
        bm = jnp.dot(o_ref[...], wbm_ref[...], preferred_element_type=F32)
        bgv = bg_ref[...]
        mg = (jax.nn.sigmoid(gc_ref[...].astype(F32) + bgv[0:1]) * bc
              + jax.nn.sigmoid(gm_ref[...].astype(F32) + bgv[1:2]) * bm)
        mgb = mg.astype(BF)
        bc_ref[...] = bc.astype(BF)
        bm_ref[...] = bm.astype(BF)
        mg_ref[...] = mgb
        mix = jnp.dot(mgb, wo_ref[...], preferred_element_type=F32)
        hprev = xp_ref[...] * gp_ref[...] + bp_ref[...]
        _ln_store(ALPHA * hprev + mix, g_ref, b_ref, xh_ref, rs_ref, hb_ref)

    def rows(n, col=0):
        return pl.BlockSpec((TMH, n), functools.partial(lambda i, col: (i, col), col=col))

    def whole(r, c):
        return pl.BlockSpec((r, c), lambda i: (0, 0))

    ln_specs, ln_shapes = _ln_out(tp, TMH)
    return _pcall(
        body, name=name, grid=(tp // TMH,),
        in_specs=[rows(D_CONV), rows(MLA_HEADS * KV_PAD), rows(D_MODEL, 2), rows(D_MODEL, 3), whole(2, D_MODEL),
                  whole(D_CONV, D_MODEL), whole(MLA_HEADS * KV_PAD, D_MODEL), whole(D_MODEL, D_MODEL), rows(D_MODEL)]
        + [_row_vec(D_MODEL)] * 4,
        out_specs=[rows(D_MODEL)] * 3 + ln_specs, out_shape=[_sds((tp, D_MODEL), BF)] * 3 + ln_shapes,
        compiler_params=_params(1),
    )(ycv, o2, p, p, bg, wbc, wbm_p, wo, xprev, gp, bp, g, b)


def loss_grad(xh, g, b, target_p, n_real, *, name):
    tp = xh.shape[0]

    def body(x_ref, g_ref, b_ref, t_ref, dy_ref, loss_ref):
        i = pl.program_id(0)

        @pl.when(i == 0)
        def _():
            loss_ref[...] = jnp.zeros_like(loss_ref)

        row = i * TM + lax.broadcasted_iota(jnp.int32, (TM, 1), 0)
        real = (row >= N_META) & (row < N_META + n_real)
        diff = jnp.where(real, x_ref[...] * g_ref[...] + b_ref[...] - t_ref[...], 0.0)
        dy_ref[...] = diff * (1.0 / D_MODEL)
        loss_ref[...] += 0.5 / D_MODEL * jnp.sum(diff * diff)

    return _pcall(
        body, name=name, grid=(tp // TM,),
        in_specs=[pl.BlockSpec((TM, D_MODEL), lambda i: (i, 0)), _row_vec(D_MODEL), _row_vec(D_MODEL),
                  pl.BlockSpec((TM, D_MODEL), lambda i: (i, 0))],
        out_specs=[pl.BlockSpec((TM, D_MODEL), lambda i: (i, 0)), pl.BlockSpec((8, LANES), lambda i: (0, 0))],
        out_shape=[_sds((tp, D_MODEL), F32), _sds((8, LANES), F32)], compiler_params=_params(1),
    )(xh, g, b, target_p)


def ln_bwd(dh, xh, rstd, g, *, branch_scale, name):
    tp = dh.shape[0]

    def body(dh_ref, xh_ref, rs_ref, g_ref, dz_ref, dzb_ref, dg_ref, db_ref):
        i = pl.program_id(0)

        @pl.when(i == 0)
        def _():
            dg_ref[...] = jnp.zeros_like(dg_ref)
            db_ref[...] = jnp.zeros_like(db_ref)

        dy = dh_ref[...]
        xhat = xh_ref[...]
        dg_ref[...] += jnp.sum(dy * xhat, axis=0, keepdims=True)
        db_ref[...] += jnp.sum(dy, axis=0, keepdims=True)
        dxh = dy * g_ref[...]
        m1 = jnp.mean(dxh, axis=-1, keepdims=True)
        m2 = jnp.mean(dxh * xhat, axis=-1, keepdims=True)
        dz = rs_ref[...] * (dxh - m1 - xhat * m2)
        dz_ref[...] = dz
        dzb_ref[...] = (branch_scale * dz).astype(BF)

    rows = pl.BlockSpec((TM, D_MODEL), lambda i: (i, 0))
    return _pcall(
        body, name=name, grid=(tp // TM,),
        in_specs=[rows, rows, pl.BlockSpec((TM, 1), lambda i: (i, 0)), _row_vec(D_MODEL)],
        out_specs=[rows, rows, _row_vec(D_MODEL), _row_vec(D_MODEL)],
        out_shape=[_sds((tp, D_MODEL), F32), _sds((tp, D_MODEL), BF), _sds((1, D_MODEL), F32), _sds((1, D_MODEL), F32)],
        compiler_params=_params(1),
    )(dh, xh, rstd, g)


def ffn_down_bwd(dzb, wd, gate, up, *, name, side=None):
    tp = dzb.shape[0]
    tn = D_FF // 2

    def body(in_refs, out_refs, scratch):
        dz_ref, wd_ref, g_ref, u_ref = in_refs
        dg_ref, du_ref = out_refs
        wd = jnp.concatenate([wd_ref[0], wd_ref[1]], axis=0)
        da = lax.dot_general(dz_ref[...], wd, NT, preferred_element_type=F32)
        g = g_ref[...].astype(F32)
        u = u_ref[...].astype(F32)
        sg = jax.nn.sigmoid(g)
        dg_ref[...] = (da * u * sg * (1.0 + g * (1.0 - sg))).astype(BF)
        du_ref[...] = (da * g * sg).astype(BF)

    blk = pl.BlockSpec((TM, tn), lambda i, j: (i, j))
    return _side_call(
        body, side, name=name, grid=(tp // TM, D_FF // tn),
        in_specs=[pl.BlockSpec((TM, D_MODEL), lambda i, j: (i, 0)), pl.BlockSpec((2, tn // 2, D_MODEL), lambda i, j: (j, 0, 0)), blk, blk],
        out_specs=[blk, blk], out_shape=[_sds((tp, D_FF), BF)] * 2, scratch_shapes=[], args=(dzb, wd, gate, up),
    )


def merge_bwd(dzb, wo, bc, bm, p, bg, wbc, wbm_p, o2, *, name):
    tp = dzb.shape[0]
    nh = MLA_HEADS

    def body(dz_ref, wo_ref, bc_ref, bm_ref, gc_ref, gm_ref, bg_ref, wbc_ref, wbm_ref, o_ref,
             dbc_ref, dbm_ref, dgg_ref, dy_ref, do_ref, dl_ref, dbg_ref):
        i = pl.program_id(0)

        @pl.when(i == 0)
        def _():
            dbg_ref[...] = jnp.zeros_like(dbg_ref)

        dmg = lax.dot_general(dz_ref[...], wo_ref[...], NT, preferred_element_type=F32)
        bgv = bg_ref[...]
        sc = jax.nn.sigmoid(gc_ref[...].astype(F32) + bgv[0:1])
        sm = jax.nn.sigmoid(gm_ref[...].astype(F32) + bgv[1:2])
        dbc = (dmg * sc).astype(BF)
        dbm = (dmg * sm).astype(BF)
        dgc = dmg * bc_ref[...].astype(F32) * sc * (1.0 - sc)
        dgm = dmg * bm_ref[...].astype(F32) * sm * (1.0 - sm)
        dbc_ref[...] = dbc
        dbm_ref[...] = dbm
        dgg_ref[...] = jnp.concatenate([dgc, dgm], axis=1).astype(BF)
        dbg_ref[...] += jnp.concatenate([jnp.sum(dgc, axis=0, keepdims=True), jnp.sum(dgm, axis=0, keepdims=True)], axis=0)
        dy_ref[...] = lax.dot_general(dbc, wbc_ref[...], NT, preferred_element_type=F32)
        do = lax.dot_general(dbm, wbm_ref[...], NT, preferred_element_type=F32)
        do_ref[...] = do.astype(BF)
        prod = do * o_ref[...].astype(F32)
        parts = []
        for h in range(nh):
            d = jnp.sum(prod[:, h * KV_PAD:(h + 1) * KV_PAD], axis=1, keepdims=True)
            parts.append(jnp.broadcast_to(d, (TMH, LANES)))
        dl_ref[...] = jnp.concatenate(parts, axis=1)

    def rows(n, col=0):
        return pl.BlockSpec((TMH, n), functools.partial(lambda i, col: (i, col), col=col))

    def whole(r, c):
        return pl.BlockSpec((r, c), lambda i: (0, 0))

    return _pcall(
        body, name=name, grid=(tp // TMH,),
        in_specs=[rows(D_MODEL), whole(D_MODEL, D_MODEL), rows(D_MODEL), rows(D_MODEL), rows(D_MODEL, 2), rows(D_MODEL, 3),
                  whole(2, D_MODEL), whole(D_CONV, D_MODEL), whole(nh * KV_PAD, D_MODEL), rows(nh * KV_PAD)],
        out_specs=[rows(D_MODEL), rows(D_MODEL), rows(2 * D_MODEL, 1), rows(D_CONV), rows(nh * KV_PAD), rows(nh * LANES),
                   whole(2, D_MODEL)],
        out_shape=[_sds((tp, D_MODEL), BF), _sds((tp, D_MODEL), BF), _sds((tp, D_IN_PAD), BF), _sds((tp, D_CONV), F32),
                   _sds((tp, nh * KV_PAD), BF), _sds((tp, nh * LANES), F32), _sds((2, D_MODEL), F32)],
        compiler_params=_params(1),
    )(dzb, wo, bc, bm, p, p, bg, wbc, wbm_p, o2)


def attn_bwd(q2, kv, kr, do2, lse, dl, *, name, side=None):
    tp = q2.shape[0]
    nh = MLA_HEADS
    nb = tp // TM
    rep = TM // LANES

    def body(in_refs, out_refs, scratch):
        q_ref, kv_ref, kr_ref, do_ref, lse_ref, dl_ref = in_refs
        dq_ref, dkv_ref, dkr_ref = out_refs
        dq_acc, s0_ref, s1_ref, dp0_ref, dp1_ref = scratch
        qi = pl.program_id(1)
        s_refs, dp_refs = (s0_ref, s1_ref), (dp0_ref, dp1_ref)

        @pl.when(qi == 0)
        def _():
            dkv_ref[...] = jnp.zeros_like(dkv_ref)
            dkr_ref[...] = jnp.zeros_like(dkr_ref)

        dq_acc[...] = jnp.zeros_like(dq_acc)

        def prefetch(k, slot):
            kvb = kv_ref[_key_rows(k), :]
            k2 = jnp.concatenate([kvb, kr_ref[_key_rows(k), :]], axis=1)
            s_refs[slot][...] = lax.dot_general(q_ref[...], k2, NT, preferred_element_type=F32)
            dp_refs[slot][...] = lax.dot_general(do_ref[...], kvb, NT, preferred_element_type=F32)

        def process(k, slot, diagonal):
            rows = _key_rows(k)
            s = s_refs[slot][...]
            if diagonal:
                s = _causal_mask(s)
            p = jnp.exp2(s - jnp.tile(lse_ref[...], (1, rep)))
            dsb = (p * (dp_refs[slot][...] - jnp.tile(dl_ref[...], (1, rep)))).astype(BF)
            dk2 = lax.dot_general(dsb, q_ref[...], TN, preferred_element_type=F32) * LN2
            dkv_ref[rows, :] += lax.dot_general(p.astype(BF), do_ref[...], TN, preferred_element_type=F32) + dk2[:, :KV_PAD]
            dkr_ref[rows, :] += dk2[:, KV_PAD:KV_PAD + LANES]
            k2 = jnp.concatenate([kv_ref[rows, :], kr_ref[rows, :]], axis=1)
            dq_acc[...] += jnp.dot(dsb, k2, preferred_element_type=F32)

        _pipelined_key_blocks(qi, prefetch, process)
        dq_ref[...] = dq_acc[...]

    def qrow(n):
        return pl.BlockSpec((TM, n), lambda h, qi: (qi, h))

    def head(n):
        return pl.BlockSpec((tp, n), lambda h, qi: (0, h))

    return _side_call(
        body, side, name=name, grid=(nh, nb),
        in_specs=[qrow(Q_PAD), head(KV_PAD), pl.BlockSpec((tp, LANES), lambda h, qi: (0, 0)), qrow(KV_PAD), qrow(LANES), qrow(LANES)],
        out_specs=[qrow(Q_PAD), head(KV_PAD), head(LANES)],
        out_shape=[_sds((tp, nh * Q_PAD), F32), _sds((tp, nh * KV_PAD), F32), _sds((tp, nh * LANES), F32)],
        scratch_shapes=[pltpu.VMEM((TM, Q_PAD), F32)] + [pltpu.VMEM((TM, TM), F32)] * 4, args=(q2, kv, kr, do2, lse, dl),
    )


def _rms_bwd(x, g, dy):
    r = lax.rsqrt(jnp.mean(x * x, axis=-1, keepdims=True) + RMS_EPS)
    gy = dy * g
    dx = r * gy - x * (r * r * r) * jnp.mean(x * gy, axis=-1, keepdims=True)
    return dx, jnp.sum(dy * x * r, axis=0, keepdims=True)


def mla_prep_bwd(dq2, dkv, dkr, p, gq, gkv, wuq_p, wukv, tabs_bwd, dp, *, name):
    tp = dq2.shape[0]
    nh = MLA_HEADS

    def body(dq_ref, dkv_ref, dkr_ref, cq_ref, ckv_ref, gq_ref, gkv_ref, wuq_ref, wukv_ref,
             cq_t, s1q_t, s2q_t, ck_t, s1k_t, s2k_t, dp_in_ref, dqb_ref, dsm_ref, dgq_ref, dgkv_ref):
        i = pl.program_id(0)

        @pl.when(i == 0)
        def _():
            dgq_ref[...] = jnp.zeros_like(dgq_ref)
            dgkv_ref[...] = jnp.zeros_like(dgkv_ref)

        dqb = _rope(dq_ref[...], cq_t[...], s1q_t[...], s2q_t[...], nh).astype(BF)
        dqb_ref[...] = dqb
        dcqn = lax.dot_general(dqb, wuq_ref[...], NT, preferred_element_type=F32)
        dcq, dgq = _rms_bwd(cq_ref[...].astype(F32), gq_ref[...], dcqn)
        dckvn = lax.dot_general(dkv_ref[...].astype(BF), wukv_ref[...], NT, preferred_element_type=F32)
        dckv, dgkv = _rms_bwd(ckv_ref[...].astype(F32), gkv_ref[...], dckvn)
        dkr_heads = dkr_ref[...]
        dkr_sum = dkr_heads[:, :LANES]
        for h in range(1, nh):
            dkr_sum = dkr_sum + dkr_heads[:, h * LANES:(h + 1) * LANES]
        dkr = _rope(dkr_sum, ck_t[...], s1k_t[...], s2k_t[...], 1)
        dsm_ref[...] = jnp.concatenate([dcq, dckv, dkr], axis=1).astype(BF)
        dgq_ref[...] += dgq
        dgkv_ref[...] += dgkv

    def rows(n, col=0):
        return pl.BlockSpec((TMH, n), functools.partial(lambda i, col: (i, col), col=col))

    return _pcall(
        body, name=name, grid=(tp // TMH,),
        in_specs=[rows(nh * Q_PAD), rows(nh * KV_PAD), rows(nh * LANES), rows(Q_LORA, 1536 // Q_LORA), rows(KV_LORA, 1792 // KV_LORA),
                  _row_vec(Q_LORA), _row_vec(KV_LORA),
                  pl.BlockSpec((Q_LORA, nh * Q_PAD), lambda i: (0, 0)), pl.BlockSpec((KV_LORA, nh * KV_PAD), lambda i: (0, 0)),
                  rows(Q_PAD), rows(Q_PAD), rows(Q_PAD), rows(LANES), rows(LANES), rows(LANES), pl.BlockSpec(memory_space=pl.ANY)],
        out_specs=[rows(nh * Q_PAD), rows(Q_LORA + KV_LORA + LANES, 1536 // (Q_LORA + KV_LORA + LANES)), _row_vec(Q_LORA),
                   _row_vec(KV_LORA)],
        out_shape=[_sds((tp, nh * Q_PAD), BF), _sds(dp.shape, dp.dtype), _sds((1, Q_LORA), F32), _sds((1, KV_LORA), F32)],
        input_output_aliases={15: 1}, compiler_params=_params(1),
    )(dq2, dkv, dkr, p, p, gq, gkv, wuq_p, wukv, *tabs_bwd, dp)


def conv_bwd(dy, p, conv, w, dp, *, name):
    tp = dy.shape[0]
    nb = tp // TM

    def body(dy_ref, b_ref, c_ref, h_ref, cv_ref, w_ref, dp_in_ref, dp_ref, dw0_ref, dw1_ref, dw2_ref, dbuf):
        i = pl.program_id(0)

        @pl.when(i == 0)
        def _():
            dbuf[TM:TM + 8, :] = jnp.zeros((8, D_CONV), F32)
            dw0_ref[...] = jnp.zeros_like(dw0_ref)
            dw1_ref[...] = jnp.zeros_like(dw1_ref)
            dw2_ref[...] = jnp.zeros_like(dw2_ref)

        dyv = dy_ref[...]
        c = c_ref[...].astype(F32)
        hh = h_ref[...].astype(F32)
        dconv = dyv * b_ref[...].astype(F32)
        dbuf[0:TM, :] = dconv
        d1 = dbuf[pl.ds(1, TM), :]
        d2 = dbuf[pl.ds(2, TM), :]
        w_all = w_ref[...]
        de = w_all[2:3] * dconv + w_all[1:2] * d1 + w_all[0:1] * d2
        e = c * hh
        dp_ref[...] = jnp.concatenate([dyv * cv_ref[...].astype(F32), de * hh, de * c], axis=1).astype(BF)
        dw0_ref[...] += jnp.sum(d2 * e, axis=0, keepdims=True)
        dw1_ref[...] += jnp.sum(d1 * e, axis=0, keepdims=True)
        dw2_ref[...] += jnp.sum(dconv * e, axis=0, keepdims=True)
        dbuf[TM:TM + 8, :] = dbuf[0:8, :]

    def col(j):
        return pl.BlockSpec((TM, D_CONV), functools.partial(lambda i, j: (nb - 1 - i, j), j=j))

    return _pcall(
        body, name=name, grid=(nb,),
        in_specs=[col(0), col(0), col(1), col(2), col(0), pl.BlockSpec((3, D_CONV), lambda i: (0, 0)),
                  pl.BlockSpec(memory_space=pl.ANY)],
        out_specs=[pl.BlockSpec((TM, 3 * D_CONV), lambda i: (nb - 1 - i, 0))] + [_row_vec(D_CONV)] * 3,
        out_shape=[_sds(dp.shape, dp.dtype)] + [_sds((1, D_CONV), F32)] * 3, input_output_aliases={6: 0},
        scratch_shapes=[pltpu.VMEM((TM + 8, D_CONV), F32)], compiler_params=_params(1),
    )(dy, p, p, p, conv, w, dp)


def adamw(w, g, m, v, *, name):
    r, c = w.shape
    tr = r
    for cand in (256, 128, 64, 32, 16, 8):
        if r % cand == 0 and r > cand:
            tr = cand
            break

    def body(w_ref, g_ref, m_ref, v_ref, d_ref, nm_ref, nv_ref):
        gv = g_ref[...]
        nm = ADAM_B1 * m_ref[...] + (1.0 - ADAM_B1) * gv
        nv = ADAM_B2 * v_ref[...] + (1.0 - ADAM_B2) * (gv * gv)
        m_hat = nm / (1.0 - ADAM_B1 ** ADAM_STEP)
        v_hat = nv / (1.0 - ADAM_B2 ** ADAM_STEP)
        d_ref[...] = -ADAM_LR * (m_hat / (jnp.sqrt(v_hat) + ADAM_EPS) + ADAM_WD * w_ref[...])
        nm_ref[...] = nm
        nv_ref[...] = nv

    blk = pl.BlockSpec((tr, c), lambda i: (i, 0))
    return _pcall(
        body, name=name, grid=(r // tr,), in_specs=[blk] * 4, out_specs=[blk] * 3,
        out_shape=[_sds((r, c), F32)] * 3, compiler_params=_params(1),
    )(w, g, m, v)


HBM_SPEC = pl.BlockSpec(memory_space=pltpu.HBM)


def _place():
    return lax.axis_index("x"), lax.axis_index("y"), lax.axis_index("c")


def _other_chips(x, y):
    return [(1 - x, y), (x, 1 - y), (1 - x, 1 - y)]


def _half(ref_or_shape_rows, c):
    return pl.ds(c * (ref_or_shape_rows // 2), ref_or_shape_rows // 2)


def gather_side(items):
    n = len(items)
    shards = [s for s, _ in items]
    layers = [l for _, l in items]

    def program(x_refs, o_refs, send_sems, recv_sems):
        x, y, c = _place()
        me = 2 * x + y
        chips = _other_chips(x, y)

        def copy(sem, src, dst, to):
            return pltpu.make_async_remote_copy(src_ref=src, dst_ref=dst, send_sem=send_sems.at[sem], recv_sem=recv_sems.at[sem],
                                                device_id=to, device_id_type=MESH)

        def src(i):
            return x_refs[i].at[layers[i], _half(x_refs[i].shape[1], c)]

        def dst(i, slot, cc):
            return o_refs[i].at[slot, _half(o_refs[i].shape[1], cc)]

        sends = [copy(6 * i + k, src(i), dst(i, me, c), (px, py, c)) for i in range(n) for k, (px, py) in enumerate(chips)]
        passed = [copy(6 * i + 3 + k, dst(i, 2 * px + py, c), dst(i, 2 * px + py, c), (x, y, 1 - c))
                  for k, (px, py) in enumerate(chips) for i in range(n)]

        def start():
            for cp in sends:
                cp.start()

        def finish():
            pos = 0
            for k, (px, py) in enumerate(chips):
                for i in range(n):
                    copy(6 * i + k, src(i), dst(i, 2 * px + py, c), (px, py, c)).wait_recv()
                    passed[pos].start()
                    pos += 1
            for k, (px, py) in enumerate(chips):
                for i in range(n):
                    copy(6 * i + 3 + k, dst(i, 2 * px + py, 1 - c), dst(i, 2 * px + py, 1 - c), (x, y, 1 - c)).wait_recv()
            for cp in sends + passed:
                cp.wait_send()

        return start, finish

    prefilled = [jnp.broadcast_to(s[l][None], (4,) + s.shape[1:]) for s, l in items]
    return shards, prefilled, 6 * n, program


def scatter_side(pss):
    n = len(pss)

    def program(p_refs, o_refs, send_sems, recv_sems):
        x, y, c = _place()
        me = 2 * x + y
        chips = _other_chips(x, y)

        def copy(i, k, j_src, j_dst, to):
            return pltpu.make_async_remote_copy(src_ref=p_refs[i].at[j_src], dst_ref=o_refs[i].at[j_dst],
                                                send_sem=send_sems.at[3 * i + k], recv_sem=recv_sems.at[3 * i + k],
                                                device_id=to, device_id_type=MESH)

        sends = [copy(i, k, 2 * px + py, me, (px, py, c)) for i in range(n) for k, (px, py) in enumerate(chips)]

        def start():
            for cp in sends:
                cp.start()

        def finish():
            for i in range(n):
                for k, (px, py) in enumerate(chips):
                    copy(i, k, me, 2 * px + py, (px, py, c)).wait_recv()
            for cp in sends:
                cp.wait_send()

        return start, finish

    xi, yi, _ = _place()
    own = jnp.arange(4)[:, None, None] == 2 * xi + yi
    prefilled = [jnp.where(own, p, jnp.zeros_like(p)) for p in pss]
    return list(pss), prefilled, 3 * n, program


def exchange_alone(side, *, name):
    inputs, prefilled, n_sems, program = side
    a, b = len(inputs), len(prefilled)

    def body(*refs):
        start, finish = program(refs[:a], refs[a + b:a + 2 * b], refs[-2], refs[-1])
        start()
        finish()

    return _pcall(
        body, name=name, in_specs=[HBM_SPEC] * (a + b), out_specs=[HBM_SPEC] * b, out_shape=[_sds(p.shape, p.dtype) for p in prefilled],
        input_output_aliases={a + i: i for i in range(b)}, scratch_shapes=[pltpu.SemaphoreType.DMA((n_sems,))] * 2,
    )(*inputs, *prefilled)


def pair_exchange(gs, *, name):
    n = len(gs)

    def body(*refs):
        g_refs, o_refs = refs[:n], refs[n:2 * n]
        send_sems, recv_sems = refs[2 * n:]
        x, y, c = _place()
        cps = [pltpu.make_async_remote_copy(src_ref=g_refs[i].at[:, _half(g_refs[i].shape[1], 1 - c)], dst_ref=o_refs[i],
                                            send_sem=send_sems.at[i], recv_sem=recv_sems.at[i], device_id=(x, y, 1 - c),
                                            device_id_type=MESH)
               for i in range(n)]
        for cp in cps:
            cp.start()
        for cp in cps:
            cp.wait()

    return _pcall(
        body, name=name, in_specs=[HBM_SPEC] * n, out_specs=[HBM_SPEC] * n,
        out_shape=[_sds((4, g.shape[1] // 2, g.shape[2]), g.dtype) for g in gs],
        scratch_shapes=[pltpu.SemaphoreType.DMA((n,)), pltpu.SemaphoreType.DMA((n,))],
    )(*gs)


def _comm_rows(a, b, itemsize):
    return a // 2 if a * b * itemsize > (3 << 19) and a % 16 == 0 else a


def pair_add(g, s1, c_idx, *, name):
    n, a, b = g.shape
    ah = a // 2
    ta = _comm_rows(ah, b, 2)
    nblk = ah // ta

    def body(c_ref, g_ref, s_ref, o_ref):
        o_ref[...] = (g_ref[...].astype(F32) + s_ref[...].astype(F32)).astype(o_ref.dtype)

    grid_spec = pltpu.PrefetchScalarGridSpec(
        num_scalar_prefetch=1, grid=(n, nblk),
        in_specs=[pl.BlockSpec((1, ta, b), lambda j, i, c_ref: (j, c_ref[0] * nblk + i, 0)),
                  pl.BlockSpec((1, ta, b), lambda j, i, c_ref: (j, i, 0))],
        out_specs=pl.BlockSpec((1, ta, b), lambda j, i, c_ref: (j, i, 0)),
    )
    return _pcall(body, name=name, grid_spec=grid_spec, out_shape=_sds((n, ah, b), g.dtype), compiler_params=_params(2))(
        c_idx, g, s1)


def sum_chunks(s2, *, name):
    n, a, b = s2.shape
    ta = _comm_rows(a, b, 4)

    def body(s_ref, o_ref):
        acc = s_ref[0].astype(F32)
        for j in range(1, n):
            acc = acc + s_ref[j].astype(F32)
        o_ref[...] = acc

    return _pcall(
        body, name=name, grid=(a // ta,), in_specs=[pl.BlockSpec((n, ta, b), lambda i: (0, i, 0))],
        out_specs=pl.BlockSpec((ta, b), lambda i: (i, 0)), out_shape=_sds((a, b), F32), compiler_params=_params(1),
    )(s2)


def pair_gather(rcs, *, name):
    n = len(rcs)

    def body(*refs):
        r_refs, o_refs = refs[:n], refs[2 * n:3 * n]
        send_sems, recv_sems = refs[3 * n:]
        x, y, c = _place()

        def copy(i, half):
            return pltpu.make_async_remote_copy(src_ref=r_refs[i], dst_ref=o_refs[i].at[half], send_sem=send_sems.at[i],
                                                recv_sem=recv_sems.at[i], device_id=(x, y, 1 - c), device_id_type=MESH)

        sends = [copy(i, c) for i in range(n)]
        for cp in sends:
            cp.start()
        for i in range(n):
            copy(i, 1 - c).wait_recv()
        for cp in sends:
            cp.wait_send()

    prefilled = [jnp.broadcast_to(r[None], (2,) + r.shape) for r in rcs]
    return _pcall(
        body, name=name, in_specs=[HBM_SPEC] * (2 * n), out_specs=[HBM_SPEC] * n,
        out_shape=[_sds(p.shape, p.dtype) for p in prefilled], input_output_aliases={n + i: i for i in range(n)},
        scratch_shapes=[pltpu.SemaphoreType.DMA((n,)), pltpu.SemaphoreType.DMA((n,))],
    )(*rcs, *prefilled)


def exchange_small(arrs, *, reduce, name):
    n = len(arrs)

    def body(*refs):
        v_refs, o_refs = refs[:n], refs[n:2 * n]
        bufs = refs[2 * n:3 * n] if reduce else o_refs
        send_sems, recv_sems = refs[-2:]
        x, y, c = _place()
        me = 4 * x + 2 * y + c
        for i in range(n):
            bufs[i][me] = v_refs[i][...]

        def peer(k):
            dx, dy, dc = (k >> 2) & 1, (k >> 1) & 1, k & 1
            return (1 - x if dx else x, 1 - y if dy else y, 1 - c if dc else c)

        def copy(i, k, slot):
            return pltpu.make_async_remote_copy(src_ref=v_refs[i], dst_ref=bufs[i].at[slot], send_sem=send_sems.at[7 * i + k - 1],
                                                recv_sem=recv_sems.at[7 * i + k - 1], device_id=peer(k), device_id_type=MESH)

        sends = [copy(i, k, me) for i in range(n) for k in range(1, 8)]
        for cp in sends:
            cp.start()
        for i in range(n):
            for k in range(1, 8):
                px, py, pc = peer(k)
                copy(i, k, 4 * px + 2 * py + pc).wait_recv()
        for cp in sends:
            cp.wait_send()
        if reduce:
            for i in range(n):
                acc = bufs[i][0]
                for d in range(1, 8):
                    acc = acc + bufs[i][d]
                o_refs[i][...] = acc

    vmem = pl.BlockSpec(memory_space=pltpu.VMEM)
    stacked = [(8,) + a.shape for a in arrs]
    return _pcall(
        body, name=name, in_specs=[vmem] * n, out_specs=[vmem] * n,
        out_shape=[_sds(a.shape if reduce else s, F32) for a, s in zip(arrs, stacked)],
        scratch_shapes=([pltpu.VMEM(s, F32) for s in stacked] if reduce else [])
        + [pltpu.SemaphoreType.DMA((7 * n,)), pltpu.SemaphoreType.DMA((7 * n,))],
    )(*arrs)


def _pad_rows(n, mult):
    return -(-n // mult) * mult


def _chip_major(g, b):
    return g.reshape(g.shape[0], 4, b).transpose(1, 0, 2)


def _rope_tables(tp):
    inv_freq = 1.0 / (ROPE_BASE ** (jnp.arange(0, QK_ROPE, 2, dtype=F32) / QK_ROPE))
    ang = jnp.arange(tp, dtype=F32)[:, None] * inv_freq[None, :]
    cos, sin = jnp.cos(ang), jnp.sin(ang)
    one = lambda n: jnp.ones((tp, n), F32)
    zero = lambda n: jnp.zeros((tp, n), F32)
    cq = jnp.concatenate([one(128), cos, cos, one(96)], axis=1)
    s1q = jnp.concatenate([zero(144), sin, zero(96)], axis=1)
    s2q = jnp.concatenate([zero(128), -sin, zero(112)], axis=1)
    ck = jnp.concatenate([cos, cos, zero(96)], axis=1)
    s1k = jnp.concatenate([zero(16), sin, zero(96)], axis=1)
    s2k = jnp.concatenate([-sin, zero(112)], axis=1)
    fwd = (cq * (ATT_SCALE * LOG2E), s1q * (ATT_SCALE * LOG2E), s2q * (ATT_SCALE * LOG2E), ck, s1k, s2k)
    bwd = (cq * ATT_SCALE, -s1q * ATT_SCALE, -s2q * ATT_SCALE, ck, -s1k, -s2k)
    return fwd, bwd


def _pad_w_in(w):
    return jnp.concatenate([w[:, :1952], jnp.zeros((w.shape[0], 96), w.dtype), w[:, 1952:]], axis=1)


def _pad_w_uq(w):
    w = w.reshape(Q_LORA, MLA_HEADS, QK_NOPE + QK_ROPE)
    z = lambda n: jnp.zeros((Q_LORA, MLA_HEADS, n), w.dtype)
    return jnp.concatenate([w[..., :QK_NOPE], z(64), w[..., QK_NOPE:], z(96)], axis=-1).reshape(Q_LORA, MLA_HEADS * Q_PAD)


def _unpad_w_uq(w):
    w = w.reshape(Q_LORA, MLA_HEADS, Q_PAD)
    return jnp.concatenate([w[..., :QK_NOPE], w[..., 128:128 + QK_ROPE]], axis=-1).reshape(Q_LORA, MLA_HEADS * (QK_NOPE + QK_ROPE))


def _pad_w_br_mla(w):
    w = w.reshape(MLA_HEADS, V_HEAD, D_MODEL)
    return jnp.concatenate([jnp.zeros_like(w), w], axis=1).reshape(MLA_HEADS * KV_PAD, D_MODEL)


def _unpad_w_br_mla(w):
    return w.reshape(MLA_HEADS, KV_PAD, D_MODEL)[:, V_HEAD:].reshape(MLA_HEADS * V_HEAD, D_MODEL)


def _riding(hooks, where, l, *args):
    make = hooks.get(where)
    ride = make(l, *args) if make else None
    return ride if ride else (None, lambda results: None)


def _layer_fwd(l, st, xprev, gp, bp, hb, w, tabs, hooks):
    ln_g, ln_b = w["ln_g"], w["ln_b"]
    lg = lambda k: ln_g[l, k][None]
    lb = lambda k: ln_b[l, k][None]
    s = {}
    s["x0"], s["gp0"], s["bp0"], s["hb0"] = xprev, gp, bp, hb
    side, got = _riding(hooks, "ffn1_fwd", l)
    s["g1"], s["u1"], s["a1"], *extras = ffn_up(hb, w["ffn1_w_up"][l], name="ffn_up", side=side)
    got(extras)
    s["xh1"], s["rs1"], s["hb1"] = down_ln(s["a1"], w["ffn1_w_down"][l], xprev, gp, bp, lg(0), lb(0), name="ffn_down_ln")
    s["p"] = mm_rows([(s["hb1"], w["mix_w_in"][l], False, 0)], D_IN_PAD, name="mix_in", tn=1024, out_dtype=BF)
    gq, gkv = w["q_norm_g"][l][None], w["kv_norm_g"][l][None]
    s["cqn"], s["ckvn"], s["q2"], s["kv"], s["kr"] = mla_prep(s["p"], gq, gkv, w["w_uq"][l], w["w_ukv"][l], tabs, name="mla_prep")
    side, got = _riding(hooks, "attn_fwd", l)
    s["o2"], s["lse"], *extras = attn_fwd(s["q2"], s["kv"], s["kr"], name="attn_fwd", side=side)
    got(extras)
    s["ycv"], s["conv"] = conv_fwd(s["p"], w["conv_w"][l], name="conv_fwd")
    s["bc"], s["bm"], s["mg"], s["xh2"], s["rs2"], s["hb2"] = merge_out_ln(
        s["ycv"], s["o2"], s["p"], w["mix_b_gate"][l], w["w_br_conv"][l], w["w_br_mla"][l], w["w_o"][l],
        s["xh1"], lg(0), lb(0), lg(1), lb(1), name="merge_out_ln")
    s["g2"], s["u2"], s["a2"] = ffn_up(s["hb2"], w["ffn2_w_up"][l], name="ffn_up")
    s["xh3"], s["rs3"], s["hb3"] = down_ln(s["a2"], w["ffn2_w_down"][l], s["xh2"], lg(1), lb(1), lg(2), lb(2), name="ffn_down_ln")
    st.append(s)
    return s["xh3"], lg(2), lb(2), s["hb3"]


def _ffn_bwd(which, l, g, hooks, dh, w_up, w_down, ln_gain, hb_in, gate, up, act, xh, rs):
    dz, dzb, dgam, dbet = ln_bwd(dh, xh, rs, ln_gain, branch_scale=0.5, name="ln_bwd")
    g[which + "_w_down"] = tn_mm(act, dzb, tm=D_FF // 2, name="dw_down", shard=("rows", D_FF // 4))
    side, got = _riding(hooks, which + "_down_bwd", l, g)
    dgate, dup, *extras = ffn_down_bwd(dzb, w_down, gate, up, name="ffn_down_bwd", side=side)
    got(extras)
    d_w = tn_mm(hb_in, dgate, tm=512, name="dw_up", shard=("cols", D_FF // 2), slot0=0)
    g[which + "_w_up"] = tn_mm(hb_in, dup, tm=512, name="dw_up", shard=("cols", D_FF // 2), slot0=2, dst=d_w)
    side, got = _riding(hooks, which + "_up_bwd", l, g)
    dh_in = mm_rows([(dgate, w_up, True, 0), (dup, w_up, True, 1)], D_MODEL, name="ffn_up_bwd", tn=512, addend=dz, add_scale=ALPHA,
                    side=side)
    if side is not None:
        dh_in, *extras = dh_in
        got(extras)
    return dh_in, dgam, dbet


def _layer_bwd(l, s, dh, w, tabs_bwd, hooks):
    ln_g = w["ln_g"]
    lg = lambda k: ln_g[l, k][None]
    g = {}
    dh, dg2, db2 = _ffn_bwd("ffn2", l, g, hooks, dh, w["ffn2_w_up"][l], w["ffn2_w_down"][l], lg(2), s["hb2"], s["g2"], s["u2"],
                            s["a2"], s["xh3"], s["rs3"])
    dz, dzb, dg1, db1 = ln_bwd(dh, s["xh2"], s["rs2"], lg(1), branch_scale=1.0, name="ln_bwd")
    g["w_o"] = tn_mm(s["mg"], dzb, tm=1024, name="dw_o", shard=("rows", D_MODEL // 4))
    dbc, dbm, dp, dycv, do2, dl, g["mix_b_gate"] = merge_bwd(
        dzb, w["w_o"][l], s["bc"], s["bm"], s["p"], w["mix_b_gate"][l], w["w_br_conv"][l], w["w_br_mla"][l], s["o2"], name="merge_bwd")
    g["w_br_conv"] = tn_mm(s["ycv"], dbc, tm=512, name="dw_br_conv", shard=("cols", D_MODEL // 4))
    g["w_br_mla"] = _chip_major(_unpad_w_br_mla(tn_mm(s["o2"], dbm, tm=1024, name="dw_br_mla")), D_MODEL // 4)
    side, got = _riding(hooks, "attn_bwd", l, g)
    dq2, dkv, dkr, *extras = attn_bwd(s["q2"], s["kv"], s["kr"], do2, s["lse"], dl, name="attn_bwd", side=side)
    got(extras)
    gq, gkv = w["q_norm_g"][l][None], w["kv_norm_g"][l][None]
    dqb, dp, g["q_norm_g"], g["kv_norm_g"] = mla_prep_bwd(dq2, dkv, dkr, s["p"], gq, gkv, w["w_uq"][l], w["w_ukv"][l], tabs_bwd, dp,
                                                          name="mla_prep_bwd")
    g["w_uq"] = _chip_major(_unpad_w_uq(tn_mm(s["cqn"], dqb, tm=Q_LORA, name="dw_uq")), MLA_HEADS * (QK_NOPE + QK_ROPE) // 4)
    g["w_ukv"] = tn_mm(s["ckvn"], dkv, tm=KV_LORA, name="dw_ukv", shard=("cols", MLA_HEADS * KV_PAD // 4))
    dp, dw0, dw1, dw2 = conv_bwd(dycv, s["p"], s["conv"], w["conv_w"][l], dp, name="conv_bwd")
    g["conv_w"] = jnp.concatenate([dw0, dw1, dw2], axis=0)
    d_in = tn_mm(s["hb1"], dp, tm=512, name="dw_in")
    g["mix_w_in"] = _chip_major(jnp.concatenate([d_in[:, :1952], d_in[:, 2048:]], axis=1), D_IN // 4)
    dh = mm_rows([(dp, w["mix_w_in"][l], True, 0)], D_MODEL, name="mix_in_bwd", tn=512, addend=dz, add_scale=ALPHA)
    dh, dg0, db0 = _ffn_bwd("ffn1", l, g, hooks, dh, w["ffn1_w_up"][l], w["ffn1_w_down"][l], lg(0), s["hb0"], s["g1"], s["u1"],
                            s["a1"], s["xh1"], s["rs1"])
    g["ln_g"] = jnp.concatenate([dg0, dg1, dg2], axis=0)
    g["ln_b"] = jnp.concatenate([db0, db1, db2], axis=0)
    return dh, g


BIG = ("ffn1_w_up", "ffn1_w_down", "mix_w_in", "w_uq", "w_ukv", "w_br_conv", "w_br_mla", "w_o", "ffn2_w_up", "ffn2_w_down")
BIG_AXIS = (2, 1, 2, 2, 2, 2, 2, 1, 2, 1)
FFN1_MATRICES = ("ffn1_w_up", "ffn1_w_down")
MIXER_MATRICES = ("mix_w_in", "w_uq", "w_ukv", "w_br_conv", "w_br_mla", "w_o")
FFN2_MATRICES = ("ffn2_w_up", "ffn2_w_down")
SMALL_SHARDED = ("meta_tokens", "mix_b_gate", "conv_w", "ln_g", "ln_b")
SMALL_REPLICATED = ("q_norm_g", "kv_norm_g")
WEIGHTS = ("meta_tokens", "ffn1_w_up", "ffn1_w_down", "mix_w_in", "mix_b_gate", "conv_w", "q_norm_g", "w_uq", "kv_norm_g", "w_ukv",
           "w_br_conv", "w_br_mla", "w_o", "ffn2_w_up", "ffn2_w_down", "ln_g", "ln_b")


def _view2d(a):
    return a.reshape(-1, a.shape[-1])


def _local_grads(x_row, target_row, w, hooks=None):
    hooks = hooks or {}
    seq = x_row.shape[0]
    t_real = N_META + seq
    tp = _pad_rows(t_real, TM)
    pad = tp - t_real
    h0 = jnp.concatenate([w["meta_tokens"], x_row, jnp.zeros((pad, D_MODEL), F32)], axis=0)
    target_p = jnp.concatenate([jnp.zeros((N_META, D_MODEL), F32), target_row, jnp.zeros((pad, D_MODEL), F32)], axis=0)
    tabs, tabs_bwd = _rope_tables(tp)
    ones = jnp.ones((1, D_MODEL), F32)
    zeros = jnp.zeros((1, D_MODEL), F32)
    saved = []
    cur = (h0, ones, zeros, h0.astype(BF))
    for l in range(DEPTH):
        cur = _layer_fwd(l, saved, *cur, w, tabs, hooks)
    dh, loss_acc = loss_grad(cur[0], cur[1], cur[2], target_p, seq, name="loss_grad")
    grads = [None] * DEPTH
    for l in reversed(range(DEPTH)):
        dh, grads[l] = _layer_bwd(l, saved[l], dh, w, tabs_bwd, hooks)
        if "layer_bwd_done" in hooks:
            hooks["layer_bwd_done"](l, grads[l])
    return loss_acc, dh[N_META:t_real], dh[:N_META], grads


def kernel(x, meta_tokens, ffn1_w_up, ffn1_w_down, mix_w_in, mix_b_gate, conv_w, q_norm_g, w_uq, kv_norm_g, w_ukv, w_br_conv, w_br_mla, w_o, ffn2_w_up, ffn2_w_down, ln_g, ln_b, loss_target, m_meta_tokens, m_ffn1_w_up, m_ffn1_w_down, m_mix_w_in, m_mix_b_gate, m_conv_w, m_q_norm_g, m_w_uq, m_kv_norm_g, m_w_ukv, m_w_br_conv, m_w_br_mla, m_w_o, m_ffn2_w_up, m_ffn2_w_down, m_ln_g, m_ln_b, v_meta_tokens, v_ffn1_w_up, v_ffn1_w_down, v_mix_w_in, v_mix_b_gate, v_conv_w, v_q_norm_g, v_w_uq, v_kv_norm_g, v_w_ukv, v_w_br_conv, v_w_br_mla, v_w_o, v_ffn2_w_up, v_ffn2_w_down, v_ln_g, v_ln_b):
    local = dict(meta_tokens=meta_tokens, ffn1_w_up=ffn1_w_up, ffn1_w_down=ffn1_w_down, mix_w_in=mix_w_in, mix_b_gate=mix_b_gate,
                 conv_w=conv_w, q_norm_g=q_norm_g, w_uq=w_uq, kv_norm_g=kv_norm_g, w_ukv=w_ukv, w_br_conv=w_br_conv,
                 w_br_mla=w_br_mla, w_o=w_o, ffn2_w_up=ffn2_w_up, ffn2_w_down=ffn2_w_down, ln_g=ln_g, ln_b=ln_b)
    mom_m = dict(zip(WEIGHTS, (m_meta_tokens, m_ffn1_w_up, m_ffn1_w_down, m_mix_w_in, m_mix_b_gate, m_conv_w, m_q_norm_g, m_w_uq,
                               m_kv_norm_g, m_w_ukv, m_w_br_conv, m_w_br_mla, m_w_o, m_ffn2_w_up, m_ffn2_w_down, m_ln_g, m_ln_b)))
    mom_v = dict(zip(WEIGHTS, (v_meta_tokens, v_ffn1_w_up, v_ffn1_w_down, v_mix_w_in, v_mix_b_gate, v_conv_w, v_q_norm_g, v_w_uq,
                               v_kv_norm_g, v_w_ukv, v_w_br_conv, v_w_br_mla, v_w_o, v_ffn2_w_up, v_ffn2_w_down, v_ln_g, v_ln_b)))
    xi, yi, ci = _place()
    chip = 2 * xi + yi

    shards = {n: local[n].astype(BF) for n in BIG}
    axis = dict(zip(BIG, BIG_AXIS))
    pad_layout = {"mix_w_in": _pad_w_in, "w_uq": _pad_w_uq, "w_br_mla": _pad_w_br_mla}
    w = {n: [None] * DEPTH for n in BIG}

    def fetch(keys):
        def install(gathered):
            for (n, l), g in zip(keys, gathered):
                if n in FFN1_MATRICES + FFN2_MATRICES:
                    w[n][l] = g
                    continue
                full = jnp.concatenate([g[j] for j in range(4)], axis=axis[n] - 1)
                w[n][l] = pad_layout[n](full) if n in pad_layout else full
        return gather_side([(shards[n], l) for n, l in keys]), install

    first, install_first = fetch([(n, 0) for n in FFN1_MATRICES])
    install_first(exchange_alone(first, name="gather_weights"))
    fetch_under = {("ffn1_fwd", 0): [(n, 0) for n in MIXER_MATRICES],
                   ("attn_fwd", 0): [(n, 0) for n in FFN2_MATRICES] + [(n, 1) for n in BIG]}
    hooks = {where: functools.partial(lambda l, where: fetch(fetch_under[where, l]) if (where, l) in fetch_under else None, where=where)
             for where in ("ffn1_fwd", "attn_fwd")}
    stacked = exchange_small([_view2d(local[n]) for n in SMALL_SHARDED], reduce=False, name="gather_small")
    for n, st in zip(SMALL_SHARDED, stacked):
        full = jnp.concatenate([st[2 * j] for j in range(4)], axis=-1)
        w[n] = full.reshape(local[n].shape[:-1] + (full.shape[-1],))
    for n in SMALL_REPLICATED:
        w[n] = local[n]

    c_idx = jnp.reshape(ci, (1,)).astype(jnp.int32)
    done, from_chips = {}, {}

    def send(keys, grad_of):
        glist = [grad_of[k] for k in keys]
        from_sibling = pair_exchange(glist, name="rs_pair_exchange")
        sums = [pair_add(a, s, c_idx, name="rs_pair_add") for a, s in zip(glist, from_sibling)]
        return scatter_side(sums), lambda results: from_chips.update(zip(keys, results))

    send_under = {"attn_bwd": FFN2_MATRICES + ("w_o", "w_br_conv", "w_br_mla"),
                  "ffn1_down_bwd": ("mix_w_in", "w_uq", "w_ukv", "ffn1_w_down"), "ffn1_up_bwd": ("ffn1_w_up",)}
    hooks["layer_bwd_done"] = lambda l, g: done.update({(n, l): g[n] for n in BIG})

    def send_hook(where):
        def hook(l, g):
            if l != 0:
                return None
            keys = [(n, 0) for n in send_under[where]] + ([(n, 1) for n in BIG] if where == "attn_bwd" else [])
            return send(keys, {**done, **{(n, 0): g[n] for n in send_under[where]}})
        return hook

    for where in send_under:
        hooks[where] = send_hook(where)

    loss_acc, grad_x, d_meta, grads = _local_grads(x[0], loss_target[0], w, hooks)
    grad_x = grad_x[None]
    keys = [(n, l) for n in BIG for l in range(DEPTH)]
    reduced = pair_gather([sum_chunks(from_chips[k], name="rs_sum") for k in keys], name="rs_pair_gather")
    reduced = {k: r.reshape(local[k[0]].shape[1:]) for k, r in zip(keys, reduced)}
    gshard = {n: jnp.stack([reduced[n, l] for l in range(DEPTH)]) for n in BIG}

    small_names = SMALL_SHARDED + SMALL_REPLICATED
    gsmall = {n: jnp.concatenate([grads[l][n] for l in range(DEPTH)], axis=0) for n in small_names if n != "meta_tokens"}
    gsmall["meta_tokens"] = d_meta
    small_red = exchange_small([gsmall[n] for n in small_names] + [loss_acc], reduce=True, name="reduce_small")
    loss = small_red[-1][0, 0]
    for n, full in zip(small_names, small_red[:-1]):
        if n in SMALL_SHARDED:
            sh = local[n].shape[-1]
            full = lax.dynamic_slice_in_dim(full, chip * sh, sh, axis=1)
        gshard[n] = full.reshape(local[n].shape)

    delta, new_m, new_v = {}, {}, {}
    for n in WEIGHTS:
        shape = local[n].shape
        d, nm, nv = adamw(_view2d(local[n]), _view2d(gshard[n]), _view2d(mom_m[n]), _view2d(mom_v[n]), name="adamw")
        delta[n], new_m[n], new_v[n] = d.reshape(shape), nm.reshape(shape), nv.reshape(shape)
    return (loss, grad_x, *[gshard[n] for n in WEIGHTS], *[delta[n] for n in WEIGHTS], *[new_m[n] for n in WEIGHTS],
            *[new_v[n] for n in WEIGHTS])
```

```python
import functools

import jax
import jax.numpy as jnp
from jax import lax
from jax.experimental import pallas as pl
from jax.experimental.pallas import tpu as pltpu

F32 = jnp.float32
BF = jnp.bfloat16
MESH = pl.DeviceIdType.MESH

D_MODEL = 1024
DEPTH = 2
N_META = 16
D_CONV = 512
MLA_HEADS = 8
QK_NOPE = 64
QK_ROPE = 32
V_HEAD = 64
Q_LORA = 256
KV_LORA = 128
ROPE_BASE = 10000.0
NEG_INF = -1e30
D_FF = 2816
ALPHA = (2 * DEPTH) ** 0.25
LN_EPS = 1e-5
RMS_EPS = 1e-6
ATT_SCALE = (QK_NOPE + QK_ROPE) ** -0.5
LOG2E = 1.4426950408889634
LN2 = 0.6931471805599453
D_IN = 4000
D_IN_PAD = 4096
Q_PAD = 256
KV_PAD = 128

ADAM_LR = 0.001
ADAM_B1 = 0.9
ADAM_B2 = 0.999
ADAM_EPS = 1e-08
ADAM_WD = 0.01
ADAM_STEP = 10

TM = 768
TMH = 384
LANES = 128
COMM_COLS = 512
COMM_ROW_BLOCK = 1472
VMEM_LIMIT_BYTES = 50 * 1024 * 1024

NT = (((1,), (1,)), ((), ()))
TN = (((0,), (0,)), ((), ()))


def _pcall(body, **kw):
    return pl.pallas_call(body, **kw)


def _params(n_axes):
    return pltpu.CompilerParams(dimension_semantics=("arbitrary",) * n_axes, vmem_limit_bytes=VMEM_LIMIT_BYTES)


def _sds(shape, dtype):
    return jax.ShapeDtypeStruct(shape, dtype)


def mm_rows(pairs, n_out, *, name, tn=None, addend=None, add_scale=1.0, out_dtype=F32, side=None):
    tp = pairs[0][0].shape[0]
    tn = tn or n_out
    in_specs, args = [], []
    for a, b, nt, kb in pairs:
        k = a.shape[1]
        in_specs.append(pl.BlockSpec((TM, k), lambda i, j: (i, 0)))
        if nt and b.ndim == 3:
            in_specs.append(pl.BlockSpec((2, tn, k // 2), functools.partial(lambda i, j, kb: (kb, j, 0), kb=kb)))
        elif nt:
            in_specs.append(pl.BlockSpec((tn, k), functools.partial(lambda i, j, kb: (j, kb), kb=kb)))
        else:
            in_specs.append(pl.BlockSpec((k, tn), lambda i, j: (0, j)))
        args += [a, b]
    if addend is not None:
        in_specs.append(pl.BlockSpec((TM, tn), lambda i, j: (i, j)))
        args.append(addend)
    n_pairs = len(pairs)
    nts = [p[2] for p in pairs]

    def body(refs, out_refs, scratch):
        o_ref = out_refs[0]
        acc = None
        for p in range(n_pairs):
            a = refs[2 * p][...].astype(BF)
            b = refs[2 * p + 1][...]
            if b.ndim == 3:
                b = jnp.concatenate([b[0], b[1]], axis=1)
            d = lax.dot_general(a, b, NT if nts[p] else (((1,), (0,)), ((), ())), preferred_element_type=F32)
            acc = d if acc is None else acc + d
        if addend is not None:
            acc = acc + add_scale * refs[2 * n_pairs][...].astype(F32)
        o_ref[...] = acc.astype(o_ref.dtype)

    out = _side_call(
        body, side, name=name, grid=(tp // TM, n_out // tn), in_specs=in_specs,
        out_specs=[pl.BlockSpec((TM, tn), lambda i, j: (i, j))], out_shape=[_sds((tp, n_out), out_dtype)],
        scratch_shapes=[], args=args,
    )
    return out if side is not None else out[0]


def tn_mm(a, b, *, tm, name, out_dtype=BF, shard=None, slot0=0, dst=None):
    tp, m = a.shape
    n = b.shape[1]
    nk = tp // TM
    if shard is None:
        pieces, out_block, out_index, out_full = 1, (tm, n), (lambda i, k: (i, 0)), (m, n)
    elif shard[0] == "cols":
        pieces = n // shard[1]
        out_block, out_full = (pieces, tm, shard[1]), (4, m, shard[1])
        out_index = lambda i, k: (slot0 // pieces, i, 0)
    else:
        pieces = tm // shard[1]
        out_block, out_full = (pieces, shard[1], n), (4, m // 4, n)
        out_index = lambda i, k: (i, 0, 0)

    def body(a_ref, b_ref, *rest):
        o_ref, acc_ref = rest[-2], rest[-1]
        k = pl.program_id(1)

        @pl.when(k == 0)
        def _():
            acc_ref[...] = jnp.zeros_like(acc_ref)

        acc_ref[...] += lax.dot_general(a_ref[...].astype(BF), b_ref[...].astype(BF), TN, preferred_element_type=F32)

        @pl.when(k == nk - 1)
        def _():
            if shard is None:
                o_ref[...] = acc_ref[...].astype(o_ref.dtype)
            elif shard[0] == "cols":
                for j in range(pieces):
                    o_ref[j] = acc_ref[:, j * shard[1]:(j + 1) * shard[1]].astype(o_ref.dtype)
            else:
                for j in range(pieces):
                    o_ref[j] = acc_ref[j * shard[1]:(j + 1) * shard[1], :].astype(o_ref.dtype)

    in_specs = [pl.BlockSpec((TM, tm), lambda i, k: (k, i)), pl.BlockSpec((TM, n), lambda i, k: (k, 0))]
    args = [a, b]
    aliases = {}
    if dst is not None:
        in_specs.append(pl.BlockSpec(memory_space=pl.ANY))
        args.append(dst)
        aliases = {2: 0}
    return _pcall(
        body, name=name, grid=(m // tm, nk), in_specs=in_specs, out_specs=pl.BlockSpec(out_block, out_index),
        out_shape=_sds(out_full, out_dtype), input_output_aliases=aliases,
        scratch_shapes=[pltpu.VMEM((tm, n), F32)], compiler_params=_params(2),
    )(*args)


def _ln_store(z, g_ref, b_ref, xh_ref, rs_ref, hb_ref):
    mu = jnp.mean(z, axis=-1, keepdims=True)
    zc = z - mu
    var = jnp.mean(zc * zc, axis=-1, keepdims=True)
    rstd = lax.rsqrt(var + LN_EPS)
    xh = zc * rstd
    xh_ref[...] = xh
    rs_ref[...] = rstd
    hb_ref[...] = (xh * g_ref[...] + b_ref[...]).astype(BF)


def _ln_out(tp, tm=TM):
    specs = [pl.BlockSpec((tm, D_MODEL), lambda i: (i, 0)), pl.BlockSpec((tm, 1), lambda i: (i, 0)),
             pl.BlockSpec((tm, D_MODEL), lambda i: (i, 0))]
    shapes = [_sds((tp, D_MODEL), F32), _sds((tp, 1), F32), _sds((tp, D_MODEL), BF)]
    return specs, shapes


def _row_vec(n):
    return pl.BlockSpec((1, n), lambda i: (0, 0))


def ffn_up(hb, wup, *, name, side=None):
    tp = hb.shape[0]
    tn = D_FF // 2
    nj = D_FF // tn

    def body(in_refs, out_refs, scratch):
        h_ref, wg_ref, wu_ref = in_refs
        g_ref, u_ref, a_ref = out_refs
        h = h_ref[...]
        g = jnp.dot(h, wg_ref[0], preferred_element_type=F32)
        u = jnp.dot(h, wu_ref[0], preferred_element_type=F32)
        g_ref[...] = g.astype(BF)
        u_ref[...] = u.astype(BF)
        a_ref[...] = (g * jax.nn.sigmoid(g) * u).astype(BF)

    blk = pl.BlockSpec((TM, tn), lambda i, j: (i, j))
    return _side_call(
        body, side, name=name, grid=(tp // TM, nj),
        in_specs=[pl.BlockSpec((TM, D_MODEL), lambda i, j: (i, 0)), pl.BlockSpec((1, D_MODEL, tn), lambda i, j: (j, 0, 0)),
                  pl.BlockSpec((1, D_MODEL, tn), lambda i, j: (j + nj, 0, 0))],
        out_specs=[blk, blk, blk], out_shape=[_sds((tp, D_FF), BF)] * 3, scratch_shapes=[], args=(hb, wup, wup),
    )


def down_ln(a, wd, xprev, gp, bp, g, b, *, name):
    tp = a.shape[0]

    def body(a_ref, wd_ref, xp_ref, gp_ref, bp_ref, g_ref, b_ref, xh_ref, rs_ref, hb_ref):
        wd = jnp.concatenate([wd_ref[j] for j in range(4)], axis=0)
        f = jnp.dot(a_ref[...], wd, preferred_element_type=F32)
        hprev = xp_ref[...] * gp_ref[...] + bp_ref[...]
        _ln_store(ALPHA * hprev + 0.5 * f, g_ref, b_ref, xh_ref, rs_ref, hb_ref)

    out_specs, out_shape = _ln_out(tp)
    return _pcall(
        body, name=name, grid=(tp // TM,),
        in_specs=[pl.BlockSpec((TM, D_FF), lambda i: (i, 0)), pl.BlockSpec((4, D_FF // 4, D_MODEL), lambda i: (0, 0, 0)),
                  pl.BlockSpec((TM, D_MODEL), lambda i: (i, 0))] + [_row_vec(D_MODEL)] * 4,
        out_specs=out_specs, out_shape=out_shape, compiler_params=_params(1),
    )(a, wd, xprev, gp, bp, g, b)


def _rope(x, c, s1, s2, reps):
    n = x.shape[1]
    if reps > 1:
        c, s1, s2 = (jnp.tile(t, (1, reps)) for t in (c, s1, s2))
    return x * c + pltpu.roll(x, 16, 1) * s1 + pltpu.roll(x, n - 16, 1) * s2


def _rms(x, g):
    r = lax.rsqrt(jnp.mean(x * x, axis=-1, keepdims=True) + RMS_EPS)
    return x * r * g, r


def mla_prep(p, gq, gkv, wuq_p, wukv, tabs, *, name):
    tp = p.shape[0]
    nh = MLA_HEADS

    def body(cq_ref, ckv_ref, kr_ref, gq_ref, gkv_ref, wuq_ref, wukv_ref, cq_t, s1q_t, s2q_t, ck_t, s1k_t, s2k_t,
             cqn_ref, ckvn_ref, q2_ref, kv_ref, krr_ref):
        cqn, _ = _rms(cq_ref[...].astype(F32), gq_ref[...])
        ckvn, _ = _rms(ckv_ref[...].astype(F32), gkv_ref[...])
        cqn = cqn.astype(BF)
        ckvn = ckvn.astype(BF)
        cqn_ref[...] = cqn
        ckvn_ref[...] = ckvn
        q = jnp.dot(cqn, wuq_ref[...], preferred_element_type=F32)
        q2_ref[...] = _rope(q, cq_t[...], s1q_t[...], s2q_t[...], nh).astype(BF)
        kv_ref[...] = jnp.dot(ckvn, wukv_ref[...], preferred_element_type=F32).astype(BF)
        krr_ref[...] = _rope(kr_ref[...].astype(F32), ck_t[...], s1k_t[...], s2k_t[...], 1).astype(BF)

    def rows(n, col=0):
        return pl.BlockSpec((TMH, n), functools.partial(lambda i, col: (i, col), col=col))

    return _pcall(
        body, name=name, grid=(tp // TMH,),
        in_specs=[rows(Q_LORA, 1536 // Q_LORA), rows(KV_LORA, 1792 // KV_LORA), rows(LANES, 1920 // LANES),
                  _row_vec(Q_LORA), _row_vec(KV_LORA),
                  pl.BlockSpec((Q_LORA, nh * Q_PAD), lambda i: (0, 0)), pl.BlockSpec((KV_LORA, nh * KV_PAD), lambda i: (0, 0)),
                  rows(Q_PAD), rows(Q_PAD), rows(Q_PAD), rows(LANES), rows(LANES), rows(LANES)],
        out_specs=[rows(Q_LORA), rows(KV_LORA), rows(nh * Q_PAD), rows(nh * KV_PAD), rows(LANES)],
        out_shape=[_sds((tp, Q_LORA), BF), _sds((tp, KV_LORA), BF), _sds((tp, nh * Q_PAD), BF),
                   _sds((tp, nh * KV_PAD), BF), _sds((tp, LANES), BF)],
        compiler_params=_params(1),
    )(p, p, p, gq, gkv, wuq_p, wukv, *tabs)


def _causal_mask(s):
    qpos = lax.broadcasted_iota(jnp.int32, (TM, TM), 0)
    kpos = lax.broadcasted_iota(jnp.int32, (TM, TM), 1)
    return jnp.where(kpos <= qpos, s, NEG_INF)


def _key_rows(k):
    return pl.ds(pl.multiple_of(k * TM, TM), TM)


def _pipelined_key_blocks(n, prefetch, process):
    prefetch(0, 0)

    def pair(j, carry):
        prefetch(2 * j + 1, 1)
        process(2 * j, 0, False)
        prefetch(2 * j + 2, 0)
        process(2 * j + 1, 1, False)
        return carry

    lax.fori_loop(0, n // 2, pair, 0)

    @pl.when(n % 2 == 1)
    def _():
        prefetch(n, 1)
        process(n - 1, 0, False)
        process(n, 1, True)

    @pl.when(n % 2 == 0)
    def _():
        process(n, 0, True)


def _side_call(body_main, side, *, name, grid, in_specs, out_specs, out_shape, scratch_shapes, args):
    n_in, n_out, n_scr = len(in_specs), len(out_specs), len(scratch_shapes)
    s_in, s_pre, n_sems, program = side if side is not None else ((), (), 0, None)
    a, b = len(s_in), len(s_pre)

    def body(*refs):
        in_refs = refs[:n_in]
        out_refs = refs[n_in + a + b:n_in + a + b + n_out]
        scr = refs[n_in + a + 2 * b + n_out:n_in + a + 2 * b + n_out + n_scr]
        if side is not None:
            side_in = refs[n_in:n_in + a]
            side_out = refs[n_in + a + b + n_out:n_in + a + 2 * b + n_out]
            start, finish = program(side_in, side_out, refs[-2], refs[-1])

            @pl.when((pl.program_id(0) == 0) & (pl.program_id(1) == 0))
            def _():
                start()

        body_main(in_refs, out_refs, scr)
        if side is not None:
            @pl.when((pl.program_id(0) == grid[0] - 1) & (pl.program_id(1) == grid[1] - 1))
            def _():
                finish()

    sems = [pltpu.SemaphoreType.DMA((n_sems,))] * 2 if side is not None else []
    return _pcall(
        body, name=name, grid=grid, in_specs=list(in_specs) + [HBM_SPEC] * (a + b), out_specs=list(out_specs) + [HBM_SPEC] * b,
        out_shape=list(out_shape) + [_sds(p.shape, p.dtype) for p in s_pre],
        input_output_aliases={n_in + a + i: n_out + i for i in range(b)},
        scratch_shapes=list(scratch_shapes) + sems, compiler_params=_params(2),
    )(*args, *s_in, *s_pre)


def attn_fwd(q2, kv, kr, *, name, side=None):
    tp = q2.shape[0]
    nh = MLA_HEADS
    nb = tp // TM
    rep = TM // LANES

    def body(in_refs, out_refs, scratch):
        q_ref, kv_ref, kr_ref = in_refs
        o_ref, lse_ref = out_refs
        m_ref, l_ref, acc_ref, s0_ref, s1_ref, p_ref, alpha_ref = scratch
        qi = pl.program_id(1)
        s_refs = (s0_ref, s1_ref)
        m_ref[...] = jnp.full_like(m_ref, NEG_INF)
        l_ref[...] = jnp.zeros_like(l_ref)
        acc_ref[...] = jnp.zeros_like(acc_ref)

        def prefetch(k, slot):
            k2 = jnp.concatenate([kv_ref[_key_rows(k), :], kr_ref[_key_rows(k), :]], axis=1)
            s_refs[slot][...] = lax.dot_general(q_ref[...], k2, NT, preferred_element_type=F32)

        def process(k, slot, diagonal):
            for r in range(TM // LANES):
                rows = slice(r * LANES, (r + 1) * LANES)
                s = s_refs[slot][rows, :]
                if diagonal:
                    qpos = r * LANES + lax.broadcasted_iota(jnp.int32, (LANES, TM), 0)
                    s = jnp.where(lax.broadcasted_iota(jnp.int32, (LANES, TM), 1) <= qpos, s, NEG_INF)
                m_prev = m_ref[rows, :]
                m_new = jnp.maximum(m_prev, jnp.max(s, axis=1, keepdims=True))
                alpha = jnp.exp2(m_prev - m_new)
                p = jnp.exp2(s - jnp.tile(m_new, (1, rep)))
                lane_sums = p[:, 0:LANES]
                for t in range(1, rep):
                    lane_sums = lane_sums + p[:, t * LANES:(t + 1) * LANES]
                l_ref[rows, :] = alpha * l_ref[rows, :] + lane_sums
                p_ref[rows, :] = p.astype(BF)
                alpha_ref[rows, :] = alpha
                m_ref[rows, :] = m_new
            acc_ref[...] = alpha_ref[...] * acc_ref[...] + jnp.dot(p_ref[...], kv_ref[_key_rows(k), :], preferred_element_type=F32)

        _pipelined_key_blocks(qi, prefetch, process)
        l = jnp.sum(l_ref[...], axis=1, keepdims=True)
        o_ref[...] = (acc_ref[...] / l).astype(BF)
        lse_ref[...] = m_ref[...] + jnp.log2(l)

    return _side_call(
        body, side, name=name, grid=(nh, nb),
        in_specs=[pl.BlockSpec((TM, Q_PAD), lambda h, qi: (qi, h)), pl.BlockSpec((tp, KV_PAD), lambda h, qi: (0, h)),
                  pl.BlockSpec((tp, LANES), lambda h, qi: (0, 0))],
        out_specs=[pl.BlockSpec((TM, KV_PAD), lambda h, qi: (qi, h)), pl.BlockSpec((TM, LANES), lambda h, qi: (qi, h))],
        out_shape=[_sds((tp, nh * KV_PAD), BF), _sds((tp, nh * LANES), F32)],
        scratch_shapes=[pltpu.VMEM((TM, LANES), F32)] * 3 + [pltpu.VMEM((TM, TM), F32)] * 2
        + [pltpu.VMEM((TM, TM), BF), pltpu.VMEM((TM, LANES), F32)], args=(q2, kv, kr),
    )


def conv_fwd(p, w, *, name):
    tp = p.shape[0]

    def body(b_ref, c_ref, h_ref, w_ref, y_ref, cv_ref, ebuf):
        i = pl.program_id(0)

        @pl.when(i == 0)
        def _():
            ebuf[0:8, :] = jnp.zeros((8, D_CONV), F32)

        e = c_ref[...].astype(F32) * h_ref[...].astype(F32)
        ebuf[8:8 + TM, :] = e
        w_all = w_ref[...]
        conv = w_all[0:1] * ebuf[pl.ds(6, TM), :] + w_all[1:2] * ebuf[pl.ds(7, TM), :] + w_all[2:3] * e
        cv_ref[...] = conv.astype(BF)
        y_ref[...] = (b_ref[...].astype(F32) * conv).astype(BF)
        ebuf[0:8, :] = ebuf[TM:TM + 8, :]

    def col(j):
        return pl.BlockSpec((TM, D_CONV), functools.partial(lambda i, j: (i, j), j=j))

    return _pcall(
        body, name=name, grid=(tp // TM,),
        in_specs=[col(0), col(1), col(2), pl.BlockSpec((3, D_CONV), lambda i: (0, 0))],
        out_specs=[col(0), col(0)], out_shape=[_sds((tp, D_CONV), BF)] * 2,
        scratch_shapes=[pltpu.VMEM((TM + 8, D_CONV), F32)], compiler_params=_params(1),
    )(p, p, p, w)


def merge_out_ln(ycv, o2, p, bg, wbc, wbm_p, wo, xprev, gp, bp, g, b, *, name):
    tp = ycv.shape[0]

    def body(y_ref, o_ref, gc_ref, gm_ref, bg_ref, wbc_ref, wbm_ref, wo_ref, xp_ref, gp_ref, bp_ref, g_ref, b_ref,
             bc_ref, bm_ref, mg_ref, xh_ref, rs_ref, hb_ref):
        bc = jnp.dot(y_ref[...], wbc_ref[...], preferred_element_type=F32)
        bm = jnp.dot(o_ref[...], wbm_ref[...], preferred_element_type=F32)
        bgv = bg_ref[...]
        mg = (jax.nn.sigmoid(gc_ref[...].astype(F32) + bgv[0:1]) * bc
              + jax.nn.sigmoid(gm_ref[...].astype(F32) + bgv[1:2]) * bm)
        mgb = mg.astype(BF)
        bc_ref[...] = bc.astype(BF)
        bm_ref[...] = bm.astype(BF)
        mg_ref[...] = mgb
        mix = jnp.dot(mgb, wo_ref[...], preferred_element_type=F32)
        hprev = xp_ref[...] * gp_ref[...] + bp_ref[...]
        _ln_store(ALPHA * hprev + mix, g_ref, b_ref, xh_ref, rs_ref, hb_ref)

    def rows(n, col=0):
        return pl.BlockSpec((TMH, n), functools.partial(lambda i, col: (i, col), col=col))

    def whole(r, c):
        return pl.BlockSpec((r, c), lambda i: (0, 0))

    ln_specs, ln_shapes = _ln_out(tp, TMH)
    return _pcall(
        body, name=name, grid=(tp // TMH,),
        in_specs=[rows(D_CONV), rows(MLA_HEADS * KV_PAD), rows(D_MODEL, 2), rows(D_MODEL, 3), whole(2, D_MODEL),
                  whole(D_CONV, D_MODEL), whole(MLA_HEADS * KV_PAD, D_MODEL), whole(D_MODEL, D_MODEL), rows(D_MODEL)]
        + [_row_vec(D_MODEL)] * 4,
        out_specs=[rows(D_MODEL)] * 3 + ln_specs, out_shape=[_sds((tp, D_MODEL), BF)] * 3 + ln_shapes,
        compiler_params=_params(1),
    )(ycv, o2, p, p, bg, wbc, wbm_p, wo, xprev, gp, bp, g, b)


def loss_grad(xh, g, b, target_p, n_real, *, name):
    tp = xh.shape[0]

    def body(x_ref, g_ref, b_ref, t_ref, dy_ref, loss_ref):
        i = pl.program_id(0)

        @pl.when(i == 0)
        def _():
            loss_ref[...] = jnp.zeros_like(loss_ref)

        row = i * TM + lax.broadcasted_iota(jnp.int32, (TM, 1), 0)
        real = (row >= N_META) & (row < N_META + n_real)
        diff = jnp.where(real, x_ref[...] * g_ref[...] + b_ref[...] - t_ref[...], 0.0)
        dy_ref[...] = diff * (1.0 / D_MODEL)
        loss_ref[...] += 0.5 / D_MODEL * jnp.sum(diff * diff)

    return _pcall(
        body, name=name, grid=(tp // TM,),
        in_specs=[pl.BlockSpec((TM, D_MODEL), lambda i: (i, 0)), _row_vec(D_MODEL), _row_vec(D_MODEL),
                  pl.BlockSpec((TM, D_MODEL), lambda i: (i, 0))],
        out_specs=[pl.BlockSpec((TM, D_MODEL), lambda i: (i, 0)), pl.BlockSpec((8, LANES), lambda i: (0, 0))],
        out_shape=[_sds((tp, D_MODEL), F32), _sds((8, LANES), F32)], compiler_params=_params(1),
    )(xh, g, b, target_p)


def ln_bwd(dh, xh, rstd, g, *, branch_scale, name):
    tp = dh.shape[0]

    def body(dh_ref, xh_ref, rs_ref, g_ref, dzb_ref, dg_ref, db_ref):
        i = pl.program_id(0)

        @pl.when(i == 0)
        def _():
            dg_ref[...] = jnp.zeros_like(dg_ref)
            db_ref[...] = jnp.zeros_like(db_ref)

        dy = dh_ref[...]
        xhat = xh_ref[...]
        dg_ref[...] += jnp.sum(dy * xhat, axis=0, keepdims=True)
        db_ref[...] += jnp.sum(dy, axis=0, keepdims=True)
        dxh = dy * g_ref[...]
        m1 = jnp.mean(dxh, axis=-1, keepdims=True)
        m2 = jnp.mean(dxh * xhat, axis=-1, keepdims=True)
        dz = rs_ref[...] * (dxh - m1 - xhat * m2)
        dzb_ref[...] = (branch_scale * dz).astype(BF)

    rows = pl.BlockSpec((TM, D_MODEL), lambda i: (i, 0))
    return _pcall(
        body, name=name, grid=(tp // TM,),
        in_specs=[rows, rows, pl.BlockSpec((TM, 1), lambda i: (i, 0)), _row_vec(D_MODEL)],
        out_specs=[rows, _row_vec(D_MODEL), _row_vec(D_MODEL)],
        out_shape=[_sds((tp, D_MODEL), BF), _sds((1, D_MODEL), F32), _sds((1, D_MODEL), F32)],
        compiler_params=_params(1),
    )(dh, xh, rstd, g)


def ffn_down_bwd(dzb, wd, gate, up, *, name, side=None):
    tp = dzb.shape[0]
    tn = D_FF // 2

    def body(in_refs, out_refs, scratch):
        dz_ref, wd_ref, g_ref, u_ref = in_refs
        dg_ref, du_ref = out_refs
        wd = jnp.concatenate([wd_ref[0], wd_ref[1]], axis=0)
        da = lax.dot_general(dz_ref[...], wd, NT, preferred_element_type=F32)
        g = g_ref[...].astype(F32)
        u = u_ref[...].astype(F32)
        sg = jax.nn.sigmoid(g)
        dg_ref[...] = (da * u * sg * (1.0 + g * (1.0 - sg))).astype(BF)
        du_ref[...] = (da * g * sg).astype(BF)

    blk = pl.BlockSpec((TM, tn), lambda i, j: (i, j))
    return _side_call(
        body, side, name=name, grid=(tp // TM, D_FF // tn),
        in_specs=[pl.BlockSpec((TM, D_MODEL), lambda i, j: (i, 0)), pl.BlockSpec((2, tn // 2, D_MODEL), lambda i, j: (j, 0, 0)), blk, blk],
        out_specs=[blk, blk], out_shape=[_sds((tp, D_FF), BF)] * 2, scratch_shapes=[], args=(dzb, wd, gate, up),
    )


def merge_bwd(dzb, wo, bc, bm, p, bg, wbc, wbm_p, o2, *, name):
    tp = dzb.shape[0]
    nh = MLA_HEADS

    def body(dz_ref, wo_ref, bc_ref, bm_ref, gc_ref, gm_ref, bg_ref, wbc_ref, wbm_ref, o_ref,
             dbc_ref, dbm_ref, dgg_ref, dy_ref, do_ref, dl_ref, dbg_ref):
        i = pl.program_id(0)

        @pl.when(i == 0)
        def _():
            dbg_ref[...] = jnp.zeros_like(dbg_ref)

        dmg = lax.dot_general(dz_ref[...], wo_ref[...], NT, preferred_element_type=F32)
        bgv = bg_ref[...]
        sc = jax.nn.sigmoid(gc_ref[...].astype(F32) + bgv[0:1])
        sm = jax.nn.sigmoid(gm_ref[...].astype(F32) + bgv[1:2])
        dbc = (dmg * sc).astype(BF)
        dbm = (dmg * sm).astype(BF)
        dgc = dmg * bc_ref[...].astype(F32) * sc * (1.0 - sc)
        dgm = dmg * bm_ref[...].astype(F32) * sm * (1.0 - sm)
        dbc_ref[...] = dbc
        dbm_ref[...] = dbm
        dgg_ref[...] = jnp.concatenate([dgc, dgm], axis=1).astype(BF)
        dbg_ref[...] += jnp.concatenate([jnp.sum(dgc, axis=0, keepdims=True), jnp.sum(dgm, axis=0, keepdims=True)], axis=0)
        dy_ref[...] = lax.dot_general(dbc, wbc_ref[...], NT, preferred_element_type=F32)
        do = lax.dot_general(dbm, wbm_ref[...], NT, preferred_element_type=F32)
        do_ref[...] = do.astype(BF)
        prod = do * o_ref[...].astype(F32)
        parts = []
        for h in range(nh):
            d = jnp.sum(prod[:, h * KV_PAD:(h + 1) * KV_PAD], axis=1, keepdims=True)
            parts.append(jnp.broadcast_to(d, (TMH, LANES)))
        dl_ref[...] = jnp.concatenate(parts, axis=1)

    def rows(n, col=0):
        return pl.BlockSpec((TMH, n), functools.partial(lambda i, col: (i, col), col=col))

    def whole(r, c):
        return pl.BlockSpec((r, c), lambda i: (0, 0))

    return _pcall(
        body, name=name, grid=(tp // TMH,),
        in_specs=[rows(D_MODEL), whole(D_MODEL, D_MODEL), rows(D_MODEL), rows(D_MODEL), rows(D_MODEL, 2), rows(D_MODEL, 3),
                  whole(2, D_MODEL), whole(D_CONV, D_MODEL), whole(nh * KV_PAD, D_MODEL), rows(nh * KV_PAD)],
        out_specs=[rows(D_MODEL), rows(D_MODEL), rows(2 * D_MODEL, 1), rows(D_CONV), rows(nh * KV_PAD), rows(nh * LANES),
                   whole(2, D_MODEL)],
        out_shape=[_sds((tp, D_MODEL), BF), _sds((tp, D_MODEL), BF), _sds((tp, D_IN_PAD), BF), _sds((tp, D_CONV), F32),
                   _sds((tp, nh * KV_PAD), BF), _sds((tp, nh * LANES), F32), _sds((2, D_MODEL), F32)],
        compiler_params=_params(1),
    )(dzb, wo, bc, bm, p, p, bg, wbc, wbm_p, o2)


def attn_bwd(q2, kv, kr, do2, lse, dl, *, name, side=None):
    tp = q2.shape[0]
    nh = MLA_HEADS
    nb = tp // TM
    rep = TM // LANES

    def body(in_refs, out_refs, scratch):
        q_ref, kv_ref, kr_ref, do_ref, lse_ref, dl_ref = in_refs
        dq_ref, dkv_ref, dkr_ref = out_refs
        dq_acc, s0_ref, s1_ref, dp0_ref, dp1_ref = scratch
        qi = pl.program_id(1)
        s_refs, dp_refs = (s0_ref, s1_ref), (dp0_ref, dp1_ref)

        @pl.when(qi == 0)
        def _():
            dkv_ref[...] = jnp.zeros_like(dkv_ref)
            dkr_ref[...] = jnp.zeros_like(dkr_ref)

        dq_acc[...] = jnp.zeros_like(dq_acc)

        def prefetch(k, slot):
            kvb = kv_ref[_key_rows(k), :]
            k2 = jnp.concatenate([kvb, kr_ref[_key_rows(k), :]], axis=1)
            s_refs[slot][...] = lax.dot_general(q_ref[...], k2, NT, preferred_element_type=F32)
            dp_refs[slot][...] = lax.dot_general(do_ref[...], kvb, NT, preferred_element_type=F32)

        def process(k, slot, diagonal):
            rows = _key_rows(k)
            s = s_refs[slot][...]
            if diagonal:
                s = _causal_mask(s)
            p = jnp.exp2(s - jnp.tile(lse_ref[...], (1, rep)))
            dsb = (p * (dp_refs[slot][...] - jnp.tile(dl_ref[...], (1, rep)))).astype(BF)
            dk2 = lax.dot_general(dsb, q_ref[...], TN, preferred_element_type=F32) * LN2
            dkv_ref[rows, :] += lax.dot_general(p.astype(BF), do_ref[...], TN, preferred_element_type=F32) + dk2[:, :KV_PAD]
            dkr_ref[rows, :] += dk2[:, KV_PAD:KV_PAD + LANES]
            k2 = jnp.concatenate([kv_ref[rows, :], kr_ref[rows, :]], axis=1)
            dq_acc[...] += jnp.dot(dsb, k2, preferred_element_type=F32)

        _pipelined_key_blocks(qi, prefetch, process)
        dq_ref[...] = dq_acc[...]

    def qrow(n):
        return pl.BlockSpec((TM, n), lambda h, qi: (qi, h))

    def head(n):
        return pl.BlockSpec((tp, n), lambda h, qi: (0, h))

    return _side_call(
        body, side, name=name, grid=(nh, nb),
        in_specs=[qrow(Q_PAD), head(KV_PAD), pl.BlockSpec((tp, LANES), lambda h, qi: (0, 0)), qrow(KV_PAD), qrow(LANES), qrow(LANES)],
        out_specs=[qrow(Q_PAD), head(KV_PAD), head(LANES)],
        out_shape=[_sds((tp, nh * Q_PAD), F32), _sds((tp, nh * KV_PAD), F32), _sds((tp, nh * LANES), F32)],
        scratch_shapes=[pltpu.VMEM((TM, Q_PAD), F32)] + [pltpu.VMEM((TM, TM), F32)] * 4, args=(q2, kv, kr, do2, lse, dl),
    )


def _rms_bwd(x, g, dy):
    r = lax.rsqrt(jnp.mean(x * x, axis=-1, keepdims=True) + RMS_EPS)
    gy = dy * g
    dx = r * gy - x * (r * r * r) * jnp.mean(x * gy, axis=-1, keepdims=True)
    return dx, jnp.sum(dy * x * r, axis=0, keepdims=True)


def mla_prep_bwd(dq2, dkv, dkr, p, gq, gkv, wuq_p, wukv, tabs_bwd, dp, *, name):
    tp = dq2.shape[0]
    nh = MLA_HEADS

    def body(dq_ref, dkv_ref, dkr_ref, cq_ref, ckv_ref, gq_ref, gkv_ref, wuq_ref, wukv_ref,
             cq_t, s1q_t, s2q_t, ck_t, s1k_t, s2k_t, dp_in_ref, dqb_ref, dsm_ref, dgq_ref, dgkv_ref):
        i = pl.program_id(0)

        @pl.when(i == 0)
        def _():
            dgq_ref[...] = jnp.zeros_like(dgq_ref)
            dgkv_ref[...] = jnp.zeros_like(dgkv_ref)

        dqb = _rope(dq_ref[...], cq_t[...], s1q_t[...], s2q_t[...], nh).astype(BF)
        dqb_ref[...] = dqb
        dcqn = lax.dot_general(dqb, wuq_ref[...], NT, preferred_element_type=F32)
        dcq, dgq = _rms_bwd(cq_ref[...].astype(F32), gq_ref[...], dcqn)
        dckvn = lax.dot_general(dkv_ref[...].astype(BF), wukv_ref[...], NT, preferred_element_type=F32)
        dckv, dgkv = _rms_bwd(ckv_ref[...].astype(F32), gkv_ref[...], dckvn)
        dkr_heads = dkr_ref[...]
        dkr_sum = dkr_heads[:, :LANES]
        for h in range(1, nh):
            dkr_sum = dkr_sum + dkr_heads[:, h * LANES:(h + 1) * LANES]
        dkr = _rope(dkr_sum, ck_t[...], s1k_t[...], s2k_t[...], 1)
        dsm_ref[...] = jnp.concatenate([dcq, dckv, dkr], axis=1).astype(BF)
        dgq_ref[...] += dgq
        dgkv_ref[...] += dgkv

    def rows(n, col=0):
        return pl.BlockSpec((TMH, n), functools.partial(lambda i, col: (i, col), col=col))

    return _pcall(
        body, name=name, grid=(tp // TMH,),
        in_specs=[rows(nh * Q_PAD), rows(nh * KV_PAD), rows(nh * LANES), rows(Q_LORA, 1536 // Q_LORA), rows(KV_LORA, 1792 // KV_LORA),
                  _row_vec(Q_LORA), _row_vec(KV_LORA),
                  pl.BlockSpec((Q_LORA, nh * Q_PAD), lambda i: (0, 0)), pl.BlockSpec((KV_LORA, nh * KV_PAD), lambda i: (0, 0)),
                  rows(Q_PAD), rows(Q_PAD), rows(Q_PAD), rows(LANES), rows(LANES), rows(LANES), pl.BlockSpec(memory_space=pl.ANY)],
        out_specs=[rows(nh * Q_PAD), rows(Q_LORA + KV_LORA + LANES, 1536 // (Q_LORA + KV_LORA + LANES)), _row_vec(Q_LORA),
                   _row_vec(KV_LORA)],
        out_shape=[_sds((tp, nh * Q_PAD), BF), _sds(dp.shape, dp.dtype), _sds((1, Q_LORA), F32), _sds((1, KV_LORA), F32)],
        input_output_aliases={15: 1}, compiler_params=_params(1),
    )(dq2, dkv, dkr, p, p, gq, gkv, wuq_p, wukv, *tabs_bwd, dp)


def conv_bwd(dy, p, conv, w, dp, *, name):
    tp = dy.shape[0]
    nb = tp // TM

    def body(dy_ref, b_ref, c_ref, h_ref, cv_ref, w_ref, dp_in_ref, dp_ref, dw0_ref, dw1_ref, dw2_ref, dbuf):
        i = pl.program_id(0)

        @pl.when(i == 0)
        def _():
            dbuf[TM:TM + 8, :] = jnp.zeros((8, D_CONV), F32)
            dw0_ref[...] = jnp.zeros_like(dw0_ref)
            dw1_ref[...] = jnp.zeros_like(dw1_ref)
            dw2_ref[...] = jnp.zeros_like(dw2_ref)

        dyv = dy_ref[...]
        c = c_ref[...].astype(F32)
        hh = h_ref[...].astype(F32)
        dconv = dyv * b_ref[...].astype(F32)
        dbuf[0:TM, :] = dconv
        d1 = dbuf[pl.ds(1, TM), :]
        d2 = dbuf[pl.ds(2, TM), :]
        w_all = w_ref[...]
        de = w_all[2:3] * dconv + w_all[1:2] * d1 + w_all[0:1] * d2
        e = c * hh
        dp_ref[...] = jnp.concatenate([dyv * cv_ref[...].astype(F32), de * hh, de * c], axis=1).astype(BF)
        dw0_ref[...] += jnp.sum(d2 * e, axis=0, keepdims=True)
        dw1_ref[...] += jnp.sum(d1 * e, axis=0, keepdims=True)
        dw2_ref[...] += jnp.sum(dconv * e, axis=0, keepdims=True)
        dbuf[TM:TM + 8, :] = dbuf[0:8, :]

    def col(j):
        return pl.BlockSpec((TM, D_CONV), functools.partial(lambda i, j: (nb - 1 - i, j), j=j))

    return _pcall(
        body, name=name, grid=(nb,),
        in_specs=[col(0), col(0), col(1), col(2), col(0), pl.BlockSpec((3, D_CONV), lambda i: (0, 0)),
                  pl.BlockSpec(memory_space=pl.ANY)],
        out_specs=[pl.BlockSpec((TM, 3 * D_CONV), lambda i: (nb - 1 - i, 0))] + [_row_vec(D_CONV)] * 3,
        out_shape=[_sds(dp.shape, dp.dtype)] + [_sds((1, D_CONV), F32)] * 3, input_output_aliases={6: 0},
        scratch_shapes=[pltpu.VMEM((TM + 8, D_CONV), F32)], compiler_params=_params(1),
    )(dy, p, p, p, conv, w, dp)


def adamw(w, g, m, v, *, name):
    r, c = w.shape
    tr = r
    for cand in (256, 128, 64, 32, 16, 8):
        if r % cand == 0 and r > cand:
            tr = cand
            break

    def body(w_ref, g_ref, m_ref, v_ref, d_ref, nm_ref, nv_ref):
        gv = g_ref[...]
        nm = ADAM_B1 * m_ref[...] + (1.0 - ADAM_B1) * gv
        nv = ADAM_B2 * v_ref[...] + (1.0 - ADAM_B2) * (gv * gv)
        m_hat = nm / (1.0 - ADAM_B1 ** ADAM_STEP)
        v_hat = nv / (1.0 - ADAM_B2 ** ADAM_STEP)
        d_ref[...] = -ADAM_LR * (m_hat / (jnp.sqrt(v_hat) + ADAM_EPS) + ADAM_WD * w_ref[...])
        nm_ref[...] = nm
        nv_ref[...] = nv

    blk = pl.BlockSpec((tr, c), lambda i: (i, 0))
    return _pcall(
        body, name=name, grid=(r // tr,), in_specs=[blk] * 4, out_specs=[blk] * 3,
        out_shape=[_sds((r, c), F32)] * 3, compiler_params=_params(1),
    )(w, g, m, v)


HBM_SPEC = pl.BlockSpec(memory_space=pltpu.HBM)


def _place():
    return lax.axis_index("x"), lax.axis_index("y"), lax.axis_index("c")


def _other_chips(x, y):
    return [(1 - x, y), (x, 1 - y), (1 - x, 1 - y)]


def _half(ref_or_shape_rows, c):
    return pl.ds(c * (ref_or_shape_rows // 2), ref_or_shape_rows // 2)


def gather_side(items):
    n = len(items)
    shards = [s for s, _ in items]
    layers = [l for _, l in items]

    def program(x_refs, o_refs, send_sems, recv_sems):
        x, y, c = _place()
        me = 2 * x + y
        chips = _other_chips(x, y)

        def copy(sem, src, dst, to):
            return pltpu.make_async_remote_copy(src_ref=src, dst_ref=dst, send_sem=send_sems.at[sem], recv_sem=recv_sems.at[sem],
                                                device_id=to, device_id_type=MESH)

        def src(i):
            return x_refs[i].at[layers[i], _half(x_refs[i].shape[1], c)]

        def dst(i, slot, cc):
            return o_refs[i].at[slot, _half(o_refs[i].shape[1], cc)]

        sends = [copy(6 * i + k, src(i), dst(i, me, c), (px, py, c)) for i in range(n) for k, (px, py) in enumerate(chips)]
        passed = [copy(6 * i + 3 + k, dst(i, 2 * px + py, c), dst(i, 2 * px + py, c), (x, y, 1 - c))
                  for k, (px, py) in enumerate(chips) for i in range(n)]

        def start():
            for cp in sends:
                cp.start()

        def finish():
            pos = 0
            for k, (px, py) in enumerate(chips):
                for i in range(n):
                    copy(6 * i + k, src(i), dst(i, 2 * px + py, c), (px, py, c)).wait_recv()
                    passed[pos].start()
                    pos += 1
            for k, (px, py) in enumerate(chips):
                for i in range(n):
                    copy(6 * i + 3 + k, dst(i, 2 * px + py, 1 - c), dst(i, 2 * px + py, 1 - c), (x, y, 1 - c)).wait_recv()
            for cp in sends + passed:
                cp.wait_send()

        return start, finish

    prefilled = [jnp.broadcast_to(s[l][None], (4,) + s.shape[1:]) for s, l in items]
    return shards, prefilled, 6 * n, program


def scatter_side(pss):
    n = len(pss)

    def program(p_refs, o_refs, send_sems, recv_sems):
        x, y, c = _place()
        me = 2 * x + y
        chips = _other_chips(x, y)

        def copy(i, k, j_src, j_dst, to):
            return pltpu.make_async_remote_copy(src_ref=p_refs[i].at[j_src], dst_ref=o_refs[i].at[j_dst],
                                                send_sem=send_sems.at[3 * i + k], recv_sem=recv_sems.at[3 * i + k],
                                                device_id=to, device_id_type=MESH)

        sends = [copy(i, k, 2 * px + py, me, (px, py, c)) for i in range(n) for k, (px, py) in enumerate(chips)]

        def start():
            for cp in sends:
                cp.start()

        def finish():
            for i in range(n):
                for k, (px, py) in enumerate(chips):
                    copy(i, k, me, 2 * px + py, (px, py, c)).wait_recv()
            for cp in sends:
                cp.wait_send()

        return start, finish

    xi, yi, _ = _place()
    own = jnp.arange(4)[:, None, None] == 2 * xi + yi
    prefilled = [jnp.where(own, p, jnp.zeros_like(p)) for p in pss]
    return list(pss), prefilled, 3 * n, program


def exchange_alone(side, *, name):
    inputs, prefilled, n_sems, program = side
    a, b = len(inputs), len(prefilled)

    def body(*refs):
        start, finish = program(refs[:a], refs[a + b:a + 2 * b], refs[-2], refs[-1])
        start()
        finish()

    return _pcall(
        body, name=name, in_specs=[HBM_SPEC] * (a + b), out_specs=[HBM_SPEC] * b, out_shape=[_sds(p.shape, p.dtype) for p in prefilled],
        input_output_aliases={a + i: i for i in range(b)}, scratch_shapes=[pltpu.SemaphoreType.DMA((n_sems,))] * 2,
    )(*inputs, *prefilled)


def pair_exchange(gs, *, name):
    n = len(gs)

    def body(*refs):
        g_refs, o_refs = refs[:n], refs[n:2 * n]
        send_sems, recv_sems = refs[2 * n:]
        x, y, c = _place()
        cps = [pltpu.make_async_remote_copy(src_ref=g_refs[i].at[:, _half(g_refs[i].shape[1], 1 - c)], dst_ref=o_refs[i],
                                            send_sem=send_sems.at[i], recv_sem=recv_sems.at[i], device_id=(x, y, 1 - c),
                                            device_id_type=MESH)
               for i in range(n)]
        for cp in cps:
            cp.start()
        for cp in cps:
            cp.wait()

    return _pcall(
        body, name=name, in_specs=[HBM_SPEC] * n, out_specs=[HBM_SPEC] * n,
        out_shape=[_sds((4, g.shape[1] // 2, g.shape[2]), g.dtype) for g in gs],
        scratch_shapes=[pltpu.SemaphoreType.DMA((n,)), pltpu.SemaphoreType.DMA((n,))],
    )(*gs)


def _comm_rows(a, b, itemsize):
    return a // 2 if a * b * itemsize > (3 << 19) and a % 16 == 0 else a


def pair_add(g, s1, c_idx, *, name):
    n, a, b = g.shape
    ah = a // 2
    ta = _comm_rows(ah, b, 2)
    nblk = ah // ta

    def body(c_ref, g_ref, s_ref, o_ref):
        o_ref[...] = (g_ref[...].astype(F32) + s_ref[...].astype(F32)).astype(o_ref.dtype)

    grid_spec = pltpu.PrefetchScalarGridSpec(
        num_scalar_prefetch=1, grid=(n, nblk),
        in_specs=[pl.BlockSpec((1, ta, b), lambda j, i, c_ref: (j, c_ref[0] * nblk + i, 0)),
                  pl.BlockSpec((1, ta, b), lambda j, i, c_ref: (j, i, 0))],
        out_specs=pl.BlockSpec((1, ta, b), lambda j, i, c_ref: (j, i, 0)),
    )
    return _pcall(body, name=name, grid_spec=grid_spec, out_shape=_sds((n, ah, b), g.dtype), compiler_params=_params(2))(
        c_idx, g, s1)


def sum_chunks(s2, *, name):
    n, a, b = s2.shape
    ta = _comm_rows(a, b, 4)

    def body(s_ref, o_ref):
        acc = s_ref[0].astype(F32)
        for j in range(1, n):
            acc = acc + s_ref[j].astype(F32)
        o_ref[...] = acc

    return _pcall(
        body, name=name, grid=(a // ta,), in_specs=[pl.BlockSpec((n, ta, b), lambda i: (0, i, 0))],
        out_specs=pl.BlockSpec((ta, b), lambda i: (i, 0)), out_shape=_sds((a, b), F32), compiler_params=_params(1),
    )(s2)


def pair_gather(rcs, *, name):
    n = len(rcs)

    def body(*refs):
        r_refs, o_refs = refs[:n], refs[2 * n:3 * n]
        send_sems, recv_sems = refs[3 * n:]
        x, y, c = _place()

        def copy(i, half):
            return pltpu.make_async_remote_copy(src_ref=r_refs[i], dst_ref=o_refs[i].at[half], send_sem=send_sems.at[i],
                                                recv_sem=recv_sems.at[i], device_id=(x, y, 1 - c), device_id_type=MESH)

        sends = [copy(i, c) for i in range(n)]
        for cp in sends:
            cp.start()
        for i in range(n):
            copy(i, 1 - c).wait_recv()
        for cp in sends:
            cp.wait_send()

    prefilled = [jnp.broadcast_to(r[None], (2,) + r.shape) for r in rcs]
    return _pcall(
        body, name=name, in_specs=[HBM_SPEC] * (2 * n), out_specs=[HBM_SPEC] * n,
        out_shape=[_sds(p.shape, p.dtype) for p in prefilled], input_output_aliases={n + i: i for i in range(n)},
        scratch_shapes=[pltpu.SemaphoreType.DMA((n,)), pltpu.SemaphoreType.DMA((n,))],
    )(*rcs, *prefilled)


def exchange_small(arrs, *, reduce, name):
    n = len(arrs)

    def body(*refs):
        v_refs, o_refs = refs[:n], refs[n:2 * n]
        bufs = refs[2 * n:3 * n] if reduce else o_refs
        send_sems, recv_sems = refs[-2:]
        x, y, c = _place()
        me = 4 * x + 2 * y + c
        for i in range(n):
            bufs[i][me] = v_refs[i][...]

        def peer(k):
            dx, dy, dc = (k >> 2) & 1, (k >> 1) & 1, k & 1
            return (1 - x if dx else x, 1 - y if dy else y, 1 - c if dc else c)

        def copy(i, k, slot):
            return pltpu.make_async_remote_copy(src_ref=v_refs[i], dst_ref=bufs[i].at[slot], send_sem=send_sems.at[7 * i + k - 1],
                                                recv_sem=recv_sems.at[7 * i + k - 1], device_id=peer(k), device_id_type=MESH)

        sends = [copy(i, k, me) for i in range(n) for k in range(1, 8)]
        for cp in sends:
            cp.start()
        for i in range(n):
            for k in range(1, 8):
                px, py, pc = peer(k)
                copy(i, k, 4 * px + 2 * py + pc).wait_recv()
        for cp in sends:
            cp.wait_send()
        if reduce:
            for i in range(n):
                acc = bufs[i][0]
                for d in range(1, 8):
                    acc = acc + bufs[i][d]
                o_refs[i][...] = acc

    vmem = pl.BlockSpec(memory_space=pltpu.VMEM)
    stacked = [(8,) + a.shape for a in arrs]
    return _pcall(
        body, name=name, in_specs=[vmem] * n, out_specs=[vmem] * n,
        out_shape=[_sds(a.shape if reduce else s, F32) for a, s in zip(arrs, stacked)],
        scratch_shapes=([pltpu.VMEM(s, F32) for s in stacked] if reduce else [])
        + [pltpu.SemaphoreType.DMA((7 * n,)), pltpu.SemaphoreType.DMA((7 * n,))],
    )(*arrs)


def _pad_rows(n, mult):
    return -(-n // mult) * mult


def _chip_major(g, b):
    return g.reshape(g.shape[0], 4, b).transpose(1, 0, 2)


def _rope_tables(tp):
    inv_freq = 1.0 / (ROPE_BASE ** (jnp.arange(0, QK_ROPE, 2, dtype=F32) / QK_ROPE))
    ang = jnp.arange(tp, dtype=F32)[:, None] * inv_freq[None, :]
    cos, sin = jnp.cos(ang), jnp.sin(ang)
    one = lambda n: jnp.ones((tp, n), F32)
    zero = lambda n: jnp.zeros((tp, n), F32)
    cq = jnp.concatenate([one(128), cos, cos, one(96)], axis=1)
    s1q = jnp.concatenate([zero(144), sin, zero(96)], axis=1)
    s2q = jnp.concatenate([zero(128), -sin, zero(112)], axis=1)
    ck = jnp.concatenate([cos, cos, zero(96)], axis=1)
    s1k = jnp.concatenate([zero(16), sin, zero(96)], axis=1)
    s2k = jnp.concatenate([-sin, zero(112)], axis=1)
    fwd = (cq * (ATT_SCALE * LOG2E), s1q * (ATT_SCALE * LOG2E), s2q * (ATT_SCALE * LOG2E), ck, s1k, s2k)
    bwd = (cq * ATT_SCALE, -s1q * ATT_SCALE, -s2q * ATT_SCALE, ck, -s1k, -s2k)
    return fwd, bwd


def _pad_w_in(w):
    return jnp.concatenate([w[:, :1952], jnp.zeros((w.shape[0], 96), w.dtype), w[:, 1952:]], axis=1)


def _pad_w_uq(w):
    w = w.reshape(Q_LORA, MLA_HEADS, QK_NOPE + QK_ROPE)
    z = lambda n: jnp.zeros((Q_LORA, MLA_HEADS, n), w.dtype)
    return jnp.concatenate([w[..., :QK_NOPE], z(64), w[..., QK_NOPE:], z(96)], axis=-1).reshape(Q_LORA, MLA_HEADS * Q_PAD)


def _unpad_w_uq(w):
    w = w.reshape(Q_LORA, MLA_HEADS, Q_PAD)
    return jnp.concatenate([w[..., :QK_NOPE], w[..., 128:128 + QK_ROPE]], axis=-1).reshape(Q_LORA, MLA_HEADS * (QK_NOPE + QK_ROPE))


def _pad_w_br_mla(w):
    w = w.reshape(MLA_HEADS, V_HEAD, D_MODEL)
    return jnp.concatenate([jnp.zeros_like(w), w], axis=1).reshape(MLA_HEADS * KV_PAD, D_MODEL)


def _unpad_w_br_mla(w):
    return w.reshape(MLA_HEADS, KV_PAD, D_MODEL)[:, V_HEAD:].reshape(MLA_HEADS * V_HEAD, D_MODEL)


def _riding(hooks, where, l, *args):
    make = hooks.get(where)
    ride = make(l, *args) if make else None
    return ride if ride else (None, lambda results: None)


def _layer_fwd(l, st, xprev, gp, bp, hb, w, tabs, hooks):
    ln_g, ln_b = w["ln_g"], w["ln_b"]
    lg = lambda k: ln_g[l, k][None]
    lb = lambda k: ln_b[l, k][None]
    s = {}
    s["x0"], s["gp0"], s["bp0"], s["hb0"] = xprev, gp, bp, hb
    side, got = _riding(hooks, "ffn1_fwd", l)
    s["g1"], s["u1"], s["a1"], *extras = ffn_up(hb, w["ffn1_w_up"][l], name="ffn_up", side=side)
    got(extras)
    s["xh1"], s["rs1"], s["hb1"] = down_ln(s["a1"], w["ffn1_w_down"][l], xprev, gp, bp, lg(0), lb(0), name="ffn_down_ln")
    s["p"] = mm_rows([(s["hb1"], w["mix_w_in"][l], False, 0)], D_IN_PAD, name="mix_in", tn=1024, out_dtype=BF)
    gq, gkv = w["q_norm_g"][l][None], w["kv_norm_g"][l][None]
    s["cqn"], s["ckvn"], s["q2"], s["kv"], s["kr"] = mla_prep(s["p"], gq, gkv, w["w_uq"][l], w["w_ukv"][l], tabs, name="mla_prep")
    side, got = _riding(hooks, "attn_fwd", l)
    s["o2"], s["lse"], *extras = attn_fwd(s["q2"], s["kv"], s["kr"], name="attn_fwd", side=side)
    got(extras)
    s["ycv"], s["conv"] = conv_fwd(s["p"], w["conv_w"][l], name="conv_fwd")
    s["bc"], s["bm"], s["mg"], s["xh2"], s["rs2"], s["hb2"] = merge_out_ln(
        s["ycv"], s["o2"], s["p"], w["mix_b_gate"][l], w["w_br_conv"][l], w["w_br_mla"][l], w["w_o"][l],
        s["xh1"], lg(0), lb(0), lg(1), lb(1), name="merge_out_ln")
    s["g2"], s["u2"], s["a2"] = ffn_up(s["hb2"], w["ffn2_w_up"][l], name="ffn_up")
    s["xh3"], s["rs3"], s["hb3"] = down_ln(s["a2"], w["ffn2_w_down"][l], s["xh2"], lg(1), lb(1), lg(2), lb(2), name="ffn_down_ln")
    st.append(s)
    return s["xh3"], lg(2), lb(2), s["hb3"]


def _ffn_bwd(which, l, g, hooks, dh, w_up, w_down, ln_gain, hb_in, gate, up, act, xh, rs):
    dzb, dgam, dbet = ln_bwd(dh, xh, rs, ln_gain, branch_scale=0.5, name="ln_bwd")
    g[which + "_w_down"] = tn_mm(act, dzb, tm=D_FF // 2, name="dw_down", shard=("rows", D_FF // 4))
    side, got = _riding(hooks, which + "_down_bwd", l, g)
    dgate, dup, *extras = ffn_down_bwd(dzb, w_down, gate, up, name="ffn_down_bwd", side=side)
    got(extras)
    d_w = tn_mm(hb_in, dgate, tm=512, name="dw_up", shard=("cols", D_FF // 2), slot0=0)
    g[which + "_w_up"] = tn_mm(hb_in, dup, tm=512, name="dw_up", shard=("cols", D_FF // 2), slot0=2, dst=d_w)
    side, got = _riding(hooks, which + "_up_bwd", l, g)
    dh_in = mm_rows([(dgate, w_up, True, 0), (dup, w_up, True, 1)], D_MODEL, name="ffn_up_bwd", tn=512, addend=dzb, add_scale=2.0 * ALPHA,
                    side=side)
    if side is not None:
        dh_in, *extras = dh_in
        got(extras)
    return dh_in, dgam, dbet


def _layer_bwd(l, s, dh, w, tabs_bwd, hooks):
    ln_g = w["ln_g"]
    lg = lambda k: ln_g[l, k][None]
    g = {}
    dh, dg2, db2 = _ffn_bwd("ffn2", l, g, hooks, dh, w["ffn2_w_up"][l], w["ffn2_w_down"][l], lg(2), s["hb2"], s["g2"], s["u2"],
                            s["a2"], s["xh3"], s["rs3"])
    dzb, dg1, db1 = ln_bwd(dh, s["xh2"], s["rs2"], lg(1), branch_scale=1.0, name="ln_bwd")
    g["w_o"] = tn_mm(s["mg"], dzb, tm=1024, name="dw_o", shard=("rows", D_MODEL // 4))
    dbc, dbm, dp, dycv, do2, dl, g["mix_b_gate"] = merge_bwd(
        dzb, w["w_o"][l], s["bc"], s["bm"], s["p"], w["mix_b_gate"][l], w["w_br_conv"][l], w["w_br_mla"][l], s["o2"], name="merge_bwd")
    g["w_br_conv"] = tn_mm(s["ycv"], dbc, tm=512, name="dw_br_conv", shard=("cols", D_MODEL // 4))
    g["w_br_mla"] = _chip_major(_unpad_w_br_mla(tn_mm(s["o2"], dbm, tm=1024, name="dw_br_mla")), D_MODEL // 4)
    side, got = _riding(hooks, "attn_bwd", l, g)
    dq2, dkv, dkr, *extras = attn_bwd(s["q2"], s["kv"], s["kr"], do2, s["lse"], dl, name="attn_bwd", side=side)
    got(extras)
    gq, gkv = w["q_norm_g"][l][None], w["kv_norm_g"][l][None]
    dqb, dp, g["q_norm_g"], g["kv_norm_g"] = mla_prep_bwd(dq2, dkv, dkr, s["p"], gq, gkv, w["w_uq"][l], w["w_ukv"][l], tabs_bwd, dp,
                                                          name="mla_prep_bwd")
    g["w_uq"] = _chip_major(_unpad_w_uq(tn_mm(s["cqn"], dqb, tm=Q_LORA, name="dw_uq")), MLA_HEADS * (QK_NOPE + QK_ROPE) // 4)
    g["w_ukv"] = tn_mm(s["ckvn"], dkv, tm=KV_LORA, name="dw_ukv", shard=("cols", MLA_HEADS * KV_PAD // 4))
    dp, dw0, dw1, dw2 = conv_bwd(dycv, s["p"], s["conv"], w["conv_w"][l], dp, name="conv_bwd")
    g["conv_w"] = jnp.concatenate([dw0, dw1, dw2], axis=0)
    d_in = tn_mm(s["hb1"], dp, tm=512, name="dw_in")
    g["mix_w_in"] = _chip_major(jnp.concatenate([d_in[:, :1952], d_in[:, 2048:]], axis=1), D_IN // 4)
    dh = mm_rows([(dp, w["mix_w_in"][l], True, 0)], D_MODEL, name="mix_in_bwd", tn=512, addend=dzb, add_scale=ALPHA)
    dh, dg0, db0 = _ffn_bwd("ffn1", l, g, hooks, dh, w["ffn1_w_up"][l], w["ffn1_w_down"][l], lg(0), s["hb0"], s["g1"], s["u1"],
                            s["a1"], s["xh1"], s["rs1"])
    g["ln_g"] = jnp.concatenate([dg0, dg1, dg2], axis=0)
    g["ln_b"] = jnp.concatenate([db0, db1, db2], axis=0)
    return dh, g


BIG = ("ffn1_w_up", "ffn1_w_down", "mix_w_in", "w_uq", "w_ukv", "w_br_conv", "w_br_mla", "w_o", "ffn2_w_up", "ffn2_w_down")
BIG_AXIS = (2, 1, 2, 2, 2, 2, 2, 1, 2, 1)
FFN1_MATRICES = ("ffn1_w_up", "ffn1_w_down")
MIXER_MATRICES = ("mix_w_in", "w_uq", "w_ukv", "w_br_conv", "w_br_mla", "w_o")
FFN2_MATRICES = ("ffn2_w_up", "ffn2_w_down")
SMALL_SHARDED = ("meta_tokens", "mix_b_gate", "conv_w", "ln_g", "ln_b")
SMALL_REPLICATED = ("q_norm_g", "kv_norm_g")
WEIGHTS = ("meta_tokens", "ffn1_w_up", "ffn1_w_down", "mix_w_in", "mix_b_gate", "conv_w", "q_norm_g", "w_uq", "kv_norm_g", "w_ukv",
           "w_br_conv", "w_br_mla", "w_o", "ffn2_w_up", "ffn2_w_down", "ln_g", "ln_b")


def _view2d(a):
    return a.reshape(-1, a.shape[-1])


def _local_grads(x_row, target_row, w, hooks=None):
    hooks = hooks or {}
    seq = x_row.shape[0]
    t_real = N_META + seq
    tp = _pad_rows(t_real, TM)
    pad = tp - t_real
    h0 = jnp.concatenate([w["meta_tokens"], x_row, jnp.zeros((pad, D_MODEL), F32)], axis=0)
    target_p = jnp.concatenate([jnp.zeros((N_META, D_MODEL), F32), target_row, jnp.zeros((pad, D_MODEL), F32)], axis=0)
    tabs, tabs_bwd = _rope_tables(tp)
    ones = jnp.ones((1, D_MODEL), F32)
    zeros = jnp.zeros((1, D_MODEL), F32)
    saved = []
    cur = (h0, ones, zeros, h0.astype(BF))
    for l in range(DEPTH):
        cur = _layer_fwd(l, saved, *cur, w, tabs, hooks)
    dh, loss_acc = loss_grad(cur[0], cur[1], cur[2], target_p, seq, name="loss_grad")
    grads = [None] * DEPTH
    for l in reversed(range(DEPTH)):
        dh, grads[l] = _layer_bwd(l, saved[l], dh, w, tabs_bwd, hooks)
        if "layer_bwd_done" in hooks:
            hooks["layer_bwd_done"](l, grads[l])
    return loss_acc, dh[N_META:t_real], dh[:N_META], grads


def kernel(x, meta_tokens, ffn1_w_up, ffn1_w_down, mix_w_in, mix_b_gate, conv_w, q_norm_g, w_uq, kv_norm_g, w_ukv, w_br_conv, w_br_mla, w_o, ffn2_w_up, ffn2_w_down, ln_g, ln_b, loss_target, m_meta_tokens, m_ffn1_w_up, m_ffn1_w_down, m_mix_w_in, m_mix_b_gate, m_conv_w, m_q_norm_g, m_w_uq, m_kv_norm_g, m_w_ukv, m_w_br_conv, m_w_br_mla, m_w_o, m_ffn2_w_up, m_ffn2_w_down, m_ln_g, m_ln_b, v_meta_tokens, v_ffn1_w_up, v_ffn1_w_down, v_mix_w_in, v_mix_b_gate, v_conv_w, v_q_norm_g, v_w_uq, v_kv_norm_g, v_w_ukv, v_w_br_conv, v_w_br_mla, v_w_o, v_ffn2_w_up, v_ffn2_w_down, v_ln_g, v_ln_b):
    local = dict(meta_tokens=meta_tokens, ffn1_w_up=ffn1_w_up, ffn1_w_down=ffn1_w_down, mix_w_in=mix_w_in, mix_b_gate=mix_b_gate,
                 conv_w=conv_w, q_norm_g=q_norm_g, w_uq=w_uq, kv_norm_g=kv_norm_g, w_ukv=w_ukv, w_br_conv=w_br_conv,
                 w_br_mla=w_br_mla, w_o=w_o, ffn2_w_up=ffn2_w_up, ffn2_w_down=ffn2_w_down, ln_g=ln_g, ln_b=ln_b)
    mom_m = dict(zip(WEIGHTS, (m_meta_tokens, m_ffn1_w_up, m_ffn1_w_down, m_mix_w_in, m_mix_b_gate, m_conv_w, m_q_norm_g, m_w_uq,
                               m_kv_norm_g, m_w_ukv, m_w_br_conv, m_w_br_mla, m_w_o, m_ffn2_w_up, m_ffn2_w_down, m_ln_g, m_ln_b)))
    mom_v = dict(zip(WEIGHTS, (v_meta_tokens, v_ffn1_w_up, v_ffn1_w_down, v_mix_w_in, v_mix_b_gate, v_conv_w, v_q_norm_g, v_w_uq,
                               v_kv_norm_g, v_w_ukv, v_w_br_conv, v_w_br_mla, v_w_o, v_ffn2_w_up, v_ffn2_w_down, v_ln_g, v_ln_b)))
    xi, yi, ci = _place()
    chip = 2 * xi + yi

    shards = {n: local[n].astype(BF) for n in BIG}
    axis = dict(zip(BIG, BIG_AXIS))
    pad_layout = {"mix_w_in": _pad_w_in, "w_uq": _pad_w_uq, "w_br_mla": _pad_w_br_mla}
    w = {n: [None] * DEPTH for n in BIG}

    def fetch(keys):
        def install(gathered):
            for (n, l), g in zip(keys, gathered):
                if n in FFN1_MATRICES + FFN2_MATRICES:
                    w[n][l] = g
                    continue
                full = jnp.concatenate([g[j] for j in range(4)], axis=axis[n] - 1)
                w[n][l] = pad_layout[n](full) if n in pad_layout else full
        return gather_side([(shards[n], l) for n, l in keys]), install

    first, install_first = fetch([(n, 0) for n in FFN1_MATRICES])
    install_first(exchange_alone(first, name="gather_weights"))
    fetch_under = {("ffn1_fwd", 0): [(n, 0) for n in MIXER_MATRICES],
                   ("attn_fwd", 0): [(n, 0) for n in FFN2_MATRICES] + [(n, 1) for n in BIG]}
    hooks = {where: functools.partial(lambda l, where: fetch(fetch_under[where, l]) if (where, l) in fetch_under else None, where=where)
             for where in ("ffn1_fwd", "attn_fwd")}
    stacked = exchange_small([_view2d(local[n]) for n in SMALL_SHARDED], reduce=False, name="gather_small")
    for n, st in zip(SMALL_SHARDED, stacked):
        full = jnp.concatenate([st[2 * j] for j in range(4)], axis=-1)
        w[n] = full.reshape(local[n].shape[:-1] + (full.shape[-1],))
    for n in SMALL_REPLICATED:
        w[n] = local[n]

    c_idx = jnp.reshape(ci, (1,)).astype(jnp.int32)
    done, from_chips = {}, {}

    def send(keys, grad_of):
        glist = [grad_of[k] for k in keys]
        from_sibling = pair_exchange(glist, name="rs_pair_exchange")
        sums = [pair_add(a, s, c_idx, name="rs_pair_add") for a, s in zip(glist, from_sibling)]
        return scatter_side(sums), lambda results: from_chips.update(zip(keys, results))

    send_under = {"attn_bwd": FFN2_MATRICES + ("w_o", "w_br_conv", "w_br_mla"),
                  "ffn1_down_bwd": ("mix_w_in", "w_uq", "w_ukv", "ffn1_w_down"), "ffn1_up_bwd": ("ffn1_w_up",)}
    hooks["layer_bwd_done"] = lambda l, g: done.update({(n, l): g[n] for n in BIG})

    def send_hook(where):
        def hook(l, g):
            if l != 0:
                return None
            keys = [(n, 0) for n in send_under[where]] + ([(n, 1) for n in BIG] if where == "attn_bwd" else [])
            return send(keys, {**done, **{(n, 0): g[n] for n in send_under[where]}})
        return hook

    for where in send_under:
        hooks[where] = send_hook(where)

    loss_acc, grad_x, d_meta, grads = _local_grads(x[0], loss_target[0], w, hooks)
    grad_x = grad_x[None]
    keys = [(n, l) for n in BIG for l in range(DEPTH)]
    reduced = pair_gather([sum_chunks(from_chips[k], name="rs_sum") for k in keys], name="rs_pair_gather")
    reduced = {k: r.reshape(local[k[0]].shape[1:]) for k, r in zip(keys, reduced)}
    gshard = {n: jnp.stack([reduced[n, l] for l in range(DEPTH)]) for n in BIG}

    small_names = SMALL_SHARDED + SMALL_REPLICATED
    gsmall = {n: jnp.concatenate([grads[l][n] for l in range(DEPTH)], axis=0) for n in small_names if n != "meta_tokens"}
    gsmall["meta_tokens"] = d_meta
    small_red = exchange_small([gsmall[n] for n in small_names] + [loss_acc], reduce=True, name="reduce_small")
    loss = small_red[-1][0, 0]
    for n, full in zip(small_names, small_red[:-1]):
        if n in SMALL_SHARDED:
            sh = local[n].shape[-1]
            full = lax.dynamic_slice_in_dim(full, chip * sh, sh, axis=1)
        gshard[n] = full.reshape(local[n].shape)

    delta, new_m, new_v = {}, {}, {}
    for n in WEIGHTS:
        shape = local[n].shape
        d, nm, nv = adamw(_view2d(local[n]), _view2d(gshard[n]), _view2d(mom_m[n]), _view2d(mom_v[n]), name="adamw")
        delta[n], new_m[n], new_v[n] = d.reshape(shape), nm.reshape(shape), nv.reshape(shape)
    return (loss, grad_x, *[gshard[n] for n in WEIGHTS], *[delta[n] for n in WEIGHTS], *[new_m[n] for n in WEIGHTS],
            *[new_v[n] for n in WEIGHTS])
```

```python
import functools

import jax
import jax.numpy as jnp
from jax import lax
from jax.experimental import pallas as pl
from jax.experimental.pallas import tpu as pltpu

F32 = jnp.float32
BF = jnp.bfloat16
MESH = pl.DeviceIdType.MESH

D_MODEL = 1024
DEPTH = 2
N_META = 16
D_CONV = 512
MLA_HEADS = 8
QK_NOPE = 64
QK_ROPE = 32
V_HEAD = 64
Q_LORA = 256
KV_LORA = 128
ROPE_BASE = 10000.0
NEG_INF = -1e30
D_FF = 2816
ALPHA = (2 * DEPTH) ** 0.25
LN_EPS = 1e-5
RMS_EPS = 1e-6
ATT_SCALE = (QK_NOPE + QK_ROPE) ** -0.5
LOG2E = 1.4426950408889634
LN2 = 0.6931471805599453
D_IN = 4000
D_IN_PAD = 4096
Q_PAD = 256
KV_PAD = 128

ADAM_LR = 0.001
ADAM_B1 = 0.9
ADAM_B2 = 0.999
ADAM_EPS = 1e-08
ADAM_WD = 0.01
ADAM_STEP = 10

TM = 768
TMH = 384
LANES = 128
COMM_COLS = 512
COMM_ROW_BLOCK = 1472
VMEM_LIMIT_BYTES = 50 * 1024 * 1024

NT = (((1,), (1,)), ((), ()))
TN = (((0,), (0,)), ((), ()))


def _pcall(body, **kw):
    return pl.pallas_call(body, **kw)


def _params(n_axes):
    return pltpu.CompilerParams(dimension_semantics=("arbitrary",) * n_axes, vmem_limit_bytes=VMEM_LIMIT_BYTES)


def _sds(shape, dtype):
    return jax.ShapeDtypeStruct(shape, dtype)


def mm_rows(pairs, n_out, *, name, tn=None, addend=None, add_scale=1.0, out_dtype=F32, side=None):
    tp = pairs[0][0].shape[0]
    tn = tn or n_out
    in_specs, args = [], []
    for a, b, nt, kb in pairs:
        k = a.shape[1]
        in_specs.append(pl.BlockSpec((TM, k), lambda i, j: (i, 0)))
        if nt and b.ndim == 3:
            in_specs.append(pl.BlockSpec((2, tn, k // 2), functools.partial(lambda i, j, kb: (kb, j, 0), kb=kb)))
        elif nt:
            in_specs.append(pl.BlockSpec((tn, k), functools.partial(lambda i, j, kb: (j, kb), kb=kb)))
        else:
            in_specs.append(pl.BlockSpec((k, tn), lambda i, j: (0, j)))
        args += [a, b]
    if addend is not None:
        in_specs.append(pl.BlockSpec((TM, tn), lambda i, j: (i, j)))
        args.append(addend)
    n_pairs = len(pairs)
    nts = [p[2] for p in pairs]

    def body(refs, out_refs, scratch):
        o_ref = out_refs[0]
        acc = None
        for p in range(n_pairs):
            a = refs[2 * p][...].astype(BF)
            b = refs[2 * p + 1][...]
            if b.ndim == 3:
                b = jnp.concatenate([b[0], b[1]], axis=1)
            d = lax.dot_general(a, b, NT if nts[p] else (((1,), (0,)), ((), ())), preferred_element_type=F32)
            acc = d if acc is None else acc + d
        if addend is not None:
            acc = acc + add_scale * refs[2 * n_pairs][...].astype(F32)
        o_ref[...] = acc.astype(o_ref.dtype)

    out = _side_call(
        body, side, name=name, grid=(tp // TM, n_out // tn), in_specs=in_specs,
        out_specs=[pl.BlockSpec((TM, tn), lambda i, j: (i, j))], out_shape=[_sds((tp, n_out), out_dtype)],
        scratch_shapes=[], args=args,
    )
    return out if side is not None else out[0]


def tn_mm(a, b, *, tm, name, out_dtype=BF, shard=None, slot0=0, dst=None):
    tp, m = a.shape
    n = b.shape[1]
    nk = tp // TM
    if shard is None:
        pieces, out_block, out_index, out_full = 1, (tm, n), (lambda i, k: (i, 0)), (m, n)
    elif shard[0] == "cols":
        pieces = n // shard[1]
        out_block, out_full = (pieces, tm, shard[1]), (4, m, shard[1])
        out_index = lambda i, k: (slot0 // pieces, i, 0)
    else:
        pieces = tm // shard[1]
        out_block, out_full = (pieces, shard[1], n), (4, m // 4, n)
        out_index = lambda i, k: (i, 0, 0)

    def body(a_ref, b_ref, *rest):
        o_ref, acc_ref = rest[-2], rest[-1]
        k = pl.program_id(1)

        @pl.when(k == 0)
        def _():
            acc_ref[...] = jnp.zeros_like(acc_ref)

        acc_ref[...] += lax.dot_general(a_ref[...].astype(BF), b_ref[...].astype(BF), TN, preferred_element_type=F32)

        @pl.when(k == nk - 1)
        def _():
            if shard is None:
                o_ref[...] = acc_ref[...].astype(o_ref.dtype)
            elif shard[0] == "cols":
                for j in range(pieces):
                    o_ref[j] = acc_ref[:, j * shard[1]:(j + 1) * shard[1]].astype(o_ref.dtype)
            else:
                for j in range(pieces):
                    o_ref[j] = acc_ref[j * shard[1]:(j + 1) * shard[1], :].astype(o_ref.dtype)

    in_specs = [pl.BlockSpec((TM, tm), lambda i, k: (k, i)), pl.BlockSpec((TM, n), lambda i, k: (k, 0))]
    args = [a, b]
    aliases = {}
    if dst is not None:
        in_specs.append(pl.BlockSpec(memory_space=pl.ANY))
        args.append(dst)
        aliases = {2: 0}
    return _pcall(
        body, name=name, grid=(m // tm, nk), in_specs=in_specs, out_specs=pl.BlockSpec(out_block, out_index),
        out_shape=_sds(out_full, out_dtype), input_output_aliases=aliases,
        scratch_shapes=[pltpu.VMEM((tm, n), F32)], compiler_params=_params(2),
    )(*args)


def _ln_store(z, g_ref, b_ref, xh_ref, rs_ref, hb_ref):
    mu = jnp.mean(z, axis=-1, keepdims=True)
    zc = z - mu
    var = jnp.mean(zc * zc, axis=-1, keepdims=True)
    rstd = lax.rsqrt(var + LN_EPS)
    xh = zc * rstd
    xh_ref[...] = xh
    rs_ref[...] = rstd
    hb_ref[...] = (xh * g_ref[...] + b_ref[...]).astype(BF)


def _ln_out(tp, tm=TM):
    specs = [pl.BlockSpec((tm, D_MODEL), lambda i: (i, 0)), pl.BlockSpec((tm, 1), lambda i: (i, 0)),
             pl.BlockSpec((tm, D_MODEL), lambda i: (i, 0))]
    shapes = [_sds((tp, D_MODEL), F32), _sds((tp, 1), F32), _sds((tp, D_MODEL), BF)]
    return specs, shapes


def _row_vec(n):
    return pl.BlockSpec((1, n), lambda i: (0, 0))


def ffn_up(hb, wup, *, name, side=None):
    tp = hb.shape[0]
    tn = D_FF // 2
    nj = D_FF // tn

    def body(in_refs, out_refs, scratch):
        h_ref, wg_ref, wu_ref = in_refs
        g_ref, u_ref, a_ref = out_refs
        h = h_ref[...]
        g = jnp.dot(h, wg_ref[0], preferred_element_type=F32)
        u = jnp.dot(h, wu_ref[0], preferred_element_type=F32)
        g_ref[...] = g.astype(BF)
        u_ref[...] = u.astype(BF)
        a_ref[...] = (g * jax.nn.sigmoid(g) * u).astype(BF)

    blk = pl.BlockSpec((TM, tn), lambda i, j: (i, j))
    return _side_call(
        body, side, name=name, grid=(tp // TM, nj),
        in_specs=[pl.BlockSpec((TM, D_MODEL), lambda i, j: (i, 0)), pl.BlockSpec((1, D_MODEL, tn), lambda i, j: (j, 0, 0)),
                  pl.BlockSpec((1, D_MODEL, tn), lambda i, j: (j + nj, 0, 0))],
        out_specs=[blk, blk, blk], out_shape=[_sds((tp, D_FF), BF)] * 3, scratch_shapes=[], args=(hb, wup, wup),
    )


def down_ln(a, wd, xprev, gp, bp, g, b, *, name):
    tp = a.shape[0]

    def body(a_ref, wd_ref, xp_ref, gp_ref, bp_ref, g_ref, b_ref, xh_ref, rs_ref, hb_ref):
        wd = jnp.concatenate([wd_ref[j] for j in range(4)], axis=0)
        f = jnp.dot(a_ref[...], wd, preferred_element_type=F32)
        hprev = xp_ref[...] * gp_ref[...] + bp_ref[...]
        _ln_store(ALPHA * hprev + 0.5 * f, g_ref, b_ref, xh_ref, rs_ref, hb_ref)

    out_specs, out_shape = _ln_out(tp)
    return _pcall(
        body, name=name, grid=(tp // TM,),
        in_specs=[pl.BlockSpec((TM, D_FF), lambda i: (i, 0)), pl.BlockSpec((4, D_FF // 4, D_MODEL), lambda i: (0, 0, 0)),
                  pl.BlockSpec((TM, D_MODEL), lambda i: (i, 0))] + [_row_vec(D_MODEL)] * 4,
        out_specs=out_specs, out_shape=out_shape, compiler_params=_params(1),
    )(a, wd, xprev, gp, bp, g, b)


def _rope(x, c, s1, s2, reps):
    n = x.shape[1]
    if reps > 1:
        c, s1, s2 = (jnp.tile(t, (1, reps)) for t in (c, s1, s2))
    return x * c + pltpu.roll(x, 16, 1) * s1 + pltpu.roll(x, n - 16, 1) * s2


def _rms(x, g):
    r = lax.rsqrt(jnp.mean(x * x, axis=-1, keepdims=True) + RMS_EPS)
    return x * r * g, r


def mla_prep(p, gq, gkv, wuq_p, wukv, tabs, *, name):
    tp = p.shape[0]
    nh = MLA_HEADS

    def body(cq_ref, ckv_ref, kr_ref, gq_ref, gkv_ref, wuq_ref, wukv_ref, cq_t, s1q_t, s2q_t, ck_t, s1k_t, s2k_t,
             cqn_ref, ckvn_ref, q2_ref, kv_ref, krr_ref):
        cqn, _ = _rms(cq_ref[...].astype(F32), gq_ref[...])
        ckvn, _ = _rms(ckv_ref[...].astype(F32), gkv_ref[...])
        cqn = cqn.astype(BF)
        ckvn = ckvn.astype(BF)
        cqn_ref[...] = cqn
        ckvn_ref[...] = ckvn
        q = jnp.dot(cqn, wuq_ref[...], preferred_element_type=F32)
        q2_ref[...] = _rope(q, cq_t[...], s1q_t[...], s2q_t[...], nh).astype(BF)
        kv_ref[...] = jnp.dot(ckvn, wukv_ref[...], preferred_element_type=F32).astype(BF)
        krr_ref[...] = _rope(kr_ref[...].astype(F32), ck_t[...], s1k_t[...], s2k_t[...], 1).astype(BF)

    def rows(n, col=0):
        return pl.BlockSpec((TMH, n), functools.partial(lambda i, col: (i, col), col=col))

    return _pcall(
        body, name=name, grid=(tp // TMH,),
        in_specs=[rows(Q_LORA, 1536 // Q_LORA), rows(KV_LORA, 1792 // KV_LORA), rows(LANES, 1920 // LANES),
                  _row_vec(Q_LORA), _row_vec(KV_LORA),
                  pl.BlockSpec((Q_LORA, nh * Q_PAD), lambda i: (0, 0)), pl.BlockSpec((KV_LORA, nh * KV_PAD), lambda i: (0, 0)),
                  rows(Q_PAD), rows(Q_PAD), rows(Q_PAD), rows(LANES), rows(LANES), rows(LANES)],
        out_specs=[rows(Q_LORA), rows(KV_LORA), rows(nh * Q_PAD), rows(nh * KV_PAD), rows(LANES)],
        out_shape=[_sds((tp, Q_LORA), BF), _sds((tp, KV_LORA), BF), _sds((tp, nh * Q_PAD), BF),
                   _sds((tp, nh * KV_PAD), BF), _sds((tp, LANES), BF)],
        compiler_params=_params(1),
    )(p, p, p, gq, gkv, wuq_p, wukv, *tabs)


def _causal_mask(s):
    qpos = lax.broadcasted_iota(jnp.int32, (TM, TM), 0)
    kpos = lax.broadcasted_iota(jnp.int32, (TM, TM), 1)
    return jnp.where(kpos <= qpos, s, NEG_INF)


def _key_rows(k):
    return pl.ds(pl.multiple_of(k * TM, TM), TM)


def _pipelined_key_blocks(n, prefetch, process):
    prefetch(0, 0)

    def pair(j, carry):
        prefetch(2 * j + 1, 1)
        process(2 * j, 0, False)
        prefetch(2 * j + 2, 0)
        process(2 * j + 1, 1, False)
        return carry

    lax.fori_loop(0, n // 2, pair, 0)

    @pl.when(n % 2 == 1)
    def _():
        prefetch(n, 1)
        process(n - 1, 0, False)
        process(n, 1, True)

    @pl.when(n % 2 == 0)
    def _():
        process(n, 0, True)


def _side_call(body_main, side, *, name, grid, in_specs, out_specs, out_shape, scratch_shapes, args):
    n_in, n_out, n_scr = len(in_specs), len(out_specs), len(scratch_shapes)
    s_in, s_pre, n_sems, program = side if side is not None else ((), (), 0, None)
    a, b = len(s_in), len(s_pre)

    def body(*refs):
        in_refs = refs[:n_in]
        out_refs = refs[n_in + a + b:n_in + a + b + n_out]
        scr = refs[n_in + a + 2 * b + n_out:n_in + a + 2 * b + n_out + n_scr]
        if side is not None:
            side_in = refs[n_in:n_in + a]
            side_out = refs[n_in + a + b + n_out:n_in + a + 2 * b + n_out]
            start, finish = program(side_in, side_out, refs[-2], refs[-1])

            @pl.when((pl.program_id(0) == 0) & (pl.program_id(1) == 0))
            def _():
                start()

        body_main(in_refs, out_refs, scr)
        if side is not None:
            @pl.when((pl.program_id(0) == grid[0] - 1) & (pl.program_id(1) == grid[1] - 1))
            def _():
                finish()

    sems = [pltpu.SemaphoreType.DMA((n_sems,))] * 2 if side is not None else []
    return _pcall(
        body, name=name, grid=grid, in_specs=list(in_specs) + [HBM_SPEC] * (a + b), out_specs=list(out_specs) + [HBM_SPEC] * b,
        out_shape=list(out_shape) + [_sds(p.shape, p.dtype) for p in s_pre],
        input_output_aliases={n_in + a + i: n_out + i for i in range(b)},
        scratch_shapes=list(scratch_shapes) + sems, compiler_params=_params(2),
    )(*args, *s_in, *s_pre)


def attn_fwd(q2, kv, kr, *, name, side=None):
    tp = q2.shape[0]
    nh = MLA_HEADS
    nb = tp // TM
    rep = TM // LANES

    def body(in_refs, out_refs, scratch):
        q_ref, kv_ref, kr_ref = in_refs
        o_ref, lse_ref = out_refs
        m_ref, l_ref, acc_ref, s0_ref, s1_ref, p_ref, alpha_ref = scratch
        qi = pl.program_id(1)
        s_refs = (s0_ref, s1_ref)
        m_ref[...] = jnp.full_like(m_ref, NEG_INF)
        l_ref[...] = jnp.zeros_like(l_ref)
        acc_ref[...] = jnp.zeros_like(acc_ref)

        def prefetch(k, slot):
            k2 = jnp.concatenate([kv_ref[_key_rows(k), :], kr_ref[_key_rows(k), :]], axis=1)
            s_refs[slot][...] = lax.dot_general(q_ref[...], k2, NT, preferred_element_type=F32)

        def process(k, slot, diagonal):
            for r in range(TM // LANES):
                rows = slice(r * LANES, (r + 1) * LANES)
                s = s_refs[slot][rows, :]
                if diagonal:
                    qpos = r * LANES + lax.broadcasted_iota(jnp.int32, (LANES, TM), 0)
                    s = jnp.where(lax.broadcasted_iota(jnp.int32, (LANES, TM), 1) <= qpos, s, NEG_INF)
                m_prev = m_ref[rows, :]
                m_new = jnp.maximum(m_prev, jnp.max(s, axis=1, keepdims=True))
                alpha = jnp.exp2(m_prev - m_new)
                p = jnp.exp2(s - jnp.tile(m_new, (1, rep)))
                lane_sums = p[:, 0:LANES]
                for t in range(1, rep):
                    lane_sums = lane_sums + p[:, t * LANES:(t + 1) * LANES]
                l_ref[rows, :] = alpha * l_ref[rows, :] + lane_sums
                p_ref[rows, :] = p.astype(BF)
                alpha_ref[rows, :] = alpha
                m_ref[rows, :] = m_new
            acc_ref[...] = alpha_ref[...] * acc_ref[...] + jnp.dot(p_ref[...], kv_ref[_key_rows(k), :], preferred_element_type=F32)

        _pipelined_key_blocks(qi, prefetch, process)
        l = jnp.sum(l_ref[...], axis=1, keepdims=True)
        o_ref[...] = (acc_ref[...] / l).astype(BF)
        lse_ref[...] = m_ref[...] + jnp.log2(l)

    return _side_call(
        body, side, name=name, grid=(nh, nb),
        in_specs=[pl.BlockSpec((TM, Q_PAD), lambda h, qi: (qi, h)), pl.BlockSpec((tp, KV_PAD), lambda h, qi: (0, h)),
                  pl.BlockSpec((tp, LANES), lambda h, qi: (0, 0))],
        out_specs=[pl.BlockSpec((TM, KV_PAD), lambda h, qi: (qi, h)), pl.BlockSpec((TM, LANES), lambda h, qi: (qi, h))],
        out_shape=[_sds((tp, nh * KV_PAD), BF), _sds((tp, nh * LANES), F32)],
        scratch_shapes=[pltpu.VMEM((TM, LANES), F32)] * 3 + [pltpu.VMEM((TM, TM), F32)] * 2
        + [pltpu.VMEM((TM, TM), BF), pltpu.VMEM((TM, LANES), F32)], args=(q2, kv, kr),
    )


def conv_fwd(p, w, *, name):
    tp = p.shape[0]

    def body(b_ref, c_ref, h_ref, w_ref, y_ref, cv_ref, ebuf):
        i = pl.program_id(0)

        @pl.when(i == 0)
        def _():
            ebuf[0:8, :] = jnp.zeros((8, D_CONV), F32)

        e = c_ref[...].astype(F32) * h_ref[...].astype(F32)
        ebuf[8:8 + TM, :] = e
        w_all = w_ref[...]
        conv = w_all[0:1] * ebuf[pl.ds(6, TM), :] + w_all[1:2] * ebuf[pl.ds(7, TM), :] + w_all[2:3] * e
        cv_ref[...] = conv.astype(BF)
        y_ref[...] = (b_ref[...].astype(F32) * conv).astype(BF)
        ebuf[0:8, :] = ebuf[TM:TM + 8, :]

    def col(j):
        return pl.BlockSpec((TM, D_CONV), functools.partial(lambda i, j: (i, j), j=j))

    return _pcall(
        body, name=name, grid=(tp // TM,),
        in_specs=[col(0), col(1), col(2), pl.BlockSpec((3, D_CONV), lambda i: (0, 0))],
        out_specs=[col(0), col(0)], out_shape=[_sds((tp, D_CONV), BF)] * 2,
        scratch_shapes=[pltpu.VMEM((TM + 8, D_CONV), F32)], compiler_params=_params(1),
    )(p, p, p, w)


def merge_out_ln(ycv, o2, p, bg, wbc, wbm_p, wo, xprev, gp, bp, g, b, *, name):
    tp = ycv.shape[0]

    def body(y_ref, o_ref, gc_ref, gm_ref, bg_ref, wbc_ref, wbm_ref, wo_ref, xp_ref, gp_ref, bp_ref, g_ref, b_ref,
             bc_ref, bm_ref, mg_ref, xh_ref, rs_ref, hb_ref):
        bc = jnp.dot(y_ref[...], wbc_ref[...], preferred_element_type=F32)
        bm = jnp.dot(o_ref[...], wbm_ref[...], preferred_element_type=F32)
        bgv = bg_ref[...]
        mg = (jax.nn.sigmoid(gc_ref[...].astype(F32) + bgv[0:1]) * bc
              + jax.nn.sigmoid(gm_ref[...].astype(F32) + bgv[1:2]) * bm)
        mgb = mg.astype(BF)
        bc_ref[...] = bc.astype(BF)
        bm_ref[...] = bm.astype(BF)
        mg_ref[...] = mgb
        mix = jnp.dot(mgb, wo_ref[...], preferred_element_type=F32)
        hprev = xp_ref[...] * gp_ref[...] + bp_ref[...]
        _ln_store(ALPHA * hprev + mix, g_ref, b_ref, xh_ref, rs_ref, hb_ref)

    def rows(n, col=0):
        return pl.BlockSpec((TMH, n), functools.partial(lambda i, col: (i, col), col=col))

    def whole(r, c):
        return pl.BlockSpec((r, c), lambda i: (0, 0))

    ln_specs, ln_shapes = _ln_out(tp, TMH)
    return _pcall(
        body, name=name, grid=(tp // TMH,),
        in_specs=[rows(D_CONV), rows(MLA_HEADS * KV_PAD), rows(D_MODEL, 2), rows(D_MODEL, 3), whole(2, D_MODEL),
                  whole(D_CONV, D_MODEL), whole(MLA_HEADS * KV_PAD, D_MODEL), whole(D_MODEL, D_MODEL), rows(D_MODEL)]
        + [_row_vec(D_MODEL)] * 4,
        out_specs=[rows(D_MODEL)] * 3 + ln_specs, out_shape=[_sds((tp, D_MODEL), BF)] * 3 + ln_shapes,
        compiler_params=_params(1),
    )(ycv, o2, p, p, bg, wbc, wbm_p, wo, xprev, gp, bp, g, b)


def ln_bwd(dh, xh, rstd, g, *, branch_scale, name):
    tp = xh.shape[0]
    from_loss = isinstance(dh, tuple)

    def body(*refs):
        if from_loss:
            xh_ref, rs_ref, g_ref, b_ref, t_ref, dzb_ref, dg_ref, db_ref, loss_ref = refs
        else:
            dh_ref, xh_ref, rs_ref, g_ref, dzb_ref, dg_ref, db_ref = refs
        i = pl.program_id(0)

        @pl.when(i == 0)
        def _():
            dg_ref[...] = jnp.zeros_like(dg_ref)
            db_ref[...] = jnp.zeros_like(db_ref)
            if from_loss:
                loss_ref[...] = jnp.zeros_like(loss_ref)

        xhat = xh_ref[...]
        if from_loss:
            row = i * TM + lax.broadcasted_iota(jnp.int32, (TM, 1), 0)
            real = (row >= N_META) & (row < N_META + dh[2])
            diff = jnp.where(real, xhat * g_ref[...] + b_ref[...] - t_ref[...], 0.0)
            loss_ref[...] += 0.5 / D_MODEL * jnp.sum(diff * diff)
            dy = diff * (1.0 / D_MODEL)
        else:
            dy = dh_ref[...]
        dg_ref[...] += jnp.sum(dy * xhat, axis=0, keepdims=True)
        db_ref[...] += jnp.sum(dy, axis=0, keepdims=True)
        dxh = dy * g_ref[...]
        m1 = jnp.mean(dxh, axis=-1, keepdims=True)
        m2 = jnp.mean(dxh * xhat, axis=-1, keepdims=True)
        dz = rs_ref[...] * (dxh - m1 - xhat * m2)
        dzb_ref[...] = (branch_scale * dz).astype(BF)

    rows = pl.BlockSpec((TM, D_MODEL), lambda i: (i, 0))
    stat = pl.BlockSpec((TM, 1), lambda i: (i, 0))
    vec = _row_vec(D_MODEL)
    out_specs = [rows, vec, vec]
    out_shape = [_sds((tp, D_MODEL), BF), _sds((1, D_MODEL), F32), _sds((1, D_MODEL), F32)]
    if from_loss:
        in_specs, args = [rows, stat, vec, vec, rows], (xh, rstd, g, dh[0], dh[1])
        out_specs.append(pl.BlockSpec((8, LANES), lambda i: (0, 0)))
        out_shape.append(_sds((8, LANES), F32))
    else:
        in_specs, args = [rows, rows, stat, vec], (dh, xh, rstd, g)
    return _pcall(body, name=name, grid=(tp // TM,), in_specs=in_specs, out_specs=out_specs, out_shape=out_shape,
                  compiler_params=_params(1))(*args)


def ffn_down_bwd(dzb, wd, gate, up, *, name, side=None):
    tp = dzb.shape[0]
    tn = D_FF // 2

    def body(in_refs, out_refs, scratch):
        dz_ref, wd_ref, g_ref, u_ref = in_refs
        dg_ref, du_ref = out_refs
        wd = jnp.concatenate([wd_ref[0], wd_ref[1]], axis=0)
        da = lax.dot_general(dz_ref[...], wd, NT, preferred_element_type=F32)
        g = g_ref[...].astype(F32)
        u = u_ref[...].astype(F32)
        sg = jax.nn.sigmoid(g)
        dg_ref[...] = (da * u * sg * (1.0 + g * (1.0 - sg))).astype(BF)
        du_ref[...] = (da * g * sg).astype(BF)

    blk = pl.BlockSpec((TM, tn), lambda i, j: (i, j))
    return _side_call(
        body, side, name=name, grid=(tp // TM, D_FF // tn),
        in_specs=[pl.BlockSpec((TM, D_MODEL), lambda i, j: (i, 0)), pl.BlockSpec((2, tn // 2, D_MODEL), lambda i, j: (j, 0, 0)), blk, blk],
        out_specs=[blk, blk], out_shape=[_sds((tp, D_FF), BF)] * 2, scratch_shapes=[], args=(dzb, wd, gate, up),
    )


def merge_bwd(dzb, wo, bc, bm, p, bg, wbc, wbm_p, o2, *, name):
    tp = dzb.shape[0]
    nh = MLA_HEADS

    def body(dz_ref, wo_ref, bc_ref, bm_ref, gc_ref, gm_ref, bg_ref, wbc_ref, wbm_ref, o_ref,
             dbc_ref, dbm_ref, dgg_ref, dy_ref, do_ref, dl_ref, dbg_ref):
        i = pl.program_id(0)

        @pl.when(i == 0)
        def _():
            dbg_ref[...] = jnp.zeros_like(dbg_ref)

        dmg = lax.dot_general(dz_ref[...], wo_ref[...], NT, preferred_element_type=F32)
        bgv = bg_ref[...]
        sc = jax.nn.sigmoid(gc_ref[...].astype(F32) + bgv[0:1])
        sm = jax.nn.sigmoid(gm_ref[...].astype(F32) + bgv[1:2])
        dbc = (dmg * sc).astype(BF)
        dbm = (dmg * sm).astype(BF)
        dgc = dmg * bc_ref[...].astype(F32) * sc * (1.0 - sc)
        dgm = dmg * bm_ref[...].astype(F32) * sm * (1.0 - sm)
        dbc_ref[...] = dbc
        dbm_ref[...] = dbm
        dgg_ref[...] = jnp.concatenate([dgc, dgm], axis=1).astype(BF)
        dbg_ref[...] += jnp.concatenate([jnp.sum(dgc, axis=0, keepdims=True), jnp.sum(dgm, axis=0, keepdims=True)], axis=0)
        dy_ref[...] = lax.dot_general(dbc, wbc_ref[...], NT, preferred_element_type=F32)
        do = lax.dot_general(dbm, wbm_ref[...], NT, preferred_element_type=F32)
        do_ref[...] = do.astype(BF)
        prod = do * o_ref[...].astype(F32)
        parts = []
        for h in range(nh):
            d = jnp.sum(prod[:, h * KV_PAD:(h + 1) * KV_PAD], axis=1, keepdims=True)
            parts.append(jnp.broadcast_to(d, (TMH, LANES)))
        dl_ref[...] = jnp.concatenate(parts, axis=1)

    def rows(n, col=0):
        return pl.BlockSpec((TMH, n), functools.partial(lambda i, col: (i, col), col=col))

    def whole(r, c):
        return pl.BlockSpec((r, c), lambda i: (0, 0))

    return _pcall(
        body, name=name, grid=(tp // TMH,),
        in_specs=[rows(D_MODEL), whole(D_MODEL, D_MODEL), rows(D_MODEL), rows(D_MODEL), rows(D_MODEL, 2), rows(D_MODEL, 3),
                  whole(2, D_MODEL), whole(D_CONV, D_MODEL), whole(nh * KV_PAD, D_MODEL), rows(nh * KV_PAD)],
        out_specs=[rows(D_MODEL), rows(D_MODEL), rows(2 * D_MODEL, 1), rows(D_CONV), rows(nh * KV_PAD), rows(nh * LANES),
                   whole(2, D_MODEL)],
        out_shape=[_sds((tp, D_MODEL), BF), _sds((tp, D_MODEL), BF), _sds((tp, D_IN_PAD), BF), _sds((tp, D_CONV), F32),
                   _sds((tp, nh * KV_PAD), BF), _sds((tp, nh * LANES), F32), _sds((2, D_MODEL), F32)],
        compiler_params=_params(1),
    )(dzb, wo, bc, bm, p, p, bg, wbc, wbm_p, o2)


def attn_bwd(q2, kv, kr, do2, lse, dl, *, name, side=None):
    tp = q2.shape[0]
    nh = MLA_HEADS
    nb = tp // TM
    rep = TM // LANES

    def body(in_refs, out_refs, scratch):
        q_ref, kv_ref, kr_ref, do_ref, lse_ref, dl_ref = in_refs
        dq_ref, dkv_ref, dkr_ref = out_refs
        dq_acc, s0_ref, s1_ref, dp0_ref, dp1_ref = scratch
        qi = pl.program_id(1)
        s_refs, dp_refs = (s0_ref, s1_ref), (dp0_ref, dp1_ref)

        @pl.when(qi == 0)
        def _():
            dkv_ref[...] = jnp.zeros_like(dkv_ref)

        @pl.when((qi == 0) & (pl.program_id(0) == 0))
        def _():
            dkr_ref[...] = jnp.zeros_like(dkr_ref)

        dq_acc[...] = jnp.zeros_like(dq_acc)

        def prefetch(k, slot):
            kvb = kv_ref[_key_rows(k), :]
            k2 = jnp.concatenate([kvb, kr_ref[_key_rows(k), :]], axis=1)
            s_refs[slot][...] = lax.dot_general(q_ref[...], k2, NT, preferred_element_type=F32)
            dp_refs[slot][...] = lax.dot_general(do_ref[...], kvb, NT, preferred_element_type=F32)

        def process(k, slot, diagonal):
            rows = _key_rows(k)
            s = s_refs[slot][...]
            if diagonal:
                s = _causal_mask(s)
            p = jnp.exp2(s - jnp.tile(lse_ref[...], (1, rep)))
            dsb = (p * (dp_refs[slot][...] - jnp.tile(dl_ref[...], (1, rep)))).astype(BF)
            dk2 = lax.dot_general(dsb, q_ref[...], TN, preferred_element_type=F32) * LN2
            dkv_ref[rows, :] += lax.dot_general(p.astype(BF), do_ref[...], TN, preferred_element_type=F32) + dk2[:, :KV_PAD]
            dkr_ref[rows, :] += dk2[:, KV_PAD:KV_PAD + LANES]
            k2 = jnp.concatenate([kv_ref[rows, :], kr_ref[rows, :]], axis=1)
            dq_acc[...] += jnp.dot(dsb, k2, preferred_element_type=F32)

        _pipelined_key_blocks(qi, prefetch, process)
        dq_ref[...] = dq_acc[...].astype(BF)

    def qrow(n):
        return pl.BlockSpec((TM, n), lambda h, qi: (qi, h))

    def head(n):
        return pl.BlockSpec((tp, n), lambda h, qi: (0, h))

    return _side_call(
        body, side, name=name, grid=(nh, nb),
        in_specs=[qrow(Q_PAD), head(KV_PAD), pl.BlockSpec((tp, LANES), lambda h, qi: (0, 0)), qrow(KV_PAD), qrow(LANES), qrow(LANES)],
        out_specs=[qrow(Q_PAD), head(KV_PAD), pl.BlockSpec((tp, LANES), lambda h, qi: (0, 0))],
        out_shape=[_sds((tp, nh * Q_PAD), BF), _sds((tp, nh * KV_PAD), F32), _sds((tp, LANES), F32)],
        scratch_shapes=[pltpu.VMEM((TM, Q_PAD), F32)] + [pltpu.VMEM((TM, TM), F32)] * 4, args=(q2, kv, kr, do2, lse, dl),
    )


def _rms_bwd(x, g, dy):
    r = lax.rsqrt(jnp.mean(x * x, axis=-1, keepdims=True) + RMS_EPS)
    gy = dy * g
    dx = r * gy - x * (r * r * r) * jnp.mean(x * gy, axis=-1, keepdims=True)
    return dx, jnp.sum(dy * x * r, axis=0, keepdims=True)


def mla_prep_bwd(dq2, dkv, dkr, p, gq, gkv, wuq_p, wukv, tabs_bwd, dp, *, name):
    tp = dq2.shape[0]
    nh = MLA_HEADS

    def body(dq_ref, dkv_ref, dkr_ref, cq_ref, ckv_ref, gq_ref, gkv_ref, wuq_ref, wukv_ref,
             cq_t, s1q_t, s2q_t, ck_t, s1k_t, s2k_t, dp_in_ref, dqb_ref, dsm_ref, dgq_ref, dgkv_ref):
        i = pl.program_id(0)

        @pl.when(i == 0)
        def _():
            dgq_ref[...] = jnp.zeros_like(dgq_ref)
            dgkv_ref[...] = jnp.zeros_like(dgkv_ref)

        dqb = _rope(dq_ref[...].astype(F32), cq_t[...], s1q_t[...], s2q_t[...], nh).astype(BF)
        dqb_ref[...] = dqb
        dcqn = lax.dot_general(dqb, wuq_ref[...], NT, preferred_element_type=F32)
        dcq, dgq = _rms_bwd(cq_ref[...].astype(F32), gq_ref[...], dcqn)
        dckvn = lax.dot_general(dkv_ref[...].astype(BF), wukv_ref[...], NT, preferred_element_type=F32)
        dckv, dgkv = _rms_bwd(ckv_ref[...].astype(F32), gkv_ref[...], dckvn)
        dkr = _rope(dkr_ref[...], ck_t[...], s1k_t[...], s2k_t[...], 1)
        dsm_ref[...] = jnp.concatenate([dcq, dckv, dkr], axis=1).astype(BF)
        dgq_ref[...] += dgq
        dgkv_ref[...] += dgkv

    def rows(n, col=0):
        return pl.BlockSpec((TMH, n), functools.partial(lambda i, col: (i, col), col=col))

    return _pcall(
        body, name=name, grid=(tp // TMH,),
        in_specs=[rows(nh * Q_PAD), rows(nh * KV_PAD), rows(LANES), rows(Q_LORA, 1536 // Q_LORA), rows(KV_LORA, 1792 // KV_LORA),
                  _row_vec(Q_LORA), _row_vec(KV_LORA),
                  pl.BlockSpec((Q_LORA, nh * Q_PAD), lambda i: (0, 0)), pl.BlockSpec((KV_LORA, nh * KV_PAD), lambda i: (0, 0)),
                  rows(Q_PAD), rows(Q_PAD), rows(Q_PAD), rows(LANES), rows(LANES), rows(LANES), pl.BlockSpec(memory_space=pl.ANY)],
        out_specs=[rows(nh * Q_PAD), rows(Q_LORA + KV_LORA + LANES, 1536 // (Q_LORA + KV_LORA + LANES)), _row_vec(Q_LORA),
                   _row_vec(KV_LORA)],
        out_shape=[_sds((tp, nh * Q_PAD), BF), _sds(dp.shape, dp.dtype), _sds((1, Q_LORA), F32), _sds((1, KV_LORA), F32)],
        input_output_aliases={15: 1}, compiler_params=_params(1),
    )(dq2, dkv, dkr, p, p, gq, gkv, wuq_p, wukv, *tabs_bwd, dp)


def conv_bwd(dy, p, conv, w, dp, *, name):
    tp = dy.shape[0]
    nb = tp // TM

    def body(dy_ref, b_ref, c_ref, h_ref, cv_ref, w_ref, dp_in_ref, dp_ref, dw0_ref, dw1_ref, dw2_ref, dbuf):
        i = pl.program_id(0)

        @pl.when(i == 0)
        def _():
            dbuf[TM:TM + 8, :] = jnp.zeros((8, D_CONV), F32)
            dw0_ref[...] = jnp.zeros_like(dw0_ref)
            dw1_ref[...] = jnp.zeros_like(dw1_ref)
            dw2_ref[...] = jnp.zeros_like(dw2_ref)

        dyv = dy_ref[...]
        c = c_ref[...].astype(F32)
        hh = h_ref[...].astype(F32)
        dconv = dyv * b_ref[...].astype(F32)
        dbuf[0:TM, :] = dconv
        d1 = dbuf[pl.ds(1, TM), :]
        d2 = dbuf[pl.ds(2, TM), :]
        w_all = w_ref[...]
        de = w_all[2:3] * dconv + w_all[1:2] * d1 + w_all[0:1] * d2
        e = c * hh
        dp_ref[...] = jnp.concatenate([dyv * cv_ref[...].astype(F32), de * hh, de * c], axis=1).astype(BF)
        dw0_ref[...] += jnp.sum(d2 * e, axis=0, keepdims=True)
        dw1_ref[...] += jnp.sum(d1 * e, axis=0, keepdims=True)
        dw2_ref[...] += jnp.sum(dconv * e, axis=0, keepdims=True)
        dbuf[TM:TM + 8, :] = dbuf[0:8, :]

    def col(j):
        return pl.BlockSpec((TM, D_CONV), functools.partial(lambda i, j: (nb - 1 - i, j), j=j))

    return _pcall(
        body, name=name, grid=(nb,),
        in_specs=[col(0), col(0), col(1), col(2), col(0), pl.BlockSpec((3, D_CONV), lambda i: (0, 0)),
                  pl.BlockSpec(memory_space=pl.ANY)],
        out_specs=[pl.BlockSpec((TM, 3 * D_CONV), lambda i: (nb - 1 - i, 0))] + [_row_vec(D_CONV)] * 3,
        out_shape=[_sds(dp.shape, dp.dtype)] + [_sds((1, D_CONV), F32)] * 3, input_output_aliases={6: 0},
        scratch_shapes=[pltpu.VMEM((TM + 8, D_CONV), F32)], compiler_params=_params(1),
    )(dy, p, p, p, conv, w, dp)


def adamw(w, g, m, v, *, name):
    r, c = w.shape
    tr = r
    for cand in (256, 128, 64, 32, 16, 8):
        if r % cand == 0 and r > cand:
            tr = cand
            break

    def body(w_ref, g_ref, m_ref, v_ref, d_ref, nm_ref, nv_ref):
        gv = g_ref[...]
        nm = ADAM_B1 * m_ref[...] + (1.0 - ADAM_B1) * gv
        nv = ADAM_B2 * v_ref[...] + (1.0 - ADAM_B2) * (gv * gv)
        m_hat = nm / (1.0 - ADAM_B1 ** ADAM_STEP)
        v_hat = nv / (1.0 - ADAM_B2 ** ADAM_STEP)
        d_ref[...] = -ADAM_LR * (m_hat / (jnp.sqrt(v_hat) + ADAM_EPS) + ADAM_WD * w_ref[...])
        nm_ref[...] = nm
        nv_ref[...] = nv

    blk = pl.BlockSpec((tr, c), lambda i: (i, 0))
    return _pcall(
        body, name=name, grid=(r // tr,), in_specs=[blk] * 4, out_specs=[blk] * 3,
        out_shape=[_sds((r, c), F32)] * 3, compiler_params=_params(1),
    )(w, g, m, v)


HBM_SPEC = pl.BlockSpec(memory_space=pltpu.HBM)


def _place():
    return lax.axis_index("x"), lax.axis_index("y"), lax.axis_index("c")


def _other_chips(x, y):
    return [(1 - x, y), (x, 1 - y), (1 - x, 1 - y)]


def _half(ref_or_shape_rows, c):
    return pl.ds(c * (ref_or_shape_rows // 2), ref_or_shape_rows // 2)


def gather_side(items):
    n = len(items)
    shards = [s for s, _ in items]
    layers = [l for _, l in items]

    def program(x_refs, o_refs, send_sems, recv_sems):
        x, y, c = _place()
        me = 2 * x + y
        chips = _other_chips(x, y)

        def copy(sem, src, dst, to):
            return pltpu.make_async_remote_copy(src_ref=src, dst_ref=dst, send_sem=send_sems.at[sem], recv_sem=recv_sems.at[sem],
                                                device_id=to, device_id_type=MESH)

        def src(i):
            return x_refs[i].at[layers[i], _half(x_refs[i].shape[1], c)]

        def dst(i, slot, cc):
            return o_refs[i].at[slot, _half(o_refs[i].shape[1], cc)]

        sends = [copy(6 * i + k, src(i), dst(i, me, c), (px, py, c)) for i in range(n) for k, (px, py) in enumerate(chips)]
        passed = [copy(6 * i + 3 + k, dst(i, 2 * px + py, c), dst(i, 2 * px + py, c), (x, y, 1 - c))
                  for k, (px, py) in enumerate(chips) for i in range(n)]

        def start():
            for cp in sends:
                cp.start()

        def finish():
            pos = 0
            for k, (px, py) in enumerate(chips):
                for i in range(n):
                    copy(6 * i + k, src(i), dst(i, 2 * px + py, c), (px, py, c)).wait_recv()
                    passed[pos].start()
                    pos += 1
            for k, (px, py) in enumerate(chips):
                for i in range(n):
                    copy(6 * i + 3 + k, dst(i, 2 * px + py, 1 - c), dst(i, 2 * px + py, 1 - c), (x, y, 1 - c)).wait_recv()
            for cp in sends + passed:
                cp.wait_send()

        return start, finish

    prefilled = [jnp.broadcast_to(s[l][None], (4,) + s.shape[1:]) for s, l in items]
    return shards, prefilled, 6 * n, program


def scatter_side(pss):
    n = len(pss)

    def program(p_refs, o_refs, send_sems, recv_sems):
        x, y, c = _place()
        me = 2 * x + y
        chips = _other_chips(x, y)

        def copy(i, k, j_src, j_dst, to):
            return pltpu.make_async_remote_copy(src_ref=p_refs[i].at[j_src], dst_ref=o_refs[i].at[j_dst],
                                                send_sem=send_sems.at[3 * i + k], recv_sem=recv_sems.at[3 * i + k],
                                                device_id=to, device_id_type=MESH)

        sends = [copy(i, k, 2 * px + py, me, (px, py, c)) for i in range(n) for k, (px, py) in enumerate(chips)]

        def start():
            for cp in sends:
                cp.start()

        def finish():
            for i in range(n):
                for k, (px, py) in enumerate(chips):
                    copy(i, k, me, 2 * px + py, (px, py, c)).wait_recv()
            for cp in sends:
                cp.wait_send()

        return start, finish

    xi, yi, _ = _place()
    own = jnp.arange(4)[:, None, None] == 2 * xi + yi
    prefilled = [jnp.where(own, p, jnp.zeros_like(p)) for p in pss]
    return list(pss), prefilled, 3 * n, program


def exchange_alone(side, *, name):
    inputs, prefilled, n_sems, program = side
    a, b = len(inputs), len(prefilled)

    def body(*refs):
        start, finish = program(refs[:a], refs[a + b:a + 2 * b], refs[-2], refs[-1])
        start()
        finish()

    return _pcall(
        body, name=name, in_specs=[HBM_SPEC] * (a + b), out_specs=[HBM_SPEC] * b, out_shape=[_sds(p.shape, p.dtype) for p in prefilled],
        input_output_aliases={a + i: i for i in range(b)}, scratch_shapes=[pltpu.SemaphoreType.DMA((n_sems,))] * 2,
    )(*inputs, *prefilled)


def pair_exchange(gs, *, name):
    n = len(gs)

    def body(*refs):
        g_refs, o_refs = refs[:n], refs[n:2 * n]
        send_sems, recv_sems = refs[2 * n:]
        x, y, c = _place()
        cps = [pltpu.make_async_remote_copy(src_ref=g_refs[i].at[:, _half(g_refs[i].shape[1], 1 - c)], dst_ref=o_refs[i],
                                            send_sem=send_sems.at[i], recv_sem=recv_sems.at[i], device_id=(x, y, 1 - c),
                                            device_id_type=MESH)
               for i in range(n)]
        for cp in cps:
            cp.start()
        for cp in cps:
            cp.wait()

    return _pcall(
        body, name=name, in_specs=[HBM_SPEC] * n, out_specs=[HBM_SPEC] * n,
        out_shape=[_sds((4, g.shape[1] // 2, g.shape[2]), g.dtype) for g in gs],
        scratch_shapes=[pltpu.SemaphoreType.DMA((n,)), pltpu.SemaphoreType.DMA((n,))],
    )(*gs)


def _comm_rows(a, b, itemsize):
    return a // 2 if a * b * itemsize > (3 << 19) and a % 16 == 0 else a


def pair_add(g, s1, c_idx, *, name):
    n, a, b = g.shape
    ah = a // 2
    ta = _comm_rows(ah, b, 2)
    nblk = ah // ta

    def body(c_ref, g_ref, s_ref, o_ref):
        o_ref[...] = (g_ref[...].astype(F32) + s_ref[...].astype(F32)).astype(o_ref.dtype)

    grid_spec = pltpu.PrefetchScalarGridSpec(
        num_scalar_prefetch=1, grid=(n, nblk),
        in_specs=[pl.BlockSpec((1, ta, b), lambda j, i, c_ref: (j, c_ref[0] * nblk + i, 0)),
                  pl.BlockSpec((1, ta, b), lambda j, i, c_ref: (j, i, 0))],
        out_specs=pl.BlockSpec((1, ta, b), lambda j, i, c_ref: (j, i, 0)),
    )
    return _pcall(body, name=name, grid_spec=grid_spec, out_shape=_sds((n, ah, b), g.dtype), compiler_params=_params(2))(
        c_idx, g, s1)


def sum_chunks(s2, *, name):
    n, a, b = s2.shape
    ta = _comm_rows(a, b, 4)

    def body(s_ref, o_ref):
        acc = s_ref[0].astype(F32)
        for j in range(1, n):
            acc = acc + s_ref[j].astype(F32)
        o_ref[...] = acc

    return _pcall(
        body, name=name, grid=(a // ta,), in_specs=[pl.BlockSpec((n, ta, b), lambda i: (0, i, 0))],
        out_specs=pl.BlockSpec((ta, b), lambda i: (i, 0)), out_shape=_sds((a, b), F32), compiler_params=_params(1),
    )(s2)


def pair_gather(rcs, *, name):
    n = len(rcs)

    def body(*refs):
        r_refs, o_refs = refs[:n], refs[2 * n:3 * n]
        send_sems, recv_sems = refs[3 * n:]
        x, y, c = _place()

        def copy(i, half):
            return pltpu.make_async_remote_copy(src_ref=r_refs[i], dst_ref=o_refs[i].at[half], send_sem=send_sems.at[i],
                                                recv_sem=recv_sems.at[i], device_id=(x, y, 1 - c), device_id_type=MESH)

        sends = [copy(i, c) for i in range(n)]
        for cp in sends:
            cp.start()
        for i in range(n):
            copy(i, 1 - c).wait_recv()
        for cp in sends:
            cp.wait_send()

    prefilled = [jnp.broadcast_to(r[None], (2,) + r.shape) for r in rcs]
    return _pcall(
        body, name=name, in_specs=[HBM_SPEC] * (2 * n), out_specs=[HBM_SPEC] * n,
        out_shape=[_sds(p.shape, p.dtype) for p in prefilled], input_output_aliases={n + i: i for i in range(n)},
        scratch_shapes=[pltpu.SemaphoreType.DMA((n,)), pltpu.SemaphoreType.DMA((n,))],
    )(*rcs, *prefilled)


def exchange_small(arrs, *, reduce, name):
    n = len(arrs)

    def body(*refs):
        v_refs, o_refs = refs[:n], refs[n:2 * n]
        bufs = refs[2 * n:3 * n] if reduce else o_refs
        send_sems, recv_sems = refs[-2:]
        x, y, c = _place()
        me = 4 * x + 2 * y + c
        for i in range(n):
            bufs[i][me] = v_refs[i][...]

        def peer(k):
            dx, dy, dc = (k >> 2) & 1, (k >> 1) & 1, k & 1
            return (1 - x if dx else x, 1 - y if dy else y, 1 - c if dc else c)

        def copy(i, k, slot):
            return pltpu.make_async_remote_copy(src_ref=v_refs[i], dst_ref=bufs[i].at[slot], send_sem=send_sems.at[7 * i + k - 1],
                                                recv_sem=recv_sems.at[7 * i + k - 1], device_id=peer(k), device_id_type=MESH)

        sends = [copy(i, k, me) for i in range(n) for k in range(1, 8)]
        for cp in sends:
            cp.start()
        for i in range(n):
            for k in range(1, 8):
                px, py, pc = peer(k)
                copy(i, k, 4 * px + 2 * py + pc).wait_recv()
        for cp in sends:
            cp.wait_send()
        if reduce:
            for i in range(n):
                acc = bufs[i][0]
                for d in range(1, 8):
                    acc = acc + bufs[i][d]
                o_refs[i][...] = acc

    vmem = pl.BlockSpec(memory_space=pltpu.VMEM)
    stacked = [(8,) + a.shape for a in arrs]
    return _pcall(
        body, name=name, in_specs=[vmem] * n, out_specs=[vmem] * n,
        out_shape=[_sds(a.shape if reduce else s, F32) for a, s in zip(arrs, stacked)],
        scratch_shapes=([pltpu.VMEM(s, F32) for s in stacked] if reduce else [])
        + [pltpu.SemaphoreType.DMA((7 * n,)), pltpu.SemaphoreType.DMA((7 * n,))],
    )(*arrs)


def _pad_rows(n, mult):
    return -(-n // mult) * mult


def _chip_major(g, b):
    return g.reshape(g.shape[0], 4, b).transpose(1, 0, 2)


def _rope_tables(tp):
    inv_freq = 1.0 / (ROPE_BASE ** (jnp.arange(0, QK_ROPE, 2, dtype=F32) / QK_ROPE))
    ang = jnp.arange(tp, dtype=F32)[:, None] * inv_freq[None, :]
    cos, sin = jnp.cos(ang), jnp.sin(ang)
    one = lambda n: jnp.ones((tp, n), F32)
    zero = lambda n: jnp.zeros((tp, n), F32)
    cq = jnp.concatenate([one(128), cos, cos, one(96)], axis=1)
    s1q = jnp.concatenate([zero(144), sin, zero(96)], axis=1)
    s2q = jnp.concatenate([zero(128), -sin, zero(112)], axis=1)
    ck = jnp.concatenate([cos, cos, zero(96)], axis=1)
    s1k = jnp.concatenate([zero(16), sin, zero(96)], axis=1)
    s2k = jnp.concatenate([-sin, zero(112)], axis=1)
    fwd = (cq * (ATT_SCALE * LOG2E), s1q * (ATT_SCALE * LOG2E), s2q * (ATT_SCALE * LOG2E), ck, s1k, s2k)
    bwd = (cq * ATT_SCALE, -s1q * ATT_SCALE, -s2q * ATT_SCALE, ck, -s1k, -s2k)
    return fwd, bwd


def _pad_w_in(w):
    return jnp.concatenate([w[:, :1952], jnp.zeros((w.shape[0], 96), w.dtype), w[:, 1952:]], axis=1)


def _pad_w_uq(w):
    w = w.reshape(Q_LORA, MLA_HEADS, QK_NOPE + QK_ROPE)
    z = lambda n: jnp.zeros((Q_LORA, MLA_HEADS, n), w.dtype)
    return jnp.concatenate([w[..., :QK_NOPE], z(64), w[..., QK_NOPE:], z(96)], axis=-1).reshape(Q_LORA, MLA_HEADS * Q_PAD)


def _unpad_w_uq(w):
    w = w.reshape(Q_LORA, MLA_HEADS, Q_PAD)
    return jnp.concatenate([w[..., :QK_NOPE], w[..., 128:128 + QK_ROPE]], axis=-1).reshape(Q_LORA, MLA_HEADS * (QK_NOPE + QK_ROPE))


def _pad_w_br_mla(w):
    w = w.reshape(MLA_HEADS, V_HEAD, D_MODEL)
    return jnp.concatenate([jnp.zeros_like(w), w], axis=1).reshape(MLA_HEADS * KV_PAD, D_MODEL)


def _unpad_w_br_mla(w):
    return w.reshape(MLA_HEADS, KV_PAD, D_MODEL)[:, V_HEAD:].reshape(MLA_HEADS * V_HEAD, D_MODEL)


def _riding(hooks, where, l, *args):
    make = hooks.get(where)
    ride = make(l, *args) if make else None
    return ride if ride else (None, lambda results: None)


def _layer_fwd(l, st, xprev, gp, bp, hb, w, tabs, hooks):
    ln_g, ln_b = w["ln_g"], w["ln_b"]
    lg = lambda k: ln_g[l, k][None]
    lb = lambda k: ln_b[l, k][None]
    s = {}
    s["x0"], s["gp0"], s["bp0"], s["hb0"] = xprev, gp, bp, hb
    side, got = _riding(hooks, "ffn1_fwd", l)
    s["g1"], s["u1"], s["a1"], *extras = ffn_up(hb, w["ffn1_w_up"][l], name="ffn_up", side=side)
    got(extras)
    s["xh1"], s["rs1"], s["hb1"] = down_ln(s["a1"], w["ffn1_w_down"][l], xprev, gp, bp, lg(0), lb(0), name="ffn_down_ln")
    s["p"] = mm_rows([(s["hb1"], w["mix_w_in"][l], False, 0)], D_IN_PAD, name="mix_in", tn=1024, out_dtype=BF)
    gq, gkv = w["q_norm_g"][l][None], w["kv_norm_g"][l][None]
    s["cqn"], s["ckvn"], s["q2"], s["kv"], s["kr"] = mla_prep(s["p"], gq, gkv, w["w_uq"][l], w["w_ukv"][l], tabs, name="mla_prep")
    side, got = _riding(hooks, "attn_fwd", l)
    s["o2"], s["lse"], *extras = attn_fwd(s["q2"], s["kv"], s["kr"], name="attn_fwd", side=side)
    got(extras)
    s["ycv"], s["conv"] = conv_fwd(s["p"], w["conv_w"][l], name="conv_fwd")
    s["bc"], s["bm"], s["mg"], s["xh2"], s["rs2"], s["hb2"] = merge_out_ln(
        s["ycv"], s["o2"], s["p"], w["mix_b_gate"][l], w["w_br_conv"][l], w["w_br_mla"][l], w["w_o"][l],
        s["xh1"], lg(0), lb(0), lg(1), lb(1), name="merge_out_ln")
    s["g2"], s["u2"], s["a2"] = ffn_up(s["hb2"], w["ffn2_w_up"][l], name="ffn_up")
    s["xh3"], s["rs3"], s["hb3"] = down_ln(s["a2"], w["ffn2_w_down"][l], s["xh2"], lg(1), lb(1), lg(2), lb(2), name="ffn_down_ln")
    st.append(s)
    return s["xh3"], lg(2), lb(2), s["hb3"]


def _ffn_bwd(which, l, g, hooks, dh, w_up, w_down, ln_gain, hb_in, gate, up, act, xh, rs):
    dzb, dgam, dbet, *loss_acc = ln_bwd(dh, xh, rs, ln_gain, branch_scale=0.5, name="ln_bwd")
    if loss_acc:
        g["loss"] = loss_acc[0]
    g[which + "_w_down"] = tn_mm(act, dzb, tm=D_FF // 2, name="dw_down", shard=("rows", D_FF // 4))
    side, got = _riding(hooks, which + "_down_bwd", l, g)
    dgate, dup, *extras = ffn_down_bwd(dzb, w_down, gate, up, name="ffn_down_bwd", side=side)
    got(extras)
    d_w = tn_mm(hb_in, dgate, tm=512, name="dw_up", shard=("cols", D_FF // 2), slot0=0)
    g[which + "_w_up"] = tn_mm(hb_in, dup, tm=512, name="dw_up", shard=("cols", D_FF // 2), slot0=2, dst=d_w)
    side, got = _riding(hooks, which + "_up_bwd", l, g)
    dh_in = mm_rows([(dgate, w_up, True, 0), (dup, w_up, True, 1)], D_MODEL, name="ffn_up_bwd", tn=512, addend=dzb, add_scale=2.0 * ALPHA,
                    side=side)
    if side is not None:
        dh_in, *extras = dh_in
        got(extras)
    return dh_in, dgam, dbet


def _layer_bwd(l, s, dh, w, tabs_bwd, hooks):
    ln_g = w["ln_g"]
    lg = lambda k: ln_g[l, k][None]
    g = {}
    dh, dg2, db2 = _ffn_bwd("ffn2", l, g, hooks, dh, w["ffn2_w_up"][l], w["ffn2_w_down"][l], lg(2), s["hb2"], s["g2"], s["u2"],
                            s["a2"], s["xh3"], s["rs3"])
    dzb, dg1, db1 = ln_bwd(dh, s["xh2"], s["rs2"], lg(1), branch_scale=1.0, name="ln_bwd")
    g["w_o"] = tn_mm(s["mg"], dzb, tm=1024, name="dw_o", shard=("rows", D_MODEL // 4))
    dbc, dbm, dp, dycv, do2, dl, g["mix_b_gate"] = merge_bwd(
        dzb, w["w_o"][l], s["bc"], s["bm"], s["p"], w["mix_b_gate"][l], w["w_br_conv"][l], w["w_br_mla"][l], s["o2"], name="merge_bwd")
    g["w_br_conv"] = tn_mm(s["ycv"], dbc, tm=512, name="dw_br_conv", shard=("cols", D_MODEL // 4))
    g["w_br_mla"] = _chip_major(_unpad_w_br_mla(tn_mm(s["o2"], dbm, tm=1024, name="dw_br_mla")), D_MODEL // 4)
    side, got = _riding(hooks, "attn_bwd", l, g)
    dq2, dkv, dkr, *extras = attn_bwd(s["q2"], s["kv"], s["kr"], do2, s["lse"], dl, name="attn_bwd", side=side)
    got(extras)
    gq, gkv = w["q_norm_g"][l][None], w["kv_norm_g"][l][None]
    dqb, dp, g["q_norm_g"], g["kv_norm_g"] = mla_prep_bwd(dq2, dkv, dkr, s["p"], gq, gkv, w["w_uq"][l], w["w_ukv"][l], tabs_bwd, dp,
                                                          name="mla_prep_bwd")
    g["w_uq"] = _chip_major(_unpad_w_uq(tn_mm(s["cqn"], dqb, tm=Q_LORA, name="dw_uq")), MLA_HEADS * (QK_NOPE + QK_ROPE) // 4)
    g["w_ukv"] = tn_mm(s["ckvn"], dkv, tm=KV_LORA, name="dw_ukv", shard=("cols", MLA_HEADS * KV_PAD // 4))
    dp, dw0, dw1, dw2 = conv_bwd(dycv, s["p"], s["conv"], w["conv_w"][l], dp, name="conv_bwd")
    g["conv_w"] = jnp.concatenate([dw0, dw1, dw2], axis=0)
    d_in = tn_mm(s["hb1"], dp, tm=512, name="dw_in")
    g["mix_w_in"] = _chip_major(jnp.concatenate([d_in[:, :1952], d_in[:, 2048:]], axis=1), D_IN // 4)
    dh = mm_rows([(dp, w["mix_w_in"][l], True, 0)], D_MODEL, name="mix_in_bwd", tn=512, addend=dzb, add_scale=ALPHA)
    dh, dg0, db0 = _ffn_bwd("ffn1", l, g, hooks, dh, w["ffn1_w_up"][l], w["ffn1_w_down"][l], lg(0), s["hb0"], s["g1"], s["u1"],
                            s["a1"], s["xh1"], s["rs1"])
    g["ln_g"] = jnp.concatenate([dg0, dg1, dg2], axis=0)
    g["ln_b"] = jnp.concatenate([db0, db1, db2], axis=0)
    return dh, g


BIG = ("ffn1_w_up", "ffn1_w_down", "mix_w_in", "w_uq", "w_ukv", "w_br_conv", "w_br_mla", "w_o", "ffn2_w_up", "ffn2_w_down")
BIG_AXIS = (2, 1, 2, 2, 2, 2, 2, 1, 2, 1)
FFN1_MATRICES = ("ffn1_w_up", "ffn1_w_down")
MIXER_MATRICES = ("mix_w_in", "w_uq", "w_ukv", "w_br_conv", "w_br_mla", "w_o")
FFN2_MATRICES = ("ffn2_w_up", "ffn2_w_down")
SMALL_SHARDED = ("meta_tokens", "mix_b_gate", "conv_w", "ln_g", "ln_b")
SMALL_REPLICATED = ("q_norm_g", "kv_norm_g")
WEIGHTS = ("meta_tokens", "ffn1_w_up", "ffn1_w_down", "mix_w_in", "mix_b_gate", "conv_w", "q_norm_g", "w_uq", "kv_norm_g", "w_ukv",
           "w_br_conv", "w_br_mla", "w_o", "ffn2_w_up", "ffn2_w_down", "ln_g", "ln_b")


def _view2d(a):
    return a.reshape(-1, a.shape[-1])


def _local_grads(x_row, target_row, w, hooks=None):
    hooks = hooks or {}
    seq = x_row.shape[0]
    t_real = N_META + seq
    tp = _pad_rows(t_real, TM)
    pad = tp - t_real
    h0 = jnp.concatenate([w["meta_tokens"], x_row, jnp.zeros((pad, D_MODEL), F32)], axis=0)
    target_p = jnp.concatenate([jnp.zeros((N_META, D_MODEL), F32), target_row, jnp.zeros((pad, D_MODEL), F32)], axis=0)
    tabs, tabs_bwd = _rope_tables(tp)
    ones = jnp.ones((1, D_MODEL), F32)
    zeros = jnp.zeros((1, D_MODEL), F32)
    saved = []
    cur = (h0, ones, zeros, h0.astype(BF))
    for l in range(DEPTH):
        cur = _layer_fwd(l, saved, *cur, w, tabs, hooks)
    dh = (cur[2], target_p, seq)
    grads = [None] * DEPTH
    for l in reversed(range(DEPTH)):
        dh, grads[l] = _layer_bwd(l, saved[l], dh, w, tabs_bwd, hooks)
        if "layer_bwd_done" in hooks:
            hooks["layer_bwd_done"](l, grads[l])
    return grads[DEPTH - 1].pop("loss"), dh[N_META:t_real], dh[:N_META], grads


def kernel(x, meta_tokens, ffn1_w_up, ffn1_w_down, mix_w_in, mix_b_gate, conv_w, q_norm_g, w_uq, kv_norm_g, w_ukv, w_br_conv, w_br_mla, w_o, ffn2_w_up, ffn2_w_down, ln_g, ln_b, loss_target, m_meta_tokens, m_ffn1_w_up, m_ffn1_w_down, m_mix_w_in, m_mix_b_gate, m_conv_w, m_q_norm_g, m_w_uq, m_kv_norm_g, m_w_ukv, m_w_br_conv, m_w_br_mla, m_w_o, m_ffn2_w_up, m_ffn2_w_down, m_ln_g, m_ln_b, v_meta_tokens, v_ffn1_w_up, v_ffn1_w_down, v_mix_w_in, v_mix_b_gate, v_conv_w, v_q_norm_g, v_w_uq, v_kv_norm_g, v_w_ukv, v_w_br_conv, v_w_br_mla, v_w_o, v_ffn2_w_up, v_ffn2_w_down, v_ln_g, v_ln_b):
    local = dict(meta_tokens=meta_tokens, ffn1_w_up=ffn1_w_up, ffn1_w_down=ffn1_w_down, mix_w_in=mix_w_in, mix_b_gate=mix_b_gate,
                 conv_w=conv_w, q_norm_g=q_norm_g, w_uq=w_uq, kv_norm_g=kv_norm_g, w_ukv=w_ukv, w_br_conv=w_br_conv,
                 w_br_mla=w_br_mla, w_o=w_o, ffn2_w_up=ffn2_w_up, ffn2_w_down=ffn2_w_down, ln_g=ln_g, ln_b=ln_b)
    mom_m = dict(zip(WEIGHTS, (m_meta_tokens, m_ffn1_w_up, m_ffn1_w_down, m_mix_w_in, m_mix_b_gate, m_conv_w, m_q_norm_g, m_w_uq,
                               m_kv_norm_g, m_w_ukv, m_w_br_conv, m_w_br_mla, m_w_o, m_ffn2_w_up, m_ffn2_w_down, m_ln_g, m_ln_b)))
    mom_v = dict(zip(WEIGHTS, (v_meta_tokens, v_ffn1_w_up, v_ffn1_w_down, v_mix_w_in, v_mix_b_gate, v_conv_w, v_q_norm_g, v_w_uq,
                               v_kv_norm_g, v_w_ukv, v_w_br_conv, v_w_br_mla, v_w_o, v_ffn2_w_up, v_ffn2_w_down, v_ln_g, v_ln_b)))
    xi, yi, ci = _place()
    chip = 2 * xi + yi

    shards = {n: local[n].astype(BF) for n in BIG}
    axis = dict(zip(BIG, BIG_AXIS))
    pad_layout = {"mix_w_in": _pad_w_in, "w_uq": _pad_w_uq, "w_br_mla": _pad_w_br_mla}
    w = {n: [None] * DEPTH for n in BIG}

    def fetch(keys):
        def install(gathered):
            for (n, l), g in zip(keys, gathered):
                if n in FFN1_MATRICES + FFN2_MATRICES:
                    w[n][l] = g
                    continue
                full = jnp.concatenate([g[j] for j in range(4)], axis=axis[n] - 1)
                w[n][l] = pad_layout[n](full) if n in pad_layout else full
        return gather_side([(shards[n], l) for n, l in keys]), install

    first, install_first = fetch([(n, 0) for n in FFN1_MATRICES])
    install_first(exchange_alone(first, name="gather_weights"))
    fetch_under = {("ffn1_fwd", 0): [(n, 0) for n in MIXER_MATRICES],
                   ("attn_fwd", 0): [(n, 0) for n in FFN2_MATRICES] + [(n, 1) for n in BIG]}
    hooks = {where: functools.partial(lambda l, where: fetch(fetch_under[where, l]) if (where, l) in fetch_under else None, where=where)
             for where in ("ffn1_fwd", "attn_fwd")}
    stacked = exchange_small([_view2d(local[n]) for n in SMALL_SHARDED], reduce=False, name="gather_small")
    for n, st in zip(SMALL_SHARDED, stacked):
        full = jnp.concatenate([st[2 * j] for j in range(4)], axis=-1)
        w[n] = full.reshape(local[n].shape[:-1] + (full.shape[-1],))
    for n in SMALL_REPLICATED:
        w[n] = local[n]

    c_idx = jnp.reshape(ci, (1,)).astype(jnp.int32)
    done, from_chips = {}, {}

    def send(keys, grad_of):
        glist = [grad_of[k] for k in keys]
        from_sibling = pair_exchange(glist, name="rs_pair_exchange")
        sums = [pair_add(a, s, c_idx, name="rs_pair_add") for a, s in zip(glist, from_sibling)]
        return scatter_side(sums), lambda results: from_chips.update(zip(keys, results))

    send_under = {"attn_bwd": FFN2_MATRICES + ("w_o", "w_br_conv", "w_br_mla"),
                  "ffn1_down_bwd": ("mix_w_in", "w_uq", "w_ukv", "ffn1_w_down"), "ffn1_up_bwd": ("ffn1_w_up",)}
    hooks["layer_bwd_done"] = lambda l, g: done.update({(n, l): g[n] for n in BIG})

    def send_hook(where):
        def hook(l, g):
            if l != 0:
                return None
            keys = [(n, 0) for n in send_under[where]] + ([(n, 1) for n in BIG] if where == "attn_bwd" else [])
            return send(keys, {**done, **{(n, 0): g[n] for n in send_under[where]}})
        return hook

    for where in send_under:
        hooks[where] = send_hook(where)

    loss_acc, grad_x, d_meta, grads = _local_grads(x[0], loss_target[0], w, hooks)
    grad_x = grad_x[None]
    keys = [(n, l) for n in BIG for l in range(DEPTH)]
    reduced = pair_gather([sum_chunks(from_chips[k], name="rs_sum") for k in keys], name="rs_pair_gather")
    reduced = {k: r.reshape(local[k[0]].shape[1:]) for k, r in zip(keys, reduced)}
    gshard = {n: jnp.stack([reduced[n, l] for l in range(DEPTH)]) for n in BIG}

    small_names = SMALL_SHARDED + SMALL_REPLICATED
    gsmall = {n: jnp.concatenate([grads[l][n] for l in range(DEPTH)], axis=0) for n in small_names if n != "meta_tokens"}
    gsmall["meta_tokens"] = d_meta
    small_red = exchange_small([gsmall[n] for n in small_names] + [loss_acc], reduce=True, name="reduce_small")
    loss = small_red[-1][0, 0]
    for n, full in zip(small_names, small_red[:-1]):
        if n in SMALL_SHARDED:
            sh = local[n].shape[-1]
            full = lax.dynamic_slice_in_dim(full, chip * sh, sh, axis=1)
        gshard[n] = full.reshape(local[n].shape)

    delta, new_m, new_v = {}, {}, {}
    for n in WEIGHTS:
        shape = local[n].shape
        d, nm, nv = adamw(_view2d(local[n]), _view2d(gshard[n]), _view2d(mom_m[n]), _view2d(mom_v[n]), name="adamw")
        delta[n], new_m[n], new_v[n] = d.reshape(shape), nm.reshape(shape), nv.reshape(shape)
    return (loss, grad_x, *[gshard[n] for n in WEIGHTS], *[delta[n] for n in WEIGHTS], *[new_m[n] for n in WEIGHTS],
            *[new_v[n] for n in WEIGHTS])
```

```python
import functools

import jax
import jax.numpy as jnp
from jax import lax
from jax.experimental import pallas as pl
from jax.experimental.pallas import tpu as pltpu

F32 = jnp.float32
BF = jnp.bfloat16
MESH = pl.DeviceIdType.MESH

D_MODEL = 1024
DEPTH = 2
N_META = 16
D_CONV = 512
MLA_HEADS = 8
QK_NOPE = 64
QK_ROPE = 32
V_HEAD = 64
Q_LORA = 256
KV_LORA = 128
ROPE_BASE = 10000.0
NEG_INF = -1e30
D_FF = 2816
ALPHA = (2 * DEPTH) ** 0.25
LN_EPS = 1e-5
RMS_EPS = 1e-6
ATT_SCALE = (QK_NOPE + QK_ROPE) ** -0.5
LOG2E = 1.4426950408889634
LN2 = 0.6931471805599453
D_IN = 4000
D_IN_PAD = 4096
Q_PAD = 256
KV_PAD = 128

ADAM_LR = 0.001
ADAM_B1 = 0.9
ADAM_B2 = 0.999
ADAM_EPS = 1e-08
ADAM_WD = 0.01
ADAM_STEP = 10

TM = 768
TMH = 384
LANES = 128
VMEM_LIMIT_BYTES = 50 * 1024 * 1024

NT = (((1,), (1,)), ((), ()))
TN = (((0,), (0,)), ((), ()))


def _pcall(body, **kw):
    return pl.pallas_call(body, **kw)


def _params(n_axes):
    return pltpu.CompilerParams(dimension_semantics=("arbitrary",) * n_axes, vmem_limit_bytes=VMEM_LIMIT_BYTES)


def _sds(shape, dtype):
    return jax.ShapeDtypeStruct(shape, dtype)


def mm_rows(pairs, n_out, *, name, tn=None, addend=None, add_scale=1.0, out_dtype=F32, side=None):
    tp = pairs[0][0].shape[0]
    tn = tn or n_out
    in_specs, args = [], []
    for a, b, nt, kb in pairs:
        k = a.shape[1]
        in_specs.append(pl.BlockSpec((TM, k), lambda i, j: (i, 0)))
        if nt and b.ndim == 3:
            in_specs.append(pl.BlockSpec((2, tn, k // 2), functools.partial(lambda i, j, kb: (kb, j, 0), kb=kb)))
        elif nt:
            in_specs.append(pl.BlockSpec((tn, k), functools.partial(lambda i, j, kb: (j, kb), kb=kb)))
        else:
            in_specs.append(pl.BlockSpec((k, tn), lambda i, j: (0, j)))
        args += [a, b]
    if addend is not None:
        in_specs.append(pl.BlockSpec((TM, tn), lambda i, j: (i, j)))
        args.append(addend)
    n_pairs = len(pairs)
    nts = [p[2] for p in pairs]

    def body(refs, out_refs, scratch):
        o_ref = out_refs[0]
        acc = None
        for p in range(n_pairs):
            a = refs[2 * p][...].astype(BF)
            b = refs[2 * p + 1][...]
            if b.ndim == 3:
                b = jnp.concatenate([b[0], b[1]], axis=1)
            d = lax.dot_general(a, b, NT if nts[p] else (((1,), (0,)), ((), ())), preferred_element_type=F32)
            acc = d if acc is None else acc + d
        if addend is not None:
            acc = acc + add_scale * refs[2 * n_pairs][...].astype(F32)
        o_ref[...] = acc.astype(o_ref.dtype)

    out = _side_call(
        body, side, name=name, grid=(tp // TM, n_out // tn), in_specs=in_specs,
        out_specs=[pl.BlockSpec((TM, tn), lambda i, j: (i, j))], out_shape=[_sds((tp, n_out), out_dtype)],
        scratch_shapes=[], args=args,
    )
    return out if side is not None else out[0]


def tn_mm(a, b, *, tm, name, out_dtype=BF, shard=None, slot0=0, dst=None):
    tp, m = a.shape
    n = b.shape[1]
    nk = tp // TM
    if shard is None:
        pieces, out_block, out_index, out_full = 1, (tm, n), (lambda i, k: (i, 0)), (m, n)
    elif shard[0] == "cols":
        pieces = n // shard[1]
        out_block, out_full = (pieces, tm, shard[1]), (4, m, shard[1])
        out_index = lambda i, k: (slot0 // pieces, i, 0)
    else:
        pieces = tm // shard[1]
        out_block, out_full = (pieces, shard[1], n), (4, m // 4, n)
        out_index = lambda i, k: (i, 0, 0)

    def body(a_ref, b_ref, *rest):
        o_ref, acc_ref = rest[-2], rest[-1]
        k = pl.program_id(1)

        @pl.when(k == 0)
        def _():
            acc_ref[...] = jnp.zeros_like(acc_ref)

        acc_ref[...] += lax.dot_general(a_ref[...].astype(BF), b_ref[...].astype(BF), TN, preferred_element_type=F32)

        @pl.when(k == nk - 1)
        def _():
            if shard is None:
                o_ref[...] = acc_ref[...].astype(o_ref.dtype)
            elif shard[0] == "cols":
                for j in range(pieces):
                    o_ref[j] = acc_ref[:, j * shard[1]:(j + 1) * shard[1]].astype(o_ref.dtype)
            else:
                for j in range(pieces):
                    o_ref[j] = acc_ref[j * shard[1]:(j + 1) * shard[1], :].astype(o_ref.dtype)

    in_specs = [pl.BlockSpec((TM, tm), lambda i, k: (k, i)), pl.BlockSpec((TM, n), lambda i, k: (k, 0))]
    args = [a, b]
    aliases = {}
    if dst is not None:
        in_specs.append(pl.BlockSpec(memory_space=pl.ANY))
        args.append(dst)
        aliases = {2: 0}
    return _pcall(
        body, name=name, grid=(m // tm, nk), in_specs=in_specs, out_specs=pl.BlockSpec(out_block, out_index),
        out_shape=_sds(out_full, out_dtype), input_output_aliases=aliases,
        scratch_shapes=[pltpu.VMEM((tm, n), F32)], compiler_params=_params(2),
    )(*args)


def _ln_store(z, g_ref, b_ref, xh_ref, rs_ref, hb_ref):
    mu = jnp.mean(z, axis=-1, keepdims=True)
    zc = z - mu
    var = jnp.mean(zc * zc, axis=-1, keepdims=True)
    rstd = lax.rsqrt(var + LN_EPS)
    xh = zc * rstd
    xh_ref[...] = xh
    rs_ref[...] = rstd
    hb_ref[...] = (xh * g_ref[...] + b_ref[...]).astype(BF)


def _ln_out(tp, tm=TM):
    specs = [pl.BlockSpec((tm, D_MODEL), lambda i: (i, 0)), pl.BlockSpec((tm, 1), lambda i: (i, 0)),
             pl.BlockSpec((tm, D_MODEL), lambda i: (i, 0))]
    shapes = [_sds((tp, D_MODEL), F32), _sds((tp, 1), F32), _sds((tp, D_MODEL), BF)]
    return specs, shapes


def _row_vec(n):
    return pl.BlockSpec((1, n), lambda i: (0, 0))


def ffn_up(hb, wup, *, name, side=None):
    tp = hb.shape[0]
    tn = D_FF // 2
    nj = D_FF // tn

    def body(in_refs, out_refs, scratch):
        h_ref, wg_ref, wu_ref = in_refs
        g_ref, u_ref, a_ref = out_refs
        h = h_ref[...]
        g = jnp.dot(h, wg_ref[0], preferred_element_type=F32)
        u = jnp.dot(h, wu_ref[0], preferred_element_type=F32)
        g_ref[...] = g.astype(BF)
        u_ref[...] = u.astype(BF)
        a_ref[...] = (g * jax.nn.sigmoid(g) * u).astype(BF)

    blk = pl.BlockSpec((TM, tn), lambda i, j: (i, j))
    return _side_call(
        body, side, name=name, grid=(tp // TM, nj),
        in_specs=[pl.BlockSpec((TM, D_MODEL), lambda i, j: (i, 0)), pl.BlockSpec((1, D_MODEL, tn), lambda i, j: (j, 0, 0)),
                  pl.BlockSpec((1, D_MODEL, tn), lambda i, j: (j + nj, 0, 0))],
        out_specs=[blk, blk, blk], out_shape=[_sds((tp, D_FF), BF)] * 3, scratch_shapes=[], args=(hb, wup, wup),
    )


def down_ln(a, wd, xprev, gp, bp, g, b, *, name):
    tp = a.shape[0]

    def body(a_ref, wd_ref, xp_ref, gp_ref, bp_ref, g_ref, b_ref, xh_ref, rs_ref, hb_ref):
        wd = jnp.concatenate([wd_ref[j] for j in range(4)], axis=0)
        f = jnp.dot(a_ref[...], wd, preferred_element_type=F32)
        hprev = xp_ref[...] * gp_ref[...] + bp_ref[...]
        _ln_store(ALPHA * hprev + 0.5 * f, g_ref, b_ref, xh_ref, rs_ref, hb_ref)

    out_specs, out_shape = _ln_out(tp)
    return _pcall(
        body, name=name, grid=(tp // TM,),
        in_specs=[pl.BlockSpec((TM, D_FF), lambda i: (i, 0)), pl.BlockSpec((4, D_FF // 4, D_MODEL), lambda i: (0, 0, 0)),
                  pl.BlockSpec((TM, D_MODEL), lambda i: (i, 0))] + [_row_vec(D_MODEL)] * 4,
        out_specs=out_specs, out_shape=out_shape, compiler_params=_params(1),
    )(a, wd, xprev, gp, bp, g, b)


def _rope(x, c, s1, s2, reps):
    n = x.shape[1]
    if reps > 1:
        c, s1, s2 = (jnp.tile(t, (1, reps)) for t in (c, s1, s2))
    return x * c + pltpu.roll(x, 16, 1) * s1 + pltpu.roll(x, n - 16, 1) * s2


def _rms(x, g):
    r = lax.rsqrt(jnp.mean(x * x, axis=-1, keepdims=True) + RMS_EPS)
    return x * r * g, r


def mla_prep(p, gq, gkv, wuq_p, wukv, tabs, *, name):
    tp = p.shape[0]
    nh = MLA_HEADS

    def body(cq_ref, ckv_ref, kr_ref, gq_ref, gkv_ref, wuq_ref, wukv_ref, cq_t, s1q_t, s2q_t, ck_t, s1k_t, s2k_t,
             cqn_ref, ckvn_ref, q2_ref, kv_ref, krr_ref):
        cqn, _ = _rms(cq_ref[...].astype(F32), gq_ref[...])
        ckvn, _ = _rms(ckv_ref[...].astype(F32), gkv_ref[...])
        cqn = cqn.astype(BF)
        ckvn = ckvn.astype(BF)
        cqn_ref[...] = cqn
        ckvn_ref[...] = ckvn
        q = jnp.dot(cqn, wuq_ref[...], preferred_element_type=F32)
        q2_ref[...] = _rope(q, cq_t[...], s1q_t[...], s2q_t[...], nh).astype(BF)
        kv_ref[...] = jnp.dot(ckvn, wukv_ref[...], preferred_element_type=F32).astype(BF)
        krr_ref[...] = _rope(kr_ref[...].astype(F32), ck_t[...], s1k_t[...], s2k_t[...], 1).astype(BF)

    def rows(n, col=0):
        return pl.BlockSpec((TMH, n), functools.partial(lambda i, col: (i, col), col=col))

    return _pcall(
        body, name=name, grid=(tp // TMH,),
        in_specs=[rows(Q_LORA, 1536 // Q_LORA), rows(KV_LORA, 1792 // KV_LORA), rows(LANES, 1920 // LANES),
                  _row_vec(Q_LORA), _row_vec(KV_LORA),
                  pl.BlockSpec((Q_LORA, nh * Q_PAD), lambda i: (0, 0)), pl.BlockSpec((KV_LORA, nh * KV_PAD), lambda i: (0, 0)),
                  rows(Q_PAD), rows(Q_PAD), rows(Q_PAD), rows(LANES), rows(LANES), rows(LANES)],
        out_specs=[rows(Q_LORA), rows(KV_LORA), rows(nh * Q_PAD), rows(nh * KV_PAD), rows(LANES)],
        out_shape=[_sds((tp, Q_LORA), BF), _sds((tp, KV_LORA), BF), _sds((tp, nh * Q_PAD), BF),
                   _sds((tp, nh * KV_PAD), BF), _sds((tp, LANES), BF)],
        compiler_params=_params(1),
    )(p, p, p, gq, gkv, wuq_p, wukv, *tabs)


def _causal_mask(s):
    qpos = lax.broadcasted_iota(jnp.int32, (TM, TM), 0)
    kpos = lax.broadcasted_iota(jnp.int32, (TM, TM), 1)
    return jnp.where(kpos <= qpos, s, NEG_INF)


def _key_rows(k):
    return pl.ds(pl.multiple_of(k * TM, TM), TM)


def _pipelined_key_blocks(n, prefetch, process):
    prefetch(0, 0)

    def pair(j, carry):
        prefetch(2 * j + 1, 1)
        process(2 * j, 0, False)
        prefetch(2 * j + 2, 0)
        process(2 * j + 1, 1, False)
        return carry

    lax.fori_loop(0, n // 2, pair, 0)

    @pl.when(n % 2 == 1)
    def _():
        prefetch(n, 1)
        process(n - 1, 0, False)
        process(n, 1, True)

    @pl.when(n % 2 == 0)
    def _():
        process(n, 0, True)


def _side_call(body_main, side, *, name, grid, in_specs, out_specs, out_shape, scratch_shapes, args):
    n_in, n_out, n_scr = len(in_specs), len(out_specs), len(scratch_shapes)
    s_in, s_pre, n_sems, program = side if side is not None else ((), (), 0, None)
    a, b = len(s_in), len(s_pre)

    def body(*refs):
        in_refs = refs[:n_in]
        out_refs = refs[n_in + a + b:n_in + a + b + n_out]
        scr = refs[n_in + a + 2 * b + n_out:n_in + a + 2 * b + n_out + n_scr]
        if side is not None:
            side_in = refs[n_in:n_in + a]
            side_out = refs[n_in + a + b + n_out:n_in + a + 2 * b + n_out]
            start, finish = program(side_in, side_out, refs[-2], refs[-1])

            @pl.when((pl.program_id(0) == 0) & (pl.program_id(1) == 0))
            def _():
                start()

        body_main(in_refs, out_refs, scr)
        if side is not None:
            @pl.when((pl.program_id(0) == grid[0] - 1) & (pl.program_id(1) == grid[1] - 1))
            def _():
                finish()

    sems = [pltpu.SemaphoreType.DMA((n_sems,))] * 2 if side is not None else []
    return _pcall(
        body, name=name, grid=grid, in_specs=list(in_specs) + [HBM_SPEC] * (a + b), out_specs=list(out_specs) + [HBM_SPEC] * b,
        out_shape=list(out_shape) + [_sds(p.shape, p.dtype) for p in s_pre],
        input_output_aliases={n_in + a + i: n_out + i for i in range(b)},
        scratch_shapes=list(scratch_shapes) + sems, compiler_params=_params(2),
    )(*args, *s_in, *s_pre)


def attn_fwd(q2, kv, kr, *, name, side=None):
    tp = q2.shape[0]
    nh = MLA_HEADS
    nb = tp // TM
    rep = TM // LANES

    def body(in_refs, out_refs, scratch):
        q_ref, kv_ref, kr_ref = in_refs
        o_ref, lse_ref = out_refs
        m_ref, l_ref, acc_ref, s0_ref, s1_ref, p_ref, alpha_ref = scratch
        qi = pl.program_id(1)
        s_refs = (s0_ref, s1_ref)
        m_ref[...] = jnp.full_like(m_ref, NEG_INF)
        l_ref[...] = jnp.zeros_like(l_ref)
        acc_ref[...] = jnp.zeros_like(acc_ref)

        def prefetch(k, slot):
            k2 = jnp.concatenate([kv_ref[_key_rows(k), :], kr_ref[_key_rows(k), :]], axis=1)
            s_refs[slot][...] = lax.dot_general(q_ref[...], k2, NT, preferred_element_type=F32)

        def process(k, slot, diagonal):
            for r in range(TM // LANES):
                rows = slice(r * LANES, (r + 1) * LANES)
                s = s_refs[slot][rows, :]
                if diagonal:
                    qpos = r * LANES + lax.broadcasted_iota(jnp.int32, (LANES, TM), 0)
                    s = jnp.where(lax.broadcasted_iota(jnp.int32, (LANES, TM), 1) <= qpos, s, NEG_INF)
                m_prev = m_ref[rows, :]
                m_new = jnp.maximum(m_prev, jnp.max(s, axis=1, keepdims=True))
                alpha = jnp.exp2(m_prev - m_new)
                p = jnp.exp2(s - jnp.tile(m_new, (1, rep)))
                lane_sums = p[:, 0:LANES]
                for t in range(1, rep):
                    lane_sums = lane_sums + p[:, t * LANES:(t + 1) * LANES]
                l_ref[rows, :] = alpha * l_ref[rows, :] + lane_sums
                p_ref[rows, :] = p.astype(BF)
                alpha_ref[rows, :] = alpha
                m_ref[rows, :] = m_new
            acc_ref[...] = alpha_ref[...] * acc_ref[...] + jnp.dot(p_ref[...], kv_ref[_key_rows(k), :], preferred_element_type=F32)

        _pipelined_key_blocks(qi, prefetch, process)
        l = jnp.sum(l_ref[...], axis=1, keepdims=True)
        o_ref[...] = (acc_ref[...] / l).astype(BF)
        lse_ref[0] = jnp.transpose(m_ref[...] + jnp.log2(l))[0:1, :]

    return _side_call(
        body, side, name=name, grid=(nh, nb),
        in_specs=[pl.BlockSpec((TM, Q_PAD), lambda h, qi: (qi, h)), pl.BlockSpec((tp, KV_PAD), lambda h, qi: (0, h)),
                  pl.BlockSpec((tp, LANES), lambda h, qi: (0, 0))],
        out_specs=[pl.BlockSpec((TM, KV_PAD), lambda h, qi: (qi, h)), pl.BlockSpec((1, 1, TM), lambda h, qi: (h * nb + qi, 0, 0))],
        out_shape=[_sds((tp, nh * KV_PAD), BF), _sds((nh * nb, 1, TM), F32)],
        scratch_shapes=[pltpu.VMEM((TM, LANES), F32)] * 3 + [pltpu.VMEM((TM, TM), F32)] * 2
        + [pltpu.VMEM((TM, TM), BF), pltpu.VMEM((TM, LANES), F32)], args=(q2, kv, kr),
    )


def conv_fwd(p, w, *, name):
    tp = p.shape[0]

    def body(b_ref, c_ref, h_ref, w_ref, y_ref, cv_ref, ebuf):
        i = pl.program_id(0)

        @pl.when(i == 0)
        def _():
            ebuf[0:8, :] = jnp.zeros((8, D_CONV), F32)

        e = c_ref[...].astype(F32) * h_ref[...].astype(F32)
        ebuf[8:8 + TM, :] = e
        w_all = w_ref[...]
        conv = w_all[0:1] * ebuf[pl.ds(6, TM), :] + w_all[1:2] * ebuf[pl.ds(7, TM), :] + w_all[2:3] * e
        cv_ref[...] = conv.astype(BF)
        y_ref[...] = (b_ref[...].astype(F32) * conv).astype(BF)
        ebuf[0:8, :] = ebuf[TM:TM + 8, :]

    def col(j):
        return pl.BlockSpec((TM, D_CONV), functools.partial(lambda i, j: (i, j), j=j))

    return _pcall(
        body, name=name, grid=(tp // TM,),
        in_specs=[col(0), col(1), col(2), pl.BlockSpec((3, D_CONV), lambda i: (0, 0))],
        out_specs=[col(0), col(0)], out_shape=[_sds((tp, D_CONV), BF)] * 2,
        scratch_shapes=[pltpu.VMEM((TM + 8, D_CONV), F32)], compiler_params=_params(1),
    )(p, p, p, w)


def merge_out_ln(ycv, o2, p, bg, wbc, wbm_p, wo, xprev, gp, bp, g, b, *, name):
    tp = ycv.shape[0]

    def body(y_ref, o_ref, gc_ref, gm_ref, bg_ref, wbc_ref, wbm_ref, wo_ref, xp_ref, gp_ref, bp_ref, g_ref, b_ref,
             bc_ref, bm_ref, mg_ref, xh_ref, rs_ref, hb_ref):
        bc = jnp.dot(y_ref[...], wbc_ref[...], preferred_element_type=F32)
        bm = jnp.dot(o_ref[...], wbm_ref[...], preferred_element_type=F32)
        bgv = bg_ref[...]
        mg = (jax.nn.sigmoid(gc_ref[...].astype(F32) + bgv[0:1]) * bc
              + jax.nn.sigmoid(gm_ref[...].astype(F32) + bgv[1:2]) * bm)
        mgb = mg.astype(BF)
        bc_ref[...] = bc.astype(BF)
        bm_ref[...] = bm.astype(BF)
        mg_ref[...] = mgb
        mix = jnp.dot(mgb, wo_ref[...], preferred_element_type=F32)
        hprev = xp_ref[...] * gp_ref[...] + bp_ref[...]
        _ln_store(ALPHA * hprev + mix, g_ref, b_ref, xh_ref, rs_ref, hb_ref)

    def rows(n, col=0):
        return pl.BlockSpec((TMH, n), functools.partial(lambda i, col: (i, col), col=col))

    def whole(r, c):
        return pl.BlockSpec((r, c), lambda i: (0, 0))

    ln_specs, ln_shapes = _ln_out(tp, TMH)
    return _pcall(
        body, name=name, grid=(tp // TMH,),
        in_specs=[rows(D_CONV), rows(MLA_HEADS * KV_PAD), rows(D_MODEL, 2), rows(D_MODEL, 3), whole(2, D_MODEL),
                  whole(D_CONV, D_MODEL), whole(MLA_HEADS * KV_PAD, D_MODEL), whole(D_MODEL, D_MODEL), rows(D_MODEL)]
        + [_row_vec(D_MODEL)] * 4,
        out_specs=[rows(D_MODEL)] * 3 + ln_specs, out_shape=[_sds((tp, D_MODEL), BF)] * 3 + ln_shapes,
        compiler_params=_params(1),
    )(ycv, o2, p, p, bg, wbc, wbm_p, wo, xprev, gp, bp, g, b)


def ln_bwd(dh, xh, rstd, g, *, branch_scale, name):
    tp = xh.shape[0]
    from_loss = isinstance(dh, tuple)

    def body(*refs):
        if from_loss:
            xh_ref, rs_ref, g_ref, b_ref, t_ref, dzb_ref, dg_ref, db_ref, loss_ref = refs
        else:
            dh_ref, xh_ref, rs_ref, g_ref, dzb_ref, dg_ref, db_ref = refs
        i = pl.program_id(0)

        @pl.when(i == 0)
        def _():
            dg_ref[...] = jnp.zeros_like(dg_ref)
            db_ref[...] = jnp.zeros_like(db_ref)
            if from_loss:
                loss_ref[...] = jnp.zeros_like(loss_ref)

        xhat = xh_ref[...]
        if from_loss:
            row = i * TM + lax.broadcasted_iota(jnp.int32, (TM, 1), 0)
            real = (row >= N_META) & (row < N_META + dh[2])
            diff = jnp.where(real, xhat * g_ref[...] + b_ref[...] - t_ref[...], 0.0)
            loss_ref[...] += 0.5 / D_MODEL * jnp.sum(diff * diff)
            dy = diff * (1.0 / D_MODEL)
        else:
            dy = dh_ref[...]
        dg_ref[...] += jnp.sum(dy * xhat, axis=0, keepdims=True)
        db_ref[...] += jnp.sum(dy, axis=0, keepdims=True)
        dxh = dy * g_ref[...]
        m1 = jnp.mean(dxh, axis=-1, keepdims=True)
        m2 = jnp.mean(dxh * xhat, axis=-1, keepdims=True)
        dz = rs_ref[...] * (dxh - m1 - xhat * m2)
        dzb_ref[...] = (branch_scale * dz).astype(BF)

    rows = pl.BlockSpec((TM, D_MODEL), lambda i: (i, 0))
    stat = pl.BlockSpec((TM, 1), lambda i: (i, 0))
    vec = _row_vec(D_MODEL)
    out_specs = [rows, vec, vec]
    out_shape = [_sds((tp, D_MODEL), BF), _sds((1, D_MODEL), F32), _sds((1, D_MODEL), F32)]
    if from_loss:
        in_specs, args = [rows, stat, vec, vec, rows], (xh, rstd, g, dh[0], dh[1])
        out_specs.append(pl.BlockSpec((8, LANES), lambda i: (0, 0)))
        out_shape.append(_sds((8, LANES), F32))
    else:
        in_specs, args = [rows, rows, stat, vec], (dh, xh, rstd, g)
    return _pcall(body, name=name, grid=(tp // TM,), in_specs=in_specs, out_specs=out_specs, out_shape=out_shape,
                  compiler_params=_params(1))(*args)


def ffn_down_bwd(dzb, wd, gate, up, *, name, side=None):
    tp = dzb.shape[0]
    tn = D_FF // 2

    def body(in_refs, out_refs, scratch):
        dz_ref, wd_ref, g_ref, u_ref = in_refs
        dg_ref, du_ref = out_refs
        wd = jnp.concatenate([wd_ref[0], wd_ref[1]], axis=0)
        da = lax.dot_general(dz_ref[...], wd, NT, preferred_element_type=F32)
        g = g_ref[...].astype(F32)
        u = u_ref[...].astype(F32)
        sg = jax.nn.sigmoid(g)
        dg_ref[...] = (da * u * sg * (1.0 + g * (1.0 - sg))).astype(BF)
        du_ref[...] = (da * g * sg).astype(BF)

    blk = pl.BlockSpec((TM, tn), lambda i, j: (i, j))
    return _side_call(
        body, side, name=name, grid=(tp // TM, D_FF // tn),
        in_specs=[pl.BlockSpec((TM, D_MODEL), lambda i, j: (i, 0)), pl.BlockSpec((2, tn // 2, D_MODEL), lambda i, j: (j, 0, 0)), blk, blk],
        out_specs=[blk, blk], out_shape=[_sds((tp, D_FF), BF)] * 2, scratch_shapes=[], args=(dzb, wd, gate, up),
    )


def merge_bwd(dzb, wo, bc, bm, p, bg, wbc, wbm_p, *, name):
    tp = dzb.shape[0]
    nh = MLA_HEADS

    def body(dz_ref, wo_ref, bc_ref, bm_ref, gc_ref, gm_ref, bg_ref, wbc_ref, wbm_ref,
             dbc_ref, dbm_ref, dgg_ref, dy_ref, do_ref, dbg_ref):
        i = pl.program_id(0)

        @pl.when(i == 0)
        def _():
            dbg_ref[...] = jnp.zeros_like(dbg_ref)

        dmg = lax.dot_general(dz_ref[...], wo_ref[...], NT, preferred_element_type=F32)
        bgv = bg_ref[...]
        sc = jax.nn.sigmoid(gc_ref[...].astype(F32) + bgv[0:1])
        sm = jax.nn.sigmoid(gm_ref[...].astype(F32) + bgv[1:2])
        dbc = (dmg * sc).astype(BF)
        dbm = (dmg * sm).astype(BF)
        dgc = dmg * bc_ref[...].astype(F32) * sc * (1.0 - sc)
        dgm = dmg * bm_ref[...].astype(F32) * sm * (1.0 - sm)
        dbc_ref[...] = dbc
        dbm_ref[...] = dbm
        dgg_ref[...] = jnp.concatenate([dgc, dgm], axis=1).astype(BF)
        dbg_ref[...] += jnp.concatenate([jnp.sum(dgc, axis=0, keepdims=True), jnp.sum(dgm, axis=0, keepdims=True)], axis=0)
        dy_ref[...] = lax.dot_general(dbc, wbc_ref[...], NT, preferred_element_type=F32)
        do_ref[...] = lax.dot_general(dbm, wbm_ref[...], NT, preferred_element_type=F32).astype(BF)

    def rows(n, col=0):
        return pl.BlockSpec((TMH, n), functools.partial(lambda i, col: (i, col), col=col))

    def whole(r, c):
        return pl.BlockSpec((r, c), lambda i: (0, 0))

    return _pcall(
        body, name=name, grid=(tp // TMH,),
        in_specs=[rows(D_MODEL), whole(D_MODEL, D_MODEL), rows(D_MODEL), rows(D_MODEL), rows(D_MODEL, 2), rows(D_MODEL, 3),
                  whole(2, D_MODEL), whole(D_CONV, D_MODEL), whole(nh * KV_PAD, D_MODEL)],
        out_specs=[rows(D_MODEL), rows(D_MODEL), rows(2 * D_MODEL, 1), rows(D_CONV), rows(nh * KV_PAD), whole(2, D_MODEL)],
        out_shape=[_sds((tp, D_MODEL), BF), _sds((tp, D_MODEL), BF), _sds((tp, D_IN_PAD), BF), _sds((tp, D_CONV), F32),
                   _sds((tp, nh * KV_PAD), BF), _sds((2, D_MODEL), F32)],
        compiler_params=_params(1),
    )(dzb, wo, bc, bm, p, p, bg, wbc, wbm_p)


def attn_bwd(q2, kv, kr, do2, lse, o2, *, name, side=None):
    tp = q2.shape[0]
    nh = MLA_HEADS
    nb = tp // TM
    rep = TM // LANES

    def body(in_refs, out_refs, scratch):
        q_ref, kv_ref, kr_ref, do_ref, lse_ref, o_ref = in_refs
        dq_ref, dkv_ref, dkr_ref = out_refs
        dq_acc, s0_ref, s1_ref, dp0_ref, dp1_ref, lse_col, dl_col = scratch
        qi = pl.program_id(1)
        s_refs, dp_refs = (s0_ref, s1_ref), (dp0_ref, dp1_ref)
        lse_col[...] = jnp.transpose(jnp.broadcast_to(lse_ref[0], (LANES, TM)))
        delta = jnp.sum(do_ref[...].astype(F32) * o_ref[...].astype(F32), axis=1, keepdims=True)
        dl_col[...] = jnp.broadcast_to(delta, (TM, LANES))

        @pl.when(qi == 0)
        def _():
            dkv_ref[...] = jnp.zeros_like(dkv_ref)

        @pl.when((qi == 0) & (pl.program_id(0) == 0))
        def _():
            dkr_ref[...] = jnp.zeros_like(dkr_ref)

        dq_acc[...] = jnp.zeros_like(dq_acc)

        def prefetch(k, slot):
            kvb = kv_ref[_key_rows(k), :]
            k2 = jnp.concatenate([kvb, kr_ref[_key_rows(k), :]], axis=1)
            s_refs[slot][...] = lax.dot_general(q_ref[...], k2, NT, preferred_element_type=F32)
            dp_refs[slot][...] = lax.dot_general(do_ref[...], kvb, NT, preferred_element_type=F32)

        def process(k, slot, diagonal):
            rows = _key_rows(k)
            s = s_refs[slot][...]
            if diagonal:
                s = _causal_mask(s)
            p = jnp.exp2(s - jnp.tile(lse_col[...], (1, rep)))
            dsb = (p * (dp_refs[slot][...] - jnp.tile(dl_col[...], (1, rep)))).astype(BF)
            dk2 = lax.dot_general(dsb, q_ref[...], TN, preferred_element_type=F32) * LN2
            dkv_ref[rows, :] += lax.dot_general(p.astype(BF), do_ref[...], TN, preferred_element_type=F32) + dk2[:, :KV_PAD]
            dkr_ref[rows, :] += dk2[:, KV_PAD:KV_PAD + LANES]
            k2 = jnp.concatenate([kv_ref[rows, :], kr_ref[rows, :]], axis=1)
            dq_acc[...] += jnp.dot(dsb, k2, preferred_element_type=F32)

        _pipelined_key_blocks(qi, prefetch, process)
        dq_ref[...] = dq_acc[...].astype(BF)

    def qrow(n):
        return pl.BlockSpec((TM, n), lambda h, qi: (qi, h))

    def head(n):
        return pl.BlockSpec((tp, n), lambda h, qi: (0, h))

    return _side_call(
        body, side, name=name, grid=(nh, nb),
        in_specs=[qrow(Q_PAD), head(KV_PAD), pl.BlockSpec((tp, LANES), lambda h, qi: (0, 0)), qrow(KV_PAD),
                  pl.BlockSpec((1, 1, TM), lambda h, qi: (h * nb + qi, 0, 0)), qrow(KV_PAD)],
        out_specs=[qrow(Q_PAD), head(KV_PAD), pl.BlockSpec((tp, LANES), lambda h, qi: (0, 0))],
        out_shape=[_sds((tp, nh * Q_PAD), BF), _sds((tp, nh * KV_PAD), F32), _sds((tp, LANES), F32)],
        scratch_shapes=[pltpu.VMEM((TM, Q_PAD), F32)] + [pltpu.VMEM((TM, TM), F32)] * 4 + [pltpu.VMEM((TM, LANES), F32)] * 2,
        args=(q2, kv, kr, do2, lse, o2),
    )


def _rms_bwd(x, g, dy):
    r = lax.rsqrt(jnp.mean(x * x, axis=-1, keepdims=True) + RMS_EPS)
    gy = dy * g
    dx = r * gy - x * (r * r * r) * jnp.mean(x * gy, axis=-1, keepdims=True)
    return dx, jnp.sum(dy * x * r, axis=0, keepdims=True)


def mla_prep_bwd(dq2, dkv, dkr, p, gq, gkv, wuq_p, wukv, tabs_bwd, dp, *, name):
    tp = dq2.shape[0]
    nh = MLA_HEADS

    def body(dq_ref, dkv_ref, dkr_ref, cq_ref, ckv_ref, gq_ref, gkv_ref, wuq_ref, wukv_ref,
             cq_t, s1q_t, s2q_t, ck_t, s1k_t, s2k_t, dp_in_ref, dqb_ref, dsm_ref, dgq_ref, dgkv_ref):
        i = pl.program_id(0)

        @pl.when(i == 0)
        def _():
            dgq_ref[...] = jnp.zeros_like(dgq_ref)
            dgkv_ref[...] = jnp.zeros_like(dgkv_ref)

        dqb = _rope(dq_ref[...].astype(F32), cq_t[...], s1q_t[...], s2q_t[...], nh).astype(BF)
        dqb_ref[...] = dqb
        dcqn = lax.dot_general(dqb, wuq_ref[...], NT, preferred_element_type=F32)
        dcq, dgq = _rms_bwd(cq_ref[...].astype(F32), gq_ref[...], dcqn)
        dckvn = lax.dot_general(dkv_ref[...].astype(BF), wukv_ref[...], NT, preferred_element_type=F32)
        dckv, dgkv = _rms_bwd(ckv_ref[...].astype(F32), gkv_ref[...], dckvn)
        dkr = _rope(dkr_ref[...], ck_t[...], s1k_t[...], s2k_t[...], 1)
        dsm_ref[...] = jnp.concatenate([dcq, dckv, dkr], axis=1).astype(BF)
        dgq_ref[...] += dgq
        dgkv_ref[...] += dgkv

    def rows(n, col=0):
        return pl.BlockSpec((TMH, n), functools.partial(lambda i, col: (i, col), col=col))

    return _pcall(
        body, name=name, grid=(tp // TMH,),
        in_specs=[rows(nh * Q_PAD), rows(nh * KV_PAD), rows(LANES), rows(Q_LORA, 1536 // Q_LORA), rows(KV_LORA, 1792 // KV_LORA),
                  _row_vec(Q_LORA), _row_vec(KV_LORA),
                  pl.BlockSpec((Q_LORA, nh * Q_PAD), lambda i: (0, 0)), pl.BlockSpec((KV_LORA, nh * KV_PAD), lambda i: (0, 0)),
                  rows(Q_PAD), rows(Q_PAD), rows(Q_PAD), rows(LANES), rows(LANES), rows(LANES), pl.BlockSpec(memory_space=pl.ANY)],
        out_specs=[rows(nh * Q_PAD), rows(Q_LORA + KV_LORA + LANES, 1536 // (Q_LORA + KV_LORA + LANES)), _row_vec(Q_LORA),
                   _row_vec(KV_LORA)],
        out_shape=[_sds((tp, nh * Q_PAD), BF), _sds(dp.shape, dp.dtype), _sds((1, Q_LORA), F32), _sds((1, KV_LORA), F32)],
        input_output_aliases={15: 1}, compiler_params=_params(1),
    )(dq2, dkv, dkr, p, p, gq, gkv, wuq_p, wukv, *tabs_bwd, dp)


def conv_bwd(dy, p, conv, w, dp, *, name):
    tp = dy.shape[0]
    nb = tp // TM

    def body(dy_ref, b_ref, c_ref, h_ref, cv_ref, w_ref, dp_in_ref, dp_ref, dw0_ref, dw1_ref, dw2_ref, dbuf):
        i = pl.program_id(0)

        @pl.when(i == 0)
        def _():
            dbuf[TM:TM + 8, :] = jnp.zeros((8, D_CONV), F32)
            dw0_ref[...] = jnp.zeros_like(dw0_ref)
            dw1_ref[...] = jnp.zeros_like(dw1_ref)
            dw2_ref[...] = jnp.zeros_like(dw2_ref)

        dyv = dy_ref[...]
        c = c_ref[...].astype(F32)
        hh = h_ref[...].astype(F32)
        dconv = dyv * b_ref[...].astype(F32)
        dbuf[0:TM, :] = dconv
        d1 = dbuf[pl.ds(1, TM), :]
        d2 = dbuf[pl.ds(2, TM), :]
        w_all = w_ref[...]
        de = w_all[2:3] * dconv + w_all[1:2] * d1 + w_all[0:1] * d2
        e = c * hh
        dp_ref[...] = jnp.concatenate([dyv * cv_ref[...].astype(F32), de * hh, de * c], axis=1).astype(BF)
        dw0_ref[...] += jnp.sum(d2 * e, axis=0, keepdims=True)
        dw1_ref[...] += jnp.sum(d1 * e, axis=0, keepdims=True)
        dw2_ref[...] += jnp.sum(dconv * e, axis=0, keepdims=True)
        dbuf[TM:TM + 8, :] = dbuf[0:8, :]

    def col(j):
        return pl.BlockSpec((TM, D_CONV), functools.partial(lambda i, j: (nb - 1 - i, j), j=j))

    return _pcall(
        body, name=name, grid=(nb,),
        in_specs=[col(0), col(0), col(1), col(2), col(0), pl.BlockSpec((3, D_CONV), lambda i: (0, 0)),
                  pl.BlockSpec(memory_space=pl.ANY)],
        out_specs=[pl.BlockSpec((TM, 3 * D_CONV), lambda i: (nb - 1 - i, 0))] + [_row_vec(D_CONV)] * 3,
        out_shape=[_sds(dp.shape, dp.dtype)] + [_sds((1, D_CONV), F32)] * 3, input_output_aliases={6: 0},
        scratch_shapes=[pltpu.VMEM((TM + 8, D_CONV), F32)], compiler_params=_params(1),
    )(dy, p, p, p, conv, w, dp)


def adamw(w, g, m, v, *, name):
    r, c = w.shape
    tr = r
    for cand in (256, 128, 64, 32, 16, 8):
        if r % cand == 0 and r > cand:
            tr = cand
            break

    def body(w_ref, g_ref, m_ref, v_ref, d_ref, nm_ref, nv_ref):
        gv = g_ref[...]
        nm = ADAM_B1 * m_ref[...] + (1.0 - ADAM_B1) * gv
        nv = ADAM_B2 * v_ref[...] + (1.0 - ADAM_B2) * (gv * gv)
        m_hat = nm / (1.0 - ADAM_B1 ** ADAM_STEP)
        v_hat = nv / (1.0 - ADAM_B2 ** ADAM_STEP)
        d_ref[...] = -ADAM_LR * (m_hat / (jnp.sqrt(v_hat) + ADAM_EPS) + ADAM_WD * w_ref[...])
        nm_ref[...] = nm
        nv_ref[...] = nv

    blk = pl.BlockSpec((tr, c), lambda i: (i, 0))
    return _pcall(
        body, name=name, grid=(r // tr,), in_specs=[blk] * 4, out_specs=[blk] * 3,
        out_shape=[_sds((r, c), F32)] * 3, compiler_params=_params(1),
    )(w, g, m, v)


HBM_SPEC = pl.BlockSpec(memory_space=pltpu.HBM)


def _place():
    return lax.axis_index("x"), lax.axis_index("y"), lax.axis_index("c")


def _other_chips(x, y):
    return [(1 - x, y), (x, 1 - y), (1 - x, 1 - y)]


def _half(ref_or_shape_rows, c):
    return pl.ds(c * (ref_or_shape_rows // 2), ref_or_shape_rows // 2)


def gather_side(items):
    n = len(items)
    shards = [s for s, _ in items]
    layers = [l for _, l in items]

    def program(x_refs, o_refs, send_sems, recv_sems):
        x, y, c = _place()
        me = 2 * x + y
        chips = _other_chips(x, y)

        def copy(sem, src, dst, to):
            return pltpu.make_async_remote_copy(src_ref=src, dst_ref=dst, send_sem=send_sems.at[sem], recv_sem=recv_sems.at[sem],
                                                device_id=to, device_id_type=MESH)

        def src(i):
            return x_refs[i].at[layers[i], _half(x_refs[i].shape[1], c)]

        def dst(i, slot, cc):
            return o_refs[i].at[slot, _half(o_refs[i].shape[1], cc)]

        sends = [copy(6 * i + k, src(i), dst(i, me, c), (px, py, c)) for i in range(n) for k, (px, py) in enumerate(chips)]
        passed = [copy(6 * i + 3 + k, dst(i, 2 * px + py, c), dst(i, 2 * px + py, c), (x, y, 1 - c))
                  for k, (px, py) in enumerate(chips) for i in range(n)]

        def start():
            for cp in sends:
                cp.start()

        def finish():
            pos = 0
            for k, (px, py) in enumerate(chips):
                for i in range(n):
                    copy(6 * i + k, src(i), dst(i, 2 * px + py, c), (px, py, c)).wait_recv()
                    passed[pos].start()
                    pos += 1
            for k, (px, py) in enumerate(chips):
                for i in range(n):
                    copy(6 * i + 3 + k, dst(i, 2 * px + py, 1 - c), dst(i, 2 * px + py, 1 - c), (x, y, 1 - c)).wait_recv()
            for cp in sends + passed:
                cp.wait_send()

        return start, finish

    prefilled = [jnp.broadcast_to(s[l][None], (4,) + s.shape[1:]) for s, l in items]
    return shards, prefilled, 6 * n, program


def scatter_side(pss):
    n = len(pss)

    def program(p_refs, o_refs, send_sems, recv_sems):
        x, y, c = _place()
        me = 2 * x + y
        chips = _other_chips(x, y)

        def copy(i, k, j_src, j_dst, to):
            return pltpu.make_async_remote_copy(src_ref=p_refs[i].at[j_src], dst_ref=o_refs[i].at[j_dst],
                                                send_sem=send_sems.at[3 * i + k], recv_sem=recv_sems.at[3 * i + k],
                                                device_id=to, device_id_type=MESH)

        sends = [copy(i, k, 2 * px + py, me, (px, py, c)) for i in range(n) for k, (px, py) in enumerate(chips)]

        def start():
            for cp in sends:
                cp.start()

        def finish():
            for i in range(n):
                for k, (px, py) in enumerate(chips):
                    copy(i, k, me, 2 * px + py, (px, py, c)).wait_recv()
            for cp in sends:
                cp.wait_send()

        return start, finish

    xi, yi, _ = _place()
    own = jnp.arange(4)[:, None, None] == 2 * xi + yi
    prefilled = [jnp.where(own, p, jnp.zeros_like(p)) for p in pss]
    return list(pss), prefilled, 3 * n, program


def exchange_alone(side, *, name):
    inputs, prefilled, n_sems, program = side
    a, b = len(inputs), len(prefilled)

    def body(*refs):
        start, finish = program(refs[:a], refs[a + b:a + 2 * b], refs[-2], refs[-1])
        start()
        finish()

    return _pcall(
        body, name=name, in_specs=[HBM_SPEC] * (a + b), out_specs=[HBM_SPEC] * b, out_shape=[_sds(p.shape, p.dtype) for p in prefilled],
        input_output_aliases={a + i: i for i in range(b)}, scratch_shapes=[pltpu.SemaphoreType.DMA((n_sems,))] * 2,
    )(*inputs, *prefilled)


def pair_exchange(gs, *, name):
    n = len(gs)

    def body(*refs):
        g_refs, o_refs = refs[:n], refs[n:2 * n]
        send_sems, recv_sems = refs[2 * n:]
        x, y, c = _place()
        cps = [pltpu.make_async_remote_copy(src_ref=g_refs[i].at[:, _half(g_refs[i].shape[1], 1 - c)], dst_ref=o_refs[i],
                                            send_sem=send_sems.at[i], recv_sem=recv_sems.at[i], device_id=(x, y, 1 - c),
                                            device_id_type=MESH)
               for i in range(n)]
        for cp in cps:
            cp.start()
        for cp in cps:
            cp.wait()

    return _pcall(
        body, name=name, in_specs=[HBM_SPEC] * n, out_specs=[HBM_SPEC] * n,
        out_shape=[_sds((4, g.shape[1] // 2, g.shape[2]), g.dtype) for g in gs],
        scratch_shapes=[pltpu.SemaphoreType.DMA((n,)), pltpu.SemaphoreType.DMA((n,))],
    )(*gs)


def _comm_rows(a, b, itemsize):
    return a // 2 if a * b * itemsize > (3 << 19) and a % 16 == 0 else a


def pair_add(g, s1, c_idx, *, name):
    n, a, b = g.shape
    ah = a // 2
    ta = _comm_rows(ah, b, 2)
    nblk = ah // ta

    def body(c_ref, g_ref, s_ref, o_ref):
        o_ref[...] = (g_ref[...].astype(F32) + s_ref[...].astype(F32)).astype(o_ref.dtype)

    grid_spec = pltpu.PrefetchScalarGridSpec(
        num_scalar_prefetch=1, grid=(n, nblk),
        in_specs=[pl.BlockSpec((1, ta, b), lambda j, i, c_ref: (j, c_ref[0] * nblk + i, 0)),
                  pl.BlockSpec((1, ta, b), lambda j, i, c_ref: (j, i, 0))],
        out_specs=pl.BlockSpec((1, ta, b), lambda j, i, c_ref: (j, i, 0)),
    )
    return _pcall(body, name=name, grid_spec=grid_spec, out_shape=_sds((n, ah, b), g.dtype), compiler_params=_params(2))(
        c_idx, g, s1)


def sum_chunks(s2, *, name):
    n, a, b = s2.shape
    ta = _comm_rows(a, b, 4)

    def body(s_ref, o_ref):
        acc = s_ref[0].astype(F32)
        for j in range(1, n):
            acc = acc + s_ref[j].astype(F32)
        o_ref[...] = acc

    return _pcall(
        body, name=name, grid=(a // ta,), in_specs=[pl.BlockSpec((n, ta, b), lambda i: (0, i, 0))],
        out_specs=pl.BlockSpec((ta, b), lambda i: (i, 0)), out_shape=_sds((a, b), F32), compiler_params=_params(1),
    )(s2)


def pair_gather(rcs, *, name):
    n = len(rcs)

    def body(*refs):
        r_refs, o_refs = refs[:n], refs[2 * n:3 * n]
        send_sems, recv_sems = refs[3 * n:]
        x, y, c = _place()

        def copy(i, half):
            return pltpu.make_async_remote_copy(src_ref=r_refs[i], dst_ref=o_refs[i].at[half], send_sem=send_sems.at[i],
                                                recv_sem=recv_sems.at[i], device_id=(x, y, 1 - c), device_id_type=MESH)

        sends = [copy(i, c) for i in range(n)]
        for cp in sends:
            cp.start()
        for i in range(n):
            copy(i, 1 - c).wait_recv()
        for cp in sends:
            cp.wait_send()

    prefilled = [jnp.broadcast_to(r[None], (2,) + r.shape) for r in rcs]
    return _pcall(
        body, name=name, in_specs=[HBM_SPEC] * (2 * n), out_specs=[HBM_SPEC] * n,
        out_shape=[_sds(p.shape, p.dtype) for p in prefilled], input_output_aliases={n + i: i for i in range(n)},
        scratch_shapes=[pltpu.SemaphoreType.DMA((n,)), pltpu.SemaphoreType.DMA((n,))],
    )(*rcs, *prefilled)


def exchange_small(arrs, *, reduce, name):
    n = len(arrs)

    def body(*refs):
        v_refs, o_refs = refs[:n], refs[n:2 * n]
        bufs = refs[2 * n:3 * n] if reduce else o_refs
        send_sems, recv_sems = refs[-2:]
        x, y, c = _place()
        me = 4 * x + 2 * y + c
        for i in range(n):
            bufs[i][me] = v_refs[i][...]

        def peer(k):
            dx, dy, dc = (k >> 2) & 1, (k >> 1) & 1, k & 1
            return (1 - x if dx else x, 1 - y if dy else y, 1 - c if dc else c)

        def copy(i, k, slot):
            return pltpu.make_async_remote_copy(src_ref=v_refs[i], dst_ref=bufs[i].at[slot], send_sem=send_sems.at[7 * i + k - 1],
                                                recv_sem=recv_sems.at[7 * i + k - 1], device_id=peer(k), device_id_type=MESH)

        sends = [copy(i, k, me) for i in range(n) for k in range(1, 8)]
        for cp in sends:
            cp.start()
        for i in range(n):
            for k in range(1, 8):
                px, py, pc = peer(k)
                copy(i, k, 4 * px + 2 * py + pc).wait_recv()
        for cp in sends:
            cp.wait_send()
        if reduce:
            for i in range(n):
                acc = bufs[i][0]
                for d in range(1, 8):
                    acc = acc + bufs[i][d]
                o_refs[i][...] = acc

    vmem = pl.BlockSpec(memory_space=pltpu.VMEM)
    stacked = [(8,) + a.shape for a in arrs]
    return _pcall(
        body, name=name, in_specs=[vmem] * n, out_specs=[vmem] * n,
        out_shape=[_sds(a.shape if reduce else s, F32) for a, s in zip(arrs, stacked)],
        scratch_shapes=([pltpu.VMEM(s, F32) for s in stacked] if reduce else [])
        + [pltpu.SemaphoreType.DMA((7 * n,)), pltpu.SemaphoreType.DMA((7 * n,))],
    )(*arrs)


def _pad_rows(n, mult):
    return -(-n // mult) * mult


def _chip_major(g, b):
    return g.reshape(g.shape[0], 4, b).transpose(1, 0, 2)


def _rope_tables(tp):
    inv_freq = 1.0 / (ROPE_BASE ** (jnp.arange(0, QK_ROPE, 2, dtype=F32) / QK_ROPE))
    ang = jnp.arange(tp, dtype=F32)[:, None] * inv_freq[None, :]
    cos, sin = jnp.cos(ang), jnp.sin(ang)
    one = lambda n: jnp.ones((tp, n), F32)
    zero = lambda n: jnp.zeros((tp, n), F32)
    cq = jnp.concatenate([one(128), cos, cos, one(96)], axis=1)
    s1q = jnp.concatenate([zero(144), sin, zero(96)], axis=1)
    s2q = jnp.concatenate([zero(128), -sin, zero(112)], axis=1)
    ck = jnp.concatenate([cos, cos, zero(96)], axis=1)
    s1k = jnp.concatenate([zero(16), sin, zero(96)], axis=1)
    s2k = jnp.concatenate([-sin, zero(112)], axis=1)
    fwd = (cq * (ATT_SCALE * LOG2E), s1q * (ATT_SCALE * LOG2E), s2q * (ATT_SCALE * LOG2E), ck, s1k, s2k)
    bwd = (cq * ATT_SCALE, -s1q * ATT_SCALE, -s2q * ATT_SCALE, ck, -s1k, -s2k)
    return fwd, bwd


def _pad_w_in(w):
    return jnp.concatenate([w[:, :1952], jnp.zeros((w.shape[0], 96), w.dtype), w[:, 1952:]], axis=1)


def _pad_w_uq(w):
    w = w.reshape(Q_LORA, MLA_HEADS, QK_NOPE + QK_ROPE)
    z = lambda n: jnp.zeros((Q_LORA, MLA_HEADS, n), w.dtype)
    return jnp.concatenate([w[..., :QK_NOPE], z(64), w[..., QK_NOPE:], z(96)], axis=-1).reshape(Q_LORA, MLA_HEADS * Q_PAD)


def _unpad_w_uq(w):
    w = w.reshape(Q_LORA, MLA_HEADS, Q_PAD)
    return jnp.concatenate([w[..., :QK_NOPE], w[..., 128:128 + QK_ROPE]], axis=-1).reshape(Q_LORA, MLA_HEADS * (QK_NOPE + QK_ROPE))


def _pad_w_br_mla(w):
    w = w.reshape(MLA_HEADS, V_HEAD, D_MODEL)
    return jnp.concatenate([jnp.zeros_like(w), w], axis=1).reshape(MLA_HEADS * KV_PAD, D_MODEL)


def _unpad_w_br_mla(w):
    return w.reshape(MLA_HEADS, KV_PAD, D_MODEL)[:, V_HEAD:].reshape(MLA_HEADS * V_HEAD, D_MODEL)


def _riding(hooks, where, l, *args):
    make = hooks.get(where)
    ride = make(l, *args) if make else None
    return ride if ride else (None, lambda results: None)


def _layer_fwd(l, st, xprev, gp, bp, hb, w, tabs, hooks):
    ln_g, ln_b = w["ln_g"], w["ln_b"]
    lg = lambda k: ln_g[l, k][None]
    lb = lambda k: ln_b[l, k][None]
    s = {}
    s["x0"], s["gp0"], s["bp0"], s["hb0"] = xprev, gp, bp, hb
    side, got = _riding(hooks, "ffn1_fwd", l)
    s["g1"], s["u1"], s["a1"], *extras = ffn_up(hb, w["ffn1_w_up"][l], name="ffn_up", side=side)
    got(extras)
    s["xh1"], s["rs1"], s["hb1"] = down_ln(s["a1"], w["ffn1_w_down"][l], xprev, gp, bp, lg(0), lb(0), name="ffn_down_ln")
    s["p"] = mm_rows([(s["hb1"], w["mix_w_in"][l], False, 0)], D_IN_PAD, name="mix_in", tn=1024, out_dtype=BF)
    gq, gkv = w["q_norm_g"][l][None], w["kv_norm_g"][l][None]
    s["cqn"], s["ckvn"], s["q2"], s["kv"], s["kr"] = mla_prep(s["p"], gq, gkv, w["w_uq"][l], w["w_ukv"][l], tabs, name="mla_prep")
    side, got = _riding(hooks, "attn_fwd", l)
    s["o2"], s["lse"], *extras = attn_fwd(s["q2"], s["kv"], s["kr"], name="attn_fwd", side=side)
    got(extras)
    s["ycv"], s["conv"] = conv_fwd(s["p"], w["conv_w"][l], name="conv_fwd")
    s["bc"], s["bm"], s["mg"], s["xh2"], s["rs2"], s["hb2"] = merge_out_ln(
        s["ycv"], s["o2"], s["p"], w["mix_b_gate"][l], w["w_br_conv"][l], w["w_br_mla"][l], w["w_o"][l],
        s["xh1"], lg(0), lb(0), lg(1), lb(1), name="merge_out_ln")
    s["g2"], s["u2"], s["a2"] = ffn_up(s["hb2"], w["ffn2_w_up"][l], name="ffn_up")
    s["xh3"], s["rs3"], s["hb3"] = down_ln(s["a2"], w["ffn2_w_down"][l], s["xh2"], lg(1), lb(1), lg(2), lb(2), name="ffn_down_ln")
    st.append(s)
    return s["xh3"], lg(2), lb(2), s["hb3"]


def _ffn_bwd(which, l, g, hooks, dh, w_up, w_down, ln_gain, hb_in, gate, up, act, xh, rs):
    dzb, dgam, dbet, *loss_acc = ln_bwd(dh, xh, rs, ln_gain, branch_scale=0.5, name="ln_bwd")
    if loss_acc:
        g["loss"] = loss_acc[0]
    g[which + "_w_down"] = tn_mm(act, dzb, tm=D_FF // 2, name="dw_down", shard=("rows", D_FF // 4))
    side, got = _riding(hooks, which + "_down_bwd", l, g)
    dgate, dup, *extras = ffn_down_bwd(dzb, w_down, gate, up, name="ffn_down_bwd", side=side)
    got(extras)
    d_w = tn_mm(hb_in, dgate, tm=512, name="dw_up", shard=("cols", D_FF // 2), slot0=0)
    g[which + "_w_up"] = tn_mm(hb_in, dup, tm=512, name="dw_up", shard=("cols", D_FF // 2), slot0=2, dst=d_w)
    side, got = _riding(hooks, which + "_up_bwd", l, g)
    dh_in = mm_rows([(dgate, w_up, True, 0), (dup, w_up, True, 1)], D_MODEL, name="ffn_up_bwd", tn=512, addend=dzb, add_scale=2.0 * ALPHA,
                    side=side)
    if side is not None:
        dh_in, *extras = dh_in
        got(extras)
    return dh_in, dgam, dbet


def _layer_bwd(l, s, dh, w, tabs_bwd, hooks):
    ln_g = w["ln_g"]
    lg = lambda k: ln_g[l, k][None]
    g = {}
    dh, dg2, db2 = _ffn_bwd("ffn2", l, g, hooks, dh, w["ffn2_w_up"][l], w["ffn2_w_down"][l], lg(2), s["hb2"], s["g2"], s["u2"],
                            s["a2"], s["xh3"], s["rs3"])
    dzb, dg1, db1 = ln_bwd(dh, s["xh2"], s["rs2"], lg(1), branch_scale=1.0, name="ln_bwd")
    g["w_o"] = tn_mm(s["mg"], dzb, tm=1024, name="dw_o", shard=("rows", D_MODEL // 4))
    dbc, dbm, dp, dycv, do2, g["mix_b_gate"] = merge_bwd(
        dzb, w["w_o"][l], s["bc"], s["bm"], s["p"], w["mix_b_gate"][l], w["w_br_conv"][l], w["w_br_mla"][l], name="merge_bwd")
    g["w_br_conv"] = tn_mm(s["ycv"], dbc, tm=512, name="dw_br_conv", shard=("cols", D_MODEL // 4))
    g["w_br_mla"] = _chip_major(_unpad_w_br_mla(tn_mm(s["o2"], dbm, tm=1024, name="dw_br_mla")), D_MODEL // 4)
    side, got = _riding(hooks, "attn_bwd", l, g)
    dq2, dkv, dkr, *extras = attn_bwd(s["q2"], s["kv"], s["kr"], do2, s["lse"], s["o2"], name="attn_bwd", side=side)
    got(extras)
    gq, gkv = w["q_norm_g"][l][None], w["kv_norm_g"][l][None]
    dqb, dp, g["q_norm_g"], g["kv_norm_g"] = mla_prep_bwd(dq2, dkv, dkr, s["p"], gq, gkv, w["w_uq"][l], w["w_ukv"][l], tabs_bwd, dp,
                                                          name="mla_prep_bwd")
    g["w_uq"] = _chip_major(_unpad_w_uq(tn_mm(s["cqn"], dqb, tm=Q_LORA, name="dw_uq")), MLA_HEADS * (QK_NOPE + QK_ROPE) // 4)
    g["w_ukv"] = tn_mm(s["ckvn"], dkv, tm=KV_LORA, name="dw_ukv", shard=("cols", MLA_HEADS * KV_PAD // 4))
    dp, dw0, dw1, dw2 = conv_bwd(dycv, s["p"], s["conv"], w["conv_w"][l], dp, name="conv_bwd")
    g["conv_w"] = jnp.concatenate([dw0, dw1, dw2], axis=0)
    d_in = tn_mm(s["hb1"], dp, tm=512, name="dw_in")
    g["mix_w_in"] = _chip_major(jnp.concatenate([d_in[:, :1952], d_in[:, 2048:]], axis=1), D_IN // 4)
    dh = mm_rows([(dp, w["mix_w_in"][l], True, 0)], D_MODEL, name="mix_in_bwd", tn=512, addend=dzb, add_scale=ALPHA)
    dh, dg0, db0 = _ffn_bwd("ffn1", l, g, hooks, dh, w["ffn1_w_up"][l], w["ffn1_w_down"][l], lg(0), s["hb0"], s["g1"], s["u1"],
                            s["a1"], s["xh1"], s["rs1"])
    g["ln_g"] = jnp.concatenate([dg0, dg1, dg2], axis=0)
    g["ln_b"] = jnp.concatenate([db0, db1, db2], axis=0)
    return dh, g


BIG = ("ffn1_w_up", "ffn1_w_down", "mix_w_in", "w_uq", "w_ukv", "w_br_conv", "w_br_mla", "w_o", "ffn2_w_up", "ffn2_w_down")
BIG_AXIS = (2, 1, 2, 2, 2, 2, 2, 1, 2, 1)
FFN1_MATRICES = ("ffn1_w_up", "ffn1_w_down")
MIXER_MATRICES = ("mix_w_in", "w_uq", "w_ukv", "w_br_conv", "w_br_mla", "w_o")
FFN2_MATRICES = ("ffn2_w_up", "ffn2_w_down")
SMALL_SHARDED = ("meta_tokens", "mix_b_gate", "conv_w", "ln_g", "ln_b")
SMALL_REPLICATED = ("q_norm_g", "kv_norm_g")
WEIGHTS = ("meta_tokens", "ffn1_w_up", "ffn1_w_down", "mix_w_in", "mix_b_gate", "conv_w", "q_norm_g", "w_uq", "kv_norm_g", "w_ukv",
           "w_br_conv", "w_br_mla", "w_o", "ffn2_w_up", "ffn2_w_down", "ln_g", "ln_b")


def _view2d(a):
    return a.reshape(-1, a.shape[-1])


def _local_grads(x_row, target_row, w, hooks=None):
    hooks = hooks or {}
    seq = x_row.shape[0]
    t_real = N_META + seq
    tp = _pad_rows(t_real, TM)
    pad = tp - t_real
    h0 = jnp.concatenate([w["meta_tokens"], x_row, jnp.zeros((pad, D_MODEL), F32)], axis=0)
    target_p = jnp.concatenate([jnp.zeros((N_META, D_MODEL), F32), target_row, jnp.zeros((pad, D_MODEL), F32)], axis=0)
    tabs, tabs_bwd = _rope_tables(tp)
    ones = jnp.ones((1, D_MODEL), F32)
    zeros = jnp.zeros((1, D_MODEL), F32)
    saved = []
    cur = (h0, ones, zeros, h0.astype(BF))
    for l in range(DEPTH):
        cur = _layer_fwd(l, saved, *cur, w, tabs, hooks)
    dh = (cur[2], target_p, seq)
    grads = [None] * DEPTH
    for l in reversed(range(DEPTH)):
        dh, grads[l] = _layer_bwd(l, saved[l], dh, w, tabs_bwd, hooks)
        if "layer_bwd_done" in hooks:
            hooks["layer_bwd_done"](l, grads[l])
    return grads[DEPTH - 1].pop("loss"), dh[N_META:t_real], dh[:N_META], grads


def kernel(x, meta_tokens, ffn1_w_up, ffn1_w_down, mix_w_in, mix_b_gate, conv_w, q_norm_g, w_uq, kv_norm_g, w_ukv, w_br_conv, w_br_mla, w_o, ffn2_w_up, ffn2_w_down, ln_g, ln_b, loss_target, m_meta_tokens, m_ffn1_w_up, m_ffn1_w_down, m_mix_w_in, m_mix_b_gate, m_conv_w, m_q_norm_g, m_w_uq, m_kv_norm_g, m_w_ukv, m_w_br_conv, m_w_br_mla, m_w_o, m_ffn2_w_up, m_ffn2_w_down, m_ln_g, m_ln_b, v_meta_tokens, v_ffn1_w_up, v_ffn1_w_down, v_mix_w_in, v_mix_b_gate, v_conv_w, v_q_norm_g, v_w_uq, v_kv_norm_g, v_w_ukv, v_w_br_conv, v_w_br_mla, v_w_o, v_ffn2_w_up, v_ffn2_w_down, v_ln_g, v_ln_b):
    local = dict(meta_tokens=meta_tokens, ffn1_w_up=ffn1_w_up, ffn1_w_down=ffn1_w_down, mix_w_in=mix_w_in, mix_b_gate=mix_b_gate,
                 conv_w=conv_w, q_norm_g=q_norm_g, w_uq=w_uq, kv_norm_g=kv_norm_g, w_ukv=w_ukv, w_br_conv=w_br_conv,
                 w_br_mla=w_br_mla, w_o=w_o, ffn2_w_up=ffn2_w_up, ffn2_w_down=ffn2_w_down, ln_g=ln_g, ln_b=ln_b)
    mom_m = dict(zip(WEIGHTS, (m_meta_tokens, m_ffn1_w_up, m_ffn1_w_down, m_mix_w_in, m_mix_b_gate, m_conv_w, m_q_norm_g, m_w_uq,
                               m_kv_norm_g, m_w_ukv, m_w_br_conv, m_w_br_mla, m_w_o, m_ffn2_w_up, m_ffn2_w_down, m_ln_g, m_ln_b)))
    mom_v = dict(zip(WEIGHTS, (v_meta_tokens, v_ffn1_w_up, v_ffn1_w_down, v_mix_w_in, v_mix_b_gate, v_conv_w, v_q_norm_g, v_w_uq,
                               v_kv_norm_g, v_w_ukv, v_w_br_conv, v_w_br_mla, v_w_o, v_ffn2_w_up, v_ffn2_w_down, v_ln_g, v_ln_b)))
    xi, yi, ci = _place()
    chip = 2 * xi + yi

    shards = {n: local[n].astype(BF) for n in BIG}
    axis = dict(zip(BIG, BIG_AXIS))
    pad_layout = {"mix_w_in": _pad_w_in, "w_uq": _pad_w_uq, "w_br_mla": _pad_w_br_mla}
    w = {n: [None] * DEPTH for n in BIG}

    def fetch(keys):
        def install(gathered):
            for (n, l), g in zip(keys, gathered):
                if n in FFN1_MATRICES + FFN2_MATRICES:
                    w[n][l] = g
                    continue
                full = jnp.concatenate([g[j] for j in range(4)], axis=axis[n] - 1)
                w[n][l] = pad_layout[n](full) if n in pad_layout else full
        return gather_side([(shards[n], l) for n, l in keys]), install

    first, install_first = fetch([("ffn1_w_up", 0)])
    install_first(exchange_alone(first, name="gather_weights"))
    fetch_under = {("ffn1_fwd", 0): [("ffn1_w_down", 0)] + [(n, 0) for n in MIXER_MATRICES],
                   ("attn_fwd", 0): [(n, 0) for n in FFN2_MATRICES] + [(n, 1) for n in BIG]}
    hooks = {where: functools.partial(lambda l, where: fetch(fetch_under[where, l]) if (where, l) in fetch_under else None, where=where)
             for where in ("ffn1_fwd", "attn_fwd")}
    stacked = exchange_small([_view2d(local[n]) for n in SMALL_SHARDED], reduce=False, name="gather_small")
    for n, st in zip(SMALL_SHARDED, stacked):
        full = jnp.concatenate([st[2 * j] for j in range(4)], axis=-1)
        w[n] = full.reshape(local[n].shape[:-1] + (full.shape[-1],))
    for n in SMALL_REPLICATED:
        w[n] = local[n]

    c_idx = jnp.reshape(ci, (1,)).astype(jnp.int32)
    done, from_chips = {}, {}

    def send(keys, grad_of):
        glist = [grad_of[k] for k in keys]
        from_sibling = pair_exchange(glist, name="rs_pair_exchange")
        sums = [pair_add(a, s, c_idx, name="rs_pair_add") for a, s in zip(glist, from_sibling)]
        return scatter_side(sums), lambda results: from_chips.update(zip(keys, results))

    send_under = {"attn_bwd": FFN2_MATRICES + ("w_o", "w_br_conv", "w_br_mla"),
                  "ffn1_down_bwd": ("mix_w_in", "w_uq", "w_ukv", "ffn1_w_down"), "ffn1_up_bwd": ("ffn1_w_up",)}
    hooks["layer_bwd_done"] = lambda l, g: done.update({(n, l): g[n] for n in BIG})

    def send_hook(where):
        def hook(l, g):
            if l != 0:
                return None
            keys = [(n, 0) for n in send_under[where]] + ([(n, 1) for n in BIG] if where == "attn_bwd" else [])
            return send(keys, {**done, **{(n, 0): g[n] for n in send_under[where]}})
        return hook

    for where in send_under:
        hooks[where] = send_hook(where)

    loss_acc, grad_x, d_meta, grads = _local_grads(x[0], loss_target[0], w, hooks)
    grad_x = grad_x[None]
    keys = [(n, l) for n in BIG for l in range(DEPTH)]
    reduced = pair_gather([sum_chunks(from_chips[k], name="rs_sum") for k in keys], name="rs_pair_gather")
    reduced = {k: r.reshape(local[k[0]].shape[1:]) for k, r in zip(keys, reduced)}
    gshard = {n: jnp.stack([reduced[n, l] for l in range(DEPTH)]) for n in BIG}

    small_names = SMALL_SHARDED + SMALL_REPLICATED
    gsmall = {n: jnp.concatenate([grads[l][n] for l in range(DEPTH)], axis=0) for n in small_names if n != "meta_tokens"}
    gsmall["meta_tokens"] = d_meta
    small_red = exchange_small([gsmall[n] for n in small_names] + [loss_acc], reduce=True, name="reduce_small")
    loss = small_red[-1][0, 0]
    for n, full in zip(small_names, small_red[:-1]):
        if n in SMALL_SHARDED:
            sh = local[n].shape[-1]
            full = lax.dynamic_slice_in_dim(full, chip * sh, sh, axis=1)
        gshard[n] = full.reshape(local[n].shape)

    delta, new_m, new_v = {}, {}, {}
    for n in WEIGHTS:
        shape = local[n].shape
        d, nm, nv = adamw(_view2d(local[n]), _view2d(gshard[n]), _view2d(mom_m[n]), _view2d(mom_v[n]), name="adamw")
        delta[n], new_m[n], new_v[n] = d.reshape(shape), nm.reshape(shape), nv.reshape(shape)
    return (loss, grad_x, *[gshard[n] for n in WEIGHTS], *[delta[n] for n in WEIGHTS], *[new_m[n] for n in WEIGHTS],
            *[new_v[n] for n in WEIGHTS])
```

```python
import functools

import jax
import jax.numpy as jnp
from jax import lax
from jax.experimental import pallas as pl
from jax.experimental.pallas import tpu as pltpu

F32 = jnp.float32
BF = jnp.bfloat16
MESH = pl.DeviceIdType.MESH

D_MODEL = 1024
DEPTH = 2
N_META = 16
D_CONV = 512
MLA_HEADS = 8
QK_NOPE = 64
QK_ROPE = 32
V_HEAD = 64
Q_LORA = 256
KV_LORA = 128
ROPE_BASE = 10000.0
NEG_INF = -1e30
D_FF = 2816
ALPHA = (2 * DEPTH) ** 0.25
LN_EPS = 1e-5
RMS_EPS = 1e-6
ATT_SCALE = (QK_NOPE + QK_ROPE) ** -0.5
LOG2E = 1.4426950408889634
LN2 = 0.6931471805599453
D_IN = 4000
D_IN_PAD = 4096
Q_PAD = 256
KV_PAD = 128

ADAM_LR = 0.001
ADAM_B1 = 0.9
ADAM_B2 = 0.999
ADAM_EPS = 1e-08
ADAM_WD = 0.01
ADAM_STEP = 10

TM = 768
TMH = 384
LANES = 128
VMEM_LIMIT_BYTES = 50 * 1024 * 1024

NT = (((1,), (1,)), ((), ()))
TN = (((0,), (0,)), ((), ()))


def _pcall(body, **kw):
    return pl.pallas_call(body, **kw)


def _params(n_axes):
    return pltpu.CompilerParams(dimension_semantics=("arbitrary",) * n_axes, vmem_limit_bytes=VMEM_LIMIT_BYTES)


def _sds(shape, dtype):
    return jax.ShapeDtypeStruct(shape, dtype)


def mm_rows(pairs, n_out, *, name, tn=None, addend=None, add_scale=1.0, out_dtype=F32, side=None):
    tp = pairs[0][0].shape[0]
    tn = tn or n_out
    in_specs, args = [], []
    for a, b, nt, kb in pairs:
        k = a.shape[1]
        in_specs.append(pl.BlockSpec((TM, k), lambda i, j: (i, 0)))
        if nt and b.ndim == 3:
            in_specs.append(pl.BlockSpec((2, tn, k // 2), functools.partial(lambda i, j, kb: (kb, j, 0), kb=kb)))
        elif nt:
            in_specs.append(pl.BlockSpec((tn, k), functools.partial(lambda i, j, kb: (j, kb), kb=kb)))
        else:
            in_specs.append(pl.BlockSpec((k, tn), lambda i, j: (0, j)))
        args += [a, b]
    if addend is not None:
        in_specs.append(pl.BlockSpec((TM, tn), lambda i, j: (i, j)))
        args.append(addend)
    n_pairs = len(pairs)
    nts = [p[2] for p in pairs]

    def body(refs, out_refs, scratch):
        o_ref = out_refs[0]
        acc = None
        for p in range(n_pairs):
            a = refs[2 * p][...].astype(BF)
            b = refs[2 * p + 1][...]
            if b.ndim == 3:
                b = jnp.concatenate([b[0], b[1]], axis=1)
            d = lax.dot_general(a, b, NT if nts[p] else (((1,), (0,)), ((), ())), preferred_element_type=F32)
            acc = d if acc is None else acc + d
        if addend is not None:
            acc = acc + add_scale * refs[2 * n_pairs][...].astype(F32)
        o_ref[...] = acc.astype(o_ref.dtype)

    out = _side_call(
        body, side, name=name, grid=(tp // TM, n_out // tn), in_specs=in_specs,
        out_specs=[pl.BlockSpec((TM, tn), lambda i, j: (i, j))], out_shape=[_sds((tp, n_out), out_dtype)],
        scratch_shapes=[], args=args,
    )
    return out if side is not None else out[0]


def tn_mm(a, b, *, tm, name, out_dtype=BF, shard=None, slot0=0, dst=None):
    tp, m = a.shape
    n = b.shape[1]
    nk = tp // TM
    if shard is None:
        pieces, out_block, out_index, out_full = 1, (tm, n), (lambda i, k: (i, 0)), (m, n)
    elif shard[0] == "cols":
        pieces = n // shard[1]
        out_block, out_full = (pieces, tm, shard[1]), (4, m, shard[1])
        out_index = lambda i, k: (slot0 // pieces, i, 0)
    else:
        pieces = tm // shard[1]
        out_block, out_full = (pieces, shard[1], n), (4, m // 4, n)
        out_index = lambda i, k: (i, 0, 0)

    def body(a_ref, b_ref, *rest):
        o_ref, acc_ref = rest[-2], rest[-1]
        k = pl.program_id(1)

        @pl.when(k == 0)
        def _():
            acc_ref[...] = jnp.zeros_like(acc_ref)

        acc_ref[...] += lax.dot_general(a_ref[...].astype(BF), b_ref[...].astype(BF), TN, preferred_element_type=F32)

        @pl.when(k == nk - 1)
        def _():
            if shard is None:
                o_ref[...] = acc_ref[...].astype(o_ref.dtype)
            elif shard[0] == "cols":
                for j in range(pieces):
                    o_ref[j] = acc_ref[:, j * shard[1]:(j + 1) * shard[1]].astype(o_ref.dtype)
            else:
                for j in range(pieces):
                    o_ref[j] = acc_ref[j * shard[1]:(j + 1) * shard[1], :].astype(o_ref.dtype)

    in_specs = [pl.BlockSpec((TM, tm), lambda i, k: (k, i)), pl.BlockSpec((TM, n), lambda i, k: (k, 0))]
    args = [a, b]
    aliases = {}
    if dst is not None:
        in_specs.append(pl.BlockSpec(memory_space=pl.ANY))
        args.append(dst)
        aliases = {2: 0}
    return _pcall(
        body, name=name, grid=(m // tm, nk), in_specs=in_specs, out_specs=pl.BlockSpec(out_block, out_index),
        out_shape=_sds(out_full, out_dtype), input_output_aliases=aliases,
        scratch_shapes=[pltpu.VMEM((tm, n), F32)], compiler_params=_params(2),
    )(*args)


def _ln_store(z, g_ref, b_ref, xh_ref, rs_ref, hb_ref):
    mu = jnp.mean(z, axis=-1, keepdims=True)
    zc = z - mu
    var = jnp.mean(zc * zc, axis=-1, keepdims=True)
    rstd = lax.rsqrt(var + LN_EPS)
    xh = zc * rstd
    xh_ref[...] = xh
    rs_ref[...] = rstd
    hb_ref[...] = (xh * g_ref[...] + b_ref[...]).astype(BF)


def _ln_out(tp, tm=TM):
    specs = [pl.BlockSpec((tm, D_MODEL), lambda i: (i, 0)), pl.BlockSpec((tm, 1), lambda i: (i, 0)),
             pl.BlockSpec((tm, D_MODEL), lambda i: (i, 0))]
    shapes = [_sds((tp, D_MODEL), F32), _sds((tp, 1), F32), _sds((tp, D_MODEL), BF)]
    return specs, shapes


def _row_vec(n):
    return pl.BlockSpec((1, n), lambda i: (0, 0))


def ffn_up(hb, wup, *, name, side=None):
    tp = hb.shape[0]
    tn = D_FF // 2
    nj = D_FF // tn

    def body(in_refs, out_refs, scratch):
        h_ref, wg_ref, wu_ref = in_refs
        g_ref, u_ref, a_ref = out_refs
        h = h_ref[...]
        g = jnp.dot(h, wg_ref[0], preferred_element_type=F32)
        u = jnp.dot(h, wu_ref[0], preferred_element_type=F32)
        g_ref[...] = g.astype(BF)
        u_ref[...] = u.astype(BF)
        a_ref[...] = (g * jax.nn.sigmoid(g) * u).astype(BF)

    blk = pl.BlockSpec((TM, tn), lambda i, j: (i, j))
    return _side_call(
        body, side, name=name, grid=(tp // TM, nj),
        in_specs=[pl.BlockSpec((TM, D_MODEL), lambda i, j: (i, 0)), pl.BlockSpec((1, D_MODEL, tn), lambda i, j: (j, 0, 0)),
                  pl.BlockSpec((1, D_MODEL, tn), lambda i, j: (j + nj, 0, 0))],
        out_specs=[blk, blk, blk], out_shape=[_sds((tp, D_FF), BF)] * 3, scratch_shapes=[], args=(hb, wup, wup),
    )


def down_ln(a, wd, xprev, gp, bp, g, b, *, name):
    tp = a.shape[0]

    def body(a_ref, wd_ref, xp_ref, gp_ref, bp_ref, g_ref, b_ref, xh_ref, rs_ref, hb_ref):
        wd = jnp.concatenate([wd_ref[j] for j in range(4)], axis=0)
        f = jnp.dot(a_ref[...], wd, preferred_element_type=F32)
        hprev = xp_ref[...] * gp_ref[...] + bp_ref[...]
        _ln_store(ALPHA * hprev + 0.5 * f, g_ref, b_ref, xh_ref, rs_ref, hb_ref)

    out_specs, out_shape = _ln_out(tp)
    return _pcall(
        body, name=name, grid=(tp // TM,),
        in_specs=[pl.BlockSpec((TM, D_FF), lambda i: (i, 0)), pl.BlockSpec((4, D_FF // 4, D_MODEL), lambda i: (0, 0, 0)),
                  pl.BlockSpec((TM, D_MODEL), lambda i: (i, 0))] + [_row_vec(D_MODEL)] * 4,
        out_specs=out_specs, out_shape=out_shape, compiler_params=_params(1),
    )(a, wd, xprev, gp, bp, g, b)


def _rope(x, c, s1, s2, reps):
    n = x.shape[1]
    if reps > 1:
        c, s1, s2 = (jnp.tile(t, (1, reps)) for t in (c, s1, s2))
    return x * c + pltpu.roll(x, 16, 1) * s1 + pltpu.roll(x, n - 16, 1) * s2


def _rms(x, g):
    r = lax.rsqrt(jnp.mean(x * x, axis=-1, keepdims=True) + RMS_EPS)
    return x * r * g, r


def mla_prep(p, gq, gkv, wuq_p, wukv, tabs, *, name):
    tp = p.shape[0]
    nh = MLA_HEADS

    def body(cq_ref, ckv_ref, kr_ref, gq_ref, gkv_ref, wuq_ref, wukv_ref, cq_t, s1q_t, s2q_t, ck_t, s1k_t, s2k_t,
             cqn_ref, ckvn_ref, q2_ref, kv_ref, krr_ref):
        cqn, _ = _rms(cq_ref[...].astype(F32), gq_ref[...])
        ckvn, _ = _rms(ckv_ref[...].astype(F32), gkv_ref[...])
        cqn = cqn.astype(BF)
        ckvn = ckvn.astype(BF)
        cqn_ref[...] = cqn
        ckvn_ref[...] = ckvn
        q = jnp.dot(cqn, wuq_ref[...], preferred_element_type=F32)
        q2_ref[...] = _rope(q, cq_t[...], s1q_t[...], s2q_t[...], nh).astype(BF)
        kv_ref[...] = jnp.dot(ckvn, wukv_ref[...], preferred_element_type=F32).astype(BF)
        krr_ref[...] = _rope(kr_ref[...].astype(F32), ck_t[...], s1k_t[...], s2k_t[...], 1).astype(BF)

    def rows(n, col=0):
        return pl.BlockSpec((TMH, n), functools.partial(lambda i, col: (i, col), col=col))

    return _pcall(
        body, name=name, grid=(tp // TMH,),
        in_specs=[rows(Q_LORA, 1536 // Q_LORA), rows(KV_LORA, 1792 // KV_LORA), rows(LANES, 1920 // LANES),
                  _row_vec(Q_LORA), _row_vec(KV_LORA),
                  pl.BlockSpec((Q_LORA, nh * Q_PAD), lambda i: (0, 0)), pl.BlockSpec((KV_LORA, nh * KV_PAD), lambda i: (0, 0)),
                  rows(Q_PAD), rows(Q_PAD), rows(Q_PAD), rows(LANES), rows(LANES), rows(LANES)],
        out_specs=[rows(Q_LORA), rows(KV_LORA), rows(nh * Q_PAD), rows(nh * KV_PAD), rows(LANES)],
        out_shape=[_sds((tp, Q_LORA), BF), _sds((tp, KV_LORA), BF), _sds((tp, nh * Q_PAD), BF),
                   _sds((tp, nh * KV_PAD), BF), _sds((tp, LANES), BF)],
        compiler_params=_params(1),
    )(p, p, p, gq, gkv, wuq_p, wukv, *tabs)


def _causal_mask(s):
    qpos = lax.broadcasted_iota(jnp.int32, (TM, TM), 0)
    kpos = lax.broadcasted_iota(jnp.int32, (TM, TM), 1)
    return jnp.where(kpos <= qpos, s, NEG_INF)


def _key_rows(k):
    return pl.ds(pl.multiple_of(k * TM, TM), TM)


def _pipelined_key_blocks(n, prefetch, process):
    prefetch(0, 0)

    def pair(j, carry):
        prefetch(2 * j + 1, 1)
        process(2 * j, 0, False)
        prefetch(2 * j + 2, 0)
        process(2 * j + 1, 1, False)
        return carry

    lax.fori_loop(0, n // 2, pair, 0)

    @pl.when(n % 2 == 1)
    def _():
        prefetch(n, 1)
        process(n - 1, 0, False)
        process(n, 1, True)

    @pl.when(n % 2 == 0)
    def _():
        process(n, 0, True)


def _side_call(body_main, side, *, name, grid, in_specs, out_specs, out_shape, scratch_shapes, args):
    n_in, n_out, n_scr = len(in_specs), len(out_specs), len(scratch_shapes)
    s_in, s_pre, n_sems, program = side if side is not None else ((), (), 0, None)
    a, b = len(s_in), len(s_pre)

    def body(*refs):
        in_refs = refs[:n_in]
        out_refs = refs[n_in + a + b:n_in + a + b + n_out]
        scr = refs[n_in + a + 2 * b + n_out:n_in + a + 2 * b + n_out + n_scr]
        if side is not None:
            side_in = refs[n_in:n_in + a]
            side_out = refs[n_in + a + b + n_out:n_in + a + 2 * b + n_out]
            start, finish = program(side_in, side_out, refs[-2], refs[-1])

            @pl.when((pl.program_id(0) == 0) & (pl.program_id(1) == 0))
            def _():
                start()

        body_main(in_refs, out_refs, scr)
        if side is not None:
            @pl.when((pl.program_id(0) == grid[0] - 1) & (pl.program_id(1) == grid[1] - 1))
            def _():
                finish()

    sems = [pltpu.SemaphoreType.DMA((n_sems,))] * 2 if side is not None else []
    return _pcall(
        body, name=name, grid=grid, in_specs=list(in_specs) + [HBM_SPEC] * (a + b), out_specs=list(out_specs) + [HBM_SPEC] * b,
        out_shape=list(out_shape) + [_sds(p.shape, p.dtype) for p in s_pre],
        input_output_aliases={n_in + a + i: n_out + i for i in range(b)},
        scratch_shapes=list(scratch_shapes) + sems, compiler_params=_params(2),
    )(*args, *s_in, *s_pre)


def attn_fwd(q2, kv, kr, *, name, side=None):
    tp = q2.shape[0]
    nh = MLA_HEADS
    nb = tp // TM
    rep = TM // LANES

    def body(in_refs, out_refs, scratch):
        q_ref, kv_ref, kr_ref = in_refs
        o_ref, lse_ref = out_refs
        m_ref, l_ref, acc_ref, s0_ref, s1_ref, p_ref, alpha_ref = scratch
        qi = pl.program_id(1)
        s_refs = (s0_ref, s1_ref)
        m_ref[...] = jnp.full_like(m_ref, NEG_INF)
        l_ref[...] = jnp.zeros_like(l_ref)
        acc_ref[...] = jnp.zeros_like(acc_ref)

        def prefetch(k, slot):
            k2 = jnp.concatenate([kv_ref[_key_rows(k), :], kr_ref[_key_rows(k), :]], axis=1)
            s_refs[slot][...] = lax.dot_general(q_ref[...], k2, NT, preferred_element_type=F32)

        def process(k, slot, diagonal):
            for r in range(TM // LANES):
                rows = slice(r * LANES, (r + 1) * LANES)
                s = s_refs[slot][rows, :]
                if diagonal:
                    qpos = r * LANES + lax.broadcasted_iota(jnp.int32, (LANES, TM), 0)
                    s = jnp.where(lax.broadcasted_iota(jnp.int32, (LANES, TM), 1) <= qpos, s, NEG_INF)
                m_prev = m_ref[rows, :]
                m_new = jnp.maximum(m_prev, jnp.max(s, axis=1, keepdims=True))
                alpha = jnp.exp2(m_prev - m_new)
                p = jnp.exp2(s - jnp.tile(m_new, (1, rep)))
                lane_sums = p[:, 0:LANES]
                for t in range(1, rep):
                    lane_sums = lane_sums + p[:, t * LANES:(t + 1) * LANES]
                l_ref[rows, :] = alpha * l_ref[rows, :] + lane_sums
                p_ref[rows, :] = p.astype(BF)
                alpha_ref[rows, :] = alpha
                m_ref[rows, :] = m_new
            acc_ref[...] = alpha_ref[...] * acc_ref[...] + jnp.dot(p_ref[...], kv_ref[_key_rows(k), :], preferred_element_type=F32)

        _pipelined_key_blocks(qi, prefetch, process)
        l = jnp.sum(l_ref[...], axis=1, keepdims=True)
        o_ref[...] = (acc_ref[...] / l).astype(BF)
        lse_ref[...] = m_ref[...] + jnp.log2(l)

    return _side_call(
        body, side, name=name, grid=(nh, nb),
        in_specs=[pl.BlockSpec((TM, Q_PAD), lambda h, qi: (qi, h)), pl.BlockSpec((tp, KV_PAD), lambda h, qi: (0, h)),
                  pl.BlockSpec((tp, LANES), lambda h, qi: (0, 0))],
        out_specs=[pl.BlockSpec((TM, KV_PAD), lambda h, qi: (qi, h)), pl.BlockSpec((TM, LANES), lambda h, qi: (qi, h))],
        out_shape=[_sds((tp, nh * KV_PAD), BF), _sds((tp, nh * LANES), F32)],
        scratch_shapes=[pltpu.VMEM((TM, LANES), F32)] * 3 + [pltpu.VMEM((TM, TM), F32)] * 2
        + [pltpu.VMEM((TM, TM), BF), pltpu.VMEM((TM, LANES), F32)], args=(q2, kv, kr),
    )


def conv_fwd(p, w, *, name):
    tp = p.shape[0]

    def body(b_ref, c_ref, h_ref, w_ref, y_ref, cv_ref, ebuf):
        i = pl.program_id(0)

        @pl.when(i == 0)
        def _():
            ebuf[0:8, :] = jnp.zeros((8, D_CONV), F32)

        e = c_ref[...].astype(F32) * h_ref[...].astype(F32)
        ebuf[8:8 + TM, :] = e
        w_all = w_ref[...]
        conv = w_all[0:1] * ebuf[pl.ds(6, TM), :] + w_all[1:2] * ebuf[pl.ds(7, TM), :] + w_all[2:3] * e
        cv_ref[...] = conv.astype(BF)
        y_ref[...] = (b_ref[...].astype(F32) * conv).astype(BF)
        ebuf[0:8, :] = ebuf[TM:TM + 8, :]

    def col(j):
        return pl.BlockSpec((TM, D_CONV), functools.partial(lambda i, j: (i, j), j=j))

    return _pcall(
        body, name=name, grid=(tp // TM,),
        in_specs=[col(0), col(1), col(2), pl.BlockSpec((3, D_CONV), lambda i: (0, 0))],
        out_specs=[col(0), col(0)], out_shape=[_sds((tp, D_CONV), BF)] * 2,
        scratch_shapes=[pltpu.VMEM((TM + 8, D_CONV), F32)], compiler_params=_params(1),
    )(p, p, p, w)


def merge_out_ln(ycv, o2, p, bg, wbc, wbm_p, wo, xprev, gp, bp, g, b, *, name):
    tp = ycv.shape[0]

    def body(y_ref, o_ref, gc_ref, gm_ref, bg_ref, wbc_ref, wbm_ref, wo_ref, xp_ref, gp_ref, bp_ref, g_ref, b_ref,
             bc_ref, bm_ref, mg_ref, xh_ref, rs_ref, hb_ref):
        bc = jnp.dot(y_ref[...], wbc_ref[...], preferred_element_type=F32)
        bm = jnp.dot(o_ref[...], wbm_ref[...], preferred_element_type=F32)
        bgv = bg_ref[...]
        mg = (jax.nn.sigmoid(gc_ref[...].astype(F32) + bgv[0:1]) * bc
              + jax.nn.sigmoid(gm_ref[...].astype(F32) + bgv[1:2]) * bm)
        mgb = mg.astype(BF)
        bc_ref[...] = bc.astype(BF)
        bm_ref[...] = bm.astype(BF)
        mg_ref[...] = mgb
        mix = jnp.dot(mgb, wo_ref[...], preferred_element_type=F32)
        hprev = xp_ref[...] * gp_ref[...] + bp_ref[...]
        _ln_store(ALPHA * hprev + mix, g_ref, b_ref, xh_ref, rs_ref, hb_ref)

    def rows(n, col=0):
        return pl.BlockSpec((TMH, n), functools.partial(lambda i, col: (i, col), col=col))

    def whole(r, c):
        return pl.BlockSpec((r, c), lambda i: (0, 0))

    ln_specs, ln_shapes = _ln_out(tp, TMH)
    return _pcall(
        body, name=name, grid=(tp // TMH,),
        in_specs=[rows(D_CONV), rows(MLA_HEADS * KV_PAD), rows(D_MODEL, 2), rows(D_MODEL, 3), whole(2, D_MODEL),
                  whole(D_CONV, D_MODEL), whole(MLA_HEADS * KV_PAD, D_MODEL), whole(D_MODEL, D_MODEL), rows(D_MODEL)]
        + [_row_vec(D_MODEL)] * 4,
        out_specs=[rows(D_MODEL)] * 3 + ln_specs, out_shape=[_sds((tp, D_MODEL), BF)] * 3 + ln_shapes,
        compiler_params=_params(1),
    )(ycv, o2, p, p, bg, wbc, wbm_p, wo, xprev, gp, bp, g, b)


def ln_bwd(dh, xh, rstd, g, *, branch_scale, name):
    tp = xh.shape[0]
    from_loss = isinstance(dh, tuple)

    def body(*refs):
        if from_loss:
            xh_ref, rs_ref, g_ref, b_ref, t_ref, dzb_ref, dg_ref, db_ref, loss_ref = refs
        else:
            dh_ref, xh_ref, rs_ref, g_ref, dzb_ref, dg_ref, db_ref = refs
        i = pl.program_id(0)

        @pl.when(i == 0)
        def _():
            dg_ref[...] = jnp.zeros_like(dg_ref)
            db_ref[...] = jnp.zeros_like(db_ref)
            if from_loss:
                loss_ref[...] = jnp.zeros_like(loss_ref)

        xhat = xh_ref[...]
        if from_loss:
            row = i * TM + lax.broadcasted_iota(jnp.int32, (TM, 1), 0)
            real = (row >= N_META) & (row < N_META + dh[2])
            diff = jnp.where(real, xhat * g_ref[...] + b_ref[...] - t_ref[...], 0.0)
            loss_ref[...] += 0.5 / D_MODEL * jnp.sum(diff * diff)
            dy = diff * (1.0 / D_MODEL)
        else:
            dy = dh_ref[...]
        dg_ref[...] += jnp.sum(dy * xhat, axis=0, keepdims=True)
        db_ref[...] += jnp.sum(dy, axis=0, keepdims=True)
        dxh = dy * g_ref[...]
        m1 = jnp.mean(dxh, axis=-1, keepdims=True)
        m2 = jnp.mean(dxh * xhat, axis=-1, keepdims=True)
        dz = rs_ref[...] * (dxh - m1 - xhat * m2)
        dzb_ref[...] = (branch_scale * dz).astype(BF)

    rows = pl.BlockSpec((TM, D_MODEL), lambda i: (i, 0))
    stat = pl.BlockSpec((TM, 1), lambda i: (i, 0))
    vec = _row_vec(D_MODEL)
    out_specs = [rows, vec, vec]
    out_shape = [_sds((tp, D_MODEL), BF), _sds((1, D_MODEL), F32), _sds((1, D_MODEL), F32)]
    if from_loss:
        in_specs, args = [rows, stat, vec, vec, rows], (xh, rstd, g, dh[0], dh[1])
        out_specs.append(pl.BlockSpec((8, LANES), lambda i: (0, 0)))
        out_shape.append(_sds((8, LANES), F32))
    else:
        in_specs, args = [rows, rows, stat, vec], (dh, xh, rstd, g)
    return _pcall(body, name=name, grid=(tp // TM,), in_specs=in_specs, out_specs=out_specs, out_shape=out_shape,
                  compiler_params=_params(1))(*args)


def ffn_down_bwd(dzb, wd, gate, up, *, name, side=None):
    tp = dzb.shape[0]
    tn = D_FF // 2

    def body(in_refs, out_refs, scratch):
        dz_ref, wd_ref, g_ref, u_ref = in_refs
        dg_ref, du_ref = out_refs
        wd = jnp.concatenate([wd_ref[0], wd_ref[1]], axis=0)
        da = lax.dot_general(dz_ref[...], wd, NT, preferred_element_type=F32)
        g = g_ref[...].astype(F32)
        u = u_ref[...].astype(F32)
        sg = jax.nn.sigmoid(g)
        dg_ref[...] = (da * u * sg * (1.0 + g * (1.0 - sg))).astype(BF)
        du_ref[...] = (da * g * sg).astype(BF)

    blk = pl.BlockSpec((TM, tn), lambda i, j: (i, j))
    return _side_call(
        body, side, name=name, grid=(tp // TM, D_FF // tn),
        in_specs=[pl.BlockSpec((TM, D_MODEL), lambda i, j: (i, 0)), pl.BlockSpec((2, tn // 2, D_MODEL), lambda i, j: (j, 0, 0)), blk, blk],
        out_specs=[blk, blk], out_shape=[_sds((tp, D_FF), BF)] * 2, scratch_shapes=[], args=(dzb, wd, gate, up),
    )


def merge_bwd(dzb, wo, bc, bm, p, bg, wbc, wbm_p, o2, *, name):
    tp = dzb.shape[0]
    nh = MLA_HEADS

    def body(dz_ref, wo_ref, bc_ref, bm_ref, gc_ref, gm_ref, bg_ref, wbc_ref, wbm_ref, o_ref,
             dbc_ref, dbm_ref, dgg_ref, dy_ref, do_ref, dl_ref, dbg_ref):
        i = pl.program_id(0)

        @pl.when(i == 0)
        def _():
            dbg_ref[...] = jnp.zeros_like(dbg_ref)

        dmg = lax.dot_general(dz_ref[...], wo_ref[...], NT, preferred_element_type=F32)
        bgv = bg_ref[...]
        sc = jax.nn.sigmoid(gc_ref[...].astype(F32) + bgv[0:1])
        sm = jax.nn.sigmoid(gm_ref[...].astype(F32) + bgv[1:2])
        dbc = (dmg * sc).astype(BF)
        dbm = (dmg * sm).astype(BF)
        dgc = dmg * bc_ref[...].astype(F32) * sc * (1.0 - sc)
        dgm = dmg * bm_ref[...].astype(F32) * sm * (1.0 - sm)
        dbc_ref[...] = dbc
        dbm_ref[...] = dbm
        dgg_ref[...] = jnp.concatenate([dgc, dgm], axis=1).astype(BF)
        dbg_ref[...] += jnp.concatenate([jnp.sum(dgc, axis=0, keepdims=True), jnp.sum(dgm, axis=0, keepdims=True)], axis=0)
        dy_ref[...] = lax.dot_general(dbc, wbc_ref[...], NT, preferred_element_type=F32)
        do = lax.dot_general(dbm, wbm_ref[...], NT, preferred_element_type=F32)
        do_ref[...] = do.astype(BF)
        prod = do * o_ref[...].astype(F32)
        parts = []
        for h in range(nh):
            d = jnp.sum(prod[:, h * KV_PAD:(h + 1) * KV_PAD], axis=1, keepdims=True)
            parts.append(jnp.broadcast_to(d, (TMH, LANES)))
        dl_ref[...] = jnp.concatenate(parts, axis=1)

    def rows(n, col=0):
        return pl.BlockSpec((TMH, n), functools.partial(lambda i, col: (i, col), col=col))

    def whole(r, c):
        return pl.BlockSpec((r, c), lambda i: (0, 0))

    return _pcall(
        body, name=name, grid=(tp // TMH,),
        in_specs=[rows(D_MODEL), whole(D_MODEL, D_MODEL), rows(D_MODEL), rows(D_MODEL), rows(D_MODEL, 2), rows(D_MODEL, 3),
                  whole(2, D_MODEL), whole(D_CONV, D_MODEL), whole(nh * KV_PAD, D_MODEL), rows(nh * KV_PAD)],
        out_specs=[rows(D_MODEL), rows(D_MODEL), rows(2 * D_MODEL, 1), rows(D_CONV), rows(nh * KV_PAD), rows(nh * LANES),
                   whole(2, D_MODEL)],
        out_shape=[_sds((tp, D_MODEL), BF), _sds((tp, D_MODEL), BF), _sds((tp, D_IN_PAD), BF), _sds((tp, D_CONV), F32),
                   _sds((tp, nh * KV_PAD), BF), _sds((tp, nh * LANES), F32), _sds((2, D_MODEL), F32)],
        compiler_params=_params(1),
    )(dzb, wo, bc, bm, p, p, bg, wbc, wbm_p, o2)


def attn_bwd(q2, kv, kr, do2, lse, dl, *, name, side=None):
    tp = q2.shape[0]
    nh = MLA_HEADS
    nb = tp // TM
    rep = TM // LANES

    def body(in_refs, out_refs, scratch):
        q_ref, kv_ref, kr_ref, do_ref, lse_ref, dl_ref = in_refs
        dq_ref, dkv_ref, dkr_ref = out_refs
        dq_acc, s0_ref, s1_ref, dp0_ref, dp1_ref = scratch
        qi = pl.program_id(1)
        s_refs, dp_refs = (s0_ref, s1_ref), (dp0_ref, dp1_ref)

        @pl.when(qi == 0)
        def _():
            dkv_ref[...] = jnp.zeros_like(dkv_ref)

        @pl.when((qi == 0) & (pl.program_id(0) == 0))
        def _():
            dkr_ref[...] = jnp.zeros_like(dkr_ref)

        dq_acc[...] = jnp.zeros_like(dq_acc)

        def prefetch(k, slot):
            kvb = kv_ref[_key_rows(k), :]
            k2 = jnp.concatenate([kvb, kr_ref[_key_rows(k), :]], axis=1)
            s_refs[slot][...] = lax.dot_general(q_ref[...], k2, NT, preferred_element_type=F32)
            dp_refs[slot][...] = lax.dot_general(do_ref[...], kvb, NT, preferred_element_type=F32)

        def process(k, slot, diagonal):
            rows = _key_rows(k)
            s = s_refs[slot][...]
            if diagonal:
                s = _causal_mask(s)
            p = jnp.exp2(s - jnp.tile(lse_ref[...], (1, rep)))
            dsb = (p * (dp_refs[slot][...] - jnp.tile(dl_ref[...], (1, rep)))).astype(BF)
            dk2 = lax.dot_general(dsb, q_ref[...], TN, preferred_element_type=F32) * LN2
            dkv_ref[rows, :] += lax.dot_general(p.astype(BF), do_ref[...], TN, preferred_element_type=F32) + dk2[:, :KV_PAD]
            dkr_ref[rows, :] += dk2[:, KV_PAD:KV_PAD + LANES]
            k2 = jnp.concatenate([kv_ref[rows, :], kr_ref[rows, :]], axis=1)
            dq_acc[...] += jnp.dot(dsb, k2, preferred_element_type=F32)

        _pipelined_key_blocks(qi, prefetch, process)
        dq_ref[...] = dq_acc[...].astype(BF)

    def qrow(n):
        return pl.BlockSpec((TM, n), lambda h, qi: (qi, h))

    def head(n):
        return pl.BlockSpec((tp, n), lambda h, qi: (0, h))

    return _side_call(
        body, side, name=name, grid=(nh, nb),
        in_specs=[qrow(Q_PAD), head(KV_PAD), pl.BlockSpec((tp, LANES), lambda h, qi: (0, 0)), qrow(KV_PAD), qrow(LANES), qrow(LANES)],
        out_specs=[qrow(Q_PAD), head(KV_PAD), pl.BlockSpec((tp, LANES), lambda h, qi: (0, 0))],
        out_shape=[_sds((tp, nh * Q_PAD), BF), _sds((tp, nh * KV_PAD), F32), _sds((tp, LANES), F32)],
        scratch_shapes=[pltpu.VMEM((TM, Q_PAD), F32)] + [pltpu.VMEM((TM, TM), F32)] * 4, args=(q2, kv, kr, do2, lse, dl),
    )


def _rms_bwd(x, g, dy):
    r = lax.rsqrt(jnp.mean(x * x, axis=-1, keepdims=True) + RMS_EPS)
    gy = dy * g
    dx = r * gy - x * (r * r * r) * jnp.mean(x * gy, axis=-1, keepdims=True)
    return dx, jnp.sum(dy * x * r, axis=0, keepdims=True)


def mla_prep_bwd(dq2, dkv, dkr, p, gq, gkv, wuq_p, wukv, tabs_bwd, dp, *, name):
    tp = dq2.shape[0]
    nh = MLA_HEADS

    def body(dq_ref, dkv_ref, dkr_ref, cq_ref, ckv_ref, gq_ref, gkv_ref, wuq_ref, wukv_ref,
             cq_t, s1q_t, s2q_t, ck_t, s1k_t, s2k_t, dp_in_ref, dqb_ref, dsm_ref, dgq_ref, dgkv_ref):
        i = pl.program_id(0)

        @pl.when(i == 0)
        def _():
            dgq_ref[...] = jnp.zeros_like(dgq_ref)
            dgkv_ref[...] = jnp.zeros_like(dgkv_ref)

        dqb = _rope(dq_ref[...].astype(F32), cq_t[...], s1q_t[...], s2q_t[...], nh).astype(BF)
        dqb_ref[...] = dqb
        dcqn = lax.dot_general(dqb, wuq_ref[...], NT, preferred_element_type=F32)
        dcq, dgq = _rms_bwd(cq_ref[...].astype(F32), gq_ref[...], dcqn)
        dckvn = lax.dot_general(dkv_ref[...].astype(BF), wukv_ref[...], NT, preferred_element_type=F32)
        dckv, dgkv = _rms_bwd(ckv_ref[...].astype(F32), gkv_ref[...], dckvn)
        dkr = _rope(dkr_ref[...], ck_t[...], s1k_t[...], s2k_t[...], 1)
        dsm_ref[...] = jnp.concatenate([dcq, dckv, dkr], axis=1).astype(BF)
        dgq_ref[...] += dgq
        dgkv_ref[...] += dgkv

    def rows(n, col=0):
        return pl.BlockSpec((TMH, n), functools.partial(lambda i, col: (i, col), col=col))

    return _pcall(
        body, name=name, grid=(tp // TMH,),
        in_specs=[rows(nh * Q_PAD), rows(nh * KV_PAD), rows(LANES), rows(Q_LORA, 1536 // Q_LORA), rows(KV_LORA, 1792 // KV_LORA),
                  _row_vec(Q_LORA), _row_vec(KV_LORA),
                  pl.BlockSpec((Q_LORA, nh * Q_PAD), lambda i: (0, 0)), pl.BlockSpec((KV_LORA, nh * KV_PAD), lambda i: (0, 0)),
                  rows(Q_PAD), rows(Q_PAD), rows(Q_PAD), rows(LANES), rows(LANES), rows(LANES), pl.BlockSpec(memory_space=pl.ANY)],
        out_specs=[rows(nh * Q_PAD), rows(Q_LORA + KV_LORA + LANES, 1536 // (Q_LORA + KV_LORA + LANES)), _row_vec(Q_LORA),
                   _row_vec(KV_LORA)],
        out_shape=[_sds((tp, nh * Q_PAD), BF), _sds(dp.shape, dp.dtype), _sds((1, Q_LORA), F32), _sds((1, KV_LORA), F32)],
        input_output_aliases={15: 1}, compiler_params=_params(1),
    )(dq2, dkv, dkr, p, p, gq, gkv, wuq_p, wukv, *tabs_bwd, dp)


def conv_bwd(dy, p, conv, w, dp, *, name):
    tp = dy.shape[0]
    nb = tp // TM

    def body(dy_ref, b_ref, c_ref, h_ref, cv_ref, w_ref, dp_in_ref, dp_ref, dw0_ref, dw1_ref, dw2_ref, dbuf):
        i = pl.program_id(0)

        @pl.when(i == 0)
        def _():
            dbuf[TM:TM + 8, :] = jnp.zeros((8, D_CONV), F32)
            dw0_ref[...] = jnp.zeros_like(dw0_ref)
            dw1_ref[...] = jnp.zeros_like(dw1_ref)
            dw2_ref[...] = jnp.zeros_like(dw2_ref)

        dyv = dy_ref[...]
        c = c_ref[...].astype(F32)
        hh = h_ref[...].astype(F32)
        dconv = dyv * b_ref[...].astype(F32)
        dbuf[0:TM, :] = dconv
        d1 = dbuf[pl.ds(1, TM), :]
        d2 = dbuf[pl.ds(2, TM), :]
        w_all = w_ref[...]
        de = w_all[2:3] * dconv + w_all[1:2] * d1 + w_all[0:1] * d2
        e = c * hh
        dp_ref[...] = jnp.concatenate([dyv * cv_ref[...].astype(F32), de * hh, de * c], axis=1).astype(BF)
        dw0_ref[...] += jnp.sum(d2 * e, axis=0, keepdims=True)
        dw1_ref[...] += jnp.sum(d1 * e, axis=0, keepdims=True)
        dw2_ref[...] += jnp.sum(dconv * e, axis=0, keepdims=True)
        dbuf[TM:TM + 8, :] = dbuf[0:8, :]

    def col(j):
        return pl.BlockSpec((TM, D_CONV), functools.partial(lambda i, j: (nb - 1 - i, j), j=j))

    return _pcall(
        body, name=name, grid=(nb,),
        in_specs=[col(0), col(0), col(1), col(2), col(0), pl.BlockSpec((3, D_CONV), lambda i: (0, 0)),
                  pl.BlockSpec(memory_space=pl.ANY)],
        out_specs=[pl.BlockSpec((TM, 3 * D_CONV), lambda i: (nb - 1 - i, 0))] + [_row_vec(D_CONV)] * 3,
        out_shape=[_sds(dp.shape, dp.dtype)] + [_sds((1, D_CONV), F32)] * 3, input_output_aliases={6: 0},
        scratch_shapes=[pltpu.VMEM((TM + 8, D_CONV), F32)], compiler_params=_params(1),
    )(dy, p, p, p, conv, w, dp)


def adamw(w, g, m, v, *, name):
    r, c = w.shape
    tr = r
    for cand in (256, 128, 64, 32, 16, 8):
        if r % cand == 0 and r > cand:
            tr = cand
            break

    def body(w_ref, g_ref, m_ref, v_ref, d_ref, nm_ref, nv_ref):
        gv = g_ref[...]
        nm = ADAM_B1 * m_ref[...] + (1.0 - ADAM_B1) * gv
        nv = ADAM_B2 * v_ref[...] + (1.0 - ADAM_B2) * (gv * gv)
        m_hat = nm / (1.0 - ADAM_B1 ** ADAM_STEP)
        v_hat = nv / (1.0 - ADAM_B2 ** ADAM_STEP)
        d_ref[...] = -ADAM_LR * (m_hat / (jnp.sqrt(v_hat) + ADAM_EPS) + ADAM_WD * w_ref[...])
        nm_ref[...] = nm
        nv_ref[...] = nv

    blk = pl.BlockSpec((tr, c), lambda i: (i, 0))
    return _pcall(
        body, name=name, grid=(r // tr,), in_specs=[blk] * 4, out_specs=[blk] * 3,
        out_shape=[_sds((r, c), F32)] * 3, compiler_params=_params(1),
    )(w, g, m, v)


HBM_SPEC = pl.BlockSpec(memory_space=pltpu.HBM)


def _place():
    return lax.axis_index("x"), lax.axis_index("y"), lax.axis_index("c")


def _other_chips(x, y):
    return [(1 - x, y), (x, 1 - y), (1 - x, 1 - y)]


def _half(ref_or_shape_rows, c):
    return pl.ds(c * (ref_or_shape_rows // 2), ref_or_shape_rows // 2)


def gather_side(items):
    n = len(items)
    shards = [s for s, _ in items]
    layers = [l for _, l in items]

    def program(x_refs, o_refs, send_sems, recv_sems):
        x, y, c = _place()
        me = 2 * x + y
        chips = _other_chips(x, y)

        def copy(sem, src, dst, to):
            return pltpu.make_async_remote_copy(src_ref=src, dst_ref=dst, send_sem=send_sems.at[sem], recv_sem=recv_sems.at[sem],
                                                device_id=to, device_id_type=MESH)

        def src(i):
            return x_refs[i].at[layers[i], _half(x_refs[i].shape[1], c)]

        def dst(i, slot, cc):
            return o_refs[i].at[slot, _half(o_refs[i].shape[1], cc)]

        sends = [copy(6 * i + k, src(i), dst(i, me, c), (px, py, c)) for i in range(n) for k, (px, py) in enumerate(chips)]
        passed = [copy(6 * i + 3 + k, dst(i, 2 * px + py, c), dst(i, 2 * px + py, c), (x, y, 1 - c))
                  for k, (px, py) in enumerate(chips) for i in range(n)]

        def start():
            for cp in sends:
                cp.start()

        def finish():
            pos = 0
            for k, (px, py) in enumerate(chips):
                for i in range(n):
                    copy(6 * i + k, src(i), dst(i, 2 * px + py, c), (px, py, c)).wait_recv()
                    passed[pos].start()
                    pos += 1
            for k, (px, py) in enumerate(chips):
                for i in range(n):
                    copy(6 * i + 3 + k, dst(i, 2 * px + py, 1 - c), dst(i, 2 * px + py, 1 - c), (x, y, 1 - c)).wait_recv()
            for cp in sends + passed:
                cp.wait_send()

        return start, finish

    prefilled = [jnp.broadcast_to(s[l][None], (4,) + s.shape[1:]) for s, l in items]
    return shards, prefilled, 6 * n, program


def scatter_side(pss):
    n = len(pss)

    def program(p_refs, o_refs, send_sems, recv_sems):
        x, y, c = _place()
        me = 2 * x + y
        chips = _other_chips(x, y)

        def copy(i, k, j_src, j_dst, to):
            return pltpu.make_async_remote_copy(src_ref=p_refs[i].at[j_src], dst_ref=o_refs[i].at[j_dst],
                                                send_sem=send_sems.at[3 * i + k], recv_sem=recv_sems.at[3 * i + k],
                                                device_id=to, device_id_type=MESH)

        sends = [copy(i, k, 2 * px + py, me, (px, py, c)) for i in range(n) for k, (px, py) in enumerate(chips)]

        def start():
            for cp in sends:
                cp.start()

        def finish():
            for i in range(n):
                for k, (px, py) in enumerate(chips):
                    copy(i, k, me, 2 * px + py, (px, py, c)).wait_recv()
            for cp in sends:
                cp.wait_send()

        return start, finish

    xi, yi, _ = _place()
    own = jnp.arange(4)[:, None, None] == 2 * xi + yi
    prefilled = [jnp.where(own, p, jnp.zeros_like(p)) for p in pss]
    return list(pss), prefilled, 3 * n, program


def exchange_alone(side, *, name):
    inputs, prefilled, n_sems, program = side
    a, b = len(inputs), len(prefilled)

    def body(*refs):
        start, finish = program(refs[:a], refs[a + b:a + 2 * b], refs[-2], refs[-1])
        start()
        finish()

    return _pcall(
        body, name=name, in_specs=[HBM_SPEC] * (a + b), out_specs=[HBM_SPEC] * b, out_shape=[_sds(p.shape, p.dtype) for p in prefilled],
        input_output_aliases={a + i: i for i in range(b)}, scratch_shapes=[pltpu.SemaphoreType.DMA((n_sems,))] * 2,
    )(*inputs, *prefilled)


def pair_exchange(gs, *, name):
    n = len(gs)

    def body(*refs):
        g_refs, o_refs = refs[:n], refs[n:2 * n]
        send_sems, recv_sems = refs[2 * n:]
        x, y, c = _place()
        cps = [pltpu.make_async_remote_copy(src_ref=g_refs[i].at[:, _half(g_refs[i].shape[1], 1 - c)], dst_ref=o_refs[i],
                                            send_sem=send_sems.at[i], recv_sem=recv_sems.at[i], device_id=(x, y, 1 - c),
                                            device_id_type=MESH)
               for i in range(n)]
        for cp in cps:
            cp.start()
        for cp in cps:
            cp.wait()

    return _pcall(
        body, name=name, in_specs=[HBM_SPEC] * n, out_specs=[HBM_SPEC] * n,
        out_shape=[_sds((4, g.shape[1] // 2, g.shape[2]), g.dtype) for g in gs],
        scratch_shapes=[pltpu.SemaphoreType.DMA((n,)), pltpu.SemaphoreType.DMA((n,))],
    )(*gs)


def _comm_rows(a, b, itemsize):
    return a // 2 if a * b * itemsize > (3 << 19) and a % 16 == 0 else a


def pair_add(g, s1, c_idx, *, name):
    n, a, b = g.shape
    ah = a // 2
    ta = _comm_rows(ah, b, 2)
    nblk = ah // ta

    def body(c_ref, g_ref, s_ref, o_ref):
        o_ref[...] = (g_ref[...].astype(F32) + s_ref[...].astype(F32)).astype(o_ref.dtype)

    grid_spec = pltpu.PrefetchScalarGridSpec(
        num_scalar_prefetch=1, grid=(n, nblk),
        in_specs=[pl.BlockSpec((1, ta, b), lambda j, i, c_ref: (j, c_ref[0] * nblk + i, 0)),
                  pl.BlockSpec((1, ta, b), lambda j, i, c_ref: (j, i, 0))],
        out_specs=pl.BlockSpec((1, ta, b), lambda j, i, c_ref: (j, i, 0)),
    )
    return _pcall(body, name=name, grid_spec=grid_spec, out_shape=_sds((n, ah, b), g.dtype), compiler_params=_params(2))(
        c_idx, g, s1)


def sum_chunks(s2, *, name):
    n, a, b = s2.shape
    ta = _comm_rows(a, b, 4)

    def body(s_ref, o_ref):
        acc = s_ref[0].astype(F32)
        for j in range(1, n):
            acc = acc + s_ref[j].astype(F32)
        o_ref[...] = acc

    return _pcall(
        body, name=name, grid=(a // ta,), in_specs=[pl.BlockSpec((n, ta, b), lambda i: (0, i, 0))],
        out_specs=pl.BlockSpec((ta, b), lambda i: (i, 0)), out_shape=_sds((a, b), F32), compiler_params=_params(1),
    )(s2)


def pair_gather(rcs, *, name):
    n = len(rcs)

    def body(*refs):
        r_refs, o_refs = refs[:n], refs[2 * n:3 * n]
        send_sems, recv_sems = refs[3 * n:]
        x, y, c = _place()

        def copy(i, half):
            return pltpu.make_async_remote_copy(src_ref=r_refs[i], dst_ref=o_refs[i].at[half], send_sem=send_sems.at[i],
                                                recv_sem=recv_sems.at[i], device_id=(x, y, 1 - c), device_id_type=MESH)

        sends = [copy(i, c) for i in range(n)]
        for cp in sends:
            cp.start()
        for i in range(n):
            copy(i, 1 - c).wait_recv()
        for cp in sends:
            cp.wait_send()

    prefilled = [jnp.broadcast_to(r[None], (2,) + r.shape) for r in rcs]
    return _pcall(
        body, name=name, in_specs=[HBM_SPEC] * (2 * n), out_specs=[HBM_SPEC] * n,
        out_shape=[_sds(p.shape, p.dtype) for p in prefilled], input_output_aliases={n + i: i for i in range(n)},
        scratch_shapes=[pltpu.SemaphoreType.DMA((n,)), pltpu.SemaphoreType.DMA((n,))],
    )(*rcs, *prefilled)


def exchange_small(arrs, *, reduce, name):
    n = len(arrs)

    def body(*refs):
        v_refs, o_refs = refs[:n], refs[n:2 * n]
        bufs = refs[2 * n:3 * n] if reduce else o_refs
        send_sems, recv_sems = refs[-2:]
        x, y, c = _place()
        me = 4 * x + 2 * y + c
        for i in range(n):
            bufs[i][me] = v_refs[i][...]

        def peer(k):
            dx, dy, dc = (k >> 2) & 1, (k >> 1) & 1, k & 1
            return (1 - x if dx else x, 1 - y if dy else y, 1 - c if dc else c)

        def copy(i, k, slot):
            return pltpu.make_async_remote_copy(src_ref=v_refs[i], dst_ref=bufs[i].at[slot], send_sem=send_sems.at[7 * i + k - 1],
                                                recv_sem=recv_sems.at[7 * i + k - 1], device_id=peer(k), device_id_type=MESH)

        sends = [copy(i, k, me) for i in range(n) for k in range(1, 8)]
        for cp in sends:
            cp.start()
        for i in range(n):
            for k in range(1, 8):
                px, py, pc = peer(k)
                copy(i, k, 4 * px + 2 * py + pc).wait_recv()
        for cp in sends:
            cp.wait_send()
        if reduce:
            for i in range(n):
                acc = bufs[i][0]
                for d in range(1, 8):
                    acc = acc + bufs[i][d]
                o_refs[i][...] = acc

    vmem = pl.BlockSpec(memory_space=pltpu.VMEM)
    stacked = [(8,) + a.shape for a in arrs]
    return _pcall(
        body, name=name, in_specs=[vmem] * n, out_specs=[vmem] * n,
        out_shape=[_sds(a.shape if reduce else s, F32) for a, s in zip(arrs, stacked)],
        scratch_shapes=([pltpu.VMEM(s, F32) for s in stacked] if reduce else [])
        + [pltpu.SemaphoreType.DMA((7 * n,)), pltpu.SemaphoreType.DMA((7 * n,))],
    )(*arrs)


def _pad_rows(n, mult):
    return -(-n // mult) * mult


def _chip_major(g, b):
    return g.reshape(g.shape[0], 4, b).transpose(1, 0, 2)


def _rope_tables(tp):
    inv_freq = 1.0 / (ROPE_BASE ** (jnp.arange(0, QK_ROPE, 2, dtype=F32) / QK_ROPE))
    ang = jnp.arange(tp, dtype=F32)[:, None] * inv_freq[None, :]
    cos, sin = jnp.cos(ang), jnp.sin(ang)
    one = lambda n: jnp.ones((tp, n), F32)
    zero = lambda n: jnp.zeros((tp, n), F32)
    cq = jnp.concatenate([one(128), cos, cos, one(96)], axis=1)
    s1q = jnp.concatenate([zero(144), sin, zero(96)], axis=1)
    s2q = jnp.concatenate([zero(128), -sin, zero(112)], axis=1)
    ck = jnp.concatenate([cos, cos, zero(96)], axis=1)
    s1k = jnp.concatenate([zero(16), sin, zero(96)], axis=1)
    s2k = jnp.concatenate([-sin, zero(112)], axis=1)
    fwd = (cq * (ATT_SCALE * LOG2E), s1q * (ATT_SCALE * LOG2E), s2q * (ATT_SCALE * LOG2E), ck, s1k, s2k)
    bwd = (cq * ATT_SCALE, -s1q * ATT_SCALE, -s2q * ATT_SCALE, ck, -s1k, -s2k)
    return fwd, bwd


def _pad_w_in(w):
    return jnp.concatenate([w[:, :1952], jnp.zeros((w.shape[0], 96), w.dtype), w[:, 1952:]], axis=1)


def _pad_w_uq(w):
    w = w.reshape(Q_LORA, MLA_HEADS, QK_NOPE + QK_ROPE)
    z = lambda n: jnp.zeros((Q_LORA, MLA_HEADS, n), w.dtype)
    return jnp.concatenate([w[..., :QK_NOPE], z(64), w[..., QK_NOPE:], z(96)], axis=-1).reshape(Q_LORA, MLA_HEADS * Q_PAD)


def _unpad_w_uq(w):
    w = w.reshape(Q_LORA, MLA_HEADS, Q_PAD)
    return jnp.concatenate([w[..., :QK_NOPE], w[..., 128:128 + QK_ROPE]], axis=-1).reshape(Q_LORA, MLA_HEADS * (QK_NOPE + QK_ROPE))


def _pad_w_br_mla(w):
    w = w.reshape(MLA_HEADS, V_HEAD, D_MODEL)
    return jnp.concatenate([jnp.zeros_like(w), w], axis=1).reshape(MLA_HEADS * KV_PAD, D_MODEL)


def _unpad_w_br_mla(w):
    return w.reshape(MLA_HEADS, KV_PAD, D_MODEL)[:, V_HEAD:].reshape(MLA_HEADS * V_HEAD, D_MODEL)


def _riding(hooks, where, l, *args):
    make = hooks.get(where)
    ride = make(l, *args) if make else None
    return ride if ride else (None, lambda results: None)


def _layer_fwd(l, st, xprev, gp, bp, hb, w, tabs, hooks):
    ln_g, ln_b = w["ln_g"], w["ln_b"]
    lg = lambda k: ln_g[l, k][None]
    lb = lambda k: ln_b[l, k][None]
    s = {}
    s["x0"], s["gp0"], s["bp0"], s["hb0"] = xprev, gp, bp, hb
    side, got = _riding(hooks, "ffn1_fwd", l)
    s["g1"], s["u1"], s["a1"], *extras = ffn_up(hb, w["ffn1_w_up"][l], name="ffn_up", side=side)
    got(extras)
    s["xh1"], s["rs1"], s["hb1"] = down_ln(s["a1"], w["ffn1_w_down"][l], xprev, gp, bp, lg(0), lb(0), name="ffn_down_ln")
    s["p"] = mm_rows([(s["hb1"], w["mix_w_in"][l], False, 0)], D_IN_PAD, name="mix_in", tn=1024, out_dtype=BF)
    gq, gkv = w["q_norm_g"][l][None], w["kv_norm_g"][l][None]
    s["cqn"], s["ckvn"], s["q2"], s["kv"], s["kr"] = mla_prep(s["p"], gq, gkv, w["w_uq"][l], w["w_ukv"][l], tabs, name="mla_prep")
    side, got = _riding(hooks, "attn_fwd", l)
    s["o2"], s["lse"], *extras = attn_fwd(s["q2"], s["kv"], s["kr"], name="attn_fwd", side=side)
    got(extras)
    s["ycv"], s["conv"] = conv_fwd(s["p"], w["conv_w"][l], name="conv_fwd")
    s["bc"], s["bm"], s["mg"], s["xh2"], s["rs2"], s["hb2"] = merge_out_ln(
        s["ycv"], s["o2"], s["p"], w["mix_b_gate"][l], w["w_br_conv"][l], w["w_br_mla"][l], w["w_o"][l],
        s["xh1"], lg(0), lb(0), lg(1), lb(1), name="merge_out_ln")
    s["g2"], s["u2"], s["a2"] = ffn_up(s["hb2"], w["ffn2_w_up"][l], name="ffn_up")
    s["xh3"], s["rs3"], s["hb3"] = down_ln(s["a2"], w["ffn2_w_down"][l], s["xh2"], lg(1), lb(1), lg(2), lb(2), name="ffn_down_ln")
    st.append(s)
    return s["xh3"], lg(2), lb(2), s["hb3"]


def _ffn_bwd(which, l, g, hooks, dh, w_up, w_down, ln_gain, hb_in, gate, up, act, xh, rs):
    dzb, dgam, dbet, *loss_acc = ln_bwd(dh, xh, rs, ln_gain, branch_scale=0.5, name="ln_bwd")
    if loss_acc:
        g["loss"] = loss_acc[0]
    g[which + "_w_down"] = tn_mm(act, dzb, tm=D_FF // 2, name="dw_down", shard=("rows", D_FF // 4))
    side, got = _riding(hooks, which + "_down_bwd", l, g)
    dgate, dup, *extras = ffn_down_bwd(dzb, w_down, gate, up, name="ffn_down_bwd", side=side)
    got(extras)
    d_w = tn_mm(hb_in, dgate, tm=512, name="dw_up", shard=("cols", D_FF // 2), slot0=0)
    g[which + "_w_up"] = tn_mm(hb_in, dup, tm=512, name="dw_up", shard=("cols", D_FF // 2), slot0=2, dst=d_w)
    side, got = _riding(hooks, which + "_up_bwd", l, g)
    dh_in = mm_rows([(dgate, w_up, True, 0), (dup, w_up, True, 1)], D_MODEL, name="ffn_up_bwd", tn=512, addend=dzb, add_scale=2.0 * ALPHA,
                    side=side)
    if side is not None:
        dh_in, *extras = dh_in
        got(extras)
    return dh_in, dgam, dbet


def _layer_bwd(l, s, dh, w, tabs_bwd, hooks):
    ln_g = w["ln_g"]
    lg = lambda k: ln_g[l, k][None]
    g = {}
    dh, dg2, db2 = _ffn_bwd("ffn2", l, g, hooks, dh, w["ffn2_w_up"][l], w["ffn2_w_down"][l], lg(2), s["hb2"], s["g2"], s["u2"],
                            s["a2"], s["xh3"], s["rs3"])
    dzb, dg1, db1 = ln_bwd(dh, s["xh2"], s["rs2"], lg(1), branch_scale=1.0, name="ln_bwd")
    g["w_o"] = tn_mm(s["mg"], dzb, tm=1024, name="dw_o", shard=("rows", D_MODEL // 4))
    dbc, dbm, dp, dycv, do2, dl, g["mix_b_gate"] = merge_bwd(
        dzb, w["w_o"][l], s["bc"], s["bm"], s["p"], w["mix_b_gate"][l], w["w_br_conv"][l], w["w_br_mla"][l], s["o2"], name="merge_bwd")
    g["w_br_conv"] = tn_mm(s["ycv"], dbc, tm=512, name="dw_br_conv", shard=("cols", D_MODEL // 4))
    g["w_br_mla"] = _chip_major(_unpad_w_br_mla(tn_mm(s["o2"], dbm, tm=1024, name="dw_br_mla")), D_MODEL // 4)
    side, got = _riding(hooks, "attn_bwd", l, g)
    dq2, dkv, dkr, *extras = attn_bwd(s["q2"], s["kv"], s["kr"], do2, s["lse"], dl, name="attn_bwd", side=side)
    got(extras)
    gq, gkv = w["q_norm_g"][l][None], w["kv_norm_g"][l][None]
    dqb, dp, g["q_norm_g"], g["kv_norm_g"] = mla_prep_bwd(dq2, dkv, dkr, s["p"], gq, gkv, w["w_uq"][l], w["w_ukv"][l], tabs_bwd, dp,
                                                          name="mla_prep_bwd")
    g["w_uq"] = _chip_major(_unpad_w_uq(tn_mm(s["cqn"], dqb, tm=Q_LORA, name="dw_uq")), MLA_HEADS * (QK_NOPE + QK_ROPE) // 4)
    g["w_ukv"] = tn_mm(s["ckvn"], dkv, tm=KV_LORA, name="dw_ukv", shard=("cols", MLA_HEADS * KV_PAD // 4))
    dp, dw0, dw1, dw2 = conv_bwd(dycv, s["p"], s["conv"], w["conv_w"][l], dp, name="conv_bwd")
    g["conv_w"] = jnp.concatenate([dw0, dw1, dw2], axis=0)
    d_in = tn_mm(s["hb1"], dp, tm=512, name="dw_in")
    g["mix_w_in"] = _chip_major(jnp.concatenate([d_in[:, :1952], d_in[:, 2048:]], axis=1), D_IN // 4)
    dh = mm_rows([(dp, w["mix_w_in"][l], True, 0)], D_MODEL, name="mix_in_bwd", tn=512, addend=dzb, add_scale=ALPHA)
    dh, dg0, db0 = _ffn_bwd("ffn1", l, g, hooks, dh, w["ffn1_w_up"][l], w["ffn1_w_down"][l], lg(0), s["hb0"], s["g1"], s["u1"],
                            s["a1"], s["xh1"], s["rs1"])
    g["ln_g"] = jnp.concatenate([dg0, dg1, dg2], axis=0)
    g["ln_b"] = jnp.concatenate([db0, db1, db2], axis=0)
    return dh, g


BIG = ("ffn1_w_up", "ffn1_w_down", "mix_w_in", "w_uq", "w_ukv", "w_br_conv", "w_br_mla", "w_o", "ffn2_w_up", "ffn2_w_down")
BIG_AXIS = (2, 1, 2, 2, 2, 2, 2, 1, 2, 1)
FFN1_MATRICES = ("ffn1_w_up", "ffn1_w_down")
MIXER_MATRICES = ("mix_w_in", "w_uq", "w_ukv", "w_br_conv", "w_br_mla", "w_o")
FFN2_MATRICES = ("ffn2_w_up", "ffn2_w_down")
SMALL_SHARDED = ("meta_tokens", "mix_b_gate", "conv_w", "ln_g", "ln_b")
SMALL_REPLICATED = ("q_norm_g", "kv_norm_g")
WEIGHTS = ("meta_tokens", "ffn1_w_up", "ffn1_w_down", "mix_w_in", "mix_b_gate", "conv_w", "q_norm_g", "w_uq", "kv_norm_g", "w_ukv",
           "w_br_conv", "w_br_mla", "w_o", "ffn2_w_up", "ffn2_w_down", "ln_g", "ln_b")


def _view2d(a):
    return a.reshape(-1, a.shape[-1])


def _local_grads(x_row, target_row, w, hooks=None):
    hooks = hooks or {}
    seq = x_row.shape[0]
    t_real = N_META + seq
    tp = _pad_rows(t_real, TM)
    pad = tp - t_real
    h0 = jnp.concatenate([w["meta_tokens"], x_row, jnp.zeros((pad, D_MODEL), F32)], axis=0)
    target_p = jnp.concatenate([jnp.zeros((N_META, D_MODEL), F32), target_row, jnp.zeros((pad, D_MODEL), F32)], axis=0)
    tabs, tabs_bwd = _rope_tables(tp)
    ones = jnp.ones((1, D_MODEL), F32)
    zeros = jnp.zeros((1, D_MODEL), F32)
    saved = []
    cur = (h0, ones, zeros, h0.astype(BF))
    for l in range(DEPTH):
        cur = _layer_fwd(l, saved, *cur, w, tabs, hooks)
    dh = (cur[2], target_p, seq)
    grads = [None] * DEPTH
    for l in reversed(range(DEPTH)):
        dh, grads[l] = _layer_bwd(l, saved[l], dh, w, tabs_bwd, hooks)
        if "layer_bwd_done" in hooks:
            hooks["layer_bwd_done"](l, grads[l])
    return grads[DEPTH - 1].pop("loss"), dh[N_META:t_real], dh[:N_META], grads


def kernel(x, meta_tokens, ffn1_w_up, ffn1_w_down, mix_w_in, mix_b_gate, conv_w, q_norm_g, w_uq, kv_norm_g, w_ukv, w_br_conv, w_br_mla, w_o, ffn2_w_up, ffn2_w_down, ln_g, ln_b, loss_target, m_meta_tokens, m_ffn1_w_up, m_ffn1_w_down, m_mix_w_in, m_mix_b_gate, m_conv_w, m_q_norm_g, m_w_uq, m_kv_norm_g, m_w_ukv, m_w_br_conv, m_w_br_mla, m_w_o, m_ffn2_w_up, m_ffn2_w_down, m_ln_g, m_ln_b, v_meta_tokens, v_ffn1_w_up, v_ffn1_w_down, v_mix_w_in, v_mix_b_gate, v_conv_w, v_q_norm_g, v_w_uq, v_kv_norm_g, v_w_ukv, v_w_br_conv, v_w_br_mla, v_w_o, v_ffn2_w_up, v_ffn2_w_down, v_ln_g, v_ln_b):
    local = dict(meta_tokens=meta_tokens, ffn1_w_up=ffn1_w_up, ffn1_w_down=ffn1_w_down, mix_w_in=mix_w_in, mix_b_gate=mix_b_gate,
                 conv_w=conv_w, q_norm_g=q_norm_g, w_uq=w_uq, kv_norm_g=kv_norm_g, w_ukv=w_ukv, w_br_conv=w_br_conv,
                 w_br_mla=w_br_mla, w_o=w_o, ffn2_w_up=ffn2_w_up, ffn2_w_down=ffn2_w_down, ln_g=ln_g, ln_b=ln_b)
    mom_m = dict(zip(WEIGHTS, (m_meta_tokens, m_ffn1_w_up, m_ffn1_w_down, m_mix_w_in, m_mix_b_gate, m_conv_w, m_q_norm_g, m_w_uq,
                               m_kv_norm_g, m_w_ukv, m_w_br_conv, m_w_br_mla, m_w_o, m_ffn2_w_up, m_ffn2_w_down, m_ln_g, m_ln_b)))
    mom_v = dict(zip(WEIGHTS, (v_meta_tokens, v_ffn1_w_up, v_ffn1_w_down, v_mix_w_in, v_mix_b_gate, v_conv_w, v_q_norm_g, v_w_uq,
                               v_kv_norm_g, v_w_ukv, v_w_br_conv, v_w_br_mla, v_w_o, v_ffn2_w_up, v_ffn2_w_down, v_ln_g, v_ln_b)))
    xi, yi, ci = _place()
    chip = 2 * xi + yi

    shards = {n: local[n].astype(BF) for n in BIG}
    axis = dict(zip(BIG, BIG_AXIS))
    pad_layout = {"mix_w_in": _pad_w_in, "w_uq": _pad_w_uq, "w_br_mla": _pad_w_br_mla}
    w = {n: [None] * DEPTH for n in BIG}

    def fetch(keys):
        def install(gathered):
            for (n, l), g in zip(keys, gathered):
                if n in FFN1_MATRICES + FFN2_MATRICES:
                    w[n][l] = g
                    continue
                full = jnp.concatenate([g[j] for j in range(4)], axis=axis[n] - 1)
                w[n][l] = pad_layout[n](full) if n in pad_layout else full
        return gather_side([(shards[n], l) for n, l in keys]), install

    first, install_first = fetch([("ffn1_w_up", 0)])
    install_first(exchange_alone(first, name="gather_weights"))
    fetch_under = {("ffn1_fwd", 0): [("ffn1_w_down", 0)] + [(n, 0) for n in MIXER_MATRICES],
                   ("attn_fwd", 0): [(n, 0) for n in FFN2_MATRICES] + [(n, 1) for n in BIG]}
    hooks = {where: functools.partial(lambda l, where: fetch(fetch_under[where, l]) if (where, l) in fetch_under else None, where=where)
             for where in ("ffn1_fwd", "attn_fwd")}
    stacked = exchange_small([_view2d(local[n]) for n in SMALL_SHARDED], reduce=False, name="gather_small")
    for n, st in zip(SMALL_SHARDED, stacked):
        full = jnp.concatenate([st[2 * j] for j in range(4)], axis=-1)
        w[n] = full.reshape(local[n].shape[:-1] + (full.shape[-1],))
    for n in SMALL_REPLICATED:
        w[n] = local[n]

    c_idx = jnp.reshape(ci, (1,)).astype(jnp.int32)
    done, from_chips = {}, {}

    def send(keys, grad_of):
        glist = [grad_of[k] for k in keys]
        from_sibling = pair_exchange(glist, name="rs_pair_exchange")
        sums = [pair_add(a, s, c_idx, name="rs_pair_add") for a, s in zip(glist, from_sibling)]
        return scatter_side(sums), lambda results: from_chips.update(zip(keys, results))

    send_under = {"attn_bwd": FFN2_MATRICES + ("w_o", "w_br_conv", "w_br_mla"),
                  "ffn1_down_bwd": ("mix_w_in", "w_uq", "w_ukv", "ffn1_w_down"), "ffn1_up_bwd": ("ffn1_w_up",)}
    hooks["layer_bwd_done"] = lambda l, g: done.update({(n, l): g[n] for n in BIG})

    def send_hook(where):
        def hook(l, g):
            if l != 0:
                return None
            keys = [(n, 0) for n in send_under[where]] + ([(n, 1) for n in BIG] if where == "attn_bwd" else [])
            return send(keys, {**done, **{(n, 0): g[n] for n in send_under[where]}})
        return hook

    for where in send_under:
        hooks[where] = send_hook(where)

    loss_acc, grad_x, d_meta, grads = _local_grads(x[0], loss_target[0], w, hooks)
    grad_x = grad_x[None]
    keys = [(n, l) for n in BIG for l in range(DEPTH)]
    reduced = pair_gather([sum_chunks(from_chips[k], name="rs_sum") for k in keys], name="rs_pair_gather")
    reduced = {k: r.reshape(local[k[0]].shape[1:]) for k, r in zip(keys, reduced)}
    gshard = {n: jnp.stack([reduced[n, l] for l in range(DEPTH)]) for n in BIG}

    small_names = SMALL_SHARDED + SMALL_REPLICATED
    gsmall = {n: jnp.concatenate([grads[l][n] for l in range(DEPTH)], axis=0) for n in small_names if n != "meta_tokens"}
    gsmall["meta_tokens"] = d_meta
    small_red = exchange_small([gsmall[n] for n in small_names] + [loss_acc], reduce=True, name="reduce_small")
    loss = small_red[-1][0, 0]
    for n, full in zip(small_names, small_red[:-1]):
        if n in SMALL_SHARDED:
            sh = local[n].shape[-1]
            full = lax.dynamic_slice_in_dim(full, chip * sh, sh, axis=1)
        gshard[n] = full.reshape(local[n].shape)

    delta, new_m, new_v = {}, {}, {}
    for n in WEIGHTS:
        shape = local[n].shape
        d, nm, nv = adamw(_view2d(local[n]), _view2d(gshard[n]), _view2d(mom_m[n]), _view2d(mom_v[n]), name="adamw")
        delta[n], new_m[n], new_v[n] = d.reshape(shape), nm.reshape(shape), nv.reshape(shape)
    return (loss, grad_x, *[gshard[n] for n in WEIGHTS], *[delta[n] for n in WEIGHTS], *[new_m[n] for n in WEIGHTS],
            *[new_v[n] for n in WEIGHTS])
```

```python
import functools

import jax
import jax.numpy as jnp
from jax import lax
from jax.experimental import pallas as pl
from jax.experimental.pallas import tpu as pltpu

F32 = jnp.float32
BF = jnp.bfloat16
MESH = pl.DeviceIdType.MESH

D_MODEL = 1024
DEPTH = 2
N_META = 16
D_CONV = 512
MLA_HEADS = 8
QK_NOPE = 64
QK_ROPE = 32
V_HEAD = 64
Q_LORA = 256
KV_LORA = 128
ROPE_BASE = 10000.0
NEG_INF = -1e30
D_FF = 2816
ALPHA = (2 * DEPTH) ** 0.25
LN_EPS = 1e-5
RMS_EPS = 1e-6
ATT_SCALE = (QK_NOPE + QK_ROPE) ** -0.5
LOG2E = 1.4426950408889634
LN2 = 0.6931471805599453
D_IN = 4000
D_IN_PAD = 4096
COL_CQ = 3 * D_CONV
COL_CKV = COL_CQ + Q_LORA
COL_KR = COL_CKV + KV_LORA
COL_KR_END = COL_KR + QK_ROPE
KV_PAD = 128
COL_GATES = COL_KR + KV_PAD
Q_PAD = 256
Q_ROPE_AT = 128

ADAM_LR = 0.001
ADAM_B1 = 0.9
ADAM_B2 = 0.999
ADAM_EPS = 1e-08
ADAM_WD = 0.01
ADAM_STEP = 10

TM = 768
TMH = 384
LANES = 128
VMEM_LIMIT_BYTES = 50 * 1024 * 1024

NT = (((1,), (1,)), ((), ()))
TN = (((0,), (0,)), ((), ()))


def _pcall(body, **kw):
    return pl.pallas_call(body, **kw)


def _params(n_axes):
    return pltpu.CompilerParams(dimension_semantics=("arbitrary",) * n_axes, vmem_limit_bytes=VMEM_LIMIT_BYTES)


def _sds(shape, dtype):
    return jax.ShapeDtypeStruct(shape, dtype)


def mm_rows(pairs, n_out, *, name, tn=None, addend=None, add_scale=1.0, out_dtype=F32, side=None):
    tp = pairs[0][0].shape[0]
    tn = tn or n_out
    in_specs, args = [], []
    for a, b, nt, kb in pairs:
        k = a.shape[1]
        in_specs.append(pl.BlockSpec((TM, k), lambda i, j: (i, 0)))
        if nt and b.ndim == 3:
            in_specs.append(pl.BlockSpec((2, tn, k // 2), functools.partial(lambda i, j, kb: (kb, j, 0), kb=kb)))
        elif nt:
            in_specs.append(pl.BlockSpec((tn, k), functools.partial(lambda i, j, kb: (j, kb), kb=kb)))
        else:
            in_specs.append(pl.BlockSpec((k, tn), lambda i, j: (0, j)))
        args += [a, b]
    if addend is not None:
        in_specs.append(pl.BlockSpec((TM, tn), lambda i, j: (i, j)))
        args.append(addend)
    n_pairs = len(pairs)
    nts = [p[2] for p in pairs]

    def body(refs, out_refs, scratch):
        o_ref = out_refs[0]
        acc = None
        for p in range(n_pairs):
            a = refs[2 * p][...].astype(BF)
            b = refs[2 * p + 1][...]
            if b.ndim == 3:
                b = jnp.concatenate([b[0], b[1]], axis=1)
            d = lax.dot_general(a, b, NT if nts[p] else (((1,), (0,)), ((), ())), preferred_element_type=F32)
            acc = d if acc is None else acc + d
        if addend is not None:
            acc = acc + add_scale * refs[2 * n_pairs][...].astype(F32)
        o_ref[...] = acc.astype(o_ref.dtype)

    out = _side_call(
        body, side, name=name, grid=(tp // TM, n_out // tn), in_specs=in_specs,
        out_specs=[pl.BlockSpec((TM, tn), lambda i, j: (i, j))], out_shape=[_sds((tp, n_out), out_dtype)],
        scratch_shapes=[], args=args,
    )
    return out if side is not None else out[0]


def tn_mm(a, b, *, tm, name, out_dtype=BF, shard=None, slot0=0, dst=None):
    tp, m = a.shape
    n = b.shape[1]
    nk = tp // TM
    if shard is None:
        pieces, out_block, out_index, out_full = 1, (tm, n), (lambda i, k: (i, 0)), (m, n)
    elif shard[0] == "cols":
        pieces = n // shard[1]
        out_block, out_full = (pieces, tm, shard[1]), (4, m, shard[1])
        out_index = lambda i, k: (slot0 // pieces, i, 0)
    else:
        pieces = tm // shard[1]
        out_block, out_full = (pieces, shard[1], n), (4, m // 4, n)
        out_index = lambda i, k: (i, 0, 0)

    def body(a_ref, b_ref, *rest):
        o_ref, acc_ref = rest[-2], rest[-1]
        k = pl.program_id(1)

        @pl.when(k == 0)
        def _():
            acc_ref[...] = jnp.zeros_like(acc_ref)

        acc_ref[...] += lax.dot_general(a_ref[...].astype(BF), b_ref[...].astype(BF), TN, preferred_element_type=F32)

        @pl.when(k == nk - 1)
        def _():
            if shard is None:
                o_ref[...] = acc_ref[...].astype(o_ref.dtype)
            elif shard[0] == "cols":
                for j in range(pieces):
                    o_ref[j] = acc_ref[:, j * shard[1]:(j + 1) * shard[1]].astype(o_ref.dtype)
            else:
                for j in range(pieces):
                    o_ref[j] = acc_ref[j * shard[1]:(j + 1) * shard[1], :].astype(o_ref.dtype)

    in_specs = [pl.BlockSpec((TM, tm), lambda i, k: (k, i)), pl.BlockSpec((TM, n), lambda i, k: (k, 0))]
    args = [a, b]
    aliases = {}
    if dst is not None:
        in_specs.append(pl.BlockSpec(memory_space=pl.ANY))
        args.append(dst)
        aliases = {2: 0}
    return _pcall(
        body, name=name, grid=(m // tm, nk), in_specs=in_specs, out_specs=pl.BlockSpec(out_block, out_index),
        out_shape=_sds(out_full, out_dtype), input_output_aliases=aliases,
        scratch_shapes=[pltpu.VMEM((tm, n), F32)], compiler_params=_params(2),
    )(*args)


def _ln_store(z, g_ref, b_ref, xh_ref, rs_ref, hb_ref):
    mu = jnp.mean(z, axis=-1, keepdims=True)
    zc = z - mu
    var = jnp.mean(zc * zc, axis=-1, keepdims=True)
    rstd = lax.rsqrt(var + LN_EPS)
    xh = zc * rstd
    xh_ref[...] = xh
    rs_ref[...] = rstd
    hb_ref[...] = (xh * g_ref[...] + b_ref[...]).astype(BF)


def _ln_out(tp, tm=TM):
    specs = [pl.BlockSpec((tm, D_MODEL), lambda i: (i, 0)), pl.BlockSpec((tm, 1), lambda i: (i, 0)),
             pl.BlockSpec((tm, D_MODEL), lambda i: (i, 0))]
    shapes = [_sds((tp, D_MODEL), F32), _sds((tp, 1), F32), _sds((tp, D_MODEL), BF)]
    return specs, shapes


def _row_vec(n):
    return pl.BlockSpec((1, n), lambda i: (0, 0))


def ffn_up(hb, wup, *, name, side=None):
    tp = hb.shape[0]
    tn = D_FF // 2
    nj = D_FF // tn

    def body(in_refs, out_refs, scratch):
        h_ref, wg_ref, wu_ref = in_refs
        g_ref, u_ref, a_ref = out_refs
        h = h_ref[...]
        g = jnp.dot(h, wg_ref[0], preferred_element_type=F32)
        u = jnp.dot(h, wu_ref[0], preferred_element_type=F32)
        g_ref[...] = g.astype(BF)
        u_ref[...] = u.astype(BF)
        a_ref[...] = (g * jax.nn.sigmoid(g) * u).astype(BF)

    blk = pl.BlockSpec((TM, tn), lambda i, j: (i, j))
    return _side_call(
        body, side, name=name, grid=(tp // TM, nj),
        in_specs=[pl.BlockSpec((TM, D_MODEL), lambda i, j: (i, 0)), pl.BlockSpec((1, D_MODEL, tn), lambda i, j: (j, 0, 0)),
                  pl.BlockSpec((1, D_MODEL, tn), lambda i, j: (j + nj, 0, 0))],
        out_specs=[blk, blk, blk], out_shape=[_sds((tp, D_FF), BF)] * 3, scratch_shapes=[], args=(hb, wup, wup),
    )


def down_ln(a, wd, xprev, gp, bp, g, b, *, name):
    tp = a.shape[0]

    def body(a_ref, wd_ref, xp_ref, gp_ref, bp_ref, g_ref, b_ref, xh_ref, rs_ref, hb_ref):
        wd = jnp.concatenate([wd_ref[j] for j in range(4)], axis=0)
        f = jnp.dot(a_ref[...], wd, preferred_element_type=F32)
        hprev = xp_ref[...] * gp_ref[...] + bp_ref[...]
        _ln_store(ALPHA * hprev + 0.5 * f, g_ref, b_ref, xh_ref, rs_ref, hb_ref)

    out_specs, out_shape = _ln_out(tp)
    return _pcall(
        body, name=name, grid=(tp // TM,),
        in_specs=[pl.BlockSpec((TM, D_FF), lambda i: (i, 0)), pl.BlockSpec((4, D_FF // 4, D_MODEL), lambda i: (0, 0, 0)),
                  pl.BlockSpec((TM, D_MODEL), lambda i: (i, 0))] + [_row_vec(D_MODEL)] * 4,
        out_specs=out_specs, out_shape=out_shape, compiler_params=_params(1),
    )(a, wd, xprev, gp, bp, g, b)


def _rope(x, c, s1, s2, reps):
    n = x.shape[1]
    half = QK_ROPE // 2
    if reps > 1:
        c, s1, s2 = (jnp.tile(t, (1, reps)) for t in (c, s1, s2))
    return x * c + pltpu.roll(x, half, 1) * s1 + pltpu.roll(x, n - half, 1) * s2


def _rms(x, g):
    r = lax.rsqrt(jnp.mean(x * x, axis=-1, keepdims=True) + RMS_EPS)
    return x * r * g, r


def mla_prep(p, gq, gkv, wuq_p, wukv, tabs, *, name):
    tp = p.shape[0]
    nh = MLA_HEADS

    def body(cq_ref, ckv_ref, kr_ref, gq_ref, gkv_ref, wuq_ref, wukv_ref, cq_t, s1q_t, s2q_t, ck_t, s1k_t, s2k_t,
             cqn_ref, ckvn_ref, q2_ref, kv_ref, krr_ref):
        cqn, _ = _rms(cq_ref[...].astype(F32), gq_ref[...])
        ckvn, _ = _rms(ckv_ref[...].astype(F32), gkv_ref[...])
        cqn = cqn.astype(BF)
        ckvn = ckvn.astype(BF)
        cqn_ref[...] = cqn
        ckvn_ref[...] = ckvn
        q = jnp.dot(cqn, wuq_ref[...], preferred_element_type=F32)
        q2_ref[...] = _rope(q, cq_t[...], s1q_t[...], s2q_t[...], nh).astype(BF)
        kv_ref[...] = jnp.dot(ckvn, wukv_ref[...], preferred_element_type=F32).astype(BF)
        krr_ref[...] = _rope(kr_ref[...].astype(F32), ck_t[...], s1k_t[...], s2k_t[...], 1).astype(BF)

    def rows(n, col=0):
        return pl.BlockSpec((TMH, n), functools.partial(lambda i, col: (i, col), col=col))

    return _pcall(
        body, name=name, grid=(tp // TMH,),
        in_specs=[rows(Q_LORA, COL_CQ // Q_LORA), rows(KV_LORA, COL_CKV // KV_LORA), rows(LANES, COL_KR // LANES),
                  _row_vec(Q_LORA), _row_vec(KV_LORA),
                  pl.BlockSpec((Q_LORA, nh * Q_PAD), lambda i: (0, 0)), pl.BlockSpec((KV_LORA, nh * KV_PAD), lambda i: (0, 0)),
                  rows(Q_PAD), rows(Q_PAD), rows(Q_PAD), rows(LANES), rows(LANES), rows(LANES)],
        out_specs=[rows(Q_LORA), rows(KV_LORA), rows(nh * Q_PAD), rows(nh * KV_PAD), rows(LANES)],
        out_shape=[_sds((tp, Q_LORA), BF), _sds((tp, KV_LORA), BF), _sds((tp, nh * Q_PAD), BF),
                   _sds((tp, nh * KV_PAD), BF), _sds((tp, LANES), BF)],
        compiler_params=_params(1),
    )(p, p, p, gq, gkv, wuq_p, wukv, *tabs)


def _causal_mask(s):
    qpos = lax.broadcasted_iota(jnp.int32, (TM, TM), 0)
    kpos = lax.broadcasted_iota(jnp.int32, (TM, TM), 1)
    return jnp.where(kpos <= qpos, s, NEG_INF)


def _key_rows(k):
    return pl.ds(pl.multiple_of(k * TM, TM), TM)


def _pipelined_key_blocks(n, prefetch, process):
    prefetch(0, 0)

    def pair(j, carry):
        prefetch(2 * j + 1, 1)
        process(2 * j, 0, False)
        prefetch(2 * j + 2, 0)
        process(2 * j + 1, 1, False)
        return carry

    lax.fori_loop(0, n // 2, pair, 0)

    @pl.when(n % 2 == 1)
    def _():
        prefetch(n, 1)
        process(n - 1, 0, False)
        process(n, 1, True)

    @pl.when(n % 2 == 0)
    def _():
        process(n, 0, True)


def _side_call(body_main, side, *, name, grid, in_specs, out_specs, out_shape, scratch_shapes, args):
    n_in, n_out, n_scr = len(in_specs), len(out_specs), len(scratch_shapes)
    s_in, s_pre, n_sems, program = side if side is not None else ((), (), 0, None)
    a, b = len(s_in), len(s_pre)

    def body(*refs):
        in_refs = refs[:n_in]
        out_refs = refs[n_in + a + b:n_in + a + b + n_out]
        scr = refs[n_in + a + 2 * b + n_out:n_in + a + 2 * b + n_out + n_scr]
        if side is not None:
            side_in = refs[n_in:n_in + a]
            side_out = refs[n_in + a + b + n_out:n_in + a + 2 * b + n_out]
            start, finish = program(side_in, side_out, refs[-2], refs[-1])

            @pl.when((pl.program_id(0) == 0) & (pl.program_id(1) == 0))
            def _():
                start()

        body_main(in_refs, out_refs, scr)
        if side is not None:
            @pl.when((pl.program_id(0) == grid[0] - 1) & (pl.program_id(1) == grid[1] - 1))
            def _():
                finish()

    sems = [pltpu.SemaphoreType.DMA((n_sems,))] * 2 if side is not None else []
    return _pcall(
        body, name=name, grid=grid, in_specs=list(in_specs) + [HBM_SPEC] * (a + b), out_specs=list(out_specs) + [HBM_SPEC] * b,
        out_shape=list(out_shape) + [_sds(p.shape, p.dtype) for p in s_pre],
        input_output_aliases={n_in + a + i: n_out + i for i in range(b)},
        scratch_shapes=list(scratch_shapes) + sems, compiler_params=_params(2),
    )(*args, *s_in, *s_pre)


def attn_fwd(q2, kv, kr, *, name, side=None):
    tp = q2.shape[0]
    nh = MLA_HEADS
    nb = tp // TM
    rep = TM // LANES
    n_blocks = nb * (nb + 1) // 2

    def body(in_refs, out_refs, scratch):
        q_ref, kv_ref, kr_ref = in_refs
        o_ref, lse_ref = out_refs
        m_ref, l_ref, acc_ref, s0_ref, s1_ref, p_ref, alpha_ref = scratch
        s_refs = (s0_ref, s1_ref)

        def start_query_block():
            m_ref[...] = jnp.full_like(m_ref, NEG_INF)
            l_ref[...] = jnp.zeros_like(l_ref)
            acc_ref[...] = jnp.zeros_like(acc_ref)

        def prefetch(qi, k, slot):
            k2 = jnp.concatenate([kv_ref[_key_rows(k), :], kr_ref[_key_rows(k), :]], axis=1)
            s_refs[slot][...] = lax.dot_general(q_ref[_key_rows(qi), :], k2, NT, preferred_element_type=F32)

        def finish_query_block(qi):
            l = jnp.sum(l_ref[...], axis=1, keepdims=True)
            o_ref[_key_rows(qi), :] = (acc_ref[...] / l).astype(BF)
            lse_ref[_key_rows(qi), :] = m_ref[...] + jnp.log2(l)

        def following(qi, k):
            last = k == qi
            return jnp.where(last, qi + 1, qi), jnp.where(last, 0, k + 1)

        def block(qi, k, slot, has_next):
            nqi, nk = following(qi, k)

            @pl.when(k < qi)
            def _():
                if has_next:
                    prefetch(nqi, nk, 1 - slot)
                process(k, slot, False)

            @pl.when(k == qi)
            def _():
                if has_next:
                    prefetch(nqi, nk, 1 - slot)
                process(k, slot, True)
                finish_query_block(qi)
                start_query_block()

            return nqi, nk

        def process(k, slot, diagonal):
            for r in range(TM // LANES):
                rows = slice(r * LANES, (r + 1) * LANES)
                s = s_refs[slot][rows, :]
                if diagonal:
                    qpos = r * LANES + lax.broadcasted_iota(jnp.int32, (LANES, TM), 0)
                    s = jnp.where(lax.broadcasted_iota(jnp.int32, (LANES, TM), 1) <= qpos, s, NEG_INF)
                m_prev = m_ref[rows, :]
                m_new = jnp.maximum(m_prev, jnp.max(s, axis=1, keepdims=True))
                alpha = jnp.exp2(m_prev - m_new)
                p = jnp.exp2(s - jnp.tile(m_new, (1, rep)))
                lane_sums = p[:, 0:LANES]
                for t in range(1, rep):
                    lane_sums = lane_sums + p[:, t * LANES:(t + 1) * LANES]
                l_ref[rows, :] = alpha * l_ref[rows, :] + lane_sums
                p_ref[rows, :] = p.astype(BF)
                alpha_ref[rows, :] = alpha
                m_ref[rows, :] = m_new
            acc_ref[...] = alpha_ref[...] * acc_ref[...] + jnp.dot(p_ref[...], kv_ref[_key_rows(k), :], preferred_element_type=F32)

        start_query_block()
        prefetch(0, 0, 0)

        def two_blocks(j, carry):
            qi, k = block(*carry, 0, True)
            return block(qi, k, 1, True)

        qi, k = lax.fori_loop(0, (n_blocks - 1) // 2, two_blocks, (jnp.int32(0), jnp.int32(0)))
        if (n_blocks - 1) % 2:
            qi, k = block(qi, k, 0, True)
        block(qi, k, (n_blocks - 1) % 2, False)

    return _side_call(
        body, side, name=name, grid=(nh, 1),
        in_specs=[pl.BlockSpec((tp, Q_PAD), lambda h, z: (0, h)), pl.BlockSpec((tp, KV_PAD), lambda h, z: (0, h)),
                  pl.BlockSpec((tp, LANES), lambda h, z: (0, 0))],
        out_specs=[pl.BlockSpec((tp, KV_PAD), lambda h, z: (0, h)), pl.BlockSpec((tp, LANES), lambda h, z: (0, h))],
        out_shape=[_sds((tp, nh * KV_PAD), BF), _sds((tp, nh * LANES), F32)],
        scratch_shapes=[pltpu.VMEM((TM, LANES), F32)] * 3 + [pltpu.VMEM((TM, TM), F32)] * 2
        + [pltpu.VMEM((TM, TM), BF), pltpu.VMEM((TM, LANES), F32)], args=(q2, kv, kr),
    )


def conv_fwd(p, w, *, name):
    tp = p.shape[0]

    def body(b_ref, c_ref, h_ref, w_ref, y_ref, cv_ref, ebuf):
        i = pl.program_id(0)

        @pl.when(i == 0)
        def _():
            ebuf[0:8, :] = jnp.zeros((8, D_CONV), F32)

        e = c_ref[...].astype(F32) * h_ref[...].astype(F32)
        ebuf[8:8 + TM, :] = e
        w_all = w_ref[...]
        conv = w_all[0:1] * ebuf[pl.ds(6, TM), :] + w_all[1:2] * ebuf[pl.ds(7, TM), :] + w_all[2:3] * e
        cv_ref[...] = conv.astype(BF)
        y_ref[...] = (b_ref[...].astype(F32) * conv).astype(BF)
        ebuf[0:8, :] = ebuf[TM:TM + 8, :]

    def col(j):
        return pl.BlockSpec((TM, D_CONV), functools.partial(lambda i, j: (i, j), j=j))

    return _pcall(
        body, name=name, grid=(tp // TM,),
        in_specs=[col(0), col(1), col(2), pl.BlockSpec((3, D_CONV), lambda i: (0, 0))],
        out_specs=[col(0), col(0)], out_shape=[_sds((tp, D_CONV), BF)] * 2,
        scratch_shapes=[pltpu.VMEM((TM + 8, D_CONV), F32)], compiler_params=_params(1),
    )(p, p, p, w)


def merge_out_ln(ycv, o2, p, bg, wbc, wbm_p, wo, xprev, gp, bp, g, b, *, name):
    tp = ycv.shape[0]

    def body(y_ref, o_ref, gc_ref, gm_ref, bg_ref, wbc_ref, wbm_ref, wo_ref, xp_ref, gp_ref, bp_ref, g_ref, b_ref,
             bc_ref, bm_ref, mg_ref, xh_ref, rs_ref, hb_ref):
        bc = jnp.dot(y_ref[...], wbc_ref[...], preferred_element_type=F32)
        bm = jnp.dot(o_ref[...], wbm_ref[...], preferred_element_type=F32)
        bgv = bg_ref[...]
        mg = (jax.nn.sigmoid(gc_ref[...].astype(F32) + bgv[0:1]) * bc
              + jax.nn.sigmoid(gm_ref[...].astype(F32) + bgv[1:2]) * bm)
        mgb = mg.astype(BF)
        bc_ref[...] = bc.astype(BF)
        bm_ref[...] = bm.astype(BF)
        mg_ref[...] = mgb
        mix = jnp.dot(mgb, wo_ref[...], preferred_element_type=F32)
        hprev = xp_ref[...] * gp_ref[...] + bp_ref[...]
        _ln_store(ALPHA * hprev + mix, g_ref, b_ref, xh_ref, rs_ref, hb_ref)

    def rows(n, col=0):
        return pl.BlockSpec((TMH, n), functools.partial(lambda i, col: (i, col), col=col))

    def whole(r, c):
        return pl.BlockSpec((r, c), lambda i: (0, 0))

    ln_specs, ln_shapes = _ln_out(tp, TMH)
    return _pcall(
        body, name=name, grid=(tp // TMH,),
        in_specs=[rows(D_CONV), rows(MLA_HEADS * KV_PAD), rows(D_MODEL, COL_GATES // D_MODEL), rows(D_MODEL, COL_GATES // D_MODEL + 1),
                  whole(2, D_MODEL),
                  whole(D_CONV, D_MODEL), whole(MLA_HEADS * KV_PAD, D_MODEL), whole(D_MODEL, D_MODEL), rows(D_MODEL)]
        + [_row_vec(D_MODEL)] * 4,
        out_specs=[rows(D_MODEL)] * 3 + ln_specs, out_shape=[_sds((tp, D_MODEL), BF)] * 3 + ln_shapes,
        compiler_params=_params(1),
    )(ycv, o2, p, p, bg, wbc, wbm_p, wo, xprev, gp, bp, g, b)


def ln_bwd(dh, xh, rstd, g, *, branch_scale, name):
    tp = xh.shape[0]
    from_loss = isinstance(dh, tuple)

    def body(*refs):
        if from_loss:
            xh_ref, rs_ref, g_ref, b_ref, t_ref, dzb_ref, dg_ref, db_ref, loss_ref = refs
        else:
            dh_ref, xh_ref, rs_ref, g_ref, dzb_ref, dg_ref, db_ref = refs
        i = pl.program_id(0)

        @pl.when(i == 0)
        def _():
            dg_ref[...] = jnp.zeros_like(dg_ref)
            db_ref[...] = jnp.zeros_like(db_ref)
            if from_loss:
                loss_ref[...] = jnp.zeros_like(loss_ref)

        xhat = xh_ref[...]
        if from_loss:
            row = i * TM + lax.broadcasted_iota(jnp.int32, (TM, 1), 0)
            real = (row >= N_META) & (row < N_META + dh[2])
            diff = jnp.where(real, xhat * g_ref[...] + b_ref[...] - t_ref[...], 0.0)
            loss_ref[...] += 0.5 / D_MODEL * jnp.sum(diff * diff)
            dy = diff * (1.0 / D_MODEL)
        else:
            dy = dh_ref[...]
        dg_ref[...] += jnp.sum(dy * xhat, axis=0, keepdims=True)
        db_ref[...] += jnp.sum(dy, axis=0, keepdims=True)
        dxh = dy * g_ref[...]
        m1 = jnp.mean(dxh, axis=-1, keepdims=True)
        m2 = jnp.mean(dxh * xhat, axis=-1, keepdims=True)
        dz = rs_ref[...] * (dxh - m1 - xhat * m2)
        dzb_ref[...] = (branch_scale * dz).astype(BF)

    rows = pl.BlockSpec((TM, D_MODEL), lambda i: (i, 0))
    stat = pl.BlockSpec((TM, 1), lambda i: (i, 0))
    vec = _row_vec(D_MODEL)
    out_specs = [rows, vec, vec]
    out_shape = [_sds((tp, D_MODEL), BF), _sds((1, D_MODEL), F32), _sds((1, D_MODEL), F32)]
    if from_loss:
        in_specs, args = [rows, stat, vec, vec, rows], (xh, rstd, g, dh[0], dh[1])
        out_specs.append(pl.BlockSpec((8, LANES), lambda i: (0, 0)))
        out_shape.append(_sds((8, LANES), F32))
    else:
        in_specs, args = [rows, rows, stat, vec], (dh, xh, rstd, g)
    return _pcall(body, name=name, grid=(tp // TM,), in_specs=in_specs, out_specs=out_specs, out_shape=out_shape,
                  compiler_params=_params(1))(*args)


def ffn_down_bwd(dzb, wd, gate, up, *, name, side=None):
    tp = dzb.shape[0]
    tn = D_FF // 2

    def body(in_refs, out_refs, scratch):
        dz_ref, wd_ref, g_ref, u_ref = in_refs
        dg_ref, du_ref = out_refs
        wd = jnp.concatenate([wd_ref[0], wd_ref[1]], axis=0)
        da = lax.dot_general(dz_ref[...], wd, NT, preferred_element_type=F32)
        g = g_ref[...].astype(F32)
        u = u_ref[...].astype(F32)
        sg = jax.nn.sigmoid(g)
        dg_ref[...] = (da * u * sg * (1.0 + g * (1.0 - sg))).astype(BF)
        du_ref[...] = (da * g * sg).astype(BF)

    blk = pl.BlockSpec((TM, tn), lambda i, j: (i, j))
    return _side_call(
        body, side, name=name, grid=(tp // TM, D_FF // tn),
        in_specs=[pl.BlockSpec((TM, D_MODEL), lambda i, j: (i, 0)), pl.BlockSpec((2, tn // 2, D_MODEL), lambda i, j: (j, 0, 0)), blk, blk],
        out_specs=[blk, blk], out_shape=[_sds((tp, D_FF), BF)] * 2, scratch_shapes=[], args=(dzb, wd, gate, up),
    )


def merge_bwd(dzb, wo, bc, bm, p, bg, wbc, wbm_p, o2, *, name):
    tp = dzb.shape[0]
    nh = MLA_HEADS

    def body(dz_ref, wo_ref, bc_ref, bm_ref, gc_ref, gm_ref, bg_ref, wbc_ref, wbm_ref, o_ref,
             dbc_ref, dbm_ref, dgg_ref, dy_ref, do_ref, dl_ref, dbg_ref):
        i = pl.program_id(0)

        @pl.when(i == 0)
        def _():
            dbg_ref[...] = jnp.zeros_like(dbg_ref)

        dmg = lax.dot_general(dz_ref[...], wo_ref[...], NT, preferred_element_type=F32)
        bgv = bg_ref[...]
        sc = jax.nn.sigmoid(gc_ref[...].astype(F32) + bgv[0:1])
        sm = jax.nn.sigmoid(gm_ref[...].astype(F32) + bgv[1:2])
        dbc = (dmg * sc).astype(BF)
        dbm = (dmg * sm).astype(BF)
        dgc = dmg * bc_ref[...].astype(F32) * sc * (1.0 - sc)
        dgm = dmg * bm_ref[...].astype(F32) * sm * (1.0 - sm)
        dbc_ref[...] = dbc
        dbm_ref[...] = dbm
        dgg_ref[...] = jnp.concatenate([dgc, dgm], axis=1).astype(BF)
        dbg_ref[...] += jnp.concatenate([jnp.sum(dgc, axis=0, keepdims=True), jnp.sum(dgm, axis=0, keepdims=True)], axis=0)
        dy_ref[...] = lax.dot_general(dbc, wbc_ref[...], NT, preferred_element_type=F32)
        do = lax.dot_general(dbm, wbm_ref[...], NT, preferred_element_type=F32)
        do_ref[...] = do.astype(BF)
        prod = do * o_ref[...].astype(F32)
        parts = []
        for h in range(nh):
            d = jnp.sum(prod[:, h * KV_PAD:(h + 1) * KV_PAD], axis=1, keepdims=True)
            parts.append(jnp.broadcast_to(d, (TMH, LANES)))
        dl_ref[...] = jnp.concatenate(parts, axis=1)

    def rows(n, col=0):
        return pl.BlockSpec((TMH, n), functools.partial(lambda i, col: (i, col), col=col))

    def whole(r, c):
        return pl.BlockSpec((r, c), lambda i: (0, 0))

    return _pcall(
        body, name=name, grid=(tp // TMH,),
        in_specs=[rows(D_MODEL), whole(D_MODEL, D_MODEL), rows(D_MODEL), rows(D_MODEL), rows(D_MODEL, COL_GATES // D_MODEL),
                  rows(D_MODEL, COL_GATES // D_MODEL + 1),
                  whole(2, D_MODEL), whole(D_CONV, D_MODEL), whole(nh * KV_PAD, D_MODEL), rows(nh * KV_PAD)],
        out_specs=[rows(D_MODEL), rows(D_MODEL), rows(2 * D_MODEL, COL_GATES // (2 * D_MODEL)), rows(D_CONV), rows(nh * KV_PAD),
                   rows(nh * LANES),
                   whole(2, D_MODEL)],
        out_shape=[_sds((tp, D_MODEL), BF), _sds((tp, D_MODEL), BF), _sds((tp, D_IN_PAD), BF), _sds((tp, D_CONV), F32),
                   _sds((tp, nh * KV_PAD), BF), _sds((tp, nh * LANES), F32), _sds((2, D_MODEL), F32)],
        compiler_params=_params(1),
    )(dzb, wo, bc, bm, p, p, bg, wbc, wbm_p, o2)


def attn_bwd(q2, kv, kr, do2, lse, dl, *, name, side=None):
    tp = q2.shape[0]
    nh = MLA_HEADS
    nb = tp // TM
    rep = TM // LANES

    def body(in_refs, out_refs, scratch):
        q_ref, kv_ref, kr_ref, do_ref, lse_ref, dl_ref = in_refs
        dq_ref, dkv_ref, dkr_ref = out_refs
        dq_acc, s0_ref, s1_ref, dp0_ref, dp1_ref = scratch
        qi = pl.program_id(1)
        s_refs, dp_refs = (s0_ref, s1_ref), (dp0_ref, dp1_ref)

        @pl.when(qi == 0)
        def _():
            dkv_ref[...] = jnp.zeros_like(dkv_ref)

        @pl.when((qi == 0) & (pl.program_id(0) == 0))
        def _():
            dkr_ref[...] = jnp.zeros_like(dkr_ref)

        dq_acc[...] = jnp.zeros_like(dq_acc)

        def prefetch(k, slot):
            kvb = kv_ref[_key_rows(k), :]
            k2 = jnp.concatenate([kvb, kr_ref[_key_rows(k), :]], axis=1)
            s_refs[slot][...] = lax.dot_general(q_ref[...], k2, NT, preferred_element_type=F32)
            dp_refs[slot][...] = lax.dot_general(do_ref[...], kvb, NT, preferred_element_type=F32)

        def process(k, slot, diagonal):
            rows = _key_rows(k)
            s = s_refs[slot][...]
            if diagonal:
                s = _causal_mask(s)
            p = jnp.exp2(s - jnp.tile(lse_ref[...], (1, rep)))
            dsb = (p * (dp_refs[slot][...] - jnp.tile(dl_ref[...], (1, rep)))).astype(BF)
            dk2 = lax.dot_general(dsb, q_ref[...], TN, preferred_element_type=F32) * LN2
            dkv_ref[rows, :] += lax.dot_general(p.astype(BF), do_ref[...], TN, preferred_element_type=F32) + dk2[:, :KV_PAD]
            dkr_ref[rows, :] += dk2[:, KV_PAD:KV_PAD + LANES]
            k2 = jnp.concatenate([kv_ref[rows, :], kr_ref[rows, :]], axis=1)
            dq_acc[...] += jnp.dot(dsb, k2, preferred_element_type=F32)

        _pipelined_key_blocks(qi, prefetch, process)
        dq_ref[...] = dq_acc[...].astype(BF)

    def qrow(n):
        return pl.BlockSpec((TM, n), lambda h, qi: (qi, h))

    def head(n):
        return pl.BlockSpec((tp, n), lambda h, qi: (0, h))

    return _side_call(
        body, side, name=name, grid=(nh, nb),
        in_specs=[qrow(Q_PAD), head(KV_PAD), pl.BlockSpec((tp, LANES), lambda h, qi: (0, 0)), qrow(KV_PAD), qrow(LANES), qrow(LANES)],
        out_specs=[qrow(Q_PAD), head(KV_PAD), pl.BlockSpec((tp, LANES), lambda h, qi: (0, 0))],
        out_shape=[_sds((tp, nh * Q_PAD), BF), _sds((tp, nh * KV_PAD), F32), _sds((tp, LANES), F32)],
        scratch_shapes=[pltpu.VMEM((TM, Q_PAD), F32)] + [pltpu.VMEM((TM, TM), F32)] * 4, args=(q2, kv, kr, do2, lse, dl),
    )


def _rms_bwd(x, g, dy):
    r = lax.rsqrt(jnp.mean(x * x, axis=-1, keepdims=True) + RMS_EPS)
    gy = dy * g
    dx = r * gy - x * (r * r * r) * jnp.mean(x * gy, axis=-1, keepdims=True)
    return dx, jnp.sum(dy * x * r, axis=0, keepdims=True)


def mla_prep_bwd(dq2, dkv, dkr, p, gq, gkv, wuq_p, wukv, tabs_bwd, dp, *, name):
    tp = dq2.shape[0]
    nh = MLA_HEADS

    def body(dq_ref, dkv_ref, dkr_ref, cq_ref, ckv_ref, gq_ref, gkv_ref, wuq_ref, wukv_ref,
             cq_t, s1q_t, s2q_t, ck_t, s1k_t, s2k_t, dp_in_ref, dqb_ref, dsm_ref, dgq_ref, dgkv_ref):
        i = pl.program_id(0)

        @pl.when(i == 0)
        def _():
            dgq_ref[...] = jnp.zeros_like(dgq_ref)
            dgkv_ref[...] = jnp.zeros_like(dgkv_ref)

        dqb = _rope(dq_ref[...].astype(F32), cq_t[...], s1q_t[...], s2q_t[...], nh).astype(BF)
        dqb_ref[...] = dqb
        dcqn = lax.dot_general(dqb, wuq_ref[...], NT, preferred_element_type=F32)
        dcq, dgq = _rms_bwd(cq_ref[...].astype(F32), gq_ref[...], dcqn)
        dckvn = lax.dot_general(dkv_ref[...].astype(BF), wukv_ref[...], NT, preferred_element_type=F32)
        dckv, dgkv = _rms_bwd(ckv_ref[...].astype(F32), gkv_ref[...], dckvn)
        dkr = _rope(dkr_ref[...], ck_t[...], s1k_t[...], s2k_t[...], 1)
        dsm_ref[...] = jnp.concatenate([dcq, dckv, dkr], axis=1).astype(BF)
        dgq_ref[...] += dgq
        dgkv_ref[...] += dgkv

    def rows(n, col=0):
        return pl.BlockSpec((TMH, n), functools.partial(lambda i, col: (i, col), col=col))

    return _pcall(
        body, name=name, grid=(tp // TMH,),
        in_specs=[rows(nh * Q_PAD), rows(nh * KV_PAD), rows(LANES), rows(Q_LORA, COL_CQ // Q_LORA), rows(KV_LORA, COL_CKV // KV_LORA),
                  _row_vec(Q_LORA), _row_vec(KV_LORA),
                  pl.BlockSpec((Q_LORA, nh * Q_PAD), lambda i: (0, 0)), pl.BlockSpec((KV_LORA, nh * KV_PAD), lambda i: (0, 0)),
                  rows(Q_PAD), rows(Q_PAD), rows(Q_PAD), rows(LANES), rows(LANES), rows(LANES), pl.BlockSpec(memory_space=pl.ANY)],
        out_specs=[rows(nh * Q_PAD), rows(COL_GATES - COL_CQ, COL_CQ // (COL_GATES - COL_CQ)), _row_vec(Q_LORA),
                   _row_vec(KV_LORA)],
        out_shape=[_sds((tp, nh * Q_PAD), BF), _sds(dp.shape, dp.dtype), _sds((1, Q_LORA), F32), _sds((1, KV_LORA), F32)],
        input_output_aliases={15: 1}, compiler_params=_params(1),
    )(dq2, dkv, dkr, p, p, gq, gkv, wuq_p, wukv, *tabs_bwd, dp)


def conv_bwd(dy, p, conv, w, dp, *, name):
    tp = dy.shape[0]
    nb = tp // TM

    def body(dy_ref, b_ref, c_ref, h_ref, cv_ref, w_ref, dp_in_ref, dp_ref, dw0_ref, dw1_ref, dw2_ref, dbuf):
        i = pl.program_id(0)

        @pl.when(i == 0)
        def _():
            dbuf[TM:TM + 8, :] = jnp.zeros((8, D_CONV), F32)
            dw0_ref[...] = jnp.zeros_like(dw0_ref)
            dw1_ref[...] = jnp.zeros_like(dw1_ref)
            dw2_ref[...] = jnp.zeros_like(dw2_ref)

        dyv = dy_ref[...]
        c = c_ref[...].astype(F32)
        hh = h_ref[...].astype(F32)
        dconv = dyv * b_ref[...].astype(F32)
        dbuf[0:TM, :] = dconv
        d1 = dbuf[pl.ds(1, TM), :]
        d2 = dbuf[pl.ds(2, TM), :]
        w_all = w_ref[...]
        de = w_all[2:3] * dconv + w_all[1:2] * d1 + w_all[0:1] * d2
        e = c * hh
        dp_ref[...] = jnp.concatenate([dyv * cv_ref[...].astype(F32), de * hh, de * c], axis=1).astype(BF)
        dw0_ref[...] += jnp.sum(d2 * e, axis=0, keepdims=True)
        dw1_ref[...] += jnp.sum(d1 * e, axis=0, keepdims=True)
        dw2_ref[...] += jnp.sum(dconv * e, axis=0, keepdims=True)
        dbuf[TM:TM + 8, :] = dbuf[0:8, :]

    def col(j):
        return pl.BlockSpec((TM, D_CONV), functools.partial(lambda i, j: (nb - 1 - i, j), j=j))

    return _pcall(
        body, name=name, grid=(nb,),
        in_specs=[col(0), col(0), col(1), col(2), col(0), pl.BlockSpec((3, D_CONV), lambda i: (0, 0)),
                  pl.BlockSpec(memory_space=pl.ANY)],
        out_specs=[pl.BlockSpec((TM, 3 * D_CONV), lambda i: (nb - 1 - i, 0))] + [_row_vec(D_CONV)] * 3,
        out_shape=[_sds(dp.shape, dp.dtype)] + [_sds((1, D_CONV), F32)] * 3, input_output_aliases={6: 0},
        scratch_shapes=[pltpu.VMEM((TM + 8, D_CONV), F32)], compiler_params=_params(1),
    )(dy, p, p, p, conv, w, dp)


def adamw(w, g, m, v, *, name):
    r, c = w.shape
    tr = r
    for cand in (256, 128, 64, 32, 16, 8):
        if r % cand == 0 and r > cand:
            tr = cand
            break

    def body(w_ref, g_ref, m_ref, v_ref, d_ref, nm_ref, nv_ref):
        gv = g_ref[...]
        nm = ADAM_B1 * m_ref[...] + (1.0 - ADAM_B1) * gv
        nv = ADAM_B2 * v_ref[...] + (1.0 - ADAM_B2) * (gv * gv)
        m_hat = nm / (1.0 - ADAM_B1 ** ADAM_STEP)
        v_hat = nv / (1.0 - ADAM_B2 ** ADAM_STEP)
        d_ref[...] = -ADAM_LR * (m_hat / (jnp.sqrt(v_hat) + ADAM_EPS) + ADAM_WD * w_ref[...])
        nm_ref[...] = nm
        nv_ref[...] = nv

    blk = pl.BlockSpec((tr, c), lambda i: (i, 0))
    return _pcall(
        body, name=name, grid=(r // tr,), in_specs=[blk] * 4, out_specs=[blk] * 3,
        out_shape=[_sds((r, c), F32)] * 3, compiler_params=_params(1),
    )(w, g, m, v)


HBM_SPEC = pl.BlockSpec(memory_space=pltpu.HBM)


def _place():
    return lax.axis_index("x"), lax.axis_index("y"), lax.axis_index("c")


def _other_chips(x, y):
    return [(1 - x, y), (x, 1 - y), (1 - x, 1 - y)]


def _half(ref_or_shape_rows, c):
    return pl.ds(c * (ref_or_shape_rows // 2), ref_or_shape_rows // 2)


def gather_side(items):
    n = len(items)
    shards = [s for s, _ in items]
    layers = [l for _, l in items]

    def program(x_refs, o_refs, send_sems, recv_sems):
        x, y, c = _place()
        me = 2 * x + y
        chips = _other_chips(x, y)

        def copy(sem, src, dst, to):
            return pltpu.make_async_remote_copy(src_ref=src, dst_ref=dst, send_sem=send_sems.at[sem], recv_sem=recv_sems.at[sem],
                                                device_id=to, device_id_type=MESH)

        def src(i):
            return x_refs[i].at[layers[i], _half(x_refs[i].shape[1], c)]

        def dst(i, slot, cc):
            return o_refs[i].at[slot, _half(o_refs[i].shape[1], cc)]

        sends = [copy(6 * i + k, src(i), dst(i, me, c), (px, py, c)) for i in range(n) for k, (px, py) in enumerate(chips)]
        passed = [copy(6 * i + 3 + k, dst(i, 2 * px + py, c), dst(i, 2 * px + py, c), (x, y, 1 - c))
                  for k, (px, py) in enumerate(chips) for i in range(n)]

        def start():
            for cp in sends:
                cp.start()

        def finish():
            pos = 0
            for k, (px, py) in enumerate(chips):
                for i in range(n):
                    copy(6 * i + k, src(i), dst(i, 2 * px + py, c), (px, py, c)).wait_recv()
                    passed[pos].start()
                    pos += 1
            for k, (px, py) in enumerate(chips):
                for i in range(n):
                    copy(6 * i + 3 + k, dst(i, 2 * px + py, 1 - c), dst(i, 2 * px + py, 1 - c), (x, y, 1 - c)).wait_recv()
            for cp in sends + passed:
                cp.wait_send()

        return start, finish

    prefilled = [jnp.broadcast_to(s[l][None], (4,) + s.shape[1:]) for s, l in items]
    return shards, prefilled, 6 * n, program


def scatter_side(pss):
    n = len(pss)

    def program(p_refs, o_refs, send_sems, recv_sems):
        x, y, c = _place()
        me = 2 * x + y
        chips = _other_chips(x, y)

        def copy(i, k, j_src, j_dst, to):
            return pltpu.make_async_remote_copy(src_ref=p_refs[i].at[j_src], dst_ref=o_refs[i].at[j_dst],
                                                send_sem=send_sems.at[3 * i + k], recv_sem=recv_sems.at[3 * i + k],
                                                device_id=to, device_id_type=MESH)

        sends = [copy(i, k, 2 * px + py, me, (px, py, c)) for i in range(n) for k, (px, py) in enumerate(chips)]

        def start():
            for cp in sends:
                cp.start()

        def finish():
            for i in range(n):
                for k, (px, py) in enumerate(chips):
                    copy(i, k, me, 2 * px + py, (px, py, c)).wait_recv()
            for cp in sends:
                cp.wait_send()

        return start, finish

    xi, yi, _ = _place()
    own = jnp.arange(4)[:, None, None] == 2 * xi + yi
    prefilled = [jnp.where(own, p, jnp.zeros_like(p)) for p in pss]
    return list(pss), prefilled, 3 * n, program


def exchange_alone(side, *, name):
    inputs, prefilled, n_sems, program = side
    a, b = len(inputs), len(prefilled)

    def body(*refs):
        start, finish = program(refs[:a], refs[a + b:a + 2 * b], refs[-2], refs[-1])
        start()
        finish()

    return _pcall(
        body, name=name, in_specs=[HBM_SPEC] * (a + b), out_specs=[HBM_SPEC] * b, out_shape=[_sds(p.shape, p.dtype) for p in prefilled],
        input_output_aliases={a + i: i for i in range(b)}, scratch_shapes=[pltpu.SemaphoreType.DMA((n_sems,))] * 2,
    )(*inputs, *prefilled)


def pair_exchange(gs, *, name):
    n = len(gs)

    def body(*refs):
        g_refs, o_refs = refs[:n], refs[n:2 * n]
        send_sems, recv_sems = refs[2 * n:]
        x, y, c = _place()
        cps = [pltpu.make_async_remote_copy(src_ref=g_refs[i].at[:, _half(g_refs[i].shape[1], 1 - c)], dst_ref=o_refs[i],
                                            send_sem=send_sems.at[i], recv_sem=recv_sems.at[i], device_id=(x, y, 1 - c),
                                            device_id_type=MESH)
               for i in range(n)]
        for cp in cps:
            cp.start()
        for cp in cps:
            cp.wait()

    return _pcall(
        body, name=name, in_specs=[HBM_SPEC] * n, out_specs=[HBM_SPEC] * n,
        out_shape=[_sds((4, g.shape[1] // 2, g.shape[2]), g.dtype) for g in gs],
        scratch_shapes=[pltpu.SemaphoreType.DMA((n,)), pltpu.SemaphoreType.DMA((n,))],
    )(*gs)


def _comm_rows(a, b, itemsize):
    return a // 2 if a * b * itemsize > (3 << 19) and a % 16 == 0 else a


def pair_add(g, s1, c_idx, *, name):
    n, a, b = g.shape
    ah = a // 2
    ta = _comm_rows(ah, b, 2)
    nblk = ah // ta

    def body(c_ref, g_ref, s_ref, o_ref):
        o_ref[...] = (g_ref[...].astype(F32) + s_ref[...].astype(F32)).astype(o_ref.dtype)

    grid_spec = pltpu.PrefetchScalarGridSpec(
        num_scalar_prefetch=1, grid=(n, nblk),
        in_specs=[pl.BlockSpec((1, ta, b), lambda j, i, c_ref: (j, c_ref[0] * nblk + i, 0)),
                  pl.BlockSpec((1, ta, b), lambda j, i, c_ref: (j, i, 0))],
        out_specs=pl.BlockSpec((1, ta, b), lambda j, i, c_ref: (j, i, 0)),
    )
    return _pcall(body, name=name, grid_spec=grid_spec, out_shape=_sds((n, ah, b), g.dtype), compiler_params=_params(2))(
        c_idx, g, s1)


def sum_chunks(s2, *, name):
    n, a, b = s2.shape
    ta = _comm_rows(a, b, 4)

    def body(s_ref, o_ref):
        acc = s_ref[0].astype(F32)
        for j in range(1, n):
            acc = acc + s_ref[j].astype(F32)
        o_ref[...] = acc

    return _pcall(
        body, name=name, grid=(a // ta,), in_specs=[pl.BlockSpec((n, ta, b), lambda i: (0, i, 0))],
        out_specs=pl.BlockSpec((ta, b), lambda i: (i, 0)), out_shape=_sds((a, b), F32), compiler_params=_params(1),
    )(s2)


def pair_gather(rcs, *, name):
    n = len(rcs)

    def body(*refs):
        r_refs, o_refs = refs[:n], refs[2 * n:3 * n]
        send_sems, recv_sems = refs[3 * n:]
        x, y, c = _place()

        def copy(i, half):
            return pltpu.make_async_remote_copy(src_ref=r_refs[i], dst_ref=o_refs[i].at[half], send_sem=send_sems.at[i],
                                                recv_sem=recv_sems.at[i], device_id=(x, y, 1 - c), device_id_type=MESH)

        sends = [copy(i, c) for i in range(n)]
        for cp in sends:
            cp.start()
        for i in range(n):
            copy(i, 1 - c).wait_recv()
        for cp in sends:
            cp.wait_send()

    prefilled = [jnp.broadcast_to(r[None], (2,) + r.shape) for r in rcs]
    return _pcall(
        body, name=name, in_specs=[HBM_SPEC] * (2 * n), out_specs=[HBM_SPEC] * n,
        out_shape=[_sds(p.shape, p.dtype) for p in prefilled], input_output_aliases={n + i: i for i in range(n)},
        scratch_shapes=[pltpu.SemaphoreType.DMA((n,)), pltpu.SemaphoreType.DMA((n,))],
    )(*rcs, *prefilled)


def exchange_small(arrs, *, reduce, name):
    n = len(arrs)

    def body(*refs):
        v_refs, o_refs = refs[:n], refs[n:2 * n]
        bufs = refs[2 * n:3 * n] if reduce else o_refs
        send_sems, recv_sems = refs[-2:]
        x, y, c = _place()
        me = 4 * x + 2 * y + c
        for i in range(n):
            bufs[i][me] = v_refs[i][...]

        def peer(k):
            dx, dy, dc = (k >> 2) & 1, (k >> 1) & 1, k & 1
            return (1 - x if dx else x, 1 - y if dy else y, 1 - c if dc else c)

        def copy(i, k, slot):
            return pltpu.make_async_remote_copy(src_ref=v_refs[i], dst_ref=bufs[i].at[slot], send_sem=send_sems.at[7 * i + k - 1],
                                                recv_sem=recv_sems.at[7 * i + k - 1], device_id=peer(k), device_id_type=MESH)

        sends = [copy(i, k, me) for i in range(n) for k in range(1, 8)]
        for cp in sends:
            cp.start()
        for i in range(n):
            for k in range(1, 8):
                px, py, pc = peer(k)
                copy(i, k, 4 * px + 2 * py + pc).wait_recv()
        for cp in sends:
            cp.wait_send()
        if reduce:
            for i in range(n):
                acc = bufs[i][0]
                for d in range(1, 8):
                    acc = acc + bufs[i][d]
                o_refs[i][...] = acc

    vmem = pl.BlockSpec(memory_space=pltpu.VMEM)
    stacked = [(8,) + a.shape for a in arrs]
    return _pcall(
        body, name=name, in_specs=[vmem] * n, out_specs=[vmem] * n,
        out_shape=[_sds(a.shape if reduce else s, F32) for a, s in zip(arrs, stacked)],
        scratch_shapes=([pltpu.VMEM(s, F32) for s in stacked] if reduce else [])
        + [pltpu.SemaphoreType.DMA((7 * n,)), pltpu.SemaphoreType.DMA((7 * n,))],
    )(*arrs)


def _pad_rows(n, mult):
    return -(-n // mult) * mult


def _chip_major(g, b):
    return g.reshape(g.shape[0], 4, b).transpose(1, 0, 2)


def _rope_tables(tp):
    inv_freq = 1.0 / (ROPE_BASE ** (jnp.arange(0, QK_ROPE, 2, dtype=F32) / QK_ROPE))
    ang = jnp.arange(tp, dtype=F32)[:, None] * inv_freq[None, :]
    cos, sin = jnp.cos(ang), jnp.sin(ang)
    one = lambda n: jnp.ones((tp, n), F32)
    zero = lambda n: jnp.zeros((tp, n), F32)
    half = QK_ROPE // 2
    tail = Q_PAD - Q_ROPE_AT - QK_ROPE
    cq = jnp.concatenate([one(Q_ROPE_AT), cos, cos, one(tail)], axis=1)
    s1q = jnp.concatenate([zero(Q_ROPE_AT + half), sin, zero(tail)], axis=1)
    s2q = jnp.concatenate([zero(Q_ROPE_AT), -sin, zero(tail + half)], axis=1)
    ck = jnp.concatenate([cos, cos, zero(KV_PAD - QK_ROPE)], axis=1)
    s1k = jnp.concatenate([zero(half), sin, zero(KV_PAD - QK_ROPE)], axis=1)
    s2k = jnp.concatenate([-sin, zero(KV_PAD - half)], axis=1)
    fwd = (cq * (ATT_SCALE * LOG2E), s1q * (ATT_SCALE * LOG2E), s2q * (ATT_SCALE * LOG2E), ck, s1k, s2k)
    bwd = (cq * ATT_SCALE, -s1q * ATT_SCALE, -s2q * ATT_SCALE, ck, -s1k, -s2k)
    return fwd, bwd


def _pad_w_in(w):
    return jnp.concatenate([w[:, :COL_KR_END], jnp.zeros((w.shape[0], COL_GATES - COL_KR_END), w.dtype), w[:, COL_KR_END:]], axis=1)


def _pad_w_uq(w):
    w = w.reshape(Q_LORA, MLA_HEADS, QK_NOPE + QK_ROPE)
    z = lambda n: jnp.zeros((Q_LORA, MLA_HEADS, n), w.dtype)
    return jnp.concatenate([w[..., :QK_NOPE], z(Q_ROPE_AT - QK_NOPE), w[..., QK_NOPE:], z(Q_PAD - Q_ROPE_AT - QK_ROPE)],
                           axis=-1).reshape(Q_LORA, MLA_HEADS * Q_PAD)


def _unpad_w_uq(w):
    w = w.reshape(Q_LORA, MLA_HEADS, Q_PAD)
    return jnp.concatenate([w[..., :QK_NOPE], w[..., Q_ROPE_AT:Q_ROPE_AT + QK_ROPE]], axis=-1).reshape(
        Q_LORA, MLA_HEADS * (QK_NOPE + QK_ROPE))


def _pad_w_br_mla(w):
    w = w.reshape(MLA_HEADS, V_HEAD, D_MODEL)
    return jnp.concatenate([jnp.zeros_like(w), w], axis=1).reshape(MLA_HEADS * KV_PAD, D_MODEL)


def _unpad_w_br_mla(w):
    return w.reshape(MLA_HEADS, KV_PAD, D_MODEL)[:, V_HEAD:].reshape(MLA_HEADS * V_HEAD, D_MODEL)


def _riding(hooks, where, l, *args):
    make = hooks.get(where)
    ride = make(l, *args) if make else None
    return ride if ride else (None, lambda results: None)


def _layer_fwd(l, st, xprev, gp, bp, hb, w, tabs, hooks):
    ln_g, ln_b = w["ln_g"], w["ln_b"]
    lg = lambda k: ln_g[l, k][None]
    lb = lambda k: ln_b[l, k][None]
    s = {}
    s["x0"], s["gp0"], s["bp0"], s["hb0"] = xprev, gp, bp, hb
    side, got = _riding(hooks, "ffn1_fwd", l)
    s["g1"], s["u1"], s["a1"], *extras = ffn_up(hb, w["ffn1_w_up"][l], name="ffn_up", side=side)
    got(extras)
    s["xh1"], s["rs1"], s["hb1"] = down_ln(s["a1"], w["ffn1_w_down"][l], xprev, gp, bp, lg(0), lb(0), name="ffn_down_ln")
    s["p"] = mm_rows([(s["hb1"], w["mix_w_in"][l], False, 0)], D_IN_PAD, name="mix_in", tn=1024, out_dtype=BF)
    gq, gkv = w["q_norm_g"][l][None], w["kv_norm_g"][l][None]
    s["cqn"], s["ckvn"], s["q2"], s["kv"], s["kr"] = mla_prep(s["p"], gq, gkv, w["w_uq"][l], w["w_ukv"][l], tabs, name="mla_prep")
    side, got = _riding(hooks, "attn_fwd", l)
    s["o2"], s["lse"], *extras = attn_fwd(s["q2"], s["kv"], s["kr"], name="attn_fwd", side=side)
    got(extras)
    s["ycv"], s["conv"] = conv_fwd(s["p"], w["conv_w"][l], name="conv_fwd")
    s["bc"], s["bm"], s["mg"], s["xh2"], s["rs2"], s["hb2"] = merge_out_ln(
        s["ycv"], s["o2"], s["p"], w["mix_b_gate"][l], w["w_br_conv"][l], w["w_br_mla"][l], w["w_o"][l],
        s["xh1"], lg(0), lb(0), lg(1), lb(1), name="merge_out_ln")
    s["g2"], s["u2"], s["a2"] = ffn_up(s["hb2"], w["ffn2_w_up"][l], name="ffn_up")
    s["xh3"], s["rs3"], s["hb3"] = down_ln(s["a2"], w["ffn2_w_down"][l], s["xh2"], lg(1), lb(1), lg(2), lb(2), name="ffn_down_ln")
    st.append(s)
    return s["xh3"], lg(2), lb(2), s["hb3"]


def _ffn_bwd(which, l, g, hooks, dh, w_up, w_down, ln_gain, hb_in, gate, up, act, xh, rs):
    dzb, dgam, dbet, *loss_acc = ln_bwd(dh, xh, rs, ln_gain, branch_scale=0.5, name="ln_bwd")
    if loss_acc:
        g["loss"] = loss_acc[0]
    g[which + "_w_down"] = tn_mm(act, dzb, tm=D_FF // 2, name="dw_down", shard=("rows", D_FF // 4))
    side, got = _riding(hooks, which + "_down_bwd", l, g)
    dgate, dup, *extras = ffn_down_bwd(dzb, w_down, gate, up, name="ffn_down_bwd", side=side)
    got(extras)
    d_w = tn_mm(hb_in, dgate, tm=512, name="dw_up", shard=("cols", D_FF // 2), slot0=0)
    g[which + "_w_up"] = tn_mm(hb_in, dup, tm=512, name="dw_up", shard=("cols", D_FF // 2), slot0=2, dst=d_w)
    side, got = _riding(hooks, which + "_up_bwd", l, g)
    dh_in = mm_rows([(dgate, w_up, True, 0), (dup, w_up, True, 1)], D_MODEL, name="ffn_up_bwd", tn=512, addend=dzb, add_scale=2.0 * ALPHA,
                    side=side)
    if side is not None:
        dh_in, *extras = dh_in
        got(extras)
    return dh_in, dgam, dbet


def _layer_bwd(l, s, dh, w, tabs_bwd, hooks):
    ln_g = w["ln_g"]
    lg = lambda k: ln_g[l, k][None]
    g = {}
    dh, dg2, db2 = _ffn_bwd("ffn2", l, g, hooks, dh, w["ffn2_w_up"][l], w["ffn2_w_down"][l], lg(2), s["hb2"], s["g2"], s["u2"],
                            s["a2"], s["xh3"], s["rs3"])
    dzb, dg1, db1 = ln_bwd(dh, s["xh2"], s["rs2"], lg(1), branch_scale=1.0, name="ln_bwd")
    g["w_o"] = tn_mm(s["mg"], dzb, tm=1024, name="dw_o", shard=("rows", D_MODEL // 4))
    dbc, dbm, dp, dycv, do2, dl, g["mix_b_gate"] = merge_bwd(
        dzb, w["w_o"][l], s["bc"], s["bm"], s["p"], w["mix_b_gate"][l], w["w_br_conv"][l], w["w_br_mla"][l], s["o2"], name="merge_bwd")
    g["w_br_conv"] = tn_mm(s["ycv"], dbc, tm=512, name="dw_br_conv", shard=("cols", D_MODEL // 4))
    g["w_br_mla"] = _chip_major(_unpad_w_br_mla(tn_mm(s["o2"], dbm, tm=1024, name="dw_br_mla")), D_MODEL // 4)
    side, got = _riding(hooks, "attn_bwd", l, g)
    dq2, dkv, dkr, *extras = attn_bwd(s["q2"], s["kv"], s["kr"], do2, s["lse"], dl, name="attn_bwd", side=side)
    got(extras)
    gq, gkv = w["q_norm_g"][l][None], w["kv_norm_g"][l][None]
    dqb, dp, g["q_norm_g"], g["kv_norm_g"] = mla_prep_bwd(dq2, dkv, dkr, s["p"], gq, gkv, w["w_uq"][l], w["w_ukv"][l], tabs_bwd, dp,
                                                          name="mla_prep_bwd")
    g["w_uq"] = _chip_major(_unpad_w_uq(tn_mm(s["cqn"], dqb, tm=Q_LORA, name="dw_uq")), MLA_HEADS * (QK_NOPE + QK_ROPE) // 4)
    g["w_ukv"] = tn_mm(s["ckvn"], dkv, tm=KV_LORA, name="dw_ukv", shard=("cols", MLA_HEADS * KV_PAD // 4))
    dp, dw0, dw1, dw2 = conv_bwd(dycv, s["p"], s["conv"], w["conv_w"][l], dp, name="conv_bwd")
    g["conv_w"] = jnp.concatenate([dw0, dw1, dw2], axis=0)
    d_in = tn_mm(s["hb1"], dp, tm=512, name="dw_in")
    g["mix_w_in"] = _chip_major(jnp.concatenate([d_in[:, :COL_KR_END], d_in[:, COL_GATES:]], axis=1), D_IN // 4)
    dh = mm_rows([(dp, w["mix_w_in"][l], True, 0)], D_MODEL, name="mix_in_bwd", tn=512, addend=dzb, add_scale=ALPHA)
    dh, dg0, db0 = _ffn_bwd("ffn1", l, g, hooks, dh, w["ffn1_w_up"][l], w["ffn1_w_down"][l], lg(0), s["hb0"], s["g1"], s["u1"],
                            s["a1"], s["xh1"], s["rs1"])
    g["ln_g"] = jnp.concatenate([dg0, dg1, dg2], axis=0)
    g["ln_b"] = jnp.concatenate([db0, db1, db2], axis=0)
    return dh, g


BIG = ("ffn1_w_up", "ffn1_w_down", "mix_w_in", "w_uq", "w_ukv", "w_br_conv", "w_br_mla", "w_o", "ffn2_w_up", "ffn2_w_down")
BIG_AXIS = (2, 1, 2, 2, 2, 2, 2, 1, 2, 1)
FFN1_MATRICES = ("ffn1_w_up", "ffn1_w_down")
MIXER_MATRICES = ("mix_w_in", "w_uq", "w_ukv", "w_br_conv", "w_br_mla", "w_o")
FFN2_MATRICES = ("ffn2_w_up", "ffn2_w_down")
SMALL_SHARDED = ("meta_tokens", "mix_b_gate", "conv_w", "ln_g", "ln_b")
SMALL_REPLICATED = ("q_norm_g", "kv_norm_g")
WEIGHTS = ("meta_tokens", "ffn1_w_up", "ffn1_w_down", "mix_w_in", "mix_b_gate", "conv_w", "q_norm_g", "w_uq", "kv_norm_g", "w_ukv",
           "w_br_conv", "w_br_mla", "w_o", "ffn2_w_up", "ffn2_w_down", "ln_g", "ln_b")


def _view2d(a):
    return a.reshape(-1, a.shape[-1])


def _local_grads(x_row, target_row, w, hooks=None):
    hooks = hooks or {}
    seq = x_row.shape[0]
    t_real = N_META + seq
    tp = _pad_rows(t_real, TM)
    pad = tp - t_real
    h0 = jnp.concatenate([w["meta_tokens"], x_row, jnp.zeros((pad, D_MODEL), F32)], axis=0)
    target_p = jnp.concatenate([jnp.zeros((N_META, D_MODEL), F32), target_row, jnp.zeros((pad, D_MODEL), F32)], axis=0)
    tabs, tabs_bwd = _rope_tables(tp)
    ones = jnp.ones((1, D_MODEL), F32)
    zeros = jnp.zeros((1, D_MODEL), F32)
    saved = []
    cur = (h0, ones, zeros, h0.astype(BF))
    for l in range(DEPTH):
        cur = _layer_fwd(l, saved, *cur, w, tabs, hooks)
    dh = (cur[2], target_p, seq)
    grads = [None] * DEPTH
    for l in reversed(range(DEPTH)):
        dh, grads[l] = _layer_bwd(l, saved[l], dh, w, tabs_bwd, hooks)
        if "layer_bwd_done" in hooks:
            hooks["layer_bwd_done"](l, grads[l])
    return grads[DEPTH - 1].pop("loss"), dh[N_META:t_real], dh[:N_META], grads


def kernel(x, meta_tokens, ffn1_w_up, ffn1_w_down, mix_w_in, mix_b_gate, conv_w, q_norm_g, w_uq, kv_norm_g, w_ukv, w_br_conv, w_br_mla, w_o, ffn2_w_up, ffn2_w_down, ln_g, ln_b, loss_target, m_meta_tokens, m_ffn1_w_up, m_ffn1_w_down, m_mix_w_in, m_mix_b_gate, m_conv_w, m_q_norm_g, m_w_uq, m_kv_norm_g, m_w_ukv, m_w_br_conv, m_w_br_mla, m_w_o, m_ffn2_w_up, m_ffn2_w_down, m_ln_g, m_ln_b, v_meta_tokens, v_ffn1_w_up, v_ffn1_w_down, v_mix_w_in, v_mix_b_gate, v_conv_w, v_q_norm_g, v_w_uq, v_kv_norm_g, v_w_ukv, v_w_br_conv, v_w_br_mla, v_w_o, v_ffn2_w_up, v_ffn2_w_down, v_ln_g, v_ln_b):
    local = dict(meta_tokens=meta_tokens, ffn1_w_up=ffn1_w_up, ffn1_w_down=ffn1_w_down, mix_w_in=mix_w_in, mix_b_gate=mix_b_gate,
                 conv_w=conv_w, q_norm_g=q_norm_g, w_uq=w_uq, kv_norm_g=kv_norm_g, w_ukv=w_ukv, w_br_conv=w_br_conv,
                 w_br_mla=w_br_mla, w_o=w_o, ffn2_w_up=ffn2_w_up, ffn2_w_down=ffn2_w_down, ln_g=ln_g, ln_b=ln_b)
    mom_m = dict(zip(WEIGHTS, (m_meta_tokens, m_ffn1_w_up, m_ffn1_w_down, m_mix_w_in, m_mix_b_gate, m_conv_w, m_q_norm_g, m_w_uq,
                               m_kv_norm_g, m_w_ukv, m_w_br_conv, m_w_br_mla, m_w_o, m_ffn2_w_up, m_ffn2_w_down, m_ln_g, m_ln_b)))
    mom_v = dict(zip(WEIGHTS, (v_meta_tokens, v_ffn1_w_up, v_ffn1_w_down, v_mix_w_in, v_mix_b_gate, v_conv_w, v_q_norm_g, v_w_uq,
                               v_kv_norm_g, v_w_ukv, v_w_br_conv, v_w_br_mla, v_w_o, v_ffn2_w_up, v_ffn2_w_down, v_ln_g, v_ln_b)))
    xi, yi, ci = _place()
    chip = 2 * xi + yi

    shards = {n: local[n].astype(BF) for n in BIG}
    axis = dict(zip(BIG, BIG_AXIS))
    pad_layout = {"mix_w_in": _pad_w_in, "w_uq": _pad_w_uq, "w_br_mla": _pad_w_br_mla}
    w = {n: [None] * DEPTH for n in BIG}

    def fetch(keys):
        def install(gathered):
            for (n, l), g in zip(keys, gathered):
                if n in FFN1_MATRICES + FFN2_MATRICES:
                    w[n][l] = g
                    continue
                full = jnp.concatenate([g[j] for j in range(4)], axis=axis[n] - 1)
                w[n][l] = pad_layout[n](full) if n in pad_layout else full
        return gather_side([(shards[n], l) for n, l in keys]), install

    first, install_first = fetch([("ffn1_w_up", 0)])
    install_first(exchange_alone(first, name="gather_weights"))
    fetch_under = {("ffn1_fwd", 0): [("ffn1_w_down", 0)] + [(n, 0) for n in MIXER_MATRICES],
                   ("attn_fwd", 0): [(n, 0) for n in FFN2_MATRICES] + [(n, 1) for n in BIG]}
    hooks = {where: functools.partial(lambda l, where: fetch(fetch_under[where, l]) if (where, l) in fetch_under else None, where=where)
             for where in ("ffn1_fwd", "attn_fwd")}
    stacked = exchange_small([_view2d(local[n]) for n in SMALL_SHARDED], reduce=False, name="gather_small")
    for n, st in zip(SMALL_SHARDED, stacked):
        full = jnp.concatenate([st[2 * j] for j in range(4)], axis=-1)
        w[n] = full.reshape(local[n].shape[:-1] + (full.shape[-1],))
    for n in SMALL_REPLICATED:
        w[n] = local[n]

    c_idx = jnp.reshape(ci, (1,)).astype(jnp.int32)
    done, from_chips = {}, {}

    def send(keys, grad_of):
        glist = [grad_of[k] for k in keys]
        from_sibling = pair_exchange(glist, name="rs_pair_exchange")
        sums = [pair_add(a, s, c_idx, name="rs_pair_add") for a, s in zip(glist, from_sibling)]
        return scatter_side(sums), lambda results: from_chips.update(zip(keys, results))

    send_under = {"attn_bwd": FFN2_MATRICES + ("w_o", "w_br_conv", "w_br_mla"),
                  "ffn1_down_bwd": ("mix_w_in", "w_uq", "w_ukv", "ffn1_w_down"), "ffn1_up_bwd": ("ffn1_w_up",)}
    hooks["layer_bwd_done"] = lambda l, g: done.update({(n, l): g[n] for n in BIG})

    def send_hook(where):
        def hook(l, g):
            if l != 0:
                return None
            keys = [(n, 0) for n in send_under[where]] + ([(n, 1) for n in BIG] if where == "attn_bwd" else [])
            return send(keys, {**done, **{(n, 0): g[n] for n in send_under[where]}})
        return hook

    for where in send_under:
        hooks[where] = send_hook(where)

    loss_acc, grad_x, d_meta, grads = _local_grads(x[0], loss_target[0], w, hooks)
    grad_x = grad_x[None]
    keys = [(n, l) for n in BIG for l in range(DEPTH)]
    reduced = pair_gather([sum_chunks(from_chips[k], name="rs_sum") for k in keys], name="rs_pair_gather")
    reduced = {k: r.reshape(local[k[0]].shape[1:]) for k, r in zip(keys, reduced)}
    gshard = {n: jnp.stack([reduced[n, l] for l in range(DEPTH)]) for n in BIG}

    small_names = SMALL_SHARDED + SMALL_REPLICATED
    gsmall = {n: jnp.concatenate([grads[l][n] for l in range(DEPTH)], axis=0) for n in small_names if n != "meta_tokens"}
    gsmall["meta_tokens"] = d_meta
    small_red = exchange_small([gsmall[n] for n in small_names] + [loss_acc], reduce=True, name="reduce_small")
    loss = small_red[-1][0, 0]
    for n, full in zip(small_names, small_red[:-1]):
        if n in SMALL_SHARDED:
            sh = local[n].shape[-1]
            full = lax.dynamic_slice_in_dim(full, chip * sh, sh, axis=1)
        gshard[n] = full.reshape(local[n].shape)

    delta, new_m, new_v = {}, {}, {}
    for n in WEIGHTS:
        shape = local[n].shape
        d, nm, nv = adamw(_view2d(local[n]), _view2d(gshard[n]), _view2d(mom_m[n]), _view2d(mom_v[n]), name="adamw")
        delta[n], new_m[n], new_v[n] = d.reshape(shape), nm.reshape(shape), nv.reshape(shape)
    return (loss, grad_x, *[gshard[n] for n in WEIGHTS], *[delta[n] for n in WEIGHTS], *[new_m[n] for n in WEIGHTS],
            *[new_v[n] for n in WEIGHTS])
```

```python
import functools

import jax
import jax.numpy as jnp
from jax import lax
from jax.experimental import pallas as pl
from jax.experimental.pallas import tpu as pltpu

F32 = jnp.float32
BF = jnp.bfloat16
MESH = pl.DeviceIdType.MESH

D_MODEL = 1024
DEPTH = 2
N_META = 16
D_CONV = 512
MLA_HEADS = 8
QK_NOPE = 64
QK_ROPE = 32
V_HEAD = 64
Q_LORA = 256
KV_LORA = 128
ROPE_BASE = 10000.0
NEG_INF = -1e30
D_FF = 2816
ALPHA = (2 * DEPTH) ** 0.25
LN_EPS = 1e-5
RMS_EPS = 1e-6
ATT_SCALE = (QK_NOPE + QK_ROPE) ** -0.5
LOG2E = 1.4426950408889634
LN2 = 0.6931471805599453
D_IN = 4000
D_IN_PAD = 4096
COL_CQ = 3 * D_CONV
COL_CKV = COL_CQ + Q_LORA
COL_KR = COL_CKV + KV_LORA
COL_KR_END = COL_KR + QK_ROPE
KV_PAD = 128
COL_GATES = COL_KR + KV_PAD
Q_PAD = 256
Q_ROPE_AT = 128

ADAM_LR = 0.001
ADAM_B1 = 0.9
ADAM_B2 = 0.999
ADAM_EPS = 1e-08
ADAM_WD = 0.01
ADAM_STEP = 10

TM = 768
TMH = 384
LANES = 128
VMEM_LIMIT_BYTES = 50 * 1024 * 1024

NT = (((1,), (1,)), ((), ()))
TN = (((0,), (0,)), ((), ()))


def _pcall(body, **kw):
    return pl.pallas_call(body, **kw)


def _params(n_axes):
    return pltpu.CompilerParams(dimension_semantics=("arbitrary",) * n_axes, vmem_limit_bytes=VMEM_LIMIT_BYTES)


def _sds(shape, dtype):
    return jax.ShapeDtypeStruct(shape, dtype)


def mm_rows(pairs, n_out, *, name, tn=None, addend=None, add_scale=1.0, out_dtype=F32, side=None):
    tp = pairs[0][0].shape[0]
    tn = tn or n_out
    in_specs, args = [], []
    for a, b, nt, kb in pairs:
        k = a.shape[1]
        in_specs.append(pl.BlockSpec((TM, k), lambda i, j: (i, 0)))
        if nt and b.ndim == 3:
            in_specs.append(pl.BlockSpec((2, tn, k // 2), functools.partial(lambda i, j, kb: (kb, j, 0), kb=kb)))
        elif nt:
            in_specs.append(pl.BlockSpec((tn, k), functools.partial(lambda i, j, kb: (j, kb), kb=kb)))
        else:
            in_specs.append(pl.BlockSpec((k, tn), lambda i, j: (0, j)))
        args += [a, b]
    if addend is not None:
        in_specs.append(pl.BlockSpec((TM, tn), lambda i, j: (i, j)))
        args.append(addend)
    n_pairs = len(pairs)
    nts = [p[2] for p in pairs]

    def body(refs, out_refs, scratch):
        o_ref = out_refs[0]
        acc = None
        for p in range(n_pairs):
            a = refs[2 * p][...].astype(BF)
            b = refs[2 * p + 1][...]
            if b.ndim == 3:
                b = jnp.concatenate([b[0], b[1]], axis=1)
            d = lax.dot_general(a, b, NT if nts[p] else (((1,), (0,)), ((), ())), preferred_element_type=F32)
            acc = d if acc is None else acc + d
        if addend is not None:
            acc = acc + add_scale * refs[2 * n_pairs][...].astype(F32)
        o_ref[...] = acc.astype(o_ref.dtype)

    out = _side_call(
        body, side, name=name, grid=(tp // TM, n_out // tn), in_specs=in_specs,
        out_specs=[pl.BlockSpec((TM, tn), lambda i, j: (i, j))], out_shape=[_sds((tp, n_out), out_dtype)],
        scratch_shapes=[], args=args,
    )
    return out if side is not None else out[0]


def tn_mm(a, b, *, tm, name, out_dtype=BF, shard=None, slot0=0, dst=None):
    tp, m = a.shape
    n = b.shape[1]
    nk = tp // TM
    if shard is None:
        pieces, out_block, out_index, out_full = 1, (tm, n), (lambda i, k: (i, 0)), (m, n)
    elif shard[0] == "cols":
        pieces = n // shard[1]
        out_block, out_full = (pieces, tm, shard[1]), (4, m, shard[1])
        out_index = lambda i, k: (slot0 // pieces, i, 0)
    else:
        pieces = tm // shard[1]
        out_block, out_full = (pieces, shard[1], n), (4, m // 4, n)
        out_index = lambda i, k: (i, 0, 0)

    def body(a_ref, b_ref, *rest):
        o_ref, acc_ref = rest[-2], rest[-1]
        k = pl.program_id(1)

        @pl.when(k == 0)
        def _():
            acc_ref[...] = jnp.zeros_like(acc_ref)

        acc_ref[...] += lax.dot_general(a_ref[...].astype(BF), b_ref[...].astype(BF), TN, preferred_element_type=F32)

        @pl.when(k == nk - 1)
        def _():
            if shard is None:
                o_ref[...] = acc_ref[...].astype(o_ref.dtype)
            elif shard[0] == "cols":
                for j in range(pieces):
                    o_ref[j] = acc_ref[:, j * shard[1]:(j + 1) * shard[1]].astype(o_ref.dtype)
            else:
                for j in range(pieces):
                    o_ref[j] = acc_ref[j * shard[1]:(j + 1) * shard[1], :].astype(o_ref.dtype)

    in_specs = [pl.BlockSpec((TM, tm), lambda i, k: (k, i)), pl.BlockSpec((TM, n), lambda i, k: (k, 0))]
    args = [a, b]
    aliases = {}
    if dst is not None:
        in_specs.append(pl.BlockSpec(memory_space=pl.ANY))
        args.append(dst)
        aliases = {2: 0}
    return _pcall(
        body, name=name, grid=(m // tm, nk), in_specs=in_specs, out_specs=pl.BlockSpec(out_block, out_index),
        out_shape=_sds(out_full, out_dtype), input_output_aliases=aliases,
        scratch_shapes=[pltpu.VMEM((tm, n), F32)], compiler_params=_params(2),
    )(*args)


def _ln_store(z, g_ref, b_ref, xh_ref, rs_ref, hb_ref):
    mu = jnp.mean(z, axis=-1, keepdims=True)
    zc = z - mu
    var = jnp.mean(zc * zc, axis=-1, keepdims=True)
    rstd = lax.rsqrt(var + LN_EPS)
    xh = zc * rstd
    xh_ref[...] = xh
    rs_ref[...] = rstd
    hb_ref[...] = (xh * g_ref[...] + b_ref[...]).astype(BF)


def _ln_out(tp, tm=TM):
    specs = [pl.BlockSpec((tm, D_MODEL), lambda i: (i, 0)), pl.BlockSpec((tm, 1), lambda i: (i, 0)),
             pl.BlockSpec((tm, D_MODEL), lambda i: (i, 0))]
    shapes = [_sds((tp, D_MODEL), F32), _sds((tp, 1), F32), _sds((tp, D_MODEL), BF)]
    return specs, shapes


def _row_vec(n):
    return pl.BlockSpec((1, n), lambda i: (0, 0))


def ffn_up(hb, wup, *, name, side=None):
    tp = hb.shape[0]
    tn = D_FF // 2
    nj = D_FF // tn

    def body(in_refs, out_refs, scratch):
        h_ref, wg_ref, wu_ref = in_refs
        g_ref, u_ref, a_ref = out_refs
        h = h_ref[...]
        g = jnp.dot(h, wg_ref[0], preferred_element_type=F32)
        u = jnp.dot(h, wu_ref[0], preferred_element_type=F32)
        g_ref[...] = g.astype(BF)
        u_ref[...] = u.astype(BF)
        a_ref[...] = (g * jax.nn.sigmoid(g) * u).astype(BF)

    blk = pl.BlockSpec((TM, tn), lambda i, j: (i, j))
    return _side_call(
        body, side, name=name, grid=(tp // TM, nj),
        in_specs=[pl.BlockSpec((TM, D_MODEL), lambda i, j: (i, 0)), pl.BlockSpec((1, D_MODEL, tn), lambda i, j: (j, 0, 0)),
                  pl.BlockSpec((1, D_MODEL, tn), lambda i, j: (j + nj, 0, 0))],
        out_specs=[blk, blk, blk], out_shape=[_sds((tp, D_FF), BF)] * 3, scratch_shapes=[], args=(hb, wup, wup),
    )


def down_ln(a, wd, xprev, gp, bp, g, b, *, name):
    tp = a.shape[0]

    def body(a_ref, wd_ref, xp_ref, gp_ref, bp_ref, g_ref, b_ref, xh_ref, rs_ref, hb_ref):
        wd = jnp.concatenate([wd_ref[j] for j in range(4)], axis=0)
        f = jnp.dot(a_ref[...], wd, preferred_element_type=F32)
        hprev = xp_ref[...] * gp_ref[...] + bp_ref[...]
        _ln_store(ALPHA * hprev + 0.5 * f, g_ref, b_ref, xh_ref, rs_ref, hb_ref)

    out_specs, out_shape = _ln_out(tp)
    return _pcall(
        body, name=name, grid=(tp // TM,),
        in_specs=[pl.BlockSpec((TM, D_FF), lambda i: (i, 0)), pl.BlockSpec((4, D_FF // 4, D_MODEL), lambda i: (0, 0, 0)),
                  pl.BlockSpec((TM, D_MODEL), lambda i: (i, 0))] + [_row_vec(D_MODEL)] * 4,
        out_specs=out_specs, out_shape=out_shape, compiler_params=_params(1),
    )(a, wd, xprev, gp, bp, g, b)


def _rope(x, c, s1, s2, reps):
    n = x.shape[1]
    half = QK_ROPE // 2
    if reps > 1:
        c, s1, s2 = (jnp.tile(t, (1, reps)) for t in (c, s1, s2))
    return x * c + pltpu.roll(x, half, 1) * s1 + pltpu.roll(x, n - half, 1) * s2


def _rms(x, g):
    r = lax.rsqrt(jnp.mean(x * x, axis=-1, keepdims=True) + RMS_EPS)
    return x * r * g, r


def mla_prep(p, gq, gkv, wuq_p, wukv, tabs, *, name):
    tp = p.shape[0]
    nh = MLA_HEADS

    def body(cq_ref, ckv_ref, kr_ref, gq_ref, gkv_ref, wuq_ref, wukv_ref, cq_t, s1q_t, s2q_t, ck_t, s1k_t, s2k_t,
             cqn_ref, ckvn_ref, q2_ref, kv_ref, krr_ref):
        cqn, _ = _rms(cq_ref[...].astype(F32), gq_ref[...])
        ckvn, _ = _rms(ckv_ref[...].astype(F32), gkv_ref[...])
        cqn = cqn.astype(BF)
        ckvn = ckvn.astype(BF)
        cqn_ref[...] = cqn
        ckvn_ref[...] = ckvn
        q = jnp.dot(cqn, wuq_ref[...], preferred_element_type=F32)
        q2_ref[...] = _rope(q, cq_t[...], s1q_t[...], s2q_t[...], nh).astype(BF)
        kv_ref[...] = jnp.dot(ckvn, wukv_ref[...], preferred_element_type=F32).astype(BF)
        krr_ref[...] = _rope(kr_ref[...].astype(F32), ck_t[...], s1k_t[...], s2k_t[...], 1).astype(BF)

    def rows(n, col=0):
        return pl.BlockSpec((TMH, n), functools.partial(lambda i, col: (i, col), col=col))

    return _pcall(
        body, name=name, grid=(tp // TMH,),
        in_specs=[rows(Q_LORA, COL_CQ // Q_LORA), rows(KV_LORA, COL_CKV // KV_LORA), rows(LANES, COL_KR // LANES),
                  _row_vec(Q_LORA), _row_vec(KV_LORA),
                  pl.BlockSpec((Q_LORA, nh * Q_PAD), lambda i: (0, 0)), pl.BlockSpec((KV_LORA, nh * KV_PAD), lambda i: (0, 0)),
                  rows(Q_PAD), rows(Q_PAD), rows(Q_PAD), rows(LANES), rows(LANES), rows(LANES)],
        out_specs=[rows(Q_LORA), rows(KV_LORA), rows(nh * Q_PAD), rows(nh * KV_PAD), rows(LANES)],
        out_shape=[_sds((tp, Q_LORA), BF), _sds((tp, KV_LORA), BF), _sds((tp, nh * Q_PAD), BF),
                   _sds((tp, nh * KV_PAD), BF), _sds((tp, LANES), BF)],
        compiler_params=_params(1),
    )(p, p, p, gq, gkv, wuq_p, wukv, *tabs)


def _causal_mask(s):
    qpos = lax.broadcasted_iota(jnp.int32, (TM, TM), 0)
    kpos = lax.broadcasted_iota(jnp.int32, (TM, TM), 1)
    return jnp.where(kpos <= qpos, s, NEG_INF)


def _key_rows(k):
    return pl.ds(pl.multiple_of(k * TM, TM), TM)


def _pipelined_key_blocks(n, prefetch, process):
    prefetch(0, 0)

    def pair(j, carry):
        prefetch(2 * j + 1, 1)
        process(2 * j, 0, False)
        prefetch(2 * j + 2, 0)
        process(2 * j + 1, 1, False)
        return carry

    lax.fori_loop(0, n // 2, pair, 0)

    @pl.when(n % 2 == 1)
    def _():
        prefetch(n, 1)
        process(n - 1, 0, False)
        process(n, 1, True)

    @pl.when(n % 2 == 0)
    def _():
        process(n, 0, True)


def _side_call(body_main, side, *, name, grid, in_specs, out_specs, out_shape, scratch_shapes, args):
    n_in, n_out, n_scr = len(in_specs), len(out_specs), len(scratch_shapes)
    s_in, s_pre, n_sems, program = side if side is not None else ((), (), 0, None)
    a, b = len(s_in), len(s_pre)
    s_alias = [p for p in s_pre if not isinstance(p, jax.ShapeDtypeStruct)]
    assert len(s_alias) in (0, b)
    c = len(s_alias)

    def body(*refs):
        in_refs = refs[:n_in]
        out_refs = refs[n_in + a + c:n_in + a + c + n_out]
        scr = refs[n_in + a + c + b + n_out:n_in + a + c + b + n_out + n_scr]
        if side is not None:
            side_in = refs[n_in:n_in + a]
            side_out = refs[n_in + a + c + n_out:n_in + a + c + b + n_out]
            start, finish = program(side_in, side_out, refs[-2], refs[-1])

            @pl.when((pl.program_id(0) == 0) & (pl.program_id(1) == 0))
            def _():
                start()

        body_main(in_refs, out_refs, scr)
        if side is not None:
            @pl.when((pl.program_id(0) == grid[0] - 1) & (pl.program_id(1) == grid[1] - 1))
            def _():
                finish()

    sems = [pltpu.SemaphoreType.DMA((n_sems,))] * 2 if side is not None else []
    return _pcall(
        body, name=name, grid=grid, in_specs=list(in_specs) + [HBM_SPEC] * (a + c), out_specs=list(out_specs) + [HBM_SPEC] * b,
        out_shape=list(out_shape) + [_sds(p.shape, p.dtype) for p in s_pre],
        input_output_aliases={n_in + a + i: n_out + i for i in range(c)},
        scratch_shapes=list(scratch_shapes) + sems, compiler_params=_params(2),
    )(*args, *s_in, *s_alias)


def attn_fwd(q2, kv, kr, *, name, side=None):
    tp = q2.shape[0]
    nh = MLA_HEADS
    nb = tp // TM
    rep = TM // LANES

    def body(in_refs, out_refs, scratch):
        q_ref, kv_ref, kr_ref = in_refs
        o_ref, lse_ref = out_refs
        m_ref, l_ref, acc_ref, s0_ref, s1_ref, p_ref, alpha_ref = scratch
        qi = pl.program_id(1)
        s_refs = (s0_ref, s1_ref)
        m_ref[...] = jnp.full_like(m_ref, NEG_INF)
        l_ref[...] = jnp.zeros_like(l_ref)
        acc_ref[...] = jnp.zeros_like(acc_ref)

        def prefetch(k, slot):
            k2 = jnp.concatenate([kv_ref[_key_rows(k), :], kr_ref[_key_rows(k), :]], axis=1)
            s_refs[slot][...] = lax.dot_general(q_ref[...], k2, NT, preferred_element_type=F32)

        def process(k, slot, diagonal):
            for r in range(TM // LANES):
                rows = slice(r * LANES, (r + 1) * LANES)
                s = s_refs[slot][rows, :]
                if diagonal:
                    qpos = r * LANES + lax.broadcasted_iota(jnp.int32, (LANES, TM), 0)
                    s = jnp.where(lax.broadcasted_iota(jnp.int32, (LANES, TM), 1) <= qpos, s, NEG_INF)
                m_prev = m_ref[rows, :]
                m_new = jnp.maximum(m_prev, jnp.max(s, axis=1, keepdims=True))
                alpha = jnp.exp2(m_prev - m_new)
                p = jnp.exp2(s - jnp.tile(m_new, (1, rep)))
                lane_sums = p[:, 0:LANES]
                for t in range(1, rep):
                    lane_sums = lane_sums + p[:, t * LANES:(t + 1) * LANES]
                l_ref[rows, :] = alpha * l_ref[rows, :] + lane_sums
                p_ref[rows, :] = p.astype(BF)
                alpha_ref[rows, :] = alpha
                m_ref[rows, :] = m_new
            acc_ref[...] = alpha_ref[...] * acc_ref[...] + jnp.dot(p_ref[...], kv_ref[_key_rows(k), :], preferred_element_type=F32)

        _pipelined_key_blocks(qi, prefetch, process)
        l = jnp.sum(l_ref[...], axis=1, keepdims=True)
        o_ref[...] = (acc_ref[...] / l).astype(BF)
        lse_ref[...] = m_ref[...] + jnp.log2(l)

    return _side_call(
        body, side, name=name, grid=(nh, nb),
        in_specs=[pl.BlockSpec((TM, Q_PAD), lambda h, qi: (qi, h)), pl.BlockSpec((tp, KV_PAD), lambda h, qi: (0, h)),
                  pl.BlockSpec((tp, LANES), lambda h, qi: (0, 0))],
        out_specs=[pl.BlockSpec((TM, KV_PAD), lambda h, qi: (qi, h)), pl.BlockSpec((TM, LANES), lambda h, qi: (qi, h))],
        out_shape=[_sds((tp, nh * KV_PAD), BF), _sds((tp, nh * LANES), F32)],
        scratch_shapes=[pltpu.VMEM((TM, LANES), F32)] * 3 + [pltpu.VMEM((TM, TM), F32)] * 2
        + [pltpu.VMEM((TM, TM), BF), pltpu.VMEM((TM, LANES), F32)], args=(q2, kv, kr),
    )


def conv_fwd(p, w, *, name):
    tp = p.shape[0]

    def body(b_ref, c_ref, h_ref, w_ref, y_ref, cv_ref, ebuf):
        i = pl.program_id(0)

        @pl.when(i == 0)
        def _():
            ebuf[0:8, :] = jnp.zeros((8, D_CONV), F32)

        e = c_ref[...].astype(F32) * h_ref[...].astype(F32)
        ebuf[8:8 + TM, :] = e
        w_all = w_ref[...]
        conv = w_all[0:1] * ebuf[pl.ds(6, TM), :] + w_all[1:2] * ebuf[pl.ds(7, TM), :] + w_all[2:3] * e
        cv_ref[...] = conv.astype(BF)
        y_ref[...] = (b_ref[...].astype(F32) * conv).astype(BF)
        ebuf[0:8, :] = ebuf[TM:TM + 8, :]

    def col(j):
        return pl.BlockSpec((TM, D_CONV), functools.partial(lambda i, j: (i, j), j=j))

    return _pcall(
        body, name=name, grid=(tp // TM,),
        in_specs=[col(0), col(1), col(2), pl.BlockSpec((3, D_CONV), lambda i: (0, 0))],
        out_specs=[col(0), col(0)], out_shape=[_sds((tp, D_CONV), BF)] * 2,
        scratch_shapes=[pltpu.VMEM((TM + 8, D_CONV), F32)], compiler_params=_params(1),
    )(p, p, p, w)


def merge_out_ln(ycv, o2, p, bg, wbc, wbm_p, wo, xprev, gp, bp, g, b, *, name):
    tp = ycv.shape[0]

    def body(y_ref, o_ref, gc_ref, gm_ref, bg_ref, wbc_ref, wbm_ref, wo_ref, xp_ref, gp_ref, bp_ref, g_ref, b_ref,
             bc_ref, bm_ref, mg_ref, xh_ref, rs_ref, hb_ref):
        bc = jnp.dot(y_ref[...], wbc_ref[...], preferred_element_type=F32)
        bm = jnp.dot(o_ref[...], wbm_ref[...], preferred_element_type=F32)
        bgv = bg_ref[...]
        mg = (jax.nn.sigmoid(gc_ref[...].astype(F32) + bgv[0:1]) * bc
              + jax.nn.sigmoid(gm_ref[...].astype(F32) + bgv[1:2]) * bm)
        mgb = mg.astype(BF)
        bc_ref[...] = bc.astype(BF)
        bm_ref[...] = bm.astype(BF)
        mg_ref[...] = mgb
        mix = jnp.dot(mgb, wo_ref[...], preferred_element_type=F32)
        hprev = xp_ref[...] * gp_ref[...] + bp_ref[...]
        _ln_store(ALPHA * hprev + mix, g_ref, b_ref, xh_ref, rs_ref, hb_ref)

    def rows(n, col=0):
        return pl.BlockSpec((TMH, n), functools.partial(lambda i, col: (i, col), col=col))

    def whole(r, c):
        return pl.BlockSpec((r, c), lambda i: (0, 0))

    ln_specs, ln_shapes = _ln_out(tp, TMH)
    return _pcall(
        body, name=name, grid=(tp // TMH,),
        in_specs=[rows(D_CONV), rows(MLA_HEADS * KV_PAD), rows(D_MODEL, COL_GATES // D_MODEL), rows(D_MODEL, COL_GATES // D_MODEL + 1),
                  whole(2, D_MODEL),
                  whole(D_CONV, D_MODEL), whole(MLA_HEADS * KV_PAD, D_MODEL), whole(D_MODEL, D_MODEL), rows(D_MODEL)]
        + [_row_vec(D_MODEL)] * 4,
        out_specs=[rows(D_MODEL)] * 3 + ln_specs, out_shape=[_sds((tp, D_MODEL), BF)] * 3 + ln_shapes,
        compiler_params=_params(1),
    )(ycv, o2, p, p, bg, wbc, wbm_p, wo, xprev, gp, bp, g, b)


def ln_bwd(dh, xh, rstd, g, *, branch_scale, name):
    tp = xh.shape[0]
    from_loss = isinstance(dh, tuple)

    def body(*refs):
        if from_loss:
            xh_ref, rs_ref, g_ref, b_ref, t_ref, dzb_ref, dg_ref, db_ref, loss_ref = refs
        else:
            dh_ref, xh_ref, rs_ref, g_ref, dzb_ref, dg_ref, db_ref = refs
        i = pl.program_id(0)

        @pl.when(i == 0)
        def _():
            dg_ref[...] = jnp.zeros_like(dg_ref)
            db_ref[...] = jnp.zeros_like(db_ref)
            if from_loss:
                loss_ref[...] = jnp.zeros_like(loss_ref)

        xhat = xh_ref[...]
        if from_loss:
            row = i * TM + lax.broadcasted_iota(jnp.int32, (TM, 1), 0)
            real = (row >= N_META) & (row < N_META + dh[2])
            diff = jnp.where(real, xhat * g_ref[...] + b_ref[...] - t_ref[...], 0.0)
            loss_ref[...] += 0.5 / D_MODEL * jnp.sum(diff * diff)
            dy = diff * (1.0 / D_MODEL)
        else:
            dy = dh_ref[...]
        dg_ref[...] += jnp.sum(dy * xhat, axis=0, keepdims=True)
        db_ref[...] += jnp.sum(dy, axis=0, keepdims=True)
        dxh = dy * g_ref[...]
        m1 = jnp.mean(dxh, axis=-1, keepdims=True)
        m2 = jnp.mean(dxh * xhat, axis=-1, keepdims=True)
        dz = rs_ref[...] * (dxh - m1 - xhat * m2)
        dzb_ref[...] = (branch_scale * dz).astype(BF)

    rows = pl.BlockSpec((TM, D_MODEL), lambda i: (i, 0))
    stat = pl.BlockSpec((TM, 1), lambda i: (i, 0))
    vec = _row_vec(D_MODEL)
    out_specs = [rows, vec, vec]
    out_shape = [_sds((tp, D_MODEL), BF), _sds((1, D_MODEL), F32), _sds((1, D_MODEL), F32)]
    if from_loss:
        in_specs, args = [rows, stat, vec, vec, rows], (xh, rstd, g, dh[0], dh[1])
        out_specs.append(pl.BlockSpec((8, LANES), lambda i: (0, 0)))
        out_shape.append(_sds((8, LANES), F32))
    else:
        in_specs, args = [rows, rows, stat, vec], (dh, xh, rstd, g)
    return _pcall(body, name=name, grid=(tp // TM,), in_specs=in_specs, out_specs=out_specs, out_shape=out_shape,
                  compiler_params=_params(1))(*args)


def ffn_down_bwd(dzb, wd, gate, up, *, name, side=None):
    tp = dzb.shape[0]
    tn = D_FF // 2

    def body(in_refs, out_refs, scratch):
        dz_ref, wd_ref, g_ref, u_ref = in_refs
        dg_ref, du_ref = out_refs
        wd = jnp.concatenate([wd_ref[0], wd_ref[1]], axis=0)
        da = lax.dot_general(dz_ref[...], wd, NT, preferred_element_type=F32)
        g = g_ref[...].astype(F32)
        u = u_ref[...].astype(F32)
        sg = jax.nn.sigmoid(g)
        dg_ref[...] = (da * u * sg * (1.0 + g * (1.0 - sg))).astype(BF)
        du_ref[...] = (da * g * sg).astype(BF)

    blk = pl.BlockSpec((TM, tn), lambda i, j: (i, j))
    return _side_call(
        body, side, name=name, grid=(tp // TM, D_FF // tn),
        in_specs=[pl.BlockSpec((TM, D_MODEL), lambda i, j: (i, 0)), pl.BlockSpec((2, tn // 2, D_MODEL), lambda i, j: (j, 0, 0)), blk, blk],
        out_specs=[blk, blk], out_shape=[_sds((tp, D_FF), BF)] * 2, scratch_shapes=[], args=(dzb, wd, gate, up),
    )


def merge_bwd(dzb, wo, bc, bm, p, bg, wbc, wbm_p, o2, *, name):
    tp = dzb.shape[0]
    nh = MLA_HEADS

    def body(dz_ref, wo_ref, bc_ref, bm_ref, gc_ref, gm_ref, bg_ref, wbc_ref, wbm_ref, o_ref,
             dbc_ref, dbm_ref, dgg_ref, dy_ref, do_ref, dl_ref, dbg_ref):
        i = pl.program_id(0)

        @pl.when(i == 0)
        def _():
            dbg_ref[...] = jnp.zeros_like(dbg_ref)

        dmg = lax.dot_general(dz_ref[...], wo_ref[...], NT, preferred_element_type=F32)
        bgv = bg_ref[...]
        sc = jax.nn.sigmoid(gc_ref[...].astype(F32) + bgv[0:1])
        sm = jax.nn.sigmoid(gm_ref[...].astype(F32) + bgv[1:2])
        dbc = (dmg * sc).astype(BF)
        dbm = (dmg * sm).astype(BF)
        dgc = dmg * bc_ref[...].astype(F32) * sc * (1.0 - sc)
        dgm = dmg * bm_ref[...].astype(F32) * sm * (1.0 - sm)
        dbc_ref[...] = dbc
        dbm_ref[...] = dbm
        dgg_ref[...] = jnp.concatenate([dgc, dgm], axis=1).astype(BF)
        dbg_ref[...] += jnp.concatenate([jnp.sum(dgc, axis=0, keepdims=True), jnp.sum(dgm, axis=0, keepdims=True)], axis=0)
        dy_ref[...] = lax.dot_general(dbc, wbc_ref[...], NT, preferred_element_type=F32)
        do = lax.dot_general(dbm, wbm_ref[...], NT, preferred_element_type=F32)
        do_ref[...] = do.astype(BF)
        prod = do * o_ref[...].astype(F32)
        parts = []
        for h in range(nh):
            d = jnp.sum(prod[:, h * KV_PAD:(h + 1) * KV_PAD], axis=1, keepdims=True)
            parts.append(jnp.broadcast_to(d, (TMH, LANES)))
        dl_ref[...] = jnp.concatenate(parts, axis=1)

    def rows(n, col=0):
        return pl.BlockSpec((TMH, n), functools.partial(lambda i, col: (i, col), col=col))

    def whole(r, c):
        return pl.BlockSpec((r, c), lambda i: (0, 0))

    return _pcall(
        body, name=name, grid=(tp // TMH,),
        in_specs=[rows(D_MODEL), whole(D_MODEL, D_MODEL), rows(D_MODEL), rows(D_MODEL), rows(D_MODEL, COL_GATES // D_MODEL),
                  rows(D_MODEL, COL_GATES // D_MODEL + 1),
                  whole(2, D_MODEL), whole(D_CONV, D_MODEL), whole(nh * KV_PAD, D_MODEL), rows(nh * KV_PAD)],
        out_specs=[rows(D_MODEL), rows(D_MODEL), rows(2 * D_MODEL, COL_GATES // (2 * D_MODEL)), rows(D_CONV), rows(nh * KV_PAD),
                   rows(nh * LANES),
                   whole(2, D_MODEL)],
        out_shape=[_sds((tp, D_MODEL), BF), _sds((tp, D_MODEL), BF), _sds((tp, D_IN_PAD), BF), _sds((tp, D_CONV), F32),
                   _sds((tp, nh * KV_PAD), BF), _sds((tp, nh * LANES), F32), _sds((2, D_MODEL), F32)],
        compiler_params=_params(1),
    )(dzb, wo, bc, bm, p, p, bg, wbc, wbm_p, o2)


def attn_bwd(q2, kv, kr, do2, lse, dl, *, name, side=None):
    tp = q2.shape[0]
    nh = MLA_HEADS
    nb = tp // TM
    rep = TM // LANES

    def body(in_refs, out_refs, scratch):
        q_ref, kv_ref, kr_ref, do_ref, lse_ref, dl_ref = in_refs
        dq_ref, dkv_ref, dkr_ref = out_refs
        dq_acc, s0_ref, s1_ref, dp0_ref, dp1_ref = scratch
        qi = pl.program_id(1)
        s_refs, dp_refs = (s0_ref, s1_ref), (dp0_ref, dp1_ref)

        @pl.when(qi == 0)
        def _():
            dkv_ref[...] = jnp.zeros_like(dkv_ref)

        @pl.when((qi == 0) & (pl.program_id(0) == 0))
        def _():
            dkr_ref[...] = jnp.zeros_like(dkr_ref)

        dq_acc[...] = jnp.zeros_like(dq_acc)

        def prefetch(k, slot):
            kvb = kv_ref[_key_rows(k), :]
            k2 = jnp.concatenate([kvb, kr_ref[_key_rows(k), :]], axis=1)
            s_refs[slot][...] = lax.dot_general(q_ref[...], k2, NT, preferred_element_type=F32)
            dp_refs[slot][...] = lax.dot_general(do_ref[...], kvb, NT, preferred_element_type=F32)

        def process(k, slot, diagonal):
            rows = _key_rows(k)
            s = s_refs[slot][...]
            if diagonal:
                s = _causal_mask(s)
            p = jnp.exp2(s - jnp.tile(lse_ref[...], (1, rep)))
            dsb = (p * (dp_refs[slot][...] - jnp.tile(dl_ref[...], (1, rep)))).astype(BF)
            dk2 = lax.dot_general(dsb, q_ref[...], TN, preferred_element_type=F32) * LN2
            dkv_ref[rows, :] += lax.dot_general(p.astype(BF), do_ref[...], TN, preferred_element_type=F32) + dk2[:, :KV_PAD]
            dkr_ref[rows, :] += dk2[:, KV_PAD:KV_PAD + LANES]
            k2 = jnp.concatenate([kv_ref[rows, :], kr_ref[rows, :]], axis=1)
            dq_acc[...] += jnp.dot(dsb, k2, preferred_element_type=F32)

        _pipelined_key_blocks(qi, prefetch, process)
        dq_ref[...] = dq_acc[...].astype(BF)

    def qrow(n):
        return pl.BlockSpec((TM, n), lambda h, qi: (qi, h))

    def head(n):
        return pl.BlockSpec((tp, n), lambda h, qi: (0, h))

    return _side_call(
        body, side, name=name, grid=(nh, nb),
        in_specs=[qrow(Q_PAD), head(KV_PAD), pl.BlockSpec((tp, LANES), lambda h, qi: (0, 0)), qrow(KV_PAD), qrow(LANES), qrow(LANES)],
        out_specs=[qrow(Q_PAD), head(KV_PAD), pl.BlockSpec((tp, LANES), lambda h, qi: (0, 0))],
        out_shape=[_sds((tp, nh * Q_PAD), BF), _sds((tp, nh * KV_PAD), F32), _sds((tp, LANES), F32)],
        scratch_shapes=[pltpu.VMEM((TM, Q_PAD), F32)] + [pltpu.VMEM((TM, TM), F32)] * 4, args=(q2, kv, kr, do2, lse, dl),
    )


def _rms_bwd(x, g, dy):
    r = lax.rsqrt(jnp.mean(x * x, axis=-1, keepdims=True) + RMS_EPS)
    gy = dy * g
    dx = r * gy - x * (r * r * r) * jnp.mean(x * gy, axis=-1, keepdims=True)
    return dx, jnp.sum(dy * x * r, axis=0, keepdims=True)


def mla_prep_bwd(dq2, dkv, dkr, p, gq, gkv, wuq_p, wukv, tabs_bwd, dp, *, name):
    tp = dq2.shape[0]
    nh = MLA_HEADS

    def body(dq_ref, dkv_ref, dkr_ref, cq_ref, ckv_ref, gq_ref, gkv_ref, wuq_ref, wukv_ref,
             cq_t, s1q_t, s2q_t, ck_t, s1k_t, s2k_t, dp_in_ref, dqb_ref, dsm_ref, dgq_ref, dgkv_ref):
        i = pl.program_id(0)

        @pl.when(i == 0)
        def _():
            dgq_ref[...] = jnp.zeros_like(dgq_ref)
            dgkv_ref[...] = jnp.zeros_like(dgkv_ref)

        dqb = _rope(dq_ref[...].astype(F32), cq_t[...], s1q_t[...], s2q_t[...], nh).astype(BF)
        dqb_ref[...] = dqb
        dcqn = lax.dot_general(dqb, wuq_ref[...], NT, preferred_element_type=F32)
        dcq, dgq = _rms_bwd(cq_ref[...].astype(F32), gq_ref[...], dcqn)
        dckvn = lax.dot_general(dkv_ref[...].astype(BF), wukv_ref[...], NT, preferred_element_type=F32)
        dckv, dgkv = _rms_bwd(ckv_ref[...].astype(F32), gkv_ref[...], dckvn)
        dkr = _rope(dkr_ref[...], ck_t[...], s1k_t[...], s2k_t[...], 1)
        dsm_ref[...] = jnp.concatenate([dcq, dckv, dkr], axis=1).astype(BF)
        dgq_ref[...] += dgq
        dgkv_ref[...] += dgkv

    def rows(n, col=0):
        return pl.BlockSpec((TMH, n), functools.partial(lambda i, col: (i, col), col=col))

    return _pcall(
        body, name=name, grid=(tp // TMH,),
        in_specs=[rows(nh * Q_PAD), rows(nh * KV_PAD), rows(LANES), rows(Q_LORA, COL_CQ // Q_LORA), rows(KV_LORA, COL_CKV // KV_LORA),
                  _row_vec(Q_LORA), _row_vec(KV_LORA),
                  pl.BlockSpec((Q_LORA, nh * Q_PAD), lambda i: (0, 0)), pl.BlockSpec((KV_LORA, nh * KV_PAD), lambda i: (0, 0)),
                  rows(Q_PAD), rows(Q_PAD), rows(Q_PAD), rows(LANES), rows(LANES), rows(LANES), pl.BlockSpec(memory_space=pl.ANY)],
        out_specs=[rows(nh * Q_PAD), rows(COL_GATES - COL_CQ, COL_CQ // (COL_GATES - COL_CQ)), _row_vec(Q_LORA),
                   _row_vec(KV_LORA)],
        out_shape=[_sds((tp, nh * Q_PAD), BF), _sds(dp.shape, dp.dtype), _sds((1, Q_LORA), F32), _sds((1, KV_LORA), F32)],
        input_output_aliases={15: 1}, compiler_params=_params(1),
    )(dq2, dkv, dkr, p, p, gq, gkv, wuq_p, wukv, *tabs_bwd, dp)


def conv_bwd(dy, p, conv, w, dp, *, name):
    tp = dy.shape[0]
    nb = tp // TM

    def body(dy_ref, b_ref, c_ref, h_ref, cv_ref, w_ref, dp_in_ref, dp_ref, dw0_ref, dw1_ref, dw2_ref, dbuf):
        i = pl.program_id(0)

        @pl.when(i == 0)
        def _():
            dbuf[TM:TM + 8, :] = jnp.zeros((8, D_CONV), F32)
            dw0_ref[...] = jnp.zeros_like(dw0_ref)
            dw1_ref[...] = jnp.zeros_like(dw1_ref)
            dw2_ref[...] = jnp.zeros_like(dw2_ref)

        dyv = dy_ref[...]
        c = c_ref[...].astype(F32)
        hh = h_ref[...].astype(F32)
        dconv = dyv * b_ref[...].astype(F32)
        dbuf[0:TM, :] = dconv
        d1 = dbuf[pl.ds(1, TM), :]
        d2 = dbuf[pl.ds(2, TM), :]
        w_all = w_ref[...]
        de = w_all[2:3] * dconv + w_all[1:2] * d1 + w_all[0:1] * d2
        e = c * hh
        dp_ref[...] = jnp.concatenate([dyv * cv_ref[...].astype(F32), de * hh, de * c], axis=1).astype(BF)
        dw0_ref[...] += jnp.sum(d2 * e, axis=0, keepdims=True)
        dw1_ref[...] += jnp.sum(d1 * e, axis=0, keepdims=True)
        dw2_ref[...] += jnp.sum(dconv * e, axis=0, keepdims=True)
        dbuf[TM:TM + 8, :] = dbuf[0:8, :]

    def col(j):
        return pl.BlockSpec((TM, D_CONV), functools.partial(lambda i, j: (nb - 1 - i, j), j=j))

    return _pcall(
        body, name=name, grid=(nb,),
        in_specs=[col(0), col(0), col(1), col(2), col(0), pl.BlockSpec((3, D_CONV), lambda i: (0, 0)),
                  pl.BlockSpec(memory_space=pl.ANY)],
        out_specs=[pl.BlockSpec((TM, 3 * D_CONV), lambda i: (nb - 1 - i, 0))] + [_row_vec(D_CONV)] * 3,
        out_shape=[_sds(dp.shape, dp.dtype)] + [_sds((1, D_CONV), F32)] * 3, input_output_aliases={6: 0},
        scratch_shapes=[pltpu.VMEM((TM + 8, D_CONV), F32)], compiler_params=_params(1),
    )(dy, p, p, p, conv, w, dp)


def adamw(w, g, m, v, *, name):
    r, c = w.shape
    tr = r
    for cand in (256, 128, 64, 32, 16, 8):
        if r % cand == 0 and r > cand:
            tr = cand
            break

    def body(w_ref, g_ref, m_ref, v_ref, d_ref, nm_ref, nv_ref):
        gv = g_ref[...]
        nm = ADAM_B1 * m_ref[...] + (1.0 - ADAM_B1) * gv
        nv = ADAM_B2 * v_ref[...] + (1.0 - ADAM_B2) * (gv * gv)
        m_hat = nm / (1.0 - ADAM_B1 ** ADAM_STEP)
        v_hat = nv / (1.0 - ADAM_B2 ** ADAM_STEP)
        d_ref[...] = -ADAM_LR * (m_hat / (jnp.sqrt(v_hat) + ADAM_EPS) + ADAM_WD * w_ref[...])
        nm_ref[...] = nm
        nv_ref[...] = nv

    blk = pl.BlockSpec((tr, c), lambda i: (i, 0))
    return _pcall(
        body, name=name, grid=(r // tr,), in_specs=[blk] * 4, out_specs=[blk] * 3,
        out_shape=[_sds((r, c), F32)] * 3, compiler_params=_params(1),
    )(w, g, m, v)


HBM_SPEC = pl.BlockSpec(memory_space=pltpu.HBM)


def _place():
    return lax.axis_index("x"), lax.axis_index("y"), lax.axis_index("c")


def _other_chips(x, y):
    return [(1 - x, y), (x, 1 - y), (1 - x, 1 - y)]


def _half(ref_or_shape_rows, c):
    return pl.ds(c * (ref_or_shape_rows // 2), ref_or_shape_rows // 2)


def gather_side(items):
    n = len(items)
    shards = [s for s, _ in items]
    layers = [l for _, l in items]

    def program(x_refs, o_refs, send_sems, recv_sems):
        x, y, c = _place()
        me = 2 * x + y
        chips = _other_chips(x, y)

        def copy(sem, src, dst, to):
            return pltpu.make_async_remote_copy(src_ref=src, dst_ref=dst, send_sem=send_sems.at[sem], recv_sem=recv_sems.at[sem],
                                                device_id=to, device_id_type=MESH)

        def src(i):
            return x_refs[i].at[layers[i], _half(x_refs[i].shape[1], c)]

        def dst(i, slot, cc):
            return o_refs[i].at[slot, _half(o_refs[i].shape[1], cc)]

        sends = [copy(6 * i + k, src(i), dst(i, me, c), (px, py, c)) for i in range(n) for k, (px, py) in enumerate(chips)]
        passed = [copy(6 * i + 3 + k, dst(i, 2 * px + py, c), dst(i, 2 * px + py, c), (x, y, 1 - c))
                  for k, (px, py) in enumerate(chips) for i in range(n)]

        def start():
            for cp in sends:
                cp.start()

        def finish():
            pos = 0
            for k, (px, py) in enumerate(chips):
                for i in range(n):
                    copy(6 * i + k, src(i), dst(i, 2 * px + py, c), (px, py, c)).wait_recv()
                    passed[pos].start()
                    pos += 1
            for k, (px, py) in enumerate(chips):
                for i in range(n):
                    copy(6 * i + 3 + k, dst(i, 2 * px + py, 1 - c), dst(i, 2 * px + py, 1 - c), (x, y, 1 - c)).wait_recv()
            for cp in sends + passed:
                cp.wait_send()

        return start, finish

    prefilled = [jnp.broadcast_to(s[l][None], (4,) + s.shape[1:]) for s, l in items]
    return shards, prefilled, 6 * n, program


def scatter_side(pss):
    n = len(pss)

    def program(p_refs, o_refs, send_sems, recv_sems):
        x, y, c = _place()
        me = 2 * x + y
        chips = _other_chips(x, y)

        def copy(i, k, j_src, j_dst, to):
            return pltpu.make_async_remote_copy(src_ref=p_refs[i].at[j_src], dst_ref=o_refs[i].at[j_dst],
                                                send_sem=send_sems.at[3 * i + k], recv_sem=recv_sems.at[3 * i + k],
                                                device_id=to, device_id_type=MESH)

        sends = [copy(i, k, 2 * px + py, me, (px, py, c)) for i in range(n) for k, (px, py) in enumerate(chips)]

        def start():
            for cp in sends:
                cp.start()

        def finish():
            for i in range(n):
                for k, (px, py) in enumerate(chips):
                    copy(i, k, me, 2 * px + py, (px, py, c)).wait_recv()
            for cp in sends:
                cp.wait_send()

        return start, finish

    xi, yi, _ = _place()
    own = jnp.arange(4)[:, None, None] == 2 * xi + yi
    prefilled = [jnp.where(own, p, jnp.zeros_like(p)) for p in pss]
    return list(pss), prefilled, 3 * n, program


def exchange_alone(side, *, name):
    inputs, prefilled, n_sems, program = side
    a, b = len(inputs), len(prefilled)

    def body(*refs):
        start, finish = program(refs[:a], refs[a + b:a + 2 * b], refs[-2], refs[-1])
        start()
        finish()

    return _pcall(
        body, name=name, in_specs=[HBM_SPEC] * (a + b), out_specs=[HBM_SPEC] * b, out_shape=[_sds(p.shape, p.dtype) for p in prefilled],
        input_output_aliases={a + i: i for i in range(b)}, scratch_shapes=[pltpu.SemaphoreType.DMA((n_sems,))] * 2,
    )(*inputs, *prefilled)


def pair_exchange(gs, *, name):
    n = len(gs)

    def body(*refs):
        g_refs, o_refs = refs[:n], refs[n:2 * n]
        send_sems, recv_sems = refs[2 * n:]
        x, y, c = _place()
        cps = [pltpu.make_async_remote_copy(src_ref=g_refs[i].at[:, _half(g_refs[i].shape[1], 1 - c)], dst_ref=o_refs[i],
                                            send_sem=send_sems.at[i], recv_sem=recv_sems.at[i], device_id=(x, y, 1 - c),
                                            device_id_type=MESH)
               for i in range(n)]
        for cp in cps:
            cp.start()
        for cp in cps:
            cp.wait()

    return _pcall(
        body, name=name, in_specs=[HBM_SPEC] * n, out_specs=[HBM_SPEC] * n,
        out_shape=[_sds((4, g.shape[1] // 2, g.shape[2]), g.dtype) for g in gs],
        scratch_shapes=[pltpu.SemaphoreType.DMA((n,)), pltpu.SemaphoreType.DMA((n,))],
    )(*gs)


def pair_exchange_side(gs):
    n = len(gs)

    def program(g_refs, o_refs, send_sems, recv_sems):
        x, y, c = _place()
        cps = [pltpu.make_async_remote_copy(src_ref=g_refs[i].at[:, _half(g_refs[i].shape[1], 1 - c)], dst_ref=o_refs[i],
                                            send_sem=send_sems.at[i], recv_sem=recv_sems.at[i], device_id=(x, y, 1 - c),
                                            device_id_type=MESH)
               for i in range(n)]

        def start():
            for cp in cps:
                cp.start()

        def finish():
            for cp in cps:
                cp.wait()

        return start, finish

    return list(gs), [_sds((4, g.shape[1] // 2, g.shape[2]), g.dtype) for g in gs], n, program


def _comm_rows(a, b, itemsize):
    return a // 2 if a * b * itemsize > (3 << 19) and a % 16 == 0 else a


def pair_add(g, s1, c_idx, *, name):
    n, a, b = g.shape
    ah = a // 2
    ta = _comm_rows(ah, b, 2)
    nblk = ah // ta

    def body(c_ref, g_ref, s_ref, o_ref):
        o_ref[...] = (g_ref[...].astype(F32) + s_ref[...].astype(F32)).astype(o_ref.dtype)

    grid_spec = pltpu.PrefetchScalarGridSpec(
        num_scalar_prefetch=1, grid=(n, nblk),
        in_specs=[pl.BlockSpec((1, ta, b), lambda j, i, c_ref: (j, c_ref[0] * nblk + i, 0)),
                  pl.BlockSpec((1, ta, b), lambda j, i, c_ref: (j, i, 0))],
        out_specs=pl.BlockSpec((1, ta, b), lambda j, i, c_ref: (j, i, 0)),
    )
    return _pcall(body, name=name, grid_spec=grid_spec, out_shape=_sds((n, ah, b), g.dtype), compiler_params=_params(2))(
        c_idx, g, s1)


def sum_chunks(s2, *, name):
    n, a, b = s2.shape
    ta = _comm_rows(a, b, 4)

    def body(s_ref, o_ref):
        acc = s_ref[0].astype(F32)
        for j in range(1, n):
            acc = acc + s_ref[j].astype(F32)
        o_ref[...] = acc

    return _pcall(
        body, name=name, grid=(a // ta,), in_specs=[pl.BlockSpec((n, ta, b), lambda i: (0, i, 0))],
        out_specs=pl.BlockSpec((ta, b), lambda i: (i, 0)), out_shape=_sds((a, b), F32), compiler_params=_params(1),
    )(s2)


def pair_gather(rcs, *, name):
    n = len(rcs)

    def body(*refs):
        r_refs, o_refs = refs[:n], refs[2 * n:3 * n]
        send_sems, recv_sems = refs[3 * n:]
        x, y, c = _place()

        def copy(i, half):
            return pltpu.make_async_remote_copy(src_ref=r_refs[i], dst_ref=o_refs[i].at[half], send_sem=send_sems.at[i],
                                                recv_sem=recv_sems.at[i], device_id=(x, y, 1 - c), device_id_type=MESH)

        sends = [copy(i, c) for i in range(n)]
        for cp in sends:
            cp.start()
        for i in range(n):
            copy(i, 1 - c).wait_recv()
        for cp in sends:
            cp.wait_send()

    prefilled = [jnp.broadcast_to(r[None], (2,) + r.shape) for r in rcs]
    return _pcall(
        body, name=name, in_specs=[HBM_SPEC] * (2 * n), out_specs=[HBM_SPEC] * n,
        out_shape=[_sds(p.shape, p.dtype) for p in prefilled], input_output_aliases={n + i: i for i in range(n)},
        scratch_shapes=[pltpu.SemaphoreType.DMA((n,)), pltpu.SemaphoreType.DMA((n,))],
    )(*rcs, *prefilled)


def exchange_small(arrs, *, reduce, name):
    n = len(arrs)

    def body(*refs):
        v_refs, o_refs = refs[:n], refs[n:2 * n]
        bufs = refs[2 * n:3 * n] if reduce else o_refs
        send_sems, recv_sems = refs[-2:]
        x, y, c = _place()
        me = 4 * x + 2 * y + c
        for i in range(n):
            bufs[i][me] = v_refs[i][...]

        def peer(k):
            dx, dy, dc = (k >> 2) & 1, (k >> 1) & 1, k & 1
            return (1 - x if dx else x, 1 - y if dy else y, 1 - c if dc else c)

        def copy(i, k, slot):
            return pltpu.make_async_remote_copy(src_ref=v_refs[i], dst_ref=bufs[i].at[slot], send_sem=send_sems.at[7 * i + k - 1],
                                                recv_sem=recv_sems.at[7 * i + k - 1], device_id=peer(k), device_id_type=MESH)

        sends = [copy(i, k, me) for i in range(n) for k in range(1, 8)]
        for cp in sends:
            cp.start()
        for i in range(n):
            for k in range(1, 8):
                px, py, pc = peer(k)
                copy(i, k, 4 * px + 2 * py + pc).wait_recv()
        for cp in sends:
            cp.wait_send()
        if reduce:
            for i in range(n):
                acc = bufs[i][0]
                for d in range(1, 8):
                    acc = acc + bufs[i][d]
                o_refs[i][...] = acc

    vmem = pl.BlockSpec(memory_space=pltpu.VMEM)
    stacked = [(8,) + a.shape for a in arrs]
    return _pcall(
        body, name=name, in_specs=[vmem] * n, out_specs=[vmem] * n,
        out_shape=[_sds(a.shape if reduce else s, F32) for a, s in zip(arrs, stacked)],
        scratch_shapes=([pltpu.VMEM(s, F32) for s in stacked] if reduce else [])
        + [pltpu.SemaphoreType.DMA((7 * n,)), pltpu.SemaphoreType.DMA((7 * n,))],
    )(*arrs)


def _pad_rows(n, mult):
    return -(-n // mult) * mult


def _chip_major(g, b):
    return g.reshape(g.shape[0], 4, b).transpose(1, 0, 2)


def _rope_tables(tp):
    inv_freq = 1.0 / (ROPE_BASE ** (jnp.arange(0, QK_ROPE, 2, dtype=F32) / QK_ROPE))
    ang = jnp.arange(tp, dtype=F32)[:, None] * inv_freq[None, :]
    cos, sin = jnp.cos(ang), jnp.sin(ang)
    one = lambda n: jnp.ones((tp, n), F32)
    zero = lambda n: jnp.zeros((tp, n), F32)
    half = QK_ROPE // 2
    tail = Q_PAD - Q_ROPE_AT - QK_ROPE
    cq = jnp.concatenate([one(Q_ROPE_AT), cos, cos, one(tail)], axis=1)
    s1q = jnp.concatenate([zero(Q_ROPE_AT + half), sin, zero(tail)], axis=1)
    s2q = jnp.concatenate([zero(Q_ROPE_AT), -sin, zero(tail + half)], axis=1)
    ck = jnp.concatenate([cos, cos, zero(KV_PAD - QK_ROPE)], axis=1)
    s1k = jnp.concatenate([zero(half), sin, zero(KV_PAD - QK_ROPE)], axis=1)
    s2k = jnp.concatenate([-sin, zero(KV_PAD - half)], axis=1)
    fwd = (cq * (ATT_SCALE * LOG2E), s1q * (ATT_SCALE * LOG2E), s2q * (ATT_SCALE * LOG2E), ck, s1k, s2k)
    bwd = (cq * ATT_SCALE, -s1q * ATT_SCALE, -s2q * ATT_SCALE, ck, -s1k, -s2k)
    return fwd, bwd


def _pad_w_in(w):
    return jnp.concatenate([w[:, :COL_KR_END], jnp.zeros((w.shape[0], COL_GATES - COL_KR_END), w.dtype), w[:, COL_KR_END:]], axis=1)


def _pad_w_uq(w):
    w = w.reshape(Q_LORA, MLA_HEADS, QK_NOPE + QK_ROPE)
    z = lambda n: jnp.zeros((Q_LORA, MLA_HEADS, n), w.dtype)
    return jnp.concatenate([w[..., :QK_NOPE], z(Q_ROPE_AT - QK_NOPE), w[..., QK_NOPE:], z(Q_PAD - Q_ROPE_AT - QK_ROPE)],
                           axis=-1).reshape(Q_LORA, MLA_HEADS * Q_PAD)


def _unpad_w_uq(w):
    w = w.reshape(Q_LORA, MLA_HEADS, Q_PAD)
    return jnp.concatenate([w[..., :QK_NOPE], w[..., Q_ROPE_AT:Q_ROPE_AT + QK_ROPE]], axis=-1).reshape(
        Q_LORA, MLA_HEADS * (QK_NOPE + QK_ROPE))


def _pad_w_br_mla(w):
    w = w.reshape(MLA_HEADS, V_HEAD, D_MODEL)
    return jnp.concatenate([jnp.zeros_like(w), w], axis=1).reshape(MLA_HEADS * KV_PAD, D_MODEL)


def _unpad_w_br_mla(w):
    return w.reshape(MLA_HEADS, KV_PAD, D_MODEL)[:, V_HEAD:].reshape(MLA_HEADS * V_HEAD, D_MODEL)


def _riding(hooks, where, l, *args):
    make = hooks.get(where)
    ride = make(l, *args) if make else None
    return ride if ride else (None, lambda results: None)


def _layer_fwd(l, st, xprev, gp, bp, hb, w, tabs, hooks):
    ln_g, ln_b = w["ln_g"], w["ln_b"]
    lg = lambda k: ln_g[l, k][None]
    lb = lambda k: ln_b[l, k][None]
    s = {}
    s["x0"], s["gp0"], s["bp0"], s["hb0"] = xprev, gp, bp, hb
    side, got = _riding(hooks, "ffn1_fwd", l)
    s["g1"], s["u1"], s["a1"], *extras = ffn_up(hb, w["ffn1_w_up"][l], name="ffn_up", side=side)
    got(extras)
    s["xh1"], s["rs1"], s["hb1"] = down_ln(s["a1"], w["ffn1_w_down"][l], xprev, gp, bp, lg(0), lb(0), name="ffn_down_ln")
    s["p"] = mm_rows([(s["hb1"], w["mix_w_in"][l], False, 0)], D_IN_PAD, name="mix_in", tn=1024, out_dtype=BF)
    gq, gkv = w["q_norm_g"][l][None], w["kv_norm_g"][l][None]
    s["cqn"], s["ckvn"], s["q2"], s["kv"], s["kr"] = mla_prep(s["p"], gq, gkv, w["w_uq"][l], w["w_ukv"][l], tabs, name="mla_prep")
    side, got = _riding(hooks, "attn_fwd", l)
    s["o2"], s["lse"], *extras = attn_fwd(s["q2"], s["kv"], s["kr"], name="attn_fwd", side=side)
    got(extras)
    s["ycv"], s["conv"] = conv_fwd(s["p"], w["conv_w"][l], name="conv_fwd")
    s["bc"], s["bm"], s["mg"], s["xh2"], s["rs2"], s["hb2"] = merge_out_ln(
        s["ycv"], s["o2"], s["p"], w["mix_b_gate"][l], w["w_br_conv"][l], w["w_br_mla"][l], w["w_o"][l],
        s["xh1"], lg(0), lb(0), lg(1), lb(1), name="merge_out_ln")
    s["g2"], s["u2"], s["a2"] = ffn_up(s["hb2"], w["ffn2_w_up"][l], name="ffn_up")
    s["xh3"], s["rs3"], s["hb3"] = down_ln(s["a2"], w["ffn2_w_down"][l], s["xh2"], lg(1), lb(1), lg(2), lb(2), name="ffn_down_ln")
    st.append(s)
    return s["xh3"], lg(2), lb(2), s["hb3"]


def _ffn_bwd(which, l, g, hooks, dh, w_up, w_down, ln_gain, hb_in, gate, up, act, xh, rs):
    dzb, dgam, dbet, *loss_acc = ln_bwd(dh, xh, rs, ln_gain, branch_scale=0.5, name="ln_bwd")
    if loss_acc:
        g["loss"] = loss_acc[0]
    g[which + "_w_down"] = tn_mm(act, dzb, tm=D_FF // 2, name="dw_down", shard=("rows", D_FF // 4))
    side, got = _riding(hooks, which + "_down_bwd", l, g)
    dgate, dup, *extras = ffn_down_bwd(dzb, w_down, gate, up, name="ffn_down_bwd", side=side)
    got(extras)
    d_w = tn_mm(hb_in, dgate, tm=512, name="dw_up", shard=("cols", D_FF // 2), slot0=0)
    g[which + "_w_up"] = tn_mm(hb_in, dup, tm=512, name="dw_up", shard=("cols", D_FF // 2), slot0=2, dst=d_w)
    side, got = _riding(hooks, which + "_up_bwd", l, g)
    dh_in = mm_rows([(dgate, w_up, True, 0), (dup, w_up, True, 1)], D_MODEL, name="ffn_up_bwd", tn=512, addend=dzb, add_scale=2.0 * ALPHA,
                    side=side)
    if side is not None:
        dh_in, *extras = dh_in
        got(extras)
    return dh_in, dgam, dbet


def _layer_bwd(l, s, dh, w, tabs_bwd, hooks):
    ln_g = w["ln_g"]
    lg = lambda k: ln_g[l, k][None]
    g = {}
    dh, dg2, db2 = _ffn_bwd("ffn2", l, g, hooks, dh, w["ffn2_w_up"][l], w["ffn2_w_down"][l], lg(2), s["hb2"], s["g2"], s["u2"],
                            s["a2"], s["xh3"], s["rs3"])
    dzb, dg1, db1 = ln_bwd(dh, s["xh2"], s["rs2"], lg(1), branch_scale=1.0, name="ln_bwd")
    g["w_o"] = tn_mm(s["mg"], dzb, tm=1024, name="dw_o", shard=("rows", D_MODEL // 4))
    dbc, dbm, dp, dycv, do2, dl, g["mix_b_gate"] = merge_bwd(
        dzb, w["w_o"][l], s["bc"], s["bm"], s["p"], w["mix_b_gate"][l], w["w_br_conv"][l], w["w_br_mla"][l], s["o2"], name="merge_bwd")
    g["w_br_conv"] = tn_mm(s["ycv"], dbc, tm=512, name="dw_br_conv", shard=("cols", D_MODEL // 4))
    g["w_br_mla"] = _chip_major(_unpad_w_br_mla(tn_mm(s["o2"], dbm, tm=1024, name="dw_br_mla")), D_MODEL // 4)
    side, got = _riding(hooks, "attn_bwd", l, g)
    dq2, dkv, dkr, *extras = attn_bwd(s["q2"], s["kv"], s["kr"], do2, s["lse"], dl, name="attn_bwd", side=side)
    got(extras)
    gq, gkv = w["q_norm_g"][l][None], w["kv_norm_g"][l][None]
    dqb, dp, g["q_norm_g"], g["kv_norm_g"] = mla_prep_bwd(dq2, dkv, dkr, s["p"], gq, gkv, w["w_uq"][l], w["w_ukv"][l], tabs_bwd, dp,
                                                          name="mla_prep_bwd")
    g["w_uq"] = _chip_major(_unpad_w_uq(tn_mm(s["cqn"], dqb, tm=Q_LORA, name="dw_uq")), MLA_HEADS * (QK_NOPE + QK_ROPE) // 4)
    g["w_ukv"] = tn_mm(s["ckvn"], dkv, tm=KV_LORA, name="dw_ukv", shard=("cols", MLA_HEADS * KV_PAD // 4))
    dp, dw0, dw1, dw2 = conv_bwd(dycv, s["p"], s["conv"], w["conv_w"][l], dp, name="conv_bwd")
    g["conv_w"] = jnp.concatenate([dw0, dw1, dw2], axis=0)
    d_in = tn_mm(s["hb1"], dp, tm=512, name="dw_in")
    g["mix_w_in"] = _chip_major(jnp.concatenate([d_in[:, :COL_KR_END], d_in[:, COL_GATES:]], axis=1), D_IN // 4)
    dh = mm_rows([(dp, w["mix_w_in"][l], True, 0)], D_MODEL, name="mix_in_bwd", tn=512, addend=dzb, add_scale=ALPHA)
    dh, dg0, db0 = _ffn_bwd("ffn1", l, g, hooks, dh, w["ffn1_w_up"][l], w["ffn1_w_down"][l], lg(0), s["hb0"], s["g1"], s["u1"],
                            s["a1"], s["xh1"], s["rs1"])
    g["ln_g"] = jnp.concatenate([dg0, dg1, dg2], axis=0)
    g["ln_b"] = jnp.concatenate([db0, db1, db2], axis=0)
    return dh, g


BIG = ("ffn1_w_up", "ffn1_w_down", "mix_w_in", "w_uq", "w_ukv", "w_br_conv", "w_br_mla", "w_o", "ffn2_w_up", "ffn2_w_down")
BIG_AXIS = (2, 1, 2, 2, 2, 2, 2, 1, 2, 1)
FFN1_MATRICES = ("ffn1_w_up", "ffn1_w_down")
MIXER_MATRICES = ("mix_w_in", "w_uq", "w_ukv", "w_br_conv", "w_br_mla", "w_o")
FFN2_MATRICES = ("ffn2_w_up", "ffn2_w_down")
SMALL_SHARDED = ("meta_tokens", "mix_b_gate", "conv_w", "ln_g", "ln_b")
SMALL_REPLICATED = ("q_norm_g", "kv_norm_g")
WEIGHTS = ("meta_tokens", "ffn1_w_up", "ffn1_w_down", "mix_w_in", "mix_b_gate", "conv_w", "q_norm_g", "w_uq", "kv_norm_g", "w_ukv",
           "w_br_conv", "w_br_mla", "w_o", "ffn2_w_up", "ffn2_w_down", "ln_g", "ln_b")


def _view2d(a):
    return a.reshape(-1, a.shape[-1])


def _local_grads(x_row, target_row, w, hooks=None):
    hooks = hooks or {}
    seq = x_row.shape[0]
    t_real = N_META + seq
    tp = _pad_rows(t_real, TM)
    pad = tp - t_real
    h0 = jnp.concatenate([w["meta_tokens"], x_row, jnp.zeros((pad, D_MODEL), F32)], axis=0)
    target_p = jnp.concatenate([jnp.zeros((N_META, D_MODEL), F32), target_row, jnp.zeros((pad, D_MODEL), F32)], axis=0)
    tabs, tabs_bwd = _rope_tables(tp)
    ones = jnp.ones((1, D_MODEL), F32)
    zeros = jnp.zeros((1, D_MODEL), F32)
    saved = []
    cur = (h0, ones, zeros, h0.astype(BF))
    for l in range(DEPTH):
        cur = _layer_fwd(l, saved, *cur, w, tabs, hooks)
    dh = (cur[2], target_p, seq)
    grads = [None] * DEPTH
    for l in reversed(range(DEPTH)):
        dh, grads[l] = _layer_bwd(l, saved[l], dh, w, tabs_bwd, hooks)
        if "layer_bwd_done" in hooks:
            hooks["layer_bwd_done"](l, grads[l])
    return grads[DEPTH - 1].pop("loss"), dh[N_META:t_real], dh[:N_META], grads


def kernel(x, meta_tokens, ffn1_w_up, ffn1_w_down, mix_w_in, mix_b_gate, conv_w, q_norm_g, w_uq, kv_norm_g, w_ukv, w_br_conv, w_br_mla, w_o, ffn2_w_up, ffn2_w_down, ln_g, ln_b, loss_target, m_meta_tokens, m_ffn1_w_up, m_ffn1_w_down, m_mix_w_in, m_mix_b_gate, m_conv_w, m_q_norm_g, m_w_uq, m_kv_norm_g, m_w_ukv, m_w_br_conv, m_w_br_mla, m_w_o, m_ffn2_w_up, m_ffn2_w_down, m_ln_g, m_ln_b, v_meta_tokens, v_ffn1_w_up, v_ffn1_w_down, v_mix_w_in, v_mix_b_gate, v_conv_w, v_q_norm_g, v_w_uq, v_kv_norm_g, v_w_ukv, v_w_br_conv, v_w_br_mla, v_w_o, v_ffn2_w_up, v_ffn2_w_down, v_ln_g, v_ln_b):
    local = dict(meta_tokens=meta_tokens, ffn1_w_up=ffn1_w_up, ffn1_w_down=ffn1_w_down, mix_w_in=mix_w_in, mix_b_gate=mix_b_gate,
                 conv_w=conv_w, q_norm_g=q_norm_g, w_uq=w_uq, kv_norm_g=kv_norm_g, w_ukv=w_ukv, w_br_conv=w_br_conv,
                 w_br_mla=w_br_mla, w_o=w_o, ffn2_w_up=ffn2_w_up, ffn2_w_down=ffn2_w_down, ln_g=ln_g, ln_b=ln_b)
    mom_m = dict(zip(WEIGHTS, (m_meta_tokens, m_ffn1_w_up, m_ffn1_w_down, m_mix_w_in, m_mix_b_gate, m_conv_w, m_q_norm_g, m_w_uq,
                               m_kv_norm_g, m_w_ukv, m_w_br_conv, m_w_br_mla, m_w_o, m_ffn2_w_up, m_ffn2_w_down, m_ln_g, m_ln_b)))
    mom_v = dict(zip(WEIGHTS, (v_meta_tokens, v_ffn1_w_up, v_ffn1_w_down, v_mix_w_in, v_mix_b_gate, v_conv_w, v_q_norm_g, v_w_uq,
                               v_kv_norm_g, v_w_ukv, v_w_br_conv, v_w_br_mla, v_w_o, v_ffn2_w_up, v_ffn2_w_down, v_ln_g, v_ln_b)))
    xi, yi, ci = _place()
    chip = 2 * xi + yi

    shards = {n: local[n].astype(BF) for n in BIG}
    axis = dict(zip(BIG, BIG_AXIS))
    pad_layout = {"mix_w_in": _pad_w_in, "w_uq": _pad_w_uq, "w_br_mla": _pad_w_br_mla}
    w = {n: [None] * DEPTH for n in BIG}

    def fetch(keys):
        def install(gathered):
            for (n, l), g in zip(keys, gathered):
                if n in FFN1_MATRICES + FFN2_MATRICES:
                    w[n][l] = g
                    continue
                full = jnp.concatenate([g[j] for j in range(4)], axis=axis[n] - 1)
                w[n][l] = pad_layout[n](full) if n in pad_layout else full
        return gather_side([(shards[n], l) for n, l in keys]), install

    first, install_first = fetch([("ffn1_w_up", 0)])
    install_first(exchange_alone(first, name="gather_weights"))
    fetch_under = {("ffn1_fwd", 0): [("ffn1_w_down", 0)] + [(n, 0) for n in MIXER_MATRICES],
                   ("attn_fwd", 0): [(n, 0) for n in FFN2_MATRICES] + [(n, 1) for n in BIG]}
    hooks = {where: functools.partial(lambda l, where: fetch(fetch_under[where, l]) if (where, l) in fetch_under else None, where=where)
             for where in ("ffn1_fwd", "attn_fwd")}
    stacked = exchange_small([_view2d(local[n]) for n in SMALL_SHARDED], reduce=False, name="gather_small")
    for n, st in zip(SMALL_SHARDED, stacked):
        full = jnp.concatenate([st[2 * j] for j in range(4)], axis=-1)
        w[n] = full.reshape(local[n].shape[:-1] + (full.shape[-1],))
    for n in SMALL_REPLICATED:
        w[n] = local[n]

    c_idx = jnp.reshape(ci, (1,)).astype(jnp.int32)
    done, from_sibling, from_chips = {}, {}, {}

    def send(keys, grad_of):
        waiting = [k for k in keys if k not in from_sibling]
        from_sibling.update(zip(waiting, pair_exchange([grad_of[k] for k in waiting], name="rs_pair_exchange")))
        sums = [pair_add(grad_of[k], from_sibling[k], c_idx, name="rs_pair_add") for k in keys]
        return scatter_side(sums), lambda results: from_chips.update(zip(keys, results))

    layer1 = [(n, 1) for n in BIG]
    hooks["ffn2_down_bwd"] = lambda l, g: (pair_exchange_side([done[k] for k in layer1]),
                                           lambda results: from_sibling.update(zip(layer1, results))) if l == 0 else None

    send_under = {"attn_bwd": FFN2_MATRICES + ("w_o", "w_br_conv", "w_br_mla"),
                  "ffn1_down_bwd": ("mix_w_in", "w_uq", "w_ukv", "ffn1_w_down"), "ffn1_up_bwd": ("ffn1_w_up",)}
    hooks["layer_bwd_done"] = lambda l, g: done.update({(n, l): g[n] for n in BIG})

    def send_hook(where):
        def hook(l, g):
            if l != 0:
                return None
            keys = [(n, 0) for n in send_under[where]] + ([(n, 1) for n in BIG] if where == "attn_bwd" else [])
            return send(keys, {**done, **{(n, 0): g[n] for n in send_under[where]}})
        return hook

    for where in send_under:
        hooks[where] = send_hook(where)

    loss_acc, grad_x, d_meta, grads = _local_grads(x[0], loss_target[0], w, hooks)
    grad_x = grad_x[None]
    keys = [(n, l) for n in BIG for l in range(DEPTH)]
    reduced = pair_gather([sum_chunks(from_chips[k], name="rs_sum") for k in keys], name="rs_pair_gather")
    reduced = {k: r.reshape(local[k[0]].shape[1:]) for k, r in zip(keys, reduced)}
    gshard = {n: jnp.stack([reduced[n, l] for l in range(DEPTH)]) for n in BIG}

    small_names = SMALL_SHARDED + SMALL_REPLICATED
    gsmall = {n: jnp.concatenate([grads[l][n] for l in range(DEPTH)], axis=0) for n in small_names if n != "meta_tokens"}
    gsmall["meta_tokens"] = d_meta
    small_red = exchange_small([gsmall[n] for n in small_names] + [loss_acc], reduce=True, name="reduce_small")
    loss = small_red[-1][0, 0]
    for n, full in zip(small_names, small_red[:-1]):
        if n in SMALL_SHARDED:
            sh = local[n].shape[-1]
            full = lax.dynamic_slice_in_dim(full, chip * sh, sh, axis=1)
        gshard[n] = full.reshape(local[n].shape)

    delta, new_m, new_v = {}, {}, {}
    for n in WEIGHTS:
        shape = local[n].shape
        d, nm, nv = adamw(_view2d(local[n]), _view2d(gshard[n]), _view2d(mom_m[n]), _view2d(mom_v[n]), name="adamw")
        delta[n], new_m[n], new_v[n] = d.reshape(shape), nm.reshape(shape), nv.reshape(shape)
    return (loss, grad_x, *[gshard[n] for n in WEIGHTS], *[delta[n] for n in WEIGHTS], *[new_m[n] for n in WEIGHTS],
            *[new_v[n] for n in WEIGHTS])
```

```python
import functools

import jax
import jax.numpy as jnp
from jax import lax
from jax.experimental import pallas as pl
from jax.experimental.pallas import tpu as pltpu

F32 = jnp.float32
BF = jnp.bfloat16
MESH = pl.DeviceIdType.MESH

D_MODEL = 1024
DEPTH = 2
N_META = 16
D_CONV = 512
MLA_HEADS = 8
QK_NOPE = 64
QK_ROPE = 32
V_HEAD = 64
Q_LORA = 256
KV_LORA = 128
ROPE_BASE = 10000.0
NEG_INF = -1e30
D_FF = 2816
ALPHA = (2 * DEPTH) ** 0.25
LN_EPS = 1e-5
RMS_EPS = 1e-6
ATT_SCALE = (QK_NOPE + QK_ROPE) ** -0.5
LOG2E = 1.4426950408889634
LN2 = 0.6931471805599453
D_IN = 4000
D_IN_PAD = 4096
COL_CQ = 3 * D_CONV
COL_CKV = COL_CQ + Q_LORA
COL_KR = COL_CKV + KV_LORA
COL_KR_END = COL_KR + QK_ROPE
KV_PAD = 128
COL_GATES = COL_KR + KV_PAD
Q_PAD = 256
Q_ROPE_AT = 128

ADAM_LR = 0.001
ADAM_B1 = 0.9
ADAM_B2 = 0.999
ADAM_EPS = 1e-08
ADAM_WD = 0.01
ADAM_STEP = 10

TM = 768
TMH = 384
LANES = 128
VMEM_LIMIT_BYTES = 50 * 1024 * 1024

NT = (((1,), (1,)), ((), ()))
TN = (((0,), (0,)), ((), ()))


def _pcall(body, **kw):
    return pl.pallas_call(body, **kw)


def _params(n_axes):
    return pltpu.CompilerParams(dimension_semantics=("arbitrary",) * n_axes, vmem_limit_bytes=VMEM_LIMIT_BYTES)


def _sds(shape, dtype):
    return jax.ShapeDtypeStruct(shape, dtype)


def mm_rows(pairs, n_out, *, name, tn=None, addend=None, add_scale=1.0, out_dtype=F32, side=None):
    tp = pairs[0][0].shape[0]
    tn = tn or n_out
    in_specs, args = [], []
    for a, b, nt, kb in pairs:
        k = a.shape[1]
        in_specs.append(pl.BlockSpec((TM, k), lambda i, j: (i, 0)))
        if nt and b.ndim == 3:
            in_specs.append(pl.BlockSpec((2, tn, k // 2), functools.partial(lambda i, j, kb: (kb, j, 0), kb=kb)))
        elif nt:
            in_specs.append(pl.BlockSpec((tn, k), functools.partial(lambda i, j, kb: (j, kb), kb=kb)))
        else:
            in_specs.append(pl.BlockSpec((k, tn), lambda i, j: (0, j)))
        args += [a, b]
    if addend is not None:
        in_specs.append(pl.BlockSpec((TM, tn), lambda i, j: (i, j)))
        args.append(addend)
    n_pairs = len(pairs)
    nts = [p[2] for p in pairs]

    def body(refs, out_refs, scratch):
        o_ref = out_refs[0]
        acc = None
        for p in range(n_pairs):
            a = refs[2 * p][...].astype(BF)
            b = refs[2 * p + 1][...]
            if b.ndim == 3:
                b = jnp.concatenate([b[0], b[1]], axis=1)
            d = lax.dot_general(a, b, NT if nts[p] else (((1,), (0,)), ((), ())), preferred_element_type=F32)
            acc = d if acc is None else acc + d
        if addend is not None:
            acc = acc + add_scale * refs[2 * n_pairs][...].astype(F32)
        o_ref[...] = acc.astype(o_ref.dtype)

    out = _side_call(
        body, side, name=name, grid=(tp // TM, n_out // tn), in_specs=in_specs,
        out_specs=[pl.BlockSpec((TM, tn), lambda i, j: (i, j))], out_shape=[_sds((tp, n_out), out_dtype)],
        scratch_shapes=[], args=args,
    )
    return out if side is not None else out[0]


def tn_mm(a, b, *, tm, name, out_dtype=BF, shard=None, slot0=0, dst=None):
    tp, m = a.shape
    n = b.shape[1]
    nk = tp // TM
    if shard is None:
        pieces, out_block, out_index, out_full = 1, (tm, n), (lambda i, k: (i, 0)), (m, n)
    elif shard[0] == "cols":
        pieces = n // shard[1]
        out_block, out_full = (pieces, tm, shard[1]), (4, m, shard[1])
        out_index = lambda i, k: (slot0 // pieces, i, 0)
    else:
        pieces = tm // shard[1]
        out_block, out_full = (pieces, shard[1], n), (4, m // 4, n)
        out_index = lambda i, k: (i, 0, 0)

    def body(a_ref, b_ref, *rest):
        o_ref, acc_ref = rest[-2], rest[-1]
        k = pl.program_id(1)

        @pl.when(k == 0)
        def _():
            acc_ref[...] = jnp.zeros_like(acc_ref)

        acc_ref[...] += lax.dot_general(a_ref[...].astype(BF), b_ref[...].astype(BF), TN, preferred_element_type=F32)

        @pl.when(k == nk - 1)
        def _():
            if shard is None:
                o_ref[...] = acc_ref[...].astype(o_ref.dtype)
            elif shard[0] == "cols":
                for j in range(pieces):
                    o_ref[j] = acc_ref[:, j * shard[1]:(j + 1) * shard[1]].astype(o_ref.dtype)
            else:
                for j in range(pieces):
                    o_ref[j] = acc_ref[j * shard[1]:(j + 1) * shard[1], :].astype(o_ref.dtype)

    in_specs = [pl.BlockSpec((TM, tm), lambda i, k: (k, i)), pl.BlockSpec((TM, n), lambda i, k: (k, 0))]
    args = [a, b]
    aliases = {}
    if dst is not None:
        in_specs.append(pl.BlockSpec(memory_space=pl.ANY))
        args.append(dst)
        aliases = {2: 0}
    return _pcall(
        body, name=name, grid=(m // tm, nk), in_specs=in_specs, out_specs=pl.BlockSpec(out_block, out_index),
        out_shape=_sds(out_full, out_dtype), input_output_aliases=aliases,
        scratch_shapes=[pltpu.VMEM((tm, n), F32)], compiler_params=_params(2),
    )(*args)


def _ln_store(z, g_ref, b_ref, xh_ref, rs_ref, hb_ref):
    mu = jnp.mean(z, axis=-1, keepdims=True)
    zc = z - mu
    var = jnp.mean(zc * zc, axis=-1, keepdims=True)
    rstd = lax.rsqrt(var + LN_EPS)
    xh = zc * rstd
    xh_ref[...] = xh
    rs_ref[...] = rstd
    hb_ref[...] = (xh * g_ref[...] + b_ref[...]).astype(BF)


def _ln_out(tp, tm=TM):
    specs = [pl.BlockSpec((tm, D_MODEL), lambda i: (i, 0)), pl.BlockSpec((tm, 1), lambda i: (i, 0)),
             pl.BlockSpec((tm, D_MODEL), lambda i: (i, 0))]
    shapes = [_sds((tp, D_MODEL), F32), _sds((tp, 1), F32), _sds((tp, D_MODEL), BF)]
    return specs, shapes


def _row_vec(n):
    return pl.BlockSpec((1, n), lambda i: (0, 0))


def ffn_up(hb, wup, *, name, side=None):
    tp = hb.shape[0]
    tn = D_FF // 2
    nj = D_FF // tn

    def body(in_refs, out_refs, scratch):
        h_ref, wg_ref, wu_ref = in_refs
        g_ref, u_ref, a_ref = out_refs
        h = h_ref[...]
        g = jnp.dot(h, wg_ref[0], preferred_element_type=F32)
        u = jnp.dot(h, wu_ref[0], preferred_element_type=F32)
        g_ref[...] = g.astype(BF)
        u_ref[...] = u.astype(BF)
        a_ref[...] = (g * jax.nn.sigmoid(g) * u).astype(BF)

    blk = pl.BlockSpec((TM, tn), lambda i, j: (i, j))
    return _side_call(
        body, side, name=name, grid=(tp // TM, nj),
        in_specs=[pl.BlockSpec((TM, D_MODEL), lambda i, j: (i, 0)), pl.BlockSpec((1, D_MODEL, tn), lambda i, j: (j, 0, 0)),
                  pl.BlockSpec((1, D_MODEL, tn), lambda i, j: (j + nj, 0, 0))],
        out_specs=[blk, blk, blk], out_shape=[_sds((tp, D_FF), BF)] * 3, scratch_shapes=[], args=(hb, wup, wup),
    )


def down_ln(a, wd, xprev, gp, bp, g, b, *, name):
    tp = a.shape[0]

    def body(a_ref, wd_ref, xp_ref, gp_ref, bp_ref, g_ref, b_ref, xh_ref, rs_ref, hb_ref):
        wd = jnp.concatenate([wd_ref[j] for j in range(4)], axis=0)
        f = jnp.dot(a_ref[...], wd, preferred_element_type=F32)
        hprev = xp_ref[...] * gp_ref[...] + bp_ref[...]
        _ln_store(ALPHA * hprev + 0.5 * f, g_ref, b_ref, xh_ref, rs_ref, hb_ref)

    out_specs, out_shape = _ln_out(tp)
    return _pcall(
        body, name=name, grid=(tp // TM,),
        in_specs=[pl.BlockSpec((TM, D_FF), lambda i: (i, 0)), pl.BlockSpec((4, D_FF // 4, D_MODEL), lambda i: (0, 0, 0)),
                  pl.BlockSpec((TM, D_MODEL), lambda i: (i, 0))] + [_row_vec(D_MODEL)] * 4,
        out_specs=out_specs, out_shape=out_shape, compiler_params=_params(1),
    )(a, wd, xprev, gp, bp, g, b)


def _rope(x, c, s1, s2, reps):
    n = x.shape[1]
    half = QK_ROPE // 2
    if reps > 1:
        c, s1, s2 = (jnp.tile(t, (1, reps)) for t in (c, s1, s2))
    return x * c + pltpu.roll(x, half, 1) * s1 + pltpu.roll(x, n - half, 1) * s2


def _rms(x, g):
    r = lax.rsqrt(jnp.mean(x * x, axis=-1, keepdims=True) + RMS_EPS)
    return x * r * g, r


def mla_prep(p, gq, gkv, wuq_p, wukv, tabs, *, name):
    tp = p.shape[0]
    nh = MLA_HEADS

    def body(cq_ref, ckv_ref, kr_ref, gq_ref, gkv_ref, wuq_ref, wukv_ref, cq_t, s1q_t, s2q_t, ck_t, s1k_t, s2k_t,
             cqn_ref, ckvn_ref, q2_ref, kv_ref, krr_ref):
        cqn, _ = _rms(cq_ref[...].astype(F32), gq_ref[...])
        ckvn, _ = _rms(ckv_ref[...].astype(F32), gkv_ref[...])
        cqn = cqn.astype(BF)
        ckvn = ckvn.astype(BF)
        cqn_ref[...] = cqn
        ckvn_ref[...] = ckvn
        q = jnp.dot(cqn, wuq_ref[...], preferred_element_type=F32)
        q2_ref[...] = _rope(q, cq_t[...], s1q_t[...], s2q_t[...], nh).astype(BF)
        kv_ref[...] = jnp.dot(ckvn, wukv_ref[...], preferred_element_type=F32).astype(BF)
        krr_ref[...] = _rope(kr_ref[...].astype(F32), ck_t[...], s1k_t[...], s2k_t[...], 1).astype(BF)

    def rows(n, col=0):
        return pl.BlockSpec((TMH, n), functools.partial(lambda i, col: (i, col), col=col))

    return _pcall(
        body, name=name, grid=(tp // TMH,),
        in_specs=[rows(Q_LORA, COL_CQ // Q_LORA), rows(KV_LORA, COL_CKV // KV_LORA), rows(LANES, COL_KR // LANES),
                  _row_vec(Q_LORA), _row_vec(KV_LORA),
                  pl.BlockSpec((Q_LORA, nh * Q_PAD), lambda i: (0, 0)), pl.BlockSpec((KV_LORA, nh * KV_PAD), lambda i: (0, 0)),
                  rows(Q_PAD), rows(Q_PAD), rows(Q_PAD), rows(LANES), rows(LANES), rows(LANES)],
        out_specs=[rows(Q_LORA), rows(KV_LORA), rows(nh * Q_PAD), rows(nh * KV_PAD), rows(LANES)],
        out_shape=[_sds((tp, Q_LORA), BF), _sds((tp, KV_LORA), BF), _sds((tp, nh * Q_PAD), BF),
                   _sds((tp, nh * KV_PAD), BF), _sds((tp, LANES), BF)],
        compiler_params=_params(1),
    )(p, p, p, gq, gkv, wuq_p, wukv, *tabs)


def _causal_mask(s):
    qpos = lax.broadcasted_iota(jnp.int32, (TM, TM), 0)
    kpos = lax.broadcasted_iota(jnp.int32, (TM, TM), 1)
    return jnp.where(kpos <= qpos, s, NEG_INF)


def _key_rows(k):
    return pl.ds(pl.multiple_of(k * TM, TM), TM)


def _pipelined_key_blocks(n, prefetch, process):
    prefetch(0, 0)

    def pair(j, carry):
        prefetch(2 * j + 1, 1)
        process(2 * j, 0, False)
        prefetch(2 * j + 2, 0)
        process(2 * j + 1, 1, False)
        return carry

    lax.fori_loop(0, n // 2, pair, 0)

    @pl.when(n % 2 == 1)
    def _():
        prefetch(n, 1)
        process(n - 1, 0, False)
        process(n, 1, True)

    @pl.when(n % 2 == 0)
    def _():
        process(n, 0, True)


def _side_call(body_main, side, *, name, grid, in_specs, out_specs, out_shape, scratch_shapes, args):
    n_in, n_out, n_scr = len(in_specs), len(out_specs), len(scratch_shapes)
    s_in, s_pre, n_sems, program = side if side is not None else ((), (), 0, None)
    a, b = len(s_in), len(s_pre)
    s_alias = [p for p in s_pre if not isinstance(p, jax.ShapeDtypeStruct)]
    assert len(s_alias) in (0, b)
    c = len(s_alias)

    def body(*refs):
        in_refs = refs[:n_in]
        out_refs = refs[n_in + a + c:n_in + a + c + n_out]
        scr = refs[n_in + a + c + b + n_out:n_in + a + c + b + n_out + n_scr]
        if side is not None:
            side_in = refs[n_in:n_in + a]
            side_out = refs[n_in + a + c + n_out:n_in + a + c + b + n_out]
            start, finish = program(side_in, side_out, refs[-2], refs[-1])

            @pl.when((pl.program_id(0) == 0) & (pl.program_id(1) == 0))
            def _():
                start()

        body_main(in_refs, out_refs, scr)
        if side is not None:
            @pl.when((pl.program_id(0) == grid[0] - 1) & (pl.program_id(1) == grid[1] - 1))
            def _():
                finish()

    sems = [pltpu.SemaphoreType.DMA((n_sems,))] * 2 if side is not None else []
    return _pcall(
        body, name=name, grid=grid, in_specs=list(in_specs) + [HBM_SPEC] * (a + c), out_specs=list(out_specs) + [HBM_SPEC] * b,
        out_shape=list(out_shape) + [_sds(p.shape, p.dtype) for p in s_pre],
        input_output_aliases={n_in + a + i: n_out + i for i in range(c)},
        scratch_shapes=list(scratch_shapes) + sems, compiler_params=_params(2),
    )(*args, *s_in, *s_alias)


def attn_fwd(q2, kv, kr, *, name, side=None):
    tp = q2.shape[0]
    nh = MLA_HEADS
    nb = tp // TM
    rep = TM // LANES

    def body(in_refs, out_refs, scratch):
        q_ref, kv_ref, kr_ref = in_refs
        o_ref, lse_ref = out_refs
        m_ref, l_ref, acc_ref, s0_ref, s1_ref, p_ref, alpha_ref = scratch
        qi = pl.program_id(1)
        s_refs = (s0_ref, s1_ref)
        m_ref[...] = jnp.full_like(m_ref, NEG_INF)
        l_ref[...] = jnp.zeros_like(l_ref)
        acc_ref[...] = jnp.zeros_like(acc_ref)

        def prefetch(k, slot):
            k2 = jnp.concatenate([kv_ref[_key_rows(k), :], kr_ref[_key_rows(k), :]], axis=1)
            s_refs[slot][...] = lax.dot_general(q_ref[...], k2, NT, preferred_element_type=F32)

        def process(k, slot, diagonal):
            for r in range(TM // LANES):
                rows = slice(r * LANES, (r + 1) * LANES)
                s = s_refs[slot][rows, :]
                if diagonal:
                    qpos = r * LANES + lax.broadcasted_iota(jnp.int32, (LANES, TM), 0)
                    s = jnp.where(lax.broadcasted_iota(jnp.int32, (LANES, TM), 1) <= qpos, s, NEG_INF)
                m_prev = m_ref[rows, :]
                m_new = jnp.maximum(m_prev, jnp.max(s, axis=1, keepdims=True))
                alpha = jnp.exp2(m_prev - m_new)
                p = jnp.exp2(s - jnp.tile(m_new, (1, rep)))
                lane_sums = p[:, 0:LANES]
                for t in range(1, rep):
                    lane_sums = lane_sums + p[:, t * LANES:(t + 1) * LANES]
                l_ref[rows, :] = alpha * l_ref[rows, :] + lane_sums
                p_ref[rows, :] = p.astype(BF)
                alpha_ref[rows, :] = alpha
                m_ref[rows, :] = m_new
            acc_ref[...] = alpha_ref[...] * acc_ref[...] + jnp.dot(p_ref[...], kv_ref[_key_rows(k), :], preferred_element_type=F32)

        _pipelined_key_blocks(qi, prefetch, process)
        l = jnp.sum(l_ref[...], axis=1, keepdims=True)
        o_ref[...] = (acc_ref[...] / l).astype(BF)
        lse_ref[...] = m_ref[...] + jnp.log2(l)

    return _side_call(
        body, side, name=name, grid=(nh, nb),
        in_specs=[pl.BlockSpec((TM, Q_PAD), lambda h, qi: (qi, h)), pl.BlockSpec((tp, KV_PAD), lambda h, qi: (0, h)),
                  pl.BlockSpec((tp, LANES), lambda h, qi: (0, 0))],
        out_specs=[pl.BlockSpec((TM, KV_PAD), lambda h, qi: (qi, h)), pl.BlockSpec((TM, LANES), lambda h, qi: (qi, h))],
        out_shape=[_sds((tp, nh * KV_PAD), BF), _sds((tp, nh * LANES), F32)],
        scratch_shapes=[pltpu.VMEM((TM, LANES), F32)] * 3 + [pltpu.VMEM((TM, TM), F32)] * 2
        + [pltpu.VMEM((TM, TM), BF), pltpu.VMEM((TM, LANES), F32)], args=(q2, kv, kr),
    )


def conv_fwd(p, w, *, name):
    tp = p.shape[0]

    def body(b_ref, c_ref, h_ref, w_ref, y_ref, cv_ref, ebuf):
        i = pl.program_id(0)

        @pl.when(i == 0)
        def _():
            ebuf[0:8, :] = jnp.zeros((8, D_CONV), F32)

        e = c_ref[...].astype(F32) * h_ref[...].astype(F32)
        ebuf[8:8 + TM, :] = e
        w_all = w_ref[...]
        conv = w_all[0:1] * ebuf[pl.ds(6, TM), :] + w_all[1:2] * ebuf[pl.ds(7, TM), :] + w_all[2:3] * e
        cv_ref[...] = conv.astype(BF)
        y_ref[...] = (b_ref[...].astype(F32) * conv).astype(BF)
        ebuf[0:8, :] = ebuf[TM:TM + 8, :]

    def col(j):
        return pl.BlockSpec((TM, D_CONV), functools.partial(lambda i, j: (i, j), j=j))

    return _pcall(
        body, name=name, grid=(tp // TM,),
        in_specs=[col(0), col(1), col(2), pl.BlockSpec((3, D_CONV), lambda i: (0, 0))],
        out_specs=[col(0), col(0)], out_shape=[_sds((tp, D_CONV), BF)] * 2,
        scratch_shapes=[pltpu.VMEM((TM + 8, D_CONV), F32)], compiler_params=_params(1),
    )(p, p, p, w)


def merge_out_ln(ycv, o2, p, bg, wbc, wbm_p, wo, xprev, gp, bp, g, b, *, name):
    tp = ycv.shape[0]

    def body(y_ref, o_ref, gc_ref, gm_ref, bg_ref, wbc_ref, wbm_ref, wo_ref, xp_ref, gp_ref, bp_ref, g_ref, b_ref,
             bc_ref, bm_ref, mg_ref, xh_ref, rs_ref, hb_ref):
        bc = jnp.dot(y_ref[...], wbc_ref[...], preferred_element_type=F32)
        bm = jnp.dot(o_ref[...], wbm_ref[...], preferred_element_type=F32)
        bgv = bg_ref[...]
        mg = (jax.nn.sigmoid(gc_ref[...].astype(F32) + bgv[0:1]) * bc
              + jax.nn.sigmoid(gm_ref[...].astype(F32) + bgv[1:2]) * bm)
        mgb = mg.astype(BF)
        bc_ref[...] = bc.astype(BF)
        bm_ref[...] = bm.astype(BF)
        mg_ref[...] = mgb
        mix = jnp.dot(mgb, wo_ref[...], preferred_element_type=F32)
        hprev = xp_ref[...] * gp_ref[...] + bp_ref[...]
        _ln_store(ALPHA * hprev + mix, g_ref, b_ref, xh_ref, rs_ref, hb_ref)

    def rows(n, col=0):
        return pl.BlockSpec((TMH, n), functools.partial(lambda i, col: (i, col), col=col))

    def whole(r, c):
        return pl.BlockSpec((r, c), lambda i: (0, 0))

    ln_specs, ln_shapes = _ln_out(tp, TMH)
    return _pcall(
        body, name=name, grid=(tp // TMH,),
        in_specs=[rows(D_CONV), rows(MLA_HEADS * KV_PAD), rows(D_MODEL, COL_GATES // D_MODEL), rows(D_MODEL, COL_GATES // D_MODEL + 1),
                  whole(2, D_MODEL),
                  whole(D_CONV, D_MODEL), whole(MLA_HEADS * KV_PAD, D_MODEL), whole(D_MODEL, D_MODEL), rows(D_MODEL)]
        + [_row_vec(D_MODEL)] * 4,
        out_specs=[rows(D_MODEL)] * 3 + ln_specs, out_shape=[_sds((tp, D_MODEL), BF)] * 3 + ln_shapes,
        compiler_params=_params(1),
    )(ycv, o2, p, p, bg, wbc, wbm_p, wo, xprev, gp, bp, g, b)


def ln_bwd(dh, xh, rstd, g, *, branch_scale, name):
    tp = xh.shape[0]
    from_loss = isinstance(dh, tuple)

    def body(*refs):
        if from_loss:
            xh_ref, rs_ref, g_ref, b_ref, t_ref, dzb_ref, dg_ref, db_ref, loss_ref = refs
        else:
            dh_ref, xh_ref, rs_ref, g_ref, dzb_ref, dg_ref, db_ref = refs
        i = pl.program_id(0)

        @pl.when(i == 0)
        def _():
            dg_ref[...] = jnp.zeros_like(dg_ref)
            db_ref[...] = jnp.zeros_like(db_ref)
            if from_loss:
                loss_ref[...] = jnp.zeros_like(loss_ref)

        xhat = xh_ref[...]
        if from_loss:
            row = i * TM + lax.broadcasted_iota(jnp.int32, (TM, 1), 0)
            real = (row >= N_META) & (row < N_META + dh[2])
            diff = jnp.where(real, xhat * g_ref[...] + b_ref[...] - t_ref[...], 0.0)
            loss_ref[...] += 0.5 / D_MODEL * jnp.sum(diff * diff)
            dy = diff * (1.0 / D_MODEL)
        else:
            dy = dh_ref[...]
        dg_ref[...] += jnp.sum(dy * xhat, axis=0, keepdims=True)
        db_ref[...] += jnp.sum(dy, axis=0, keepdims=True)
        dxh = dy * g_ref[...]
        m1 = jnp.mean(dxh, axis=-1, keepdims=True)
        m2 = jnp.mean(dxh * xhat, axis=-1, keepdims=True)
        dz = rs_ref[...] * (dxh - m1 - xhat * m2)
        dzb_ref[...] = (branch_scale * dz).astype(BF)

    rows = pl.BlockSpec((TM, D_MODEL), lambda i: (i, 0))
    stat = pl.BlockSpec((TM, 1), lambda i: (i, 0))
    vec = _row_vec(D_MODEL)
    out_specs = [rows, vec, vec]
    out_shape = [_sds((tp, D_MODEL), BF), _sds((1, D_MODEL), F32), _sds((1, D_MODEL), F32)]
    if from_loss:
        in_specs, args = [rows, stat, vec, vec, rows], (xh, rstd, g, dh[0], dh[1])
        out_specs.append(pl.BlockSpec((8, LANES), lambda i: (0, 0)))
        out_shape.append(_sds((8, LANES), F32))
    else:
        in_specs, args = [rows, rows, stat, vec], (dh, xh, rstd, g)
    return _pcall(body, name=name, grid=(tp // TM,), in_specs=in_specs, out_specs=out_specs, out_shape=out_shape,
                  compiler_params=_params(1))(*args)


def ffn_down_bwd(dzb, wd, gate, up, *, name, side=None):
    tp = dzb.shape[0]
    tn = D_FF // 2

    def body(in_refs, out_refs, scratch):
        dz_ref, wd_ref, g_ref, u_ref = in_refs
        dg_ref, du_ref = out_refs
        wd = jnp.concatenate([wd_ref[0], wd_ref[1]], axis=0)
        da = lax.dot_general(dz_ref[...], wd, NT, preferred_element_type=F32)
        g = g_ref[...].astype(F32)
        u = u_ref[...].astype(F32)
        sg = jax.nn.sigmoid(g)
        dg_ref[...] = (da * u * sg * (1.0 + g * (1.0 - sg))).astype(BF)
        du_ref[...] = (da * g * sg).astype(BF)

    blk = pl.BlockSpec((TM, tn), lambda i, j: (i, j))
    return _side_call(
        body, side, name=name, grid=(tp // TM, D_FF // tn),
        in_specs=[pl.BlockSpec((TM, D_MODEL), lambda i, j: (i, 0)), pl.BlockSpec((2, tn // 2, D_MODEL), lambda i, j: (j, 0, 0)), blk, blk],
        out_specs=[blk, blk], out_shape=[_sds((tp, D_FF), BF)] * 2, scratch_shapes=[], args=(dzb, wd, gate, up),
    )


def merge_bwd(dzb, wo, bc, bm, p, bg, wbc, wbm_p, o2, *, name):
    tp = dzb.shape[0]
    nh = MLA_HEADS

    def body(dz_ref, wo_ref, bc_ref, bm_ref, gc_ref, gm_ref, bg_ref, wbc_ref, wbm_ref, o_ref,
             dbc_ref, dbm_ref, dgg_ref, dy_ref, do_ref, dl_ref, dbg_ref):
        i = pl.program_id(0)

        @pl.when(i == 0)
        def _():
            dbg_ref[...] = jnp.zeros_like(dbg_ref)

        dmg = lax.dot_general(dz_ref[...], wo_ref[...], NT, preferred_element_type=F32)
        bgv = bg_ref[...]
        sc = jax.nn.sigmoid(gc_ref[...].astype(F32) + bgv[0:1])
        sm = jax.nn.sigmoid(gm_ref[...].astype(F32) + bgv[1:2])
        dbc = (dmg * sc).astype(BF)
        dbm = (dmg * sm).astype(BF)
        dgc = dmg * bc_ref[...].astype(F32) * sc * (1.0 - sc)
        dgm = dmg * bm_ref[...].astype(F32) * sm * (1.0 - sm)
        dbc_ref[...] = dbc
        dbm_ref[...] = dbm
        dgg_ref[...] = jnp.concatenate([dgc, dgm], axis=1).astype(BF)
        dbg_ref[...] += jnp.concatenate([jnp.sum(dgc, axis=0, keepdims=True), jnp.sum(dgm, axis=0, keepdims=True)], axis=0)
        dy_ref[...] = lax.dot_general(dbc, wbc_ref[...], NT, preferred_element_type=F32)
        do = lax.dot_general(dbm, wbm_ref[...], NT, preferred_element_type=F32)
        do_ref[...] = do.astype(BF)
        prod = do * o_ref[...].astype(F32)
        parts = []
        for h in range(nh):
            d = jnp.sum(prod[:, h * KV_PAD:(h + 1) * KV_PAD], axis=1, keepdims=True)
            parts.append(jnp.broadcast_to(d, (TMH, LANES)))
        dl_ref[...] = jnp.concatenate(parts, axis=1)

    def rows(n, col=0):
        return pl.BlockSpec((TMH, n), functools.partial(lambda i, col: (i, col), col=col))

    def whole(r, c):
        return pl.BlockSpec((r, c), lambda i: (0, 0))

    return _pcall(
        body, name=name, grid=(tp // TMH,),
        in_specs=[rows(D_MODEL), whole(D_MODEL, D_MODEL), rows(D_MODEL), rows(D_MODEL), rows(D_MODEL, COL_GATES // D_MODEL),
                  rows(D_MODEL, COL_GATES // D_MODEL + 1),
                  whole(2, D_MODEL), whole(D_CONV, D_MODEL), whole(nh * KV_PAD, D_MODEL), rows(nh * KV_PAD)],
        out_specs=[rows(D_MODEL), rows(D_MODEL), rows(2 * D_MODEL, COL_GATES // (2 * D_MODEL)), rows(D_CONV), rows(nh * KV_PAD),
                   rows(nh * LANES),
                   whole(2, D_MODEL)],
        out_shape=[_sds((tp, D_MODEL), BF), _sds((tp, D_MODEL), BF), _sds((tp, D_IN_PAD), BF), _sds((tp, D_CONV), F32),
                   _sds((tp, nh * KV_PAD), BF), _sds((tp, nh * LANES), F32), _sds((2, D_MODEL), F32)],
        compiler_params=_params(1),
    )(dzb, wo, bc, bm, p, p, bg, wbc, wbm_p, o2)


def attn_bwd(q2, kv, kr, do2, lse, dl, *, name, side=None):
    tp = q2.shape[0]
    nh = MLA_HEADS
    nb = tp // TM
    rep = TM // LANES

    def body(in_refs, out_refs, scratch):
        q_ref, kv_ref, kr_ref, do_ref, lse_ref, dl_ref = in_refs
        dq_ref, dkv_ref, dkr_ref = out_refs
        dq_acc, s0_ref, s1_ref, dp0_ref, dp1_ref = scratch
        qi = pl.program_id(1)
        s_refs, dp_refs = (s0_ref, s1_ref), (dp0_ref, dp1_ref)

        @pl.when(qi == 0)
        def _():
            dkv_ref[...] = jnp.zeros_like(dkv_ref)

        @pl.when((qi == 0) & (pl.program_id(0) == 0))
        def _():
            dkr_ref[...] = jnp.zeros_like(dkr_ref)

        dq_acc[...] = jnp.zeros_like(dq_acc)

        def prefetch(k, slot):
            kvb = kv_ref[_key_rows(k), :]
            k2 = jnp.concatenate([kvb, kr_ref[_key_rows(k), :]], axis=1)
            s_refs[slot][...] = lax.dot_general(q_ref[...], k2, NT, preferred_element_type=F32)
            dp_refs[slot][...] = lax.dot_general(do_ref[...], kvb, NT, preferred_element_type=F32)

        def process(k, slot, diagonal):
            rows = _key_rows(k)
            s = s_refs[slot][...]
            if diagonal:
                s = _causal_mask(s)
            p = jnp.exp2(s - jnp.tile(lse_ref[...], (1, rep)))
            dsb = (p * (dp_refs[slot][...] - jnp.tile(dl_ref[...], (1, rep)))).astype(BF)
            dk2 = lax.dot_general(dsb, q_ref[...], TN, preferred_element_type=F32) * LN2
            dkv_ref[rows, :] += lax.dot_general(p.astype(BF), do_ref[...], TN, preferred_element_type=F32) + dk2[:, :KV_PAD]
            dkr_ref[rows, :] += dk2[:, KV_PAD:KV_PAD + LANES]
            k2 = jnp.concatenate([kv_ref[rows, :], kr_ref[rows, :]], axis=1)
            dq_acc[...] += jnp.dot(dsb, k2, preferred_element_type=F32)

        _pipelined_key_blocks(qi, prefetch, process)
        dq_ref[...] = dq_acc[...].astype(BF)

    def qrow(n):
        return pl.BlockSpec((TM, n), lambda h, qi: (qi, h))

    def head(n):
        return pl.BlockSpec((tp, n), lambda h, qi: (0, h))

    return _side_call(
        body, side, name=name, grid=(nh, nb),
        in_specs=[qrow(Q_PAD), head(KV_PAD), pl.BlockSpec((tp, LANES), lambda h, qi: (0, 0)), qrow(KV_PAD), qrow(LANES), qrow(LANES)],
        out_specs=[qrow(Q_PAD), head(KV_PAD), pl.BlockSpec((tp, LANES), lambda h, qi: (0, 0))],
        out_shape=[_sds((tp, nh * Q_PAD), BF), _sds((tp, nh * KV_PAD), F32), _sds((tp, LANES), F32)],
        scratch_shapes=[pltpu.VMEM((TM, Q_PAD), F32)] + [pltpu.VMEM((TM, TM), F32)] * 4, args=(q2, kv, kr, do2, lse, dl),
    )


def _rms_bwd(x, g, dy):
    r = lax.rsqrt(jnp.mean(x * x, axis=-1, keepdims=True) + RMS_EPS)
    gy = dy * g
    dx = r * gy - x * (r * r * r) * jnp.mean(x * gy, axis=-1, keepdims=True)
    return dx, jnp.sum(dy * x * r, axis=0, keepdims=True)


def mla_prep_bwd(dq2, dkv, dkr, p, gq, gkv, wuq_p, wukv, tabs_bwd, dp, *, name):
    tp = dq2.shape[0]
    nh = MLA_HEADS

    def body(dq_ref, dkv_ref, dkr_ref, cq_ref, ckv_ref, gq_ref, gkv_ref, wuq_ref, wukv_ref,
             cq_t, s1q_t, s2q_t, ck_t, s1k_t, s2k_t, dp_in_ref, dqb_ref, dsm_ref, dgq_ref, dgkv_ref):
        i = pl.program_id(0)

        @pl.when(i == 0)
        def _():
            dgq_ref[...] = jnp.zeros_like(dgq_ref)
            dgkv_ref[...] = jnp.zeros_like(dgkv_ref)

        dqb = _rope(dq_ref[...].astype(F32), cq_t[...], s1q_t[...], s2q_t[...], nh).astype(BF)
        dqb_ref[...] = dqb
        dcqn = lax.dot_general(dqb, wuq_ref[...], NT, preferred_element_type=F32)
        dcq, dgq = _rms_bwd(cq_ref[...].astype(F32), gq_ref[...], dcqn)
        dckvn = lax.dot_general(dkv_ref[...].astype(BF), wukv_ref[...], NT, preferred_element_type=F32)
        dckv, dgkv = _rms_bwd(ckv_ref[...].astype(F32), gkv_ref[...], dckvn)
        dkr = _rope(dkr_ref[...], ck_t[...], s1k_t[...], s2k_t[...], 1)
        dsm_ref[...] = jnp.concatenate([dcq, dckv, dkr], axis=1).astype(BF)
        dgq_ref[...] += dgq
        dgkv_ref[...] += dgkv

    def rows(n, col=0):
        return pl.BlockSpec((TMH, n), functools.partial(lambda i, col: (i, col), col=col))

    return _pcall(
        body, name=name, grid=(tp // TMH,),
        in_specs=[rows(nh * Q_PAD), rows(nh * KV_PAD), rows(LANES), rows(Q_LORA, COL_CQ // Q_LORA), rows(KV_LORA, COL_CKV // KV_LORA),
                  _row_vec(Q_LORA), _row_vec(KV_LORA),
                  pl.BlockSpec((Q_LORA, nh * Q_PAD), lambda i: (0, 0)), pl.BlockSpec((KV_LORA, nh * KV_PAD), lambda i: (0, 0)),
                  rows(Q_PAD), rows(Q_PAD), rows(Q_PAD), rows(LANES), rows(LANES), rows(LANES), pl.BlockSpec(memory_space=pl.ANY)],
        out_specs=[rows(nh * Q_PAD), rows(COL_GATES - COL_CQ, COL_CQ // (COL_GATES - COL_CQ)), _row_vec(Q_LORA),
                   _row_vec(KV_LORA)],
        out_shape=[_sds((tp, nh * Q_PAD), BF), _sds(dp.shape, dp.dtype), _sds((1, Q_LORA), F32), _sds((1, KV_LORA), F32)],
        input_output_aliases={15: 1}, compiler_params=_params(1),
    )(dq2, dkv, dkr, p, p, gq, gkv, wuq_p, wukv, *tabs_bwd, dp)


def conv_bwd(dy, p, conv, w, dp, *, name):
    tp = dy.shape[0]
    nb = tp // TM

    def body(dy_ref, b_ref, c_ref, h_ref, cv_ref, w_ref, dp_in_ref, dp_ref, dw0_ref, dw1_ref, dw2_ref, dbuf):
        i = pl.program_id(0)

        @pl.when(i == 0)
        def _():
            dbuf[TM:TM + 8, :] = jnp.zeros((8, D_CONV), F32)
            dw0_ref[...] = jnp.zeros_like(dw0_ref)
            dw1_ref[...] = jnp.zeros_like(dw1_ref)
            dw2_ref[...] = jnp.zeros_like(dw2_ref)

        dyv = dy_ref[...]
        c = c_ref[...].astype(F32)
        hh = h_ref[...].astype(F32)
        dconv = dyv * b_ref[...].astype(F32)
        dbuf[0:TM, :] = dconv
        d1 = dbuf[pl.ds(1, TM), :]
        d2 = dbuf[pl.ds(2, TM), :]
        w_all = w_ref[...]
        de = w_all[2:3] * dconv + w_all[1:2] * d1 + w_all[0:1] * d2
        e = c * hh
        dp_ref[...] = jnp.concatenate([dyv * cv_ref[...].astype(F32), de * hh, de * c], axis=1).astype(BF)
        dw0_ref[...] += jnp.sum(d2 * e, axis=0, keepdims=True)
        dw1_ref[...] += jnp.sum(d1 * e, axis=0, keepdims=True)
        dw2_ref[...] += jnp.sum(dconv * e, axis=0, keepdims=True)
        dbuf[TM:TM + 8, :] = dbuf[0:8, :]

    def col(j):
        return pl.BlockSpec((TM, D_CONV), functools.partial(lambda i, j: (nb - 1 - i, j), j=j))

    return _pcall(
        body, name=name, grid=(nb,),
        in_specs=[col(0), col(0), col(1), col(2), col(0), pl.BlockSpec((3, D_CONV), lambda i: (0, 0)),
                  pl.BlockSpec(memory_space=pl.ANY)],
        out_specs=[pl.BlockSpec((TM, 3 * D_CONV), lambda i: (nb - 1 - i, 0))] + [_row_vec(D_CONV)] * 3,
        out_shape=[_sds(dp.shape, dp.dtype)] + [_sds((1, D_CONV), F32)] * 3, input_output_aliases={6: 0},
        scratch_shapes=[pltpu.VMEM((TM + 8, D_CONV), F32)], compiler_params=_params(1),
    )(dy, p, p, p, conv, w, dp)


def adamw(w, g, m, v, *, name):
    r, c = w.shape
    tr = r
    for cand in (256, 128, 64, 32, 16, 8):
        if r % cand == 0 and r > cand:
            tr = cand
            break

    def body(w_ref, g_ref, m_ref, v_ref, d_ref, nm_ref, nv_ref):
        gv = g_ref[...]
        nm = ADAM_B1 * m_ref[...] + (1.0 - ADAM_B1) * gv
        nv = ADAM_B2 * v_ref[...] + (1.0 - ADAM_B2) * (gv * gv)
        m_hat = nm / (1.0 - ADAM_B1 ** ADAM_STEP)
        v_hat = nv / (1.0 - ADAM_B2 ** ADAM_STEP)
        d_ref[...] = -ADAM_LR * (m_hat / (jnp.sqrt(v_hat) + ADAM_EPS) + ADAM_WD * w_ref[...])
        nm_ref[...] = nm
        nv_ref[...] = nv

    blk = pl.BlockSpec((tr, c), lambda i: (i, 0))
    return _pcall(
        body, name=name, grid=(r // tr,), in_specs=[blk] * 4, out_specs=[blk] * 3,
        out_shape=[_sds((r, c), F32)] * 3, compiler_params=_params(1),
    )(w, g, m, v)


HBM_SPEC = pl.BlockSpec(memory_space=pltpu.HBM)


def _place():
    return lax.axis_index("x"), lax.axis_index("y"), lax.axis_index("c")


def _other_chips(x, y):
    return [(1 - x, y), (x, 1 - y), (1 - x, 1 - y)]


def _half(ref_or_shape_rows, c):
    return pl.ds(c * (ref_or_shape_rows // 2), ref_or_shape_rows // 2)


def gather_side(items):
    n = len(items)
    shards = [s for s, _ in items]
    layers = [l for _, l in items]

    def program(x_refs, o_refs, send_sems, recv_sems):
        x, y, c = _place()
        me = 2 * x + y
        chips = _other_chips(x, y)

        def copy(sem, src, dst, to):
            return pltpu.make_async_remote_copy(src_ref=src, dst_ref=dst, send_sem=send_sems.at[sem], recv_sem=recv_sems.at[sem],
                                                device_id=to, device_id_type=MESH)

        def src(i):
            return x_refs[i].at[layers[i], _half(x_refs[i].shape[1], c)]

        def dst(i, slot, cc):
            return o_refs[i].at[slot, _half(o_refs[i].shape[1], cc)]

        sends = [copy(6 * i + k, src(i), dst(i, me, c), (px, py, c)) for i in range(n) for k, (px, py) in enumerate(chips)]
        passed = [copy(6 * i + 3 + k, dst(i, 2 * px + py, c), dst(i, 2 * px + py, c), (x, y, 1 - c))
                  for k, (px, py) in enumerate(chips) for i in range(n)]

        def start():
            for cp in sends:
                cp.start()

        def finish():
            pos = 0
            for k, (px, py) in enumerate(chips):
                for i in range(n):
                    copy(6 * i + k, src(i), dst(i, 2 * px + py, c), (px, py, c)).wait_recv()
                    passed[pos].start()
                    pos += 1
            for k, (px, py) in enumerate(chips):
                for i in range(n):
                    copy(6 * i + 3 + k, dst(i, 2 * px + py, 1 - c), dst(i, 2 * px + py, 1 - c), (x, y, 1 - c)).wait_recv()
            for cp in sends + passed:
                cp.wait_send()

        return start, finish

    prefilled = [jnp.broadcast_to(s[l][None], (4,) + s.shape[1:]) for s, l in items]
    return shards, prefilled, 6 * n, program


def scatter_side(pss):
    n = len(pss)

    def program(p_refs, o_refs, send_sems, recv_sems):
        x, y, c = _place()
        me = 2 * x + y
        chips = _other_chips(x, y)

        def copy(i, k, j_src, j_dst, to):
            return pltpu.make_async_remote_copy(src_ref=p_refs[i].at[j_src], dst_ref=o_refs[i].at[j_dst],
                                                send_sem=send_sems.at[3 * i + k], recv_sem=recv_sems.at[3 * i + k],
                                                device_id=to, device_id_type=MESH)

        sends = [copy(i, k, 2 * px + py, me, (px, py, c)) for i in range(n) for k, (px, py) in enumerate(chips)]

        def start():
            for cp in sends:
                cp.start()

        def finish():
            for i in range(n):
                for k, (px, py) in enumerate(chips):
                    copy(i, k, me, 2 * px + py, (px, py, c)).wait_recv()
            for cp in sends:
                cp.wait_send()

        return start, finish

    xi, yi, _ = _place()
    own = jnp.arange(4)[:, None, None] == 2 * xi + yi
    prefilled = [jnp.where(own, p, jnp.zeros_like(p)) for p in pss]
    return list(pss), prefilled, 3 * n, program


def exchange_alone(side, *, name):
    inputs, prefilled, n_sems, program = side
    a, b = len(inputs), len(prefilled)

    def body(*refs):
        start, finish = program(refs[:a], refs[a + b:a + 2 * b], refs[-2], refs[-1])
        start()
        finish()

    return _pcall(
        body, name=name, in_specs=[HBM_SPEC] * (a + b), out_specs=[HBM_SPEC] * b, out_shape=[_sds(p.shape, p.dtype) for p in prefilled],
        input_output_aliases={a + i: i for i in range(b)}, scratch_shapes=[pltpu.SemaphoreType.DMA((n_sems,))] * 2,
    )(*inputs, *prefilled)


def pair_exchange(gs, *, name):
    n = len(gs)

    def body(*refs):
        g_refs, o_refs = refs[:n], refs[n:2 * n]
        send_sems, recv_sems = refs[2 * n:]
        x, y, c = _place()
        cps = [pltpu.make_async_remote_copy(src_ref=g_refs[i].at[:, _half(g_refs[i].shape[1], 1 - c)], dst_ref=o_refs[i],
                                            send_sem=send_sems.at[i], recv_sem=recv_sems.at[i], device_id=(x, y, 1 - c),
                                            device_id_type=MESH)
               for i in range(n)]
        for cp in cps:
            cp.start()
        for cp in cps:
            cp.wait()

    return _pcall(
        body, name=name, in_specs=[HBM_SPEC] * n, out_specs=[HBM_SPEC] * n,
        out_shape=[_sds((4, g.shape[1] // 2, g.shape[2]), g.dtype) for g in gs],
        scratch_shapes=[pltpu.SemaphoreType.DMA((n,)), pltpu.SemaphoreType.DMA((n,))],
    )(*gs)


def pair_exchange_side(gs):
    n = len(gs)

    def program(g_refs, o_refs, send_sems, recv_sems):
        x, y, c = _place()
        cps = [pltpu.make_async_remote_copy(src_ref=g_refs[i].at[:, _half(g_refs[i].shape[1], 1 - c)], dst_ref=o_refs[i],
                                            send_sem=send_sems.at[i], recv_sem=recv_sems.at[i], device_id=(x, y, 1 - c),
                                            device_id_type=MESH)
               for i in range(n)]

        def start():
            for cp in cps:
                cp.start()

        def finish():
            for cp in cps:
                cp.wait()

        return start, finish

    return list(gs), [_sds((4, g.shape[1] // 2, g.shape[2]), g.dtype) for g in gs], n, program


def _comm_rows(a, b, itemsize):
    return a // 2 if a * b * itemsize > (3 << 19) and a % 16 == 0 else a


def pair_add(g, s1, c_idx, *, name):
    n, a, b = g.shape
    ah = a // 2
    ta = _comm_rows(ah, b, 2)
    nblk = ah // ta

    def body(c_ref, g_ref, s_ref, o_ref):
        o_ref[...] = (g_ref[...].astype(F32) + s_ref[...].astype(F32)).astype(o_ref.dtype)

    grid_spec = pltpu.PrefetchScalarGridSpec(
        num_scalar_prefetch=1, grid=(n, nblk),
        in_specs=[pl.BlockSpec((1, ta, b), lambda j, i, c_ref: (j, c_ref[0] * nblk + i, 0)),
                  pl.BlockSpec((1, ta, b), lambda j, i, c_ref: (j, i, 0))],
        out_specs=pl.BlockSpec((1, ta, b), lambda j, i, c_ref: (j, i, 0)),
    )
    return _pcall(body, name=name, grid_spec=grid_spec, out_shape=_sds((n, ah, b), g.dtype), compiler_params=_params(2))(
        c_idx, g, s1)


def sum_chunks(s2, *, name):
    n, a, b = s2.shape
    ta = _comm_rows(a, b, 4)

    def body(s_ref, o_ref):
        acc = s_ref[0].astype(F32)
        for j in range(1, n):
            acc = acc + s_ref[j].astype(F32)
        o_ref[...] = acc

    return _pcall(
        body, name=name, grid=(a // ta,), in_specs=[pl.BlockSpec((n, ta, b), lambda i: (0, i, 0))],
        out_specs=pl.BlockSpec((ta, b), lambda i: (i, 0)), out_shape=_sds((a, b), F32), compiler_params=_params(1),
    )(s2)


def pair_gather_side(rcs):
    n = len(rcs)

    def program(r_refs, o_refs, send_sems, recv_sems):
        x, y, c = _place()

        def copy(i, half):
            return pltpu.make_async_remote_copy(src_ref=r_refs[i], dst_ref=o_refs[i].at[half], send_sem=send_sems.at[i],
                                                recv_sem=recv_sems.at[i], device_id=(x, y, 1 - c), device_id_type=MESH)

        sends = [copy(i, c) for i in range(n)]

        def start():
            for cp in sends:
                cp.start()

        def finish():
            for i in range(n):
                copy(i, 1 - c).wait_recv()
            for cp in sends:
                cp.wait_send()

        return start, finish

    prefilled = [jnp.broadcast_to(r[None], (2,) + r.shape) for r in rcs]
    return list(rcs), prefilled, n, program


def exchange_small(arrs, *, reduce, name):
    n = len(arrs)

    def body(*refs):
        v_refs, o_refs = refs[:n], refs[n:2 * n]
        bufs = refs[2 * n:3 * n] if reduce else o_refs
        send_sems, recv_sems = refs[-2:]
        x, y, c = _place()
        me = 4 * x + 2 * y + c
        for i in range(n):
            bufs[i][me] = v_refs[i][...]

        def peer(k):
            dx, dy, dc = (k >> 2) & 1, (k >> 1) & 1, k & 1
            return (1 - x if dx else x, 1 - y if dy else y, 1 - c if dc else c)

        def copy(i, k, slot):
            return pltpu.make_async_remote_copy(src_ref=v_refs[i], dst_ref=bufs[i].at[slot], send_sem=send_sems.at[7 * i + k - 1],
                                                recv_sem=recv_sems.at[7 * i + k - 1], device_id=peer(k), device_id_type=MESH)

        sends = [copy(i, k, me) for i in range(n) for k in range(1, 8)]
        for cp in sends:
            cp.start()
        for i in range(n):
            for k in range(1, 8):
                px, py, pc = peer(k)
                copy(i, k, 4 * px + 2 * py + pc).wait_recv()
        for cp in sends:
            cp.wait_send()
        if reduce:
            for i in range(n):
                acc = bufs[i][0]
                for d in range(1, 8):
                    acc = acc + bufs[i][d]
                o_refs[i][...] = acc

    vmem = pl.BlockSpec(memory_space=pltpu.VMEM)
    stacked = [(8,) + a.shape for a in arrs]
    return _pcall(
        body, name=name, in_specs=[vmem] * n, out_specs=[vmem] * n,
        out_shape=[_sds(a.shape if reduce else s, F32) for a, s in zip(arrs, stacked)],
        scratch_shapes=([pltpu.VMEM(s, F32) for s in stacked] if reduce else [])
        + [pltpu.SemaphoreType.DMA((7 * n,)), pltpu.SemaphoreType.DMA((7 * n,))],
    )(*arrs)


def _pad_rows(n, mult):
    return -(-n // mult) * mult


def _chip_major(g, b):
    return g.reshape(g.shape[0], 4, b).transpose(1, 0, 2)


def _rope_tables(tp):
    inv_freq = 1.0 / (ROPE_BASE ** (jnp.arange(0, QK_ROPE, 2, dtype=F32) / QK_ROPE))
    ang = jnp.arange(tp, dtype=F32)[:, None] * inv_freq[None, :]
    cos, sin = jnp.cos(ang), jnp.sin(ang)
    one = lambda n: jnp.ones((tp, n), F32)
    zero = lambda n: jnp.zeros((tp, n), F32)
    half = QK_ROPE // 2
    tail = Q_PAD - Q_ROPE_AT - QK_ROPE
    cq = jnp.concatenate([one(Q_ROPE_AT), cos, cos, one(tail)], axis=1)
    s1q = jnp.concatenate([zero(Q_ROPE_AT + half), sin, zero(tail)], axis=1)
    s2q = jnp.concatenate([zero(Q_ROPE_AT), -sin, zero(tail + half)], axis=1)
    ck = jnp.concatenate([cos, cos, zero(KV_PAD - QK_ROPE)], axis=1)
    s1k = jnp.concatenate([zero(half), sin, zero(KV_PAD - QK_ROPE)], axis=1)
    s2k = jnp.concatenate([-sin, zero(KV_PAD - half)], axis=1)
    fwd = (cq * (ATT_SCALE * LOG2E), s1q * (ATT_SCALE * LOG2E), s2q * (ATT_SCALE * LOG2E), ck, s1k, s2k)
    bwd = (cq * ATT_SCALE, -s1q * ATT_SCALE, -s2q * ATT_SCALE, ck, -s1k, -s2k)
    return fwd, bwd


def _pad_w_in(w):
    return jnp.concatenate([w[:, :COL_KR_END], jnp.zeros((w.shape[0], COL_GATES - COL_KR_END), w.dtype), w[:, COL_KR_END:]], axis=1)


def _pad_w_uq(w):
    w = w.reshape(Q_LORA, MLA_HEADS, QK_NOPE + QK_ROPE)
    z = lambda n: jnp.zeros((Q_LORA, MLA_HEADS, n), w.dtype)
    return jnp.concatenate([w[..., :QK_NOPE], z(Q_ROPE_AT - QK_NOPE), w[..., QK_NOPE:], z(Q_PAD - Q_ROPE_AT - QK_ROPE)],
                           axis=-1).reshape(Q_LORA, MLA_HEADS * Q_PAD)


def _unpad_w_uq(w):
    w = w.reshape(Q_LORA, MLA_HEADS, Q_PAD)
    return jnp.concatenate([w[..., :QK_NOPE], w[..., Q_ROPE_AT:Q_ROPE_AT + QK_ROPE]], axis=-1).reshape(
        Q_LORA, MLA_HEADS * (QK_NOPE + QK_ROPE))


def _pad_w_br_mla(w):
    w = w.reshape(MLA_HEADS, V_HEAD, D_MODEL)
    return jnp.concatenate([jnp.zeros_like(w), w], axis=1).reshape(MLA_HEADS * KV_PAD, D_MODEL)


def _unpad_w_br_mla(w):
    return w.reshape(MLA_HEADS, KV_PAD, D_MODEL)[:, V_HEAD:].reshape(MLA_HEADS * V_HEAD, D_MODEL)


def _riding(hooks, where, l, *args):
    make = hooks.get(where)
    ride = make(l, *args) if make else None
    return ride if ride else (None, lambda results: None)


def _layer_fwd(l, st, xprev, gp, bp, hb, w, tabs, hooks):
    ln_g, ln_b = w["ln_g"], w["ln_b"]
    lg = lambda k: ln_g[l, k][None]
    lb = lambda k: ln_b[l, k][None]
    s = {}
    s["x0"], s["gp0"], s["bp0"], s["hb0"] = xprev, gp, bp, hb
    side, got = _riding(hooks, "ffn1_fwd", l)
    s["g1"], s["u1"], s["a1"], *extras = ffn_up(hb, w["ffn1_w_up"][l], name="ffn_up", side=side)
    got(extras)
    s["xh1"], s["rs1"], s["hb1"] = down_ln(s["a1"], w["ffn1_w_down"][l], xprev, gp, bp, lg(0), lb(0), name="ffn_down_ln")
    s["p"] = mm_rows([(s["hb1"], w["mix_w_in"][l], False, 0)], D_IN_PAD, name="mix_in", tn=1024, out_dtype=BF)
    gq, gkv = w["q_norm_g"][l][None], w["kv_norm_g"][l][None]
    s["cqn"], s["ckvn"], s["q2"], s["kv"], s["kr"] = mla_prep(s["p"], gq, gkv, w["w_uq"][l], w["w_ukv"][l], tabs, name="mla_prep")
    side, got = _riding(hooks, "attn_fwd", l)
    s["o2"], s["lse"], *extras = attn_fwd(s["q2"], s["kv"], s["kr"], name="attn_fwd", side=side)
    got(extras)
    s["ycv"], s["conv"] = conv_fwd(s["p"], w["conv_w"][l], name="conv_fwd")
    s["bc"], s["bm"], s["mg"], s["xh2"], s["rs2"], s["hb2"] = merge_out_ln(
        s["ycv"], s["o2"], s["p"], w["mix_b_gate"][l], w["w_br_conv"][l], w["w_br_mla"][l], w["w_o"][l],
        s["xh1"], lg(0), lb(0), lg(1), lb(1), name="merge_out_ln")
    s["g2"], s["u2"], s["a2"] = ffn_up(s["hb2"], w["ffn2_w_up"][l], name="ffn_up")
    s["xh3"], s["rs3"], s["hb3"] = down_ln(s["a2"], w["ffn2_w_down"][l], s["xh2"], lg(1), lb(1), lg(2), lb(2), name="ffn_down_ln")
    st.append(s)
    return s["xh3"], lg(2), lb(2), s["hb3"]


def _ffn_bwd(which, l, g, hooks, dh, w_up, w_down, ln_gain, hb_in, gate, up, act, xh, rs):
    dzb, dgam, dbet, *loss_acc = ln_bwd(dh, xh, rs, ln_gain, branch_scale=0.5, name="ln_bwd")
    if loss_acc:
        g["loss"] = loss_acc[0]
    g[which + "_w_down"] = tn_mm(act, dzb, tm=D_FF // 2, name="dw_down", shard=("rows", D_FF // 4))
    side, got = _riding(hooks, which + "_down_bwd", l, g)
    dgate, dup, *extras = ffn_down_bwd(dzb, w_down, gate, up, name="ffn_down_bwd", side=side)
    got(extras)
    d_w = tn_mm(hb_in, dgate, tm=512, name="dw_up", shard=("cols", D_FF // 2), slot0=0)
    g[which + "_w_up"] = tn_mm(hb_in, dup, tm=512, name="dw_up", shard=("cols", D_FF // 2), slot0=2, dst=d_w)
    side, got = _riding(hooks, which + "_up_bwd", l, g)
    dh_in = mm_rows([(dgate, w_up, True, 0), (dup, w_up, True, 1)], D_MODEL, name="ffn_up_bwd", tn=512, addend=dzb, add_scale=2.0 * ALPHA,
                    side=side)
    if side is not None:
        dh_in, *extras = dh_in
        got(extras)
    return dh_in, dgam, dbet


def _layer_bwd(l, s, dh, w, tabs_bwd, hooks):
    ln_g = w["ln_g"]
    lg = lambda k: ln_g[l, k][None]
    g = {}
    dh, dg2, db2 = _ffn_bwd("ffn2", l, g, hooks, dh, w["ffn2_w_up"][l], w["ffn2_w_down"][l], lg(2), s["hb2"], s["g2"], s["u2"],
                            s["a2"], s["xh3"], s["rs3"])
    dzb, dg1, db1 = ln_bwd(dh, s["xh2"], s["rs2"], lg(1), branch_scale=1.0, name="ln_bwd")
    g["w_o"] = tn_mm(s["mg"], dzb, tm=1024, name="dw_o", shard=("rows", D_MODEL // 4))
    dbc, dbm, dp, dycv, do2, dl, g["mix_b_gate"] = merge_bwd(
        dzb, w["w_o"][l], s["bc"], s["bm"], s["p"], w["mix_b_gate"][l], w["w_br_conv"][l], w["w_br_mla"][l], s["o2"], name="merge_bwd")
    g["w_br_conv"] = tn_mm(s["ycv"], dbc, tm=512, name="dw_br_conv", shard=("cols", D_MODEL // 4))
    g["w_br_mla"] = _chip_major(_unpad_w_br_mla(tn_mm(s["o2"], dbm, tm=1024, name="dw_br_mla")), D_MODEL // 4)
    side, got = _riding(hooks, "attn_bwd", l, g)
    dq2, dkv, dkr, *extras = attn_bwd(s["q2"], s["kv"], s["kr"], do2, s["lse"], dl, name="attn_bwd", side=side)
    got(extras)
    gq, gkv = w["q_norm_g"][l][None], w["kv_norm_g"][l][None]
    dqb, dp, g["q_norm_g"], g["kv_norm_g"] = mla_prep_bwd(dq2, dkv, dkr, s["p"], gq, gkv, w["w_uq"][l], w["w_ukv"][l], tabs_bwd, dp,
                                                          name="mla_prep_bwd")
    g["w_uq"] = _chip_major(_unpad_w_uq(tn_mm(s["cqn"], dqb, tm=Q_LORA, name="dw_uq")), MLA_HEADS * (QK_NOPE + QK_ROPE) // 4)
    g["w_ukv"] = tn_mm(s["ckvn"], dkv, tm=KV_LORA, name="dw_ukv", shard=("cols", MLA_HEADS * KV_PAD // 4))
    dp, dw0, dw1, dw2 = conv_bwd(dycv, s["p"], s["conv"], w["conv_w"][l], dp, name="conv_bwd")
    g["conv_w"] = jnp.concatenate([dw0, dw1, dw2], axis=0)
    d_in = tn_mm(s["hb1"], dp, tm=512, name="dw_in")
    g["mix_w_in"] = _chip_major(jnp.concatenate([d_in[:, :COL_KR_END], d_in[:, COL_GATES:]], axis=1), D_IN // 4)
    side, got = _riding(hooks, "mix_in_bwd", l)
    dh = mm_rows([(dp, w["mix_w_in"][l], True, 0)], D_MODEL, name="mix_in_bwd", tn=512, addend=dzb, add_scale=ALPHA, side=side)
    if side is not None:
        dh, *extras = dh
        got(extras)
    dh, dg0, db0 = _ffn_bwd("ffn1", l, g, hooks, dh, w["ffn1_w_up"][l], w["ffn1_w_down"][l], lg(0), s["hb0"], s["g1"], s["u1"],
                            s["a1"], s["xh1"], s["rs1"])
    g["ln_g"] = jnp.concatenate([dg0, dg1, dg2], axis=0)
    g["ln_b"] = jnp.concatenate([db0, db1, db2], axis=0)
    return dh, g


BIG = ("ffn1_w_up", "ffn1_w_down", "mix_w_in", "w_uq", "w_ukv", "w_br_conv", "w_br_mla", "w_o", "ffn2_w_up", "ffn2_w_down")
BIG_AXIS = (2, 1, 2, 2, 2, 2, 2, 1, 2, 1)
FFN1_MATRICES = ("ffn1_w_up", "ffn1_w_down")
MIXER_MATRICES = ("mix_w_in", "w_uq", "w_ukv", "w_br_conv", "w_br_mla", "w_o")
FFN2_MATRICES = ("ffn2_w_up", "ffn2_w_down")
SMALL_SHARDED = ("meta_tokens", "mix_b_gate", "conv_w", "ln_g", "ln_b")
SMALL_REPLICATED = ("q_norm_g", "kv_norm_g")
WEIGHTS = ("meta_tokens", "ffn1_w_up", "ffn1_w_down", "mix_w_in", "mix_b_gate", "conv_w", "q_norm_g", "w_uq", "kv_norm_g", "w_ukv",
           "w_br_conv", "w_br_mla", "w_o", "ffn2_w_up", "ffn2_w_down", "ln_g", "ln_b")


def _view2d(a):
    return a.reshape(-1, a.shape[-1])


def _local_grads(x_row, target_row, w, hooks=None):
    hooks = hooks or {}
    seq = x_row.shape[0]
    t_real = N_META + seq
    tp = _pad_rows(t_real, TM)
    pad = tp - t_real
    h0 = jnp.concatenate([w["meta_tokens"], x_row, jnp.zeros((pad, D_MODEL), F32)], axis=0)
    target_p = jnp.concatenate([jnp.zeros((N_META, D_MODEL), F32), target_row, jnp.zeros((pad, D_MODEL), F32)], axis=0)
    tabs, tabs_bwd = _rope_tables(tp)
    ones = jnp.ones((1, D_MODEL), F32)
    zeros = jnp.zeros((1, D_MODEL), F32)
    saved = []
    cur = (h0, ones, zeros, h0.astype(BF))
    for l in range(DEPTH):
        cur = _layer_fwd(l, saved, *cur, w, tabs, hooks)
    dh = (cur[2], target_p, seq)
    grads = [None] * DEPTH
    for l in reversed(range(DEPTH)):
        dh, grads[l] = _layer_bwd(l, saved[l], dh, w, tabs_bwd, hooks)
        if "layer_bwd_done" in hooks:
            hooks["layer_bwd_done"](l, grads[l])
    return grads[DEPTH - 1].pop("loss"), dh[N_META:t_real], dh[:N_META], grads


def kernel(x, meta_tokens, ffn1_w_up, ffn1_w_down, mix_w_in, mix_b_gate, conv_w, q_norm_g, w_uq, kv_norm_g, w_ukv, w_br_conv, w_br_mla, w_o, ffn2_w_up, ffn2_w_down, ln_g, ln_b, loss_target, m_meta_tokens, m_ffn1_w_up, m_ffn1_w_down, m_mix_w_in, m_mix_b_gate, m_conv_w, m_q_norm_g, m_w_uq, m_kv_norm_g, m_w_ukv, m_w_br_conv, m_w_br_mla, m_w_o, m_ffn2_w_up, m_ffn2_w_down, m_ln_g, m_ln_b, v_meta_tokens, v_ffn1_w_up, v_ffn1_w_down, v_mix_w_in, v_mix_b_gate, v_conv_w, v_q_norm_g, v_w_uq, v_kv_norm_g, v_w_ukv, v_w_br_conv, v_w_br_mla, v_w_o, v_ffn2_w_up, v_ffn2_w_down, v_ln_g, v_ln_b):
    local = dict(meta_tokens=meta_tokens, ffn1_w_up=ffn1_w_up, ffn1_w_down=ffn1_w_down, mix_w_in=mix_w_in, mix_b_gate=mix_b_gate,
                 conv_w=conv_w, q_norm_g=q_norm_g, w_uq=w_uq, kv_norm_g=kv_norm_g, w_ukv=w_ukv, w_br_conv=w_br_conv,
                 w_br_mla=w_br_mla, w_o=w_o, ffn2_w_up=ffn2_w_up, ffn2_w_down=ffn2_w_down, ln_g=ln_g, ln_b=ln_b)
    mom_m = dict(zip(WEIGHTS, (m_meta_tokens, m_ffn1_w_up, m_ffn1_w_down, m_mix_w_in, m_mix_b_gate, m_conv_w, m_q_norm_g, m_w_uq,
                               m_kv_norm_g, m_w_ukv, m_w_br_conv, m_w_br_mla, m_w_o, m_ffn2_w_up, m_ffn2_w_down, m_ln_g, m_ln_b)))
    mom_v = dict(zip(WEIGHTS, (v_meta_tokens, v_ffn1_w_up, v_ffn1_w_down, v_mix_w_in, v_mix_b_gate, v_conv_w, v_q_norm_g, v_w_uq,
                               v_kv_norm_g, v_w_ukv, v_w_br_conv, v_w_br_mla, v_w_o, v_ffn2_w_up, v_ffn2_w_down, v_ln_g, v_ln_b)))
    xi, yi, ci = _place()
    chip = 2 * xi + yi

    shards = {n: local[n].astype(BF) for n in BIG}
    axis = dict(zip(BIG, BIG_AXIS))
    pad_layout = {"mix_w_in": _pad_w_in, "w_uq": _pad_w_uq, "w_br_mla": _pad_w_br_mla}
    w = {n: [None] * DEPTH for n in BIG}

    def fetch(keys):
        def install(gathered):
            for (n, l), g in zip(keys, gathered):
                if n in FFN1_MATRICES + FFN2_MATRICES:
                    w[n][l] = g
                    continue
                full = jnp.concatenate([g[j] for j in range(4)], axis=axis[n] - 1)
                w[n][l] = pad_layout[n](full) if n in pad_layout else full
        return gather_side([(shards[n], l) for n, l in keys]), install

    first, install_first = fetch([("ffn1_w_up", 0)])
    install_first(exchange_alone(first, name="gather_weights"))
    fetch_under = {("ffn1_fwd", 0): [("ffn1_w_down", 0)] + [(n, 0) for n in MIXER_MATRICES],
                   ("attn_fwd", 0): [(n, 0) for n in FFN2_MATRICES] + [(n, 1) for n in BIG]}
    hooks = {where: functools.partial(lambda l, where: fetch(fetch_under[where, l]) if (where, l) in fetch_under else None, where=where)
             for where in ("ffn1_fwd", "attn_fwd")}
    stacked = exchange_small([_view2d(local[n]) for n in SMALL_SHARDED], reduce=False, name="gather_small")
    for n, st in zip(SMALL_SHARDED, stacked):
        full = jnp.concatenate([st[2 * j] for j in range(4)], axis=-1)
        w[n] = full.reshape(local[n].shape[:-1] + (full.shape[-1],))
    for n in SMALL_REPLICATED:
        w[n] = local[n]

    c_idx = jnp.reshape(ci, (1,)).astype(jnp.int32)
    done, from_sibling, from_chips = {}, {}, {}

    def send(keys, grad_of):
        waiting = [k for k in keys if k not in from_sibling]
        from_sibling.update(zip(waiting, pair_exchange([grad_of[k] for k in waiting], name="rs_pair_exchange")))
        sums = [pair_add(grad_of[k], from_sibling[k], c_idx, name="rs_pair_add") for k in keys]
        return scatter_side(sums), lambda results: from_chips.update(zip(keys, results))

    layer1 = [(n, 1) for n in BIG]
    hooks["ffn2_down_bwd"] = lambda l, g: (pair_exchange_side([done[k] for k in layer1]),
                                           lambda results: from_sibling.update(zip(layer1, results))) if l == 0 else None

    reduced = {}
    hooks["mix_in_bwd"] = lambda l: (pair_gather_side([sum_chunks(from_chips[k], name="rs_sum") for k in layer1]),
                                     lambda results: reduced.update(zip(layer1, results))) if l == 0 else None

    send_under = {"attn_bwd": FFN2_MATRICES + ("w_o", "w_br_conv", "w_br_mla"),
                  "ffn1_down_bwd": ("mix_w_in", "w_uq", "w_ukv", "ffn1_w_down"), "ffn1_up_bwd": ("ffn1_w_up",)}
    hooks["layer_bwd_done"] = lambda l, g: done.update({(n, l): g[n] for n in BIG})

    def send_hook(where):
        def hook(l, g):
            if l != 0:
                return None
            keys = [(n, 0) for n in send_under[where]] + ([(n, 1) for n in BIG] if where == "attn_bwd" else [])
            return send(keys, {**done, **{(n, 0): g[n] for n in send_under[where]}})
        return hook

    for where in send_under:
        hooks[where] = send_hook(where)

    loss_acc, grad_x, d_meta, grads = _local_grads(x[0], loss_target[0], w, hooks)
    grad_x = grad_x[None]
    layer0 = [(n, 0) for n in BIG]
    reduced.update(zip(layer0, exchange_alone(pair_gather_side([sum_chunks(from_chips[k], name="rs_sum") for k in layer0]),
                                              name="rs_pair_gather")))
    reduced = {k: r.reshape(local[k[0]].shape[1:]) for k, r in reduced.items()}
    gshard = {n: jnp.stack([reduced[n, l] for l in range(DEPTH)]) for n in BIG}

    small_names = SMALL_SHARDED + SMALL_REPLICATED
    gsmall = {n: jnp.concatenate([grads[l][n] for l in range(DEPTH)], axis=0) for n in small_names if n != "meta_tokens"}
    gsmall["meta_tokens"] = d_meta
    small_red = exchange_small([gsmall[n] for n in small_names] + [loss_acc], reduce=True, name="reduce_small")
    loss = small_red[-1][0, 0]
    for n, full in zip(small_names, small_red[:-1]):
        if n in SMALL_SHARDED:
            sh = local[n].shape[-1]
            full = lax.dynamic_slice_in_dim(full, chip * sh, sh, axis=1)
        gshard[n] = full.reshape(local[n].shape)

    delta, new_m, new_v = {}, {}, {}
    for n in WEIGHTS:
        shape = local[n].shape
        d, nm, nv = adamw(_view2d(local[n]), _view2d(gshard[n]), _view2d(mom_m[n]), _view2d(mom_v[n]), name="adamw")
        delta[n], new_m[n], new_v[n] = d.reshape(shape), nm.reshape(shape), nv.reshape(shape)
    return (loss, grad_x, *[gshard[n] for n in WEIGHTS], *[delta[n] for n in WEIGHTS], *[new_m[n] for n in WEIGHTS],
            *[new_v[n] for n in WEIGHTS])
```

```python
import functools

import jax
import jax.numpy as jnp
from jax import lax
from jax.experimental import pallas as pl
from jax.experimental.pallas import tpu as pltpu

F32 = jnp.float32
BF = jnp.bfloat16
MESH = pl.DeviceIdType.MESH

D_MODEL = 1024
DEPTH = 2
N_META = 16
D_CONV = 512
MLA_HEADS = 8
QK_NOPE = 64
QK_ROPE = 32
V_HEAD = 64
Q_LORA = 256
KV_LORA = 128
ROPE_BASE = 10000.0
NEG_INF = -1e30
D_FF = 2816
ALPHA = (2 * DEPTH) ** 0.25
LN_EPS = 1e-5
RMS_EPS = 1e-6
ATT_SCALE = (QK_NOPE + QK_ROPE) ** -0.5
LOG2E = 1.4426950408889634
LN2 = 0.6931471805599453
D_IN = 4000
D_IN_PAD = 4096
COL_CQ = 3 * D_CONV
COL_CKV = COL_CQ + Q_LORA
COL_KR = COL_CKV + KV_LORA
COL_KR_END = COL_KR + QK_ROPE
KV_PAD = 128
COL_GATES = COL_KR + KV_PAD
Q_PAD = 256
Q_ROPE_AT = 128

ADAM_LR = 0.001
ADAM_B1 = 0.9
ADAM_B2 = 0.999
ADAM_EPS = 1e-08
ADAM_WD = 0.01
ADAM_STEP = 10

TM = 768
TMH = 384
LANES = 128
VMEM_LIMIT_BYTES = 50 * 1024 * 1024

NT = (((1,), (1,)), ((), ()))
TN = (((0,), (0,)), ((), ()))


def _pcall(body, **kw):
    return pl.pallas_call(body, **kw)


def _params(n_axes):
    return pltpu.CompilerParams(dimension_semantics=("arbitrary",) * n_axes, vmem_limit_bytes=VMEM_LIMIT_BYTES)


def _sds(shape, dtype):
    return jax.ShapeDtypeStruct(shape, dtype)


def mm_rows(pairs, n_out, *, name, tn=None, addend=None, add_scale=1.0, out_dtype=F32, side=None):
    tp = pairs[0][0].shape[0]
    tn = tn or n_out
    in_specs, args = [], []
    for a, b, nt, kb in pairs:
        k = a.shape[1]
        in_specs.append(pl.BlockSpec((TM, k), lambda i, j: (i, 0)))
        if nt and b.ndim == 3:
            in_specs.append(pl.BlockSpec((2, tn, k // 2), functools.partial(lambda i, j, kb: (kb, j, 0), kb=kb)))
        elif nt:
            in_specs.append(pl.BlockSpec((tn, k), functools.partial(lambda i, j, kb: (j, kb), kb=kb)))
        else:
            in_specs.append(pl.BlockSpec((k, tn), lambda i, j: (0, j)))
        args += [a, b]
    if addend is not None:
        in_specs.append(pl.BlockSpec((TM, tn), lambda i, j: (i, j)))
        args.append(addend)
    n_pairs = len(pairs)
    nts = [p[2] for p in pairs]

    def body(refs, out_refs, scratch):
        o_ref = out_refs[0]
        acc = None
        for p in range(n_pairs):
            a = refs[2 * p][...].astype(BF)
            b = refs[2 * p + 1][...]
            if b.ndim == 3:
                b = jnp.concatenate([b[0], b[1]], axis=1)
            d = lax.dot_general(a, b, NT if nts[p] else (((1,), (0,)), ((), ())), preferred_element_type=F32)
            acc = d if acc is None else acc + d
        if addend is not None:
            acc = acc + add_scale * refs[2 * n_pairs][...].astype(F32)
        o_ref[...] = acc.astype(o_ref.dtype)

    out = _side_call(
        body, side, name=name, grid=(tp // TM, n_out // tn), in_specs=in_specs,
        out_specs=[pl.BlockSpec((TM, tn), lambda i, j: (i, j))], out_shape=[_sds((tp, n_out), out_dtype)],
        scratch_shapes=[], args=args,
    )
    return out if side is not None else out[0]


def tn_mm(a, b, *, tm, name, out_dtype=BF, shard=None, slot0=0, dst=None):
    tp, m = a.shape
    n = b.shape[1]
    nk = tp // TM
    if shard is None:
        pieces, out_block, out_index, out_full = 1, (tm, n), (lambda i, k: (i, 0)), (m, n)
    elif shard[0] == "cols":
        pieces = n // shard[1]
        out_block, out_full = (pieces, tm, shard[1]), (4, m, shard[1])
        out_index = lambda i, k: (slot0 // pieces, i, 0)
    else:
        pieces = tm // shard[1]
        out_block, out_full = (pieces, shard[1], n), (4, m // 4, n)
        out_index = lambda i, k: (i, 0, 0)

    def body(a_ref, b_ref, *rest):
        o_ref, acc_ref = rest[-2], rest[-1]
        k = pl.program_id(1)

        @pl.when(k == 0)
        def _():
            acc_ref[...] = jnp.zeros_like(acc_ref)

        acc_ref[...] += lax.dot_general(a_ref[...].astype(BF), b_ref[...].astype(BF), TN, preferred_element_type=F32)

        @pl.when(k == nk - 1)
        def _():
            if shard is None:
                o_ref[...] = acc_ref[...].astype(o_ref.dtype)
            elif shard[0] == "cols":
                for j in range(pieces):
                    o_ref[j] = acc_ref[:, j * shard[1]:(j + 1) * shard[1]].astype(o_ref.dtype)
            else:
                for j in range(pieces):
                    o_ref[j] = acc_ref[j * shard[1]:(j + 1) * shard[1], :].astype(o_ref.dtype)

    in_specs = [pl.BlockSpec((TM, tm), lambda i, k: (k, i)), pl.BlockSpec((TM, n), lambda i, k: (k, 0))]
    args = [a, b]
    aliases = {}
    if dst is not None:
        in_specs.append(pl.BlockSpec(memory_space=pl.ANY))
        args.append(dst)
        aliases = {2: 0}
    return _pcall(
        body, name=name, grid=(m // tm, nk), in_specs=in_specs, out_specs=pl.BlockSpec(out_block, out_index),
        out_shape=_sds(out_full, out_dtype), input_output_aliases=aliases,
        scratch_shapes=[pltpu.VMEM((tm, n), F32)], compiler_params=_params(2),
    )(*args)


def _ln_store(z, g_ref, b_ref, xh_ref, rs_ref, hb_ref):
    mu = jnp.mean(z, axis=-1, keepdims=True)
    zc = z - mu
    var = jnp.mean(zc * zc, axis=-1, keepdims=True)
    rstd = lax.rsqrt(var + LN_EPS)
    xh = zc * rstd
    xh_ref[...] = xh
    rs_ref[...] = rstd
    hb_ref[...] = (xh * g_ref[...] + b_ref[...]).astype(BF)


def _ln_out(tp, tm=TM):
    specs = [pl.BlockSpec((tm, D_MODEL), lambda i: (i, 0)), pl.BlockSpec((tm, 1), lambda i: (i, 0)),
             pl.BlockSpec((tm, D_MODEL), lambda i: (i, 0))]
    shapes = [_sds((tp, D_MODEL), F32), _sds((tp, 1), F32), _sds((tp, D_MODEL), BF)]
    return specs, shapes


def _row_vec(n):
    return pl.BlockSpec((1, n), lambda i: (0, 0))


def ffn_up(hb, wup, *, name, side=None):
    tp = hb.shape[0]
    tn = D_FF // 2
    nj = D_FF // tn

    def body(in_refs, out_refs, scratch):
        h_ref, wg_ref, wu_ref = in_refs
        g_ref, u_ref, a_ref = out_refs
        h = h_ref[...]
        g = jnp.dot(h, wg_ref[0], preferred_element_type=F32)
        u = jnp.dot(h, wu_ref[0], preferred_element_type=F32)
        g_ref[...] = g.astype(BF)
        u_ref[...] = u.astype(BF)
        a_ref[...] = (g * jax.nn.sigmoid(g) * u).astype(BF)

    blk = pl.BlockSpec((TM, tn), lambda i, j: (i, j))
    return _side_call(
        body, side, name=name, grid=(tp // TM, nj),
        in_specs=[pl.BlockSpec((TM, D_MODEL), lambda i, j: (i, 0)), pl.BlockSpec((1, D_MODEL, tn), lambda i, j: (j, 0, 0)),
                  pl.BlockSpec((1, D_MODEL, tn), lambda i, j: (j + nj, 0, 0))],
        out_specs=[blk, blk, blk], out_shape=[_sds((tp, D_FF), BF)] * 3, scratch_shapes=[], args=(hb, wup, wup),
    )


def down_ln(a, wd, xprev, gp, bp, g, b, *, name):
    tp = a.shape[0]

    def body(a_ref, wd_ref, xp_ref, gp_ref, bp_ref, g_ref, b_ref, xh_ref, rs_ref, hb_ref):
        wd = jnp.concatenate([wd_ref[j] for j in range(4)], axis=0)
        f = jnp.dot(a_ref[...], wd, preferred_element_type=F32)
        hprev = xp_ref[...] * gp_ref[...] + bp_ref[...]
        _ln_store(ALPHA * hprev + 0.5 * f, g_ref, b_ref, xh_ref, rs_ref, hb_ref)

    out_specs, out_shape = _ln_out(tp)
    return _pcall(
        body, name=name, grid=(tp // TM,),
        in_specs=[pl.BlockSpec((TM, D_FF), lambda i: (i, 0)), pl.BlockSpec((4, D_FF // 4, D_MODEL), lambda i: (0, 0, 0)),
                  pl.BlockSpec((TM, D_MODEL), lambda i: (i, 0))] + [_row_vec(D_MODEL)] * 4,
        out_specs=out_specs, out_shape=out_shape, compiler_params=_params(1),
    )(a, wd, xprev, gp, bp, g, b)


def _rope(x, c, s1, s2, reps):
    n = x.shape[1]
    half = QK_ROPE // 2
    if reps > 1:
        c, s1, s2 = (jnp.tile(t, (1, reps)) for t in (c, s1, s2))
    return x * c + pltpu.roll(x, half, 1) * s1 + pltpu.roll(x, n - half, 1) * s2


def _rms(x, g):
    r = lax.rsqrt(jnp.mean(x * x, axis=-1, keepdims=True) + RMS_EPS)
    return x * r * g, r


def mla_prep(p, gq, gkv, wuq_p, wukv, tabs, *, name):
    tp = p.shape[0]
    nh = MLA_HEADS

    def body(cq_ref, ckv_ref, kr_ref, gq_ref, gkv_ref, wuq_ref, wukv_ref, cq_t, s1q_t, s2q_t, ck_t, s1k_t, s2k_t,
             cqn_ref, ckvn_ref, q2_ref, kv_ref, krr_ref):
        cqn, _ = _rms(cq_ref[...].astype(F32), gq_ref[...])
        ckvn, _ = _rms(ckv_ref[...].astype(F32), gkv_ref[...])
        cqn = cqn.astype(BF)
        ckvn = ckvn.astype(BF)
        cqn_ref[...] = cqn
        ckvn_ref[...] = ckvn
        q = jnp.dot(cqn, wuq_ref[...], preferred_element_type=F32)
        q2_ref[...] = _rope(q, cq_t[...], s1q_t[...], s2q_t[...], nh).astype(BF)
        kv_ref[...] = jnp.dot(ckvn, wukv_ref[...], preferred_element_type=F32).astype(BF)
        krr_ref[...] = _rope(kr_ref[...].astype(F32), ck_t[...], s1k_t[...], s2k_t[...], 1).astype(BF)

    def rows(n, col=0):
        return pl.BlockSpec((TMH, n), functools.partial(lambda i, col: (i, col), col=col))

    return _pcall(
        body, name=name, grid=(tp // TMH,),
        in_specs=[rows(Q_LORA, COL_CQ // Q_LORA), rows(KV_LORA, COL_CKV // KV_LORA), rows(LANES, COL_KR // LANES),
                  _row_vec(Q_LORA), _row_vec(KV_LORA),
                  pl.BlockSpec((Q_LORA, nh * Q_PAD), lambda i: (0, 0)), pl.BlockSpec((KV_LORA, nh * KV_PAD), lambda i: (0, 0)),
                  rows(Q_PAD), rows(Q_PAD), rows(Q_PAD), rows(LANES), rows(LANES), rows(LANES)],
        out_specs=[rows(Q_LORA), rows(KV_LORA), rows(nh * Q_PAD), rows(nh * KV_PAD), rows(LANES)],
        out_shape=[_sds((tp, Q_LORA), BF), _sds((tp, KV_LORA), BF), _sds((tp, nh * Q_PAD), BF),
                   _sds((tp, nh * KV_PAD), BF), _sds((tp, LANES), BF)],
        compiler_params=_params(1),
    )(p, p, p, gq, gkv, wuq_p, wukv, *tabs)


def _causal_mask(s):
    qpos = lax.broadcasted_iota(jnp.int32, (TM, TM), 0)
    kpos = lax.broadcasted_iota(jnp.int32, (TM, TM), 1)
    return jnp.where(kpos <= qpos, s, NEG_INF)


def _key_rows(k):
    return pl.ds(pl.multiple_of(k * TM, TM), TM)


def _pipelined_key_blocks(n, prefetch, process):
    prefetch(0, 0)

    def pair(j, carry):
        prefetch(2 * j + 1, 1)
        process(2 * j, 0, False)
        prefetch(2 * j + 2, 0)
        process(2 * j + 1, 1, False)
        return carry

    lax.fori_loop(0, n // 2, pair, 0)

    @pl.when(n % 2 == 1)
    def _():
        prefetch(n, 1)
        process(n - 1, 0, False)
        process(n, 1, True)

    @pl.when(n % 2 == 0)
    def _():
        process(n, 0, True)


def _side_call(body_main, side, *, name, grid, in_specs, out_specs, out_shape, scratch_shapes, args):
    n_in, n_out, n_scr = len(in_specs), len(out_specs), len(scratch_shapes)
    s_in, s_pre, n_sems, program = side if side is not None else ((), (), 0, None)
    a, b = len(s_in), len(s_pre)
    s_alias = [p for p in s_pre if not isinstance(p, jax.ShapeDtypeStruct)]
    assert len(s_alias) in (0, b)
    c = len(s_alias)

    def body(*refs):
        in_refs = refs[:n_in]
        out_refs = refs[n_in + a + c:n_in + a + c + n_out]
        scr = refs[n_in + a + c + b + n_out:n_in + a + c + b + n_out + n_scr]
        if side is not None:
            side_in = refs[n_in:n_in + a]
            side_out = refs[n_in + a + c + n_out:n_in + a + c + b + n_out]
            start, finish = program(side_in, side_out, refs[-2], refs[-1])

            @pl.when((pl.program_id(0) == 0) & (pl.program_id(1) == 0))
            def _():
                start()

        body_main(in_refs, out_refs, scr)
        if side is not None:
            @pl.when((pl.program_id(0) == grid[0] - 1) & (pl.program_id(1) == grid[1] - 1))
            def _():
                finish()

    sems = [pltpu.SemaphoreType.DMA((n_sems,))] * 2 if side is not None else []
    return _pcall(
        body, name=name, grid=grid, in_specs=list(in_specs) + [HBM_SPEC] * (a + c), out_specs=list(out_specs) + [HBM_SPEC] * b,
        out_shape=list(out_shape) + [_sds(p.shape, p.dtype) for p in s_pre],
        input_output_aliases={n_in + a + i: n_out + i for i in range(c)},
        scratch_shapes=list(scratch_shapes) + sems, compiler_params=_params(2),
    )(*args, *s_in, *s_alias)


def attn_fwd(q2, kv, kr, *, name, side=None):
    tp = q2.shape[0]
    nh = MLA_HEADS
    nb = tp // TM
    rep = TM // LANES

    def body(in_refs, out_refs, scratch):
        q_ref, kv_ref, kr_ref = in_refs
        o_ref, lse_ref = out_refs
        m_ref, l_ref, acc_ref, s0_ref, s1_ref, p_ref, alpha_ref = scratch
        qi = pl.program_id(1)
        s_refs = (s0_ref, s1_ref)
        m_ref[...] = jnp.full_like(m_ref, NEG_INF)
        l_ref[...] = jnp.zeros_like(l_ref)
        acc_ref[...] = jnp.zeros_like(acc_ref)

        def prefetch(k, slot):
            k2 = jnp.concatenate([kv_ref[_key_rows(k), :], kr_ref[_key_rows(k), :]], axis=1)
            s_refs[slot][...] = lax.dot_general(q_ref[...], k2, NT, preferred_element_type=F32)

        def process(k, slot, diagonal):
            for r in range(TM // LANES):
                rows = slice(r * LANES, (r + 1) * LANES)
                s = s_refs[slot][rows, :]
                if diagonal:
                    qpos = r * LANES + lax.broadcasted_iota(jnp.int32, (LANES, TM), 0)
                    s = jnp.where(lax.broadcasted_iota(jnp.int32, (LANES, TM), 1) <= qpos, s, NEG_INF)
                m_prev = m_ref[rows, :]
                m_new = jnp.maximum(m_prev, jnp.max(s, axis=1, keepdims=True))
                alpha = jnp.exp2(m_prev - m_new)
                p = jnp.exp2(s - jnp.tile(m_new, (1, rep)))
                lane_sums = p[:, 0:LANES]
                for t in range(1, rep):
                    lane_sums = lane_sums + p[:, t * LANES:(t + 1) * LANES]
                l_ref[rows, :] = alpha * l_ref[rows, :] + lane_sums
                p_ref[rows, :] = p.astype(BF)
                alpha_ref[rows, :] = alpha
                m_ref[rows, :] = m_new
            acc_ref[...] = alpha_ref[...] * acc_ref[...] + jnp.dot(p_ref[...], kv_ref[_key_rows(k), :], preferred_element_type=F32)

        _pipelined_key_blocks(qi, prefetch, process)
        l = jnp.sum(l_ref[...], axis=1, keepdims=True)
        o_ref[...] = (acc_ref[...] / l).astype(BF)
        lse_ref[...] = m_ref[...] + jnp.log2(l)

    return _side_call(
        body, side, name=name, grid=(nh, nb),
        in_specs=[pl.BlockSpec((TM, Q_PAD), lambda h, qi: (qi, h)), pl.BlockSpec((tp, KV_PAD), lambda h, qi: (0, h)),
                  pl.BlockSpec((tp, LANES), lambda h, qi: (0, 0))],
        out_specs=[pl.BlockSpec((TM, KV_PAD), lambda h, qi: (qi, h)), pl.BlockSpec((TM, LANES), lambda h, qi: (qi, h))],
        out_shape=[_sds((tp, nh * KV_PAD), BF), _sds((tp, nh * LANES), F32)],
        scratch_shapes=[pltpu.VMEM((TM, LANES), F32)] * 3 + [pltpu.VMEM((TM, TM), F32)] * 2
        + [pltpu.VMEM((TM, TM), BF), pltpu.VMEM((TM, LANES), F32)], args=(q2, kv, kr),
    )


def conv_fwd(p, w, *, name):
    tp = p.shape[0]

    def body(b_ref, c_ref, h_ref, w_ref, y_ref, cv_ref, ebuf):
        i = pl.program_id(0)

        @pl.when(i == 0)
        def _():
            ebuf[0:8, :] = jnp.zeros((8, D_CONV), F32)

        e = c_ref[...].astype(F32) * h_ref[...].astype(F32)
        ebuf[8:8 + TM, :] = e
        w_all = w_ref[...]
        conv = w_all[0:1] * ebuf[pl.ds(6, TM), :] + w_all[1:2] * ebuf[pl.ds(7, TM), :] + w_all[2:3] * e
        cv_ref[...] = conv.astype(BF)
        y_ref[...] = (b_ref[...].astype(F32) * conv).astype(BF)
        ebuf[0:8, :] = ebuf[TM:TM + 8, :]

    def col(j):
        return pl.BlockSpec((TM, D_CONV), functools.partial(lambda i, j: (i, j), j=j))

    return _pcall(
        body, name=name, grid=(tp // TM,),
        in_specs=[col(0), col(1), col(2), pl.BlockSpec((3, D_CONV), lambda i: (0, 0))],
        out_specs=[col(0), col(0)], out_shape=[_sds((tp, D_CONV), BF)] * 2,
        scratch_shapes=[pltpu.VMEM((TM + 8, D_CONV), F32)], compiler_params=_params(1),
    )(p, p, p, w)


def merge_out_ln(ycv, o2, p, bg, wbc, wbm_p, wo, xprev, gp, bp, g, b, *, name):
    tp = ycv.shape[0]

    def body(y_ref, o_ref, gc_ref, gm_ref, bg_ref, wbc_ref, wbm_ref, wo_ref, xp_ref, gp_ref, bp_ref, g_ref, b_ref,
             bc_ref, bm_ref, mg_ref, xh_ref, rs_ref, hb_ref):
        bc = jnp.dot(y_ref[...], wbc_ref[...], preferred_element_type=F32)
        bm = jnp.dot(o_ref[...], wbm_ref[...], preferred_element_type=F32)
        bgv = bg_ref[...]
        mg = (jax.nn.sigmoid(gc_ref[...].astype(F32) + bgv[0:1]) * bc
              + jax.nn.sigmoid(gm_ref[...].astype(F32) + bgv[1:2]) * bm)
        mgb = mg.astype(BF)
        bc_ref[...] = bc.astype(BF)
        bm_ref[...] = bm.astype(BF)
        mg_ref[...] = mgb
        mix = jnp.dot(mgb, wo_ref[...], preferred_element_type=F32)
        hprev = xp_ref[...] * gp_ref[...] + bp_ref[...]
        _ln_store(ALPHA * hprev + mix, g_ref, b_ref, xh_ref, rs_ref, hb_ref)

    def rows(n, col=0):
        return pl.BlockSpec((TMH, n), functools.partial(lambda i, col: (i, col), col=col))

    def whole(r, c):
        return pl.BlockSpec((r, c), lambda i: (0, 0))

    ln_specs, ln_shapes = _ln_out(tp, TMH)
    return _pcall(
        body, name=name, grid=(tp // TMH,),
        in_specs=[rows(D_CONV), rows(MLA_HEADS * KV_PAD), rows(D_MODEL, COL_GATES // D_MODEL), rows(D_MODEL, COL_GATES // D_MODEL + 1),
                  whole(2, D_MODEL),
                  whole(D_CONV, D_MODEL), whole(MLA_HEADS * KV_PAD, D_MODEL), whole(D_MODEL, D_MODEL), rows(D_MODEL)]
        + [_row_vec(D_MODEL)] * 4,
        out_specs=[rows(D_MODEL)] * 3 + ln_specs, out_shape=[_sds((tp, D_MODEL), BF)] * 3 + ln_shapes,
        compiler_params=_params(1),
    )(ycv, o2, p, p, bg, wbc, wbm_p, wo, xprev, gp, bp, g, b)


def ln_bwd(dh, xh, rstd, g, *, branch_scale, name):
    tp = xh.shape[0]
    from_loss = isinstance(dh, tuple)

    def body(*refs):
        if from_loss:
            xh_ref, rs_ref, g_ref, b_ref, t_ref, dzb_ref, dg_ref, db_ref, loss_ref = refs
        else:
            dh_ref, xh_ref, rs_ref, g_ref, dzb_ref, dg_ref, db_ref = refs
        i = pl.program_id(0)

        @pl.when(i == 0)
        def _():
            dg_ref[...] = jnp.zeros_like(dg_ref)
            db_ref[...] = jnp.zeros_like(db_ref)
            if from_loss:
                loss_ref[...] = jnp.zeros_like(loss_ref)

        xhat = xh_ref[...]
        if from_loss:
            row = i * TM + lax.broadcasted_iota(jnp.int32, (TM, 1), 0)
            real = (row >= N_META) & (row < N_META + dh[2])
            diff = jnp.where(real, xhat * g_ref[...] + b_ref[...] - t_ref[...], 0.0)
            loss_ref[...] += 0.5 / D_MODEL * jnp.sum(diff * diff)
            dy = diff * (1.0 / D_MODEL)
        else:
            dy = dh_ref[...]
        dg_ref[...] += jnp.sum(dy * xhat, axis=0, keepdims=True)
        db_ref[...] += jnp.sum(dy, axis=0, keepdims=True)
        dxh = dy * g_ref[...]
        m1 = jnp.mean(dxh, axis=-1, keepdims=True)
        m2 = jnp.mean(dxh * xhat, axis=-1, keepdims=True)
        dz = rs_ref[...] * (dxh - m1 - xhat * m2)
        dzb_ref[...] = (branch_scale * dz).astype(BF)

    rows = pl.BlockSpec((TM, D_MODEL), lambda i: (i, 0))
    stat = pl.BlockSpec((TM, 1), lambda i: (i, 0))
    vec = _row_vec(D_MODEL)
    out_specs = [rows, vec, vec]
    out_shape = [_sds((tp, D_MODEL), BF), _sds((1, D_MODEL), F32), _sds((1, D_MODEL), F32)]
    if from_loss:
        in_specs, args = [rows, stat, vec, vec, rows], (xh, rstd, g, dh[0], dh[1])
        out_specs.append(pl.BlockSpec((8, LANES), lambda i: (0, 0)))
        out_shape.append(_sds((8, LANES), F32))
    else:
        in_specs, args = [rows, rows, stat, vec], (dh, xh, rstd, g)
    return _pcall(body, name=name, grid=(tp // TM,), in_specs=in_specs, out_specs=out_specs, out_shape=out_shape,
                  compiler_params=_params(1))(*args)


def ffn_down_bwd(dzb, wd, gate, up, *, name, side=None):
    tp = dzb.shape[0]
    tn = D_FF // 2

    def body(in_refs, out_refs, scratch):
        dz_ref, wd_ref, g_ref, u_ref = in_refs
        dg_ref, du_ref = out_refs
        wd = jnp.concatenate([wd_ref[0], wd_ref[1]], axis=0)
        da = lax.dot_general(dz_ref[...], wd, NT, preferred_element_type=F32)
        g = g_ref[...].astype(F32)
        u = u_ref[...].astype(F32)
        sg = jax.nn.sigmoid(g)
        dg_ref[...] = (da * u * sg * (1.0 + g * (1.0 - sg))).astype(BF)
        du_ref[...] = (da * g * sg).astype(BF)

    blk = pl.BlockSpec((TM, tn), lambda i, j: (i, j))
    return _side_call(
        body, side, name=name, grid=(tp // TM, D_FF // tn),
        in_specs=[pl.BlockSpec((TM, D_MODEL), lambda i, j: (i, 0)), pl.BlockSpec((2, tn // 2, D_MODEL), lambda i, j: (j, 0, 0)), blk, blk],
        out_specs=[blk, blk], out_shape=[_sds((tp, D_FF), BF)] * 2, scratch_shapes=[], args=(dzb, wd, gate, up),
    )


def merge_bwd(dzb, wo, bc, bm, p, bg, wbc, wbm_p, o2, *, name):
    tp = dzb.shape[0]
    nh = MLA_HEADS

    def body(dz_ref, wo_ref, bc_ref, bm_ref, gc_ref, gm_ref, bg_ref, wbc_ref, wbm_ref, o_ref,
             dbc_ref, dbm_ref, dgg_ref, dy_ref, do_ref, dl_ref, dbg_ref):
        i = pl.program_id(0)

        @pl.when(i == 0)
        def _():
            dbg_ref[...] = jnp.zeros_like(dbg_ref)

        dmg = lax.dot_general(dz_ref[...], wo_ref[...], NT, preferred_element_type=F32)
        bgv = bg_ref[...]
        sc = jax.nn.sigmoid(gc_ref[...].astype(F32) + bgv[0:1])
        sm = jax.nn.sigmoid(gm_ref[...].astype(F32) + bgv[1:2])
        dbc = (dmg * sc).astype(BF)
        dbm = (dmg * sm).astype(BF)
        dgc = dmg * bc_ref[...].astype(F32) * sc * (1.0 - sc)
        dgm = dmg * bm_ref[...].astype(F32) * sm * (1.0 - sm)
        dbc_ref[...] = dbc
        dbm_ref[...] = dbm
        dgg_ref[...] = jnp.concatenate([dgc, dgm], axis=1).astype(BF)
        dbg_ref[...] += jnp.concatenate([jnp.sum(dgc, axis=0, keepdims=True), jnp.sum(dgm, axis=0, keepdims=True)], axis=0)
        dy_ref[...] = lax.dot_general(dbc, wbc_ref[...], NT, preferred_element_type=F32)
        do = lax.dot_general(dbm, wbm_ref[...], NT, preferred_element_type=F32)
        do_ref[...] = do.astype(BF)
        prod = do * o_ref[...].astype(F32)
        parts = []
        for h in range(nh):
            d = jnp.sum(prod[:, h * KV_PAD:(h + 1) * KV_PAD], axis=1, keepdims=True)
            parts.append(jnp.broadcast_to(d, (TMH, LANES)))
        dl_ref[...] = jnp.concatenate(parts, axis=1)

    def rows(n, col=0):
        return pl.BlockSpec((TMH, n), functools.partial(lambda i, col: (i, col), col=col))

    def whole(r, c):
        return pl.BlockSpec((r, c), lambda i: (0, 0))

    return _pcall(
        body, name=name, grid=(tp // TMH,),
        in_specs=[rows(D_MODEL), whole(D_MODEL, D_MODEL), rows(D_MODEL), rows(D_MODEL), rows(D_MODEL, COL_GATES // D_MODEL),
                  rows(D_MODEL, COL_GATES // D_MODEL + 1),
                  whole(2, D_MODEL), whole(D_CONV, D_MODEL), whole(nh * KV_PAD, D_MODEL), rows(nh * KV_PAD)],
        out_specs=[rows(D_MODEL), rows(D_MODEL), rows(2 * D_MODEL, COL_GATES // (2 * D_MODEL)), rows(D_CONV), rows(nh * KV_PAD),
                   rows(nh * LANES),
                   whole(2, D_MODEL)],
        out_shape=[_sds((tp, D_MODEL), BF), _sds((tp, D_MODEL), BF), _sds((tp, D_IN_PAD), BF), _sds((tp, D_CONV), F32),
                   _sds((tp, nh * KV_PAD), BF), _sds((tp, nh * LANES), F32), _sds((2, D_MODEL), F32)],
        compiler_params=_params(1),
    )(dzb, wo, bc, bm, p, p, bg, wbc, wbm_p, o2)


def attn_bwd(q2, kv, kr, do2, lse, dl, *, name, side=None):
    tp = q2.shape[0]
    nh = MLA_HEADS
    nb = tp // TM
    rep = TM // LANES

    def body(in_refs, out_refs, scratch):
        q_ref, kv_ref, kr_ref, do_ref, lse_ref, dl_ref = in_refs
        dq_ref, dkv_ref, dkr_ref = out_refs
        dq_acc, s0_ref, s1_ref, dp0_ref, dp1_ref = scratch
        qi = pl.program_id(1)
        s_refs, dp_refs = (s0_ref, s1_ref), (dp0_ref, dp1_ref)

        @pl.when(qi == 0)
        def _():
            dkv_ref[...] = jnp.zeros_like(dkv_ref)

        @pl.when((qi == 0) & (pl.program_id(0) == 0))
        def _():
            dkr_ref[...] = jnp.zeros_like(dkr_ref)

        dq_acc[...] = jnp.zeros_like(dq_acc)

        def prefetch(k, slot):
            kvb = kv_ref[_key_rows(k), :]
            k2 = jnp.concatenate([kvb, kr_ref[_key_rows(k), :]], axis=1)
            s_refs[slot][...] = lax.dot_general(q_ref[...], k2, NT, preferred_element_type=F32)
            dp_refs[slot][...] = lax.dot_general(do_ref[...], kvb, NT, preferred_element_type=F32)

        def process(k, slot, diagonal):
            rows = _key_rows(k)
            s = s_refs[slot][...]
            if diagonal:
                s = _causal_mask(s)
            p = jnp.exp2(s - jnp.tile(lse_ref[...], (1, rep)))
            dsb = (p * (dp_refs[slot][...] - jnp.tile(dl_ref[...], (1, rep)))).astype(BF)
            dk2 = lax.dot_general(dsb, q_ref[...], TN, preferred_element_type=F32) * LN2
            dkv_ref[rows, :] += lax.dot_general(p.astype(BF), do_ref[...], TN, preferred_element_type=F32) + dk2[:, :KV_PAD]
            dkr_ref[rows, :] += dk2[:, KV_PAD:KV_PAD + LANES]
            k2 = jnp.concatenate([kv_ref[rows, :], kr_ref[rows, :]], axis=1)
            dq_acc[...] += jnp.dot(dsb, k2, preferred_element_type=F32)

        _pipelined_key_blocks(qi, prefetch, process)
        dq_ref[...] = dq_acc[...].astype(BF)

    def qrow(n):
        return pl.BlockSpec((TM, n), lambda h, qi: (qi, h))

    def head(n):
        return pl.BlockSpec((tp, n), lambda h, qi: (0, h))

    return _side_call(
        body, side, name=name, grid=(nh, nb),
        in_specs=[qrow(Q_PAD), head(KV_PAD), pl.BlockSpec((tp, LANES), lambda h, qi: (0, 0)), qrow(KV_PAD), qrow(LANES), qrow(LANES)],
        out_specs=[qrow(Q_PAD), head(KV_PAD), pl.BlockSpec((tp, LANES), lambda h, qi: (0, 0))],
        out_shape=[_sds((tp, nh * Q_PAD), BF), _sds((tp, nh * KV_PAD), F32), _sds((tp, LANES), F32)],
        scratch_shapes=[pltpu.VMEM((TM, Q_PAD), F32)] + [pltpu.VMEM((TM, TM), F32)] * 4, args=(q2, kv, kr, do2, lse, dl),
    )


def _rms_bwd(x, g, dy):
    r = lax.rsqrt(jnp.mean(x * x, axis=-1, keepdims=True) + RMS_EPS)
    gy = dy * g
    dx = r * gy - x * (r * r * r) * jnp.mean(x * gy, axis=-1, keepdims=True)
    return dx, jnp.sum(dy * x * r, axis=0, keepdims=True)


def mla_prep_bwd(dq2, dkv, dkr, p, gq, gkv, wuq_p, wukv, tabs_bwd, dp, *, name):
    tp = dq2.shape[0]
    nh = MLA_HEADS

    def body(dq_ref, dkv_ref, dkr_ref, cq_ref, ckv_ref, gq_ref, gkv_ref, wuq_ref, wukv_ref,
             cq_t, s1q_t, s2q_t, ck_t, s1k_t, s2k_t, dp_in_ref, dqb_ref, dsm_ref, dgq_ref, dgkv_ref):
        i = pl.program_id(0)

        @pl.when(i == 0)
        def _():
            dgq_ref[...] = jnp.zeros_like(dgq_ref)
            dgkv_ref[...] = jnp.zeros_like(dgkv_ref)

        dqb = _rope(dq_ref[...].astype(F32), cq_t[...], s1q_t[...], s2q_t[...], nh).astype(BF)
        dqb_ref[...] = dqb
        dcqn = lax.dot_general(dqb, wuq_ref[...], NT, preferred_element_type=F32)
        dcq, dgq = _rms_bwd(cq_ref[...].astype(F32), gq_ref[...], dcqn)
        dckvn = lax.dot_general(dkv_ref[...].astype(BF), wukv_ref[...], NT, preferred_element_type=F32)
        dckv, dgkv = _rms_bwd(ckv_ref[...].astype(F32), gkv_ref[...], dckvn)
        dkr = _rope(dkr_ref[...], ck_t[...], s1k_t[...], s2k_t[...], 1)
        dsm_ref[...] = jnp.concatenate([dcq, dckv, dkr], axis=1).astype(BF)
        dgq_ref[...] += dgq
        dgkv_ref[...] += dgkv

    def rows(n, col=0):
        return pl.BlockSpec((TMH, n), functools.partial(lambda i, col: (i, col), col=col))

    return _pcall(
        body, name=name, grid=(tp // TMH,),
        in_specs=[rows(nh * Q_PAD), rows(nh * KV_PAD), rows(LANES), rows(Q_LORA, COL_CQ // Q_LORA), rows(KV_LORA, COL_CKV // KV_LORA),
                  _row_vec(Q_LORA), _row_vec(KV_LORA),
                  pl.BlockSpec((Q_LORA, nh * Q_PAD), lambda i: (0, 0)), pl.BlockSpec((KV_LORA, nh * KV_PAD), lambda i: (0, 0)),
                  rows(Q_PAD), rows(Q_PAD), rows(Q_PAD), rows(LANES), rows(LANES), rows(LANES), pl.BlockSpec(memory_space=pl.ANY)],
        out_specs=[rows(nh * Q_PAD), rows(COL_GATES - COL_CQ, COL_CQ // (COL_GATES - COL_CQ)), _row_vec(Q_LORA),
                   _row_vec(KV_LORA)],
        out_shape=[_sds((tp, nh * Q_PAD), BF), _sds(dp.shape, dp.dtype), _sds((1, Q_LORA), F32), _sds((1, KV_LORA), F32)],
        input_output_aliases={15: 1}, compiler_params=_params(1),
    )(dq2, dkv, dkr, p, p, gq, gkv, wuq_p, wukv, *tabs_bwd, dp)


def conv_bwd(dy, p, conv, w, dp, *, name):
    tp = dy.shape[0]
    nb = tp // TM

    def body(dy_ref, b_ref, c_ref, h_ref, cv_ref, w_ref, dp_in_ref, dp_ref, dw0_ref, dw1_ref, dw2_ref, dbuf):
        i = pl.program_id(0)

        @pl.when(i == 0)
        def _():
            dbuf[TM:TM + 8, :] = jnp.zeros((8, D_CONV), F32)
            dw0_ref[...] = jnp.zeros_like(dw0_ref)
            dw1_ref[...] = jnp.zeros_like(dw1_ref)
            dw2_ref[...] = jnp.zeros_like(dw2_ref)

        dyv = dy_ref[...]
        c = c_ref[...].astype(F32)
        hh = h_ref[...].astype(F32)
        dconv = dyv * b_ref[...].astype(F32)
        dbuf[0:TM, :] = dconv
        d1 = dbuf[pl.ds(1, TM), :]
        d2 = dbuf[pl.ds(2, TM), :]
        w_all = w_ref[...]
        de = w_all[2:3] * dconv + w_all[1:2] * d1 + w_all[0:1] * d2
        e = c * hh
        dp_ref[...] = jnp.concatenate([dyv * cv_ref[...].astype(F32), de * hh, de * c], axis=1).astype(BF)
        dw0_ref[...] += jnp.sum(d2 * e, axis=0, keepdims=True)
        dw1_ref[...] += jnp.sum(d1 * e, axis=0, keepdims=True)
        dw2_ref[...] += jnp.sum(dconv * e, axis=0, keepdims=True)
        dbuf[TM:TM + 8, :] = dbuf[0:8, :]

    def col(j):
        return pl.BlockSpec((TM, D_CONV), functools.partial(lambda i, j: (nb - 1 - i, j), j=j))

    return _pcall(
        body, name=name, grid=(nb,),
        in_specs=[col(0), col(0), col(1), col(2), col(0), pl.BlockSpec((3, D_CONV), lambda i: (0, 0)),
                  pl.BlockSpec(memory_space=pl.ANY)],
        out_specs=[pl.BlockSpec((TM, 3 * D_CONV), lambda i: (nb - 1 - i, 0))] + [_row_vec(D_CONV)] * 3,
        out_shape=[_sds(dp.shape, dp.dtype)] + [_sds((1, D_CONV), F32)] * 3, input_output_aliases={6: 0},
        scratch_shapes=[pltpu.VMEM((TM + 8, D_CONV), F32)], compiler_params=_params(1),
    )(dy, p, p, p, conv, w, dp)


def adamw(w, g, m, v, *, name):
    r, c = w.shape
    tr = r
    for cand in (256, 128, 64, 32, 16, 8):
        if r % cand == 0 and r > cand:
            tr = cand
            break

    def body(w_ref, g_ref, m_ref, v_ref, d_ref, nm_ref, nv_ref):
        gv = g_ref[...]
        nm = ADAM_B1 * m_ref[...] + (1.0 - ADAM_B1) * gv
        nv = ADAM_B2 * v_ref[...] + (1.0 - ADAM_B2) * (gv * gv)
        m_hat = nm / (1.0 - ADAM_B1 ** ADAM_STEP)
        v_hat = nv / (1.0 - ADAM_B2 ** ADAM_STEP)
        d_ref[...] = -ADAM_LR * (m_hat / (jnp.sqrt(v_hat) + ADAM_EPS) + ADAM_WD * w_ref[...])
        nm_ref[...] = nm
        nv_ref[...] = nv

    blk = pl.BlockSpec((tr, c), lambda i: (i, 0))
    return _pcall(
        body, name=name, grid=(r // tr,), in_specs=[blk] * 4, out_specs=[blk] * 3,
        out_shape=[_sds((r, c), F32)] * 3, compiler_params=_params(1),
    )(w, g, m, v)


HBM_SPEC = pl.BlockSpec(memory_space=pltpu.HBM)


def _place():
    return lax.axis_index("x"), lax.axis_index("y"), lax.axis_index("c")


def _other_chips(x, y):
    return [(1 - x, y), (x, 1 - y), (1 - x, 1 - y)]


def _half(ref_or_shape_rows, c):
    return pl.ds(c * (ref_or_shape_rows // 2), ref_or_shape_rows // 2)


def gather_side(items):
    n = len(items)
    shards = [s for s, _ in items]
    layers = [l for _, l in items]

    def program(x_refs, o_refs, send_sems, recv_sems):
        x, y, c = _place()
        me = 2 * x + y
        chips = _other_chips(x, y)

        def copy(sem, src, dst, to):
            return pltpu.make_async_remote_copy(src_ref=src, dst_ref=dst, send_sem=send_sems.at[sem], recv_sem=recv_sems.at[sem],
                                                device_id=to, device_id_type=MESH)

        def src(i):
            return x_refs[i].at[layers[i], _half(x_refs[i].shape[1], c)]

        def dst(i, slot, cc):
            return o_refs[i].at[slot, _half(o_refs[i].shape[1], cc)]

        sends = [copy(6 * i + k, src(i), dst(i, me, c), (px, py, c)) for i in range(n) for k, (px, py) in enumerate(chips)]
        passed = [copy(6 * i + 3 + k, dst(i, 2 * px + py, c), dst(i, 2 * px + py, c), (x, y, 1 - c))
                  for k, (px, py) in enumerate(chips) for i in range(n)]

        def start():
            for cp in sends:
                cp.start()

        def finish():
            pos = 0
            for k, (px, py) in enumerate(chips):
                for i in range(n):
                    copy(6 * i + k, src(i), dst(i, 2 * px + py, c), (px, py, c)).wait_recv()
                    passed[pos].start()
                    pos += 1
            for k, (px, py) in enumerate(chips):
                for i in range(n):
                    copy(6 * i + 3 + k, dst(i, 2 * px + py, 1 - c), dst(i, 2 * px + py, 1 - c), (x, y, 1 - c)).wait_recv()
            for cp in sends + passed:
                cp.wait_send()

        return start, finish

    prefilled = [jnp.broadcast_to(s[l][None], (4,) + s.shape[1:]) for s, l in items]
    return shards, prefilled, 6 * n, program


def scatter_side(pss):
    n = len(pss)

    def program(p_refs, o_refs, send_sems, recv_sems):
        x, y, c = _place()
        me = 2 * x + y
        chips = _other_chips(x, y)

        def copy(i, k, j_src, j_dst, to):
            return pltpu.make_async_remote_copy(src_ref=p_refs[i].at[j_src], dst_ref=o_refs[i].at[j_dst],
                                                send_sem=send_sems.at[3 * i + k], recv_sem=recv_sems.at[3 * i + k],
                                                device_id=to, device_id_type=MESH)

        sends = [copy(i, k, 2 * px + py, me, (px, py, c)) for i in range(n) for k, (px, py) in enumerate(chips)]

        def start():
            for cp in sends:
                cp.start()

        def finish():
            for i in range(n):
                for k, (px, py) in enumerate(chips):
                    copy(i, k, me, 2 * px + py, (px, py, c)).wait_recv()
            for cp in sends:
                cp.wait_send()

        return start, finish

    xi, yi, _ = _place()
    own = jnp.arange(4)[:, None, None] == 2 * xi + yi
    prefilled = [jnp.where(own, p, jnp.zeros_like(p)) for p in pss]
    return list(pss), prefilled, 3 * n, program


def exchange_alone(side, *, name):
    inputs, prefilled, n_sems, program = side
    a, b = len(inputs), len(prefilled)

    def body(*refs):
        start, finish = program(refs[:a], refs[a + b:a + 2 * b], refs[-2], refs[-1])
        start()
        finish()

    return _pcall(
        body, name=name, in_specs=[HBM_SPEC] * (a + b), out_specs=[HBM_SPEC] * b, out_shape=[_sds(p.shape, p.dtype) for p in prefilled],
        input_output_aliases={a + i: i for i in range(b)}, scratch_shapes=[pltpu.SemaphoreType.DMA((n_sems,))] * 2,
    )(*inputs, *prefilled)


def pair_exchange(gs, *, name):
    n = len(gs)

    def body(*refs):
        g_refs, o_refs = refs[:n], refs[n:2 * n]
        send_sems, recv_sems = refs[2 * n:]
        x, y, c = _place()
        cps = [pltpu.make_async_remote_copy(src_ref=g_refs[i].at[:, _half(g_refs[i].shape[1], 1 - c)], dst_ref=o_refs[i],
                                            send_sem=send_sems.at[i], recv_sem=recv_sems.at[i], device_id=(x, y, 1 - c),
                                            device_id_type=MESH)
               for i in range(n)]
        for cp in cps:
            cp.start()
        for cp in cps:
            cp.wait()

    return _pcall(
        body, name=name, in_specs=[HBM_SPEC] * n, out_specs=[HBM_SPEC] * n,
        out_shape=[_sds((4, g.shape[1] // 2, g.shape[2]), g.dtype) for g in gs],
        scratch_shapes=[pltpu.SemaphoreType.DMA((n,)), pltpu.SemaphoreType.DMA((n,))],
    )(*gs)


def pair_exchange_side(gs):
    n = len(gs)

    def program(g_refs, o_refs, send_sems, recv_sems):
        x, y, c = _place()
        cps = [pltpu.make_async_remote_copy(src_ref=g_refs[i].at[:, _half(g_refs[i].shape[1], 1 - c)], dst_ref=o_refs[i],
                                            send_sem=send_sems.at[i], recv_sem=recv_sems.at[i], device_id=(x, y, 1 - c),
                                            device_id_type=MESH)
               for i in range(n)]

        def start():
            for cp in cps:
                cp.start()

        def finish():
            for cp in cps:
                cp.wait()

        return start, finish

    return list(gs), [_sds((4, g.shape[1] // 2, g.shape[2]), g.dtype) for g in gs], n, program


def _comm_rows(a, b, itemsize):
    return a // 2 if a * b * itemsize > (3 << 19) and a % 16 == 0 else a


def pair_add(g, s1, c_idx, *, name):
    n, a, b = g.shape
    ah = a // 2
    ta = _comm_rows(ah, b, 2)
    nblk = ah // ta

    def body(c_ref, g_ref, s_ref, o_ref):
        o_ref[...] = (g_ref[...].astype(F32) + s_ref[...].astype(F32)).astype(o_ref.dtype)

    grid_spec = pltpu.PrefetchScalarGridSpec(
        num_scalar_prefetch=1, grid=(n, nblk),
        in_specs=[pl.BlockSpec((1, ta, b), lambda j, i, c_ref: (j, c_ref[0] * nblk + i, 0)),
                  pl.BlockSpec((1, ta, b), lambda j, i, c_ref: (j, i, 0))],
        out_specs=pl.BlockSpec((1, ta, b), lambda j, i, c_ref: (j, i, 0)),
    )
    return _pcall(body, name=name, grid_spec=grid_spec, out_shape=_sds((n, ah, b), g.dtype), compiler_params=_params(2))(
        c_idx, g, s1)


def sum_chunks(s2, *, name):
    n, a, b = s2.shape
    ta = _comm_rows(a, b, 4)

    def body(s_ref, o_ref):
        acc = s_ref[0].astype(F32)
        for j in range(1, n):
            acc = acc + s_ref[j].astype(F32)
        o_ref[...] = acc

    return _pcall(
        body, name=name, grid=(a // ta,), in_specs=[pl.BlockSpec((n, ta, b), lambda i: (0, i, 0))],
        out_specs=pl.BlockSpec((ta, b), lambda i: (i, 0)), out_shape=_sds((a, b), F32), compiler_params=_params(1),
    )(s2)


def pair_gather_side(rcs):
    n = len(rcs)

    def program(r_refs, o_refs, send_sems, recv_sems):
        x, y, c = _place()

        def copy(i, half):
            return pltpu.make_async_remote_copy(src_ref=r_refs[i], dst_ref=o_refs[i].at[half], send_sem=send_sems.at[i],
                                                recv_sem=recv_sems.at[i], device_id=(x, y, 1 - c), device_id_type=MESH)

        sends = [copy(i, c) for i in range(n)]

        def start():
            for cp in sends:
                cp.start()

        def finish():
            for i in range(n):
                copy(i, 1 - c).wait_recv()
            for cp in sends:
                cp.wait_send()

        return start, finish

    prefilled = [jnp.broadcast_to(r[None], (2,) + r.shape) for r in rcs]
    return list(rcs), prefilled, n, program


def exchange_small(arrs, *, reduce, name):
    n = len(arrs)

    def body(*refs):
        v_refs, o_refs = refs[:n], refs[n:2 * n]
        bufs = refs[2 * n:3 * n] if reduce else o_refs
        send_sems, recv_sems = refs[-2:]
        x, y, c = _place()
        me = 4 * x + 2 * y + c
        for i in range(n):
            bufs[i][me] = v_refs[i][...]

        def peer(k):
            dx, dy, dc = (k >> 2) & 1, (k >> 1) & 1, k & 1
            return (1 - x if dx else x, 1 - y if dy else y, 1 - c if dc else c)

        def copy(i, k, slot):
            return pltpu.make_async_remote_copy(src_ref=v_refs[i], dst_ref=bufs[i].at[slot], send_sem=send_sems.at[7 * i + k - 1],
                                                recv_sem=recv_sems.at[7 * i + k - 1], device_id=peer(k), device_id_type=MESH)

        sends = [copy(i, k, me) for i in range(n) for k in range(1, 8)]
        for cp in sends:
            cp.start()
        for i in range(n):
            for k in range(1, 8):
                px, py, pc = peer(k)
                copy(i, k, 4 * px + 2 * py + pc).wait_recv()
        for cp in sends:
            cp.wait_send()
        if reduce:
            for i in range(n):
                acc = bufs[i][0]
                for d in range(1, 8):
                    acc = acc + bufs[i][d]
                o_refs[i][...] = acc

    vmem = pl.BlockSpec(memory_space=pltpu.VMEM)
    stacked = [(8,) + a.shape for a in arrs]
    return _pcall(
        body, name=name, in_specs=[vmem] * n, out_specs=[vmem] * n,
        out_shape=[_sds(a.shape if reduce else s, F32) for a, s in zip(arrs, stacked)],
        scratch_shapes=([pltpu.VMEM(s, F32) for s in stacked] if reduce else [])
        + [pltpu.SemaphoreType.DMA((7 * n,)), pltpu.SemaphoreType.DMA((7 * n,))],
    )(*arrs)


def _pad_rows(n, mult):
    return -(-n // mult) * mult


def _chip_major(g, b):
    return g.reshape(g.shape[0], 4, b).transpose(1, 0, 2)


def _rope_tables(tp):
    inv_freq = 1.0 / (ROPE_BASE ** (jnp.arange(0, QK_ROPE, 2, dtype=F32) / QK_ROPE))
    ang = jnp.arange(tp, dtype=F32)[:, None] * inv_freq[None, :]
    cos, sin = jnp.cos(ang), jnp.sin(ang)
    one = lambda n: jnp.ones((tp, n), F32)
    zero = lambda n: jnp.zeros((tp, n), F32)
    half = QK_ROPE // 2
    tail = Q_PAD - Q_ROPE_AT - QK_ROPE
    cq = jnp.concatenate([one(Q_ROPE_AT), cos, cos, one(tail)], axis=1)
    s1q = jnp.concatenate([zero(Q_ROPE_AT + half), sin, zero(tail)], axis=1)
    s2q = jnp.concatenate([zero(Q_ROPE_AT), -sin, zero(tail + half)], axis=1)
    ck = jnp.concatenate([cos, cos, zero(KV_PAD - QK_ROPE)], axis=1)
    s1k = jnp.concatenate([zero(half), sin, zero(KV_PAD - QK_ROPE)], axis=1)
    s2k = jnp.concatenate([-sin, zero(KV_PAD - half)], axis=1)
    fwd = (cq * (ATT_SCALE * LOG2E), s1q * (ATT_SCALE * LOG2E), s2q * (ATT_SCALE * LOG2E), ck, s1k, s2k)
    bwd = (cq * ATT_SCALE, -s1q * ATT_SCALE, -s2q * ATT_SCALE, ck, -s1k, -s2k)
    return fwd, bwd


def _pad_w_in(w):
    return jnp.concatenate([w[:, :COL_KR_END], jnp.zeros((w.shape[0], COL_GATES - COL_KR_END), w.dtype), w[:, COL_KR_END:]], axis=1)


def _pad_w_uq(w):
    w = w.reshape(Q_LORA, MLA_HEADS, QK_NOPE + QK_ROPE)
    z = lambda n: jnp.zeros((Q_LORA, MLA_HEADS, n), w.dtype)
    return jnp.concatenate([w[..., :QK_NOPE], z(Q_ROPE_AT - QK_NOPE), w[..., QK_NOPE:], z(Q_PAD - Q_ROPE_AT - QK_ROPE)],
                           axis=-1).reshape(Q_LORA, MLA_HEADS * Q_PAD)


def _unpad_w_uq(w):
    w = w.reshape(Q_LORA, MLA_HEADS, Q_PAD)
    return jnp.concatenate([w[..., :QK_NOPE], w[..., Q_ROPE_AT:Q_ROPE_AT + QK_ROPE]], axis=-1).reshape(
        Q_LORA, MLA_HEADS * (QK_NOPE + QK_ROPE))


def _pad_w_br_mla(w):
    w = w.reshape(MLA_HEADS, V_HEAD, D_MODEL)
    return jnp.concatenate([jnp.zeros_like(w), w], axis=1).reshape(MLA_HEADS * KV_PAD, D_MODEL)


def _unpad_w_br_mla(w):
    return w.reshape(MLA_HEADS, KV_PAD, D_MODEL)[:, V_HEAD:].reshape(MLA_HEADS * V_HEAD, D_MODEL)


def _riding(hooks, where, l, *args):
    make = hooks.get(where)
    ride = make(l, *args) if make else None
    return ride if ride else (None, lambda results: None)


def _layer_fwd(l, st, xprev, gp, bp, hb, w, tabs, hooks):
    ln_g, ln_b = w["ln_g"], w["ln_b"]
    lg = lambda k: ln_g[l, k][None]
    lb = lambda k: ln_b[l, k][None]
    s = {}
    s["x0"], s["gp0"], s["bp0"], s["hb0"] = xprev, gp, bp, hb
    side, got = _riding(hooks, "ffn1_fwd", l)
    s["g1"], s["u1"], s["a1"], *extras = ffn_up(hb, w["ffn1_w_up"][l], name="ffn_up", side=side)
    got(extras)
    s["xh1"], s["rs1"], s["hb1"] = down_ln(s["a1"], w["ffn1_w_down"][l], xprev, gp, bp, lg(0), lb(0), name="ffn_down_ln")
    s["p"] = mm_rows([(s["hb1"], w["mix_w_in"][l], False, 0)], D_IN_PAD, name="mix_in", tn=1024, out_dtype=BF)
    gq, gkv = w["q_norm_g"][l][None], w["kv_norm_g"][l][None]
    s["cqn"], s["ckvn"], s["q2"], s["kv"], s["kr"] = mla_prep(s["p"], gq, gkv, w["w_uq"][l], w["w_ukv"][l], tabs, name="mla_prep")
    side, got = _riding(hooks, "attn_fwd", l)
    s["o2"], s["lse"], *extras = attn_fwd(s["q2"], s["kv"], s["kr"], name="attn_fwd", side=side)
    got(extras)
    s["ycv"], s["conv"] = conv_fwd(s["p"], w["conv_w"][l], name="conv_fwd")
    s["bc"], s["bm"], s["mg"], s["xh2"], s["rs2"], s["hb2"] = merge_out_ln(
        s["ycv"], s["o2"], s["p"], w["mix_b_gate"][l], w["w_br_conv"][l], w["w_br_mla"][l], w["w_o"][l],
        s["xh1"], lg(0), lb(0), lg(1), lb(1), name="merge_out_ln")
    s["g2"], s["u2"], s["a2"] = ffn_up(s["hb2"], w["ffn2_w_up"][l], name="ffn_up")
    s["xh3"], s["rs3"], s["hb3"] = down_ln(s["a2"], w["ffn2_w_down"][l], s["xh2"], lg(1), lb(1), lg(2), lb(2), name="ffn_down_ln")
    st.append(s)
    return s["xh3"], lg(2), lb(2), s["hb3"]


def _ffn_bwd(which, l, g, hooks, dh, w_up, w_down, ln_gain, hb_in, gate, up, act, xh, rs):
    dzb, dgam, dbet, *loss_acc = ln_bwd(dh, xh, rs, ln_gain, branch_scale=0.5, name="ln_bwd")
    if loss_acc:
        g["loss"] = loss_acc[0]
    g[which + "_w_down"] = tn_mm(act, dzb, tm=D_FF // 2, name="dw_down", shard=("rows", D_FF // 4))
    side, got = _riding(hooks, which + "_down_bwd", l, g)
    dgate, dup, *extras = ffn_down_bwd(dzb, w_down, gate, up, name="ffn_down_bwd", side=side)
    got(extras)
    d_w = tn_mm(hb_in, dgate, tm=512, name="dw_up", shard=("cols", D_FF // 2), slot0=0)
    g[which + "_w_up"] = tn_mm(hb_in, dup, tm=512, name="dw_up", shard=("cols", D_FF // 2), slot0=2, dst=d_w)
    side, got = _riding(hooks, which + "_up_bwd", l, g)
    dh_in = mm_rows([(dgate, w_up, True, 0), (dup, w_up, True, 1)], D_MODEL, name="ffn_up_bwd", tn=512, addend=dzb, add_scale=2.0 * ALPHA,
                    side=side)
    if side is not None:
        dh_in, *extras = dh_in
        got(extras)
    return dh_in, dgam, dbet


def _layer_bwd(l, s, dh, w, tabs_bwd, hooks):
    ln_g = w["ln_g"]
    lg = lambda k: ln_g[l, k][None]
    g = {}
    dh, dg2, db2 = _ffn_bwd("ffn2", l, g, hooks, dh, w["ffn2_w_up"][l], w["ffn2_w_down"][l], lg(2), s["hb2"], s["g2"], s["u2"],
                            s["a2"], s["xh3"], s["rs3"])
    dzb, dg1, db1 = ln_bwd(dh, s["xh2"], s["rs2"], lg(1), branch_scale=1.0, name="ln_bwd")
    g["w_o"] = tn_mm(s["mg"], dzb, tm=1024, name="dw_o", shard=("rows", D_MODEL // 4))
    dbc, dbm, dp, dycv, do2, dl, g["mix_b_gate"] = merge_bwd(
        dzb, w["w_o"][l], s["bc"], s["bm"], s["p"], w["mix_b_gate"][l], w["w_br_conv"][l], w["w_br_mla"][l], s["o2"], name="merge_bwd")
    g["w_br_conv"] = tn_mm(s["ycv"], dbc, tm=512, name="dw_br_conv", shard=("cols", D_MODEL // 4))
    g["w_br_mla"] = _chip_major(_unpad_w_br_mla(tn_mm(s["o2"], dbm, tm=1024, name="dw_br_mla")), D_MODEL // 4)
    side, got = _riding(hooks, "attn_bwd", l, g)
    dq2, dkv, dkr, *extras = attn_bwd(s["q2"], s["kv"], s["kr"], do2, s["lse"], dl, name="attn_bwd", side=side)
    got(extras)
    gq, gkv = w["q_norm_g"][l][None], w["kv_norm_g"][l][None]
    dqb, dp, g["q_norm_g"], g["kv_norm_g"] = mla_prep_bwd(dq2, dkv, dkr, s["p"], gq, gkv, w["w_uq"][l], w["w_ukv"][l], tabs_bwd, dp,
                                                          name="mla_prep_bwd")
    g["w_uq"] = _chip_major(_unpad_w_uq(tn_mm(s["cqn"], dqb, tm=Q_LORA, name="dw_uq")), MLA_HEADS * (QK_NOPE + QK_ROPE) // 4)
    g["w_ukv"] = tn_mm(s["ckvn"], dkv, tm=KV_LORA, name="dw_ukv", shard=("cols", MLA_HEADS * KV_PAD // 4))
    dp, dw0, dw1, dw2 = conv_bwd(dycv, s["p"], s["conv"], w["conv_w"][l], dp, name="conv_bwd")
    g["conv_w"] = jnp.concatenate([dw0, dw1, dw2], axis=0)
    d_in = tn_mm(s["hb1"], dp, tm=512, name="dw_in")
    g["mix_w_in"] = _chip_major(jnp.concatenate([d_in[:, :COL_KR_END], d_in[:, COL_GATES:]], axis=1), D_IN // 4)
    side, got = _riding(hooks, "mix_in_bwd", l)
    dh = mm_rows([(dp, w["mix_w_in"][l], True, 0)], D_MODEL, name="mix_in_bwd", tn=512, addend=dzb, add_scale=ALPHA, side=side)
    if side is not None:
        dh, *extras = dh
        got(extras)
    dh, dg0, db0 = _ffn_bwd("ffn1", l, g, hooks, dh, w["ffn1_w_up"][l], w["ffn1_w_down"][l], lg(0), s["hb0"], s["g1"], s["u1"],
                            s["a1"], s["xh1"], s["rs1"])
    g["ln_g"] = jnp.concatenate([dg0, dg1, dg2], axis=0)
    g["ln_b"] = jnp.concatenate([db0, db1, db2], axis=0)
    return dh, g


BIG = ("ffn1_w_up", "ffn1_w_down", "mix_w_in", "w_uq", "w_ukv", "w_br_conv", "w_br_mla", "w_o", "ffn2_w_up", "ffn2_w_down")
BIG_AXIS = (2, 1, 2, 2, 2, 2, 2, 1, 2, 1)
FFN1_MATRICES = ("ffn1_w_up", "ffn1_w_down")
MIXER_MATRICES = ("mix_w_in", "w_uq", "w_ukv", "w_br_conv", "w_br_mla", "w_o")
FFN2_MATRICES = ("ffn2_w_up", "ffn2_w_down")
SMALL_SHARDED = ("meta_tokens", "mix_b_gate", "conv_w", "ln_g", "ln_b")
SMALL_REPLICATED = ("q_norm_g", "kv_norm_g")
WEIGHTS = ("meta_tokens", "ffn1_w_up", "ffn1_w_down", "mix_w_in", "mix_b_gate", "conv_w", "q_norm_g", "w_uq", "kv_norm_g", "w_ukv",
           "w_br_conv", "w_br_mla", "w_o", "ffn2_w_up", "ffn2_w_down", "ln_g", "ln_b")


def _view2d(a):
    return a.reshape(-1, a.shape[-1])


def _local_grads(x_row, target_row, w, hooks=None):
    hooks = hooks or {}
    seq = x_row.shape[0]
    t_real = N_META + seq
    tp = _pad_rows(t_real, TM)
    pad = tp - t_real
    h0 = jnp.concatenate([w["meta_tokens"], x_row, jnp.zeros((pad, D_MODEL), F32)], axis=0)
    target_p = jnp.concatenate([jnp.zeros((N_META, D_MODEL), F32), target_row, jnp.zeros((pad, D_MODEL), F32)], axis=0)
    tabs, tabs_bwd = _rope_tables(tp)
    ones = jnp.ones((1, D_MODEL), F32)
    zeros = jnp.zeros((1, D_MODEL), F32)
    saved = []
    cur = (h0, ones, zeros, h0.astype(BF))
    for l in range(DEPTH):
        cur = _layer_fwd(l, saved, *cur, w, tabs, hooks)
    dh = (cur[2], target_p, seq)
    grads = [None] * DEPTH
    for l in reversed(range(DEPTH)):
        dh, grads[l] = _layer_bwd(l, saved[l], dh, w, tabs_bwd, hooks)
        if "layer_bwd_done" in hooks:
            hooks["layer_bwd_done"](l, grads[l])
    return grads[DEPTH - 1].pop("loss"), dh[N_META:t_real], dh[:N_META], grads


def kernel(x, meta_tokens, ffn1_w_up, ffn1_w_down, mix_w_in, mix_b_gate, conv_w, q_norm_g, w_uq, kv_norm_g, w_ukv, w_br_conv, w_br_mla, w_o, ffn2_w_up, ffn2_w_down, ln_g, ln_b, loss_target, m_meta_tokens, m_ffn1_w_up, m_ffn1_w_down, m_mix_w_in, m_mix_b_gate, m_conv_w, m_q_norm_g, m_w_uq, m_kv_norm_g, m_w_ukv, m_w_br_conv, m_w_br_mla, m_w_o, m_ffn2_w_up, m_ffn2_w_down, m_ln_g, m_ln_b, v_meta_tokens, v_ffn1_w_up, v_ffn1_w_down, v_mix_w_in, v_mix_b_gate, v_conv_w, v_q_norm_g, v_w_uq, v_kv_norm_g, v_w_ukv, v_w_br_conv, v_w_br_mla, v_w_o, v_ffn2_w_up, v_ffn2_w_down, v_ln_g, v_ln_b):
    local = dict(meta_tokens=meta_tokens, ffn1_w_up=ffn1_w_up, ffn1_w_down=ffn1_w_down, mix_w_in=mix_w_in, mix_b_gate=mix_b_gate,
                 conv_w=conv_w, q_norm_g=q_norm_g, w_uq=w_uq, kv_norm_g=kv_norm_g, w_ukv=w_ukv, w_br_conv=w_br_conv,
                 w_br_mla=w_br_mla, w_o=w_o, ffn2_w_up=ffn2_w_up, ffn2_w_down=ffn2_w_down, ln_g=ln_g, ln_b=ln_b)
    mom_m = dict(zip(WEIGHTS, (m_meta_tokens, m_ffn1_w_up, m_ffn1_w_down, m_mix_w_in, m_mix_b_gate, m_conv_w, m_q_norm_g, m_w_uq,
                               m_kv_norm_g, m_w_ukv, m_w_br_conv, m_w_br_mla, m_w_o, m_ffn2_w_up, m_ffn2_w_down, m_ln_g, m_ln_b)))
    mom_v = dict(zip(WEIGHTS, (v_meta_tokens, v_ffn1_w_up, v_ffn1_w_down, v_mix_w_in, v_mix_b_gate, v_conv_w, v_q_norm_g, v_w_uq,
                               v_kv_norm_g, v_w_ukv, v_w_br_conv, v_w_br_mla, v_w_o, v_ffn2_w_up, v_ffn2_w_down, v_ln_g, v_ln_b)))
    xi, yi, ci = _place()
    chip = 2 * xi + yi

    shards = {n: local[n].astype(BF) for n in BIG}
    axis = dict(zip(BIG, BIG_AXIS))
    pad_layout = {"mix_w_in": _pad_w_in, "w_uq": _pad_w_uq, "w_br_mla": _pad_w_br_mla}
    w = {n: [None] * DEPTH for n in BIG}

    def fetch(keys):
        def install(gathered):
            for (n, l), g in zip(keys, gathered):
                if n in FFN1_MATRICES + FFN2_MATRICES:
                    w[n][l] = g
                    continue
                full = jnp.concatenate([g[j] for j in range(4)], axis=axis[n] - 1)
                w[n][l] = pad_layout[n](full) if n in pad_layout else full
        return gather_side([(shards[n], l) for n, l in keys]), install

    first, install_first = fetch([("ffn1_w_up", 0)])
    install_first(exchange_alone(first, name="gather_weights"))
    fetch_under = {("ffn1_fwd", 0): [("ffn1_w_down", 0)] + [(n, 0) for n in MIXER_MATRICES],
                   ("attn_fwd", 0): [(n, 0) for n in FFN2_MATRICES] + [(n, 1) for n in BIG]}
    hooks = {where: functools.partial(lambda l, where: fetch(fetch_under[where, l]) if (where, l) in fetch_under else None, where=where)
             for where in ("ffn1_fwd", "attn_fwd")}
    stacked = exchange_small([_view2d(local[n]) for n in SMALL_SHARDED], reduce=False, name="gather_small")
    for n, st in zip(SMALL_SHARDED, stacked):
        full = jnp.concatenate([st[2 * j] for j in range(4)], axis=-1)
        w[n] = full.reshape(local[n].shape[:-1] + (full.shape[-1],))
    for n in SMALL_REPLICATED:
        w[n] = local[n]

    c_idx = jnp.reshape(ci, (1,)).astype(jnp.int32)
    done, from_sibling, from_chips = {}, {}, {}

    def send(keys, grad_of):
        waiting = [k for k in keys if k not in from_sibling]
        from_sibling.update(zip(waiting, pair_exchange([grad_of[k] for k in waiting], name="rs_pair_exchange")))
        sums = [pair_add(grad_of[k], from_sibling[k], c_idx, name="rs_pair_add") for k in keys]
        return scatter_side(sums), lambda results: from_chips.update(zip(keys, results))

    layer1 = [(n, 1) for n in BIG]
    hooks["ffn2_down_bwd"] = lambda l, g: (pair_exchange_side([done[k] for k in layer1]),
                                           lambda results: from_sibling.update(zip(layer1, results))) if l == 0 else None

    ffn2_layer0 = [(n, 0) for n in FFN2_MATRICES]
    hooks["ffn2_up_bwd"] = lambda l, g: (pair_exchange_side([g[n] for n in FFN2_MATRICES]),
                                         lambda results: from_sibling.update(zip(ffn2_layer0, results))) if l == 0 else None

    reduced = {}
    hooks["mix_in_bwd"] = lambda l: (pair_gather_side([sum_chunks(from_chips[k], name="rs_sum") for k in layer1]),
                                     lambda results: reduced.update(zip(layer1, results))) if l == 0 else None

    send_under = {"attn_bwd": FFN2_MATRICES + ("w_o", "w_br_conv", "w_br_mla"),
                  "ffn1_down_bwd": ("mix_w_in", "w_uq", "w_ukv", "ffn1_w_down"), "ffn1_up_bwd": ("ffn1_w_up",)}
    hooks["layer_bwd_done"] = lambda l, g: done.update({(n, l): g[n] for n in BIG})

    def send_hook(where):
        def hook(l, g):
            if l != 0:
                return None
            keys = [(n, 0) for n in send_under[where]] + ([(n, 1) for n in BIG] if where == "attn_bwd" else [])
            return send(keys, {**done, **{(n, 0): g[n] for n in send_under[where]}})
        return hook

    for where in send_under:
        hooks[where] = send_hook(where)

    loss_acc, grad_x, d_meta, grads = _local_grads(x[0], loss_target[0], w, hooks)
    grad_x = grad_x[None]
    layer0 = [(n, 0) for n in BIG]
    reduced.update(zip(layer0, exchange_alone(pair_gather_side([sum_chunks(from_chips[k], name="rs_sum") for k in layer0]),
                                              name="rs_pair_gather")))
    reduced = {k: r.reshape(local[k[0]].shape[1:]) for k, r in reduced.items()}
    gshard = {n: jnp.stack([reduced[n, l] for l in range(DEPTH)]) for n in BIG}

    small_names = SMALL_SHARDED + SMALL_REPLICATED
    gsmall = {n: jnp.concatenate([grads[l][n] for l in range(DEPTH)], axis=0) for n in small_names if n != "meta_tokens"}
    gsmall["meta_tokens"] = d_meta
    small_red = exchange_small([gsmall[n] for n in small_names] + [loss_acc], reduce=True, name="reduce_small")
    loss = small_red[-1][0, 0]
    for n, full in zip(small_names, small_red[:-1]):
        if n in SMALL_SHARDED:
            sh = local[n].shape[-1]
            full = lax.dynamic_slice_in_dim(full, chip * sh, sh, axis=1)
        gshard[n] = full.reshape(local[n].shape)

    delta, new_m, new_v = {}, {}, {}
    for n in WEIGHTS:
        shape = local[n].shape
        d, nm, nv = adamw(_view2d(local[n]), _view2d(gshard[n]), _view2d(mom_m[n]), _view2d(mom_v[n]), name="adamw")
        delta[n], new_m[n], new_v[n] = d.reshape(shape), nm.reshape(shape), nv.reshape(shape)
    return (loss, grad_x, *[gshard[n] for n in WEIGHTS], *[delta[n] for n in WEIGHTS], *[new_m[n] for n in WEIGHTS],
            *[new_v[n] for n in WEIGHTS])
```
